```python
import jax, jax.numpy as jnp
from jax import lax
import numpy as np

D_MODEL = 1024
BATCH = 2
SEQ = 8192
DEPTH = 1
DEC_BATCH = 16
DEC_SEQ = 16
PAST_LEN = 2048

CHUNK = 64
GLA_HEADS = 4
GLA_DK = 64
GLA_DV = 128
GLA_GATE_RANK = 16
GLA_TAU = 16.0
ML_HEADS = 4
ML_DH = 128
CONV_W = 4
N_GROUPS = 4
EXPERTS_PER_GROUP = 8
TOP_K = 2
D_EXPERT = 256
EPS = 1e-6

GLA_QK = GLA_HEADS * GLA_DK
GLA_V = GLA_HEADS * GLA_DV
ML_W = ML_HEADS * ML_DH
D_IN = 2 * GLA_QK + 2 * GLA_V + GLA_GATE_RANK + 4 * ML_W + 2 * ML_HEADS + 2 * D_MODEL

kernel_name = 'gla_mlstm_hier_moe_stream_step'


def _split_points():
    sizes = (GLA_QK, GLA_QK, GLA_V, GLA_V, GLA_GATE_RANK, 2 * ML_W, ML_W, ML_W, ML_HEADS, ML_HEADS, 2 * D_MODEL)
    return [int(s) for s in np.cumsum(sizes)[:-1]]


def rmsnorm(x, g):
    x32 = x.astype(jnp.float32)
    y = x32 * lax.rsqrt(jnp.mean(x32 * x32, axis=-1, keepdims=True) + EPS)
    return (y * g.astype(jnp.float32)).astype(x.dtype)


def _chunk_len(L):
    return CHUNK if L % CHUNK == 0 else L


def _to_chunks(t, c):
    B, L, H = t.shape[:3]
    t = t.reshape((B, L // c, c, H) + t.shape[3:])
    return t.transpose((1, 0, 3, 2) + tuple(range(4, t.ndim)))


def _from_chunks(t):
    nc, B, H, c, d = t.shape
    return t.transpose(1, 0, 3, 2, 4).reshape(B, nc * c, H, d)


def gla_chunked(q, k, v, log_a, S0):
    c = _chunk_len(q.shape[1])
    causal = jnp.tril(jnp.ones((c, c), dtype=bool))
    xs = tuple(_to_chunks(t.astype(jnp.float32), c) for t in (q, k, v, log_a))

    def step(S, inp):
        qc, kc, vc, ac = inp
        b = jnp.cumsum(ac, axis=2)
        diff = b[:, :, :, None, :] - b[:, :, None, :, :]
        decay = jnp.exp(jnp.where(causal[None, None, :, :, None], diff, -jnp.inf))
        scores = jnp.einsum('bhtk,bhsk,bhtsk->bhts', qc, kc, decay)
        o = jnp.einsum('bhts,bhsv->bhtv', scores, vc) + jnp.einsum('bhtk,bhkv->bhtv', qc * jnp.exp(b), S)
        b_last = b[:, :, -1, :]
        S_new = jnp.exp(b_last)[..., None] * S + jnp.einsum(
            'bhsk,bhsv->bhkv', kc * jnp.exp(b_last[:, :, None, :] - b), vc)
        return S_new, o

    S_fin, o = lax.scan(step, S0.astype(jnp.float32), xs)
    return _from_chunks(o), S_fin


def mlstm_chunked(q, k, v, i_pre, f_log, C0, n0, m0):
    c = _chunk_len(q.shape[1])
    causal = jnp.tril(jnp.ones((c, c), dtype=bool))
    xs = tuple(_to_chunks(t.astype(jnp.float32), c) for t in (q, k, v, i_pre, f_log))

    def step(carry, inp):
        C, n, m = carry
        qc, kc, vc, ic, fc = inp
        b = jnp.cumsum(fc, axis=-1)
        a = b + m[..., None]
        d = jnp.where(causal, b[..., :, None] - b[..., None, :] + ic[..., None, :], -jnp.inf)
        m_t = jnp.maximum(a, jnp.max(d, axis=-1))
        w_inter = jnp.exp(a - m_t)
        s = jnp.einsum('bhtd,bhsd->bhts', qc, kc) * jnp.exp(d - m_t[..., None])
        num = jnp.einsum('bhts,bhsv->bhtv', s, vc) + w_inter[..., None] * jnp.einsum('bhtd,bhdv->bhtv', qc, C)
        den = jnp.sum(s, axis=-1) + w_inter * jnp.einsum('bhtd,bhd->bht', qc, n)
        hc = num / jnp.maximum(jnp.abs(den), jnp.exp(-m_t))[..., None]
        m_new = m_t[..., -1]
        w_s = jnp.exp(b[..., -1:] - b + ic - m_new[..., None])
        dec = jnp.exp(a[..., -1] - m_new)
        C_new = dec[..., None, None] * C + jnp.einsum('bhs,bhsd,bhsv->bhdv', w_s, kc, vc)
        n_new = dec[..., None] * n + jnp.einsum('bhs,bhsd->bhd', w_s, kc)
        return (C_new, n_new, m_new), hc

    init = (C0.astype(jnp.float32), n0.astype(jnp.float32), m0.astype(jnp.float32))
    (C_f, n_f, m_f), h = lax.scan(step, init, xs)
    return _from_chunks(h), C_f, n_f, m_f


def hier_moe(h, rg_w, rg_b, re_w, re_b, w_gate, w_up, w_down):
    n = h.shape[0]
    lg = (h @ rg_w).astype(jnp.float32) + rg_b.astype(jnp.float32)
    pg = jax.nn.softmax(lg, axis=-1)
    _, gi = lax.top_k(lg, 1)
    onehot_g = jax.nn.one_hot(gi[:, 0], N_GROUPS, dtype=jnp.float32)
    p_top = jnp.sum(pg * onehot_g, axis=-1)
    le = ((h @ re_w).astype(jnp.float32) + re_b.astype(jnp.float32)).reshape(n, N_GROUPS, EXPERTS_PER_GROUP)
    le_sel = jnp.einsum('nge,ng->ne', le, onehot_g)
    tv, ti = lax.top_k(le_sel, TOP_K)
    w2 = jax.nn.softmax(tv, axis=-1)
    comb_e = jnp.einsum('nk,nke->ne', w2, jax.nn.one_hot(ti, EXPERTS_PER_GROUP, dtype=jnp.float32))
    comb = (onehot_g[:, :, None] * comb_e[:, None, :] * p_top[:, None, None]).astype(h.dtype)
    out = jnp.zeros_like(h)
    for g in range(N_GROUPS):
        a = jnp.einsum('nd,edf->nef', h, w_gate[g])
        u = jnp.einsum('nd,edf->nef', h, w_up[g])
        out = out + jnp.einsum('nef,efd->nd', jax.nn.silu(a) * u * comb[:, g, :, None], w_down[g])
    return out


def hybrid_layer(x, gla_S0, ml_C0, ml_n0, ml_m0, conv_buf,
                 norm1_g, w_in, gla_w_gate2, gla_b_gate, gla_norm_g, w_up_gla,
                 ml_conv_w, ml_conv_b, ml_b_i, ml_b_f, w_up_ml, w_out,
                 norm2_g, router_g_w, router_g_b, router_e_w, router_e_b,
                 moe_w_gate, moe_w_up, moe_w_down):
    B, L, _ = x.shape
    h = rmsnorm(x, norm1_g)
    proj = h @ w_in
    gq, gk, gv, gr, glr, mqk, mv, mo, mi, mf, mg = jnp.split(proj, _split_points(), axis=-1)

    log_a = jax.nn.log_sigmoid((glr @ gla_w_gate2 + gla_b_gate).astype(jnp.float32)) / GLA_TAU
    q = gq.reshape(B, L, GLA_HEADS, GLA_DK) * (GLA_DK ** -0.5)
    k = gk.reshape(B, L, GLA_HEADS, GLA_DK)
    v = gv.reshape(B, L, GLA_HEADS, GLA_DV)
    o_gla, gla_S = gla_chunked(q, k, v, log_a.reshape(B, L, GLA_HEADS, GLA_DK), gla_S0)
    o_gla = rmsnorm(o_gla, gla_norm_g.reshape(GLA_HEADS, GLA_DV)).reshape(B, L, GLA_V)
    y_a = (o_gla.astype(x.dtype) * jax.nn.silu(gr)) @ w_up_gla

    conv_in = jnp.concatenate([conv_buf.astype(mqk.dtype), mqk], axis=1)
    conv = ml_conv_b + conv_in[:, 0:L] * ml_conv_w[0]
    for j in range(1, CONV_W):
        conv = conv + conv_in[:, j:j + L] * ml_conv_w[j]
    conv = jax.nn.silu(conv)
    new_conv = conv_in[:, L:]
    mq = conv[..., :ML_W].reshape(B, L, ML_HEADS, ML_DH)
    mk = conv[..., ML_W:].reshape(B, L, ML_HEADS, ML_DH) * (ML_DH ** -0.5)
    mvv = mv.reshape(B, L, ML_HEADS, ML_DH)
    i_pre = mi.astype(jnp.float32) + ml_b_i.astype(jnp.float32)
    f_log = jax.nn.log_sigmoid(mf.astype(jnp.float32) + ml_b_f.astype(jnp.float32))
    h_ml, ml_C, ml_n, ml_m = mlstm_chunked(mq, mk, mvv, i_pre, f_log, ml_C0, ml_n0, ml_m0)
    y_b = (jax.nn.sigmoid(mo) * h_ml.reshape(B, L, ML_W).astype(x.dtype)) @ w_up_ml

    gates = jax.nn.sigmoid(mg)
    x = x + (gates[..., :D_MODEL] * y_a + gates[..., D_MODEL:] * y_b) @ w_out

    hm = rmsnorm(x, norm2_g).reshape(B * L, D_MODEL)
    x = x + hier_moe(hm, router_g_w, router_g_b, router_e_w, router_e_b,
                     moe_w_gate, moe_w_up, moe_w_down).reshape(B, L, D_MODEL)
    return x, gla_S, ml_C, ml_n, ml_m, new_conv


def setup_inputs(seed: int = 0) -> dict:
    key = jax.random.key(seed)
    ks = jax.random.split(key, 32)

    def nrm(k, shape, scale):
        return scale * jax.random.normal(k, shape, jnp.float32)

    G, E, F = N_GROUPS, EXPERTS_PER_GROUP, D_EXPERT
    return {
        'x_prompt': nrm(ks[0], (BATCH, SEQ, D_MODEL), 1.0),
        'x_sample': nrm(ks[1], (DEC_BATCH, DEC_SEQ, D_MODEL), 1.0),
        'state_gla_S': nrm(ks[2], (DEPTH, DEC_BATCH, GLA_HEADS, GLA_DK, GLA_DV), 0.5),
        'state_mlstm_C': nrm(ks[3], (DEPTH, DEC_BATCH, ML_HEADS, ML_DH, ML_DH), 0.1),
        'state_mlstm_n': nrm(ks[4], (DEPTH, DEC_BATCH, ML_HEADS, ML_DH), 0.1),
        'state_mlstm_m': nrm(ks[5], (DEPTH, DEC_BATCH, ML_HEADS), 1.0),
        'state_mlstm_conv': nrm(ks[6], (DEPTH, DEC_BATCH, CONV_W - 1, 2 * ML_W), 1.0),
        'norm1_g': 1.0 + nrm(ks[7], (DEPTH, D_MODEL), 0.02),
        'w_in': nrm(ks[8], (DEPTH, D_MODEL, D_IN), D_MODEL ** -0.5),
        'gla_w_gate2': nrm(ks[9], (DEPTH, GLA_GATE_RANK, GLA_QK), GLA_GATE_RANK ** -0.5),
        'gla_b_gate': nrm(ks[10], (DEPTH, GLA_QK), 0.1),
        'gla_norm_g': 1.0 + nrm(ks[11], (DEPTH, GLA_V), 0.02),
        'w_up_gla': nrm(ks[12], (DEPTH, GLA_V, D_MODEL), GLA_V ** -0.5),
        'ml_conv_w': nrm(ks[13], (DEPTH, CONV_W, 2 * ML_W), CONV_W ** -0.5),
        'ml_conv_b': nrm(ks[14], (DEPTH, 2 * ML_W), 0.02),
        'ml_b_i': nrm(ks[15], (DEPTH, ML_HEADS), 0.1),
        'ml_b_f': jnp.linspace(3.0, 6.0, ML_HEADS, dtype=jnp.float32)[None, :] + nrm(ks[16], (DEPTH, ML_HEADS), 0.1),
        'w_up_ml': nrm(ks[17], (DEPTH, ML_W, D_MODEL), ML_W ** -0.5),
        'w_out': nrm(ks[18], (DEPTH, D_MODEL, D_MODEL), D_MODEL ** -0.5),
        'norm2_g': 1.0 + nrm(ks[19], (DEPTH, D_MODEL), 0.02),
        'router_g_w': nrm(ks[20], (DEPTH, D_MODEL, G), D_MODEL ** -0.5),
        'router_g_b': nrm(ks[21], (DEPTH, G), 0.01),
        'router_e_w': nrm(ks[22], (DEPTH, D_MODEL, G * E), D_MODEL ** -0.5),
        'router_e_b': nrm(ks[23], (DEPTH, G * E), 0.01),
        'moe_w_gate': nrm(ks[24], (DEPTH, G, E, D_MODEL, F), D_MODEL ** -0.5),
        'moe_w_up': nrm(ks[25], (DEPTH, G, E, D_MODEL, F), D_MODEL ** -0.5),
        'moe_w_down': nrm(ks[26], (DEPTH, G, E, F, D_MODEL), F ** -0.5),
        'final_g': 1.0 + nrm(ks[27], (D_MODEL,), 0.02),
    }


def _stack(states, i):
    return jnp.stack([s[i] for s in states])


def reference(x_prompt, x_sample, state_gla_S, state_mlstm_C, state_mlstm_n, state_mlstm_m, state_mlstm_conv,
              norm1_g, w_in, gla_w_gate2, gla_b_gate, gla_norm_g, w_up_gla,
              ml_conv_w, ml_conv_b, ml_b_i, ml_b_f, w_up_ml, w_out,
              norm2_g, router_g_w, router_g_b, router_e_w, router_e_b,
              moe_w_gate, moe_w_up, moe_w_down, final_g):
    B = x_prompt.shape[0]
    hp, hs = x_prompt, x_sample
    st_p, st_s = [], []
    for l in range(DEPTH):
        lw = (norm1_g[l], w_in[l], gla_w_gate2[l], gla_b_gate[l], gla_norm_g[l], w_up_gla[l],
              ml_conv_w[l], ml_conv_b[l], ml_b_i[l], ml_b_f[l], w_up_ml[l], w_out[l],
              norm2_g[l], router_g_w[l], router_g_b[l], router_e_w[l], router_e_b[l],
              moe_w_gate[l], moe_w_up[l], moe_w_down[l])
        z_S = jnp.zeros((B, GLA_HEADS, GLA_DK, GLA_DV), jnp.float32)
        z_C = jnp.zeros((B, ML_HEADS, ML_DH, ML_DH), jnp.float32)
        z_n = jnp.zeros((B, ML_HEADS, ML_DH), jnp.float32)
        z_m = jnp.zeros((B, ML_HEADS), jnp.float32)
        z_conv = jnp.zeros((B, CONV_W - 1, 2 * ML_W), x_prompt.dtype)
        hp, *sp = hybrid_layer(hp, z_S, z_C, z_n, z_m, z_conv, *lw)
        hs, *ss = hybrid_layer(hs, state_gla_S[l], state_mlstm_C[l], state_mlstm_n[l], state_mlstm_m[l],
                               state_mlstm_conv[l], *lw)
        st_p.append(sp)
        st_s.append(ss)
    y_prompt = rmsnorm(hp, final_g)
    y_sample = rmsnorm(hs, final_g)
    return (y_prompt, y_sample,
            _stack(st_p, 0), _stack(st_p, 1), _stack(st_p, 2), _stack(st_p, 3), _stack(st_p, 4),
            _stack(st_s, 0), _stack(st_s, 1), _stack(st_s, 2), _stack(st_s, 3), _stack(st_s, 4))
```

```python
import functools
import math

import numpy as np
import jax
import jax.numpy as jnp
from jax import lax
from jax.experimental import pallas as pl
from jax.experimental.pallas import tpu as pltpu

D_MODEL = 1024
GLA_HEADS = 4
GLA_DK = 64
GLA_DV = 128
GLA_GATE_RANK = 16
GLA_TAU = 16.0
ML_HEADS = 4
ML_DH = 128
CONV_W = 4
N_GROUPS = 4
EXPERTS_PER_GROUP = 8
N_EXPERTS = N_GROUPS * EXPERTS_PER_GROUP
D_EXPERT = 256
EPS = 1e-6

GLA_QK = GLA_HEADS * GLA_DK
GLA_V = GLA_HEADS * GLA_DV
ML_W = ML_HEADS * ML_DH

LANES = 128
VMEM_LIMIT = 56 * 1024 * 1024

W_GLA = 2 * GLA_QK + 2 * GLA_V
W_ML = 2 * ML_W + ML_W + ML_W
W_MG = 2 * D_MODEL
PROJ_WIDTHS = (W_GLA, LANES, W_ML, LANES, W_MG)

F32 = jnp.float32
BF16 = jnp.bfloat16


def _dot(a, b):
    return jnp.dot(a, b, preferred_element_type=F32)


def _dot_nt(a, b):
    return lax.dot_general(a, b, (((1,), (1,)), ((), ())), preferred_element_type=F32)


def _dot_tn(a, b):
    return lax.dot_general(a, b, (((0,), (0,)), ((), ())), preferred_element_type=F32)


def _split3(x):
    hi = x.astype(BF16)
    r1 = x - hi.astype(F32)
    mid = r1.astype(BF16)
    lo = (r1 - mid.astype(F32)).astype(BF16)
    return hi, mid, lo


def _dot_exact_lhs(m, x):
    hi, mid, lo = _split3(x)
    return _dot(m, hi) + _dot(m, mid) + _dot(m, lo)


def _log_sigmoid(z):
    return jnp.minimum(z, 0.0) - jnp.log1p(jnp.exp(-jnp.abs(z)))


def _sigmoid(z):
    return 1.0 / (1.0 + jnp.exp(-z))


def _rms(x, g):
    return x * lax.rsqrt(jnp.mean(x * x, axis=-1, keepdims=True) + EPS) * g


def _full_spec(shape):
    nd = len(shape)
    return pl.BlockSpec(shape, lambda *_: (0,) * nd)


def _proj_kernel(x_ref, g_ref, w_ref, *o_refs):
    h = _rms(x_ref[...], g_ref[...]).astype(BF16)
    off = 0
    for o_ref, width in zip(o_refs, PROJ_WIDTHS):
        o_ref[...] = _dot(h, w_ref[:, off:off + width])
        off += width


def _proj_call(x2, g, w_p, tm):
    n = x2.shape[0]
    return pl.pallas_call(
        _proj_kernel,
        grid=(n // tm,),
        in_specs=[pl.BlockSpec((tm, D_MODEL), lambda i: (i, 0)),
                  _full_spec(g.shape), _full_spec(w_p.shape)],
        out_specs=[pl.BlockSpec((tm, w), lambda i: (i, 0)) for w in PROJ_WIDTHS],
        out_shape=[jax.ShapeDtypeStruct((n, w), F32) for w in PROJ_WIDTHS],
        compiler_params=pltpu.CompilerParams(
            dimension_semantics=("arbitrary",), vmem_limit_bytes=VMEM_LIMIT),
        name="in_proj",
    )(x2, g, w_p)


def _gla_consts(c):
    nlev = int(math.log2(c))
    assert 1 << nlev == c
    t = np.arange(c)[:, None]
    j = np.arange(c)[None, :]
    mats = [j <= t, j > t]
    lv = np.full((c, c), -1, np.int32)
    for l in range(nlev):
        h = c >> (l + 1)
        start = (t // (2 * h)) * (2 * h)
        m = start + h - 1
        upper = (t - start) >= h
        mats.append(np.where(upper, (j > m) & (j <= t), (j > t) & (j <= m)))
        same = (j // (2 * h)) == (t // (2 * h))
        s_lower = (j % (2 * h)) < h
        lv[np.broadcast_to(upper, (c, c)) & same & s_lower] = l
    lv[np.eye(c, dtype=bool)] = nlev
    m_all = np.concatenate(mats, 0).astype(np.float32)
    return jnp.asarray(m_all, BF16), jnp.asarray(lv)


def _gla_kernel(*refs, c, t, has_state):
    if has_state:
        (pg_ref, plr_ref, s0_ref, wg2_ref, bg_ref, gn_ref, mall_ref, lv_ref,
         o_ref, sout_ref, s_scr) = refs
    else:
        (pg_ref, plr_ref, wg2_ref, bg_ref, gn_ref, mall_ref, lv_ref,
         o_ref, sout_ref, s_scr) = refs
    nlev = int(math.log2(c))
    step = pl.program_id(1)

    @pl.when(step == 0)
    def _():
        if has_state:
            s_scr[...] = s0_ref[...]
        else:
            s_scr[...] = jnp.zeros_like(s_scr)

    ones = jnp.ones((c, LANES), BF16)

    def chunk(ci, carry):
        r0 = pl.multiple_of(ci * c, c)
        rows = pl.ds(r0, c)
        q = pg_ref[rows, 0:GLA_QK] * (GLA_DK ** -0.5)
        k = pg_ref[rows, GLA_QK:2 * GLA_QK]
        v = pg_ref[rows, 2 * GLA_QK:2 * GLA_QK + GLA_V].astype(BF16)
        z = lax.dot_general(plr_ref[rows, :], wg2_ref[...], (((1,), (0,)), ((), ())),
                            precision=lax.Precision.HIGHEST,
                            preferred_element_type=F32) + bg_ref[...]
        la = _log_sigmoid(z) * (1.0 / GLA_TAU)
        la3 = _split3(la)
        mall = mall_ref[...]
        ex = _dot(mall, la3[0]) + _dot(mall, la3[1]) + _dot(mall, la3[2])
        qe = (q * jnp.exp(ex[0:c])).astype(BF16)
        kl = (k * jnp.exp(ex[c:2 * c])).astype(BF16)
        dsum = _dot_tn(la3[0], ones) + _dot_tn(la3[1], ones) + _dot_tn(la3[2], ones)
        lv = lv_ref[...]
        qb = q.astype(BF16)
        kb = k.astype(BF16)
        a = []
        for h in range(GLA_HEADS):
            hs = slice(h * GLA_DK, (h + 1) * GLA_DK)
            a.append(jnp.where(lv == nlev, _dot_nt(qb[:, hs], kb[:, hs]), 0.0))
        for l in range(nlev):
            e = jnp.exp(ex[(2 + l) * c:(3 + l) * c])
            qt = (q * e).astype(BF16)
            kt = (k * e).astype(BF16)
            for h in range(GLA_HEADS):
                hs = slice(h * GLA_DK, (h + 1) * GLA_DK)
                a[h] = jnp.where(lv == l, _dot_nt(qt[:, hs], kt[:, hs]), a[h])
        for h in range(GLA_HEADS):
            hs = slice(h * GLA_DK, (h + 1) * GLA_DK)
            vs = slice(h * GLA_DV, (h + 1) * GLA_DV)
            s_h = s_scr[h]
            o = _dot(a[h].astype(BF16), v[:, vs]) + _dot(qe[:, hs], s_h.astype(BF16))
            on = _rms(o, gn_ref[:, vs])
            r = pg_ref[rows, 2 * GLA_QK + GLA_V + h * GLA_DV:2 * GLA_QK + GLA_V + (h + 1) * GLA_DV]
            o_ref[rows, vs] = (on * (r * _sigmoid(r))).astype(o_ref.dtype)
            s_scr[h] = jnp.exp(dsum[hs, :]) * s_h + _dot_tn(kl[:, hs], v[:, vs])
        return carry

    lax.fori_loop(0, t // c, chunk, 0)

    @pl.when(step == pl.num_programs(1) - 1)
    def _():
        sout_ref[...] = s_scr[...]


def _gla_call(pg, plr, s0, wg2_p, bg, gn, c, t):
    b, l, _ = pg.shape
    m_all, lv = _gla_consts(c)
    has_state = s0 is not None
    tile = lambda w: pl.BlockSpec((None, t, w), lambda bi, i: (bi, i, 0))
    state_spec = pl.BlockSpec((None, GLA_HEADS, GLA_DK, GLA_DV), lambda bi, i: (bi, 0, 0, 0))
    in_specs = [tile(W_GLA), tile(LANES)]
    args = [pg, plr]
    if has_state:
        in_specs.append(state_spec)
        args.append(s0)
    consts = [wg2_p, bg, gn, m_all, lv]
    in_specs += [_full_spec(x.shape) for x in consts]
    return pl.pallas_call(
        functools.partial(_gla_kernel, c=c, t=t, has_state=has_state),
        grid=(b, l // t),
        in_specs=in_specs,
        out_specs=[tile(GLA_V), state_spec],
        out_shape=[jax.ShapeDtypeStruct((b, l, GLA_V), BF16),
                   jax.ShapeDtypeStruct((b, GLA_HEADS, GLA_DK, GLA_DV), F32)],
        scratch_shapes=[pltpu.VMEM((GLA_HEADS, GLA_DK, GLA_DV), F32)],
        compiler_params=pltpu.CompilerParams(
            dimension_semantics=("arbitrary", "arbitrary"), vmem_limit_bytes=VMEM_LIMIT),
        name="gla",
    )(*args, *consts)


def _mlstm_kernel(*refs, c, t, has_state):
    if has_state:
        (pml_ref, pif_ref, c0_ref, m0_ref, cv0_ref, cw_ref, cb_ref, bif_ref, tri_ref,
         o_ref, cout_ref, mout_ref, cvout_ref, c_scr, m_scr, cv_scr, qk_scr, b_scr) = refs
    else:
        (pml_ref, pif_ref, cw_ref, cb_ref, bif_ref, tri_ref,
         o_ref, cout_ref, mout_ref, cvout_ref, c_scr, m_scr, cv_scr, qk_scr, b_scr) = refs
    step = pl.program_id(1)
    hist = CONV_W - 1
    pad = 8

    @pl.when(step == 0)
    def _():
        if has_state:
            c_scr[...] = c0_ref[...]
            m_scr[...] = m0_ref[...]
            cv_scr[0:pad - hist, :] = jnp.zeros((pad - hist, 2 * ML_W), F32)
            cv_scr[pad - hist:pad, :] = cv0_ref[...]
        else:
            c_scr[...] = jnp.zeros_like(c_scr)
            m_scr[...] = jnp.zeros_like(m_scr)
            cv_scr[...] = jnp.zeros_like(cv_scr)

    x = pml_ref[:, 0:2 * ML_W]
    ext = jnp.concatenate([cv_scr[...], x], axis=0)
    acc = cb_ref[...] + ext[pad:pad + t] * cw_ref[CONV_W - 1:CONV_W, :]
    for d in range(1, CONV_W):
        acc = acc + ext[pad - d:pad - d + t] * cw_ref[CONV_W - 1 - d:CONV_W - d, :]
    conv = acc * _sigmoid(acc)
    qk_scr[:, 0:ML_W] = conv[:, 0:ML_W].astype(BF16)
    qk_scr[:, ML_W:] = (conv[:, ML_W:] * (ML_DH ** -0.5)).astype(BF16)
    cv_scr[...] = x[t - pad:t, :]

    gts = pif_ref[...] + bif_ref[...]
    lane = lax.broadcasted_iota(jnp.int32, (1, LANES), 1)
    b_scr[...] = jnp.where(lane < ML_HEADS, gts, _log_sigmoid(gts))

    row = lax.broadcasted_iota(jnp.int32, (c, c), 0)
    col = lax.broadcasted_iota(jnp.int32, (c, c), 1)
    causal = col <= row
    eye = col == row
    ones_cc = jnp.ones((c, c), BF16)
    one_col = jnp.where(lax.broadcasted_iota(jnp.int32, (c, LANES), 1) == 0, 1.0, 0.0).astype(BF16)

    def chunk(ci, carry):
        r0 = pl.multiple_of(ci * c, c)
        rows = pl.ds(r0, c)
        gt = b_scr[rows, :]
        bcum = _dot_exact_lhs(tri_ref[...], gt)
        for h in range(ML_HEADS):
            hs = slice(h * ML_DH, (h + 1) * ML_DH)
            qh = qk_scr[rows, h * ML_DH:(h + 1) * ML_DH]
            kh = qk_scr[rows, ML_W + h * ML_DH:ML_W + (h + 1) * ML_DH]
            vaug = jnp.concatenate(
                [pml_ref[rows, 2 * ML_W + h * ML_DH:2 * ML_W + (h + 1) * ML_DH].astype(BF16), one_col],
                axis=1)
            b_h = bcum[:, ML_HEADS + h:ML_HEADS + h + 1]
            w = gt[:, h:h + 1] - b_h
            wrow = _dot_exact_lhs(ones_cc, jnp.where(eye, w, 0.0))
            mprev = m_scr[h][0:1, 0:1]
            g = jnp.maximum(mprev, jnp.max(jnp.where(causal, wrow, -jnp.inf), axis=-1, keepdims=True))
            p = jnp.exp(jnp.where(causal, wrow - g, -jnp.inf))
            w_inter = jnp.exp(mprev - g)
            s = (_dot_nt(qh, kh) * p).astype(BF16)
            caug = c_scr[h]
            nd = _dot(s, vaug) + w_inter * _dot(qh, caug.astype(BF16))
            num = nd[:, 0:ML_DH]
            den = nd[:, ML_DH:ML_DH + 1]
            hh = num / jnp.maximum(jnp.abs(den), jnp.exp(-(b_h + g)))
            mo = pml_ref[rows, 3 * ML_W + h * ML_DH:3 * ML_W + (h + 1) * ML_DH]
            o_ref[rows, hs] = (_sigmoid(mo) * hh).astype(o_ref.dtype)
            g_last = g[c - 1:c, :]
            ks = (kh.astype(F32) * jnp.exp(w - g_last)).astype(BF16)
            c_scr[h] = jnp.exp(mprev - g_last) * caug + _dot_tn(ks, vaug)
            m_scr[h] = jnp.broadcast_to(b_h[c - 1:c, :] + g_last, m_scr.shape[1:])
        return carry

    lax.fori_loop(0, t // c, chunk, 0)

    @pl.when(step == pl.num_programs(1) - 1)
    def _():
        cout_ref[...] = c_scr[...]
        mout_ref[...] = m_scr[...]
        cvout_ref[...] = pml_ref[t - hist:t, 0:2 * ML_W]


def _mlstm_call(pml, pif, caug0, m0, cv0, cw, cb, bif, c, t):
    b, l, _ = pml.shape
    has_state = caug0 is not None
    tri = jnp.asarray(np.tril(np.ones((c, c), np.float32)), BF16)
    tile = lambda w: pl.BlockSpec((None, t, w), lambda bi, i: (bi, i, 0))
    c_spec = pl.BlockSpec((None, ML_HEADS, ML_DH, 2 * ML_DH), lambda bi, i: (bi, 0, 0, 0))
    m_spec = pl.BlockSpec((None, ML_HEADS, 8, LANES), lambda bi, i: (bi, 0, 0, 0))
    cv_spec = pl.BlockSpec((None, CONV_W - 1, 2 * ML_W), lambda bi, i: (bi, 0, 0))
    in_specs = [tile(W_ML), tile(LANES)]
    args = [pml, pif]
    if has_state:
        in_specs += [c_spec, m_spec, cv_spec]
        args += [caug0, m0, cv0]
    consts = [cw, cb, bif, tri]
    in_specs += [_full_spec(x.shape) for x in consts]
    return pl.pallas_call(
        functools.partial(_mlstm_kernel, c=c, t=t, has_state=has_state),
        grid=(b, l // t),
        in_specs=in_specs,
        out_specs=[tile(ML_W), c_spec, m_spec, cv_spec],
        out_shape=[jax.ShapeDtypeStruct((b, l, ML_W), BF16),
                   jax.ShapeDtypeStruct((b, ML_HEADS, ML_DH, 2 * ML_DH), F32),
                   jax.ShapeDtypeStruct((b, ML_HEADS, 8, LANES), F32),
                   jax.ShapeDtypeStruct((b, CONV_W - 1, 2 * ML_W), F32)],
        scratch_shapes=[pltpu.VMEM((ML_HEADS, ML_DH, 2 * ML_DH), F32),
                        pltpu.VMEM((ML_HEADS, 8, LANES), F32),
                        pltpu.VMEM((8, 2 * ML_W), F32),
                        pltpu.VMEM((t, 2 * ML_W), BF16),
                        pltpu.VMEM((t, LANES), F32)],
        compiler_params=pltpu.CompilerParams(
            dimension_semantics=("arbitrary", "arbitrary"), vmem_limit_bytes=VMEM_LIMIT),
        name="mlstm",
    )(*args, *consts)


def _merge_kernel(x_ref, ga_ref, hb_ref, pmg_ref, wug_ref, wum_ref, wo_ref, g2_ref,
                  wr_hi_ref, wr_lo_ref, br_ref, x1_ref, hm_ref, comb_ref):
    ya = _dot(ga_ref[...], wug_ref[...])
    yb = _dot(hb_ref[...], wum_ref[...])
    z = _sigmoid(pmg_ref[:, 0:D_MODEL]) * ya + _sigmoid(pmg_ref[:, D_MODEL:]) * yb
    x1 = x_ref[...] + _dot(z.astype(BF16), wo_ref[...])
    x1_ref[...] = x1
    hm = _rms(x1, g2_ref[...])
    hm_hi = hm.astype(BF16)
    hm_ref[...] = hm_hi
    hm_lo = (hm - hm_hi.astype(F32)).astype(BF16)
    logits = (_dot(hm_hi, wr_hi_ref[...]) + _dot(hm_lo, wr_hi_ref[...])
              + _dot(hm_hi, wr_lo_ref[...]) + br_ref[...])
    lane = lax.broadcasted_iota(jnp.int32, logits.shape, 1)
    neg = -jnp.inf
    is_g = (lane >= N_EXPERTS) & (lane < N_EXPERTS + N_GROUPS)
    lg = jnp.where(is_g, logits, neg)
    mg = jnp.max(lg, axis=-1, keepdims=True)
    p_top = 1.0 / jnp.sum(jnp.exp(lg - mg), axis=-1, keepdims=True)
    gi = jnp.min(jnp.where(lg == mg, lane, 2 * LANES), axis=-1, keepdims=True) - N_EXPERTS
    group_shift = int(math.log2(EXPERTS_PER_GROUP))
    sel = (lane < N_EXPERTS) & (jnp.right_shift(lane, group_shift) == gi)
    le = jnp.where(sel, logits, neg)
    v1 = jnp.max(le, axis=-1, keepdims=True)
    i1 = jnp.min(jnp.where(le == v1, lane, 2 * LANES), axis=-1, keepdims=True)
    le2 = jnp.where(lane == i1, neg, le)
    v2 = jnp.max(le2, axis=-1, keepdims=True)
    i2 = jnp.min(jnp.where(le2 == v2, lane, 2 * LANES), axis=-1, keepdims=True)
    e2 = jnp.exp(v2 - v1)
    wa = 1.0 / (1.0 + e2)
    wb = e2 / (1.0 + e2)
    comb_ref[...] = p_top * (jnp.where(lane == i1, wa, 0.0) + jnp.where(lane == i2, wb, 0.0))


def _merge_call(x2, ga, hb, pmg, wug, wum, wo, g2, wr_hi, wr_lo, br, tm):
    n = x2.shape[0]
    tile = lambda w: pl.BlockSpec((tm, w), lambda i: (i, 0))
    consts = [wug, wum, wo, g2, wr_hi, wr_lo, br]
    return pl.pallas_call(
        _merge_kernel,
        grid=(n // tm,),
        in_specs=[tile(D_MODEL), tile(GLA_V), tile(ML_W), tile(W_MG)]
                 + [_full_spec(x.shape) for x in consts],
        out_specs=[tile(D_MODEL), tile(D_MODEL), tile(LANES)],
        out_shape=[jax.ShapeDtypeStruct((n, D_MODEL), F32),
                   jax.ShapeDtypeStruct((n, D_MODEL), BF16),
                   jax.ShapeDtypeStruct((n, LANES), F32)],
        compiler_params=pltpu.CompilerParams(
            dimension_semantics=("arbitrary",), vmem_limit_bytes=VMEM_LIMIT),
        name="merge",
    )(x2, ga, hb, pmg, *consts)


def _moe_kernel(x1_ref, hm_ref, comb_ref, wg_ref, wu_ref, wd_ref, gf_ref, y_ref, acc_ref):
    e = pl.program_id(1)

    @pl.when(e == 0)
    def _():
        acc_ref[...] = jnp.zeros_like(acc_ref)

    hm = hm_ref[...]
    a = _dot(hm, wg_ref[...])
    u = _dot(hm, wu_ref[...])
    lane = lax.broadcasted_iota(jnp.int32, comb_ref.shape, 1)
    ce = jnp.sum(jnp.where(lane == e, comb_ref[...], 0.0), axis=-1, keepdims=True)
    hh = (a * _sigmoid(a)) * u * ce
    acc_ref[...] += _dot(hh.astype(BF16), wd_ref[...])

    @pl.when(e == pl.num_programs(1) - 1)
    def _():
        y_ref[...] = _rms(x1_ref[...] + acc_ref[...], gf_ref[...])


def _moe_call(x1, hm, comb, wg, wu, wd, gf, tm):
    n = x1.shape[0]
    tile = lambda w: pl.BlockSpec((tm, w), lambda i, e: (i, 0))
    return pl.pallas_call(
        _moe_kernel,
        grid=(n // tm, N_EXPERTS),
        in_specs=[tile(D_MODEL), tile(D_MODEL), tile(LANES),
                  pl.BlockSpec((None, D_MODEL, D_EXPERT), lambda i, e: (e, 0, 0)),
                  pl.BlockSpec((None, D_MODEL, D_EXPERT), lambda i, e: (e, 0, 0)),
                  pl.BlockSpec((None, D_EXPERT, D_MODEL), lambda i, e: (e, 0, 0)),
                  _full_spec(gf.shape)],
        out_specs=tile(D_MODEL),
        out_shape=jax.ShapeDtypeStruct((n, D_MODEL), F32),
        scratch_shapes=[pltpu.VMEM((tm, D_MODEL), F32)],
        compiler_params=pltpu.CompilerParams(
            dimension_semantics=("arbitrary", "arbitrary"), vmem_limit_bytes=VMEM_LIMIT),
        name="moe",
    )(x1, hm, comb, wg, wu, wd, gf)


def _pad_cols(w, width):
    return jnp.pad(w, ((0, 0), (0, width - w.shape[1])))


def _prep_weights(norm1_g, w_in, gla_w_gate2, gla_b_gate, gla_norm_g, w_up_gla,
                  ml_conv_w, ml_conv_b, ml_b_i, ml_b_f, w_up_ml, w_out,
                  norm2_g, router_g_w, router_g_b, router_e_w, router_e_b,
                  moe_w_gate, moe_w_up, moe_w_down, final_g):
    o_lr = W_GLA
    o_ml = o_lr + GLA_GATE_RANK
    o_if = o_ml + W_ML
    o_mg = o_if + 2 * ML_HEADS
    w_p = jnp.concatenate([
        w_in[:, 0:o_lr], _pad_cols(w_in[:, o_lr:o_ml], LANES),
        w_in[:, o_ml:o_if], _pad_cols(w_in[:, o_if:o_mg], LANES),
        w_in[:, o_mg:]], axis=1).astype(BF16)
    wr = _pad_cols(jnp.concatenate([router_e_w, router_g_w], axis=1), LANES)
    wr_hi = wr.astype(BF16)
    wr_lo = (wr - wr_hi.astype(F32)).astype(BF16)
    br = _pad_cols(jnp.concatenate([router_e_b, router_g_b])[None, :], LANES)
    return dict(
        g1=norm1_g[None, :], w_p=w_p,
        wg2_p=jnp.pad(gla_w_gate2, ((0, LANES - GLA_GATE_RANK), (0, 0))),
        bg=gla_b_gate[None, :], gn=gla_norm_g[None, :],
        wug=w_up_gla.astype(BF16),
        cw=ml_conv_w, cb=ml_conv_b[None, :],
        bif=_pad_cols(jnp.concatenate([ml_b_i, ml_b_f])[None, :], LANES),
        wum=w_up_ml.astype(BF16), wo=w_out.astype(BF16),
        g2=norm2_g[None, :], wr_hi=wr_hi, wr_lo=wr_lo, br=br,
        wg=moe_w_gate.reshape(N_EXPERTS, D_MODEL, D_EXPERT).astype(BF16),
        wu=moe_w_up.reshape(N_EXPERTS, D_MODEL, D_EXPERT).astype(BF16),
        wd=moe_w_down.reshape(N_EXPERTS, D_EXPERT, D_MODEL).astype(BF16),
        gf=final_g[None, :],
    )


def _layer(x, gla_s0, ml_c0, ml_n0, ml_m0, conv0, p, *, chunk, seq_tile, row_tile, moe_tile):
    b, l, _ = x.shape
    n = b * l
    x2 = x.reshape(n, D_MODEL)
    pg, plr, pml, pif, pmg = _proj_call(x2, p["g1"], p["w_p"], row_tile)
    r3 = lambda a: a.reshape(b, l, a.shape[-1])
    ga, gla_s = _gla_call(r3(pg), r3(plr), gla_s0, p["wg2_p"], p["bg"], p["gn"], chunk, seq_tile)
    if ml_c0 is None:
        caug0 = m0 = None
    else:
        caug0 = jnp.concatenate(
            [ml_c0, ml_n0[..., None], jnp.zeros(ml_c0.shape[:-1] + (ML_DH - 1,), F32)], axis=-1)
        m0 = jnp.broadcast_to(ml_m0[..., None, None], ml_m0.shape + (8, LANES))
    hb, caug, m_b, new_conv = _mlstm_call(r3(pml), r3(pif), caug0, m0, conv0,
                                          p["cw"], p["cb"], p["bif"], chunk, seq_tile)
    x1, hm, comb = _merge_call(x2, ga.reshape(n, GLA_V), hb.reshape(n, ML_W), pmg,
                               p["wug"], p["wum"], p["wo"], p["g2"],
                               p["wr_hi"], p["wr_lo"], p["br"], row_tile)
    y = _moe_call(x1, hm, comb, p["wg"], p["wu"], p["wd"], p["gf"], moe_tile)
    return (y.reshape(b, l, D_MODEL), gla_s[None], caug[..., 0:ML_DH][None],
            caug[..., ML_DH][None], m_b[..., 0, 0][None], new_conv[None])


def kernel(x_prompt, x_sample, state_gla_S, state_mlstm_C, state_mlstm_n, state_mlstm_m, state_mlstm_conv, norm1_g, w_in, gla_w_gate2, gla_b_gate, gla_norm_g, w_up_gla, ml_conv_w, ml_conv_b, ml_b_i, ml_b_f, w_up_ml, w_out, norm2_g, router_g_w, router_g_b, router_e_w, router_e_b, moe_w_gate, moe_w_up, moe_w_down, final_g):
    assert norm1_g.shape[0] == 1, "single-layer trunk"
    p = _prep_weights(norm1_g[0], w_in[0], gla_w_gate2[0], gla_b_gate[0], gla_norm_g[0], w_up_gla[0],
                      ml_conv_w[0], ml_conv_b[0], ml_b_i[0], ml_b_f[0], w_up_ml[0], w_out[0],
                      norm2_g[0], router_g_w[0], router_g_b[0], router_e_w[0], router_e_b[0],
                      moe_w_gate[0], moe_w_up[0], moe_w_down[0], final_g)
    yp, *sp = _layer(x_prompt, None, None, None, None, None, p,
                     chunk=128, seq_tile=512, row_tile=256, moe_tile=1024)
    dec_seq = x_sample.shape[1]
    ns = x_sample.shape[0] * dec_seq
    ys, *ss = _layer(x_sample, state_gla_S[0], state_mlstm_C[0], state_mlstm_n[0], state_mlstm_m[0],
                     state_mlstm_conv[0], p,
                     chunk=dec_seq, seq_tile=dec_seq, row_tile=ns, moe_tile=ns)
    return (yp, ys, *sp, *ss)
```

```python
import functools
import math

import numpy as np
import jax
import jax.numpy as jnp
from jax import lax
from jax.experimental import pallas as pl
from jax.experimental.pallas import tpu as pltpu

D_MODEL = 1024
GLA_HEADS = 4
GLA_DK = 64
GLA_DV = 128
GLA_GATE_RANK = 16
GLA_TAU = 16.0
ML_HEADS = 4
ML_DH = 128
CONV_W = 4
N_GROUPS = 4
EXPERTS_PER_GROUP = 8
N_EXPERTS = N_GROUPS * EXPERTS_PER_GROUP
D_EXPERT = 256
EPS = 1e-6

GLA_QK = GLA_HEADS * GLA_DK
GLA_V = GLA_HEADS * GLA_DV
ML_W = ML_HEADS * ML_DH

LANES = 128
VMEM_LIMIT = 56 * 1024 * 1024

W_GLA = 2 * GLA_QK + 2 * GLA_V
W_ML = 2 * ML_W + ML_W + ML_W
W_MG = 2 * D_MODEL
PROJ_WIDTHS = (W_GLA, LANES, W_ML, LANES, W_MG)

F32 = jnp.float32
BF16 = jnp.bfloat16


def _dot(a, b):
    return jnp.dot(a, b, preferred_element_type=F32)


def _dot_nt(a, b):
    return lax.dot_general(a, b, (((1,), (1,)), ((), ())), preferred_element_type=F32)


def _dot_tn(a, b):
    return lax.dot_general(a, b, (((0,), (0,)), ((), ())), preferred_element_type=F32)


def _split3(x):
    hi = x.astype(BF16)
    r1 = x - hi.astype(F32)
    mid = r1.astype(BF16)
    lo = (r1 - mid.astype(F32)).astype(BF16)
    return hi, mid, lo


def _dot_exact_lhs(m, x):
    hi, mid, lo = _split3(x)
    return _dot(m, hi) + _dot(m, mid) + _dot(m, lo)


def _log_sigmoid(z):
    return jnp.minimum(z, 0.0) - jnp.log1p(jnp.exp(-jnp.abs(z)))


def _sigmoid(z):
    return 1.0 / (1.0 + jnp.exp(-z))


def _rms(x, g):
    return x * lax.rsqrt(jnp.mean(x * x, axis=-1, keepdims=True) + EPS) * g


def _full_spec(shape):
    nd = len(shape)
    return pl.BlockSpec(shape, lambda *_: (0,) * nd)


def _proj_kernel(x_ref, g_ref, w_ref, *o_refs):
    h = _rms(x_ref[...], g_ref[...]).astype(BF16)
    off = 0
    for o_ref, width in zip(o_refs, PROJ_WIDTHS):
        o_ref[...] = _dot(h, w_ref[:, off:off + width])
        off += width


def _proj_call(x2, g, w_p, tm):
    n = x2.shape[0]
    return pl.pallas_call(
        _proj_kernel,
        grid=(n // tm,),
        in_specs=[pl.BlockSpec((tm, D_MODEL), lambda i: (i, 0)),
                  _full_spec(g.shape), _full_spec(w_p.shape)],
        out_specs=[pl.BlockSpec((tm, w), lambda i: (i, 0)) for w in PROJ_WIDTHS],
        out_shape=[jax.ShapeDtypeStruct((n, w), F32) for w in PROJ_WIDTHS],
        compiler_params=pltpu.CompilerParams(
            dimension_semantics=("arbitrary",), vmem_limit_bytes=VMEM_LIMIT),
        name="in_proj",
    )(x2, g, w_p)


def _gla_consts(c):
    nlev = int(math.log2(c))
    assert 1 << nlev == c
    t = np.arange(c)[:, None]
    j = np.arange(c)[None, :]
    mats = [j <= t, j > t]
    lv = np.full((c, c), -1, np.int32)
    for l in range(nlev):
        h = c >> (l + 1)
        start = (t // (2 * h)) * (2 * h)
        m = start + h - 1
        upper = (t - start) >= h
        mats.append(np.where(upper, (j > m) & (j <= t), (j > t) & (j <= m)))
        same = (j // (2 * h)) == (t // (2 * h))
        s_lower = (j % (2 * h)) < h
        lv[np.broadcast_to(upper, (c, c)) & same & s_lower] = l
    lv[np.eye(c, dtype=bool)] = nlev
    m_all = np.concatenate(mats, 0).astype(np.float32)
    return jnp.asarray(m_all, BF16), jnp.asarray(lv)


def _gla_kernel(*refs, c, t, has_state):
    if has_state:
        (pg_ref, plr_ref, s0_ref, wg2_ref, bg_ref, gn_ref, mall_ref, lv_ref,
         o_ref, sout_ref, s_scr) = refs
    else:
        (pg_ref, plr_ref, wg2_ref, bg_ref, gn_ref, mall_ref, lv_ref,
         o_ref, sout_ref, s_scr) = refs
    nlev = int(math.log2(c))
    step = pl.program_id(1)

    @pl.when(step == 0)
    def _():
        if has_state:
            s_scr[...] = s0_ref[...]
        else:
            s_scr[...] = jnp.zeros_like(s_scr)

    ones = jnp.ones((c, LANES), BF16)

    def chunk(ci, carry):
        r0 = pl.multiple_of(ci * c, c)
        rows = pl.ds(r0, c)
        q = pg_ref[rows, 0:GLA_QK] * (GLA_DK ** -0.5)
        k = pg_ref[rows, GLA_QK:2 * GLA_QK]
        v = pg_ref[rows, 2 * GLA_QK:2 * GLA_QK + GLA_V].astype(BF16)
        z = lax.dot_general(plr_ref[rows, :], wg2_ref[...], (((1,), (0,)), ((), ())),
                            precision=lax.Precision.HIGHEST,
                            preferred_element_type=F32) + bg_ref[...]
        la = _log_sigmoid(z) * (1.0 / GLA_TAU)
        la3 = _split3(la)
        mall = mall_ref[...]
        ex = _dot(mall, la3[0]) + _dot(mall, la3[1]) + _dot(mall, la3[2])
        qe = (q * jnp.exp(ex[0:c])).astype(BF16)
        kl = (k * jnp.exp(ex[c:2 * c])).astype(BF16)
        dsum = _dot_tn(la3[0], ones) + _dot_tn(la3[1], ones) + _dot_tn(la3[2], ones)
        lv = lv_ref[...]
        qb = q.astype(BF16)
        kb = k.astype(BF16)
        a = []
        for h in range(GLA_HEADS):
            hs = slice(h * GLA_DK, (h + 1) * GLA_DK)
            a.append(jnp.where(lv == nlev, _dot_nt(qb[:, hs], kb[:, hs]), 0.0))
        for l in range(nlev):
            e = jnp.exp(ex[(2 + l) * c:(3 + l) * c])
            qt = (q * e).astype(BF16)
            kt = (k * e).astype(BF16)
            for h in range(GLA_HEADS):
                hs = slice(h * GLA_DK, (h + 1) * GLA_DK)
                a[h] = jnp.where(lv == l, _dot_nt(qt[:, hs], kt[:, hs]), a[h])
        for h in range(GLA_HEADS):
            hs = slice(h * GLA_DK, (h + 1) * GLA_DK)
            vs = slice(h * GLA_DV, (h + 1) * GLA_DV)
            s_h = s_scr[h]
            o = _dot(a[h].astype(BF16), v[:, vs]) + _dot(qe[:, hs], s_h.astype(BF16))
            on = _rms(o, gn_ref[:, vs])
            r = pg_ref[rows, 2 * GLA_QK + GLA_V + h * GLA_DV:2 * GLA_QK + GLA_V + (h + 1) * GLA_DV]
            o_ref[rows, vs] = (on * (r * _sigmoid(r))).astype(o_ref.dtype)
            s_scr[h] = jnp.exp(dsum[hs, :]) * s_h + _dot_tn(kl[:, hs], v[:, vs])
        return carry

    lax.fori_loop(0, t // c, chunk, 0)

    @pl.when(step == pl.num_programs(1) - 1)
    def _():
        sout_ref[...] = s_scr[...]


def _gla_call(pg, plr, s0, wg2_p, bg, gn, c, t):
    b, l, _ = pg.shape
    m_all, lv = _gla_consts(c)
    has_state = s0 is not None
    tile = lambda w: pl.BlockSpec((None, t, w), lambda bi, i: (bi, i, 0))
    state_spec = pl.BlockSpec((None, GLA_HEADS, GLA_DK, GLA_DV), lambda bi, i: (bi, 0, 0, 0))
    in_specs = [tile(W_GLA), tile(LANES)]
    args = [pg, plr]
    if has_state:
        in_specs.append(state_spec)
        args.append(s0)
    consts = [wg2_p, bg, gn, m_all, lv]
    in_specs += [_full_spec(x.shape) for x in consts]
    return pl.pallas_call(
        functools.partial(_gla_kernel, c=c, t=t, has_state=has_state),
        grid=(b, l // t),
        in_specs=in_specs,
        out_specs=[tile(GLA_V), state_spec],
        out_shape=[jax.ShapeDtypeStruct((b, l, GLA_V), BF16),
                   jax.ShapeDtypeStruct((b, GLA_HEADS, GLA_DK, GLA_DV), F32)],
        scratch_shapes=[pltpu.VMEM((GLA_HEADS, GLA_DK, GLA_DV), F32)],
        compiler_params=pltpu.CompilerParams(
            dimension_semantics=("arbitrary", "arbitrary"), vmem_limit_bytes=VMEM_LIMIT),
        name="gla",
    )(*args, *consts)


def _mlstm_kernel(*refs, c, t, has_state):
    if has_state:
        (pml_ref, pif_ref, c0_ref, m0_ref, cv0_ref, cw_ref, cb_ref, bif_ref, tri_ref,
         o_ref, cout_ref, mout_ref, cvout_ref, c_scr, m_scr, cv_scr, qk_scr, b_scr) = refs
    else:
        (pml_ref, pif_ref, cw_ref, cb_ref, bif_ref, tri_ref,
         o_ref, cout_ref, mout_ref, cvout_ref, c_scr, m_scr, cv_scr, qk_scr, b_scr) = refs
    step = pl.program_id(1)
    hist = CONV_W - 1
    pad = 8

    @pl.when(step == 0)
    def _():
        if has_state:
            c_scr[...] = c0_ref[...]
            m_scr[...] = m0_ref[...]
            cv_scr[0:pad - hist, :] = jnp.zeros((pad - hist, 2 * ML_W), F32)
            cv_scr[pad - hist:pad, :] = cv0_ref[...]
        else:
            c_scr[...] = jnp.zeros_like(c_scr)
            m_scr[...] = jnp.zeros_like(m_scr)
            cv_scr[...] = jnp.zeros_like(cv_scr)

    x = pml_ref[:, 0:2 * ML_W]
    ext = jnp.concatenate([cv_scr[...], x], axis=0)
    acc = cb_ref[...] + ext[pad:pad + t] * cw_ref[CONV_W - 1:CONV_W, :]
    for d in range(1, CONV_W):
        acc = acc + ext[pad - d:pad - d + t] * cw_ref[CONV_W - 1 - d:CONV_W - d, :]
    conv = acc * _sigmoid(acc)
    qk_scr[:, 0:ML_W] = conv[:, 0:ML_W].astype(BF16)
    qk_scr[:, ML_W:] = (conv[:, ML_W:] * (ML_DH ** -0.5)).astype(BF16)
    cv_scr[...] = x[t - pad:t, :]

    gts = pif_ref[...] + bif_ref[...]
    lane = lax.broadcasted_iota(jnp.int32, (1, LANES), 1)
    b_scr[...] = jnp.where(lane < ML_HEADS, gts, _log_sigmoid(gts))

    row = lax.broadcasted_iota(jnp.int32, (c, c), 0)
    col = lax.broadcasted_iota(jnp.int32, (c, c), 1)
    causal = col <= row
    eye = col == row
    ones_cc = jnp.ones((c, c), BF16)
    one_col = jnp.where(lax.broadcasted_iota(jnp.int32, (c, LANES), 1) == 0, 1.0, 0.0).astype(BF16)

    def chunk(ci, carry):
        r0 = pl.multiple_of(ci * c, c)
        rows = pl.ds(r0, c)
        gt = b_scr[rows, :]
        bcum = _dot_exact_lhs(tri_ref[...], gt)
        for h in range(ML_HEADS):
            hs = slice(h * ML_DH, (h + 1) * ML_DH)
            qh = qk_scr[rows, h * ML_DH:(h + 1) * ML_DH]
            kh = qk_scr[rows, ML_W + h * ML_DH:ML_W + (h + 1) * ML_DH]
            vaug = jnp.concatenate(
                [pml_ref[rows, 2 * ML_W + h * ML_DH:2 * ML_W + (h + 1) * ML_DH].astype(BF16), one_col],
                axis=1)
            b_h = bcum[:, ML_HEADS + h:ML_HEADS + h + 1]
            w = gt[:, h:h + 1] - b_h
            wrow = _dot_exact_lhs(ones_cc, jnp.where(eye, w, 0.0))
            mprev = m_scr[h][0:1, 0:1]
            g = jnp.maximum(mprev, jnp.max(jnp.where(causal, wrow, -jnp.inf), axis=-1, keepdims=True))
            p = jnp.exp(jnp.where(causal, wrow - g, -jnp.inf))
            w_inter = jnp.exp(mprev - g)
            s = (_dot_nt(qh, kh) * p).astype(BF16)
            caug = c_scr[h]
            nd = _dot(s, vaug) + w_inter * _dot(qh, caug.astype(BF16))
            num = nd[:, 0:ML_DH]
            den = nd[:, ML_DH:ML_DH + 1]
            hh = num / jnp.maximum(jnp.abs(den), jnp.exp(-(b_h + g)))
            mo = pml_ref[rows, 3 * ML_W + h * ML_DH:3 * ML_W + (h + 1) * ML_DH]
            o_ref[rows, hs] = (_sigmoid(mo) * hh).astype(o_ref.dtype)
            g_last = g[c - 1:c, :]
            ks = (kh.astype(F32) * jnp.exp(w - g_last)).astype(BF16)
            c_scr[h] = jnp.exp(mprev - g_last) * caug + _dot_tn(ks, vaug)
            m_scr[h] = jnp.broadcast_to(b_h[c - 1:c, :] + g_last, m_scr.shape[1:])
        return carry

    lax.fori_loop(0, t // c, chunk, 0)

    @pl.when(step == pl.num_programs(1) - 1)
    def _():
        cout_ref[...] = c_scr[...]
        mout_ref[...] = m_scr[...]
        cvout_ref[...] = pml_ref[t - hist:t, 0:2 * ML_W]


def _mlstm_call(pml, pif, caug0, m0, cv0, cw, cb, bif, c, t):
    b, l, _ = pml.shape
    has_state = caug0 is not None
    tri = jnp.asarray(np.tril(np.ones((c, c), np.float32)), BF16)
    tile = lambda w: pl.BlockSpec((None, t, w), lambda bi, i: (bi, i, 0))
    c_spec = pl.BlockSpec((None, ML_HEADS, ML_DH, 2 * ML_DH), lambda bi, i: (bi, 0, 0, 0))
    m_spec = pl.BlockSpec((None, ML_HEADS, 8, LANES), lambda bi, i: (bi, 0, 0, 0))
    cv_spec = pl.BlockSpec((None, CONV_W - 1, 2 * ML_W), lambda bi, i: (bi, 0, 0))
    in_specs = [tile(W_ML), tile(LANES)]
    args = [pml, pif]
    if has_state:
        in_specs += [c_spec, m_spec, cv_spec]
        args += [caug0, m0, cv0]
    consts = [cw, cb, bif, tri]
    in_specs += [_full_spec(x.shape) for x in consts]
    return pl.pallas_call(
        functools.partial(_mlstm_kernel, c=c, t=t, has_state=has_state),
        grid=(b, l // t),
        in_specs=in_specs,
        out_specs=[tile(ML_W), c_spec, m_spec, cv_spec],
        out_shape=[jax.ShapeDtypeStruct((b, l, ML_W), BF16),
                   jax.ShapeDtypeStruct((b, ML_HEADS, ML_DH, 2 * ML_DH), F32),
                   jax.ShapeDtypeStruct((b, ML_HEADS, 8, LANES), F32),
                   jax.ShapeDtypeStruct((b, CONV_W - 1, 2 * ML_W), F32)],
        scratch_shapes=[pltpu.VMEM((ML_HEADS, ML_DH, 2 * ML_DH), F32),
                        pltpu.VMEM((ML_HEADS, 8, LANES), F32),
                        pltpu.VMEM((8, 2 * ML_W), F32),
                        pltpu.VMEM((t, 2 * ML_W), BF16),
                        pltpu.VMEM((t, LANES), F32)],
        compiler_params=pltpu.CompilerParams(
            dimension_semantics=("arbitrary", "arbitrary"), vmem_limit_bytes=VMEM_LIMIT),
        name="mlstm",
    )(*args, *consts)


def _merge_kernel(*refs, sparse):
    if sparse:
        (x_ref, ga_ref, hb_ref, pmg_ref, wug_ref, wum_ref, wo_ref, g2_ref,
         wr_hi_ref, wr_lo_ref, br_ref, tril_ref, x1_ref, hm_ref, meta_ref, cnt_ref, cnt_scr) = refs
    else:
        (x_ref, ga_ref, hb_ref, pmg_ref, wug_ref, wum_ref, wo_ref, g2_ref,
         wr_hi_ref, wr_lo_ref, br_ref, x1_ref, hm_ref, comb_ref) = refs
    ya = _dot(ga_ref[...], wug_ref[...])
    yb = _dot(hb_ref[...], wum_ref[...])
    z = _sigmoid(pmg_ref[:, 0:D_MODEL]) * ya + _sigmoid(pmg_ref[:, D_MODEL:]) * yb
    x1 = x_ref[...] + _dot(z.astype(BF16), wo_ref[...])
    x1_ref[...] = x1
    hm = _rms(x1, g2_ref[...])
    hm_hi = hm.astype(BF16)
    hm_ref[...] = hm.astype(hm_ref.dtype)
    hm_lo =(hm - hm_hi.astype(F32)).astype(BF16)
    logits = (_dot(hm_hi, wr_hi_ref[...]) + _dot(hm_lo, wr_hi_ref[...])
              + _dot(hm_hi, wr_lo_ref[...]) + br_ref[...])
    lane = lax.broadcasted_iota(jnp.int32, logits.shape, 1)
    neg = -jnp.inf
    is_g = (lane >= N_EXPERTS) & (lane < N_EXPERTS + N_GROUPS)
    lg = jnp.where(is_g, logits, neg)
    mg = jnp.max(lg, axis=-1, keepdims=True)
    p_top = 1.0 / jnp.sum(jnp.exp(lg - mg), axis=-1, keepdims=True)
    gi = jnp.min(jnp.where(lg == mg, lane, 2 * LANES), axis=-1, keepdims=True) - N_EXPERTS
    group_shift = int(math.log2(EXPERTS_PER_GROUP))
    sel = (lane < N_EXPERTS) & (jnp.right_shift(lane, group_shift) == gi)
    le = jnp.where(sel, logits, neg)
    v1 = jnp.max(le, axis=-1, keepdims=True)
    i1 = jnp.min(jnp.where(le == v1, lane, 2 * LANES), axis=-1, keepdims=True)
    le2 = jnp.where(lane == i1, neg, le)
    v2 = jnp.max(le2, axis=-1, keepdims=True)
    i2 = jnp.min(jnp.where(le2 == v2, lane, 2 * LANES), axis=-1, keepdims=True)
    e2 = jnp.exp(v2 - v1)
    wa = 1.0 / (1.0 + e2)
    wb = e2 / (1.0 + e2)
    if not sparse:
        comb_ref[...] = p_top * (jnp.where(lane == i1, wa, 0.0) + jnp.where(lane == i2, wb, 0.0))
        return

    @pl.when(pl.program_id(0) == 0)
    def _():
        cnt_scr[...] = jnp.zeros_like(cnt_scr)

    oh1 = lane == i1
    oh2 = lane == i2
    both = jnp.where(oh1 | oh2, 1.0, 0.0)
    before = _dot(tril_ref[...], both.astype(BF16)) + cnt_scr[...]
    r1 = jnp.sum(jnp.where(oh1, before, 0.0), axis=-1, keepdims=True)
    r2 = jnp.sum(jnp.where(oh2, before, 0.0), axis=-1, keepdims=True)
    cols = (i1.astype(F32), i2.astype(F32), r1, r2, p_top * wa, p_top * wb)
    meta = jnp.zeros(logits.shape, F32)
    for ci, col in enumerate(cols):
        meta = jnp.where(lane == ci, col, meta)
    meta_ref[...] = meta
    cnt_scr[...] += jnp.sum(both, axis=0, keepdims=True)
    cnt_ref[...] = cnt_scr[...]


def _merge_call(x2, ga, hb, pmg, wug, wum, wo, g2, wr_hi, wr_lo, br, tm, sparse):
    n = x2.shape[0]
    tile = lambda w: pl.BlockSpec((tm, w), lambda i: (i, 0))
    consts = [wug, wum, wo, g2, wr_hi, wr_lo, br]
    out_specs = [tile(D_MODEL), tile(D_MODEL), tile(LANES)]
    out_shape = [jax.ShapeDtypeStruct((n, D_MODEL), F32),
                 jax.ShapeDtypeStruct((n, D_MODEL), F32 if sparse else BF16),
                 jax.ShapeDtypeStruct((n, LANES), F32)]
    scratch = []
    if sparse:
        consts.append(jnp.asarray(np.tril(np.ones((tm, tm), np.float32), -1), BF16))
        out_specs.append(_full_spec((1, LANES)))
        out_shape.append(jax.ShapeDtypeStruct((1, LANES), F32))
        scratch.append(pltpu.VMEM((1, LANES), F32))
    return pl.pallas_call(
        functools.partial(_merge_kernel, sparse=sparse),
        grid=(n // tm,),
        in_specs=[tile(D_MODEL), tile(GLA_V), tile(ML_W), tile(W_MG)]
                 + [_full_spec(x.shape) for x in consts],
        out_specs=out_specs,
        out_shape=out_shape,
        scratch_shapes=scratch,
        compiler_params=pltpu.CompilerParams(
            dimension_semantics=("arbitrary",), vmem_limit_bytes=VMEM_LIMIT),
        name="merge",
    )(x2, ga, hb, pmg, *consts)


MOE_ROWS = 256


def _row_copy(src, i, dst, j, sem):
    return pltpu.make_async_copy(src.at[pl.ds(i, 1), :], dst.at[pl.ds(j, 1), :], sem)


def _dispatch_kernel(pos_ref, tail_ref, hm_ref, xs_ref, zero_ref, sem, zsem, *, tm):
    step = pl.program_id(0)

    @pl.when(step == 0)
    def _():
        zero_ref[...] = jnp.zeros_like(zero_ref)
        n_tiles = xs_ref.shape[0] // MOE_ROWS

        def clear(row):
            start = pl.multiple_of(row, MOE_ROWS)
            return pltpu.make_async_copy(zero_ref, xs_ref.at[pl.ds(start, MOE_ROWS), :], zsem)

        def unused(fn):
            def body(j, carry):
                fn(clear(j * MOE_ROWS))
                return carry
            lax.fori_loop(tail_ref[2 * N_EXPERTS], n_tiles, body, 0)

        for e in range(N_EXPERTS):
            @pl.when(tail_ref[N_EXPERTS + e] > 0)
            def _():
                clear(tail_ref[e]).start()
        unused(lambda c: c.start())
        for e in range(N_EXPERTS):
            @pl.when(tail_ref[N_EXPERTS + e] > 0)
            def _():
                clear(tail_ref[e]).wait()
        unused(lambda c: c.wait())

    base = step * tm

    def issue(t, carry):
        for slot in range(2):
            _row_copy(hm_ref, t, xs_ref, pos_ref[2 * (base + t) + slot], sem).start()
        return carry

    def drain(t, carry):
        for slot in range(2):
            _row_copy(hm_ref, t, xs_ref, pos_ref[2 * (base + t) + slot], sem).wait()
        return carry

    lax.fori_loop(0, tm, issue, 0, unroll=8)
    lax.fori_loop(0, tm, drain, 0, unroll=8)


def _dispatch_call(pos, tail, hm, n_rows, tm):
    n = hm.shape[0]
    return pl.pallas_call(
        functools.partial(_dispatch_kernel, tm=tm),
        grid_spec=pltpu.PrefetchScalarGridSpec(
            num_scalar_prefetch=2,
            grid=(n // tm,),
            in_specs=[pl.BlockSpec((tm, D_MODEL), lambda i, *_: (i, 0))],
            out_specs=pl.BlockSpec(memory_space=pl.ANY),
            scratch_shapes=[pltpu.VMEM((MOE_ROWS, D_MODEL), F32),
                            pltpu.SemaphoreType.DMA, pltpu.SemaphoreType.DMA],
        ),
        out_shape=jax.ShapeDtypeStruct((n_rows, D_MODEL), F32),
        compiler_params=pltpu.CompilerParams(
            dimension_semantics=("arbitrary",), vmem_limit_bytes=VMEM_LIMIT),
        name="moe_dispatch",
    )(pos, tail, hm)


def _gmm_kernel(te_ref, nv_ref, xs_ref, wg_ref, wu_ref, wd_ref, o_ref):
    used = pl.program_id(0) < nv_ref[0]

    @pl.when(used)
    def _():
        x = xs_ref[...].astype(BF16)
        a = _dot(x, wg_ref[...].astype(BF16))
        u = _dot(x, wu_ref[...].astype(BF16))
        hh = (a * _sigmoid(a)) * u
        o_ref[...] = _dot(hh.astype(BF16), wd_ref[...].astype(BF16))

    @pl.when(jnp.logical_not(used))
    def _():
        o_ref[...] = jnp.zeros_like(o_ref)


def _gmm_call(tile_expert, n_valid, xs, wg, wu, wd):
    n_tiles = xs.shape[0] // MOE_ROWS
    rows = lambda j, te, nv: (jnp.minimum(j, nv[0] - 1), 0)
    wsel = lambda j, te, nv: (te[j], 0, 0)
    return pl.pallas_call(
        _gmm_kernel,
        grid_spec=pltpu.PrefetchScalarGridSpec(
            num_scalar_prefetch=2,
            grid=(n_tiles,),
            in_specs=[pl.BlockSpec((MOE_ROWS, D_MODEL), rows),
                      pl.BlockSpec((None, D_MODEL, D_EXPERT), wsel),
                      pl.BlockSpec((None, D_MODEL, D_EXPERT), wsel),
                      pl.BlockSpec((None, D_EXPERT, D_MODEL), wsel)],
            out_specs=pl.BlockSpec((MOE_ROWS, D_MODEL), lambda j, te, nv: (j, 0)),
        ),
        out_shape=jax.ShapeDtypeStruct(xs.shape, F32),
        compiler_params=pltpu.CompilerParams(
            dimension_semantics=("arbitrary",), vmem_limit_bytes=VMEM_LIMIT),
        name="moe_grouped",
    )(tile_expert, n_valid, xs, wg, wu, wd)


def _combine_kernel(pos_ref, x1_ref, meta_ref, gf_ref, os_ref, y_ref, buf_ref, sem, *, tm):
    base = pl.program_id(0) * tm

    def issue(t, carry):
        for slot in range(2):
            _row_copy(os_ref, pos_ref[2 * (base + t) + slot], buf_ref.at[slot], t, sem).start()
        return carry

    def drain(t, carry):
        for slot in range(2):
            _row_copy(os_ref, pos_ref[2 * (base + t) + slot], buf_ref.at[slot], t, sem).wait()
        return carry

    lax.fori_loop(0, tm, issue, 0, unroll=8)
    lax.fori_loop(0, tm, drain, 0, unroll=8)
    y = x1_ref[...] + meta_ref[:, 4:5] * buf_ref[0] + meta_ref[:, 5:6] * buf_ref[1]
    y_ref[...] = _rms(y, gf_ref[...])


def _combine_call(pos, x1, meta, gf, out_sorted, tm):
    n = x1.shape[0]
    tile = lambda w: pl.BlockSpec((tm, w), lambda i, *_: (i, 0))
    return pl.pallas_call(
        functools.partial(_combine_kernel, tm=tm),
        grid_spec=pltpu.PrefetchScalarGridSpec(
            num_scalar_prefetch=1,
            grid=(n // tm,),
            in_specs=[tile(D_MODEL), tile(LANES), pl.BlockSpec(gf.shape, lambda i, *_: (0, 0)),
                      pl.BlockSpec(memory_space=pl.ANY)],
            out_specs=tile(D_MODEL),
            scratch_shapes=[pltpu.VMEM((2, tm, D_MODEL), F32), pltpu.SemaphoreType.DMA],
        ),
        out_shape=jax.ShapeDtypeStruct((n, D_MODEL), F32),
        compiler_params=pltpu.CompilerParams(
            dimension_semantics=("arbitrary",), vmem_limit_bytes=VMEM_LIMIT),
        name="moe_combine",
    )(pos, x1, meta, gf, out_sorted)


def _sparse_moe(x1, hm, meta, counts, p, tm):
    n = x1.shape[0]
    n_tiles = (2 * n) // MOE_ROWS + N_EXPERTS
    cnt = counts[0, :N_EXPERTS].astype(jnp.int32)
    tiles = (cnt + (MOE_ROWS - 1)) // MOE_ROWS
    tile_end = jnp.cumsum(tiles)
    tile_start = tile_end - tiles
    n_valid = tile_end[-1:]
    expert = meta[:, 0:2].astype(jnp.int32)
    rank = meta[:, 2:4].astype(jnp.int32)
    pos = (jnp.take(tile_start, expert) * MOE_ROWS + rank).reshape(-1)
    j = jnp.minimum(jnp.arange(n_tiles, dtype=jnp.int32), n_valid - 1)
    tile_expert = jnp.sum((j[:, None] >= tile_end[None, :]).astype(jnp.int32), axis=1)
    tail = jnp.concatenate([(tile_end - 1) * MOE_ROWS, cnt, n_valid]).astype(jnp.int32)
    xs = _dispatch_call(pos, tail, hm, n_tiles * MOE_ROWS, tm)
    out_sorted = _gmm_call(tile_expert, n_valid.astype(jnp.int32), xs, p["wg"], p["wu"], p["wd"])
    return _combine_call(pos, x1, meta, p["gf"], out_sorted, tm)


def _moe_kernel(x1_ref, hm_ref, comb_ref, wg_ref, wu_ref, wd_ref, gf_ref, y_ref, acc_ref):
    e = pl.program_id(1)

    @pl.when(e == 0)
    def _():
        acc_ref[...] = jnp.zeros_like(acc_ref)

    hm = hm_ref[...]
    a = _dot(hm, wg_ref[...].astype(BF16))
    u = _dot(hm, wu_ref[...].astype(BF16))
    lane = lax.broadcasted_iota(jnp.int32, comb_ref.shape, 1)
    ce = jnp.sum(jnp.where(lane == e, comb_ref[...], 0.0), axis=-1, keepdims=True)
    hh = (a * _sigmoid(a)) * u * ce
    acc_ref[...] += _dot(hh.astype(BF16), wd_ref[...].astype(BF16))

    @pl.when(e == pl.num_programs(1) - 1)
    def _():
        y_ref[...] = _rms(x1_ref[...] + acc_ref[...], gf_ref[...])


def _moe_call(x1, hm, comb, wg, wu, wd, gf, tm):
    n = x1.shape[0]
    tile = lambda w: pl.BlockSpec((tm, w), lambda i, e: (i, 0))
    return pl.pallas_call(
        _moe_kernel,
        grid=(n // tm, N_EXPERTS),
        in_specs=[tile(D_MODEL), tile(D_MODEL), tile(LANES),
                  pl.BlockSpec((None, D_MODEL, D_EXPERT), lambda i, e: (e, 0, 0)),
                  pl.BlockSpec((None, D_MODEL, D_EXPERT), lambda i, e: (e, 0, 0)),
                  pl.BlockSpec((None, D_EXPERT, D_MODEL), lambda i, e: (e, 0, 0)),
                  _full_spec(gf.shape)],
        out_specs=tile(D_MODEL),
        out_shape=jax.ShapeDtypeStruct((n, D_MODEL), F32),
        scratch_shapes=[pltpu.VMEM((tm, D_MODEL), F32)],
        compiler_params=pltpu.CompilerParams(
            dimension_semantics=("arbitrary", "arbitrary"), vmem_limit_bytes=VMEM_LIMIT),
        name="moe",
    )(x1, hm, comb, wg, wu, wd, gf)


def _pad_cols(w, width):
    return jnp.pad(w, ((0, 0), (0, width - w.shape[1])))


def _prep_weights(norm1_g, w_in, gla_w_gate2, gla_b_gate, gla_norm_g, w_up_gla,
                  ml_conv_w, ml_conv_b, ml_b_i, ml_b_f, w_up_ml, w_out,
                  norm2_g, router_g_w, router_g_b, router_e_w, router_e_b,
                  moe_w_gate, moe_w_up, moe_w_down, final_g):
    o_lr = W_GLA
    o_ml = o_lr + GLA_GATE_RANK
    o_if = o_ml + W_ML
    o_mg = o_if + 2 * ML_HEADS
    w_p = jnp.concatenate([
        w_in[:, 0:o_lr], _pad_cols(w_in[:, o_lr:o_ml], LANES),
        w_in[:, o_ml:o_if], _pad_cols(w_in[:, o_if:o_mg], LANES),
        w_in[:, o_mg:]], axis=1).astype(BF16)
    wr = _pad_cols(jnp.concatenate([router_e_w, router_g_w], axis=1), LANES)
    wr_hi = wr.astype(BF16)
    wr_lo = (wr - wr_hi.astype(F32)).astype(BF16)
    br = _pad_cols(jnp.concatenate([router_e_b, router_g_b])[None, :], LANES)
    return dict(
        g1=norm1_g[None, :], w_p=w_p,
        wg2_p=jnp.pad(gla_w_gate2, ((0, LANES - GLA_GATE_RANK), (0, 0))),
        bg=gla_b_gate[None, :], gn=gla_norm_g[None, :],
        wug=w_up_gla.astype(BF16),
        cw=ml_conv_w, cb=ml_conv_b[None, :],
        bif=_pad_cols(jnp.concatenate([ml_b_i, ml_b_f])[None, :], LANES),
        wum=w_up_ml.astype(BF16), wo=w_out.astype(BF16),
        g2=norm2_g[None, :], wr_hi=wr_hi, wr_lo=wr_lo, br=br,
        wg=moe_w_gate.reshape(N_EXPERTS, D_MODEL, D_EXPERT),
        wu=moe_w_up.reshape(N_EXPERTS, D_MODEL, D_EXPERT),
        wd=moe_w_down.reshape(N_EXPERTS, D_EXPERT, D_MODEL),
        gf=final_g[None, :],
    )


def _layer(x, gla_s0, ml_c0, ml_n0, ml_m0, conv0, p, *, chunk, seq_tile, row_tile, moe_tile, sparse):
    b, l, _ = x.shape
    n = b * l
    x2 = x.reshape(n, D_MODEL)
    pg, plr, pml, pif, pmg = _proj_call(x2, p["g1"], p["w_p"], row_tile)
    r3 = lambda a: a.reshape(b, l, a.shape[-1])
    ga, gla_s = _gla_call(r3(pg), r3(plr), gla_s0, p["wg2_p"], p["bg"], p["gn"], chunk, seq_tile)
    if ml_c0 is None:
        caug0 = m0 = None
    else:
        caug0 = jnp.concatenate(
            [ml_c0, ml_n0[..., None], jnp.zeros(ml_c0.shape[:-1] + (ML_DH - 1,), F32)], axis=-1)
        m0 = jnp.broadcast_to(ml_m0[..., None, None], ml_m0.shape + (8, LANES))
    hb, caug, m_b, new_conv = _mlstm_call(r3(pml), r3(pif), caug0, m0, conv0,
                                          p["cw"], p["cb"], p["bif"], chunk, seq_tile)
    x1, hm, route, *counts = _merge_call(x2, ga.reshape(n, GLA_V), hb.reshape(n, ML_W), pmg,
                                         p["wug"], p["wum"], p["wo"], p["g2"],
                                         p["wr_hi"], p["wr_lo"], p["br"], row_tile, sparse)
    if sparse:
        y = _sparse_moe(x1, hm, route, counts[0], p, moe_tile)
    else:
        y = _moe_call(x1, hm, route, p["wg"], p["wu"], p["wd"], p["gf"], moe_tile)
    return (y.reshape(b, l, D_MODEL), gla_s[None], caug[..., 0:ML_DH][None],
            caug[..., ML_DH][None], m_b[..., 0, 0][None], new_conv[None])


def kernel(x_prompt, x_sample, state_gla_S, state_mlstm_C, state_mlstm_n, state_mlstm_m, state_mlstm_conv, norm1_g, w_in, gla_w_gate2, gla_b_gate, gla_norm_g, w_up_gla, ml_conv_w, ml_conv_b, ml_b_i, ml_b_f, w_up_ml, w_out, norm2_g, router_g_w, router_g_b, router_e_w, router_e_b, moe_w_gate, moe_w_up, moe_w_down, final_g):
    assert norm1_g.shape[0] == 1, "single-layer trunk"
    p = _prep_weights(norm1_g[0], w_in[0], gla_w_gate2[0], gla_b_gate[0], gla_norm_g[0], w_up_gla[0],
                      ml_conv_w[0], ml_conv_b[0], ml_b_i[0], ml_b_f[0], w_up_ml[0], w_out[0],
                      norm2_g[0], router_g_w[0], router_g_b[0], router_e_w[0], router_e_b[0],
                      moe_w_gate[0], moe_w_up[0], moe_w_down[0], final_g)
    yp, *sp = _layer(x_prompt, None, None, None, None, None, p,
                     chunk=128, seq_tile=512, row_tile=256, moe_tile=256, sparse=True)
    dec_seq = x_sample.shape[1]
    ns = x_sample.shape[0] * dec_seq
    ys, *ss = _layer(x_sample, state_gla_S[0], state_mlstm_C[0], state_mlstm_n[0], state_mlstm_m[0],
                     state_mlstm_conv[0], p,
                     chunk=dec_seq, seq_tile=dec_seq, row_tile=ns, moe_tile=ns, sparse=False)
    return (yp, ys, *sp, *ss)
```

```python
import functools
import math

import numpy as np
import jax
import jax.numpy as jnp
from jax import lax
from jax.experimental import pallas as pl
from jax.experimental.pallas import tpu as pltpu

D_MODEL = 1024
GLA_HEADS = 4
GLA_DK = 64
GLA_DV = 128
GLA_GATE_RANK = 16
GLA_TAU = 16.0
ML_HEADS = 4
ML_DH = 128
CONV_W = 4
N_GROUPS = 4
EXPERTS_PER_GROUP = 8
N_EXPERTS = N_GROUPS * EXPERTS_PER_GROUP
D_EXPERT = 256
EPS = 1e-6

GLA_QK = GLA_HEADS * GLA_DK
GLA_V = GLA_HEADS * GLA_DV
ML_W = ML_HEADS * ML_DH

LANES = 128
VMEM_LIMIT = 56 * 1024 * 1024

W_GLA = 2 * GLA_QK + 2 * GLA_V
W_ML = 2 * ML_W + ML_W + ML_W
W_MG = 2 * D_MODEL
PROJ_WIDTHS = (W_GLA, LANES, W_ML, LANES, W_MG)

F32 = jnp.float32
BF16 = jnp.bfloat16


def _dot(a, b):
    return jnp.dot(a, b, preferred_element_type=F32)


def _dot_nt(a, b):
    return lax.dot_general(a, b, (((1,), (1,)), ((), ())), preferred_element_type=F32)


def _dot_tn(a, b):
    return lax.dot_general(a, b, (((0,), (0,)), ((), ())), preferred_element_type=F32)


def _split3(x):
    hi = x.astype(BF16)
    r1 = x - hi.astype(F32)
    mid = r1.astype(BF16)
    lo = (r1 - mid.astype(F32)).astype(BF16)
    return hi, mid, lo


def _dot_exact_lhs(m, x):
    hi, mid, lo = _split3(x)
    return _dot(m, hi) + _dot(m, mid) + _dot(m, lo)


def _log_sigmoid(z):
    return jnp.minimum(z, 0.0) - jnp.log1p(jnp.exp(-jnp.abs(z)))


def _sigmoid(z):
    return 1.0 / (1.0 + jnp.exp(-z))


def _rms(x, g):
    return x * lax.rsqrt(jnp.mean(x * x, axis=-1, keepdims=True) + EPS) * g


def _full_spec(shape):
    nd = len(shape)
    return pl.BlockSpec(shape, lambda *_: (0,) * nd)


def _proj_kernel(x_ref, g_ref, w_ref, *o_refs):
    h = _rms(x_ref[...], g_ref[...]).astype(BF16)
    off = 0
    for o_ref, width in zip(o_refs, PROJ_WIDTHS):
        o_ref[...] = _dot(h, w_ref[:, off:off + width])
        off += width


def _proj_call(x2, g, w_p, tm):
    n = x2.shape[0]
    return pl.pallas_call(
        _proj_kernel,
        grid=(n // tm,),
        in_specs=[pl.BlockSpec((tm, D_MODEL), lambda i: (i, 0)),
                  _full_spec(g.shape), _full_spec(w_p.shape)],
        out_specs=[pl.BlockSpec((tm, w), lambda i: (i, 0)) for w in PROJ_WIDTHS],
        out_shape=[jax.ShapeDtypeStruct((n, w), F32) for w in PROJ_WIDTHS],
        compiler_params=pltpu.CompilerParams(
            dimension_semantics=("arbitrary",), vmem_limit_bytes=VMEM_LIMIT),
        name="in_proj",
    )(x2, g, w_p)


def _gla_consts(c):
    nlev = int(math.log2(c))
    assert 1 << nlev == c
    t = np.arange(c)[:, None]
    j = np.arange(c)[None, :]
    lv = np.full((c, c), -1, np.int32)
    for l in range(nlev):
        h = c >> (l + 1)
        upper = (t % (2 * h)) >= h
        same = (j // (2 * h)) == (t // (2 * h))
        s_lower = (j % (2 * h)) < h
        lv[np.broadcast_to(upper, (c, c)) & same & s_lower] = l
    lv[np.eye(c, dtype=bool)] = nlev
    tri = (j <= t).astype(np.float32)
    return jnp.asarray(tri, BF16), jnp.asarray(np.concatenate([lv, lv], axis=1))


def _gla_kernel(*refs, c, t, has_state):
    if has_state:
        (pg_ref, plr_ref, s0_ref, wg2_ref, bg_ref, gn_ref, tri_ref, lv_ref,
         o_ref, sout_ref, s_scr) = refs
    else:
        (pg_ref, plr_ref, wg2_ref, bg_ref, gn_ref, tri_ref, lv_ref,
         o_ref, sout_ref, s_scr) = refs
    nlev = int(math.log2(c))
    step = pl.program_id(1)

    @pl.when(step == 0)
    def _():
        if has_state:
            s_scr[...] = s0_ref[...]
        else:
            s_scr[...] = jnp.zeros_like(s_scr)

    lane_k = lax.broadcasted_iota(jnp.int32, (c, GLA_QK), 1)
    first_of_pair = (lane_k % (2 * GLA_DK)) < GLA_DK
    row_k = lax.broadcasted_iota(jnp.int32, (GLA_QK, GLA_DV), 0)
    row_t = lax.broadcasted_iota(jnp.int32, (c, GLA_QK), 0)

    def block_ref(b, blk, idx):
        if blk >= 8:
            b3 = b.reshape(c // blk, blk, GLA_QK)
            return jnp.broadcast_to(b3[:, idx:idx + 1, :], b3.shape).reshape(c, GLA_QK)
        r = row_t % blk
        out = b
        for sh in range(-idx, blk - idx):
            if sh != 0:
                out = jnp.where(r - idx == sh, pltpu.roll(b, sh % c, axis=0), out)
        return out

    def chunk(ci, carry):
        r0 = ci * c
        rows = pl.ds(r0, c)
        q = pg_ref[rows, 0:GLA_QK] * (GLA_DK ** -0.5)
        k = pg_ref[rows, GLA_QK:2 * GLA_QK]
        v = pg_ref[rows, 2 * GLA_QK:2 * GLA_QK + GLA_V].astype(BF16)
        glr = plr_ref[rows, :]
        g_hi = glr.astype(BF16)
        g_lo = (glr - g_hi.astype(F32)).astype(BF16)
        z = (_dot(g_hi, wg2_ref[0]) + _dot(g_lo, wg2_ref[0]) + _dot(g_hi, wg2_ref[1])) + bg_ref[...]
        la = _log_sigmoid(z) * (1.0 / GLA_TAU)
        b = _dot_exact_lhs(tri_ref[...], la)
        b_last = b[c - 1:c, :]
        qe = (q * jnp.exp(b)).astype(BF16)
        kl = (k * jnp.exp(b_last - b)).astype(BF16)
        dcol = jnp.exp(jnp.broadcast_to(b_last, (LANES, GLA_QK)).T)

        k_a = jnp.where(first_of_pair, k, 0.0)
        k_b = k - k_a
        lv2 = lv_ref[...]
        a = [jnp.zeros((c, 2 * c), F32) for _ in range(GLA_HEADS // 2)]
        for l in range(nlev + 1):
            if l < nlev:
                half = c >> (l + 1)
                d = b - block_ref(b, 2 * half, half - 1)
                e = jnp.exp(jnp.minimum(d, -d))
                qt, kta, ktb = q * e, k_a * e, k_b * e
            else:
                qt, kta, ktb = q, k_a, k_b
            qt, kta, ktb = qt.astype(BF16), kta.astype(BF16), ktb.astype(BF16)
            for pr in range(GLA_HEADS // 2):
                ls = slice(pr * 2 * GLA_DK, (pr + 1) * 2 * GLA_DK)
                rhs = jnp.concatenate([kta[:, ls], ktb[:, ls]], axis=0)
                a[pr] = jnp.where(lv2 == l, _dot_nt(qt[:, ls], rhs), a[pr])

        s_all = s_scr[...]
        s_bd = jnp.concatenate(
            [jnp.where((row_k // GLA_DK) == h, s_all, 0.0).astype(BF16) for h in range(GLA_HEADS)], axis=1)
        o_inter = _dot(qe, s_bd)
        u_all = _dot_tn(kl, v)
        for h in range(GLA_HEADS):
            vs = slice(h * GLA_DV, (h + 1) * GLA_DV)
            ks = slice(h * GLA_DK, (h + 1) * GLA_DK)
            a_h = a[h // 2][:, (h % 2) * c:(h % 2 + 1) * c]
            o = _dot(a_h.astype(BF16), v[:, vs]) + o_inter[:, vs]
            on = _rms(o, gn_ref[:, vs])
            r = pg_ref[rows, 2 * GLA_QK + GLA_V + h * GLA_DV:2 * GLA_QK + GLA_V + (h + 1) * GLA_DV]
            o_ref[rows, vs] = (on * (r * _sigmoid(r))).astype(o_ref.dtype)
            s_scr[ks, :] = dcol[ks, :] * s_all[ks, :] + u_all[ks, vs]
        return carry

    for ci in range(t // c):
        chunk(ci, 0)

    @pl.when(step == pl.num_programs(1) - 1)
    def _():
        sout_ref[...] = s_scr[...]


def _gla_call(pg, plr, s0, wg2_p, bg, gn, c, t):
    b, l, _ = pg.shape
    tri, lv2 = _gla_consts(c)
    has_state = s0 is not None
    tile = lambda w: pl.BlockSpec((None, t, w), lambda bi, i: (bi, i, 0))
    state_spec = pl.BlockSpec((None, GLA_QK, GLA_DV), lambda bi, i: (bi, 0, 0))
    in_specs = [tile(W_GLA), tile(LANES)]
    args = [pg, plr]
    if has_state:
        in_specs.append(state_spec)
        args.append(s0)
    consts = [wg2_p, bg, gn, tri, lv2]
    in_specs += [_full_spec(x.shape) for x in consts]
    return pl.pallas_call(
        functools.partial(_gla_kernel, c=c, t=t, has_state=has_state),
        grid=(b, l // t),
        in_specs=in_specs,
        out_specs=[tile(GLA_V), state_spec],
        out_shape=[jax.ShapeDtypeStruct((b, l, GLA_V), BF16),
                   jax.ShapeDtypeStruct((b, GLA_QK, GLA_DV), F32)],
        scratch_shapes=[pltpu.VMEM((GLA_QK, GLA_DV), F32)],
        compiler_params=pltpu.CompilerParams(
            dimension_semantics=("arbitrary", "arbitrary"), vmem_limit_bytes=VMEM_LIMIT),
        name="gla",
    )(*args, *consts)


def _mlstm_kernel(*refs, c, t, has_state):
    if has_state:
        (pml_ref, pif_ref, c0_ref, m0_ref, cv0_ref, cw_ref, cb_ref, bif_ref, tri_ref, sel_ref,
         o_ref, cout_ref, mout_ref, cvout_ref, c_scr, m_scr, cv_scr, qk_scr) = refs
    else:
        (pml_ref, pif_ref, cw_ref, cb_ref, bif_ref, tri_ref, sel_ref,
         o_ref, cout_ref, mout_ref, cvout_ref, c_scr, m_scr, cv_scr, qk_scr) = refs
    step = pl.program_id(1)
    hist = CONV_W - 1
    pad = 8

    @pl.when(step == 0)
    def _():
        if has_state:
            c_scr[...] = c0_ref[...]
            m_scr[...] = m0_ref[...]
            cv_scr[0:pad - hist, :] = jnp.zeros((pad - hist, 2 * ML_W), F32)
            cv_scr[pad - hist:pad, :] = cv0_ref[...]
        else:
            c_scr[...] = jnp.zeros_like(c_scr)
            m_scr[...] = jnp.zeros_like(m_scr)
            cv_scr[0:pad, :] = jnp.zeros((pad, 2 * ML_W), F32)

    cv_scr[pad:pad + t, :] = pml_ref[:, 0:2 * ML_W]
    acc = cb_ref[...] + cv_scr[pad:pad + t, :] * cw_ref[CONV_W - 1:CONV_W, :]
    for d in range(1, CONV_W):
        acc = acc + cv_scr[pad - d:pad - d + t, :] * cw_ref[CONV_W - 1 - d:CONV_W - d, :]
    conv = acc * _sigmoid(acc)
    qk_scr[:, 0:ML_W] = conv[:, 0:ML_W].astype(BF16)
    qk_scr[:, ML_W:] = (conv[:, ML_W:] * (ML_DH ** -0.5)).astype(BF16)
    cv_scr[0:pad, :] = cv_scr[t:t + pad, :]

    gts = pif_ref[...] + bif_ref[...]
    flog = pltpu.roll(_log_sigmoid(gts), LANES - ML_HEADS, axis=1)

    lane = lax.broadcasted_iota(jnp.int32, (c, LANES), 1)
    row_c = lax.broadcasted_iota(jnp.int32, (c, LANES), 0)
    causal = (lax.broadcasted_iota(jnp.int32, (c, c), 1) <= lax.broadcasted_iota(jnp.int32, (c, c), 0))
    ones_v = jnp.ones((c, ML_DH), BF16)

    def slots(pieces):
        out = jnp.zeros((c, LANES), F32)
        for j, piece in enumerate(pieces):
            if not isinstance(piece, float) and j > 0:
                piece = pltpu.roll(piece, ML_HEADS * j, axis=1)
            out = jnp.where((lane >= ML_HEADS * j) & (lane < ML_HEADS * (j + 1)), piece, out)
        return out

    def split3f(x):
        return [p.astype(F32) for p in _split3(x)]

    def chunk(ci, carry):
        r0 = ci * c
        rows = pl.ds(r0, c)
        ip = gts[r0:r0 + c]
        bc = _dot_exact_lhs(tri_ref[...], flog[r0:r0 + c])
        w = ip - bc
        cm = w
        for j in range(int(math.log2(c))):
            sh = 1 << j
            cm = jnp.where(row_c >= sh, jnp.maximum(cm, pltpu.roll(cm, sh, axis=0)), cm)
        mprev = m_scr[0:1, :]
        g = jnp.maximum(mprev, cm)
        g_last = g[c - 1:c, :]
        m_scr[...] = jnp.broadcast_to(bc[c - 1:c, :] + g_last, m_scr.shape)
        lhs_all = slots([1.0, 1.0, 1.0] + split3f(g))
        rhs = slots(split3f(w) + [-1.0, -1.0, -1.0]).astype(BF16)
        y = slots(split3f(mprev - g) + split3f(-(bc + g)) + split3f(w - g_last)).astype(BF16)
        for h in range(ML_HEADS):
            hs = slice(h * ML_DH, (h + 1) * ML_DH)
            qh = qk_scr[rows, h * ML_DH:(h + 1) * ML_DH]
            kh = qk_scr[rows, ML_W + h * ML_DH:ML_W + (h + 1) * ML_DH]
            vaug = jnp.concatenate(
                [pml_ref[rows, 2 * ML_W + h * ML_DH:2 * ML_W + (h + 1) * ML_DH].astype(BF16), ones_v],
                axis=1)
            lhs = jnp.where(lane % ML_HEADS == h, lhs_all, 0.0).astype(BF16)
            p = jnp.exp(jnp.where(causal, _dot_nt(lhs, rhs), -jnp.inf))
            bx = jnp.exp(_dot(y, sel_ref[h]))
            w_inter = bx[:, 0:ML_DH]
            e_mt = bx[:, ML_DH:2 * ML_DH]
            w_state = bx[:, 2 * ML_DH:3 * ML_DH]
            s = (_dot_nt(qh, kh) * p).astype(BF16)
            caug = c_scr[h]
            nd = _dot(s, vaug) + jnp.concatenate([w_inter, w_inter], axis=1) * _dot(qh, caug.astype(BF16))
            hh = nd[:, 0:ML_DH] / jnp.maximum(jnp.abs(nd[:, ML_DH:]), e_mt)
            mo = pml_ref[rows, 3 * ML_W + h * ML_DH:3 * ML_W + (h + 1) * ML_DH]
            o_ref[rows, hs] = (_sigmoid(mo) * hh).astype(o_ref.dtype)
            ks = (kh.astype(F32) * w_state).astype(BF16)
            dec = w_inter[c - 1:c, :]
            c_scr[h] = jnp.concatenate([dec, dec], axis=1) * caug + _dot_tn(ks, vaug)
        return carry

    for ci in range(t // c):
        chunk(ci, 0)

    @pl.when(step == pl.num_programs(1) - 1)
    def _():
        cout_ref[...] = c_scr[...]
        mout_ref[...] = m_scr[...]
        cvout_ref[...] = pml_ref[t - hist:t, 0:2 * ML_W]


def _mlstm_call(pml, pif, caug0, m0, cv0, cw, cb, bif, c, t):
    b, l, _ = pml.shape
    has_state = caug0 is not None
    tri = jnp.asarray(np.tril(np.ones((c, c), np.float32)), BF16)
    sel = np.zeros((ML_HEADS, LANES, 3 * ML_DH), np.float32)
    for h in range(ML_HEADS):
        for slot in range(9):
            sel[h, ML_HEADS * slot + h, (slot // 3) * ML_DH:(slot // 3 + 1) * ML_DH] = 1.0
    sel = jnp.asarray(sel, BF16)
    tile = lambda w: pl.BlockSpec((None, t, w), lambda bi, i: (bi, i, 0))
    c_spec = pl.BlockSpec((None, ML_HEADS, ML_DH, 2 * ML_DH), lambda bi, i: (bi, 0, 0, 0))
    m_spec = pl.BlockSpec((None, 8, LANES), lambda bi, i: (bi, 0, 0))
    cv_spec = pl.BlockSpec((None, CONV_W - 1, 2 * ML_W), lambda bi, i: (bi, 0, 0))
    in_specs = [tile(W_ML), tile(LANES)]
    args = [pml, pif]
    if has_state:
        in_specs += [c_spec, m_spec, cv_spec]
        args += [caug0, m0, cv0]
    consts = [cw, cb, bif, tri, sel]
    in_specs += [_full_spec(x.shape) for x in consts]
    return pl.pallas_call(
        functools.partial(_mlstm_kernel, c=c, t=t, has_state=has_state),
        grid=(b, l // t),
        in_specs=in_specs,
        out_specs=[tile(ML_W), c_spec, m_spec, cv_spec],
        out_shape=[jax.ShapeDtypeStruct((b, l, ML_W), BF16),
                   jax.ShapeDtypeStruct((b, ML_HEADS, ML_DH, 2 * ML_DH), F32),
                   jax.ShapeDtypeStruct((b, 8, LANES), F32),
                   jax.ShapeDtypeStruct((b, CONV_W - 1, 2 * ML_W), F32)],
        scratch_shapes=[pltpu.VMEM((ML_HEADS, ML_DH, 2 * ML_DH), F32),
                        pltpu.VMEM((8, LANES), F32),
                        pltpu.VMEM((8 + t, 2 * ML_W), F32),
                        pltpu.VMEM((t, 2 * ML_W), BF16)],
        compiler_params=pltpu.CompilerParams(
            dimension_semantics=("arbitrary", "arbitrary"), vmem_limit_bytes=VMEM_LIMIT),
        name="mlstm",
    )(*args, *consts)


def _merge_kernel(*refs, sparse):
    if sparse:
        (x_ref, ga_ref, hb_ref, pmg_ref, wug_ref, wum_ref, wo_ref, g2_ref,
         wr_hi_ref, wr_lo_ref, br_ref, tril_ref, x1_ref, hm_ref, meta_ref, cnt_ref, cnt_scr) = refs
    else:
        (x_ref, ga_ref, hb_ref, pmg_ref, wug_ref, wum_ref, wo_ref, g2_ref,
         wr_hi_ref, wr_lo_ref, br_ref, x1_ref, hm_ref, comb_ref) = refs
    ya = _dot(ga_ref[...], wug_ref[...])
    yb = _dot(hb_ref[...], wum_ref[...])
    z = _sigmoid(pmg_ref[:, 0:D_MODEL]) * ya + _sigmoid(pmg_ref[:, D_MODEL:]) * yb
    x1 = x_ref[...] + _dot(z.astype(BF16), wo_ref[...])
    x1_ref[...] = x1
    hm = _rms(x1, g2_ref[...])
    hm_hi = hm.astype(BF16)
    hm_ref[...] = hm.astype(hm_ref.dtype)
    hm_lo =(hm - hm_hi.astype(F32)).astype(BF16)
    logits = (_dot(hm_hi, wr_hi_ref[...]) + _dot(hm_lo, wr_hi_ref[...])
              + _dot(hm_hi, wr_lo_ref[...]) + br_ref[...])
    lane = lax.broadcasted_iota(jnp.int32, logits.shape, 1)
    neg = -jnp.inf
    is_g = (lane >= N_EXPERTS) & (lane < N_EXPERTS + N_GROUPS)
    lg = jnp.where(is_g, logits, neg)
    mg = jnp.max(lg, axis=-1, keepdims=True)
    p_top = 1.0 / jnp.sum(jnp.exp(lg - mg), axis=-1, keepdims=True)
    gi = jnp.min(jnp.where(lg == mg, lane, 2 * LANES), axis=-1, keepdims=True) - N_EXPERTS
    group_shift = int(math.log2(EXPERTS_PER_GROUP))
    sel = (lane < N_EXPERTS) & (jnp.right_shift(lane, group_shift) == gi)
    le = jnp.where(sel, logits, neg)
    v1 = jnp.max(le, axis=-1, keepdims=True)
    i1 = jnp.min(jnp.where(le == v1, lane, 2 * LANES), axis=-1, keepdims=True)
    le2 = jnp.where(lane == i1, neg, le)
    v2 = jnp.max(le2, axis=-1, keepdims=True)
    i2 = jnp.min(jnp.where(le2 == v2, lane, 2 * LANES), axis=-1, keepdims=True)
    e2 = jnp.exp(v2 - v1)
    wa = 1.0 / (1.0 + e2)
    wb = e2 / (1.0 + e2)
    if not sparse:
        comb_ref[...] = p_top * (jnp.where(lane == i1, wa, 0.0) + jnp.where(lane == i2, wb, 0.0))
        return

    @pl.when(pl.program_id(0) == 0)
    def _():
        cnt_scr[...] = jnp.zeros_like(cnt_scr)

    oh1 = lane == i1
    oh2 = lane == i2
    both = jnp.where(oh1 | oh2, 1.0, 0.0)
    before = _dot(tril_ref[...], both.astype(BF16)) + cnt_scr[...]
    r1 = jnp.sum(jnp.where(oh1, before, 0.0), axis=-1, keepdims=True)
    r2 = jnp.sum(jnp.where(oh2, before, 0.0), axis=-1, keepdims=True)
    cols = (i1.astype(F32), i2.astype(F32), r1, r2, p_top * wa, p_top * wb)
    meta = jnp.zeros(logits.shape, F32)
    for ci, col in enumerate(cols):
        meta = jnp.where(lane == ci, col, meta)
    meta_ref[...] = meta
    cnt_scr[...] += jnp.sum(both, axis=0, keepdims=True)
    cnt_ref[...] = cnt_scr[...]


def _merge_call(x2, ga, hb, pmg, wug, wum, wo, g2, wr_hi, wr_lo, br, tm, sparse):
    n = x2.shape[0]
    tile = lambda w: pl.BlockSpec((tm, w), lambda i: (i, 0))
    consts = [wug, wum, wo, g2, wr_hi, wr_lo, br]
    out_specs = [tile(D_MODEL), tile(D_MODEL), tile(LANES)]
    out_shape = [jax.ShapeDtypeStruct((n, D_MODEL), F32),
                 jax.ShapeDtypeStruct((n, D_MODEL), F32 if sparse else BF16),
                 jax.ShapeDtypeStruct((n, LANES), F32)]
    scratch = []
    if sparse:
        consts.append(jnp.asarray(np.tril(np.ones((tm, tm), np.float32), -1), BF16))
        out_specs.append(_full_spec((1, LANES)))
        out_shape.append(jax.ShapeDtypeStruct((1, LANES), F32))
        scratch.append(pltpu.VMEM((1, LANES), F32))
    return pl.pallas_call(
        functools.partial(_merge_kernel, sparse=sparse),
        grid=(n // tm,),
        in_specs=[tile(D_MODEL), tile(GLA_V), tile(ML_W), tile(W_MG)]
                 + [_full_spec(x.shape) for x in consts],
        out_specs=out_specs,
        out_shape=out_shape,
        scratch_shapes=scratch,
        compiler_params=pltpu.CompilerParams(
            dimension_semantics=("arbitrary",), vmem_limit_bytes=VMEM_LIMIT),
        name="merge",
    )(x2, ga, hb, pmg, *consts)


MOE_ROWS = 256


def _row_copy(src, i, dst, j, sem):
    return pltpu.make_async_copy(src.at[pl.ds(i, 1), :], dst.at[pl.ds(j, 1), :], sem)


def _dispatch_kernel(pos_ref, tail_ref, hm_ref, xs_ref, zero_ref, sem, zsem, *, tm):
    step = pl.program_id(0)

    @pl.when(step == 0)
    def _():
        zero_ref[...] = jnp.zeros_like(zero_ref)
        n_tiles = xs_ref.shape[0] // MOE_ROWS

        def clear(row):
            start = pl.multiple_of(row, MOE_ROWS)
            return pltpu.make_async_copy(zero_ref, xs_ref.at[pl.ds(start, MOE_ROWS), :], zsem)

        def unused(fn):
            def body(j, carry):
                fn(clear(j * MOE_ROWS))
                return carry
            lax.fori_loop(tail_ref[2 * N_EXPERTS], n_tiles, body, 0)

        for e in range(N_EXPERTS):
            @pl.when(tail_ref[N_EXPERTS + e] > 0)
            def _():
                clear(tail_ref[e]).start()
        unused(lambda c: c.start())
        for e in range(N_EXPERTS):
            @pl.when(tail_ref[N_EXPERTS + e] > 0)
            def _():
                clear(tail_ref[e]).wait()
        unused(lambda c: c.wait())

    base = step * tm

    def issue(t, carry):
        for slot in range(2):
            _row_copy(hm_ref, t, xs_ref, pos_ref[2 * (base + t) + slot], sem).start()
        return carry

    def drain(t, carry):
        for slot in range(2):
            _row_copy(hm_ref, t, xs_ref, pos_ref[2 * (base + t) + slot], sem).wait()
        return carry

    lax.fori_loop(0, tm, issue, 0, unroll=8)
    lax.fori_loop(0, tm, drain, 0, unroll=8)


def _dispatch_call(pos, tail, hm, n_rows, tm):
    n = hm.shape[0]
    return pl.pallas_call(
        functools.partial(_dispatch_kernel, tm=tm),
        grid_spec=pltpu.PrefetchScalarGridSpec(
            num_scalar_prefetch=2,
            grid=(n // tm,),
            in_specs=[pl.BlockSpec((tm, D_MODEL), lambda i, *_: (i, 0))],
            out_specs=pl.BlockSpec(memory_space=pl.ANY),
            scratch_shapes=[pltpu.VMEM((MOE_ROWS, D_MODEL), F32),
                            pltpu.SemaphoreType.DMA, pltpu.SemaphoreType.DMA],
        ),
        out_shape=jax.ShapeDtypeStruct((n_rows, D_MODEL), F32),
        compiler_params=pltpu.CompilerParams(
            dimension_semantics=("arbitrary",), vmem_limit_bytes=VMEM_LIMIT),
        name="moe_dispatch",
    )(pos, tail, hm)


def _gmm_kernel(te_ref, nv_ref, xs_ref, wg_ref, wu_ref, wd_ref, o_ref):
    used = pl.program_id(0) < nv_ref[0]

    @pl.when(used)
    def _():
        x = xs_ref[...].astype(BF16)
        a = _dot(x, wg_ref[...].astype(BF16))
        u = _dot(x, wu_ref[...].astype(BF16))
        hh = (a * _sigmoid(a)) * u
        o_ref[...] = _dot(hh.astype(BF16), wd_ref[...].astype(BF16))

    @pl.when(jnp.logical_not(used))
    def _():
        o_ref[...] = jnp.zeros_like(o_ref)


def _gmm_call(tile_expert, n_valid, xs, wg, wu, wd):
    n_tiles = xs.shape[0] // MOE_ROWS
    rows = lambda j, te, nv: (jnp.minimum(j, nv[0] - 1), 0)
    wsel = lambda j, te, nv: (te[j], 0, 0)
    return pl.pallas_call(
        _gmm_kernel,
        grid_spec=pltpu.PrefetchScalarGridSpec(
            num_scalar_prefetch=2,
            grid=(n_tiles,),
            in_specs=[pl.BlockSpec((MOE_ROWS, D_MODEL), rows),
                      pl.BlockSpec((None, D_MODEL, D_EXPERT), wsel),
                      pl.BlockSpec((None, D_MODEL, D_EXPERT), wsel),
                      pl.BlockSpec((None, D_EXPERT, D_MODEL), wsel)],
            out_specs=pl.BlockSpec((MOE_ROWS, D_MODEL), lambda j, te, nv: (j, 0)),
        ),
        out_shape=jax.ShapeDtypeStruct(xs.shape, F32),
        compiler_params=pltpu.CompilerParams(
            dimension_semantics=("arbitrary",), vmem_limit_bytes=VMEM_LIMIT),
        name="moe_grouped",
    )(tile_expert, n_valid, xs, wg, wu, wd)


def _combine_kernel(pos_ref, x1_ref, meta_ref, gf_ref, os_ref, y_ref, buf_ref, sem, *, tm):
    base = pl.program_id(0) * tm

    def issue(t, carry):
        for slot in range(2):
            _row_copy(os_ref, pos_ref[2 * (base + t) + slot], buf_ref.at[slot], t, sem).start()
        return carry

    def drain(t, carry):
        for slot in range(2):
            _row_copy(os_ref, pos_ref[2 * (base + t) + slot], buf_ref.at[slot], t, sem).wait()
        return carry

    lax.fori_loop(0, tm, issue, 0, unroll=8)
    lax.fori_loop(0, tm, drain, 0, unroll=8)
    y = x1_ref[...] + meta_ref[:, 4:5] * buf_ref[0] + meta_ref[:, 5:6] * buf_ref[1]
    y_ref[...] = _rms(y, gf_ref[...])


def _combine_call(pos, x1, meta, gf, out_sorted, tm):
    n = x1.shape[0]
    tile = lambda w: pl.BlockSpec((tm, w), lambda i, *_: (i, 0))
    return pl.pallas_call(
        functools.partial(_combine_kernel, tm=tm),
        grid_spec=pltpu.PrefetchScalarGridSpec(
            num_scalar_prefetch=1,
            grid=(n // tm,),
            in_specs=[tile(D_MODEL), tile(LANES), pl.BlockSpec(gf.shape, lambda i, *_: (0, 0)),
                      pl.BlockSpec(memory_space=pl.ANY)],
            out_specs=tile(D_MODEL),
            scratch_shapes=[pltpu.VMEM((2, tm, D_MODEL), F32), pltpu.SemaphoreType.DMA],
        ),
        out_shape=jax.ShapeDtypeStruct((n, D_MODEL), F32),
        compiler_params=pltpu.CompilerParams(
            dimension_semantics=("arbitrary",), vmem_limit_bytes=VMEM_LIMIT),
        name="moe_combine",
    )(pos, x1, meta, gf, out_sorted)


def _sparse_moe(x1, hm, meta, counts, p, tm):
    n = x1.shape[0]
    n_tiles = (2 * n) // MOE_ROWS + N_EXPERTS
    cnt = counts[0, :N_EXPERTS].astype(jnp.int32)
    tiles = (cnt + (MOE_ROWS - 1)) // MOE_ROWS
    tile_end = jnp.cumsum(tiles)
    tile_start = tile_end - tiles
    n_valid = tile_end[-1:]
    expert = meta[:, 0:2].astype(jnp.int32)
    rank = meta[:, 2:4].astype(jnp.int32)
    pos = (jnp.take(tile_start, expert) * MOE_ROWS + rank).reshape(-1)
    j = jnp.minimum(jnp.arange(n_tiles, dtype=jnp.int32), n_valid - 1)
    tile_expert = jnp.sum((j[:, None] >= tile_end[None, :]).astype(jnp.int32), axis=1)
    tail = jnp.concatenate([(tile_end - 1) * MOE_ROWS, cnt, n_valid]).astype(jnp.int32)
    xs = _dispatch_call(pos, tail, hm, n_tiles * MOE_ROWS, tm)
    out_sorted = _gmm_call(tile_expert, n_valid.astype(jnp.int32), xs, p["wg"], p["wu"], p["wd"])
    return _combine_call(pos, x1, meta, p["gf"], out_sorted, tm)


def _moe_kernel(x1_ref, hm_ref, comb_ref, wg_ref, wu_ref, wd_ref, gf_ref, y_ref, acc_ref):
    e = pl.program_id(1)

    @pl.when(e == 0)
    def _():
        acc_ref[...] = jnp.zeros_like(acc_ref)

    hm = hm_ref[...]
    a = _dot(hm, wg_ref[...].astype(BF16))
    u = _dot(hm, wu_ref[...].astype(BF16))
    lane = lax.broadcasted_iota(jnp.int32, comb_ref.shape, 1)
    ce = jnp.sum(jnp.where(lane == e, comb_ref[...], 0.0), axis=-1, keepdims=True)
    hh = (a * _sigmoid(a)) * u * ce
    acc_ref[...] += _dot(hh.astype(BF16), wd_ref[...].astype(BF16))

    @pl.when(e == pl.num_programs(1) - 1)
    def _():
        y_ref[...] = _rms(x1_ref[...] + acc_ref[...], gf_ref[...])


def _moe_call(x1, hm, comb, wg, wu, wd, gf, tm):
    n = x1.shape[0]
    tile = lambda w: pl.BlockSpec((tm, w), lambda i, e: (i, 0))
    return pl.pallas_call(
        _moe_kernel,
        grid=(n // tm, N_EXPERTS),
        in_specs=[tile(D_MODEL), tile(D_MODEL), tile(LANES),
                  pl.BlockSpec((None, D_MODEL, D_EXPERT), lambda i, e: (e, 0, 0)),
                  pl.BlockSpec((None, D_MODEL, D_EXPERT), lambda i, e: (e, 0, 0)),
                  pl.BlockSpec((None, D_EXPERT, D_MODEL), lambda i, e: (e, 0, 0)),
                  _full_spec(gf.shape)],
        out_specs=tile(D_MODEL),
        out_shape=jax.ShapeDtypeStruct((n, D_MODEL), F32),
        scratch_shapes=[pltpu.VMEM((tm, D_MODEL), F32)],
        compiler_params=pltpu.CompilerParams(
            dimension_semantics=("arbitrary", "arbitrary"), vmem_limit_bytes=VMEM_LIMIT),
        name="moe",
    )(x1, hm, comb, wg, wu, wd, gf)


def _pad_cols(w, width):
    return jnp.pad(w, ((0, 0), (0, width - w.shape[1])))


def _prep_weights(norm1_g, w_in, gla_w_gate2, gla_b_gate, gla_norm_g, w_up_gla,
                  ml_conv_w, ml_conv_b, ml_b_i, ml_b_f, w_up_ml, w_out,
                  norm2_g, router_g_w, router_g_b, router_e_w, router_e_b,
                  moe_w_gate, moe_w_up, moe_w_down, final_g):
    o_lr = W_GLA
    o_ml = o_lr + GLA_GATE_RANK
    o_if = o_ml + W_ML
    o_mg = o_if + 2 * ML_HEADS
    w_p = jnp.concatenate([
        w_in[:, 0:o_lr], _pad_cols(w_in[:, o_lr:o_ml], LANES),
        w_in[:, o_ml:o_if], _pad_cols(w_in[:, o_if:o_mg], LANES),
        w_in[:, o_mg:]], axis=1).astype(BF16)
    wr = _pad_cols(jnp.concatenate([router_e_w, router_g_w], axis=1), LANES)
    wr_hi = wr.astype(BF16)
    wr_lo = (wr - wr_hi.astype(F32)).astype(BF16)
    br = _pad_cols(jnp.concatenate([router_e_b, router_g_b])[None, :], LANES)
    wg2 = jnp.pad(gla_w_gate2, ((0, LANES - GLA_GATE_RANK), (0, 0)))
    wg2_hi = wg2.astype(BF16)
    return dict(
        g1=norm1_g[None, :], w_p=w_p,
        wg2_p=jnp.stack([wg2_hi, (wg2 - wg2_hi.astype(F32)).astype(BF16)]),
        bg=gla_b_gate[None, :], gn=gla_norm_g[None, :],
        wug=w_up_gla.astype(BF16),
        cw=ml_conv_w, cb=ml_conv_b[None, :],
        bif=_pad_cols(jnp.concatenate([ml_b_i, ml_b_f])[None, :], LANES),
        wum=w_up_ml.astype(BF16), wo=w_out.astype(BF16),
        g2=norm2_g[None, :], wr_hi=wr_hi, wr_lo=wr_lo, br=br,
        wg=moe_w_gate.reshape(N_EXPERTS, D_MODEL, D_EXPERT),
        wu=moe_w_up.reshape(N_EXPERTS, D_MODEL, D_EXPERT),
        wd=moe_w_down.reshape(N_EXPERTS, D_EXPERT, D_MODEL),
        gf=final_g[None, :],
    )


def _layer(x, gla_s0, ml_c0, ml_n0, ml_m0, conv0, p, *, chunk, seq_tile, row_tile, moe_tile, sparse):
    b, l, _ = x.shape
    n = b * l
    x2 = x.reshape(n, D_MODEL)
    pg, plr, pml, pif, pmg = _proj_call(x2, p["g1"], p["w_p"], row_tile)
    r3 = lambda a: a.reshape(b, l, a.shape[-1])
    s0 = None if gla_s0 is None else gla_s0.reshape(b, GLA_QK, GLA_DV)
    ga, gla_s = _gla_call(r3(pg), r3(plr), s0, p["wg2_p"], p["bg"], p["gn"], chunk, seq_tile)
    gla_s = gla_s.reshape(b, GLA_HEADS, GLA_DK, GLA_DV)
    if ml_c0 is None:
        caug0 = m0 = None
    else:
        caug0 = jnp.concatenate([ml_c0, jnp.broadcast_to(ml_n0[..., None], ml_c0.shape)], axis=-1)
        m0 = jnp.broadcast_to(_pad_cols(ml_m0, LANES)[:, None, :], (b, 8, LANES))
    hb, caug, m_b, new_conv = _mlstm_call(r3(pml), r3(pif), caug0, m0, conv0,
                                          p["cw"], p["cb"], p["bif"], chunk, seq_tile)
    x1, hm, route, *counts = _merge_call(x2, ga.reshape(n, GLA_V), hb.reshape(n, ML_W), pmg,
                                         p["wug"], p["wum"], p["wo"], p["g2"],
                                         p["wr_hi"], p["wr_lo"], p["br"], row_tile, sparse)
    if sparse:
        y = _sparse_moe(x1, hm, route, counts[0], p, moe_tile)
    else:
        y = _moe_call(x1, hm, route, p["wg"], p["wu"], p["wd"], p["gf"], moe_tile)
    return (y.reshape(b, l, D_MODEL), gla_s[None], caug[..., 0:ML_DH][None],
            caug[..., ML_DH][None], m_b[:, 0, 0:ML_HEADS][None], new_conv[None])


def kernel(x_prompt, x_sample, state_gla_S, state_mlstm_C, state_mlstm_n, state_mlstm_m, state_mlstm_conv, norm1_g, w_in, gla_w_gate2, gla_b_gate, gla_norm_g, w_up_gla, ml_conv_w, ml_conv_b, ml_b_i, ml_b_f, w_up_ml, w_out, norm2_g, router_g_w, router_g_b, router_e_w, router_e_b, moe_w_gate, moe_w_up, moe_w_down, final_g):
    assert norm1_g.shape[0] == 1, "single-layer trunk"
    p = _prep_weights(norm1_g[0], w_in[0], gla_w_gate2[0], gla_b_gate[0], gla_norm_g[0], w_up_gla[0],
                      ml_conv_w[0], ml_conv_b[0], ml_b_i[0], ml_b_f[0], w_up_ml[0], w_out[0],
                      norm2_g[0], router_g_w[0], router_g_b[0], router_e_w[0], router_e_b[0],
                      moe_w_gate[0], moe_w_up[0], moe_w_down[0], final_g)
    yp, *sp = _layer(x_prompt, None, None, None, None, None, p,
                     chunk=128, seq_tile=512, row_tile=256, moe_tile=256, sparse=True)
    dec_seq = x_sample.shape[1]
    ns = x_sample.shape[0] * dec_seq
    ys, *ss = _layer(x_sample, state_gla_S[0], state_mlstm_C[0], state_mlstm_n[0], state_mlstm_m[0],
                     state_mlstm_conv[0], p,
                     chunk=dec_seq, seq_tile=dec_seq, row_tile=ns, moe_tile=ns, sparse=False)
    return (yp, ys, *sp, *ss)
```

```python
import functools
import math

import numpy as np
import jax
import jax.numpy as jnp
from jax import lax
from jax.experimental import pallas as pl
from jax.experimental.pallas import tpu as pltpu

D_MODEL = 1024
GLA_HEADS = 4
GLA_DK = 64
GLA_DV = 128
GLA_GATE_RANK = 16
GLA_TAU = 16.0
ML_HEADS = 4
ML_DH = 128
CONV_W = 4
N_GROUPS = 4
EXPERTS_PER_GROUP = 8
N_EXPERTS = N_GROUPS * EXPERTS_PER_GROUP
D_EXPERT = 256
EPS = 1e-6

GLA_QK = GLA_HEADS * GLA_DK
GLA_V = GLA_HEADS * GLA_DV
ML_W = ML_HEADS * ML_DH

LANES = 128
VMEM_LIMIT = 56 * 1024 * 1024

W_GLA = 2 * GLA_QK + 2 * GLA_V
W_ML = 2 * ML_W + ML_W + ML_W
W_MG = 2 * D_MODEL
PROJ_WIDTHS = (W_GLA, LANES, W_ML, LANES, W_MG)

F32 = jnp.float32
BF16 = jnp.bfloat16


def _dot(a, b):
    return jnp.dot(a, b, preferred_element_type=F32)


def _dot_nt(a, b):
    return lax.dot_general(a, b, (((1,), (1,)), ((), ())), preferred_element_type=F32)


def _dot_tn(a, b):
    return lax.dot_general(a, b, (((0,), (0,)), ((), ())), preferred_element_type=F32)


def _split3(x):
    hi = x.astype(BF16)
    r1 = x - hi.astype(F32)
    mid = r1.astype(BF16)
    lo = (r1 - mid.astype(F32)).astype(BF16)
    return hi, mid, lo


def _dot_exact_lhs(m, x):
    hi, mid, lo = _split3(x)
    return _dot(m, hi) + _dot(m, mid) + _dot(m, lo)


def _log_sigmoid(z):
    return jnp.minimum(z, 0.0) - jnp.log1p(jnp.exp(-jnp.abs(z)))


def _sigmoid(z):
    return 1.0 / (1.0 + jnp.exp(-z))


def _rms(x, g):
    return x * lax.rsqrt(jnp.mean(x * x, axis=-1, keepdims=True) + EPS) * g


def _full_spec(shape):
    nd = len(shape)
    return pl.BlockSpec(shape, lambda *_: (0,) * nd)


def _proj_kernel(x_ref, g_ref, w_ref, *o_refs):
    h = _rms(x_ref[...], g_ref[...]).astype(BF16)
    off = 0
    for o_ref, width in zip(o_refs, PROJ_WIDTHS):
        o_ref[...] = _dot(h, w_ref[:, off:off + width])
        off += width


def _proj_call(x2, g, w_p, tm):
    n = x2.shape[0]
    return pl.pallas_call(
        _proj_kernel,
        grid=(n // tm,),
        in_specs=[pl.BlockSpec((tm, D_MODEL), lambda i: (i, 0)),
                  _full_spec(g.shape), _full_spec(w_p.shape)],
        out_specs=[pl.BlockSpec((tm, w), lambda i: (i, 0)) for w in PROJ_WIDTHS],
        out_shape=[jax.ShapeDtypeStruct((n, w), F32) for w in PROJ_WIDTHS],
        compiler_params=pltpu.CompilerParams(
            dimension_semantics=("arbitrary",), vmem_limit_bytes=VMEM_LIMIT),
        name="in_proj",
    )(x2, g, w_p)


def _gla_consts(c):
    nlev = int(math.log2(c))
    assert 1 << nlev == c
    t = np.arange(c)[:, None]
    j = np.arange(c)[None, :]
    lv = np.full((c, c), -1, np.int32)
    for l in range(nlev):
        h = c >> (l + 1)
        upper = (t % (2 * h)) >= h
        same = (j // (2 * h)) == (t // (2 * h))
        s_lower = (j % (2 * h)) < h
        lv[np.broadcast_to(upper, (c, c)) & same & s_lower] = l
    lv[np.eye(c, dtype=bool)] = nlev
    tri = (j <= t).astype(np.float32)
    return jnp.asarray(tri, BF16), jnp.asarray(np.concatenate([lv, lv], axis=1))


def _gla_kernel(*refs, c, t, has_state):
    if has_state:
        (pg_ref, plr_ref, s0_ref, wg2_ref, bg_ref, gn_ref, tri_ref, lv_ref,
         o_ref, sout_ref, s_scr) = refs
    else:
        (pg_ref, plr_ref, wg2_ref, bg_ref, gn_ref, tri_ref, lv_ref,
         o_ref, sout_ref, s_scr) = refs
    nlev = int(math.log2(c))
    step = pl.program_id(1)

    @pl.when(step == 0)
    def _():
        if has_state:
            s_scr[...] = s0_ref[...]
        else:
            s_scr[...] = jnp.zeros_like(s_scr)

    lane_k = lax.broadcasted_iota(jnp.int32, (c, GLA_QK), 1)
    first_of_pair = (lane_k % (2 * GLA_DK)) < GLA_DK
    row_k = lax.broadcasted_iota(jnp.int32, (GLA_QK, GLA_DV), 0)
    row_t = lax.broadcasted_iota(jnp.int32, (c, GLA_QK), 0)

    def block_ref(b, blk, idx):
        if blk >= 8:
            b3 = b.reshape(c // blk, blk, GLA_QK)
            return jnp.broadcast_to(b3[:, idx:idx + 1, :], b3.shape).reshape(c, GLA_QK)
        r = row_t % blk
        out = b
        for sh in range(-idx, blk - idx):
            if sh != 0:
                out = jnp.where(r - idx == sh, pltpu.roll(b, sh % c, axis=0), out)
        return out

    def chunk(ci, carry):
        r0 = ci * c
        rows = pl.ds(r0, c)
        q = pg_ref[rows, 0:GLA_QK] * (GLA_DK ** -0.5)
        k = pg_ref[rows, GLA_QK:2 * GLA_QK]
        v = pg_ref[rows, 2 * GLA_QK:2 * GLA_QK + GLA_V].astype(BF16)
        glr = plr_ref[rows, :]
        g_hi = glr.astype(BF16)
        g_lo = (glr - g_hi.astype(F32)).astype(BF16)
        z = (_dot(g_hi, wg2_ref[0]) + _dot(g_lo, wg2_ref[0]) + _dot(g_hi, wg2_ref[1])) + bg_ref[...]
        la = _log_sigmoid(z) * (1.0 / GLA_TAU)
        b = _dot_exact_lhs(tri_ref[...], la)
        b_last = b[c - 1:c, :]
        qe = (q * jnp.exp(b)).astype(BF16)
        kl = (k * jnp.exp(b_last - b)).astype(BF16)
        dcol = jnp.exp(jnp.broadcast_to(b_last, (LANES, GLA_QK)).T)

        k_a = jnp.where(first_of_pair, k, 0.0)
        k_b = k - k_a
        lv2 = lv_ref[...]
        a = [jnp.zeros((c, 2 * c), F32) for _ in range(GLA_HEADS // 2)]
        for l in range(nlev + 1):
            if l < nlev:
                half = c >> (l + 1)
                d = b - block_ref(b, 2 * half, half - 1)
                e = jnp.exp(jnp.minimum(d, -d))
                qt, kta, ktb = q * e, k_a * e, k_b * e
            else:
                qt, kta, ktb = q, k_a, k_b
            qt, kta, ktb = qt.astype(BF16), kta.astype(BF16), ktb.astype(BF16)
            for pr in range(GLA_HEADS // 2):
                ls = slice(pr * 2 * GLA_DK, (pr + 1) * 2 * GLA_DK)
                rhs = jnp.concatenate([kta[:, ls], ktb[:, ls]], axis=0)
                a[pr] = jnp.where(lv2 == l, _dot_nt(qt[:, ls], rhs), a[pr])

        s_all = s_scr[...]
        s_bd = jnp.concatenate(
            [jnp.where((row_k // GLA_DK) == h, s_all, 0.0).astype(BF16) for h in range(GLA_HEADS)], axis=1)
        o_inter = _dot(qe, s_bd)
        u_all = _dot_tn(kl, v)
        for h in range(GLA_HEADS):
            vs = slice(h * GLA_DV, (h + 1) * GLA_DV)
            ks = slice(h * GLA_DK, (h + 1) * GLA_DK)
            a_h = a[h // 2][:, (h % 2) * c:(h % 2 + 1) * c]
            o = _dot(a_h.astype(BF16), v[:, vs]) + o_inter[:, vs]
            on = _rms(o, gn_ref[:, vs])
            r = pg_ref[rows, 2 * GLA_QK + GLA_V + h * GLA_DV:2 * GLA_QK + GLA_V + (h + 1) * GLA_DV]
            o_ref[rows, vs] = (on * (r * _sigmoid(r))).astype(o_ref.dtype)
            s_scr[ks, :] = dcol[ks, :] * s_all[ks, :] + u_all[ks, vs]
        return carry

    for ci in range(t // c):
        chunk(ci, 0)

    @pl.when(step == pl.num_programs(1) - 1)
    def _():
        sout_ref[...] = s_scr[...]


def _gla_call(pg, plr, s0, wg2_p, bg, gn, c, t):
    b, l, _ = pg.shape
    tri, lv2 = _gla_consts(c)
    has_state = s0 is not None
    tile = lambda w: pl.BlockSpec((None, t, w), lambda bi, i: (bi, i, 0))
    state_spec = pl.BlockSpec((None, GLA_QK, GLA_DV), lambda bi, i: (bi, 0, 0))
    in_specs = [tile(W_GLA), tile(LANES)]
    args = [pg, plr]
    if has_state:
        in_specs.append(state_spec)
        args.append(s0)
    consts = [wg2_p, bg, gn, tri, lv2]
    in_specs += [_full_spec(x.shape) for x in consts]
    return pl.pallas_call(
        functools.partial(_gla_kernel, c=c, t=t, has_state=has_state),
        grid=(b, l // t),
        in_specs=in_specs,
        out_specs=[tile(GLA_V), state_spec],
        out_shape=[jax.ShapeDtypeStruct((b, l, GLA_V), BF16),
                   jax.ShapeDtypeStruct((b, GLA_QK, GLA_DV), F32)],
        scratch_shapes=[pltpu.VMEM((GLA_QK, GLA_DV), F32)],
        compiler_params=pltpu.CompilerParams(
            dimension_semantics=("arbitrary", "arbitrary"), vmem_limit_bytes=VMEM_LIMIT),
        name="gla",
    )(*args, *consts)


def _mlstm_kernel(*refs, c, t, has_state):
    if has_state:
        (pml_ref, pif_ref, c0_ref, m0_ref, cv0_ref, cw_ref, cb_ref, bif_ref, tri_ref, sel_ref,
         o_ref, cout_ref, mout_ref, cvout_ref, c_scr, m_scr, cv_scr, qk_scr) = refs
    else:
        (pml_ref, pif_ref, cw_ref, cb_ref, bif_ref, tri_ref, sel_ref,
         o_ref, cout_ref, mout_ref, cvout_ref, c_scr, m_scr, cv_scr, qk_scr) = refs
    step = pl.program_id(1)
    hist = CONV_W - 1
    pad = 8

    @pl.when(step == 0)
    def _():
        if has_state:
            c_scr[...] = c0_ref[...]
            m_scr[...] = m0_ref[...]
            cv_scr[0:pad - hist, :] = jnp.zeros((pad - hist, 2 * ML_W), F32)
            cv_scr[pad - hist:pad, :] = cv0_ref[...]
        else:
            c_scr[...] = jnp.zeros_like(c_scr)
            m_scr[...] = jnp.zeros_like(m_scr)
            cv_scr[0:pad, :] = jnp.zeros((pad, 2 * ML_W), F32)

    cv_scr[pad:pad + t, :] = pml_ref[:, 0:2 * ML_W]
    acc = cb_ref[...] + cv_scr[pad:pad + t, :] * cw_ref[CONV_W - 1:CONV_W, :]
    for d in range(1, CONV_W):
        acc = acc + cv_scr[pad - d:pad - d + t, :] * cw_ref[CONV_W - 1 - d:CONV_W - d, :]
    conv = acc * _sigmoid(acc)
    qk_scr[:, 0:ML_W] = conv[:, 0:ML_W].astype(BF16)
    qk_scr[:, ML_W:] = (conv[:, ML_W:] * (ML_DH ** -0.5)).astype(BF16)
    cv_scr[0:pad, :] = cv_scr[t:t + pad, :]

    gts = pif_ref[...] + bif_ref[...]
    flog = pltpu.roll(_log_sigmoid(gts), LANES - ML_HEADS, axis=1)

    lane = lax.broadcasted_iota(jnp.int32, (c, LANES), 1)
    row_c = lax.broadcasted_iota(jnp.int32, (c, LANES), 0)
    causal = (lax.broadcasted_iota(jnp.int32, (c, c), 1) <= lax.broadcasted_iota(jnp.int32, (c, c), 0))
    ones_v = jnp.ones((c, ML_DH), BF16)

    def slots(pieces):
        out = jnp.zeros((c, LANES), F32)
        for j, piece in enumerate(pieces):
            if not isinstance(piece, float) and j > 0:
                piece = pltpu.roll(piece, ML_HEADS * j, axis=1)
            out = jnp.where((lane >= ML_HEADS * j) & (lane < ML_HEADS * (j + 1)), piece, out)
        return out

    def split3f(x):
        return [p.astype(F32) for p in _split3(x)]

    def chunk(ci, carry):
        r0 = ci * c
        rows = pl.ds(r0, c)
        ip = gts[r0:r0 + c]
        bc = _dot_exact_lhs(tri_ref[...], flog[r0:r0 + c])
        w = ip - bc
        cm = w
        for j in range(int(math.log2(c))):
            sh = 1 << j
            cm = jnp.where(row_c >= sh, jnp.maximum(cm, pltpu.roll(cm, sh, axis=0)), cm)
        mprev = m_scr[0:1, :]
        g = jnp.maximum(mprev, cm)
        g_last = g[c - 1:c, :]
        m_scr[...] = jnp.broadcast_to(bc[c - 1:c, :] + g_last, m_scr.shape)
        lhs_all = slots([1.0, 1.0, 1.0] + split3f(g))
        rhs = slots(split3f(w) + [-1.0, -1.0, -1.0]).astype(BF16)
        y = slots(split3f(mprev - g) + split3f(-(bc + g)) + split3f(w - g_last)).astype(BF16)
        for h in range(ML_HEADS):
            hs = slice(h * ML_DH, (h + 1) * ML_DH)
            qh = qk_scr[rows, h * ML_DH:(h + 1) * ML_DH]
            kh = qk_scr[rows, ML_W + h * ML_DH:ML_W + (h + 1) * ML_DH]
            vaug = jnp.concatenate(
                [pml_ref[rows, 2 * ML_W + h * ML_DH:2 * ML_W + (h + 1) * ML_DH].astype(BF16), ones_v],
                axis=1)
            lhs = jnp.where(lane % ML_HEADS == h, lhs_all, 0.0).astype(BF16)
            p = jnp.exp(jnp.where(causal, _dot_nt(lhs, rhs), -jnp.inf))
            bx = jnp.exp(_dot(y, sel_ref[h]))
            w_inter = bx[:, 0:ML_DH]
            e_mt = bx[:, ML_DH:2 * ML_DH]
            w_state = bx[:, 2 * ML_DH:3 * ML_DH]
            s = (_dot_nt(qh, kh) * p).astype(BF16)
            caug = c_scr[h]
            nd = _dot(s, vaug) + jnp.concatenate([w_inter, w_inter], axis=1) * _dot(qh, caug.astype(BF16))
            hh = nd[:, 0:ML_DH] / jnp.maximum(jnp.abs(nd[:, ML_DH:]), e_mt)
            mo = pml_ref[rows, 3 * ML_W + h * ML_DH:3 * ML_W + (h + 1) * ML_DH]
            o_ref[rows, hs] = (_sigmoid(mo) * hh).astype(o_ref.dtype)
            ks = (kh.astype(F32) * w_state).astype(BF16)
            dec = w_inter[c - 1:c, :]
            c_scr[h] = jnp.concatenate([dec, dec], axis=1) * caug + _dot_tn(ks, vaug)
        return carry

    for ci in range(t // c):
        chunk(ci, 0)

    @pl.when(step == pl.num_programs(1) - 1)
    def _():
        cout_ref[...] = c_scr[...]
        mout_ref[...] = m_scr[...]
        cvout_ref[...] = pml_ref[t - hist:t, 0:2 * ML_W]


def _mlstm_call(pml, pif, caug0, m0, cv0, cw, cb, bif, c, t):
    b, l, _ = pml.shape
    has_state = caug0 is not None
    tri = jnp.asarray(np.tril(np.ones((c, c), np.float32)), BF16)
    sel = np.zeros((ML_HEADS, LANES, 3 * ML_DH), np.float32)
    for h in range(ML_HEADS):
        for slot in range(9):
            sel[h, ML_HEADS * slot + h, (slot // 3) * ML_DH:(slot // 3 + 1) * ML_DH] = 1.0
    sel = jnp.asarray(sel, BF16)
    tile = lambda w: pl.BlockSpec((None, t, w), lambda bi, i: (bi, i, 0))
    c_spec = pl.BlockSpec((None, ML_HEADS, ML_DH, 2 * ML_DH), lambda bi, i: (bi, 0, 0, 0))
    m_spec = pl.BlockSpec((None, 8, LANES), lambda bi, i: (bi, 0, 0))
    cv_spec = pl.BlockSpec((None, CONV_W - 1, 2 * ML_W), lambda bi, i: (bi, 0, 0))
    in_specs = [tile(W_ML), tile(LANES)]
    args = [pml, pif]
    if has_state:
        in_specs += [c_spec, m_spec, cv_spec]
        args += [caug0, m0, cv0]
    consts = [cw, cb, bif, tri, sel]
    in_specs += [_full_spec(x.shape) for x in consts]
    return pl.pallas_call(
        functools.partial(_mlstm_kernel, c=c, t=t, has_state=has_state),
        grid=(b, l // t),
        in_specs=in_specs,
        out_specs=[tile(ML_W), c_spec, m_spec, cv_spec],
        out_shape=[jax.ShapeDtypeStruct((b, l, ML_W), BF16),
                   jax.ShapeDtypeStruct((b, ML_HEADS, ML_DH, 2 * ML_DH), F32),
                   jax.ShapeDtypeStruct((b, 8, LANES), F32),
                   jax.ShapeDtypeStruct((b, CONV_W - 1, 2 * ML_W), F32)],
        scratch_shapes=[pltpu.VMEM((ML_HEADS, ML_DH, 2 * ML_DH), F32),
                        pltpu.VMEM((8, LANES), F32),
                        pltpu.VMEM((8 + t, 2 * ML_W), F32),
                        pltpu.VMEM((t, 2 * ML_W), BF16)],
        compiler_params=pltpu.CompilerParams(
            dimension_semantics=("arbitrary", "arbitrary"), vmem_limit_bytes=VMEM_LIMIT),
        name="mlstm",
    )(*args, *consts)


def _merge_kernel(*refs, sparse):
    if sparse:
        (x_ref, ga_ref, hb_ref, pmg_ref, wug_ref, wum_ref, wo_ref, g2_ref,
         wr_hi_ref, wr_lo_ref, br_ref, tril_ref, triu_ref, x1_ref, hs_ref, meta_ref, tab_ref, cnt_scr) = refs
    else:
        (x_ref, ga_ref, hb_ref, pmg_ref, wug_ref, wum_ref, wo_ref, g2_ref,
         wr_hi_ref, wr_lo_ref, br_ref, x1_ref, hm_ref, comb_ref) = refs
    ya = _dot(ga_ref[...], wug_ref[...])
    yb = _dot(hb_ref[...], wum_ref[...])
    z = _sigmoid(pmg_ref[:, 0:D_MODEL]) * ya + _sigmoid(pmg_ref[:, D_MODEL:]) * yb
    x1 = x_ref[...] + _dot(z.astype(BF16), wo_ref[...])
    x1_ref[...] = x1
    hm = _rms(x1, g2_ref[...])
    hm_hi = hm.astype(BF16)
    if not sparse:
        hm_ref[...] = hm_hi
    hm_lo =(hm - hm_hi.astype(F32)).astype(BF16)
    logits = (_dot(hm_hi, wr_hi_ref[...]) + _dot(hm_lo, wr_hi_ref[...])
              + _dot(hm_hi, wr_lo_ref[...]) + br_ref[...])
    lane = lax.broadcasted_iota(jnp.int32, logits.shape, 1)
    neg = -jnp.inf
    is_g = (lane >= N_EXPERTS) & (lane < N_EXPERTS + N_GROUPS)
    lg = jnp.where(is_g, logits, neg)
    mg = jnp.max(lg, axis=-1, keepdims=True)
    p_top = 1.0 / jnp.sum(jnp.exp(lg - mg), axis=-1, keepdims=True)
    gi = jnp.min(jnp.where(lg == mg, lane, 2 * LANES), axis=-1, keepdims=True) - N_EXPERTS
    group_shift = int(math.log2(EXPERTS_PER_GROUP))
    sel = (lane < N_EXPERTS) & (jnp.right_shift(lane, group_shift) == gi)
    le = jnp.where(sel, logits, neg)
    v1 = jnp.max(le, axis=-1, keepdims=True)
    i1 = jnp.min(jnp.where(le == v1, lane, 2 * LANES), axis=-1, keepdims=True)
    le2 = jnp.where(lane == i1, neg, le)
    v2 = jnp.max(le2, axis=-1, keepdims=True)
    i2 = jnp.min(jnp.where(le2 == v2, lane, 2 * LANES), axis=-1, keepdims=True)
    e2 = jnp.exp(v2 - v1)
    wa = 1.0 / (1.0 + e2)
    wb = e2 / (1.0 + e2)
    if not sparse:
        comb_ref[...] = p_top * (jnp.where(lane == i1, wa, 0.0) + jnp.where(lane == i2, wb, 0.0))
        return

    @pl.when(pl.program_id(0) == 0)
    def _():
        cnt_scr[...] = jnp.zeros_like(cnt_scr)

    tm = logits.shape[0]
    oh1 = lane == i1
    oh2 = lane == i2
    both = jnp.where(oh1 | oh2, 1.0, 0.0)
    cnt = jnp.sum(both, axis=0, keepdims=True)
    cnt = jnp.floor((cnt + (SEG_ALIGN - 1)) * (1.0 / SEG_ALIGN)) * SEG_ALIGN
    lower = _dot(jnp.broadcast_to(cnt, (8, LANES)).astype(BF16), triu_ref[...])[0:1, :]
    lpos = _dot(tril_ref[...], both.astype(BF16)) + lower
    lp1 = jnp.sum(jnp.where(oh1, lpos, 0.0), axis=-1, keepdims=True)
    lp2 = jnp.sum(jnp.where(oh2, lpos, 0.0), axis=-1, keepdims=True)
    pos = lax.broadcasted_iota(jnp.int32, (tm, hs_ref.shape[0]), 1)
    onehot = jnp.where((pos == lp1.astype(jnp.int32)) | (pos == lp2.astype(jnp.int32)), 1.0, 0.0)
    hs_ref[...] = _pack_pairs(_dot_tn(onehot.astype(BF16), hm_hi), exact=True)
    cols = (lp1, lp2, p_top * wa, p_top * wb)
    meta = jnp.zeros(logits.shape, F32)
    for ci, col in enumerate(cols):
        meta = jnp.where(lane == ci, col, meta)
    meta_ref[...] = meta
    row8 = lax.broadcasted_iota(jnp.int32, (8, LANES), 0)
    tab_ref[...] = jnp.where(row8 == 0, cnt, jnp.where(row8 == 1, lower, jnp.where(row8 == 2, cnt_scr[...], 0.0)))
    cnt_scr[...] += cnt


def _merge_call(x2, ga, hb, pmg, wug, wum, wo, g2, wr_hi, wr_lo, br, tm, sparse):
    n = x2.shape[0]
    tile = lambda w: pl.BlockSpec((tm, w), lambda i: (i, 0))
    consts = [wug, wum, wo, g2, wr_hi, wr_lo, br]
    scratch = []
    if sparse:
        consts.append(jnp.asarray(np.tril(np.ones((tm, tm), np.float32), -1), BF16))
        consts.append(jnp.asarray(np.triu(np.ones((LANES, LANES), np.float32), 1), BF16))
        tiles = lambda *shape: pl.BlockSpec((None,) + shape, lambda i: (i, 0, 0))
        local_rows = 2 * tm + SEG_ALIGN * N_EXPERTS
        out_specs = [tile(D_MODEL), tiles(local_rows, D_MODEL // 2), tile(LANES), tiles(8, LANES)]
        out_shape = [jax.ShapeDtypeStruct((n, D_MODEL), F32),
                     jax.ShapeDtypeStruct((n // tm, local_rows, D_MODEL // 2), jnp.uint32),
                     jax.ShapeDtypeStruct((n, LANES), F32),
                     jax.ShapeDtypeStruct((n // tm, 8, LANES), F32)]
        scratch.append(pltpu.VMEM((1, LANES), F32))
    else:
        out_specs = [tile(D_MODEL), tile(D_MODEL), tile(LANES)]
        out_shape = [jax.ShapeDtypeStruct((n, D_MODEL), F32),
                     jax.ShapeDtypeStruct((n, D_MODEL), BF16),
                     jax.ShapeDtypeStruct((n, LANES), F32)]
    return pl.pallas_call(
        functools.partial(_merge_kernel, sparse=sparse),
        grid=(n // tm,),
        in_specs=[tile(D_MODEL), tile(GLA_V), tile(ML_W), tile(W_MG)]
                 + [_full_spec(x.shape) for x in consts],
        out_specs=out_specs,
        out_shape=out_shape,
        scratch_shapes=scratch,
        compiler_params=pltpu.CompilerParams(
            dimension_semantics=("arbitrary",), vmem_limit_bytes=VMEM_LIMIT),
        name="merge",
    )(x2, ga, hb, pmg, *consts)


MOE_ROWS = 256
SEG_ALIGN = 8
SEG_SIZES = (256, 128, 64, 32, 16, 8)
HIGH16 = 0xFFFF0000


def _pack_pairs(x, exact=False):
    w = x.shape[1] // 2
    if not exact:
        x = x.astype(BF16).astype(F32)
    bits = lax.bitcast_convert_type(x, jnp.uint32)
    return (bits[:, :w] & jnp.uint32(HIGH16)) | (bits[:, w:] >> 16)


def _unpack_pairs(u):
    hi = lax.bitcast_convert_type(u & jnp.uint32(HIGH16), F32)
    lo = lax.bitcast_convert_type(u << 16, F32)
    return jnp.concatenate([hi.astype(BF16), lo.astype(BF16)], axis=1)


def _segment_kernel(*refs, reverse, n_experts):
    if reverse:
        cnt_ref, loff_ref, gpos_ref, flat_ref, tiled_ref, zero_ref, sem = refs
    else:
        cnt_ref, loff_ref, gpos_ref, tail_ref, tiled_ref, flat_ref, zero_ref, sem, zsem = refs
    step = pl.program_id(0)

    if reverse:
        @pl.when(step == 0)
        def _():
            zero_ref[...] = jnp.zeros_like(zero_ref)

    if not reverse:
        @pl.when(step == 0)
        def _():
            zero_ref[...] = jnp.zeros_like(zero_ref)
            n_tiles = flat_ref.shape[0] // MOE_ROWS

            def clear(row):
                start = pl.multiple_of(row, MOE_ROWS)
                return pltpu.make_async_copy(zero_ref, flat_ref.at[pl.ds(start, MOE_ROWS), :], zsem)

            def unused(fn):
                def body(j, carry):
                    fn(clear(j * MOE_ROWS))
                    return carry
                lax.fori_loop(tail_ref[2 * n_experts], n_tiles, body, 0)

            for e in range(n_experts):
                @pl.when(tail_ref[n_experts + e] > 0)
                def _():
                    clear(tail_ref[e]).start()
            unused(lambda c: c.start())
            for e in range(n_experts):
                @pl.when(tail_ref[n_experts + e] > 0)
                def _():
                    clear(tail_ref[e]).wait()
            unused(lambda c: c.wait())

    def segments(fn):
        def body(e, carry):
            idx = step * n_experts + e
            n, lo, gp = cnt_ref[idx], loff_ref[idx], gpos_ref[idx]
            for size in SEG_SIZES:
                @pl.when((n & size) != 0)
                def _():
                    off = n & ~(2 * size - 1)
                    local = tiled_ref.at[step, pl.ds(pl.multiple_of(lo + off, SEG_ALIGN), size), :]
                    glob = flat_ref.at[pl.ds(pl.multiple_of(gp + off, SEG_ALIGN), size), :]
                    src, dst = (glob, local) if reverse else (local, glob)
                    fn(pltpu.make_async_copy(src, dst, sem))
            return carry
        lax.fori_loop(0, n_experts, body, 0)
        if reverse:
            last = step * n_experts + n_experts - 1
            used = loff_ref[last] + cnt_ref[last]
            rest = tiled_ref.shape[1] - used
            for size in SEG_SIZES:
                @pl.when((rest & size) != 0)
                def _():
                    start = pl.multiple_of(used + (rest & ~(2 * size - 1)), SEG_ALIGN)
                    fn(pltpu.make_async_copy(zero_ref.at[pl.ds(0, size), :],
                                             tiled_ref.at[step, pl.ds(start, size), :], sem))

    segments(lambda c: c.start())
    segments(lambda c: c.wait())


def _segment_call(tables, tail, src, n_rows, reverse):
    any_spec = pl.BlockSpec(memory_space=pl.ANY)
    if reverse:
        n_tiles, out_shape, args = n_rows[0], n_rows, (*tables, src)
        scratch = [pltpu.VMEM((MOE_ROWS, src.shape[1]), src.dtype), pltpu.SemaphoreType.DMA]
    else:
        n_tiles, out_shape = src.shape[0], (n_rows, src.shape[2])
        scratch = [pltpu.VMEM((MOE_ROWS, src.shape[2]), src.dtype),
                   pltpu.SemaphoreType.DMA, pltpu.SemaphoreType.DMA]
        args = (*tables, tail, src)
    return pl.pallas_call(
        functools.partial(_segment_kernel, reverse=reverse, n_experts=N_EXPERTS),
        grid_spec=pltpu.PrefetchScalarGridSpec(
            num_scalar_prefetch=len(args) - 1,
            grid=(n_tiles,),
            in_specs=[any_spec],
            out_specs=any_spec,
            scratch_shapes=scratch,
        ),
        out_shape=jax.ShapeDtypeStruct(out_shape, src.dtype),
        compiler_params=pltpu.CompilerParams(
            dimension_semantics=("arbitrary",), vmem_limit_bytes=VMEM_LIMIT),
        name="moe_undispatch" if reverse else "moe_dispatch",
    )(*args)


def _gmm_kernel(te_ref, nv_ref, xs_ref, wg_ref, wu_ref, wd_ref, o_ref):
    used = pl.program_id(0) < nv_ref[0]

    @pl.when(used)
    def _():
        x = _unpack_pairs(xs_ref[...])
        a = _dot(x, wg_ref[...].astype(BF16))
        u = _dot(x, wu_ref[...].astype(BF16))
        hh = (a * _sigmoid(a)) * u
        o_ref[...] = _pack_pairs(_dot(hh.astype(BF16), wd_ref[...].astype(BF16)))

    @pl.when(jnp.logical_not(used))
    def _():
        o_ref[...] = jnp.zeros_like(o_ref)


def _gmm_call(tile_expert, n_valid, xs, wg, wu, wd):
    n_tiles = xs.shape[0] // MOE_ROWS
    rows = lambda j, te, nv: (jnp.minimum(j, nv[0] - 1), 0)
    wsel = lambda j, te, nv: (te[j], 0, 0)
    return pl.pallas_call(
        _gmm_kernel,
        grid_spec=pltpu.PrefetchScalarGridSpec(
            num_scalar_prefetch=2,
            grid=(n_tiles,),
            in_specs=[pl.BlockSpec((MOE_ROWS, D_MODEL // 2), rows),
                      pl.BlockSpec((None, D_MODEL, D_EXPERT), wsel),
                      pl.BlockSpec((None, D_MODEL, D_EXPERT), wsel),
                      pl.BlockSpec((None, D_EXPERT, D_MODEL), wsel)],
            out_specs=pl.BlockSpec((MOE_ROWS, D_MODEL // 2), lambda j, te, nv: (j, 0)),
        ),
        out_shape=jax.ShapeDtypeStruct(xs.shape, xs.dtype),
        compiler_params=pltpu.CompilerParams(
            dimension_semantics=("arbitrary",), vmem_limit_bytes=VMEM_LIMIT),
        name="moe_grouped",
    )(tile_expert, n_valid, xs, wg, wu, wd)


def _combine_kernel(x1_ref, meta_ref, gf_ref, ol_ref, y_ref):
    tm = x1_ref.shape[0]
    rows = _unpack_pairs(ol_ref[...])
    pos = lax.broadcasted_iota(jnp.int32, (tm, ol_ref.shape[0]), 1)
    lp1 = meta_ref[:, 0:1].astype(jnp.int32)
    lp2 = meta_ref[:, 1:2].astype(jnp.int32)
    q = jnp.where(pos == lp1, meta_ref[:, 2:3], 0.0) + jnp.where(pos == lp2, meta_ref[:, 3:4], 0.0)
    q_hi = q.astype(BF16)
    q_lo = (q - q_hi.astype(F32)).astype(BF16)
    y = x1_ref[...] + _dot(q_hi, rows) + _dot(q_lo, rows)
    y_ref[...] = _rms(y, gf_ref[...])


def _combine_call(x1, meta, gf, out_local, tm):
    n = x1.shape[0]
    tile = lambda w: pl.BlockSpec((tm, w), lambda i: (i, 0))
    return pl.pallas_call(
        _combine_kernel,
        grid=(n // tm,),
        in_specs=[tile(D_MODEL), tile(LANES), _full_spec(gf.shape),
                  pl.BlockSpec((None,) + out_local.shape[1:], lambda i: (i, 0, 0))],
        out_specs=tile(D_MODEL),
        out_shape=jax.ShapeDtypeStruct((n, D_MODEL), F32),
        compiler_params=pltpu.CompilerParams(
            dimension_semantics=("arbitrary",), vmem_limit_bytes=VMEM_LIMIT),
        name="moe_combine",
    )(x1, meta, gf, out_local)


def _sparse_moe(x1, hs, meta, tab, p, tm):
    n_tiles = -(-(hs.shape[0] * hs.shape[1]) // MOE_ROWS) + N_EXPERTS
    tab = tab[:, :, :N_EXPERTS].astype(jnp.int32)
    cnt, loff, before = tab[:, 0], tab[:, 1], tab[:, 2]
    total = before[-1] + cnt[-1]
    tiles = (total + (MOE_ROWS - 1)) // MOE_ROWS
    tile_end = jnp.cumsum(tiles)
    tile_start = tile_end - tiles
    n_valid = tile_end[-1:]
    gpos = tile_start[None, :] * MOE_ROWS + before
    j = jnp.minimum(jnp.arange(n_tiles, dtype=jnp.int32), n_valid - 1)
    tile_expert = jnp.sum((j[:, None] >= tile_end[None, :]).astype(jnp.int32), axis=1)
    tail = jnp.concatenate([(tile_end - 1) * MOE_ROWS, total, n_valid]).astype(jnp.int32)
    tables = (cnt.reshape(-1), loff.reshape(-1), gpos.reshape(-1))
    xs = _segment_call(tables, tail, hs, n_tiles * MOE_ROWS, reverse=False)
    out_sorted = _gmm_call(tile_expert, n_valid.astype(jnp.int32), xs, p["wg"], p["wu"], p["wd"])
    out_local = _segment_call(tables, None, out_sorted, hs.shape, reverse=True)
    return _combine_call(x1, meta, p["gf"], out_local, tm)


def _moe_kernel(x1_ref, hm_ref, comb_ref, wg_ref, wu_ref, wd_ref, gf_ref, y_ref, acc_ref):
    e = pl.program_id(1)

    @pl.when(e == 0)
    def _():
        acc_ref[...] = jnp.zeros_like(acc_ref)

    hm = hm_ref[...]
    a = _dot(hm, wg_ref[...].astype(BF16))
    u = _dot(hm, wu_ref[...].astype(BF16))
    lane = lax.broadcasted_iota(jnp.int32, comb_ref.shape, 1)
    ce = jnp.sum(jnp.where(lane == e, comb_ref[...], 0.0), axis=-1, keepdims=True)
    hh = (a * _sigmoid(a)) * u * ce
    acc_ref[...] += _dot(hh.astype(BF16), wd_ref[...].astype(BF16))

    @pl.when(e == pl.num_programs(1) - 1)
    def _():
        y_ref[...] = _rms(x1_ref[...] + acc_ref[...], gf_ref[...])


def _moe_call(x1, hm, comb, wg, wu, wd, gf, tm):
    n = x1.shape[0]
    tile = lambda w: pl.BlockSpec((tm, w), lambda i, e: (i, 0))
    return pl.pallas_call(
        _moe_kernel,
        grid=(n // tm, N_EXPERTS),
        in_specs=[tile(D_MODEL), tile(D_MODEL), tile(LANES),
                  pl.BlockSpec((None, D_MODEL, D_EXPERT), lambda i, e: (e, 0, 0)),
                  pl.BlockSpec((None, D_MODEL, D_EXPERT), lambda i, e: (e, 0, 0)),
                  pl.BlockSpec((None, D_EXPERT, D_MODEL), lambda i, e: (e, 0, 0)),
                  _full_spec(gf.shape)],
        out_specs=tile(D_MODEL),
        out_shape=jax.ShapeDtypeStruct((n, D_MODEL), F32),
        scratch_shapes=[pltpu.VMEM((tm, D_MODEL), F32)],
        compiler_params=pltpu.CompilerParams(
            dimension_semantics=("arbitrary", "arbitrary"), vmem_limit_bytes=VMEM_LIMIT),
        name="moe",
    )(x1, hm, comb, wg, wu, wd, gf)


def _pad_cols(w, width):
    return jnp.pad(w, ((0, 0), (0, width - w.shape[1])))


def _prep_weights(norm1_g, w_in, gla_w_gate2, gla_b_gate, gla_norm_g, w_up_gla,
                  ml_conv_w, ml_conv_b, ml_b_i, ml_b_f, w_up_ml, w_out,
                  norm2_g, router_g_w, router_g_b, router_e_w, router_e_b,
                  moe_w_gate, moe_w_up, moe_w_down, final_g):
    o_lr = W_GLA
    o_ml = o_lr + GLA_GATE_RANK
    o_if = o_ml + W_ML
    o_mg = o_if + 2 * ML_HEADS
    w_p = jnp.concatenate([
        w_in[:, 0:o_lr], _pad_cols(w_in[:, o_lr:o_ml], LANES),
        w_in[:, o_ml:o_if], _pad_cols(w_in[:, o_if:o_mg], LANES),
        w_in[:, o_mg:]], axis=1).astype(BF16)
    wr = _pad_cols(jnp.concatenate([router_e_w, router_g_w], axis=1), LANES)
    wr_hi = wr.astype(BF16)
    wr_lo = (wr - wr_hi.astype(F32)).astype(BF16)
    br = _pad_cols(jnp.concatenate([router_e_b, router_g_b])[None, :], LANES)
    wg2 = jnp.pad(gla_w_gate2, ((0, LANES - GLA_GATE_RANK), (0, 0)))
    wg2_hi = wg2.astype(BF16)
    return dict(
        g1=norm1_g[None, :], w_p=w_p,
        wg2_p=jnp.stack([wg2_hi, (wg2 - wg2_hi.astype(F32)).astype(BF16)]),
        bg=gla_b_gate[None, :], gn=gla_norm_g[None, :],
        wug=w_up_gla.astype(BF16),
        cw=ml_conv_w, cb=ml_conv_b[None, :],
        bif=_pad_cols(jnp.concatenate([ml_b_i, ml_b_f])[None, :], LANES),
        wum=w_up_ml.astype(BF16), wo=w_out.astype(BF16),
        g2=norm2_g[None, :], wr_hi=wr_hi, wr_lo=wr_lo, br=br,
        wg=moe_w_gate.reshape(N_EXPERTS, D_MODEL, D_EXPERT),
        wu=moe_w_up.reshape(N_EXPERTS, D_MODEL, D_EXPERT),
        wd=moe_w_down.reshape(N_EXPERTS, D_EXPERT, D_MODEL),
        gf=final_g[None, :],
    )


def _layer(x, gla_s0, ml_c0, ml_n0, ml_m0, conv0, p, *, chunk, seq_tile, row_tile, moe_tile, sparse):
    b, l, _ = x.shape
    n = b * l
    x2 = x.reshape(n, D_MODEL)
    pg, plr, pml, pif, pmg = _proj_call(x2, p["g1"], p["w_p"], row_tile)
    r3 = lambda a: a.reshape(b, l, a.shape[-1])
    s0 = None if gla_s0 is None else gla_s0.reshape(b, GLA_QK, GLA_DV)
    ga, gla_s = _gla_call(r3(pg), r3(plr), s0, p["wg2_p"], p["bg"], p["gn"], chunk, seq_tile)
    gla_s = gla_s.reshape(b, GLA_HEADS, GLA_DK, GLA_DV)
    if ml_c0 is None:
        caug0 = m0 = None
    else:
        caug0 = jnp.concatenate([ml_c0, jnp.broadcast_to(ml_n0[..., None], ml_c0.shape)], axis=-1)
        m0 = jnp.broadcast_to(_pad_cols(ml_m0, LANES)[:, None, :], (b, 8, LANES))
    hb, caug, m_b, new_conv = _mlstm_call(r3(pml), r3(pif), caug0, m0, conv0,
                                          p["cw"], p["cb"], p["bif"], chunk, seq_tile)
    x1, hm, route, *counts = _merge_call(x2, ga.reshape(n, GLA_V), hb.reshape(n, ML_W), pmg,
                                         p["wug"], p["wum"], p["wo"], p["g2"],
                                         p["wr_hi"], p["wr_lo"], p["br"], row_tile, sparse)
    if sparse:
        y = _sparse_moe(x1, hm, route, counts[0], p, moe_tile)
    else:
        y = _moe_call(x1, hm, route, p["wg"], p["wu"], p["wd"], p["gf"], moe_tile)
    return (y.reshape(b, l, D_MODEL), gla_s[None], caug[..., 0:ML_DH][None],
            caug[..., ML_DH][None], m_b[:, 0, 0:ML_HEADS][None], new_conv[None])


def kernel(x_prompt, x_sample, state_gla_S, state_mlstm_C, state_mlstm_n, state_mlstm_m, state_mlstm_conv, norm1_g, w_in, gla_w_gate2, gla_b_gate, gla_norm_g, w_up_gla, ml_conv_w, ml_conv_b, ml_b_i, ml_b_f, w_up_ml, w_out, norm2_g, router_g_w, router_g_b, router_e_w, router_e_b, moe_w_gate, moe_w_up, moe_w_down, final_g):
    assert norm1_g.shape[0] == 1, "single-layer trunk"
    p = _prep_weights(norm1_g[0], w_in[0], gla_w_gate2[0], gla_b_gate[0], gla_norm_g[0], w_up_gla[0],
                      ml_conv_w[0], ml_conv_b[0], ml_b_i[0], ml_b_f[0], w_up_ml[0], w_out[0],
                      norm2_g[0], router_g_w[0], router_g_b[0], router_e_w[0], router_e_b[0],
                      moe_w_gate[0], moe_w_up[0], moe_w_down[0], final_g)
    yp, *sp = _layer(x_prompt, None, None, None, None, None, p,
                     chunk=128, seq_tile=512, row_tile=256, moe_tile=256, sparse=True)
    dec_seq = x_sample.shape[1]
    ns = x_sample.shape[0] * dec_seq
    ys, *ss = _layer(x_sample, state_gla_S[0], state_mlstm_C[0], state_mlstm_n[0], state_mlstm_m[0],
                     state_mlstm_conv[0], p,
                     chunk=dec_seq, seq_tile=dec_seq, row_tile=ns, moe_tile=ns, sparse=False)
    return (yp, ys, *sp, *ss)
```

```python
import functools
import math

import numpy as np
import jax
import jax.numpy as jnp
from jax import lax
from jax.experimental import pallas as pl
from jax.experimental.pallas import tpu as pltpu

D_MODEL = 1024
GLA_HEADS = 4
GLA_DK = 64
GLA_DV = 128
GLA_GATE_RANK = 16
GLA_TAU = 16.0
ML_HEADS = 4
ML_DH = 128
CONV_W = 4
N_GROUPS = 4
EXPERTS_PER_GROUP = 8
N_EXPERTS = N_GROUPS * EXPERTS_PER_GROUP
D_EXPERT = 256
EPS = 1e-6

GLA_QK = GLA_HEADS * GLA_DK
GLA_V = GLA_HEADS * GLA_DV
ML_W = ML_HEADS * ML_DH

LANES = 128
VMEM_LIMIT = 56 * 1024 * 1024

W_GLA = 2 * GLA_QK + 2 * GLA_V
W_ML = 2 * ML_W + ML_W + ML_W
W_MG = 2 * D_MODEL
PROJ_WIDTHS = (W_GLA, LANES, W_ML, LANES, W_MG)

F32 = jnp.float32
BF16 = jnp.bfloat16


def _dot(a, b):
    return jnp.dot(a, b, preferred_element_type=F32)


def _dot_nt(a, b):
    return lax.dot_general(a, b, (((1,), (1,)), ((), ())), preferred_element_type=F32)


def _dot_tn(a, b):
    return lax.dot_general(a, b, (((0,), (0,)), ((), ())), preferred_element_type=F32)


def _split3(x):
    hi = x.astype(BF16)
    r1 = x - hi.astype(F32)
    mid = r1.astype(BF16)
    lo = (r1 - mid.astype(F32)).astype(BF16)
    return hi, mid, lo


def _dot_exact_lhs(m, x):
    hi, mid, lo = _split3(x)
    return _dot(m, hi) + _dot(m, mid) + _dot(m, lo)


def _log_sigmoid(z):
    return jnp.minimum(z, 0.0) - jnp.log1p(jnp.exp(-jnp.abs(z)))


def _sigmoid(z):
    return 1.0 / (1.0 + jnp.exp(-z))


def _rms(x, g):
    return x * lax.rsqrt(jnp.mean(x * x, axis=-1, keepdims=True) + EPS) * g


def _full_spec(shape):
    nd = len(shape)
    return pl.BlockSpec(shape, lambda *_: (0,) * nd)


def _proj_kernel(x_ref, g_ref, w_ref, *o_refs):
    h = _rms(x_ref[...], g_ref[...]).astype(BF16)
    off = 0
    for o_ref, width in zip(o_refs, PROJ_WIDTHS):
        o_ref[...] = _dot(h, w_ref[:, off:off + width])
        off += width


def _proj_call(x2, g, w_p, tm):
    n = x2.shape[0]
    return pl.pallas_call(
        _proj_kernel,
        grid=(n // tm,),
        in_specs=[pl.BlockSpec((tm, D_MODEL), lambda i: (i, 0)),
                  _full_spec(g.shape), _full_spec(w_p.shape)],
        out_specs=[pl.BlockSpec((tm, w), lambda i: (i, 0)) for w in PROJ_WIDTHS],
        out_shape=[jax.ShapeDtypeStruct((n, w), F32) for w in PROJ_WIDTHS],
        compiler_params=pltpu.CompilerParams(
            dimension_semantics=("arbitrary",), vmem_limit_bytes=VMEM_LIMIT),
        name="in_proj",
    )(x2, g, w_p)


def _gla_consts(c):
    nlev = int(math.log2(c))
    assert 1 << nlev == c
    t = np.arange(c)[:, None]
    j = np.arange(c)[None, :]
    lv = np.full((c, c), -1, np.int32)
    for l in range(nlev):
        h = c >> (l + 1)
        upper = (t % (2 * h)) >= h
        same = (j // (2 * h)) == (t // (2 * h))
        s_lower = (j % (2 * h)) < h
        lv[np.broadcast_to(upper, (c, c)) & same & s_lower] = l
    lv[np.eye(c, dtype=bool)] = nlev
    tri = (j <= t).astype(np.float32)
    return jnp.asarray(tri, BF16), jnp.asarray(np.concatenate([lv, lv], axis=1))


def _gla_kernel(*refs, c, t, has_state):
    if has_state:
        (pg_ref, plr_ref, s0_ref, wg2_ref, bg_ref, gn_ref, tri_ref, lv_ref,
         o_ref, sout_ref, s_scr) = refs
    else:
        (pg_ref, plr_ref, wg2_ref, bg_ref, gn_ref, tri_ref, lv_ref,
         o_ref, sout_ref, s_scr) = refs
    nlev = int(math.log2(c))
    step = pl.program_id(1)

    @pl.when(step == 0)
    def _():
        if has_state:
            s_scr[...] = s0_ref[...]
        else:
            s_scr[...] = jnp.zeros_like(s_scr)

    lane_k = lax.broadcasted_iota(jnp.int32, (c, GLA_QK), 1)
    first_of_pair = (lane_k % (2 * GLA_DK)) < GLA_DK
    row_k = lax.broadcasted_iota(jnp.int32, (GLA_QK, GLA_DV), 0)
    row_t = lax.broadcasted_iota(jnp.int32, (c, GLA_QK), 0)

    def block_ref(b, blk, idx):
        if blk >= 8:
            b3 = b.reshape(c // blk, blk, GLA_QK)
            return jnp.broadcast_to(b3[:, idx:idx + 1, :], b3.shape).reshape(c, GLA_QK)
        r = row_t % blk
        out = b
        for sh in range(-idx, blk - idx):
            if sh != 0:
                out = jnp.where(r - idx == sh, pltpu.roll(b, sh % c, axis=0), out)
        return out

    def chunk(ci, carry):
        r0 = ci * c
        rows = pl.ds(r0, c)
        q = pg_ref[rows, 0:GLA_QK] * (GLA_DK ** -0.5)
        k = pg_ref[rows, GLA_QK:2 * GLA_QK]
        v = pg_ref[rows, 2 * GLA_QK:2 * GLA_QK + GLA_V].astype(BF16)
        glr = plr_ref[rows, :]
        g_hi = glr.astype(BF16)
        g_lo = (glr - g_hi.astype(F32)).astype(BF16)
        z = (_dot(g_hi, wg2_ref[0]) + _dot(g_lo, wg2_ref[0]) + _dot(g_hi, wg2_ref[1])) + bg_ref[...]
        la = _log_sigmoid(z) * (1.0 / GLA_TAU)
        b = _dot_exact_lhs(tri_ref[...], la)
        b_last = b[c - 1:c, :]
        qe = (q * jnp.exp(b)).astype(BF16)
        kl = (k * jnp.exp(b_last - b)).astype(BF16)
        dcol = jnp.exp(jnp.broadcast_to(b_last, (LANES, GLA_QK)).T)

        k_a = jnp.where(first_of_pair, k, 0.0)
        k_b = k - k_a
        lv2 = lv_ref[...]
        a = [jnp.zeros((c, 2 * c), F32) for _ in range(GLA_HEADS // 2)]
        for l in range(nlev + 1):
            if l < nlev:
                half = c >> (l + 1)
                d = b - block_ref(b, 2 * half, half - 1)
                e = jnp.exp(jnp.minimum(d, -d))
                qt, kta, ktb = q * e, k_a * e, k_b * e
            else:
                qt, kta, ktb = q, k_a, k_b
            qt, kta, ktb = qt.astype(BF16), kta.astype(BF16), ktb.astype(BF16)
            for pr in range(GLA_HEADS // 2):
                ls = slice(pr * 2 * GLA_DK, (pr + 1) * 2 * GLA_DK)
                rhs = jnp.concatenate([kta[:, ls], ktb[:, ls]], axis=0)
                a[pr] = jnp.where(lv2 == l, _dot_nt(qt[:, ls], rhs), a[pr])

        s_all = s_scr[...]
        s_bd = jnp.concatenate(
            [jnp.where((row_k // GLA_DK) == h, s_all, 0.0).astype(BF16) for h in range(GLA_HEADS)], axis=1)
        o_inter = _dot(qe, s_bd)
        u_all = _dot_tn(kl, v)
        for h in range(GLA_HEADS):
            vs = slice(h * GLA_DV, (h + 1) * GLA_DV)
            ks = slice(h * GLA_DK, (h + 1) * GLA_DK)
            a_h = a[h // 2][:, (h % 2) * c:(h % 2 + 1) * c]
            o = _dot(a_h.astype(BF16), v[:, vs]) + o_inter[:, vs]
            on = _rms(o, gn_ref[:, vs])
            r = pg_ref[rows, 2 * GLA_QK + GLA_V + h * GLA_DV:2 * GLA_QK + GLA_V + (h + 1) * GLA_DV]
            o_ref[rows, vs] = (on * (r * _sigmoid(r))).astype(o_ref.dtype)
            s_scr[ks, :] = dcol[ks, :] * s_all[ks, :] + u_all[ks, vs]
        return carry

    for ci in range(t // c):
        chunk(ci, 0)

    @pl.when(step == pl.num_programs(1) - 1)
    def _():
        sout_ref[...] = s_scr[...]


def _gla_call(pg, plr, s0, wg2_p, bg, gn, c, t):
    b, l, _ = pg.shape
    tri, lv2 = _gla_consts(c)
    has_state = s0 is not None
    tile = lambda w: pl.BlockSpec((None, t, w), lambda bi, i: (bi, i, 0))
    state_spec = pl.BlockSpec((None, GLA_QK, GLA_DV), lambda bi, i: (bi, 0, 0))
    in_specs = [tile(W_GLA), tile(LANES)]
    args = [pg, plr]
    if has_state:
        in_specs.append(state_spec)
        args.append(s0)
    consts = [wg2_p, bg, gn, tri, lv2]
    in_specs += [_full_spec(x.shape) for x in consts]
    return pl.pallas_call(
        functools.partial(_gla_kernel, c=c, t=t, has_state=has_state),
        grid=(b, l // t),
        in_specs=in_specs,
        out_specs=[tile(GLA_V), state_spec],
        out_shape=[jax.ShapeDtypeStruct((b, l, GLA_V), BF16),
                   jax.ShapeDtypeStruct((b, GLA_QK, GLA_DV), F32)],
        scratch_shapes=[pltpu.VMEM((GLA_QK, GLA_DV), F32)],
        compiler_params=pltpu.CompilerParams(
            dimension_semantics=("arbitrary", "arbitrary"), vmem_limit_bytes=VMEM_LIMIT),
        name="gla",
    )(*args, *consts)


def _mlstm_kernel(*refs, c, t, has_state):
    if has_state:
        (pml_ref, pif_ref, c0_ref, m0_ref, cv0_ref, cw_ref, cb_ref, bif_ref, tri_ref, sel_ref,
         o_ref, cout_ref, mout_ref, cvout_ref, c_scr, m_scr, cv_scr, qk_scr) = refs
    else:
        (pml_ref, pif_ref, cw_ref, cb_ref, bif_ref, tri_ref, sel_ref,
         o_ref, cout_ref, mout_ref, cvout_ref, c_scr, m_scr, cv_scr, qk_scr) = refs
    step = pl.program_id(1)
    hist = CONV_W - 1
    pad = 8

    @pl.when(step == 0)
    def _():
        if has_state:
            c_scr[...] = c0_ref[...]
            m_scr[...] = m0_ref[...]
            cv_scr[0:pad - hist, :] = jnp.zeros((pad - hist, 2 * ML_W), F32)
            cv_scr[pad - hist:pad, :] = cv0_ref[...]
        else:
            c_scr[...] = jnp.zeros_like(c_scr)
            m_scr[...] = jnp.zeros_like(m_scr)
            cv_scr[0:pad, :] = jnp.zeros((pad, 2 * ML_W), F32)

    cv_scr[pad:pad + t, :] = pml_ref[:, 0:2 * ML_W]
    acc = cb_ref[...] + cv_scr[pad:pad + t, :] * cw_ref[CONV_W - 1:CONV_W, :]
    for d in range(1, CONV_W):
        acc = acc + cv_scr[pad - d:pad - d + t, :] * cw_ref[CONV_W - 1 - d:CONV_W - d, :]
    conv = acc * _sigmoid(acc)
    qk_scr[:, 0:ML_W] = conv[:, 0:ML_W].astype(BF16)
    qk_scr[:, ML_W:] = (conv[:, ML_W:] * (ML_DH ** -0.5)).astype(BF16)
    cv_scr[0:pad, :] = cv_scr[t:t + pad, :]

    gts = pif_ref[...] + bif_ref[...]
    flog = pltpu.roll(_log_sigmoid(gts), LANES - ML_HEADS, axis=1)

    lane = lax.broadcasted_iota(jnp.int32, (c, LANES), 1)
    row_c = lax.broadcasted_iota(jnp.int32, (c, LANES), 0)
    causal = (lax.broadcasted_iota(jnp.int32, (c, c), 1) <= lax.broadcasted_iota(jnp.int32, (c, c), 0))
    ones_v = jnp.ones((c, ML_DH), BF16)

    def slots(pieces):
        out = jnp.zeros((c, LANES), F32)
        for j, piece in enumerate(pieces):
            if not isinstance(piece, float) and j > 0:
                piece = pltpu.roll(piece, ML_HEADS * j, axis=1)
            out = jnp.where((lane >= ML_HEADS * j) & (lane < ML_HEADS * (j + 1)), piece, out)
        return out

    def split3f(x):
        return [p.astype(F32) for p in _split3(x)]

    def chunk(ci, carry):
        r0 = ci * c
        rows = pl.ds(r0, c)
        ip = gts[r0:r0 + c]
        bc = _dot_exact_lhs(tri_ref[...], flog[r0:r0 + c])
        w = ip - bc
        cm = w
        for j in range(int(math.log2(c))):
            sh = 1 << j
            cm = jnp.where(row_c >= sh, jnp.maximum(cm, pltpu.roll(cm, sh, axis=0)), cm)
        mprev = m_scr[0:1, :]
        g = jnp.maximum(mprev, cm)
        g_last = g[c - 1:c, :]
        m_scr[...] = jnp.broadcast_to(bc[c - 1:c, :] + g_last, m_scr.shape)
        lhs_all = slots([1.0, 1.0, 1.0] + split3f(g))
        rhs = slots(split3f(w) + [-1.0, -1.0, -1.0]).astype(BF16)
        y = slots(split3f(mprev - g) + split3f(-(bc + g)) + split3f(w - g_last)).astype(BF16)
        for h in range(ML_HEADS):
            hs = slice(h * ML_DH, (h + 1) * ML_DH)
            qh = qk_scr[rows, h * ML_DH:(h + 1) * ML_DH]
            kh = qk_scr[rows, ML_W + h * ML_DH:ML_W + (h + 1) * ML_DH]
            vaug = jnp.concatenate(
                [pml_ref[rows, 2 * ML_W + h * ML_DH:2 * ML_W + (h + 1) * ML_DH].astype(BF16), ones_v],
                axis=1)
            lhs = jnp.where(lane % ML_HEADS == h, lhs_all, 0.0).astype(BF16)
            p = jnp.exp(jnp.where(causal, _dot_nt(lhs, rhs), -jnp.inf))
            bx = jnp.exp(_dot(y, sel_ref[h]))
            w_inter = bx[:, 0:ML_DH]
            e_mt = bx[:, ML_DH:2 * ML_DH]
            w_state = bx[:, 2 * ML_DH:3 * ML_DH]
            s = (_dot_nt(qh, kh) * p).astype(BF16)
            caug = c_scr[h]
            nd = _dot(s, vaug) + jnp.concatenate([w_inter, w_inter], axis=1) * _dot(qh, caug.astype(BF16))
            hh = nd[:, 0:ML_DH] / jnp.maximum(jnp.abs(nd[:, ML_DH:]), e_mt)
            mo = pml_ref[rows, 3 * ML_W + h * ML_DH:3 * ML_W + (h + 1) * ML_DH]
            o_ref[rows, hs] = (_sigmoid(mo) * hh).astype(o_ref.dtype)
            ks = (kh.astype(F32) * w_state).astype(BF16)
            dec = w_inter[c - 1:c, :]
            c_scr[h] = jnp.concatenate([dec, dec], axis=1) * caug + _dot_tn(ks, vaug)
        return carry

    for ci in range(t // c):
        chunk(ci, 0)

    @pl.when(step == pl.num_programs(1) - 1)
    def _():
        cout_ref[...] = c_scr[...]
        mout_ref[...] = m_scr[...]
        cvout_ref[...] = pml_ref[t - hist:t, 0:2 * ML_W]


def _mlstm_call(pml, pif, caug0, m0, cv0, cw, cb, bif, c, t):
    b, l, _ = pml.shape
    has_state = caug0 is not None
    tri = jnp.asarray(np.tril(np.ones((c, c), np.float32)), BF16)
    sel = np.zeros((ML_HEADS, LANES, 3 * ML_DH), np.float32)
    for h in range(ML_HEADS):
        for slot in range(9):
            sel[h, ML_HEADS * slot + h, (slot // 3) * ML_DH:(slot // 3 + 1) * ML_DH] = 1.0
    sel = jnp.asarray(sel, BF16)
    tile = lambda w: pl.BlockSpec((None, t, w), lambda bi, i: (bi, i, 0))
    c_spec = pl.BlockSpec((None, ML_HEADS, ML_DH, 2 * ML_DH), lambda bi, i: (bi, 0, 0, 0))
    m_spec = pl.BlockSpec((None, 8, LANES), lambda bi, i: (bi, 0, 0))
    cv_spec = pl.BlockSpec((None, CONV_W - 1, 2 * ML_W), lambda bi, i: (bi, 0, 0))
    in_specs = [tile(W_ML), tile(LANES)]
    args = [pml, pif]
    if has_state:
        in_specs += [c_spec, m_spec, cv_spec]
        args += [caug0, m0, cv0]
    consts = [cw, cb, bif, tri, sel]
    in_specs += [_full_spec(x.shape) for x in consts]
    return pl.pallas_call(
        functools.partial(_mlstm_kernel, c=c, t=t, has_state=has_state),
        grid=(b, l // t),
        in_specs=in_specs,
        out_specs=[tile(ML_W), c_spec, m_spec, cv_spec],
        out_shape=[jax.ShapeDtypeStruct((b, l, ML_W), BF16),
                   jax.ShapeDtypeStruct((b, ML_HEADS, ML_DH, 2 * ML_DH), F32),
                   jax.ShapeDtypeStruct((b, 8, LANES), F32),
                   jax.ShapeDtypeStruct((b, CONV_W - 1, 2 * ML_W), F32)],
        scratch_shapes=[pltpu.VMEM((ML_HEADS, ML_DH, 2 * ML_DH), F32),
                        pltpu.VMEM((8, LANES), F32),
                        pltpu.VMEM((8 + t, 2 * ML_W), F32),
                        pltpu.VMEM((t, 2 * ML_W), BF16)],
        compiler_params=pltpu.CompilerParams(
            dimension_semantics=("arbitrary", "arbitrary"), vmem_limit_bytes=VMEM_LIMIT),
        name="mlstm",
    )(*args, *consts)


def _merge_kernel(*refs, sparse):
    if sparse:
        (x_ref, ga_ref, hb_ref, pmg_ref, wug_ref, wum_ref, wo_ref, g2_ref,
         wr_hi_ref, wr_lo_ref, br_ref, tril_ref, x1_ref, hm_ref, meta_ref, cnt_ref, cnt_scr) = refs
    else:
        (x_ref, ga_ref, hb_ref, pmg_ref, wug_ref, wum_ref, wo_ref, g2_ref,
         wr_hi_ref, wr_lo_ref, br_ref, x1_ref, hm_ref, comb_ref) = refs
    ya = _dot(ga_ref[...], wug_ref[...])
    yb = _dot(hb_ref[...], wum_ref[...])
    z = _sigmoid(pmg_ref[:, 0:D_MODEL]) * ya + _sigmoid(pmg_ref[:, D_MODEL:]) * yb
    x1 = x_ref[...] + _dot(z.astype(BF16), wo_ref[...])
    x1_ref[...] = x1
    hm = _rms(x1, g2_ref[...])
    hm_hi = hm.astype(BF16)
    if sparse:
        _store_row_tiles(hm_ref, hm)
    else:
        hm_ref[...] = hm_hi
    hm_lo = (hm - hm_hi.astype(F32)).astype(BF16)
    logits = (_dot(hm_hi, wr_hi_ref[...]) + _dot(hm_lo, wr_hi_ref[...])
              + _dot(hm_hi, wr_lo_ref[...]) + br_ref[...])
    lane = lax.broadcasted_iota(jnp.int32, logits.shape, 1)
    neg = -jnp.inf
    is_g = (lane >= N_EXPERTS) & (lane < N_EXPERTS + N_GROUPS)
    lg = jnp.where(is_g, logits, neg)
    mg = jnp.max(lg, axis=-1, keepdims=True)
    p_top = 1.0 / jnp.sum(jnp.exp(lg - mg), axis=-1, keepdims=True)
    gi = jnp.min(jnp.where(lg == mg, lane, 2 * LANES), axis=-1, keepdims=True) - N_EXPERTS
    group_shift = int(math.log2(EXPERTS_PER_GROUP))
    sel = (lane < N_EXPERTS) & (jnp.right_shift(lane, group_shift) == gi)
    le = jnp.where(sel, logits, neg)
    v1 = jnp.max(le, axis=-1, keepdims=True)
    i1 = jnp.min(jnp.where(le == v1, lane, 2 * LANES), axis=-1, keepdims=True)
    le2 = jnp.where(lane == i1, neg, le)
    v2 = jnp.max(le2, axis=-1, keepdims=True)
    i2 = jnp.min(jnp.where(le2 == v2, lane, 2 * LANES), axis=-1, keepdims=True)
    e2 = jnp.exp(v2 - v1)
    wa = 1.0 / (1.0 + e2)
    wb = e2 / (1.0 + e2)
    if not sparse:
        comb_ref[...] = p_top * (jnp.where(lane == i1, wa, 0.0) + jnp.where(lane == i2, wb, 0.0))
        return

    @pl.when(pl.program_id(0) == 0)
    def _():
        cnt_scr[...] = jnp.zeros_like(cnt_scr)

    oh1 = lane == i1
    oh2 = lane == i2
    both = jnp.where(oh1 | oh2, 1.0, 0.0)
    before = _dot(tril_ref[...], both.astype(BF16)) + cnt_scr[...]
    r1 = jnp.sum(jnp.where(oh1, before, 0.0), axis=-1, keepdims=True)
    r2 = jnp.sum(jnp.where(oh2, before, 0.0), axis=-1, keepdims=True)
    cols = (i1.astype(F32), i2.astype(F32), r1, r2, p_top * wa, p_top * wb)
    meta = jnp.zeros(logits.shape, F32)
    for ci, col in enumerate(cols):
        meta = jnp.where(lane == ci, col, meta)
    meta_ref[...] = meta
    cnt_scr[...] += jnp.sum(both, axis=0, keepdims=True)
    cnt_ref[...] = cnt_scr[...]


def _merge_call(x2, ga, hb, pmg, wug, wum, wo, g2, wr_hi, wr_lo, br, tm, sparse):
    n = x2.shape[0]
    tile = lambda w: pl.BlockSpec((tm, w), lambda i: (i, 0))
    consts = [wug, wum, wo, g2, wr_hi, wr_lo, br]
    out_specs = [tile(D_MODEL), tile(D_MODEL), tile(LANES)]
    out_shape = [jax.ShapeDtypeStruct((n, D_MODEL), F32),
                 jax.ShapeDtypeStruct((n, D_MODEL), BF16),
                 jax.ShapeDtypeStruct((n, LANES), F32)]
    scratch = []
    if sparse:
        out_specs[1] = pl.BlockSpec((tm * ROW_TILE, LANES), lambda i: (i, 0))
        out_shape[1] = jax.ShapeDtypeStruct((n * ROW_TILE, LANES), F32)
        consts.append(jnp.asarray(np.tril(np.ones((tm, tm), np.float32), -1), BF16))
        out_specs.append(_full_spec((1, LANES)))
        out_shape.append(jax.ShapeDtypeStruct((1, LANES), F32))
        scratch.append(pltpu.VMEM((1, LANES), F32))
    return pl.pallas_call(
        functools.partial(_merge_kernel, sparse=sparse),
        grid=(n // tm,),
        in_specs=[tile(D_MODEL), tile(GLA_V), tile(ML_W), tile(W_MG)]
                 + [_full_spec(x.shape) for x in consts],
        out_specs=out_specs,
        out_shape=out_shape,
        scratch_shapes=scratch,
        compiler_params=pltpu.CompilerParams(
            dimension_semantics=("arbitrary",), vmem_limit_bytes=VMEM_LIMIT),
        name="merge",
    )(x2, ga, hb, pmg, *consts)


MOE_ROWS = 256
ROW_TILE = D_MODEL // LANES


def _store_row_tiles(ref, x):
    r = x.shape[0]
    for k in range(ROW_TILE):
        ref[pl.ds(k, r, stride=ROW_TILE), :] = x[:, k * LANES:(k + 1) * LANES]


def _load_row_tiles(ref, r, k):
    return ref[pl.ds(k, r, stride=ROW_TILE), :]


def _row_copy(src, i, dst, j, sem):
    src_row = src.at[pl.ds(pl.multiple_of(i * ROW_TILE, ROW_TILE), ROW_TILE), :]
    dst_row = dst.at[pl.ds(pl.multiple_of(j * ROW_TILE, ROW_TILE), ROW_TILE), :]
    return pltpu.make_async_copy(src_row, dst_row, sem)


def _dispatch_kernel(pos0_ref, pos1_ref, tail_ref, hm_ref, xs_ref, zero_ref, sem, zsem, *, tm):
    step = pl.program_id(0)

    @pl.when(step == 0)
    def _():
        zero_ref[...] = jnp.zeros_like(zero_ref)
        n_tiles = xs_ref.shape[0] // (MOE_ROWS * ROW_TILE)

        def clear(row):
            start = pl.multiple_of(row * ROW_TILE, MOE_ROWS * ROW_TILE)
            return pltpu.make_async_copy(zero_ref, xs_ref.at[pl.ds(start, MOE_ROWS * ROW_TILE), :], zsem)

        def unused(fn):
            def body(j, carry):
                fn(clear(j * MOE_ROWS))
                return carry
            lax.fori_loop(tail_ref[2 * N_EXPERTS], n_tiles, body, 0)

        for e in range(N_EXPERTS):
            @pl.when(tail_ref[N_EXPERTS + e] > 0)
            def _():
                clear(tail_ref[e]).start()
        unused(lambda c: c.start())
        for e in range(N_EXPERTS):
            @pl.when(tail_ref[N_EXPERTS + e] > 0)
            def _():
                clear(tail_ref[e]).wait()
        unused(lambda c: c.wait())

    base = step * tm

    def issue(t, carry):
        for pos_ref in (pos0_ref, pos1_ref):
            _row_copy(hm_ref, t, xs_ref, pos_ref[base + t], sem).start()
        return carry

    def drain(t, carry):
        for pos_ref in (pos0_ref, pos1_ref):
            _row_copy(hm_ref, t, xs_ref, pos_ref[base + t], sem).wait()
        return carry

    lax.fori_loop(0, tm, issue, 0, unroll=8)
    lax.fori_loop(0, tm, drain, 0, unroll=8)


def _dispatch_call(pos0, pos1, tail, hm, n_rows, tm):
    n = hm.shape[0] // ROW_TILE
    return pl.pallas_call(
        functools.partial(_dispatch_kernel, tm=tm),
        grid_spec=pltpu.PrefetchScalarGridSpec(
            num_scalar_prefetch=3,
            grid=(n // tm,),
            in_specs=[pl.BlockSpec((tm * ROW_TILE, LANES), lambda i, *_: (i, 0))],
            out_specs=pl.BlockSpec(memory_space=pl.ANY),
            scratch_shapes=[pltpu.VMEM((MOE_ROWS * ROW_TILE, LANES), F32),
                            pltpu.SemaphoreType.DMA, pltpu.SemaphoreType.DMA],
        ),
        out_shape=jax.ShapeDtypeStruct((n_rows * ROW_TILE, LANES), F32),
        compiler_params=pltpu.CompilerParams(
            dimension_semantics=("arbitrary",), vmem_limit_bytes=VMEM_LIMIT),
        name="moe_dispatch",
    )(pos0, pos1, tail, hm)


def _gmm_kernel(te_ref, nv_ref, xs_ref, wg_ref, wu_ref, wd_ref, o_ref):
    used = pl.program_id(0) < nv_ref[0]

    @pl.when(used)
    def _():
        x = jnp.concatenate([_load_row_tiles(xs_ref, MOE_ROWS, k).astype(BF16) for k in range(ROW_TILE)], axis=1)
        a = _dot(x, wg_ref[...].astype(BF16))
        u = _dot(x, wu_ref[...].astype(BF16))
        hh = (a * _sigmoid(a)) * u
        _store_row_tiles(o_ref, _dot(hh.astype(BF16), wd_ref[...].astype(BF16)))

    @pl.when(jnp.logical_not(used))
    def _():
        o_ref[...] = jnp.zeros_like(o_ref)


def _gmm_call(tile_expert, n_valid, xs, wg, wu, wd):
    n_tiles = xs.shape[0] // (MOE_ROWS * ROW_TILE)
    rows = lambda j, te, nv: (jnp.minimum(j, nv[0] - 1), 0)
    wsel = lambda j, te, nv: (te[j], 0, 0)
    return pl.pallas_call(
        _gmm_kernel,
        grid_spec=pltpu.PrefetchScalarGridSpec(
            num_scalar_prefetch=2,
            grid=(n_tiles,),
            in_specs=[pl.BlockSpec((MOE_ROWS * ROW_TILE, LANES), rows),
                      pl.BlockSpec((None, D_MODEL, D_EXPERT), wsel),
                      pl.BlockSpec((None, D_MODEL, D_EXPERT), wsel),
                      pl.BlockSpec((None, D_EXPERT, D_MODEL), wsel)],
            out_specs=pl.BlockSpec((MOE_ROWS * ROW_TILE, LANES), lambda j, te, nv: (j, 0)),
        ),
        out_shape=jax.ShapeDtypeStruct(xs.shape, F32),
        compiler_params=pltpu.CompilerParams(
            dimension_semantics=("arbitrary",), vmem_limit_bytes=VMEM_LIMIT),
        name="moe_grouped",
    )(tile_expert, n_valid, xs, wg, wu, wd)


def _combine_kernel(pos0_ref, pos1_ref, x1_ref, meta_ref, gf_ref, os_ref, y_ref, buf_ref, sem, *, tm):
    base = pl.program_id(0) * tm

    def issue(t, carry):
        for slot, pos_ref in enumerate((pos0_ref, pos1_ref)):
            _row_copy(os_ref, pos_ref[base + t], buf_ref.at[slot], t, sem).start()
        return carry

    def drain(t, carry):
        for slot, pos_ref in enumerate((pos0_ref, pos1_ref)):
            _row_copy(os_ref, pos_ref[base + t], buf_ref.at[slot], t, sem).wait()
        return carry

    lax.fori_loop(0, tm, issue, 0, unroll=8)
    lax.fori_loop(0, tm, drain, 0, unroll=8)
    w0 = jnp.broadcast_to(meta_ref[:, 4:5], (tm, LANES))
    w1 = jnp.broadcast_to(meta_ref[:, 5:6], (tm, LANES))
    moe = jnp.concatenate(
        [w0 * _load_row_tiles(buf_ref.at[0], tm, k) + w1 * _load_row_tiles(buf_ref.at[1], tm, k)
         for k in range(ROW_TILE)], axis=1)
    y_ref[...] = _rms(x1_ref[...] + moe, gf_ref[...])


def _combine_call(pos0, pos1, x1, meta, gf, out_sorted, tm):
    n = x1.shape[0]
    tile = lambda w: pl.BlockSpec((tm, w), lambda i, *_: (i, 0))
    return pl.pallas_call(
        functools.partial(_combine_kernel, tm=tm),
        grid_spec=pltpu.PrefetchScalarGridSpec(
            num_scalar_prefetch=2,
            grid=(n // tm,),
            in_specs=[tile(D_MODEL), tile(LANES), pl.BlockSpec(gf.shape, lambda i, *_: (0, 0)),
                      pl.BlockSpec(memory_space=pl.ANY)],
            out_specs=tile(D_MODEL),
            scratch_shapes=[pltpu.VMEM((2, tm * ROW_TILE, LANES), F32), pltpu.SemaphoreType.DMA],
        ),
        out_shape=jax.ShapeDtypeStruct((n, D_MODEL), F32),
        compiler_params=pltpu.CompilerParams(
            dimension_semantics=("arbitrary",), vmem_limit_bytes=VMEM_LIMIT),
        name="moe_combine",
    )(pos0, pos1, x1, meta, gf, out_sorted)


def _sparse_moe(x1, hm, meta, counts, p, tm):
    n = x1.shape[0]
    n_tiles = (2 * n) // MOE_ROWS + N_EXPERTS
    cnt = counts[0, :N_EXPERTS].astype(jnp.int32)
    tiles = (cnt + (MOE_ROWS - 1)) // MOE_ROWS
    tile_end = jnp.cumsum(tiles)
    tile_start = tile_end - tiles
    n_valid = tile_end[-1:]
    experts = jnp.arange(N_EXPERTS, dtype=jnp.int32)

    def positions(slot):
        first = jnp.sum(jnp.where(meta[:, slot, None].astype(jnp.int32) == experts, tile_start * MOE_ROWS, 0), axis=1)
        return first + meta[:, 2 + slot].astype(jnp.int32)

    pos0, pos1 = positions(0), positions(1)
    j = jnp.minimum(jnp.arange(n_tiles, dtype=jnp.int32), n_valid - 1)
    tile_expert = jnp.sum((j[:, None] >= tile_end[None, :]).astype(jnp.int32), axis=1)
    tail = jnp.concatenate([(tile_end - 1) * MOE_ROWS, cnt, n_valid]).astype(jnp.int32)
    xs = _dispatch_call(pos0, pos1, tail, hm, n_tiles * MOE_ROWS, tm)
    out_sorted = _gmm_call(tile_expert, n_valid.astype(jnp.int32), xs, p["wg"], p["wu"], p["wd"])
    return _combine_call(pos0, pos1, x1, meta, p["gf"], out_sorted, tm)


def _moe_kernel(x1_ref, hm_ref, comb_ref, wg_ref, wu_ref, wd_ref, gf_ref, y_ref, acc_ref):
    e = pl.program_id(1)

    @pl.when(e == 0)
    def _():
        acc_ref[...] = jnp.zeros_like(acc_ref)

    hm = hm_ref[...]
    a = _dot(hm, wg_ref[...].astype(BF16))
    u = _dot(hm, wu_ref[...].astype(BF16))
    lane = lax.broadcasted_iota(jnp.int32, comb_ref.shape, 1)
    ce = jnp.sum(jnp.where(lane == e, comb_ref[...], 0.0), axis=-1, keepdims=True)
    hh = (a * _sigmoid(a)) * u * ce
    acc_ref[...] += _dot(hh.astype(BF16), wd_ref[...].astype(BF16))

    @pl.when(e == pl.num_programs(1) - 1)
    def _():
        y_ref[...] = _rms(x1_ref[...] + acc_ref[...], gf_ref[...])


def _moe_call(x1, hm, comb, wg, wu, wd, gf, tm):
    n = x1.shape[0]
    tile = lambda w: pl.BlockSpec((tm, w), lambda i, e: (i, 0))
    return pl.pallas_call(
        _moe_kernel,
        grid=(n // tm, N_EXPERTS),
        in_specs=[tile(D_MODEL), tile(D_MODEL), tile(LANES),
                  pl.BlockSpec((None, D_MODEL, D_EXPERT), lambda i, e: (e, 0, 0)),
                  pl.BlockSpec((None, D_MODEL, D_EXPERT), lambda i, e: (e, 0, 0)),
                  pl.BlockSpec((None, D_EXPERT, D_MODEL), lambda i, e: (e, 0, 0)),
                  _full_spec(gf.shape)],
        out_specs=tile(D_MODEL),
        out_shape=jax.ShapeDtypeStruct((n, D_MODEL), F32),
        scratch_shapes=[pltpu.VMEM((tm, D_MODEL), F32)],
        compiler_params=pltpu.CompilerParams(
            dimension_semantics=("arbitrary", "arbitrary"), vmem_limit_bytes=VMEM_LIMIT),
        name="moe",
    )(x1, hm, comb, wg, wu, wd, gf)


def _pad_cols(w, width):
    return jnp.pad(w, ((0, 0), (0, width - w.shape[1])))


def _prep_weights(norm1_g, w_in, gla_w_gate2, gla_b_gate, gla_norm_g, w_up_gla,
                  ml_conv_w, ml_conv_b, ml_b_i, ml_b_f, w_up_ml, w_out,
                  norm2_g, router_g_w, router_g_b, router_e_w, router_e_b,
                  moe_w_gate, moe_w_up, moe_w_down, final_g):
    o_lr = W_GLA
    o_ml = o_lr + GLA_GATE_RANK
    o_if = o_ml + W_ML
    o_mg = o_if + 2 * ML_HEADS
    w_p = jnp.concatenate([
        w_in[:, 0:o_lr], _pad_cols(w_in[:, o_lr:o_ml], LANES),
        w_in[:, o_ml:o_if], _pad_cols(w_in[:, o_if:o_mg], LANES),
        w_in[:, o_mg:]], axis=1).astype(BF16)
    wr = _pad_cols(jnp.concatenate([router_e_w, router_g_w], axis=1), LANES)
    wr_hi = wr.astype(BF16)
    wr_lo = (wr - wr_hi.astype(F32)).astype(BF16)
    br = _pad_cols(jnp.concatenate([router_e_b, router_g_b])[None, :], LANES)
    wg2 = jnp.pad(gla_w_gate2, ((0, LANES - GLA_GATE_RANK), (0, 0)))
    wg2_hi = wg2.astype(BF16)
    return dict(
        g1=norm1_g[None, :], w_p=w_p,
        wg2_p=jnp.stack([wg2_hi, (wg2 - wg2_hi.astype(F32)).astype(BF16)]),
        bg=gla_b_gate[None, :], gn=gla_norm_g[None, :],
        wug=w_up_gla.astype(BF16),
        cw=ml_conv_w, cb=ml_conv_b[None, :],
        bif=_pad_cols(jnp.concatenate([ml_b_i, ml_b_f])[None, :], LANES),
        wum=w_up_ml.astype(BF16), wo=w_out.astype(BF16),
        g2=norm2_g[None, :], wr_hi=wr_hi, wr_lo=wr_lo, br=br,
        wg=moe_w_gate.reshape(N_EXPERTS, D_MODEL, D_EXPERT),
        wu=moe_w_up.reshape(N_EXPERTS, D_MODEL, D_EXPERT),
        wd=moe_w_down.reshape(N_EXPERTS, D_EXPERT, D_MODEL),
        gf=final_g[None, :],
    )


def _layer(x, gla_s0, ml_c0, ml_n0, ml_m0, conv0, p, *, chunk, seq_tile, row_tile, moe_tile, sparse):
    b, l, _ = x.shape
    n = b * l
    x2 = x.reshape(n, D_MODEL)
    pg, plr, pml, pif, pmg = _proj_call(x2, p["g1"], p["w_p"], row_tile)
    r3 = lambda a: a.reshape(b, l, a.shape[-1])
    s0 = None if gla_s0 is None else gla_s0.reshape(b, GLA_QK, GLA_DV)
    ga, gla_s = _gla_call(r3(pg), r3(plr), s0, p["wg2_p"], p["bg"], p["gn"], chunk, seq_tile)
    gla_s = gla_s.reshape(b, GLA_HEADS, GLA_DK, GLA_DV)
    if ml_c0 is None:
        caug0 = m0 = None
    else:
        caug0 = jnp.concatenate([ml_c0, jnp.broadcast_to(ml_n0[..., None], ml_c0.shape)], axis=-1)
        m0 = jnp.broadcast_to(_pad_cols(ml_m0, LANES)[:, None, :], (b, 8, LANES))
    hb, caug, m_b, new_conv = _mlstm_call(r3(pml), r3(pif), caug0, m0, conv0,
                                          p["cw"], p["cb"], p["bif"], chunk, seq_tile)
    x1, hm, route, *counts = _merge_call(x2, ga.reshape(n, GLA_V), hb.reshape(n, ML_W), pmg,
                                         p["wug"], p["wum"], p["wo"], p["g2"],
                                         p["wr_hi"], p["wr_lo"], p["br"], row_tile, sparse)
    if sparse:
        y = _sparse_moe(x1, hm, route, counts[0], p, moe_tile)
    else:
        y = _moe_call(x1, hm, route, p["wg"], p["wu"], p["wd"], p["gf"], moe_tile)
    return (y.reshape(b, l, D_MODEL), gla_s[None], caug[..., 0:ML_DH][None],
            caug[..., ML_DH][None], m_b[:, 0, 0:ML_HEADS][None], new_conv[None])


def kernel(x_prompt, x_sample, state_gla_S, state_mlstm_C, state_mlstm_n, state_mlstm_m, state_mlstm_conv, norm1_g, w_in, gla_w_gate2, gla_b_gate, gla_norm_g, w_up_gla, ml_conv_w, ml_conv_b, ml_b_i, ml_b_f, w_up_ml, w_out, norm2_g, router_g_w, router_g_b, router_e_w, router_e_b, moe_w_gate, moe_w_up, moe_w_down, final_g):
    assert norm1_g.shape[0] == 1, "single-layer trunk"
    p = _prep_weights(norm1_g[0], w_in[0], gla_w_gate2[0], gla_b_gate[0], gla_norm_g[0], w_up_gla[0],
                      ml_conv_w[0], ml_conv_b[0], ml_b_i[0], ml_b_f[0], w_up_ml[0], w_out[0],
                      norm2_g[0], router_g_w[0], router_g_b[0], router_e_w[0], router_e_b[0],
                      moe_w_gate[0], moe_w_up[0], moe_w_down[0], final_g)
    yp, *sp = _layer(x_prompt, None, None, None, None, None, p,
                     chunk=128, seq_tile=512, row_tile=256, moe_tile=256, sparse=True)
    dec_seq = x_sample.shape[1]
    ns = x_sample.shape[0] * dec_seq
    ys, *ss = _layer(x_sample, state_gla_S[0], state_mlstm_C[0], state_mlstm_n[0], state_mlstm_m[0],
                     state_mlstm_conv[0], p,
                     chunk=dec_seq, seq_tile=dec_seq, row_tile=ns, moe_tile=ns, sparse=False)
    return (yp, ys, *sp, *ss)
```

```python
import functools
import math

import numpy as np
import jax
import jax.numpy as jnp
from jax import lax
from jax.experimental import pallas as pl
from jax.experimental.pallas import tpu as pltpu

D_MODEL = 1024
GLA_HEADS = 4
GLA_DK = 64
GLA_DV = 128
GLA_GATE_RANK = 16
GLA_TAU = 16.0
ML_HEADS = 4
ML_DH = 128
CONV_W = 4
N_GROUPS = 4
EXPERTS_PER_GROUP = 8
N_EXPERTS = N_GROUPS * EXPERTS_PER_GROUP
D_EXPERT = 256
EPS = 1e-6

GLA_QK = GLA_HEADS * GLA_DK
GLA_V = GLA_HEADS * GLA_DV
ML_W = ML_HEADS * ML_DH

LANES = 128
VMEM_LIMIT = 56 * 1024 * 1024

W_GLA = 2 * GLA_QK + 2 * GLA_V
W_ML = 2 * ML_W + ML_W + ML_W
W_MG = 2 * D_MODEL
PROJ_WIDTHS = (W_GLA, LANES, W_ML, LANES, W_MG)

F32 = jnp.float32
BF16 = jnp.bfloat16


def _dot(a, b):
    return jnp.dot(a, b, preferred_element_type=F32)


def _dot_nt(a, b):
    return lax.dot_general(a, b, (((1,), (1,)), ((), ())), preferred_element_type=F32)


def _dot_tn(a, b):
    return lax.dot_general(a, b, (((0,), (0,)), ((), ())), preferred_element_type=F32)


def _split3(x):
    hi = x.astype(BF16)
    r1 = x - hi.astype(F32)
    mid = r1.astype(BF16)
    lo = (r1 - mid.astype(F32)).astype(BF16)
    return hi, mid, lo


def _dot_exact_lhs(m, x):
    hi, mid, lo = _split3(x)
    return _dot(m, hi) + _dot(m, mid) + _dot(m, lo)


def _log_sigmoid(z):
    return jnp.minimum(z, 0.0) - jnp.log1p(jnp.exp(-jnp.abs(z)))


def _sigmoid(z):
    return 1.0 / (1.0 + jnp.exp(-z))


def _rms(x, g):
    return x * lax.rsqrt(jnp.mean(x * x, axis=-1, keepdims=True) + EPS) * g


def _full_spec(shape):
    nd = len(shape)
    return pl.BlockSpec(shape, lambda *_: (0,) * nd)


def _proj_kernel(x_ref, g_ref, w_ref, *o_refs):
    h = _rms(x_ref[...], g_ref[...]).astype(BF16)
    off = 0
    for o_ref, width in zip(o_refs, PROJ_WIDTHS):
        o_ref[...] = _dot(h, w_ref[:, off:off + width])
        off += width


def _proj_call(x2, g, w_p, tm):
    n = x2.shape[0]
    return pl.pallas_call(
        _proj_kernel,
        grid=(n // tm,),
        in_specs=[pl.BlockSpec((tm, D_MODEL), lambda i: (i, 0)),
                  _full_spec(g.shape), _full_spec(w_p.shape)],
        out_specs=[pl.BlockSpec((tm, w), lambda i: (i, 0)) for w in PROJ_WIDTHS],
        out_shape=[jax.ShapeDtypeStruct((n, w), F32) for w in PROJ_WIDTHS],
        compiler_params=pltpu.CompilerParams(
            dimension_semantics=("arbitrary",), vmem_limit_bytes=VMEM_LIMIT),
        name="in_proj",
    )(x2, g, w_p)


def _gla_consts(c):
    nlev = int(math.log2(c))
    assert 1 << nlev == c
    t = np.arange(c)[:, None]
    j = np.arange(c)[None, :]
    lv = np.full((c, c), -1, np.int32)
    for l in range(nlev):
        h = c >> (l + 1)
        upper = (t % (2 * h)) >= h
        same = (j // (2 * h)) == (t // (2 * h))
        s_lower = (j % (2 * h)) < h
        lv[np.broadcast_to(upper, (c, c)) & same & s_lower] = l
    lv[np.eye(c, dtype=bool)] = nlev
    tri = (j <= t).astype(np.float32)
    return jnp.asarray(tri, BF16), jnp.asarray(np.concatenate([lv, lv], axis=1))


def _gla_kernel(*refs, c, t, has_state):
    if has_state:
        (pg_ref, plr_ref, s0_ref, wg2_ref, bg_ref, gn_ref, tri_ref, lv_ref,
         o_ref, sout_ref, s_scr) = refs
    else:
        (pg_ref, plr_ref, wg2_ref, bg_ref, gn_ref, tri_ref, lv_ref,
         o_ref, sout_ref, s_scr) = refs
    nlev = int(math.log2(c))
    step = pl.program_id(1)

    @pl.when(step == 0)
    def _():
        if has_state:
            s_scr[...] = s0_ref[...]
        else:
            s_scr[...] = jnp.zeros_like(s_scr)

    lane_k = lax.broadcasted_iota(jnp.int32, (c, GLA_QK), 1)
    first_of_pair = (lane_k % (2 * GLA_DK)) < GLA_DK
    row_k = lax.broadcasted_iota(jnp.int32, (GLA_QK, GLA_DV), 0)
    row_t = lax.broadcasted_iota(jnp.int32, (c, GLA_QK), 0)

    def block_ref(b, blk, idx):
        if blk >= 8:
            b3 = b.reshape(c // blk, blk, GLA_QK)
            return jnp.broadcast_to(b3[:, idx:idx + 1, :], b3.shape).reshape(c, GLA_QK)
        r = row_t % blk
        out = b
        for sh in range(-idx, blk - idx):
            if sh != 0:
                out = jnp.where(r - idx == sh, pltpu.roll(b, sh % c, axis=0), out)
        return out

    def chunk(ci, carry):
        r0 = ci * c
        rows = pl.ds(r0, c)
        q = pg_ref[rows, 0:GLA_QK] * (GLA_DK ** -0.5)
        k = pg_ref[rows, GLA_QK:2 * GLA_QK]
        v = pg_ref[rows, 2 * GLA_QK:2 * GLA_QK + GLA_V].astype(BF16)
        glr = plr_ref[rows, :]
        g_hi = glr.astype(BF16)
        g_lo = (glr - g_hi.astype(F32)).astype(BF16)
        z = (_dot(g_hi, wg2_ref[0]) + _dot(g_lo, wg2_ref[0]) + _dot(g_hi, wg2_ref[1])) + bg_ref[...]
        la = _log_sigmoid(z) * (1.0 / GLA_TAU)
        b = _dot_exact_lhs(tri_ref[...], la)
        b_last = b[c - 1:c, :]
        qe = (q * jnp.exp(b)).astype(BF16)
        kl = (k * jnp.exp(b_last - b)).astype(BF16)
        dcol = jnp.exp(jnp.broadcast_to(b_last, (LANES, GLA_QK)).T)

        k_a = jnp.where(first_of_pair, k, 0.0)
        k_b = k - k_a
        lv2 = lv_ref[...]
        a = [jnp.zeros((c, 2 * c), F32) for _ in range(GLA_HEADS // 2)]
        for l in range(nlev + 1):
            if l < nlev:
                half = c >> (l + 1)
                d = b - block_ref(b, 2 * half, half - 1)
                e = jnp.exp(jnp.minimum(d, -d))
                qt, kta, ktb = q * e, k_a * e, k_b * e
            else:
                qt, kta, ktb = q, k_a, k_b
            qt, kta, ktb = qt.astype(BF16), kta.astype(BF16), ktb.astype(BF16)
            for pr in range(GLA_HEADS // 2):
                ls = slice(pr * 2 * GLA_DK, (pr + 1) * 2 * GLA_DK)
                rhs = jnp.concatenate([kta[:, ls], ktb[:, ls]], axis=0)
                a[pr] = jnp.where(lv2 == l, _dot_nt(qt[:, ls], rhs), a[pr])

        s_all = s_scr[...]
        s_bd = jnp.concatenate(
            [jnp.where((row_k // GLA_DK) == h, s_all, 0.0).astype(BF16) for h in range(GLA_HEADS)], axis=1)
        o_inter = _dot(qe, s_bd)
        u_all = _dot_tn(kl, v)
        for h in range(GLA_HEADS):
            vs = slice(h * GLA_DV, (h + 1) * GLA_DV)
            ks = slice(h * GLA_DK, (h + 1) * GLA_DK)
            a_h = a[h // 2][:, (h % 2) * c:(h % 2 + 1) * c]
            o = _dot(a_h.astype(BF16), v[:, vs]) + o_inter[:, vs]
            on = _rms(o, gn_ref[:, vs])
            r = pg_ref[rows, 2 * GLA_QK + GLA_V + h * GLA_DV:2 * GLA_QK + GLA_V + (h + 1) * GLA_DV]
            o_ref[rows, vs] = (on * (r * _sigmoid(r))).astype(o_ref.dtype)
            s_scr[ks, :] = dcol[ks, :] * s_all[ks, :] + u_all[ks, vs]
        return carry

    for ci in range(t // c):
        chunk(ci, 0)

    @pl.when(step == pl.num_programs(1) - 1)
    def _():
        sout_ref[...] = s_scr[...]


def _gla_call(pg, plr, s0, wg2_p, bg, gn, c, t):
    b, l, _ = pg.shape
    tri, lv2 = _gla_consts(c)
    has_state = s0 is not None
    tile = lambda w: pl.BlockSpec((None, t, w), lambda bi, i: (bi, i, 0))
    state_spec = pl.BlockSpec((None, GLA_QK, GLA_DV), lambda bi, i: (bi, 0, 0))
    in_specs = [tile(W_GLA), tile(LANES)]
    args = [pg, plr]
    if has_state:
        in_specs.append(state_spec)
        args.append(s0)
    consts = [wg2_p, bg, gn, tri, lv2]
    in_specs += [_full_spec(x.shape) for x in consts]
    return pl.pallas_call(
        functools.partial(_gla_kernel, c=c, t=t, has_state=has_state),
        grid=(b, l // t),
        in_specs=in_specs,
        out_specs=[tile(GLA_V), state_spec],
        out_shape=[jax.ShapeDtypeStruct((b, l, GLA_V), BF16),
                   jax.ShapeDtypeStruct((b, GLA_QK, GLA_DV), F32)],
        scratch_shapes=[pltpu.VMEM((GLA_QK, GLA_DV), F32)],
        compiler_params=pltpu.CompilerParams(
            dimension_semantics=("arbitrary", "arbitrary"), vmem_limit_bytes=VMEM_LIMIT),
        name="gla",
    )(*args, *consts)


def _mlstm_kernel(*refs, c, t, has_state):
    if has_state:
        (pml_ref, pif_ref, c0_ref, m0_ref, cv0_ref, cw_ref, cb_ref, bif_ref, tri_ref, sel_ref,
         o_ref, cout_ref, mout_ref, cvout_ref, c_scr, m_scr, cv_scr, qk_scr) = refs
    else:
        (pml_ref, pif_ref, cw_ref, cb_ref, bif_ref, tri_ref, sel_ref,
         o_ref, cout_ref, mout_ref, cvout_ref, c_scr, m_scr, cv_scr, qk_scr) = refs
    step = pl.program_id(1)
    hist = CONV_W - 1
    pad = 8

    @pl.when(step == 0)
    def _():
        if has_state:
            c_scr[...] = c0_ref[...]
            m_scr[...] = m0_ref[...]
            cv_scr[0:pad - hist, :] = jnp.zeros((pad - hist, 2 * ML_W), F32)
            cv_scr[pad - hist:pad, :] = cv0_ref[...]
        else:
            c_scr[...] = jnp.zeros_like(c_scr)
            m_scr[...] = jnp.zeros_like(m_scr)
            cv_scr[0:pad, :] = jnp.zeros((pad, 2 * ML_W), F32)

    cv_scr[pad:pad + t, :] = pml_ref[:, 0:2 * ML_W]
    acc = cb_ref[...] + cv_scr[pad:pad + t, :] * cw_ref[CONV_W - 1:CONV_W, :]
    for d in range(1, CONV_W):
        acc = acc + cv_scr[pad - d:pad - d + t, :] * cw_ref[CONV_W - 1 - d:CONV_W - d, :]
    conv = acc * _sigmoid(acc)
    qk_scr[:, 0:ML_W] = conv[:, 0:ML_W].astype(BF16)
    qk_scr[:, ML_W:] = (conv[:, ML_W:] * (ML_DH ** -0.5)).astype(BF16)
    cv_scr[0:pad, :] = cv_scr[t:t + pad, :]

    gts = pif_ref[...] + bif_ref[...]
    flog = pltpu.roll(_log_sigmoid(gts), LANES - ML_HEADS, axis=1)

    lane = lax.broadcasted_iota(jnp.int32, (c, LANES), 1)
    row_c = lax.broadcasted_iota(jnp.int32, (c, LANES), 0)
    causal = (lax.broadcasted_iota(jnp.int32, (c, c), 1) <= lax.broadcasted_iota(jnp.int32, (c, c), 0))
    ones_v = jnp.ones((c, ML_DH), BF16)

    def slots(pieces):
        out = jnp.zeros((c, LANES), F32)
        for j, piece in enumerate(pieces):
            if not isinstance(piece, float) and j > 0:
                piece = pltpu.roll(piece, ML_HEADS * j, axis=1)
            out = jnp.where((lane >= ML_HEADS * j) & (lane < ML_HEADS * (j + 1)), piece, out)
        return out

    def split3f(x):
        return [p.astype(F32) for p in _split3(x)]

    def chunk(ci, carry):
        r0 = ci * c
        rows = pl.ds(r0, c)
        ip = gts[r0:r0 + c]
        bc = _dot_exact_lhs(tri_ref[...], flog[r0:r0 + c])
        w = ip - bc
        cm = w
        for j in range(int(math.log2(c))):
            sh = 1 << j
            cm = jnp.where(row_c >= sh, jnp.maximum(cm, pltpu.roll(cm, sh, axis=0)), cm)
        mprev = m_scr[0:1, :]
        g = jnp.maximum(mprev, cm)
        g_last = g[c - 1:c, :]
        m_scr[...] = jnp.broadcast_to(bc[c - 1:c, :] + g_last, m_scr.shape)
        lhs_all = slots([1.0, 1.0, 1.0] + split3f(g))
        rhs = slots(split3f(w) + [-1.0, -1.0, -1.0]).astype(BF16)
        y = slots(split3f(mprev - g) + split3f(-(bc + g)) + split3f(w - g_last)).astype(BF16)
        for h in range(ML_HEADS):
            hs = slice(h * ML_DH, (h + 1) * ML_DH)
            qh = qk_scr[rows, h * ML_DH:(h + 1) * ML_DH]
            kh = qk_scr[rows, ML_W + h * ML_DH:ML_W + (h + 1) * ML_DH]
            vaug = jnp.concatenate(
                [pml_ref[rows, 2 * ML_W + h * ML_DH:2 * ML_W + (h + 1) * ML_DH].astype(BF16), ones_v],
                axis=1)
            lhs = jnp.where(lane % ML_HEADS == h, lhs_all, 0.0).astype(BF16)
            p = jnp.exp(jnp.where(causal, _dot_nt(lhs, rhs), -jnp.inf))
            bx = jnp.exp(_dot(y, sel_ref[h]))
            w_inter = bx[:, 0:ML_DH]
            e_mt = bx[:, ML_DH:2 * ML_DH]
            w_state = bx[:, 2 * ML_DH:3 * ML_DH]
            s = (_dot_nt(qh, kh) * p).astype(BF16)
            caug = c_scr[h]
            nd = _dot(s, vaug) + jnp.concatenate([w_inter, w_inter], axis=1) * _dot(qh, caug.astype(BF16))
            hh = nd[:, 0:ML_DH] / jnp.maximum(jnp.abs(nd[:, ML_DH:]), e_mt)
            mo = pml_ref[rows, 3 * ML_W + h * ML_DH:3 * ML_W + (h + 1) * ML_DH]
            o_ref[rows, hs] = (_sigmoid(mo) * hh).astype(o_ref.dtype)
            ks = (kh.astype(F32) * w_state).astype(BF16)
            dec = w_inter[c - 1:c, :]
            c_scr[h] = jnp.concatenate([dec, dec], axis=1) * caug + _dot_tn(ks, vaug)
        return carry

    for ci in range(t // c):
        chunk(ci, 0)

    @pl.when(step == pl.num_programs(1) - 1)
    def _():
        cout_ref[...] = c_scr[...]
        mout_ref[...] = m_scr[...]
        cvout_ref[...] = pml_ref[t - hist:t, 0:2 * ML_W]


def _mlstm_call(pml, pif, caug0, m0, cv0, cw, cb, bif, c, t):
    b, l, _ = pml.shape
    has_state = caug0 is not None
    tri = jnp.asarray(np.tril(np.ones((c, c), np.float32)), BF16)
    sel = np.zeros((ML_HEADS, LANES, 3 * ML_DH), np.float32)
    for h in range(ML_HEADS):
        for slot in range(9):
            sel[h, ML_HEADS * slot + h, (slot // 3) * ML_DH:(slot // 3 + 1) * ML_DH] = 1.0
    sel = jnp.asarray(sel, BF16)
    tile = lambda w: pl.BlockSpec((None, t, w), lambda bi, i: (bi, i, 0))
    c_spec = pl.BlockSpec((None, ML_HEADS, ML_DH, 2 * ML_DH), lambda bi, i: (bi, 0, 0, 0))
    m_spec = pl.BlockSpec((None, 8, LANES), lambda bi, i: (bi, 0, 0))
    cv_spec = pl.BlockSpec((None, CONV_W - 1, 2 * ML_W), lambda bi, i: (bi, 0, 0))
    in_specs = [tile(W_ML), tile(LANES)]
    args = [pml, pif]
    if has_state:
        in_specs += [c_spec, m_spec, cv_spec]
        args += [caug0, m0, cv0]
    consts = [cw, cb, bif, tri, sel]
    in_specs += [_full_spec(x.shape) for x in consts]
    return pl.pallas_call(
        functools.partial(_mlstm_kernel, c=c, t=t, has_state=has_state),
        grid=(b, l // t),
        in_specs=in_specs,
        out_specs=[tile(ML_W), c_spec, m_spec, cv_spec],
        out_shape=[jax.ShapeDtypeStruct((b, l, ML_W), BF16),
                   jax.ShapeDtypeStruct((b, ML_HEADS, ML_DH, 2 * ML_DH), F32),
                   jax.ShapeDtypeStruct((b, 8, LANES), F32),
                   jax.ShapeDtypeStruct((b, CONV_W - 1, 2 * ML_W), F32)],
        scratch_shapes=[pltpu.VMEM((ML_HEADS, ML_DH, 2 * ML_DH), F32),
                        pltpu.VMEM((8, LANES), F32),
                        pltpu.VMEM((8 + t, 2 * ML_W), F32),
                        pltpu.VMEM((t, 2 * ML_W), BF16)],
        compiler_params=pltpu.CompilerParams(
            dimension_semantics=("arbitrary", "arbitrary"), vmem_limit_bytes=VMEM_LIMIT),
        name="mlstm",
    )(*args, *consts)


def _merge_kernel(*refs, sparse):
    if sparse:
        (x_ref, ga_ref, hb_ref, pmg_ref, wug_ref, wum_ref, wo_ref, g2_ref,
         wr_hi_ref, wr_lo_ref, br_ref, tril_ref, triu_ref, x1_ref, hs_ref, meta_ref, tab_ref, cnt_scr) = refs
    else:
        (x_ref, ga_ref, hb_ref, pmg_ref, wug_ref, wum_ref, wo_ref, g2_ref,
         wr_hi_ref, wr_lo_ref, br_ref, x1_ref, hm_ref, comb_ref) = refs
    ya = _dot(ga_ref[...], wug_ref[...])
    yb = _dot(hb_ref[...], wum_ref[...])
    z = _sigmoid(pmg_ref[:, 0:D_MODEL]) * ya + _sigmoid(pmg_ref[:, D_MODEL:]) * yb
    x1 = x_ref[...] + _dot(z.astype(BF16), wo_ref[...])
    x1_ref[...] = x1
    hm = _rms(x1, g2_ref[...])
    hm_hi = hm.astype(BF16)
    if not sparse:
        hm_ref[...] = hm_hi
    hm_lo = (hm - hm_hi.astype(F32)).astype(BF16)
    logits = (_dot(hm_hi, wr_hi_ref[...]) + _dot(hm_lo, wr_hi_ref[...])
              + _dot(hm_hi, wr_lo_ref[...]) + br_ref[...])
    lane = lax.broadcasted_iota(jnp.int32, logits.shape, 1)
    neg = -jnp.inf
    is_g = (lane >= N_EXPERTS) & (lane < N_EXPERTS + N_GROUPS)
    lg = jnp.where(is_g, logits, neg)
    mg = jnp.max(lg, axis=-1, keepdims=True)
    p_top = 1.0 / jnp.sum(jnp.exp(lg - mg), axis=-1, keepdims=True)
    gi = jnp.min(jnp.where(lg == mg, lane, 2 * LANES), axis=-1, keepdims=True) - N_EXPERTS
    group_shift = int(math.log2(EXPERTS_PER_GROUP))
    sel = (lane < N_EXPERTS) & (jnp.right_shift(lane, group_shift) == gi)
    le = jnp.where(sel, logits, neg)
    v1 = jnp.max(le, axis=-1, keepdims=True)
    i1 = jnp.min(jnp.where(le == v1, lane, 2 * LANES), axis=-1, keepdims=True)
    le2 = jnp.where(lane == i1, neg, le)
    v2 = jnp.max(le2, axis=-1, keepdims=True)
    i2 = jnp.min(jnp.where(le2 == v2, lane, 2 * LANES), axis=-1, keepdims=True)
    e2 = jnp.exp(v2 - v1)
    wa = 1.0 / (1.0 + e2)
    wb = e2 / (1.0 + e2)
    if not sparse:
        comb_ref[...] = p_top * (jnp.where(lane == i1, wa, 0.0) + jnp.where(lane == i2, wb, 0.0))
        return

    @pl.when(pl.program_id(0) == 0)
    def _():
        cnt_scr[...] = jnp.zeros_like(cnt_scr)

    tm = logits.shape[0]
    oh1 = lane == i1
    oh2 = lane == i2
    both = jnp.where(oh1 | oh2, 1.0, 0.0)
    cnt = jnp.sum(both, axis=0, keepdims=True)
    cnt = jnp.floor((cnt + (SEG_ALIGN - 1)) * (1.0 / SEG_ALIGN)) * SEG_ALIGN
    lower = _dot(jnp.broadcast_to(cnt, (8, LANES)).astype(BF16), triu_ref[...])[0:1, :]
    lpos = _dot(tril_ref[...], both.astype(BF16)) + lower
    lp1 = jnp.sum(jnp.where(oh1, lpos, 0.0), axis=-1, keepdims=True)
    lp2 = jnp.sum(jnp.where(oh2, lpos, 0.0), axis=-1, keepdims=True)
    pos = lax.broadcasted_iota(jnp.int32, (tm, hs_ref.shape[0]), 1)
    onehot = jnp.where((pos == lp1.astype(jnp.int32)) | (pos == lp2.astype(jnp.int32)), 1.0, 0.0)
    hs_ref[...] = _dot_tn(onehot.astype(BF16), hm_hi).astype(BF16)
    cols = (lp1, lp2, p_top * wa, p_top * wb)
    meta = jnp.zeros(logits.shape, F32)
    for ci, col in enumerate(cols):
        meta = jnp.where(lane == ci, col, meta)
    meta_ref[...] = meta
    row8 = lax.broadcasted_iota(jnp.int32, (8, LANES), 0)
    tab_ref[...] = jnp.where(row8 == 0, cnt, jnp.where(row8 == 1, lower, jnp.where(row8 == 2, cnt_scr[...], 0.0)))
    cnt_scr[...] += cnt


def _merge_call(x2, ga, hb, pmg, wug, wum, wo, g2, wr_hi, wr_lo, br, tm, sparse):
    n = x2.shape[0]
    tile = lambda w: pl.BlockSpec((tm, w), lambda i: (i, 0))
    consts = [wug, wum, wo, g2, wr_hi, wr_lo, br]
    out_specs = [tile(D_MODEL), tile(D_MODEL), tile(LANES)]
    out_shape = [jax.ShapeDtypeStruct((n, D_MODEL), F32),
                 jax.ShapeDtypeStruct((n, D_MODEL), BF16),
                 jax.ShapeDtypeStruct((n, LANES), F32)]
    scratch = []
    if sparse:
        local_rows = 2 * tm + SEG_ALIGN * N_EXPERTS
        tiles = lambda *shape: pl.BlockSpec((None,) + shape, lambda i: (i, 0, 0))
        out_specs[1] = tiles(local_rows, D_MODEL)
        out_shape[1] = jax.ShapeDtypeStruct((n // tm, local_rows, D_MODEL), BF16)
        consts.append(jnp.asarray(np.tril(np.ones((tm, tm), np.float32), -1), BF16))
        consts.append(jnp.asarray(np.triu(np.ones((LANES, LANES), np.float32), 1), BF16))
        out_specs.append(tiles(8, LANES))
        out_shape.append(jax.ShapeDtypeStruct((n // tm, 8, LANES), F32))
        scratch.append(pltpu.VMEM((1, LANES), F32))
    return pl.pallas_call(
        functools.partial(_merge_kernel, sparse=sparse),
        grid=(n // tm,),
        in_specs=[tile(D_MODEL), tile(GLA_V), tile(ML_W), tile(W_MG)]
                 + [_full_spec(x.shape) for x in consts],
        out_specs=out_specs,
        out_shape=out_shape,
        scratch_shapes=scratch,
        compiler_params=pltpu.CompilerParams(
            dimension_semantics=("arbitrary",), vmem_limit_bytes=VMEM_LIMIT),
        name="merge",
    )(x2, ga, hb, pmg, *consts)


MOE_ROWS = 256
SEG_ALIGN = 16
SEG_SIZES = (256, 128, 64, 32, 16)


def _segment_copies(fn, tables, tile, local_ref, buffer, flat_ref, sem, to_flat):
    cnt_ref, loff_ref, gpos_ref = tables

    def body(e, carry):
        idx = tile * N_EXPERTS + e
        n, lo, gp = cnt_ref[idx], loff_ref[idx], gpos_ref[idx]
        for size in SEG_SIZES:
            @pl.when((n & size) != 0)
            def _():
                off = n & ~(2 * size - 1)
                local = local_ref.at[buffer, pl.ds(pl.multiple_of(lo + off, SEG_ALIGN), size), :]
                flat = flat_ref.at[pl.ds(pl.multiple_of(gp + off, SEG_ALIGN), size), :]
                src, dst = (local, flat) if to_flat else (flat, local)
                fn(pltpu.make_async_copy(src, dst, sem))
        return carry

    lax.fori_loop(0, N_EXPERTS, body, 0)


def _dispatch_kernel(cnt_ref, loff_ref, gpos_ref, tail_ref, hs_ref, xs_ref, zero_ref, sem, zsem):
    step = pl.program_id(0)

    @pl.when(step == 0)
    def _():
        zero_ref[...] = jnp.zeros_like(zero_ref)
        n_tiles = xs_ref.shape[0] // MOE_ROWS

        def clear(row):
            start = pl.multiple_of(row, MOE_ROWS)
            return pltpu.make_async_copy(zero_ref, xs_ref.at[pl.ds(start, MOE_ROWS), :], zsem)

        def unused(fn):
            def body(j, carry):
                fn(clear(j * MOE_ROWS))
                return carry
            lax.fori_loop(tail_ref[2 * N_EXPERTS], n_tiles, body, 0)

        for e in range(N_EXPERTS):
            @pl.when(tail_ref[N_EXPERTS + e] > 0)
            def _():
                clear(tail_ref[e]).start()
        unused(lambda c: c.start())
        for e in range(N_EXPERTS):
            @pl.when(tail_ref[N_EXPERTS + e] > 0)
            def _():
                clear(tail_ref[e]).wait()
        unused(lambda c: c.wait())

    tables = (cnt_ref, loff_ref, gpos_ref)
    _segment_copies(lambda c: c.start(), tables, step, hs_ref, 0, xs_ref, sem, True)
    _segment_copies(lambda c: c.wait(), tables, step, hs_ref, 0, xs_ref, sem, True)


def _dispatch_call(tables, tail, hs, n_rows):
    return pl.pallas_call(
        _dispatch_kernel,
        grid_spec=pltpu.PrefetchScalarGridSpec(
            num_scalar_prefetch=4,
            grid=(hs.shape[0],),
            in_specs=[pl.BlockSpec((1,) + hs.shape[1:], lambda i, *_: (i, 0, 0))],
            out_specs=pl.BlockSpec(memory_space=pl.ANY),
            scratch_shapes=[pltpu.VMEM((MOE_ROWS, D_MODEL), hs.dtype),
                            pltpu.SemaphoreType.DMA, pltpu.SemaphoreType.DMA],
        ),
        out_shape=jax.ShapeDtypeStruct((n_rows, D_MODEL), hs.dtype),
        compiler_params=pltpu.CompilerParams(
            dimension_semantics=("arbitrary",), vmem_limit_bytes=VMEM_LIMIT),
        name="moe_dispatch",
    )(*tables, tail, hs)


def _gmm_kernel(te_ref, nv_ref, xs_ref, wg_ref, wu_ref, wd_ref, o_ref):
    used = pl.program_id(0) < nv_ref[0]

    @pl.when(used)
    def _():
        x = xs_ref[...]
        a = _dot(x, wg_ref[...].astype(BF16))
        u = _dot(x, wu_ref[...].astype(BF16))
        hh = (a * _sigmoid(a)) * u
        o_ref[...] = _dot(hh.astype(BF16), wd_ref[...].astype(BF16)).astype(o_ref.dtype)

    @pl.when(jnp.logical_not(used))
    def _():
        o_ref[...] = jnp.zeros_like(o_ref)


def _gmm_call(tile_expert, n_valid, xs, wg, wu, wd):
    n_tiles = xs.shape[0] // MOE_ROWS
    rows = lambda j, te, nv: (jnp.minimum(j, nv[0] - 1), 0)
    wsel = lambda j, te, nv: (te[j], 0, 0)
    return pl.pallas_call(
        _gmm_kernel,
        grid_spec=pltpu.PrefetchScalarGridSpec(
            num_scalar_prefetch=2,
            grid=(n_tiles,),
            in_specs=[pl.BlockSpec((MOE_ROWS, D_MODEL), rows),
                      pl.BlockSpec((None, D_MODEL, D_EXPERT), wsel),
                      pl.BlockSpec((None, D_MODEL, D_EXPERT), wsel),
                      pl.BlockSpec((None, D_EXPERT, D_MODEL), wsel)],
            out_specs=pl.BlockSpec((MOE_ROWS, D_MODEL), lambda j, te, nv: (j, 0)),
        ),
        out_shape=jax.ShapeDtypeStruct(xs.shape, xs.dtype),
        compiler_params=pltpu.CompilerParams(
            dimension_semantics=("arbitrary",), vmem_limit_bytes=VMEM_LIMIT),
        name="moe_grouped",
    )(tile_expert, n_valid, xs, wg, wu, wd)


def _combine_kernel(cnt_ref, loff_ref, gpos_ref, x1_ref, meta_ref, gf_ref, os_ref, y_ref, buf_ref, sem):
    step = pl.program_id(0)
    n_steps = pl.num_programs(0)
    tm = x1_ref.shape[0]
    tables = (cnt_ref, loff_ref, gpos_ref)

    def fetch(fn, tile):
        slot = tile % 2
        _segment_copies(fn, tables, tile, buf_ref, slot, os_ref, sem.at[slot], False)

    @pl.when(step == 0)
    def _():
        buf_ref[...] = jnp.zeros_like(buf_ref)
        fetch(lambda c: c.start(), step)

    @pl.when(step + 1 < n_steps)
    def _():
        fetch(lambda c: c.start(), step + 1)

    fetch(lambda c: c.wait(), step)
    rows = buf_ref[step % 2]
    pos = lax.broadcasted_iota(jnp.int32, (tm, rows.shape[0]), 1)
    lp1 = meta_ref[:, 0:1].astype(jnp.int32)
    lp2 = meta_ref[:, 1:2].astype(jnp.int32)
    q = jnp.where(pos == lp1, meta_ref[:, 2:3], 0.0) + jnp.where(pos == lp2, meta_ref[:, 3:4], 0.0)
    q_hi = q.astype(BF16)
    q_lo = (q - q_hi.astype(F32)).astype(BF16)
    y = x1_ref[...] + _dot(q_hi, rows) + _dot(q_lo, rows)
    y_ref[...] = _rms(y, gf_ref[...])


def _combine_call(tables, x1, meta, gf, out_sorted, local_rows, tm):
    n = x1.shape[0]
    tile = lambda w: pl.BlockSpec((tm, w), lambda i, *_: (i, 0))
    return pl.pallas_call(
        _combine_kernel,
        grid_spec=pltpu.PrefetchScalarGridSpec(
            num_scalar_prefetch=3,
            grid=(n // tm,),
            in_specs=[tile(D_MODEL), tile(LANES), pl.BlockSpec(gf.shape, lambda i, *_: (0, 0)),
                      pl.BlockSpec(memory_space=pl.ANY)],
            out_specs=tile(D_MODEL),
            scratch_shapes=[pltpu.VMEM((2, local_rows, D_MODEL), out_sorted.dtype),
                            pltpu.SemaphoreType.DMA((2,))],
        ),
        out_shape=jax.ShapeDtypeStruct((n, D_MODEL), F32),
        compiler_params=pltpu.CompilerParams(
            dimension_semantics=("arbitrary",), vmem_limit_bytes=VMEM_LIMIT),
        name="moe_combine",
    )(*tables, x1, meta, gf, out_sorted)


def _sparse_moe(x1, hs, meta, tab, p, tm):
    n_tiles = -(-(hs.shape[0] * hs.shape[1]) // MOE_ROWS) + N_EXPERTS
    tab = tab[:, :, :N_EXPERTS].astype(jnp.int32)
    cnt, loff, before = tab[:, 0], tab[:, 1], tab[:, 2]
    total = before[-1] + cnt[-1]
    tiles = (total + (MOE_ROWS - 1)) // MOE_ROWS
    tile_end = jnp.cumsum(tiles)
    tile_start = tile_end - tiles
    n_valid = tile_end[-1:]
    gpos = tile_start[None, :] * MOE_ROWS + before
    j = jnp.minimum(jnp.arange(n_tiles, dtype=jnp.int32), n_valid - 1)
    tile_expert = jnp.sum((j[:, None] >= tile_end[None, :]).astype(jnp.int32), axis=1)
    tail = jnp.concatenate([(tile_end - 1) * MOE_ROWS, total, n_valid]).astype(jnp.int32)
    tables = (cnt.reshape(-1), loff.reshape(-1), gpos.reshape(-1))
    xs = _dispatch_call(tables, tail, hs, n_tiles * MOE_ROWS)
    out_sorted = _gmm_call(tile_expert, n_valid.astype(jnp.int32), xs, p["wg"], p["wu"], p["wd"])
    return _combine_call(tables, x1, meta, p["gf"], out_sorted, hs.shape[1], tm)


def _moe_kernel(x1_ref, hm_ref, comb_ref, wg_ref, wu_ref, wd_ref, gf_ref, y_ref, acc_ref):
    e = pl.program_id(1)

    @pl.when(e == 0)
    def _():
        acc_ref[...] = jnp.zeros_like(acc_ref)

    hm = hm_ref[...]
    a = _dot(hm, wg_ref[...].astype(BF16))
    u = _dot(hm, wu_ref[...].astype(BF16))
    lane = lax.broadcasted_iota(jnp.int32, comb_ref.shape, 1)
    ce = jnp.sum(jnp.where(lane == e, comb_ref[...], 0.0), axis=-1, keepdims=True)
    hh = (a * _sigmoid(a)) * u * ce
    acc_ref[...] += _dot(hh.astype(BF16), wd_ref[...].astype(BF16))

    @pl.when(e == pl.num_programs(1) - 1)
    def _():
        y_ref[...] = _rms(x1_ref[...] + acc_ref[...], gf_ref[...])


def _moe_call(x1, hm, comb, wg, wu, wd, gf, tm):
    n = x1.shape[0]
    tile = lambda w: pl.BlockSpec((tm, w), lambda i, e: (i, 0))
    return pl.pallas_call(
        _moe_kernel,
        grid=(n // tm, N_EXPERTS),
        in_specs=[tile(D_MODEL), tile(D_MODEL), tile(LANES),
                  pl.BlockSpec((None, D_MODEL, D_EXPERT), lambda i, e: (e, 0, 0)),
                  pl.BlockSpec((None, D_MODEL, D_EXPERT), lambda i, e: (e, 0, 0)),
                  pl.BlockSpec((None, D_EXPERT, D_MODEL), lambda i, e: (e, 0, 0)),
                  _full_spec(gf.shape)],
        out_specs=tile(D_MODEL),
        out_shape=jax.ShapeDtypeStruct((n, D_MODEL), F32),
        scratch_shapes=[pltpu.VMEM((tm, D_MODEL), F32)],
        compiler_params=pltpu.CompilerParams(
            dimension_semantics=("arbitrary", "arbitrary"), vmem_limit_bytes=VMEM_LIMIT),
        name="moe",
    )(x1, hm, comb, wg, wu, wd, gf)


def _pad_cols(w, width):
    return jnp.pad(w, ((0, 0), (0, width - w.shape[1])))


def _prep_weights(norm1_g, w_in, gla_w_gate2, gla_b_gate, gla_norm_g, w_up_gla,
                  ml_conv_w, ml_conv_b, ml_b_i, ml_b_f, w_up_ml, w_out,
                  norm2_g, router_g_w, router_g_b, router_e_w, router_e_b,
                  moe_w_gate, moe_w_up, moe_w_down, final_g):
    o_lr = W_GLA
    o_ml = o_lr + GLA_GATE_RANK
    o_if = o_ml + W_ML
    o_mg = o_if + 2 * ML_HEADS
    w_p = jnp.concatenate([
        w_in[:, 0:o_lr], _pad_cols(w_in[:, o_lr:o_ml], LANES),
        w_in[:, o_ml:o_if], _pad_cols(w_in[:, o_if:o_mg], LANES),
        w_in[:, o_mg:]], axis=1).astype(BF16)
    wr = _pad_cols(jnp.concatenate([router_e_w, router_g_w], axis=1), LANES)
    wr_hi = wr.astype(BF16)
    wr_lo = (wr - wr_hi.astype(F32)).astype(BF16)
    br = _pad_cols(jnp.concatenate([router_e_b, router_g_b])[None, :], LANES)
    wg2 = jnp.pad(gla_w_gate2, ((0, LANES - GLA_GATE_RANK), (0, 0)))
    wg2_hi = wg2.astype(BF16)
    return dict(
        g1=norm1_g[None, :], w_p=w_p,
        wg2_p=jnp.stack([wg2_hi, (wg2 - wg2_hi.astype(F32)).astype(BF16)]),
        bg=gla_b_gate[None, :], gn=gla_norm_g[None, :],
        wug=w_up_gla.astype(BF16),
        cw=ml_conv_w, cb=ml_conv_b[None, :],
        bif=_pad_cols(jnp.concatenate([ml_b_i, ml_b_f])[None, :], LANES),
        wum=w_up_ml.astype(BF16), wo=w_out.astype(BF16),
        g2=norm2_g[None, :], wr_hi=wr_hi, wr_lo=wr_lo, br=br,
        wg=moe_w_gate.reshape(N_EXPERTS, D_MODEL, D_EXPERT),
        wu=moe_w_up.reshape(N_EXPERTS, D_MODEL, D_EXPERT),
        wd=moe_w_down.reshape(N_EXPERTS, D_EXPERT, D_MODEL),
        gf=final_g[None, :],
    )


def _layer(x, gla_s0, ml_c0, ml_n0, ml_m0, conv0, p, *, chunk, seq_tile, row_tile, moe_tile, sparse):
    b, l, _ = x.shape
    n = b * l
    x2 = x.reshape(n, D_MODEL)
    pg, plr, pml, pif, pmg = _proj_call(x2, p["g1"], p["w_p"], row_tile)
    r3 = lambda a: a.reshape(b, l, a.shape[-1])
    s0 = None if gla_s0 is None else gla_s0.reshape(b, GLA_QK, GLA_DV)
    ga, gla_s = _gla_call(r3(pg), r3(plr), s0, p["wg2_p"], p["bg"], p["gn"], chunk, seq_tile)
    gla_s = gla_s.reshape(b, GLA_HEADS, GLA_DK, GLA_DV)
    if ml_c0 is None:
        caug0 = m0 = None
    else:
        caug0 = jnp.concatenate([ml_c0, jnp.broadcast_to(ml_n0[..., None], ml_c0.shape)], axis=-1)
        m0 = jnp.broadcast_to(_pad_cols(ml_m0, LANES)[:, None, :], (b, 8, LANES))
    hb, caug, m_b, new_conv = _mlstm_call(r3(pml), r3(pif), caug0, m0, conv0,
                                          p["cw"], p["cb"], p["bif"], chunk, seq_tile)
    x1, hm, route, *counts = _merge_call(x2, ga.reshape(n, GLA_V), hb.reshape(n, ML_W), pmg,
                                         p["wug"], p["wum"], p["wo"], p["g2"],
                                         p["wr_hi"], p["wr_lo"], p["br"], row_tile, sparse)
    if sparse:
        y = _sparse_moe(x1, hm, route, counts[0], p, moe_tile)
    else:
        y = _moe_call(x1, hm, route, p["wg"], p["wu"], p["wd"], p["gf"], moe_tile)
    return (y.reshape(b, l, D_MODEL), gla_s[None], caug[..., 0:ML_DH][None],
            caug[..., ML_DH][None], m_b[:, 0, 0:ML_HEADS][None], new_conv[None])


def kernel(x_prompt, x_sample, state_gla_S, state_mlstm_C, state_mlstm_n, state_mlstm_m, state_mlstm_conv, norm1_g, w_in, gla_w_gate2, gla_b_gate, gla_norm_g, w_up_gla, ml_conv_w, ml_conv_b, ml_b_i, ml_b_f, w_up_ml, w_out, norm2_g, router_g_w, router_g_b, router_e_w, router_e_b, moe_w_gate, moe_w_up, moe_w_down, final_g):
    assert norm1_g.shape[0] == 1, "single-layer trunk"
    p = _prep_weights(norm1_g[0], w_in[0], gla_w_gate2[0], gla_b_gate[0], gla_norm_g[0], w_up_gla[0],
                      ml_conv_w[0], ml_conv_b[0], ml_b_i[0], ml_b_f[0], w_up_ml[0], w_out[0],
                      norm2_g[0], router_g_w[0], router_g_b[0], router_e_w[0], router_e_b[0],
                      moe_w_gate[0], moe_w_up[0], moe_w_down[0], final_g)
    yp, *sp = _layer(x_prompt, None, None, None, None, None, p,
                     chunk=128, seq_tile=512, row_tile=256, moe_tile=256, sparse=True)
    dec_seq = x_sample.shape[1]
    ns = x_sample.shape[0] * dec_seq
    ys, *ss = _layer(x_sample, state_gla_S[0], state_mlstm_C[0], state_mlstm_n[0], state_mlstm_m[0],
                     state_mlstm_conv[0], p,
                     chunk=dec_seq, seq_tile=dec_seq, row_tile=ns, moe_tile=ns, sparse=False)
    return (yp, ys, *sp, *ss)
```

```python
import functools
import math

import numpy as np
import jax
import jax.numpy as jnp
from jax import lax
from jax.experimental import pallas as pl
from jax.experimental.pallas import tpu as pltpu

D_MODEL = 1024
GLA_HEADS = 4
GLA_DK = 64
GLA_DV = 128
GLA_GATE_RANK = 16
GLA_TAU = 16.0
ML_HEADS = 4
ML_DH = 128
CONV_W = 4
N_GROUPS = 4
EXPERTS_PER_GROUP = 8
N_EXPERTS = N_GROUPS * EXPERTS_PER_GROUP
D_EXPERT = 256
EPS = 1e-6

GLA_QK = GLA_HEADS * GLA_DK
GLA_V = GLA_HEADS * GLA_DV
ML_W = ML_HEADS * ML_DH

LANES = 128
VMEM_LIMIT = 56 * 1024 * 1024

W_GLA = 2 * GLA_QK + 2 * GLA_V
W_ML = 2 * ML_W + ML_W + ML_W
W_MG = 2 * D_MODEL
PROJ_WIDTHS = (W_GLA, LANES, W_ML, LANES, W_MG)

F32 = jnp.float32
BF16 = jnp.bfloat16

PROJ_OUTPUTS = ((2 * GLA_QK, F32), (2 * GLA_V, BF16), (LANES, F32), (2 * ML_W, F32), (2 * ML_W, BF16),
                (LANES, F32), (W_MG, BF16))


def _dot(a, b):
    return jnp.dot(a, b, preferred_element_type=F32)


def _dot_nt(a, b):
    return lax.dot_general(a, b, (((1,), (1,)), ((), ())), preferred_element_type=F32)


def _dot_tn(a, b):
    return lax.dot_general(a, b, (((0,), (0,)), ((), ())), preferred_element_type=F32)


def _split3(x):
    hi = x.astype(BF16)
    r1 = x - hi.astype(F32)
    mid = r1.astype(BF16)
    lo = (r1 - mid.astype(F32)).astype(BF16)
    return hi, mid, lo


def _dot_exact_lhs(m, x):
    hi, mid, lo = _split3(x)
    return _dot(m, hi) + _dot(m, mid) + _dot(m, lo)


def _log_sigmoid(z):
    return jnp.minimum(z, 0.0) - jnp.log1p(jnp.exp(-jnp.abs(z)))


def _sigmoid(z):
    return 1.0 / (1.0 + jnp.exp(-z))


def _rms(x, g):
    return x * lax.rsqrt(jnp.mean(x * x, axis=-1, keepdims=True) + EPS) * g


def _full_spec(shape):
    nd = len(shape)
    return pl.BlockSpec(shape, lambda *_: (0,) * nd)


def _proj_kernel(x_ref, g_ref, w_ref, qk_ref, vr_ref, lr_ref, mqk_ref, mvo_ref, if_ref, mg_ref):
    h = _rms(x_ref[...], g_ref[...]).astype(BF16)
    starts = np.cumsum((0,) + PROJ_WIDTHS)

    def cols(group, lo, hi):
        return _dot(h, w_ref[:, starts[group] + lo:starts[group] + hi])

    qk_ref[...] = cols(0, 0, 2 * GLA_QK)
    vr_ref[:, 0:GLA_V] = cols(0, 2 * GLA_QK, 2 * GLA_QK + GLA_V).astype(BF16)
    r = cols(0, 2 * GLA_QK + GLA_V, W_GLA)
    vr_ref[:, GLA_V:] = (r * _sigmoid(r)).astype(BF16)
    lr_ref[...] = cols(1, 0, LANES)
    mqk_ref[...] = cols(2, 0, 2 * ML_W)
    mvo_ref[:, 0:ML_W] = cols(2, 2 * ML_W, 3 * ML_W).astype(BF16)
    mvo_ref[:, ML_W:] = _sigmoid(cols(2, 3 * ML_W, W_ML)).astype(BF16)
    if_ref[...] = cols(3, 0, LANES)
    mg_ref[...] = _sigmoid(cols(4, 0, W_MG)).astype(BF16)


def _proj_call(x2, g, w_p, tm):
    n = x2.shape[0]
    return pl.pallas_call(
        _proj_kernel,
        grid=(n // tm,),
        in_specs=[pl.BlockSpec((tm, D_MODEL), lambda i: (i, 0)),
                  _full_spec(g.shape), _full_spec(w_p.shape)],
        out_specs=[pl.BlockSpec((tm, w), lambda i: (i, 0)) for w, _ in PROJ_OUTPUTS],
        out_shape=[jax.ShapeDtypeStruct((n, w), dt) for w, dt in PROJ_OUTPUTS],
        compiler_params=pltpu.CompilerParams(
            dimension_semantics=("arbitrary",), vmem_limit_bytes=VMEM_LIMIT),
        name="in_proj",
    )(x2, g, w_p)


def _gla_consts(c):
    nlev = int(math.log2(c))
    assert 1 << nlev == c
    t = np.arange(c)[:, None]
    j = np.arange(c)[None, :]
    lv = np.full((c, c), -1, np.int32)
    for l in range(nlev):
        h = c >> (l + 1)
        upper = (t % (2 * h)) >= h
        same = (j // (2 * h)) == (t // (2 * h))
        s_lower = (j % (2 * h)) < h
        lv[np.broadcast_to(upper, (c, c)) & same & s_lower] = l
    lv[np.eye(c, dtype=bool)] = nlev
    tri = (j <= t).astype(np.float32)
    return jnp.asarray(tri, BF16), jnp.asarray(np.concatenate([lv, lv], axis=1))


def _gla_kernel(*refs, c, t, has_state):
    if has_state:
        (qk_ref, vr_ref, plr_ref, s0_ref, wg2_ref, bg_ref, gn_ref, tri_ref, lv_ref,
         o_ref, sout_ref, s_scr) = refs
    else:
        (qk_ref, vr_ref, plr_ref, wg2_ref, bg_ref, gn_ref, tri_ref, lv_ref,
         o_ref, sout_ref, s_scr) = refs
    nlev = int(math.log2(c))
    step = pl.program_id(1)

    @pl.when(step == 0)
    def _():
        if has_state:
            s_scr[...] = s0_ref[...]
        else:
            s_scr[...] = jnp.zeros_like(s_scr)

    lane_k = lax.broadcasted_iota(jnp.int32, (c, GLA_QK), 1)
    first_of_pair = (lane_k % (2 * GLA_DK)) < GLA_DK
    row_k = lax.broadcasted_iota(jnp.int32, (GLA_QK, GLA_DV), 0)
    row_t = lax.broadcasted_iota(jnp.int32, (c, GLA_QK), 0)

    def block_ref(b, blk, idx):
        if blk >= 8:
            b3 = b.reshape(c // blk, blk, GLA_QK)
            return jnp.broadcast_to(b3[:, idx:idx + 1, :], b3.shape).reshape(c, GLA_QK)
        r = row_t % blk
        out = b
        for sh in range(-idx, blk - idx):
            if sh != 0:
                out = jnp.where(r - idx == sh, pltpu.roll(b, sh % c, axis=0), out)
        return out

    def chunk(ci, carry):
        r0 = ci * c
        rows = pl.ds(r0, c)
        q = qk_ref[rows, 0:GLA_QK] * (GLA_DK ** -0.5)
        k = qk_ref[rows, GLA_QK:2 * GLA_QK]
        v = vr_ref[rows, 0:GLA_V]
        glr = plr_ref[rows, :]
        g_hi = glr.astype(BF16)
        g_lo = (glr - g_hi.astype(F32)).astype(BF16)
        z = (_dot(g_hi, wg2_ref[0]) + _dot(g_lo, wg2_ref[0]) + _dot(g_hi, wg2_ref[1])) + bg_ref[...]
        la = _log_sigmoid(z) * (1.0 / GLA_TAU)
        b = _dot_exact_lhs(tri_ref[...], la)
        b_last = b[c - 1:c, :]
        qe = (q * jnp.exp(b)).astype(BF16)
        kl = (k * jnp.exp(b_last - b)).astype(BF16)
        dcol = jnp.exp(jnp.broadcast_to(b_last, (LANES, GLA_QK)).T)

        k_a = jnp.where(first_of_pair, k, 0.0)
        k_b = k - k_a
        lv2 = lv_ref[...]
        a = [jnp.zeros((c, 2 * c), F32) for _ in range(GLA_HEADS // 2)]
        for l in range(nlev + 1):
            if l < nlev:
                half = c >> (l + 1)
                d = b - block_ref(b, 2 * half, half - 1)
                e = jnp.exp(jnp.minimum(d, -d))
                qt, kta, ktb = q * e, k_a * e, k_b * e
            else:
                qt, kta, ktb = q, k_a, k_b
            qt, kta, ktb = qt.astype(BF16), kta.astype(BF16), ktb.astype(BF16)
            for pr in range(GLA_HEADS // 2):
                ls = slice(pr * 2 * GLA_DK, (pr + 1) * 2 * GLA_DK)
                rhs = jnp.concatenate([kta[:, ls], ktb[:, ls]], axis=0)
                a[pr] = jnp.where(lv2 == l, _dot_nt(qt[:, ls], rhs), a[pr])

        s_all = s_scr[...]
        s_bd = jnp.concatenate(
            [jnp.where((row_k // GLA_DK) == h, s_all, 0.0).astype(BF16) for h in range(GLA_HEADS)], axis=1)
        o_inter = _dot(qe, s_bd)
        u_all = _dot_tn(kl, v)
        for h in range(GLA_HEADS):
            vs = slice(h * GLA_DV, (h + 1) * GLA_DV)
            ks = slice(h * GLA_DK, (h + 1) * GLA_DK)
            a_h = a[h // 2][:, (h % 2) * c:(h % 2 + 1) * c]
            o = _dot(a_h.astype(BF16), v[:, vs]) + o_inter[:, vs]
            on = _rms(o, gn_ref[:, vs])
            gate = vr_ref[rows, GLA_V + h * GLA_DV:GLA_V + (h + 1) * GLA_DV]
            o_ref[rows, vs] = (on * gate.astype(F32)).astype(o_ref.dtype)
            s_scr[ks, :] = dcol[ks, :] * s_all[ks, :] + u_all[ks, vs]
        return carry

    for ci in range(t // c):
        chunk(ci, 0)

    @pl.when(step == pl.num_programs(1) - 1)
    def _():
        sout_ref[...] = s_scr[...]


def _gla_call(qk, vr, plr, s0, wg2_p, bg, gn, c, t):
    b, l, _ = qk.shape
    tri, lv2 = _gla_consts(c)
    has_state = s0 is not None
    tile = lambda w: pl.BlockSpec((None, t, w), lambda bi, i: (bi, i, 0))
    state_spec = pl.BlockSpec((None, GLA_QK, GLA_DV), lambda bi, i: (bi, 0, 0))
    in_specs = [tile(2 * GLA_QK), tile(2 * GLA_V), tile(LANES)]
    args = [qk, vr, plr]
    if has_state:
        in_specs.append(state_spec)
        args.append(s0)
    consts = [wg2_p, bg, gn, tri, lv2]
    in_specs += [_full_spec(x.shape) for x in consts]
    return pl.pallas_call(
        functools.partial(_gla_kernel, c=c, t=t, has_state=has_state),
        grid=(b, l // t),
        in_specs=in_specs,
        out_specs=[tile(GLA_V), state_spec],
        out_shape=[jax.ShapeDtypeStruct((b, l, GLA_V), BF16),
                   jax.ShapeDtypeStruct((b, GLA_QK, GLA_DV), F32)],
        scratch_shapes=[pltpu.VMEM((GLA_QK, GLA_DV), F32)],
        compiler_params=pltpu.CompilerParams(
            dimension_semantics=("arbitrary", "arbitrary"), vmem_limit_bytes=VMEM_LIMIT),
        name="gla",
    )(*args, *consts)


def _mlstm_kernel(*refs, c, t, has_state):
    if has_state:
        (mqk_ref, mvo_ref, pif_ref, c0_ref, m0_ref, cv0_ref, cw_ref, cb_ref, bif_ref, tri_ref, sel_ref,
         o_ref, cout_ref, mout_ref, cvout_ref, c_scr, m_scr, cv_scr, qk_scr) = refs
    else:
        (mqk_ref, mvo_ref, pif_ref, cw_ref, cb_ref, bif_ref, tri_ref, sel_ref,
         o_ref, cout_ref, mout_ref, cvout_ref, c_scr, m_scr, cv_scr, qk_scr) = refs
    step = pl.program_id(1)
    hist = CONV_W - 1
    pad = 8

    @pl.when(step == 0)
    def _():
        if has_state:
            c_scr[...] = c0_ref[...]
            m_scr[...] = m0_ref[...]
            cv_scr[0:pad - hist, :] = jnp.zeros((pad - hist, 2 * ML_W), F32)
            cv_scr[pad - hist:pad, :] = cv0_ref[...]
        else:
            c_scr[...] = jnp.zeros_like(c_scr)
            m_scr[...] = jnp.zeros_like(m_scr)
            cv_scr[0:pad, :] = jnp.zeros((pad, 2 * ML_W), F32)

    cv_scr[pad:pad + t, :] = mqk_ref[...]
    acc = cb_ref[...] + cv_scr[pad:pad + t, :] * cw_ref[CONV_W - 1:CONV_W, :]
    for d in range(1, CONV_W):
        acc = acc + cv_scr[pad - d:pad - d + t, :] * cw_ref[CONV_W - 1 - d:CONV_W - d, :]
    conv = acc * _sigmoid(acc)
    qk_scr[:, 0:ML_W] = conv[:, 0:ML_W].astype(BF16)
    qk_scr[:, ML_W:] = (conv[:, ML_W:] * (ML_DH ** -0.5)).astype(BF16)
    cv_scr[0:pad, :] = cv_scr[t:t + pad, :]

    gts = pif_ref[...] + bif_ref[...]
    flog = pltpu.roll(_log_sigmoid(gts), LANES - ML_HEADS, axis=1)

    lane = lax.broadcasted_iota(jnp.int32, (c, LANES), 1)
    row_c = lax.broadcasted_iota(jnp.int32, (c, LANES), 0)
    causal = (lax.broadcasted_iota(jnp.int32, (c, c), 1) <= lax.broadcasted_iota(jnp.int32, (c, c), 0))
    ones_v = jnp.ones((c, ML_DH), BF16)

    def slots(pieces):
        out = jnp.zeros((c, LANES), F32)
        for j, piece in enumerate(pieces):
            if not isinstance(piece, float) and j > 0:
                piece = pltpu.roll(piece, ML_HEADS * j, axis=1)
            out = jnp.where((lane >= ML_HEADS * j) & (lane < ML_HEADS * (j + 1)), piece, out)
        return out

    def split3f(x):
        return [p.astype(F32) for p in _split3(x)]

    def chunk(ci, carry):
        r0 = ci * c
        rows = pl.ds(r0, c)
        ip = gts[r0:r0 + c]
        bc = _dot_exact_lhs(tri_ref[...], flog[r0:r0 + c])
        w = ip - bc
        cm = w
        for j in range(int(math.log2(c))):
            sh = 1 << j
            cm = jnp.where(row_c >= sh, jnp.maximum(cm, pltpu.roll(cm, sh, axis=0)), cm)
        mprev = m_scr[0:1, :]
        g = jnp.maximum(mprev, cm)
        g_last = g[c - 1:c, :]
        m_scr[...] = jnp.broadcast_to(bc[c - 1:c, :] + g_last, m_scr.shape)
        lhs_all = slots([1.0, 1.0, 1.0] + split3f(g))
        rhs = slots(split3f(w) + [-1.0, -1.0, -1.0]).astype(BF16)
        y = slots(split3f(mprev - g) + split3f(-(bc + g)) + split3f(w - g_last)).astype(BF16)
        for h in range(ML_HEADS):
            hs = slice(h * ML_DH, (h + 1) * ML_DH)
            qh = qk_scr[rows, h * ML_DH:(h + 1) * ML_DH]
            kh = qk_scr[rows, ML_W + h * ML_DH:ML_W + (h + 1) * ML_DH]
            vaug = jnp.concatenate(
                [mvo_ref[rows, h * ML_DH:(h + 1) * ML_DH], ones_v],
                axis=1)
            lhs = jnp.where(lane % ML_HEADS == h, lhs_all, 0.0).astype(BF16)
            p = jnp.exp(jnp.where(causal, _dot_nt(lhs, rhs), -jnp.inf))
            bx = jnp.exp(_dot(y, sel_ref[h]))
            w_inter = bx[:, 0:ML_DH]
            e_mt = bx[:, ML_DH:2 * ML_DH]
            w_state = bx[:, 2 * ML_DH:3 * ML_DH]
            s = (_dot_nt(qh, kh) * p).astype(BF16)
            caug = c_scr[h]
            nd = _dot(s, vaug) + jnp.concatenate([w_inter, w_inter], axis=1) * _dot(qh, caug.astype(BF16))
            hh = nd[:, 0:ML_DH] / jnp.maximum(jnp.abs(nd[:, ML_DH:]), e_mt)
            o_gate = mvo_ref[rows, ML_W + h * ML_DH:ML_W + (h + 1) * ML_DH]
            o_ref[rows, hs] = (o_gate.astype(F32) * hh).astype(o_ref.dtype)
            ks = (kh.astype(F32) * w_state).astype(BF16)
            dec = w_inter[c - 1:c, :]
            c_scr[h] = jnp.concatenate([dec, dec], axis=1) * caug + _dot_tn(ks, vaug)
        return carry

    for ci in range(t // c):
        chunk(ci, 0)

    @pl.when(step == pl.num_programs(1) - 1)
    def _():
        cout_ref[...] = c_scr[...]
        mout_ref[...] = m_scr[...]
        cvout_ref[...] = mqk_ref[t - hist:t, :]


def _mlstm_call(mqk, mvo, pif, caug0, m0, cv0, cw, cb, bif, c, t):
    b, l, _ = mqk.shape
    has_state = caug0 is not None
    tri = jnp.asarray(np.tril(np.ones((c, c), np.float32)), BF16)
    sel = np.zeros((ML_HEADS, LANES, 3 * ML_DH), np.float32)
    for h in range(ML_HEADS):
        for slot in range(9):
            sel[h, ML_HEADS * slot + h, (slot // 3) * ML_DH:(slot // 3 + 1) * ML_DH] = 1.0
    sel = jnp.asarray(sel, BF16)
    tile = lambda w: pl.BlockSpec((None, t, w), lambda bi, i: (bi, i, 0))
    c_spec = pl.BlockSpec((None, ML_HEADS, ML_DH, 2 * ML_DH), lambda bi, i: (bi, 0, 0, 0))
    m_spec = pl.BlockSpec((None, 8, LANES), lambda bi, i: (bi, 0, 0))
    cv_spec = pl.BlockSpec((None, CONV_W - 1, 2 * ML_W), lambda bi, i: (bi, 0, 0))
    in_specs = [tile(2 * ML_W), tile(2 * ML_W), tile(LANES)]
    args = [mqk, mvo, pif]
    if has_state:
        in_specs += [c_spec, m_spec, cv_spec]
        args += [caug0, m0, cv0]
    consts = [cw, cb, bif, tri, sel]
    in_specs += [_full_spec(x.shape) for x in consts]
    return pl.pallas_call(
        functools.partial(_mlstm_kernel, c=c, t=t, has_state=has_state),
        grid=(b, l // t),
        in_specs=in_specs,
        out_specs=[tile(ML_W), c_spec, m_spec, cv_spec],
        out_shape=[jax.ShapeDtypeStruct((b, l, ML_W), BF16),
                   jax.ShapeDtypeStruct((b, ML_HEADS, ML_DH, 2 * ML_DH), F32),
                   jax.ShapeDtypeStruct((b, 8, LANES), F32),
                   jax.ShapeDtypeStruct((b, CONV_W - 1, 2 * ML_W), F32)],
        scratch_shapes=[pltpu.VMEM((ML_HEADS, ML_DH, 2 * ML_DH), F32),
                        pltpu.VMEM((8, LANES), F32),
                        pltpu.VMEM((8 + t, 2 * ML_W), F32),
                        pltpu.VMEM((t, 2 * ML_W), BF16)],
        compiler_params=pltpu.CompilerParams(
            dimension_semantics=("arbitrary", "arbitrary"), vmem_limit_bytes=VMEM_LIMIT),
        name="mlstm",
    )(*args, *consts)


def _merge_kernel(*refs, sparse):
    if sparse:
        (x_ref, ga_ref, hb_ref, pmg_ref, wug_ref, wum_ref, wo_ref, g2_ref,
         wr_hi_ref, wr_lo_ref, br_ref, tril_ref, triu_ref, x1_ref, hs_ref, meta_ref, tab_ref, cnt_scr) = refs
    else:
        (x_ref, ga_ref, hb_ref, pmg_ref, wug_ref, wum_ref, wo_ref, g2_ref,
         wr_hi_ref, wr_lo_ref, br_ref, x1_ref, hm_ref, comb_ref) = refs
    ya = _dot(ga_ref[...], wug_ref[...])
    yb = _dot(hb_ref[...], wum_ref[...])
    z = pmg_ref[:, 0:D_MODEL].astype(F32) * ya + pmg_ref[:, D_MODEL:].astype(F32) * yb
    x1 = x_ref[...] + _dot(z.astype(BF16), wo_ref[...])
    x1_ref[...] = x1
    hm = _rms(x1, g2_ref[...])
    hm_hi = hm.astype(BF16)
    if not sparse:
        hm_ref[...] = hm_hi
    hm_lo = (hm - hm_hi.astype(F32)).astype(BF16)
    logits = (_dot(hm_hi, wr_hi_ref[...]) + _dot(hm_lo, wr_hi_ref[...])
              + _dot(hm_hi, wr_lo_ref[...]) + br_ref[...])
    lane = lax.broadcasted_iota(jnp.int32, logits.shape, 1)
    neg = -jnp.inf
    is_g = (lane >= N_EXPERTS) & (lane < N_EXPERTS + N_GROUPS)
    lg = jnp.where(is_g, logits, neg)
    mg = jnp.max(lg, axis=-1, keepdims=True)
    p_top = 1.0 / jnp.sum(jnp.exp(lg - mg), axis=-1, keepdims=True)
    gi = jnp.min(jnp.where(lg == mg, lane, 2 * LANES), axis=-1, keepdims=True) - N_EXPERTS
    group_shift = int(math.log2(EXPERTS_PER_GROUP))
    sel = (lane < N_EXPERTS) & (jnp.right_shift(lane, group_shift) == gi)
    le = jnp.where(sel, logits, neg)
    v1 = jnp.max(le, axis=-1, keepdims=True)
    i1 = jnp.min(jnp.where(le == v1, lane, 2 * LANES), axis=-1, keepdims=True)
    le2 = jnp.where(lane == i1, neg, le)
    v2 = jnp.max(le2, axis=-1, keepdims=True)
    i2 = jnp.min(jnp.where(le2 == v2, lane, 2 * LANES), axis=-1, keepdims=True)
    e2 = jnp.exp(v2 - v1)
    wa = 1.0 / (1.0 + e2)
    wb = e2 / (1.0 + e2)
    if not sparse:
        comb_ref[...] = p_top * (jnp.where(lane == i1, wa, 0.0) + jnp.where(lane == i2, wb, 0.0))
        return

    @pl.when(pl.program_id(0) == 0)
    def _():
        cnt_scr[...] = jnp.zeros_like(cnt_scr)

    tm = logits.shape[0]
    oh1 = lane == i1
    oh2 = lane == i2
    both = jnp.where(oh1 | oh2, 1.0, 0.0)
    cnt = jnp.sum(both, axis=0, keepdims=True)
    cnt = jnp.floor((cnt + (SEG_ALIGN - 1)) * (1.0 / SEG_ALIGN)) * SEG_ALIGN
    lower = _dot(jnp.broadcast_to(cnt, (8, LANES)).astype(BF16), triu_ref[...])[0:1, :]
    lpos = _dot(tril_ref[...], both.astype(BF16)) + lower
    lp1 = jnp.sum(jnp.where(oh1, lpos, 0.0), axis=-1, keepdims=True)
    lp2 = jnp.sum(jnp.where(oh2, lpos, 0.0), axis=-1, keepdims=True)
    pos = lax.broadcasted_iota(jnp.int32, (tm, hs_ref.shape[0]), 1)
    onehot = jnp.where((pos == lp1.astype(jnp.int32)) | (pos == lp2.astype(jnp.int32)), 1.0, 0.0)
    hs_ref[...] = _dot_tn(onehot.astype(BF16), hm_hi).astype(BF16)
    cols = (lp1, lp2, p_top * wa, p_top * wb)
    meta = jnp.zeros(logits.shape, F32)
    for ci, col in enumerate(cols):
        meta = jnp.where(lane == ci, col, meta)
    meta_ref[...] = meta
    row8 = lax.broadcasted_iota(jnp.int32, (8, LANES), 0)
    tab_ref[...] = jnp.where(row8 == 0, cnt, jnp.where(row8 == 1, lower, jnp.where(row8 == 2, cnt_scr[...], 0.0)))
    cnt_scr[...] += cnt


def _merge_call(x2, ga, hb, pmg, wug, wum, wo, g2, wr_hi, wr_lo, br, tm, sparse):
    n = x2.shape[0]
    tile = lambda w: pl.BlockSpec((tm, w), lambda i: (i, 0))
    consts = [wug, wum, wo, g2, wr_hi, wr_lo, br]
    out_specs = [tile(D_MODEL), tile(D_MODEL), tile(LANES)]
    out_shape = [jax.ShapeDtypeStruct((n, D_MODEL), F32),
                 jax.ShapeDtypeStruct((n, D_MODEL), BF16),
                 jax.ShapeDtypeStruct((n, LANES), F32)]
    scratch = []
    if sparse:
        local_rows = 2 * tm + SEG_ALIGN * N_EXPERTS
        tiles = lambda *shape: pl.BlockSpec((None,) + shape, lambda i: (i, 0, 0))
        out_specs[1] = tiles(local_rows, D_MODEL)
        out_shape[1] = jax.ShapeDtypeStruct((n // tm, local_rows, D_MODEL), BF16)
        consts.append(jnp.asarray(np.tril(np.ones((tm, tm), np.float32), -1), BF16))
        consts.append(jnp.asarray(np.triu(np.ones((LANES, LANES), np.float32), 1), BF16))
        out_specs.append(tiles(8, LANES))
        out_shape.append(jax.ShapeDtypeStruct((n // tm, 8, LANES), F32))
        scratch.append(pltpu.VMEM((1, LANES), F32))
    return pl.pallas_call(
        functools.partial(_merge_kernel, sparse=sparse),
        grid=(n // tm,),
        in_specs=[tile(D_MODEL), tile(GLA_V), tile(ML_W), tile(W_MG)]
                 + [_full_spec(x.shape) for x in consts],
        out_specs=out_specs,
        out_shape=out_shape,
        scratch_shapes=scratch,
        compiler_params=pltpu.CompilerParams(
            dimension_semantics=("arbitrary",), vmem_limit_bytes=VMEM_LIMIT),
        name="merge",
    )(x2, ga, hb, pmg, *consts)


MOE_ROWS = 256
SEG_ALIGN = 16
SEG_SIZES = (256, 128, 64, 32, 16)


def _segment_copies(fn, tables, tile, local_ref, buffer, flat_ref, sem, to_flat):
    cnt_ref, loff_ref, gpos_ref = tables

    def body(e, carry):
        idx = tile * N_EXPERTS + e
        n, lo, gp = cnt_ref[idx], loff_ref[idx], gpos_ref[idx]
        for size in SEG_SIZES:
            @pl.when((n & size) != 0)
            def _():
                off = n & ~(2 * size - 1)
                local = local_ref.at[buffer, pl.ds(pl.multiple_of(lo + off, SEG_ALIGN), size), :]
                flat = flat_ref.at[pl.ds(pl.multiple_of(gp + off, SEG_ALIGN), size), :]
                src, dst = (local, flat) if to_flat else (flat, local)
                fn(pltpu.make_async_copy(src, dst, sem))
        return carry

    lax.fori_loop(0, N_EXPERTS, body, 0)


def _dispatch_kernel(cnt_ref, loff_ref, gpos_ref, tail_ref, hs_ref, xs_ref, zero_ref, sem, zsem):
    step = pl.program_id(0)

    @pl.when(step == 0)
    def _():
        zero_ref[...] = jnp.zeros_like(zero_ref)
        n_tiles = xs_ref.shape[0] // MOE_ROWS

        def clear(row):
            start = pl.multiple_of(row, MOE_ROWS)
            return pltpu.make_async_copy(zero_ref, xs_ref.at[pl.ds(start, MOE_ROWS), :], zsem)

        def unused(fn):
            def body(j, carry):
                fn(clear(j * MOE_ROWS))
                return carry
            lax.fori_loop(tail_ref[2 * N_EXPERTS], n_tiles, body, 0)

        for e in range(N_EXPERTS):
            @pl.when(tail_ref[N_EXPERTS + e] > 0)
            def _():
                clear(tail_ref[e]).start()
        unused(lambda c: c.start())
        for e in range(N_EXPERTS):
            @pl.when(tail_ref[N_EXPERTS + e] > 0)
            def _():
                clear(tail_ref[e]).wait()
        unused(lambda c: c.wait())

    tables = (cnt_ref, loff_ref, gpos_ref)
    _segment_copies(lambda c: c.start(), tables, step, hs_ref, 0, xs_ref, sem, True)
    _segment_copies(lambda c: c.wait(), tables, step, hs_ref, 0, xs_ref, sem, True)


def _dispatch_call(tables, tail, hs, n_rows):
    return pl.pallas_call(
        _dispatch_kernel,
        grid_spec=pltpu.PrefetchScalarGridSpec(
            num_scalar_prefetch=4,
            grid=(hs.shape[0],),
            in_specs=[pl.BlockSpec((1,) + hs.shape[1:], lambda i, *_: (i, 0, 0))],
            out_specs=pl.BlockSpec(memory_space=pl.ANY),
            scratch_shapes=[pltpu.VMEM((MOE_ROWS, D_MODEL), hs.dtype),
                            pltpu.SemaphoreType.DMA, pltpu.SemaphoreType.DMA],
        ),
        out_shape=jax.ShapeDtypeStruct((n_rows, D_MODEL), hs.dtype),
        compiler_params=pltpu.CompilerParams(
            dimension_semantics=("arbitrary",), vmem_limit_bytes=VMEM_LIMIT),
        name="moe_dispatch",
    )(*tables, tail, hs)


def _gmm_kernel(te_ref, nv_ref, xs_ref, wg_ref, wu_ref, wd_ref, o_ref, wgu_scr, wd_scr):
    j = pl.program_id(0)
    used = j < nv_ref[0]

    @pl.when(used & ((j == 0) | (te_ref[j] != te_ref[jnp.maximum(j - 1, 0)])))
    def _():
        wgu_scr[:, 0:D_EXPERT] = wg_ref[...].astype(BF16)
        wgu_scr[:, D_EXPERT:] = wu_ref[...].astype(BF16)
        wd_scr[...] = wd_ref[...].astype(BF16)

    @pl.when(used)
    def _():
        au = _dot(xs_ref[...], wgu_scr[...])
        a = au[:, 0:D_EXPERT]
        hh = (a * _sigmoid(a)) * au[:, D_EXPERT:]
        o_ref[...] = _dot(hh.astype(BF16), wd_scr[...]).astype(o_ref.dtype)

    @pl.when(jnp.logical_not(used))
    def _():
        o_ref[...] = jnp.zeros_like(o_ref)


def _gmm_call(tile_expert, n_valid, xs, wg, wu, wd):
    n_tiles = xs.shape[0] // MOE_ROWS
    rows = lambda j, te, nv: (jnp.minimum(j, nv[0] - 1), 0)
    wsel = lambda j, te, nv: (te[j], 0, 0)
    return pl.pallas_call(
        _gmm_kernel,
        grid_spec=pltpu.PrefetchScalarGridSpec(
            num_scalar_prefetch=2,
            grid=(n_tiles,),
            in_specs=[pl.BlockSpec((MOE_ROWS, D_MODEL), rows),
                      pl.BlockSpec((None, D_MODEL, D_EXPERT), wsel),
                      pl.BlockSpec((None, D_MODEL, D_EXPERT), wsel),
                      pl.BlockSpec((None, D_EXPERT, D_MODEL), wsel)],
            out_specs=pl.BlockSpec((MOE_ROWS, D_MODEL), lambda j, te, nv: (j, 0)),
            scratch_shapes=[pltpu.VMEM((D_MODEL, 2 * D_EXPERT), BF16),
                            pltpu.VMEM((D_EXPERT, D_MODEL), BF16)],
        ),
        out_shape=jax.ShapeDtypeStruct(xs.shape, xs.dtype),
        compiler_params=pltpu.CompilerParams(
            dimension_semantics=("arbitrary",), vmem_limit_bytes=VMEM_LIMIT),
        name="moe_grouped",
    )(tile_expert, n_valid, xs, wg, wu, wd)


def _combine_kernel(cnt_ref, loff_ref, gpos_ref, x1_ref, meta_ref, gf_ref, os_ref, y_ref, buf_ref, sem):
    step = pl.program_id(0)
    n_steps = pl.num_programs(0)
    tm = x1_ref.shape[0]
    tables = (cnt_ref, loff_ref, gpos_ref)

    def fetch(fn, tile):
        slot = tile % 2
        _segment_copies(fn, tables, tile, buf_ref, slot, os_ref, sem.at[slot], False)

    @pl.when(step == 0)
    def _():
        buf_ref[...] = jnp.zeros_like(buf_ref)
        fetch(lambda c: c.start(), step)

    @pl.when(step + 1 < n_steps)
    def _():
        fetch(lambda c: c.start(), step + 1)

    fetch(lambda c: c.wait(), step)
    rows = buf_ref[step % 2]
    pos = lax.broadcasted_iota(jnp.int32, (tm, rows.shape[0]), 1)
    lp1 = meta_ref[:, 0:1].astype(jnp.int32)
    lp2 = meta_ref[:, 1:2].astype(jnp.int32)
    q = jnp.where(pos == lp1, meta_ref[:, 2:3], 0.0) + jnp.where(pos == lp2, meta_ref[:, 3:4], 0.0)
    q_hi = q.astype(BF16)
    q_lo = (q - q_hi.astype(F32)).astype(BF16)
    y = x1_ref[...] + _dot(q_hi, rows) + _dot(q_lo, rows)
    y_ref[...] = _rms(y, gf_ref[...])


def _combine_call(tables, x1, meta, gf, out_sorted, local_rows, tm):
    n = x1.shape[0]
    tile = lambda w: pl.BlockSpec((tm, w), lambda i, *_: (i, 0))
    return pl.pallas_call(
        _combine_kernel,
        grid_spec=pltpu.PrefetchScalarGridSpec(
            num_scalar_prefetch=3,
            grid=(n // tm,),
            in_specs=[tile(D_MODEL), tile(LANES), pl.BlockSpec(gf.shape, lambda i, *_: (0, 0)),
                      pl.BlockSpec(memory_space=pl.ANY)],
            out_specs=tile(D_MODEL),
            scratch_shapes=[pltpu.VMEM((2, local_rows, D_MODEL), out_sorted.dtype),
                            pltpu.SemaphoreType.DMA((2,))],
        ),
        out_shape=jax.ShapeDtypeStruct((n, D_MODEL), F32),
        compiler_params=pltpu.CompilerParams(
            dimension_semantics=("arbitrary",), vmem_limit_bytes=VMEM_LIMIT),
        name="moe_combine",
    )(*tables, x1, meta, gf, out_sorted)


def _sparse_moe(x1, hs, meta, tab, p, tm):
    n_tiles = -(-(hs.shape[0] * hs.shape[1]) // MOE_ROWS) + N_EXPERTS
    tab = tab[:, :, :N_EXPERTS].astype(jnp.int32)
    cnt, loff, before = tab[:, 0], tab[:, 1], tab[:, 2]
    total = before[-1] + cnt[-1]
    tiles = (total + (MOE_ROWS - 1)) // MOE_ROWS
    tile_end = jnp.cumsum(tiles)
    tile_start = tile_end - tiles
    n_valid = tile_end[-1:]
    gpos = tile_start[None, :] * MOE_ROWS + before
    j = jnp.minimum(jnp.arange(n_tiles, dtype=jnp.int32), n_valid - 1)
    tile_expert = jnp.sum((j[:, None] >= tile_end[None, :]).astype(jnp.int32), axis=1)
    tail = jnp.concatenate([(tile_end - 1) * MOE_ROWS, total, n_valid]).astype(jnp.int32)
    tables = (cnt.reshape(-1), loff.reshape(-1), gpos.reshape(-1))
    xs = _dispatch_call(tables, tail, hs, n_tiles * MOE_ROWS)
    out_sorted = _gmm_call(tile_expert, n_valid.astype(jnp.int32), xs, p["wg"], p["wu"], p["wd"])
    return _combine_call(tables, x1, meta, p["gf"], out_sorted, hs.shape[1], tm)


def _moe_kernel(x1_ref, hm_ref, comb_ref, wg_ref, wu_ref, wd_ref, gf_ref, y_ref, acc_ref):
    e = pl.program_id(1)

    @pl.when(e == 0)
    def _():
        acc_ref[...] = jnp.zeros_like(acc_ref)

    hm = hm_ref[...]
    a = _dot(hm, wg_ref[...].astype(BF16))
    u = _dot(hm, wu_ref[...].astype(BF16))
    lane = lax.broadcasted_iota(jnp.int32, comb_ref.shape, 1)
    ce = jnp.sum(jnp.where(lane == e, comb_ref[...], 0.0), axis=-1, keepdims=True)
    hh = (a * _sigmoid(a)) * u * ce
    acc_ref[...] += _dot(hh.astype(BF16), wd_ref[...].astype(BF16))

    @pl.when(e == pl.num_programs(1) - 1)
    def _():
        y_ref[...] = _rms(x1_ref[...] + acc_ref[...], gf_ref[...])


def _moe_call(x1, hm, comb, wg, wu, wd, gf, tm):
    n = x1.shape[0]
    tile = lambda w: pl.BlockSpec((tm, w), lambda i, e: (i, 0))
    return pl.pallas_call(
        _moe_kernel,
        grid=(n // tm, N_EXPERTS),
        in_specs=[tile(D_MODEL), tile(D_MODEL), tile(LANES),
                  pl.BlockSpec((None, D_MODEL, D_EXPERT), lambda i, e: (e, 0, 0)),
                  pl.BlockSpec((None, D_MODEL, D_EXPERT), lambda i, e: (e, 0, 0)),
                  pl.BlockSpec((None, D_EXPERT, D_MODEL), lambda i, e: (e, 0, 0)),
                  _full_spec(gf.shape)],
        out_specs=tile(D_MODEL),
        out_shape=jax.ShapeDtypeStruct((n, D_MODEL), F32),
        scratch_shapes=[pltpu.VMEM((tm, D_MODEL), F32)],
        compiler_params=pltpu.CompilerParams(
            dimension_semantics=("arbitrary", "arbitrary"), vmem_limit_bytes=VMEM_LIMIT),
        name="moe",
    )(x1, hm, comb, wg, wu, wd, gf)


def _pad_cols(w, width):
    return jnp.pad(w, ((0, 0), (0, width - w.shape[1])))


def _prep_weights(norm1_g, w_in, gla_w_gate2, gla_b_gate, gla_norm_g, w_up_gla,
                  ml_conv_w, ml_conv_b, ml_b_i, ml_b_f, w_up_ml, w_out,
                  norm2_g, router_g_w, router_g_b, router_e_w, router_e_b,
                  moe_w_gate, moe_w_up, moe_w_down, final_g):
    o_lr = W_GLA
    o_ml = o_lr + GLA_GATE_RANK
    o_if = o_ml + W_ML
    o_mg = o_if + 2 * ML_HEADS
    w_p = jnp.concatenate([
        w_in[:, 0:o_lr], _pad_cols(w_in[:, o_lr:o_ml], LANES),
        w_in[:, o_ml:o_if], _pad_cols(w_in[:, o_if:o_mg], LANES),
        w_in[:, o_mg:]], axis=1).astype(BF16)
    wr = _pad_cols(jnp.concatenate([router_e_w, router_g_w], axis=1), LANES)
    wr_hi = wr.astype(BF16)
    wr_lo = (wr - wr_hi.astype(F32)).astype(BF16)
    br = _pad_cols(jnp.concatenate([router_e_b, router_g_b])[None, :], LANES)
    wg2 = jnp.pad(gla_w_gate2, ((0, LANES - GLA_GATE_RANK), (0, 0)))
    wg2_hi = wg2.astype(BF16)
    return dict(
        g1=norm1_g[None, :], w_p=w_p,
        wg2_p=jnp.stack([wg2_hi, (wg2 - wg2_hi.astype(F32)).astype(BF16)]),
        bg=gla_b_gate[None, :], gn=gla_norm_g[None, :],
        wug=w_up_gla.astype(BF16),
        cw=ml_conv_w, cb=ml_conv_b[None, :],
        bif=_pad_cols(jnp.concatenate([ml_b_i, ml_b_f])[None, :], LANES),
        wum=w_up_ml.astype(BF16), wo=w_out.astype(BF16),
        g2=norm2_g[None, :], wr_hi=wr_hi, wr_lo=wr_lo, br=br,
        wg=moe_w_gate.reshape(N_EXPERTS, D_MODEL, D_EXPERT),
        wu=moe_w_up.reshape(N_EXPERTS, D_MODEL, D_EXPERT),
        wd=moe_w_down.reshape(N_EXPERTS, D_EXPERT, D_MODEL),
        gf=final_g[None, :],
    )


def _layer(x, gla_s0, ml_c0, ml_n0, ml_m0, conv0, p, *, chunk, seq_tile, row_tile, moe_tile, sparse):
    b, l, _ = x.shape
    n = b * l
    x2 = x.reshape(n, D_MODEL)
    qk, vr, plr, mqk, mvo, pif, pmg = _proj_call(x2, p["g1"], p["w_p"], row_tile)
    r3 = lambda a: a.reshape(b, l, a.shape[-1])
    s0 = None if gla_s0 is None else gla_s0.reshape(b, GLA_QK, GLA_DV)
    ga, gla_s = _gla_call(r3(qk), r3(vr), r3(plr), s0, p["wg2_p"], p["bg"], p["gn"], chunk, seq_tile)
    gla_s = gla_s.reshape(b, GLA_HEADS, GLA_DK, GLA_DV)
    if ml_c0 is None:
        caug0 = m0 = None
    else:
        caug0 = jnp.concatenate([ml_c0, jnp.broadcast_to(ml_n0[..., None], ml_c0.shape)], axis=-1)
        m0 = jnp.broadcast_to(_pad_cols(ml_m0, LANES)[:, None, :], (b, 8, LANES))
    hb, caug, m_b, new_conv = _mlstm_call(r3(mqk), r3(mvo), r3(pif), caug0, m0, conv0,
                                          p["cw"], p["cb"], p["bif"], chunk, seq_tile)
    x1, hm, route, *counts = _merge_call(x2, ga.reshape(n, GLA_V), hb.reshape(n, ML_W), pmg,
                                         p["wug"], p["wum"], p["wo"], p["g2"],
                                         p["wr_hi"], p["wr_lo"], p["br"], row_tile, sparse)
    if sparse:
        y = _sparse_moe(x1, hm, route, counts[0], p, moe_tile)
    else:
        y = _moe_call(x1, hm, route, p["wg"], p["wu"], p["wd"], p["gf"], moe_tile)
    return (y.reshape(b, l, D_MODEL), gla_s[None], caug[..., 0:ML_DH][None],
            caug[..., ML_DH][None], m_b[:, 0, 0:ML_HEADS][None], new_conv[None])


def kernel(x_prompt, x_sample, state_gla_S, state_mlstm_C, state_mlstm_n, state_mlstm_m, state_mlstm_conv, norm1_g, w_in, gla_w_gate2, gla_b_gate, gla_norm_g, w_up_gla, ml_conv_w, ml_conv_b, ml_b_i, ml_b_f, w_up_ml, w_out, norm2_g, router_g_w, router_g_b, router_e_w, router_e_b, moe_w_gate, moe_w_up, moe_w_down, final_g):
    assert norm1_g.shape[0] == 1, "single-layer trunk"
    p = _prep_weights(norm1_g[0], w_in[0], gla_w_gate2[0], gla_b_gate[0], gla_norm_g[0], w_up_gla[0],
                      ml_conv_w[0], ml_conv_b[0], ml_b_i[0], ml_b_f[0], w_up_ml[0], w_out[0],
                      norm2_g[0], router_g_w[0], router_g_b[0], router_e_w[0], router_e_b[0],
                      moe_w_gate[0], moe_w_up[0], moe_w_down[0], final_g)
    yp, *sp = _layer(x_prompt, None, None, None, None, None, p,
                     chunk=128, seq_tile=512, row_tile=256, moe_tile=256, sparse=True)
    dec_seq = x_sample.shape[1]
    ns = x_sample.shape[0] * dec_seq
    ys, *ss = _layer(x_sample, state_gla_S[0], state_mlstm_C[0], state_mlstm_n[0], state_mlstm_m[0],
                     state_mlstm_conv[0], p,
                     chunk=dec_seq, seq_tile=dec_seq, row_tile=ns, moe_tile=ns, sparse=False)
    return (yp, ys, *sp, *ss)
```

```python
import functools
import math

import numpy as np
import jax
import jax.numpy as jnp
from jax import lax
from jax.experimental import pallas as pl
from jax.experimental.pallas import tpu as pltpu

D_MODEL = 1024
GLA_HEADS = 4
GLA_DK = 64
GLA_DV = 128
GLA_GATE_RANK = 16
GLA_TAU = 16.0
ML_HEADS = 4
ML_DH = 128
CONV_W = 4
N_GROUPS = 4
EXPERTS_PER_GROUP = 8
N_EXPERTS = N_GROUPS * EXPERTS_PER_GROUP
D_EXPERT = 256
EPS = 1e-6

GLA_QK = GLA_HEADS * GLA_DK
GLA_V = GLA_HEADS * GLA_DV
ML_W = ML_HEADS * ML_DH

LANES = 128
VMEM_LIMIT = 56 * 1024 * 1024

W_GLA = 2 * GLA_QK + 2 * GLA_V
W_ML = 2 * ML_W + ML_W + ML_W
W_MG = 2 * D_MODEL
PROJ_WIDTHS = (W_GLA, LANES, W_ML, LANES, W_MG)

F32 = jnp.float32
BF16 = jnp.bfloat16

PROJ_OUTPUTS = ((2 * GLA_QK, F32), (2 * GLA_V, BF16), (LANES, F32), (2 * ML_W, F32), (2 * ML_W, BF16),
                (LANES, F32), (W_MG, BF16))


def _dot(a, b):
    return jnp.dot(a, b, preferred_element_type=F32)


def _dot_nt(a, b):
    return lax.dot_general(a, b, (((1,), (1,)), ((), ())), preferred_element_type=F32)


def _dot_tn(a, b):
    return lax.dot_general(a, b, (((0,), (0,)), ((), ())), preferred_element_type=F32)


def _split3(x):
    hi = x.astype(BF16)
    r1 = x - hi.astype(F32)
    mid = r1.astype(BF16)
    lo = (r1 - mid.astype(F32)).astype(BF16)
    return hi, mid, lo


def _dot_exact_lhs(m, x):
    hi, mid, lo = _split3(x)
    return _dot(m, hi) + _dot(m, mid) + _dot(m, lo)


def _log_sigmoid(z):
    return jnp.minimum(z, 0.0) - jnp.log1p(jnp.exp(-jnp.abs(z)))


def _sigmoid(z):
    return 1.0 / (1.0 + jnp.exp(-z))


def _rms(x, g):
    return x * lax.rsqrt(jnp.mean(x * x, axis=-1, keepdims=True) + EPS) * g


def _full_spec(shape):
    nd = len(shape)
    return pl.BlockSpec(shape, lambda *_: (0,) * nd)


def _proj_kernel(x_ref, g_ref, w_ref, qk_ref, vr_ref, lr_ref, mqk_ref, mvo_ref, if_ref, mg_ref):
    h = _rms(x_ref[...], g_ref[...]).astype(BF16)
    starts = np.cumsum((0,) + PROJ_WIDTHS)

    def cols(group, lo, hi):
        return _dot(h, w_ref[:, starts[group] + lo:starts[group] + hi])

    qk_ref[...] = cols(0, 0, 2 * GLA_QK)
    vr_ref[:, 0:GLA_V] = cols(0, 2 * GLA_QK, 2 * GLA_QK + GLA_V).astype(BF16)
    r = cols(0, 2 * GLA_QK + GLA_V, W_GLA)
    vr_ref[:, GLA_V:] = (r * _sigmoid(r)).astype(BF16)
    lr_ref[...] = cols(1, 0, LANES)
    mqk_ref[...] = cols(2, 0, 2 * ML_W)
    mvo_ref[:, 0:ML_W] = cols(2, 2 * ML_W, 3 * ML_W).astype(BF16)
    mvo_ref[:, ML_W:] = _sigmoid(cols(2, 3 * ML_W, W_ML)).astype(BF16)
    if_ref[...] = cols(3, 0, LANES)
    mg_ref[...] = _sigmoid(cols(4, 0, W_MG)).astype(BF16)


def _proj_call(x2, g, w_p, tm):
    n = x2.shape[0]
    return pl.pallas_call(
        _proj_kernel,
        grid=(n // tm,),
        in_specs=[pl.BlockSpec((tm, D_MODEL), lambda i: (i, 0)),
                  _full_spec(g.shape), _full_spec(w_p.shape)],
        out_specs=[pl.BlockSpec((tm, w), lambda i: (i, 0)) for w, _ in PROJ_OUTPUTS],
        out_shape=[jax.ShapeDtypeStruct((n, w), dt) for w, dt in PROJ_OUTPUTS],
        compiler_params=pltpu.CompilerParams(
            dimension_semantics=("arbitrary",), vmem_limit_bytes=VMEM_LIMIT),
        name="in_proj",
    )(x2, g, w_p)


def _gla_consts(c):
    nlev = int(math.log2(c))
    assert 1 << nlev == c
    t = np.arange(c)[:, None]
    j = np.arange(c)[None, :]
    lv = np.full((c, c), -1, np.int32)
    for l in range(nlev):
        h = c >> (l + 1)
        upper = (t % (2 * h)) >= h
        same = (j // (2 * h)) == (t // (2 * h))
        s_lower = (j % (2 * h)) < h
        lv[np.broadcast_to(upper, (c, c)) & same & s_lower] = l
    lv[np.eye(c, dtype=bool)] = nlev
    tri = (j <= t).astype(np.float32)
    return jnp.asarray(tri, BF16), jnp.asarray(np.concatenate([lv, lv], axis=1))


def _gla_kernel(*refs, c, t, has_state):
    if has_state:
        (qk_ref, vr_ref, plr_ref, s0_ref, wg2_ref, bg_ref, gn_ref, tri_ref, lv_ref,
         o_ref, sout_ref, s_scr) = refs
    else:
        (qk_ref, vr_ref, plr_ref, wg2_ref, bg_ref, gn_ref, tri_ref, lv_ref,
         o_ref, sout_ref, s_scr) = refs
    nlev = int(math.log2(c))
    step = pl.program_id(1)

    @pl.when(step == 0)
    def _():
        if has_state:
            s_scr[...] = s0_ref[...]
        else:
            s_scr[...] = jnp.zeros_like(s_scr)

    lane_k = lax.broadcasted_iota(jnp.int32, (c, GLA_QK), 1)
    first_of_pair = (lane_k % (2 * GLA_DK)) < GLA_DK
    row_k = lax.broadcasted_iota(jnp.int32, (GLA_QK, GLA_DV), 0)
    row_t = lax.broadcasted_iota(jnp.int32, (c, GLA_QK), 0)

    def block_ref(b, blk, idx):
        if blk >= 8:
            b3 = b.reshape(c // blk, blk, GLA_QK)
            return jnp.broadcast_to(b3[:, idx:idx + 1, :], b3.shape).reshape(c, GLA_QK)
        r = row_t % blk
        out = b
        for sh in range(-idx, blk - idx):
            if sh != 0:
                out = jnp.where(r - idx == sh, pltpu.roll(b, sh % c, axis=0), out)
        return out

    def chunk(ci, carry):
        r0 = ci * c
        rows = pl.ds(r0, c)
        q = qk_ref[rows, 0:GLA_QK] * (GLA_DK ** -0.5)
        k = qk_ref[rows, GLA_QK:2 * GLA_QK]
        v = vr_ref[rows, 0:GLA_V]
        glr = plr_ref[rows, :]
        g_hi = glr.astype(BF16)
        g_lo = (glr - g_hi.astype(F32)).astype(BF16)
        z = (_dot(g_hi, wg2_ref[0]) + _dot(g_lo, wg2_ref[0]) + _dot(g_hi, wg2_ref[1])) + bg_ref[...]
        la = _log_sigmoid(z) * (1.0 / GLA_TAU)
        b = _dot_exact_lhs(tri_ref[...], la)
        b_last = b[c - 1:c, :]
        qe = (q * jnp.exp(b)).astype(BF16)
        kl = (k * jnp.exp(b_last - b)).astype(BF16)
        dcol = jnp.exp(jnp.broadcast_to(b_last, (LANES, GLA_QK)).T)

        k_a = jnp.where(first_of_pair, k, 0.0)
        k_b = k - k_a
        lv2 = lv_ref[...]
        a = [jnp.zeros((c, 2 * c), F32) for _ in range(GLA_HEADS // 2)]
        for l in range(nlev + 1):
            if l < nlev:
                half = c >> (l + 1)
                d = b - block_ref(b, 2 * half, half - 1)
                e = jnp.exp(jnp.minimum(d, -d))
                qt, kta, ktb = q * e, k_a * e, k_b * e
            else:
                qt, kta, ktb = q, k_a, k_b
            qt, kta, ktb = qt.astype(BF16), kta.astype(BF16), ktb.astype(BF16)
            for pr in range(GLA_HEADS // 2):
                ls = slice(pr * 2 * GLA_DK, (pr + 1) * 2 * GLA_DK)
                rhs = jnp.concatenate([kta[:, ls], ktb[:, ls]], axis=0)
                a[pr] = jnp.where(lv2 == l, _dot_nt(qt[:, ls], rhs), a[pr])

        s_all = s_scr[...]
        s_bd = jnp.concatenate(
            [jnp.where((row_k // GLA_DK) == h, s_all, 0.0).astype(BF16) for h in range(GLA_HEADS)], axis=1)
        o_inter = _dot(qe, s_bd)
        u_all = _dot_tn(kl, v)
        for h in range(GLA_HEADS):
            vs = slice(h * GLA_DV, (h + 1) * GLA_DV)
            ks = slice(h * GLA_DK, (h + 1) * GLA_DK)
            a_h = a[h // 2][:, (h % 2) * c:(h % 2 + 1) * c]
            o = _dot(a_h.astype(BF16), v[:, vs]) + o_inter[:, vs]
            on = _rms(o, gn_ref[:, vs])
            gate = vr_ref[rows, GLA_V + h * GLA_DV:GLA_V + (h + 1) * GLA_DV]
            o_ref[rows, vs] = (on * gate.astype(F32)).astype(o_ref.dtype)
            s_scr[ks, :] = dcol[ks, :] * s_all[ks, :] + u_all[ks, vs]
        return carry

    for ci in range(t // c):
        chunk(ci, 0)

    @pl.when(step == pl.num_programs(1) - 1)
    def _():
        sout_ref[...] = s_scr[...]


def _gla_call(qk, vr, plr, s0, wg2_p, bg, gn, c, t):
    b, l, _ = qk.shape
    tri, lv2 = _gla_consts(c)
    has_state = s0 is not None
    tile = lambda w: pl.BlockSpec((None, t, w), lambda bi, i: (bi, i, 0))
    state_spec = pl.BlockSpec((None, GLA_QK, GLA_DV), lambda bi, i: (bi, 0, 0))
    in_specs = [tile(2 * GLA_QK), tile(2 * GLA_V), tile(LANES)]
    args = [qk, vr, plr]
    if has_state:
        in_specs.append(state_spec)
        args.append(s0)
    consts = [wg2_p, bg, gn, tri, lv2]
    in_specs += [_full_spec(x.shape) for x in consts]
    return pl.pallas_call(
        functools.partial(_gla_kernel, c=c, t=t, has_state=has_state),
        grid=(b, l // t),
        in_specs=in_specs,
        out_specs=[tile(GLA_V), state_spec],
        out_shape=[jax.ShapeDtypeStruct((b, l, GLA_V), BF16),
                   jax.ShapeDtypeStruct((b, GLA_QK, GLA_DV), F32)],
        scratch_shapes=[pltpu.VMEM((GLA_QK, GLA_DV), F32)],
        compiler_params=pltpu.CompilerParams(
            dimension_semantics=("arbitrary", "arbitrary"), vmem_limit_bytes=VMEM_LIMIT),
        name="gla",
    )(*args, *consts)


def _mlstm_kernel(*refs, c, t, has_state):
    if has_state:
        (mqk_ref, mvo_ref, pif_ref, c0_ref, n0_ref, m0_ref, cv0_ref, cw_ref, cb_ref, bif_ref, tri_ref, sel_ref,
         o_ref, cout_ref, nout_ref, mout_ref, cvout_ref, c_scr, m_scr, cv_scr, qk_scr) = refs
    else:
        (mqk_ref, mvo_ref, pif_ref, cw_ref, cb_ref, bif_ref, tri_ref, sel_ref,
         o_ref, cout_ref, nout_ref, mout_ref, cvout_ref, c_scr, m_scr, cv_scr, qk_scr) = refs
    step = pl.program_id(1)
    hist = CONV_W - 1
    pad = 8

    @pl.when(step == 0)
    def _():
        if has_state:
            for h in range(ML_HEADS):
                c_scr[h, :, 0:ML_DH] = c0_ref[h]
                c_scr[h, :, ML_DH:] = jnp.broadcast_to(n0_ref[h], (ML_DH, ML_DH)).T
            m_scr[...] = m0_ref[...]
            cv_scr[0:pad - hist, :] = jnp.zeros((pad - hist, 2 * ML_W), F32)
            cv_scr[pad - hist:pad, :] = cv0_ref[...]
        else:
            c_scr[...] = jnp.zeros_like(c_scr)
            m_scr[...] = jnp.zeros_like(m_scr)
            cv_scr[0:pad, :] = jnp.zeros((pad, 2 * ML_W), F32)

    cv_scr[pad:pad + t, :] = mqk_ref[...]
    acc = cb_ref[...] + cv_scr[pad:pad + t, :] * cw_ref[CONV_W - 1:CONV_W, :]
    for d in range(1, CONV_W):
        acc = acc + cv_scr[pad - d:pad - d + t, :] * cw_ref[CONV_W - 1 - d:CONV_W - d, :]
    conv = acc * _sigmoid(acc)
    qk_scr[:, 0:ML_W] = conv[:, 0:ML_W].astype(BF16)
    qk_scr[:, ML_W:] = (conv[:, ML_W:] * (ML_DH ** -0.5)).astype(BF16)
    cv_scr[0:pad, :] = cv_scr[t:t + pad, :]

    gts = pif_ref[...] + bif_ref[...]
    flog = pltpu.roll(_log_sigmoid(gts), LANES - ML_HEADS, axis=1)

    lane = lax.broadcasted_iota(jnp.int32, (c, LANES), 1)
    row_c = lax.broadcasted_iota(jnp.int32, (c, LANES), 0)
    causal = (lax.broadcasted_iota(jnp.int32, (c, c), 1) <= lax.broadcasted_iota(jnp.int32, (c, c), 0))
    ones_v = jnp.ones((c, ML_DH), BF16)

    def slots(pieces):
        out = jnp.zeros((c, LANES), F32)
        for j, piece in enumerate(pieces):
            if not isinstance(piece, float) and j > 0:
                piece = pltpu.roll(piece, ML_HEADS * j, axis=1)
            out = jnp.where((lane >= ML_HEADS * j) & (lane < ML_HEADS * (j + 1)), piece, out)
        return out

    def split3f(x):
        return [p.astype(F32) for p in _split3(x)]

    def chunk(ci, carry):
        r0 = ci * c
        rows = pl.ds(r0, c)
        ip = gts[r0:r0 + c]
        bc = _dot_exact_lhs(tri_ref[...], flog[r0:r0 + c])
        w = ip - bc
        cm = w
        for j in range(int(math.log2(c))):
            sh = 1 << j
            cm = jnp.where(row_c >= sh, jnp.maximum(cm, pltpu.roll(cm, sh, axis=0)), cm)
        mprev = m_scr[0:1, :]
        g = jnp.maximum(mprev, cm)
        g_last = g[c - 1:c, :]
        m_scr[...] = jnp.broadcast_to(bc[c - 1:c, :] + g_last, m_scr.shape)
        lhs_all = slots([1.0, 1.0, 1.0] + split3f(g))
        rhs = slots(split3f(w) + [-1.0, -1.0, -1.0]).astype(BF16)
        y = slots(split3f(mprev - g) + split3f(-(bc + g)) + split3f(w - g_last)).astype(BF16)
        for h in range(ML_HEADS):
            hs = slice(h * ML_DH, (h + 1) * ML_DH)
            qh = qk_scr[rows, h * ML_DH:(h + 1) * ML_DH]
            kh = qk_scr[rows, ML_W + h * ML_DH:ML_W + (h + 1) * ML_DH]
            vaug = jnp.concatenate(
                [mvo_ref[rows, h * ML_DH:(h + 1) * ML_DH], ones_v],
                axis=1)
            lhs = jnp.where(lane % ML_HEADS == h, lhs_all, 0.0).astype(BF16)
            p = jnp.exp(jnp.where(causal, _dot_nt(lhs, rhs), -jnp.inf))
            bx = jnp.exp(_dot(y, sel_ref[h]))
            w_inter = bx[:, 0:ML_DH]
            e_mt = bx[:, ML_DH:2 * ML_DH]
            w_state = bx[:, 2 * ML_DH:3 * ML_DH]
            s = (_dot_nt(qh, kh) * p).astype(BF16)
            caug = c_scr[h]
            nd = _dot(s, vaug) + jnp.concatenate([w_inter, w_inter], axis=1) * _dot(qh, caug.astype(BF16))
            hh = nd[:, 0:ML_DH] / jnp.maximum(jnp.abs(nd[:, ML_DH:]), e_mt)
            o_gate = mvo_ref[rows, ML_W + h * ML_DH:ML_W + (h + 1) * ML_DH]
            o_ref[rows, hs] = (o_gate.astype(F32) * hh).astype(o_ref.dtype)
            ks = (kh.astype(F32) * w_state).astype(BF16)
            dec = w_inter[c - 1:c, :]
            c_scr[h] = jnp.concatenate([dec, dec], axis=1) * caug + _dot_tn(ks, vaug)
        return carry

    for ci in range(t // c):
        chunk(ci, 0)

    @pl.when(step == pl.num_programs(1) - 1)
    def _():
        for h in range(ML_HEADS):
            cout_ref[h] = c_scr[h, :, 0:ML_DH]
            nout_ref[h] = c_scr[h, :, ML_DH:].T[0:8, :]
        mout_ref[...] = m_scr[...]
        cvout_ref[...] = mqk_ref[t - hist:t, :]


def _mlstm_call(mqk, mvo, pif, c0, n0, m0, cv0, cw, cb, bif, c, t):
    b, l, _ = mqk.shape
    has_state = c0 is not None
    tri = jnp.asarray(np.tril(np.ones((c, c), np.float32)), BF16)
    sel = np.zeros((ML_HEADS, LANES, 3 * ML_DH), np.float32)
    for h in range(ML_HEADS):
        for slot in range(9):
            sel[h, ML_HEADS * slot + h, (slot // 3) * ML_DH:(slot // 3 + 1) * ML_DH] = 1.0
    sel = jnp.asarray(sel, BF16)
    tile = lambda w: pl.BlockSpec((None, t, w), lambda bi, i: (bi, i, 0))
    heads = lambda *shape: pl.BlockSpec((None, ML_HEADS) + shape, lambda bi, i: (bi, 0, 0, 0))
    c_spec = heads(ML_DH, ML_DH)
    m_spec = pl.BlockSpec((None, 8, LANES), lambda bi, i: (bi, 0, 0))
    cv_spec = pl.BlockSpec((None, CONV_W - 1, 2 * ML_W), lambda bi, i: (bi, 0, 0))
    in_specs = [tile(2 * ML_W), tile(2 * ML_W), tile(LANES)]
    args = [mqk, mvo, pif]
    if has_state:
        in_specs += [c_spec, heads(1, ML_DH), m_spec, cv_spec]
        args += [c0, n0, m0, cv0]
    consts = [cw, cb, bif, tri, sel]
    in_specs += [_full_spec(x.shape) for x in consts]
    return pl.pallas_call(
        functools.partial(_mlstm_kernel, c=c, t=t, has_state=has_state),
        grid=(b, l // t),
        in_specs=in_specs,
        out_specs=[tile(ML_W), c_spec, heads(8, ML_DH), m_spec, cv_spec],
        out_shape=[jax.ShapeDtypeStruct((b, l, ML_W), BF16),
                   jax.ShapeDtypeStruct((b, ML_HEADS, ML_DH, ML_DH), F32),
                   jax.ShapeDtypeStruct((b, ML_HEADS, 8, ML_DH), F32),
                   jax.ShapeDtypeStruct((b, 8, LANES), F32),
                   jax.ShapeDtypeStruct((b, CONV_W - 1, 2 * ML_W), F32)],
        scratch_shapes=[pltpu.VMEM((ML_HEADS, ML_DH, 2 * ML_DH), F32),
                        pltpu.VMEM((8, LANES), F32),
                        pltpu.VMEM((8 + t, 2 * ML_W), F32),
                        pltpu.VMEM((t, 2 * ML_W), BF16)],
        compiler_params=pltpu.CompilerParams(
            dimension_semantics=("arbitrary", "arbitrary"), vmem_limit_bytes=VMEM_LIMIT),
        name="mlstm",
    )(*args, *consts)


def _merge_kernel(*refs, sparse):
    if sparse:
        (x_ref, ga_ref, hb_ref, pmg_ref, wug_ref, wum_ref, wo_ref, g2_ref,
         wr_hi_ref, wr_lo_ref, br_ref, tril_ref, triu_ref, x1_ref, hs_ref, meta_ref, tab_ref, cnt_scr) = refs
    else:
        (x_ref, ga_ref, hb_ref, pmg_ref, wug_ref, wum_ref, wo_ref, g2_ref,
         wr_hi_ref, wr_lo_ref, br_ref, x1_ref, hm_ref, comb_ref) = refs
    ya = _dot(ga_ref[...], wug_ref[...])
    yb = _dot(hb_ref[...], wum_ref[...])
    z = pmg_ref[:, 0:D_MODEL].astype(F32) * ya + pmg_ref[:, D_MODEL:].astype(F32) * yb
    x1 = x_ref[...] + _dot(z.astype(BF16), wo_ref[...])
    x1_ref[...] = x1
    hm = _rms(x1, g2_ref[...])
    hm_hi = hm.astype(BF16)
    if not sparse:
        hm_ref[...] = hm_hi
    hm_lo = (hm - hm_hi.astype(F32)).astype(BF16)
    hi_both = _dot(hm_hi, jnp.concatenate([wr_hi_ref[...], wr_lo_ref[...]], axis=1))
    logits = hi_both[:, 0:LANES] + hi_both[:, LANES:] + _dot(hm_lo, wr_hi_ref[...]) + br_ref[...]
    lane = lax.broadcasted_iota(jnp.int32, logits.shape, 1)
    neg = -jnp.inf
    is_g = (lane >= N_EXPERTS) & (lane < N_EXPERTS + N_GROUPS)
    lg = jnp.where(is_g, logits, neg)
    mg = jnp.max(lg, axis=-1, keepdims=True)
    p_top = 1.0 / jnp.sum(jnp.exp(lg - mg), axis=-1, keepdims=True)
    gi = jnp.min(jnp.where(lg == mg, lane, 2 * LANES), axis=-1, keepdims=True) - N_EXPERTS
    group_shift = int(math.log2(EXPERTS_PER_GROUP))
    sel = (lane < N_EXPERTS) & (jnp.right_shift(lane, group_shift) == gi)
    le = jnp.where(sel, logits, neg)
    v1 = jnp.max(le, axis=-1, keepdims=True)
    i1 = jnp.min(jnp.where(le == v1, lane, 2 * LANES), axis=-1, keepdims=True)
    le2 = jnp.where(lane == i1, neg, le)
    v2 = jnp.max(le2, axis=-1, keepdims=True)
    i2 = jnp.min(jnp.where(le2 == v2, lane, 2 * LANES), axis=-1, keepdims=True)
    e2 = jnp.exp(v2 - v1)
    wa = 1.0 / (1.0 + e2)
    wb = e2 / (1.0 + e2)
    if not sparse:
        comb_ref[...] = p_top * (jnp.where(lane == i1, wa, 0.0) + jnp.where(lane == i2, wb, 0.0))
        return

    @pl.when(pl.program_id(0) == 0)
    def _():
        cnt_scr[...] = jnp.zeros_like(cnt_scr)

    tm = logits.shape[0]
    oh1 = lane == i1
    oh2 = lane == i2
    both = jnp.where(oh1 | oh2, 1.0, 0.0)
    cnt = jnp.sum(both, axis=0, keepdims=True)
    cnt = jnp.floor((cnt + (SEG_ALIGN - 1)) * (1.0 / SEG_ALIGN)) * SEG_ALIGN
    lower = _dot(jnp.broadcast_to(cnt, (8, LANES)).astype(BF16), triu_ref[...])[0:1, :]
    lpos = _dot(tril_ref[...], both.astype(BF16)) + lower
    lp1 = jnp.sum(jnp.where(oh1, lpos, 0.0), axis=-1, keepdims=True)
    lp2 = jnp.sum(jnp.where(oh2, lpos, 0.0), axis=-1, keepdims=True)
    pos = lax.broadcasted_iota(jnp.int32, (tm, hs_ref.shape[0]), 1)
    onehot = jnp.where((pos == lp1.astype(jnp.int32)) | (pos == lp2.astype(jnp.int32)), 1.0, 0.0)
    hs_ref[...] = _dot_tn(onehot.astype(BF16), hm_hi).astype(BF16)
    cols = (lp1, lp2, p_top * wa, p_top * wb)
    meta = jnp.zeros(logits.shape, F32)
    for ci, col in enumerate(cols):
        meta = jnp.where(lane == ci, col, meta)
    meta_ref[...] = meta
    row8 = lax.broadcasted_iota(jnp.int32, (8, LANES), 0)
    tab_ref[...] = jnp.where(row8 == 0, cnt, jnp.where(row8 == 1, lower, jnp.where(row8 == 2, cnt_scr[...], 0.0)))
    cnt_scr[...] += cnt


def _merge_call(x2, ga, hb, pmg, wug, wum, wo, g2, wr_hi, wr_lo, br, tm, sparse):
    n = x2.shape[0]
    tile = lambda w: pl.BlockSpec((tm, w), lambda i: (i, 0))
    consts = [wug, wum, wo, g2, wr_hi, wr_lo, br]
    out_specs = [tile(D_MODEL), tile(D_MODEL), tile(LANES)]
    out_shape = [jax.ShapeDtypeStruct((n, D_MODEL), F32),
                 jax.ShapeDtypeStruct((n, D_MODEL), BF16),
                 jax.ShapeDtypeStruct((n, LANES), F32)]
    scratch = []
    if sparse:
        local_rows = 2 * tm + SEG_ALIGN * N_EXPERTS
        tiles = lambda *shape: pl.BlockSpec((None,) + shape, lambda i: (i, 0, 0))
        out_specs[1] = tiles(local_rows, D_MODEL)
        out_shape[1] = jax.ShapeDtypeStruct((n // tm, local_rows, D_MODEL), BF16)
        consts.append(jnp.asarray(np.tril(np.ones((tm, tm), np.float32), -1), BF16))
        consts.append(jnp.asarray(np.triu(np.ones((LANES, LANES), np.float32), 1), BF16))
        out_specs.append(tiles(8, LANES))
        out_shape.append(jax.ShapeDtypeStruct((n // tm, 8, LANES), F32))
        scratch.append(pltpu.VMEM((1, LANES), F32))
    return pl.pallas_call(
        functools.partial(_merge_kernel, sparse=sparse),
        grid=(n // tm,),
        in_specs=[tile(D_MODEL), tile(GLA_V), tile(ML_W), tile(W_MG)]
                 + [_full_spec(x.shape) for x in consts],
        out_specs=out_specs,
        out_shape=out_shape,
        scratch_shapes=scratch,
        compiler_params=pltpu.CompilerParams(
            dimension_semantics=("arbitrary",), vmem_limit_bytes=VMEM_LIMIT),
        name="merge",
    )(x2, ga, hb, pmg, *consts)


MOE_ROWS = 512
SEG_ALIGN = 16
SEG_SIZES = (256, 128, 64, 32, 16)


def _segment_copies(fn, tables, tile, local_ref, buffer, flat_ref, sem, to_flat):
    cnt_ref, loff_ref, gpos_ref = tables

    def body(e, carry):
        idx = tile * N_EXPERTS + e
        n, lo, gp = cnt_ref[idx], loff_ref[idx], gpos_ref[idx]
        for size in SEG_SIZES:
            @pl.when((n & size) != 0)
            def _():
                off = n & ~(2 * size - 1)
                local = local_ref.at[buffer, pl.ds(pl.multiple_of(lo + off, SEG_ALIGN), size), :]
                flat = flat_ref.at[pl.ds(pl.multiple_of(gp + off, SEG_ALIGN), size), :]
                src, dst = (local, flat) if to_flat else (flat, local)
                fn(pltpu.make_async_copy(src, dst, sem))
        return carry

    lax.fori_loop(0, N_EXPERTS, body, 0)


def _dispatch_kernel(cnt_ref, loff_ref, gpos_ref, tail_ref, hs_ref, xs_ref, zero_ref, sem, zsem):
    step = pl.program_id(0)

    @pl.when(step == 0)
    def _():
        zero_ref[...] = jnp.zeros_like(zero_ref)
        n_tiles = xs_ref.shape[0] // MOE_ROWS

        def clear(row):
            start = pl.multiple_of(row, MOE_ROWS)
            return pltpu.make_async_copy(zero_ref, xs_ref.at[pl.ds(start, MOE_ROWS), :], zsem)

        def unused(fn):
            def body(j, carry):
                fn(clear(j * MOE_ROWS))
                return carry
            lax.fori_loop(tail_ref[2 * N_EXPERTS], n_tiles, body, 0)

        for e in range(N_EXPERTS):
            @pl.when(tail_ref[N_EXPERTS + e] > 0)
            def _():
                clear(tail_ref[e]).start()
        unused(lambda c: c.start())
        for e in range(N_EXPERTS):
            @pl.when(tail_ref[N_EXPERTS + e] > 0)
            def _():
                clear(tail_ref[e]).wait()
        unused(lambda c: c.wait())

    tables = (cnt_ref, loff_ref, gpos_ref)
    _segment_copies(lambda c: c.start(), tables, step, hs_ref, 0, xs_ref, sem, True)
    _segment_copies(lambda c: c.wait(), tables, step, hs_ref, 0, xs_ref, sem, True)


def _dispatch_call(tables, tail, hs, n_rows):
    return pl.pallas_call(
        _dispatch_kernel,
        grid_spec=pltpu.PrefetchScalarGridSpec(
            num_scalar_prefetch=4,
            grid=(hs.shape[0],),
            in_specs=[pl.BlockSpec((1,) + hs.shape[1:], lambda i, *_: (i, 0, 0))],
            out_specs=pl.BlockSpec(memory_space=pl.ANY),
            scratch_shapes=[pltpu.VMEM((MOE_ROWS, D_MODEL), hs.dtype),
                            pltpu.SemaphoreType.DMA, pltpu.SemaphoreType.DMA],
        ),
        out_shape=jax.ShapeDtypeStruct((n_rows, D_MODEL), hs.dtype),
        compiler_params=pltpu.CompilerParams(
            dimension_semantics=("arbitrary",), vmem_limit_bytes=VMEM_LIMIT),
        name="moe_dispatch",
    )(*tables, tail, hs)


def _gmm_kernel(te_ref, nv_ref, xs_ref, wg_ref, wu_ref, wd_ref, o_ref, wgu_scr, wd_scr):
    j = pl.program_id(0)
    used = j < nv_ref[0]

    @pl.when(used & ((j == 0) | (te_ref[j] != te_ref[jnp.maximum(j - 1, 0)])))
    def _():
        wgu_scr[:, 0:D_EXPERT] = wg_ref[...].astype(BF16)
        wgu_scr[:, D_EXPERT:] = wu_ref[...].astype(BF16)
        wd_scr[...] = wd_ref[...].astype(BF16)

    @pl.when(used)
    def _():
        au = _dot(xs_ref[...], wgu_scr[...])
        a = au[:, 0:D_EXPERT]
        hh = (a * _sigmoid(a)) * au[:, D_EXPERT:]
        o_ref[...] = _dot(hh.astype(BF16), wd_scr[...]).astype(o_ref.dtype)

    @pl.when(jnp.logical_not(used))
    def _():
        o_ref[...] = jnp.zeros_like(o_ref)


def _gmm_call(tile_expert, n_valid, xs, wg, wu, wd):
    n_tiles = xs.shape[0] // MOE_ROWS
    rows = lambda j, te, nv: (jnp.minimum(j, nv[0] - 1), 0)
    wsel = lambda j, te, nv: (te[j], 0, 0)
    return pl.pallas_call(
        _gmm_kernel,
        grid_spec=pltpu.PrefetchScalarGridSpec(
            num_scalar_prefetch=2,
            grid=(n_tiles,),
            in_specs=[pl.BlockSpec((MOE_ROWS, D_MODEL), rows),
                      pl.BlockSpec((None, D_MODEL, D_EXPERT), wsel),
                      pl.BlockSpec((None, D_MODEL, D_EXPERT), wsel),
                      pl.BlockSpec((None, D_EXPERT, D_MODEL), wsel)],
            out_specs=pl.BlockSpec((MOE_ROWS, D_MODEL), lambda j, te, nv: (j, 0)),
            scratch_shapes=[pltpu.VMEM((D_MODEL, 2 * D_EXPERT), BF16),
                            pltpu.VMEM((D_EXPERT, D_MODEL), BF16)],
        ),
        out_shape=jax.ShapeDtypeStruct(xs.shape, xs.dtype),
        compiler_params=pltpu.CompilerParams(
            dimension_semantics=("arbitrary",), vmem_limit_bytes=VMEM_LIMIT),
        name="moe_grouped",
    )(tile_expert, n_valid, xs, wg, wu, wd)


def _combine_kernel(cnt_ref, loff_ref, gpos_ref, x1_ref, meta_ref, gf_ref, os_ref, y_ref, buf_ref, sem):
    step = pl.program_id(0)
    n_steps = pl.num_programs(0)
    tm = x1_ref.shape[0]
    tables = (cnt_ref, loff_ref, gpos_ref)

    def fetch(fn, tile):
        slot = tile % 2
        _segment_copies(fn, tables, tile, buf_ref, slot, os_ref, sem.at[slot], False)

    @pl.when(step == 0)
    def _():
        buf_ref[...] = jnp.zeros_like(buf_ref)
        fetch(lambda c: c.start(), step)

    @pl.when(step + 1 < n_steps)
    def _():
        fetch(lambda c: c.start(), step + 1)

    fetch(lambda c: c.wait(), step)
    rows = buf_ref[step % 2]
    pos = lax.broadcasted_iota(jnp.int32, (tm, rows.shape[0]), 1)
    lp1 = meta_ref[:, 0:1].astype(jnp.int32)
    lp2 = meta_ref[:, 1:2].astype(jnp.int32)
    q = jnp.where(pos == lp1, meta_ref[:, 2:3], 0.0) + jnp.where(pos == lp2, meta_ref[:, 3:4], 0.0)
    y = x1_ref[...] + _dot(q.astype(BF16), rows)
    y_ref[...] = _rms(y, gf_ref[...])


def _combine_call(tables, x1, meta, gf, out_sorted, local_rows, tm):
    n = x1.shape[0]
    tile = lambda w: pl.BlockSpec((tm, w), lambda i, *_: (i, 0))
    return pl.pallas_call(
        _combine_kernel,
        grid_spec=pltpu.PrefetchScalarGridSpec(
            num_scalar_prefetch=3,
            grid=(n // tm,),
            in_specs=[tile(D_MODEL), tile(LANES), pl.BlockSpec(gf.shape, lambda i, *_: (0, 0)),
                      pl.BlockSpec(memory_space=pl.ANY)],
            out_specs=tile(D_MODEL),
            scratch_shapes=[pltpu.VMEM((2, local_rows, D_MODEL), out_sorted.dtype),
                            pltpu.SemaphoreType.DMA((2,))],
        ),
        out_shape=jax.ShapeDtypeStruct((n, D_MODEL), F32),
        compiler_params=pltpu.CompilerParams(
            dimension_semantics=("arbitrary",), vmem_limit_bytes=VMEM_LIMIT),
        name="moe_combine",
    )(*tables, x1, meta, gf, out_sorted)


def _sparse_moe(x1, hs, meta, tab, p, tm):
    n_tiles = -(-(hs.shape[0] * hs.shape[1]) // MOE_ROWS) + N_EXPERTS
    tab = tab[:, :, :N_EXPERTS].astype(jnp.int32)
    cnt, loff, before = tab[:, 0], tab[:, 1], tab[:, 2]
    total = before[-1] + cnt[-1]
    tiles = (total + (MOE_ROWS - 1)) // MOE_ROWS
    tile_end = jnp.cumsum(tiles)
    tile_start = tile_end - tiles
    n_valid = tile_end[-1:]
    gpos = tile_start[None, :] * MOE_ROWS + before
    j = jnp.minimum(jnp.arange(n_tiles, dtype=jnp.int32), n_valid - 1)
    tile_expert = jnp.sum((j[:, None] >= tile_end[None, :]).astype(jnp.int32), axis=1)
    tail = jnp.concatenate([(tile_end - 1) * MOE_ROWS, total, n_valid]).astype(jnp.int32)
    tables = (cnt.reshape(-1), loff.reshape(-1), gpos.reshape(-1))
    xs = _dispatch_call(tables, tail, hs, n_tiles * MOE_ROWS)
    out_sorted = _gmm_call(tile_expert, n_valid.astype(jnp.int32), xs, p["wg"], p["wu"], p["wd"])
    return _combine_call(tables, x1, meta, p["gf"], out_sorted, hs.shape[1], tm)


def _moe_kernel(x1_ref, hm_ref, comb_ref, wg_ref, wu_ref, wd_ref, gf_ref, y_ref, acc_ref):
    e = pl.program_id(1)

    @pl.when(e == 0)
    def _():
        acc_ref[...] = jnp.zeros_like(acc_ref)

    hm = hm_ref[...]
    a = _dot(hm, wg_ref[...].astype(BF16))
    u = _dot(hm, wu_ref[...].astype(BF16))
    lane = lax.broadcasted_iota(jnp.int32, comb_ref.shape, 1)
    ce = jnp.sum(jnp.where(lane == e, comb_ref[...], 0.0), axis=-1, keepdims=True)
    hh = (a * _sigmoid(a)) * u * ce
    acc_ref[...] += _dot(hh.astype(BF16), wd_ref[...].astype(BF16))

    @pl.when(e == pl.num_programs(1) - 1)
    def _():
        y_ref[...] = _rms(x1_ref[...] + acc_ref[...], gf_ref[...])


def _moe_call(x1, hm, comb, wg, wu, wd, gf, tm):
    n = x1.shape[0]
    tile = lambda w: pl.BlockSpec((tm, w), lambda i, e: (i, 0))
    return pl.pallas_call(
        _moe_kernel,
        grid=(n // tm, N_EXPERTS),
        in_specs=[tile(D_MODEL), tile(D_MODEL), tile(LANES),
                  pl.BlockSpec((None, D_MODEL, D_EXPERT), lambda i, e: (e, 0, 0)),
                  pl.BlockSpec((None, D_MODEL, D_EXPERT), lambda i, e: (e, 0, 0)),
                  pl.BlockSpec((None, D_EXPERT, D_MODEL), lambda i, e: (e, 0, 0)),
                  _full_spec(gf.shape)],
        out_specs=tile(D_MODEL),
        out_shape=jax.ShapeDtypeStruct((n, D_MODEL), F32),
        scratch_shapes=[pltpu.VMEM((tm, D_MODEL), F32)],
        compiler_params=pltpu.CompilerParams(
            dimension_semantics=("arbitrary", "arbitrary"), vmem_limit_bytes=VMEM_LIMIT),
        name="moe",
    )(x1, hm, comb, wg, wu, wd, gf)


def _pad_cols(w, width):
    return jnp.pad(w, ((0, 0), (0, width - w.shape[1])))


def _prep_weights(norm1_g, w_in, gla_w_gate2, gla_b_gate, gla_norm_g, w_up_gla,
                  ml_conv_w, ml_conv_b, ml_b_i, ml_b_f, w_up_ml, w_out,
                  norm2_g, router_g_w, router_g_b, router_e_w, router_e_b,
                  moe_w_gate, moe_w_up, moe_w_down, final_g):
    o_lr = W_GLA
    o_ml = o_lr + GLA_GATE_RANK
    o_if = o_ml + W_ML
    o_mg = o_if + 2 * ML_HEADS
    w_p = jnp.concatenate([
        w_in[:, 0:o_lr], _pad_cols(w_in[:, o_lr:o_ml], LANES),
        w_in[:, o_ml:o_if], _pad_cols(w_in[:, o_if:o_mg], LANES),
        w_in[:, o_mg:]], axis=1).astype(BF16)
    wr = _pad_cols(jnp.concatenate([router_e_w, router_g_w], axis=1), LANES)
    wr_hi = wr.astype(BF16)
    wr_lo = (wr - wr_hi.astype(F32)).astype(BF16)
    br = _pad_cols(jnp.concatenate([router_e_b, router_g_b])[None, :], LANES)
    wg2 = jnp.pad(gla_w_gate2, ((0, LANES - GLA_GATE_RANK), (0, 0)))
    wg2_hi = wg2.astype(BF16)
    return dict(
        g1=norm1_g[None, :], w_p=w_p,
        wg2_p=jnp.stack([wg2_hi, (wg2 - wg2_hi.astype(F32)).astype(BF16)]),
        bg=gla_b_gate[None, :], gn=gla_norm_g[None, :],
        wug=w_up_gla.astype(BF16),
        cw=ml_conv_w, cb=ml_conv_b[None, :],
        bif=_pad_cols(jnp.concatenate([ml_b_i, ml_b_f])[None, :], LANES),
        wum=w_up_ml.astype(BF16), wo=w_out.astype(BF16),
        g2=norm2_g[None, :], wr_hi=wr_hi, wr_lo=wr_lo, br=br,
        wg=moe_w_gate.reshape(N_EXPERTS, D_MODEL, D_EXPERT),
        wu=moe_w_up.reshape(N_EXPERTS, D_MODEL, D_EXPERT),
        wd=moe_w_down.reshape(N_EXPERTS, D_EXPERT, D_MODEL),
        gf=final_g[None, :],
    )


def _layer(x, gla_s0, ml_c0, ml_n0, ml_m0, conv0, p, *, chunk, seq_tile, row_tile, moe_tile, sparse):
    b, l, _ = x.shape
    n = b * l
    x2 = x.reshape(n, D_MODEL)
    qk, vr, plr, mqk, mvo, pif, pmg = _proj_call(x2, p["g1"], p["w_p"], row_tile)
    r3 = lambda a: a.reshape(b, l, a.shape[-1])
    s0 = None if gla_s0 is None else gla_s0.reshape(b, GLA_QK, GLA_DV)
    ga, gla_s = _gla_call(r3(qk), r3(vr), r3(plr), s0, p["wg2_p"], p["bg"], p["gn"], chunk, seq_tile)
    gla_s = gla_s.reshape(b, GLA_HEADS, GLA_DK, GLA_DV)
    if ml_c0 is None:
        n0 = m0 = None
    else:
        n0 = ml_n0[:, :, None, :]
        m0 = jnp.broadcast_to(_pad_cols(ml_m0, LANES)[:, None, :], (b, 8, LANES))
    hb, ml_c, ml_n, m_b, new_conv = _mlstm_call(r3(mqk), r3(mvo), r3(pif), ml_c0, n0, m0, conv0,
                                                p["cw"], p["cb"], p["bif"], chunk, seq_tile)
    x1, hm, route, *counts = _merge_call(x2, ga.reshape(n, GLA_V), hb.reshape(n, ML_W), pmg,
                                         p["wug"], p["wum"], p["wo"], p["g2"],
                                         p["wr_hi"], p["wr_lo"], p["br"], row_tile, sparse)
    if sparse:
        y = _sparse_moe(x1, hm, route, counts[0], p, moe_tile)
    else:
        y = _moe_call(x1, hm, route, p["wg"], p["wu"], p["wd"], p["gf"], moe_tile)
    return (y.reshape(b, l, D_MODEL), gla_s[None], ml_c[None],
            ml_n[:, :, 0, :][None], m_b[:, 0, 0:ML_HEADS][None], new_conv[None])


def kernel(x_prompt, x_sample, state_gla_S, state_mlstm_C, state_mlstm_n, state_mlstm_m, state_mlstm_conv, norm1_g, w_in, gla_w_gate2, gla_b_gate, gla_norm_g, w_up_gla, ml_conv_w, ml_conv_b, ml_b_i, ml_b_f, w_up_ml, w_out, norm2_g, router_g_w, router_g_b, router_e_w, router_e_b, moe_w_gate, moe_w_up, moe_w_down, final_g):
    assert norm1_g.shape[0] == 1, "single-layer trunk"
    p = _prep_weights(norm1_g[0], w_in[0], gla_w_gate2[0], gla_b_gate[0], gla_norm_g[0], w_up_gla[0],
                      ml_conv_w[0], ml_conv_b[0], ml_b_i[0], ml_b_f[0], w_up_ml[0], w_out[0],
                      norm2_g[0], router_g_w[0], router_g_b[0], router_e_w[0], router_e_b[0],
                      moe_w_gate[0], moe_w_up[0], moe_w_down[0], final_g)
    yp, *sp = _layer(x_prompt, None, None, None, None, None, p,
                     chunk=128, seq_tile=512, row_tile=256, moe_tile=256, sparse=True)
    dec_seq = x_sample.shape[1]
    ns = x_sample.shape[0] * dec_seq
    ys, *ss = _layer(x_sample, state_gla_S[0], state_mlstm_C[0], state_mlstm_n[0], state_mlstm_m[0],
                     state_mlstm_conv[0], p,
                     chunk=dec_seq, seq_tile=dec_seq, row_tile=ns, moe_tile=ns, sparse=False)
    return (yp, ys, *sp, *ss)
```

```python
import functools
import math

import numpy as np
import jax
import jax.numpy as jnp
from jax import lax
from jax.experimental import pallas as pl
from jax.experimental.pallas import tpu as pltpu

D_MODEL = 1024
GLA_HEADS = 4
GLA_DK = 64
GLA_DV = 128
GLA_GATE_RANK = 16
GLA_TAU = 16.0
ML_HEADS = 4
ML_DH = 128
CONV_W = 4
N_GROUPS = 4
EXPERTS_PER_GROUP = 8
N_EXPERTS = N_GROUPS * EXPERTS_PER_GROUP
D_EXPERT = 256
EPS = 1e-6

GLA_QK = GLA_HEADS * GLA_DK
GLA_V = GLA_HEADS * GLA_DV
ML_W = ML_HEADS * ML_DH

LANES = 128
VMEM_LIMIT = 56 * 1024 * 1024

W_GLA = 2 * GLA_QK + 2 * GLA_V
W_ML = 2 * ML_W + ML_W + ML_W
W_MG = 2 * D_MODEL
PROJ_WIDTHS = (W_GLA, LANES, W_ML, LANES, W_MG)
PROJ_SOURCE_WIDTHS = (W_GLA, GLA_GATE_RANK, W_ML, 2 * ML_HEADS, W_MG)

F32 = jnp.float32
BF16 = jnp.bfloat16

PROJ_OUTPUTS = ((2 * GLA_QK, F32), (2 * GLA_V, BF16), (LANES, F32), (2 * ML_W, F32), (2 * ML_W, BF16),
                (LANES, F32), (W_MG, BF16))


def _dot(a, b):
    return jnp.dot(a, b, preferred_element_type=F32)


def _dot_nt(a, b):
    return lax.dot_general(a, b, (((1,), (1,)), ((), ())), preferred_element_type=F32)


def _dot_tn(a, b):
    return lax.dot_general(a, b, (((0,), (0,)), ((), ())), preferred_element_type=F32)


def _split3(x):
    hi = x.astype(BF16)
    r1 = x - hi.astype(F32)
    mid = r1.astype(BF16)
    lo = (r1 - mid.astype(F32)).astype(BF16)
    return hi, mid, lo


def _dot_exact_lhs(m, x):
    hi, mid, lo = _split3(x)
    return _dot(m, hi) + _dot(m, mid) + _dot(m, lo)


def _log_sigmoid(z):
    return jnp.minimum(z, 0.0) - jnp.log1p(jnp.exp(-jnp.abs(z)))


def _sigmoid(z):
    return 1.0 / (1.0 + jnp.exp(-z))


def _rms(x, g):
    return x * lax.rsqrt(jnp.mean(x * x, axis=-1, keepdims=True) + EPS) * g


def _full_spec(shape):
    nd = len(shape)
    return pl.BlockSpec(shape, lambda *_: (0,) * nd)


def _proj_kernel(x_ref, g_ref, win_ref, qk_ref, vr_ref, lr_ref, mqk_ref, mvo_ref, if_ref, mg_ref, w_ref):
    starts = np.cumsum((0,) + PROJ_WIDTHS)

    @pl.when(pl.program_id(0) == 0)
    def _():
        src = np.cumsum((0,) + PROJ_SOURCE_WIDTHS)
        row_chunk = 256
        for r0 in range(0, D_MODEL, row_chunk):
            rows = slice(r0, r0 + row_chunk)
            for g, width in enumerate(PROJ_SOURCE_WIDTHS):
                w_ref[rows, starts[g]:starts[g] + width] = win_ref[rows, src[g]:src[g] + width].astype(BF16)
                if width < PROJ_WIDTHS[g]:
                    w_ref[rows, starts[g] + width:starts[g + 1]] = jnp.zeros(
                        (row_chunk, PROJ_WIDTHS[g] - width), BF16)

    h = _rms(x_ref[...], g_ref[...]).astype(BF16)

    def cols(group, lo, hi):
        return _dot(h, w_ref[:, starts[group] + lo:starts[group] + hi])

    qk_ref[...] = cols(0, 0, 2 * GLA_QK)
    vr_ref[:, 0:GLA_V] = cols(0, 2 * GLA_QK, 2 * GLA_QK + GLA_V).astype(BF16)
    r = cols(0, 2 * GLA_QK + GLA_V, W_GLA)
    vr_ref[:, GLA_V:] = (r * _sigmoid(r)).astype(BF16)
    lr_ref[...] = cols(1, 0, LANES)
    mqk_ref[...] = cols(2, 0, 2 * ML_W)
    mvo_ref[:, 0:ML_W] = cols(2, 2 * ML_W, 3 * ML_W).astype(BF16)
    mvo_ref[:, ML_W:] = _sigmoid(cols(2, 3 * ML_W, W_ML)).astype(BF16)
    if_ref[...] = cols(3, 0, LANES)
    mg_ref[...] = _sigmoid(cols(4, 0, W_MG)).astype(BF16)


def _proj_call(x2, g, w_in, tm):
    n = x2.shape[0]
    assert w_in.shape == (D_MODEL, sum(PROJ_SOURCE_WIDTHS))
    return pl.pallas_call(
        _proj_kernel,
        grid=(n // tm,),
        in_specs=[pl.BlockSpec((tm, D_MODEL), lambda i: (i, 0)),
                  _full_spec(g.shape),
                  pl.BlockSpec(w_in.shape, lambda i: (0, 0), pipeline_mode=pl.Buffered(1))],
        out_specs=[pl.BlockSpec((tm, w), lambda i: (i, 0)) for w, _ in PROJ_OUTPUTS],
        out_shape=[jax.ShapeDtypeStruct((n, w), dt) for w, dt in PROJ_OUTPUTS],
        scratch_shapes=[pltpu.VMEM((D_MODEL, sum(PROJ_WIDTHS)), BF16)],
        compiler_params=pltpu.CompilerParams(
            dimension_semantics=("arbitrary",), vmem_limit_bytes=VMEM_LIMIT),
        name="in_proj",
    )(x2, g, w_in)


def _gla_consts(c):
    nlev = int(math.log2(c))
    assert 1 << nlev == c
    t = np.arange(c)[:, None]
    j = np.arange(c)[None, :]
    lv = np.full((c, c), -1, np.int32)
    for l in range(nlev):
        h = c >> (l + 1)
        upper = (t % (2 * h)) >= h
        same = (j // (2 * h)) == (t // (2 * h))
        s_lower = (j % (2 * h)) < h
        lv[np.broadcast_to(upper, (c, c)) & same & s_lower] = l
    lv[np.eye(c, dtype=bool)] = nlev
    tri = (j <= t).astype(np.float32)
    return jnp.asarray(tri, BF16), jnp.asarray(np.concatenate([lv, lv], axis=1))


def _gla_kernel(*refs, c, t, has_state):
    if has_state:
        (qk_ref, vr_ref, plr_ref, s0_ref, wg2_ref, bg_ref, gn_ref, tri_ref, lv_ref,
         o_ref, sout_ref, s_scr) = refs
    else:
        (qk_ref, vr_ref, plr_ref, wg2_ref, bg_ref, gn_ref, tri_ref, lv_ref,
         o_ref, sout_ref, s_scr) = refs
    nlev = int(math.log2(c))
    step = pl.program_id(1)

    @pl.when(step == 0)
    def _():
        if has_state:
            s_scr[...] = s0_ref[...]
        else:
            s_scr[...] = jnp.zeros_like(s_scr)

    lane_k = lax.broadcasted_iota(jnp.int32, (c, GLA_QK), 1)
    first_of_pair = (lane_k % (2 * GLA_DK)) < GLA_DK
    row_k = lax.broadcasted_iota(jnp.int32, (GLA_QK, GLA_DV), 0)
    row_t = lax.broadcasted_iota(jnp.int32, (c, GLA_QK), 0)

    def block_ref(b, blk, idx):
        if blk >= 8:
            b3 = b.reshape(c // blk, blk, GLA_QK)
            return jnp.broadcast_to(b3[:, idx:idx + 1, :], b3.shape).reshape(c, GLA_QK)
        r = row_t % blk
        out = b
        for sh in range(-idx, blk - idx):
            if sh != 0:
                out = jnp.where(r - idx == sh, pltpu.roll(b, sh % c, axis=0), out)
        return out

    def chunk(ci, carry):
        r0 = ci * c
        rows = pl.ds(r0, c)
        q = qk_ref[rows, 0:GLA_QK] * (GLA_DK ** -0.5)
        k = qk_ref[rows, GLA_QK:2 * GLA_QK]
        v = vr_ref[rows, 0:GLA_V]
        glr = plr_ref[rows, :]
        g_hi = glr.astype(BF16)
        g_lo = (glr - g_hi.astype(F32)).astype(BF16)
        z = (_dot(g_hi, wg2_ref[0]) + _dot(g_lo, wg2_ref[0]) + _dot(g_hi, wg2_ref[1])) + bg_ref[...]
        la = _log_sigmoid(z) * (1.0 / GLA_TAU)
        b = _dot_exact_lhs(tri_ref[...], la)
        b_last = b[c - 1:c, :]
        qe = (q * jnp.exp(b)).astype(BF16)
        kl = (k * jnp.exp(b_last - b)).astype(BF16)
        dcol = jnp.exp(jnp.broadcast_to(b_last, (LANES, GLA_QK)).T)

        k_a = jnp.where(first_of_pair, k, 0.0)
        k_b = k - k_a
        lv2 = lv_ref[...]
        a = [jnp.zeros((c, 2 * c), F32) for _ in range(GLA_HEADS // 2)]
        for l in range(nlev + 1):
            if l < nlev:
                half = c >> (l + 1)
                d = b - block_ref(b, 2 * half, half - 1)
                e = jnp.exp(jnp.minimum(d, -d))
                qt, kta, ktb = q * e, k_a * e, k_b * e
            else:
                qt, kta, ktb = q, k_a, k_b
            qt, kta, ktb = qt.astype(BF16), kta.astype(BF16), ktb.astype(BF16)
            for pr in range(GLA_HEADS // 2):
                ls = slice(pr * 2 * GLA_DK, (pr + 1) * 2 * GLA_DK)
                rhs = jnp.concatenate([kta[:, ls], ktb[:, ls]], axis=0)
                a[pr] = jnp.where(lv2 == l, _dot_nt(qt[:, ls], rhs), a[pr])

        s_all = s_scr[...]
        s_bd = jnp.concatenate(
            [jnp.where((row_k // GLA_DK) == h, s_all, 0.0).astype(BF16) for h in range(GLA_HEADS)], axis=1)
        o_inter = _dot(qe, s_bd)
        u_all = _dot_tn(kl, v)
        for h in range(GLA_HEADS):
            vs = slice(h * GLA_DV, (h + 1) * GLA_DV)
            ks = slice(h * GLA_DK, (h + 1) * GLA_DK)
            a_h = a[h // 2][:, (h % 2) * c:(h % 2 + 1) * c]
            o = _dot(a_h.astype(BF16), v[:, vs]) + o_inter[:, vs]
            on = _rms(o, gn_ref[:, vs])
            gate = vr_ref[rows, GLA_V + h * GLA_DV:GLA_V + (h + 1) * GLA_DV]
            o_ref[rows, vs] = (on * gate.astype(F32)).astype(o_ref.dtype)
            s_scr[ks, :] = dcol[ks, :] * s_all[ks, :] + u_all[ks, vs]
        return carry

    for ci in range(t // c):
        chunk(ci, 0)

    @pl.when(step == pl.num_programs(1) - 1)
    def _():
        sout_ref[...] = s_scr[...]


def _gla_call(qk, vr, plr, s0, wg2_p, bg, gn, c, t):
    b, l, _ = qk.shape
    tri, lv2 = _gla_consts(c)
    has_state = s0 is not None
    tile = lambda w: pl.BlockSpec((None, t, w), lambda bi, i: (bi, i, 0))
    state_spec = pl.BlockSpec((None, GLA_QK, GLA_DV), lambda bi, i: (bi, 0, 0))
    in_specs = [tile(2 * GLA_QK), tile(2 * GLA_V), tile(LANES)]
    args = [qk, vr, plr]
    if has_state:
        in_specs.append(state_spec)
        args.append(s0)
    consts = [wg2_p, bg, gn, tri, lv2]
    in_specs += [_full_spec(x.shape) for x in consts]
    return pl.pallas_call(
        functools.partial(_gla_kernel, c=c, t=t, has_state=has_state),
        grid=(b, l // t),
        in_specs=in_specs,
        out_specs=[tile(GLA_V), state_spec],
        out_shape=[jax.ShapeDtypeStruct((b, l, GLA_V), BF16),
                   jax.ShapeDtypeStruct((b, GLA_QK, GLA_DV), F32)],
        scratch_shapes=[pltpu.VMEM((GLA_QK, GLA_DV), F32)],
        compiler_params=pltpu.CompilerParams(
            dimension_semantics=("arbitrary", "arbitrary"), vmem_limit_bytes=VMEM_LIMIT),
        name="gla",
    )(*args, *consts)


def _mlstm_kernel(*refs, c, t, has_state):
    if has_state:
        (mqk_ref, mvo_ref, pif_ref, c0_ref, n0_ref, m0_ref, cv0_ref, cw_ref, cb_ref, bif_ref, tri_ref, sel_ref,
         o_ref, cout_ref, nout_ref, mout_ref, cvout_ref, c_scr, m_scr, cv_scr, qk_scr) = refs
    else:
        (mqk_ref, mvo_ref, pif_ref, cw_ref, cb_ref, bif_ref, tri_ref, sel_ref,
         o_ref, cout_ref, nout_ref, mout_ref, cvout_ref, c_scr, m_scr, cv_scr, qk_scr) = refs
    step = pl.program_id(1)
    hist = CONV_W - 1
    pad = 8

    @pl.when(step == 0)
    def _():
        if has_state:
            for h in range(ML_HEADS):
                c_scr[h, :, 0:ML_DH] = c0_ref[h]
                c_scr[h, :, ML_DH:] = jnp.broadcast_to(n0_ref[h], (ML_DH, ML_DH)).T
            m_scr[...] = m0_ref[...]
            cv_scr[0:pad - hist, :] = jnp.zeros((pad - hist, 2 * ML_W), F32)
            cv_scr[pad - hist:pad, :] = cv0_ref[...]
        else:
            c_scr[...] = jnp.zeros_like(c_scr)
            m_scr[...] = jnp.zeros_like(m_scr)
            cv_scr[0:pad, :] = jnp.zeros((pad, 2 * ML_W), F32)

    cv_scr[pad:pad + t, :] = mqk_ref[...]
    acc = cb_ref[...] + cv_scr[pad:pad + t, :] * cw_ref[CONV_W - 1:CONV_W, :]
    for d in range(1, CONV_W):
        acc = acc + cv_scr[pad - d:pad - d + t, :] * cw_ref[CONV_W - 1 - d:CONV_W - d, :]
    conv = acc * _sigmoid(acc)
    qk_scr[:, 0:ML_W] = conv[:, 0:ML_W].astype(BF16)
    qk_scr[:, ML_W:] = (conv[:, ML_W:] * (ML_DH ** -0.5)).astype(BF16)
    cv_scr[0:pad, :] = cv_scr[t:t + pad, :]

    gts = pif_ref[...] + bif_ref[...]
    flog = pltpu.roll(_log_sigmoid(gts), LANES - ML_HEADS, axis=1)

    lane = lax.broadcasted_iota(jnp.int32, (c, LANES), 1)
    row_c = lax.broadcasted_iota(jnp.int32, (c, LANES), 0)
    causal = (lax.broadcasted_iota(jnp.int32, (c, c), 1) <= lax.broadcasted_iota(jnp.int32, (c, c), 0))
    ones_v = jnp.ones((c, ML_DH), BF16)

    def slots(pieces):
        out = jnp.zeros((c, LANES), F32)
        for j, piece in enumerate(pieces):
            if not isinstance(piece, float) and j > 0:
                piece = pltpu.roll(piece, ML_HEADS * j, axis=1)
            out = jnp.where((lane >= ML_HEADS * j) & (lane < ML_HEADS * (j + 1)), piece, out)
        return out

    def split3f(x):
        return [p.astype(F32) for p in _split3(x)]

    def chunk(ci, carry):
        r0 = ci * c
        rows = pl.ds(r0, c)
        ip = gts[r0:r0 + c]
        bc = _dot_exact_lhs(tri_ref[...], flog[r0:r0 + c])
        w = ip - bc
        cm = w
        for j in range(int(math.log2(c))):
            sh = 1 << j
            cm = jnp.where(row_c >= sh, jnp.maximum(cm, pltpu.roll(cm, sh, axis=0)), cm)
        mprev = m_scr[0:1, :]
        g = jnp.maximum(mprev, cm)
        g_last = g[c - 1:c, :]
        m_scr[...] = jnp.broadcast_to(bc[c - 1:c, :] + g_last, m_scr.shape)
        lhs_all = slots([1.0, 1.0, 1.0] + split3f(g))
        rhs = slots(split3f(w) + [-1.0, -1.0, -1.0]).astype(BF16)
        y = slots(split3f(mprev - g) + split3f(-(bc + g)) + split3f(w - g_last)).astype(BF16)
        for h in range(ML_HEADS):
            hs = slice(h * ML_DH, (h + 1) * ML_DH)
            qh = qk_scr[rows, h * ML_DH:(h + 1) * ML_DH]
            kh = qk_scr[rows, ML_W + h * ML_DH:ML_W + (h + 1) * ML_DH]
            vaug = jnp.concatenate(
                [mvo_ref[rows, h * ML_DH:(h + 1) * ML_DH], ones_v],
                axis=1)
            lhs = jnp.where(lane % ML_HEADS == h, lhs_all, 0.0).astype(BF16)
            p = jnp.exp(jnp.where(causal, _dot_nt(lhs, rhs), -jnp.inf))
            bx = jnp.exp(_dot(y, sel_ref[h]))
            w_inter = bx[:, 0:ML_DH]
            e_mt = bx[:, ML_DH:2 * ML_DH]
            w_state = bx[:, 2 * ML_DH:3 * ML_DH]
            s = (_dot_nt(qh, kh) * p).astype(BF16)
            caug = c_scr[h]
            nd = _dot(s, vaug) + jnp.concatenate([w_inter, w_inter], axis=1) * _dot(qh, caug.astype(BF16))
            hh = nd[:, 0:ML_DH] / jnp.maximum(jnp.abs(nd[:, ML_DH:]), e_mt)
            o_gate = mvo_ref[rows, ML_W + h * ML_DH:ML_W + (h + 1) * ML_DH]
            o_ref[rows, hs] = (o_gate.astype(F32) * hh).astype(o_ref.dtype)
            ks = (kh.astype(F32) * w_state).astype(BF16)
            dec = w_inter[c - 1:c, :]
            c_scr[h] = jnp.concatenate([dec, dec], axis=1) * caug + _dot_tn(ks, vaug)
        return carry

    for ci in range(t // c):
        chunk(ci, 0)

    @pl.when(step == pl.num_programs(1) - 1)
    def _():
        for h in range(ML_HEADS):
            cout_ref[h] = c_scr[h, :, 0:ML_DH]
            nout_ref[h] = c_scr[h, :, ML_DH:].T[0:8, :]
        mout_ref[...] = m_scr[...]
        cvout_ref[...] = mqk_ref[t - hist:t, :]


def _mlstm_call(mqk, mvo, pif, c0, n0, m0, cv0, cw, cb, bif, c, t):
    b, l, _ = mqk.shape
    has_state = c0 is not None
    tri = jnp.asarray(np.tril(np.ones((c, c), np.float32)), BF16)
    sel = np.zeros((ML_HEADS, LANES, 3 * ML_DH), np.float32)
    for h in range(ML_HEADS):
        for slot in range(9):
            sel[h, ML_HEADS * slot + h, (slot // 3) * ML_DH:(slot // 3 + 1) * ML_DH] = 1.0
    sel = jnp.asarray(sel, BF16)
    tile = lambda w: pl.BlockSpec((None, t, w), lambda bi, i: (bi, i, 0))
    heads = lambda *shape: pl.BlockSpec((None, ML_HEADS) + shape, lambda bi, i: (bi, 0, 0, 0))
    c_spec = heads(ML_DH, ML_DH)
    m_spec = pl.BlockSpec((None, 8, LANES), lambda bi, i: (bi, 0, 0))
    cv_spec = pl.BlockSpec((None, CONV_W - 1, 2 * ML_W), lambda bi, i: (bi, 0, 0))
    in_specs = [tile(2 * ML_W), tile(2 * ML_W), tile(LANES)]
    args = [mqk, mvo, pif]
    if has_state:
        in_specs += [c_spec, heads(1, ML_DH), m_spec, cv_spec]
        args += [c0, n0, m0, cv0]
    consts = [cw, cb, bif, tri, sel]
    in_specs += [_full_spec(x.shape) for x in consts]
    return pl.pallas_call(
        functools.partial(_mlstm_kernel, c=c, t=t, has_state=has_state),
        grid=(b, l // t),
        in_specs=in_specs,
        out_specs=[tile(ML_W), c_spec, heads(8, ML_DH), m_spec, cv_spec],
        out_shape=[jax.ShapeDtypeStruct((b, l, ML_W), BF16),
                   jax.ShapeDtypeStruct((b, ML_HEADS, ML_DH, ML_DH), F32),
                   jax.ShapeDtypeStruct((b, ML_HEADS, 8, ML_DH), F32),
                   jax.ShapeDtypeStruct((b, 8, LANES), F32),
                   jax.ShapeDtypeStruct((b, CONV_W - 1, 2 * ML_W), F32)],
        scratch_shapes=[pltpu.VMEM((ML_HEADS, ML_DH, 2 * ML_DH), F32),
                        pltpu.VMEM((8, LANES), F32),
                        pltpu.VMEM((8 + t, 2 * ML_W), F32),
                        pltpu.VMEM((t, 2 * ML_W), BF16)],
        compiler_params=pltpu.CompilerParams(
            dimension_semantics=("arbitrary", "arbitrary"), vmem_limit_bytes=VMEM_LIMIT),
        name="mlstm",
    )(*args, *consts)


def _merge_kernel(x_ref, ga_ref, hb_ref, pmg_ref, wug_ref, wum_ref, wo_ref, g2_ref,
                  wr_hi_ref, wr_lo_ref, br_ref, tril_ref, triu_ref, cnt0_ref,
                  x1_ref, hs_ref, meta_ref, tab_ref, cnt_scr):
    ya = _dot(ga_ref[...], wug_ref[...])
    yb = _dot(hb_ref[...], wum_ref[...])
    z = pmg_ref[:, 0:D_MODEL].astype(F32) * ya + pmg_ref[:, D_MODEL:].astype(F32) * yb
    x1 = x_ref[...] + _dot(z.astype(BF16), wo_ref[...])
    x1_ref[...] = x1
    hm = _rms(x1, g2_ref[...])
    hm_hi = hm.astype(BF16)
    hm_lo = (hm - hm_hi.astype(F32)).astype(BF16)
    hi_both = _dot(hm_hi, jnp.concatenate([wr_hi_ref[...], wr_lo_ref[...]], axis=1))
    logits = hi_both[:, 0:LANES] + hi_both[:, LANES:] + _dot(hm_lo, wr_hi_ref[...]) + br_ref[...]
    lane = lax.broadcasted_iota(jnp.int32, logits.shape, 1)
    neg = -jnp.inf
    is_g = (lane >= N_EXPERTS) & (lane < N_EXPERTS + N_GROUPS)
    lg = jnp.where(is_g, logits, neg)
    mg = jnp.max(lg, axis=-1, keepdims=True)
    p_top = 1.0 / jnp.sum(jnp.exp(lg - mg), axis=-1, keepdims=True)
    gi = jnp.min(jnp.where(lg == mg, lane, 2 * LANES), axis=-1, keepdims=True) - N_EXPERTS
    group_shift = int(math.log2(EXPERTS_PER_GROUP))
    sel = (lane < N_EXPERTS) & (jnp.right_shift(lane, group_shift) == gi)
    le = jnp.where(sel, logits, neg)
    v1 = jnp.max(le, axis=-1, keepdims=True)
    i1 = jnp.min(jnp.where(le == v1, lane, 2 * LANES), axis=-1, keepdims=True)
    le2 = jnp.where(lane == i1, neg, le)
    v2 = jnp.max(le2, axis=-1, keepdims=True)
    i2 = jnp.min(jnp.where(le2 == v2, lane, 2 * LANES), axis=-1, keepdims=True)
    e2 = jnp.exp(v2 - v1)
    wa = 1.0 / (1.0 + e2)
    wb = e2 / (1.0 + e2)

    @pl.when(pl.program_id(0) == 0)
    def _():
        cnt_scr[...] = cnt0_ref[...]

    tm = logits.shape[0]
    oh1 = lane == i1
    oh2 = lane == i2
    both = jnp.where(oh1 | oh2, 1.0, 0.0)
    cnt = jnp.sum(both, axis=0, keepdims=True)
    cnt = jnp.floor((cnt + (SEG_ALIGN - 1)) * (1.0 / SEG_ALIGN)) * SEG_ALIGN
    lower = _dot(jnp.broadcast_to(cnt, (8, LANES)).astype(BF16), triu_ref[...])[0:1, :]
    lpos = _dot(tril_ref[...], both.astype(BF16)) + lower
    lp1 = jnp.sum(jnp.where(oh1, lpos, 0.0), axis=-1, keepdims=True)
    lp2 = jnp.sum(jnp.where(oh2, lpos, 0.0), axis=-1, keepdims=True)
    pos = lax.broadcasted_iota(jnp.int32, (tm, hs_ref.shape[0]), 1)
    onehot = jnp.where((pos == lp1.astype(jnp.int32)) | (pos == lp2.astype(jnp.int32)), 1.0, 0.0)
    hs_ref[...] = _dot_tn(onehot.astype(BF16), hm_hi).astype(BF16)
    cols = (lp1, lp2, p_top * wa, p_top * wb)
    meta = jnp.zeros(logits.shape, F32)
    for ci, col in enumerate(cols):
        meta = jnp.where(lane == ci, col, meta)
    meta_ref[...] = meta
    row8 = lax.broadcasted_iota(jnp.int32, (8, LANES), 0)
    tab_ref[...] = jnp.where(row8 == 0, cnt, jnp.where(row8 == 1, lower, jnp.where(row8 == 2, cnt_scr[...], 0.0)))
    cnt_scr[...] += cnt


def _merge_call(x2, ga, hb, pmg, wug, wum, wo, g2, wr_hi, wr_lo, br, cnt0, tm):
    n = x2.shape[0]
    tile = lambda w: pl.BlockSpec((tm, w), lambda i: (i, 0))
    tiles = lambda *shape: pl.BlockSpec((None,) + shape, lambda i: (i, 0, 0))
    local_rows = 2 * tm + SEG_ALIGN * N_EXPERTS
    consts = [wug, wum, wo, g2, wr_hi, wr_lo, br,
              jnp.asarray(np.tril(np.ones((tm, tm), np.float32), -1), BF16),
              jnp.asarray(np.triu(np.ones((LANES, LANES), np.float32), 1), BF16),
              cnt0]
    return pl.pallas_call(
        _merge_kernel,
        grid=(n // tm,),
        in_specs=[tile(D_MODEL), tile(GLA_V), tile(ML_W), tile(W_MG)]
                 + [_full_spec(x.shape) for x in consts],
        out_specs=[tile(D_MODEL), tiles(local_rows, D_MODEL), tile(LANES), tiles(8, LANES)],
        out_shape=[jax.ShapeDtypeStruct((n, D_MODEL), F32),
                   jax.ShapeDtypeStruct((n // tm, local_rows, D_MODEL), BF16),
                   jax.ShapeDtypeStruct((n, LANES), F32),
                   jax.ShapeDtypeStruct((n // tm, 8, LANES), F32)],
        scratch_shapes=[pltpu.VMEM((1, LANES), F32)],
        compiler_params=pltpu.CompilerParams(
            dimension_semantics=("arbitrary",), vmem_limit_bytes=VMEM_LIMIT),
        name="merge",
    )(x2, ga, hb, pmg, *consts)


MOE_ROWS = 512
SEG_ALIGN = 16
SEG_SIZES = (256, 128, 64, 32, 16)


def _segment_copies(fn, tables, tile, local_ref, buffer, flat_ref, sem, to_flat):
    cnt_ref, loff_ref, gpos_ref = tables

    def body(e, carry):
        idx = tile * N_EXPERTS + e
        n, lo, gp = cnt_ref[idx], loff_ref[idx], gpos_ref[idx]
        for size in SEG_SIZES:
            @pl.when((n & size) != 0)
            def _():
                off = n & ~(2 * size - 1)
                local = local_ref.at[buffer, pl.ds(pl.multiple_of(lo + off, SEG_ALIGN), size), :]
                flat = flat_ref.at[pl.ds(pl.multiple_of(gp + off, SEG_ALIGN), size), :]
                src, dst = (local, flat) if to_flat else (flat, local)
                fn(pltpu.make_async_copy(src, dst, sem))
        return carry

    lax.fori_loop(0, N_EXPERTS, body, 0)


def _dispatch_kernel(*refs, tile0, first):
    if first:
        cnt_ref, loff_ref, gpos_ref, tail_ref, hs_ref, xs_ref, zero_ref, sem, zsem = refs
    else:
        cnt_ref, loff_ref, gpos_ref, tail_ref, hs_ref, _, xs_ref, zero_ref, sem, zsem = refs
    step = pl.program_id(0)

    @pl.when((step == 0) & first)
    def _():
        zero_ref[...] = jnp.zeros_like(zero_ref)
        n_tiles = xs_ref.shape[0] // MOE_ROWS

        def clear(row):
            start = pl.multiple_of(row, MOE_ROWS)
            return pltpu.make_async_copy(zero_ref, xs_ref.at[pl.ds(start, MOE_ROWS), :], zsem)

        def unused(fn):
            def body(j, carry):
                fn(clear(j * MOE_ROWS))
                return carry
            lax.fori_loop(tail_ref[2 * N_EXPERTS], n_tiles, body, 0)

        for e in range(N_EXPERTS):
            @pl.when(tail_ref[N_EXPERTS + e] > 0)
            def _():
                clear(tail_ref[e]).start()
        unused(lambda c: c.start())
        for e in range(N_EXPERTS):
            @pl.when(tail_ref[N_EXPERTS + e] > 0)
            def _():
                clear(tail_ref[e]).wait()
        unused(lambda c: c.wait())

    tables = (cnt_ref, loff_ref, gpos_ref)
    _segment_copies(lambda c: c.start(), tables, tile0 + step, hs_ref, 0, xs_ref, sem, True)
    _segment_copies(lambda c: c.wait(), tables, tile0 + step, hs_ref, 0, xs_ref, sem, True)


def _dispatch_call(tables, tail, hs, tile0, n_rows, xs=None):
    first = xs is None
    any_spec = pl.BlockSpec(memory_space=pl.ANY)
    in_specs = [pl.BlockSpec((1,) + hs.shape[1:], lambda i, *_: (i, 0, 0))]
    args = [*tables, tail, hs]
    if not first:
        in_specs.append(any_spec)
        args.append(xs)
    return pl.pallas_call(
        functools.partial(_dispatch_kernel, tile0=tile0, first=first),
        grid_spec=pltpu.PrefetchScalarGridSpec(
            num_scalar_prefetch=4,
            grid=(hs.shape[0],),
            in_specs=in_specs,
            out_specs=any_spec,
            scratch_shapes=[pltpu.VMEM((MOE_ROWS, D_MODEL), hs.dtype),
                            pltpu.SemaphoreType.DMA, pltpu.SemaphoreType.DMA],
        ),
        out_shape=jax.ShapeDtypeStruct((n_rows, D_MODEL), hs.dtype),
        input_output_aliases={} if first else {len(args) - 1: 0},
        compiler_params=pltpu.CompilerParams(
            dimension_semantics=("arbitrary",), vmem_limit_bytes=VMEM_LIMIT),
        name="moe_dispatch",
    )(*args)


def _gmm_kernel(te_ref, nv_ref, xs_ref, wg_ref, wu_ref, wd_ref, o_ref, wgu_scr, wd_scr):
    j = pl.program_id(0)
    used = j < nv_ref[0]

    @pl.when(used & ((j == 0) | (te_ref[j] != te_ref[jnp.maximum(j - 1, 0)])))
    def _():
        wgu_scr[:, 0:D_EXPERT] = wg_ref[...].astype(BF16)
        wgu_scr[:, D_EXPERT:] = wu_ref[...].astype(BF16)
        wd_scr[...] = wd_ref[...].astype(BF16)

    @pl.when(used)
    def _():
        au = _dot(xs_ref[...], wgu_scr[...])
        a = au[:, 0:D_EXPERT]
        hh = (a * _sigmoid(a)) * au[:, D_EXPERT:]
        o_ref[...] = _dot(hh.astype(BF16), wd_scr[...]).astype(o_ref.dtype)

    @pl.when(jnp.logical_not(used))
    def _():
        o_ref[...] = jnp.zeros_like(o_ref)


def _gmm_call(tile_expert, n_valid, xs, wg, wu, wd):
    n_tiles = xs.shape[0] // MOE_ROWS
    rows = lambda j, te, nv: (jnp.minimum(j, nv[0] - 1), 0)
    wsel = lambda j, te, nv: (te[j], 0, 0)
    return pl.pallas_call(
        _gmm_kernel,
        grid_spec=pltpu.PrefetchScalarGridSpec(
            num_scalar_prefetch=2,
            grid=(n_tiles,),
            in_specs=[pl.BlockSpec((MOE_ROWS, D_MODEL), rows),
                      pl.BlockSpec((None, D_MODEL, D_EXPERT), wsel),
                      pl.BlockSpec((None, D_MODEL, D_EXPERT), wsel),
                      pl.BlockSpec((None, D_EXPERT, D_MODEL), wsel)],
            out_specs=pl.BlockSpec((MOE_ROWS, D_MODEL), lambda j, te, nv: (j, 0)),
            scratch_shapes=[pltpu.VMEM((D_MODEL, 2 * D_EXPERT), BF16),
                            pltpu.VMEM((D_EXPERT, D_MODEL), BF16)],
        ),
        out_shape=jax.ShapeDtypeStruct(xs.shape, xs.dtype),
        compiler_params=pltpu.CompilerParams(
            dimension_semantics=("arbitrary",), vmem_limit_bytes=VMEM_LIMIT),
        name="moe_grouped",
    )(tile_expert, n_valid, xs, wg, wu, wd)


def _combine_kernel(cnt_ref, loff_ref, gpos_ref, x1_ref, meta_ref, gf_ref, os_ref, y_ref, buf_ref, sem,
                    *, tile0):
    step = pl.program_id(0)
    n_steps = pl.num_programs(0)
    tm = x1_ref.shape[0]
    tables = (cnt_ref, loff_ref, gpos_ref)

    def fetch(fn, i):
        slot = i % 2
        _segment_copies(fn, tables, tile0 + i, buf_ref, slot, os_ref, sem.at[slot], False)

    @pl.when(step == 0)
    def _():
        buf_ref[...] = jnp.zeros_like(buf_ref)
        fetch(lambda c: c.start(), step)

    @pl.when(step + 1 < n_steps)
    def _():
        fetch(lambda c: c.start(), step + 1)

    fetch(lambda c: c.wait(), step)
    rows = buf_ref[step % 2]
    pos = lax.broadcasted_iota(jnp.int32, (tm, rows.shape[0]), 1)
    lp1 = meta_ref[:, 0:1].astype(jnp.int32)
    lp2 = meta_ref[:, 1:2].astype(jnp.int32)
    q = jnp.where(pos == lp1, meta_ref[:, 2:3], 0.0) + jnp.where(pos == lp2, meta_ref[:, 3:4], 0.0)
    y = x1_ref[...] + _dot(q.astype(BF16), rows)
    y_ref[...] = _rms(y, gf_ref[...])


def _combine_call(tables, x1, meta, gf, out_sorted, tile0, local_rows, tm):
    n = x1.shape[0]
    tile = lambda w: pl.BlockSpec((tm, w), lambda i, *_: (i, 0))
    return pl.pallas_call(
        functools.partial(_combine_kernel, tile0=tile0),
        grid_spec=pltpu.PrefetchScalarGridSpec(
            num_scalar_prefetch=3,
            grid=(n // tm,),
            in_specs=[tile(D_MODEL), tile(LANES), pl.BlockSpec(gf.shape, lambda i, *_: (0, 0)),
                      pl.BlockSpec(memory_space=pl.ANY)],
            out_specs=tile(D_MODEL),
            scratch_shapes=[pltpu.VMEM((2, local_rows, D_MODEL), out_sorted.dtype),
                            pltpu.SemaphoreType.DMA((2,))],
        ),
        out_shape=jax.ShapeDtypeStruct((n, D_MODEL), F32),
        compiler_params=pltpu.CompilerParams(
            dimension_semantics=("arbitrary",), vmem_limit_bytes=VMEM_LIMIT),
        name="moe_combine",
    )(*tables, x1, meta, gf, out_sorted)


def _sparse_moe(parts, p, tm):
    local_rows = parts[0][1].shape[1]
    token_tiles = [part[1].shape[0] for part in parts]
    n_tiles = -(-(sum(token_tiles) * local_rows) // MOE_ROWS) + N_EXPERTS
    tab = jnp.concatenate([part[3] for part in parts], axis=0)
    tab = tab[:, :, :N_EXPERTS].astype(jnp.int32)
    cnt, loff, before = tab[:, 0], tab[:, 1], tab[:, 2]
    total = before[-1] + cnt[-1]
    tiles = (total + (MOE_ROWS - 1)) // MOE_ROWS
    tile_end = jnp.cumsum(tiles)
    tile_start = tile_end - tiles
    n_valid = tile_end[-1:]
    gpos = tile_start[None, :] * MOE_ROWS + before
    j = jnp.minimum(jnp.arange(n_tiles, dtype=jnp.int32), n_valid - 1)
    tile_expert = jnp.sum((j[:, None] >= tile_end[None, :]).astype(jnp.int32), axis=1)
    tail = jnp.concatenate([(tile_end - 1) * MOE_ROWS, total, n_valid]).astype(jnp.int32)
    tables = (cnt.reshape(-1), loff.reshape(-1), gpos.reshape(-1))
    xs = None
    for part, tile0 in zip(parts, np.cumsum([0] + token_tiles[:-1])):
        xs = _dispatch_call(tables, tail, part[1], int(tile0), n_tiles * MOE_ROWS, xs)
    out_sorted = _gmm_call(tile_expert, n_valid.astype(jnp.int32), xs, p["wg"], p["wu"], p["wd"])
    return [_combine_call(tables, part[0], part[2], p["gf"], out_sorted, int(tile0), local_rows, tm)
            for part, tile0 in zip(parts, np.cumsum([0] + token_tiles[:-1]))]


def _pad_cols(w, width):
    return jnp.pad(w, ((0, 0), (0, width - w.shape[1])))


def _prep_weights(norm1_g, w_in, gla_w_gate2, gla_b_gate, gla_norm_g, w_up_gla,
                  ml_conv_w, ml_conv_b, ml_b_i, ml_b_f, w_up_ml, w_out,
                  norm2_g, router_g_w, router_g_b, router_e_w, router_e_b,
                  moe_w_gate, moe_w_up, moe_w_down, final_g):
    wr =_pad_cols(jnp.concatenate([router_e_w, router_g_w], axis=1), LANES)
    wr_hi = wr.astype(BF16)
    wr_lo = (wr - wr_hi.astype(F32)).astype(BF16)
    br = _pad_cols(jnp.concatenate([router_e_b, router_g_b])[None, :], LANES)
    wg2 = jnp.pad(gla_w_gate2, ((0, LANES - GLA_GATE_RANK), (0, 0)))
    wg2_hi = wg2.astype(BF16)
    return dict(
        g1=norm1_g[None, :], w_in=w_in,
        wg2_p=jnp.stack([wg2_hi, (wg2 - wg2_hi.astype(F32)).astype(BF16)]),
        bg=gla_b_gate[None, :], gn=gla_norm_g[None, :],
        wug=w_up_gla.astype(BF16),
        cw=ml_conv_w, cb=ml_conv_b[None, :],
        bif=_pad_cols(jnp.concatenate([ml_b_i, ml_b_f])[None, :], LANES),
        wum=w_up_ml.astype(BF16), wo=w_out.astype(BF16),
        g2=norm2_g[None, :], wr_hi=wr_hi, wr_lo=wr_lo, br=br,
        wg=moe_w_gate.reshape(N_EXPERTS, D_MODEL, D_EXPERT),
        wu=moe_w_up.reshape(N_EXPERTS, D_MODEL, D_EXPERT),
        wd=moe_w_down.reshape(N_EXPERTS, D_EXPERT, D_MODEL),
        gf=final_g[None, :],
    )


def _mixers(x, gla_s0, ml_c0, ml_n0, ml_m0, conv0, p, placed, *, chunk, seq_tile, row_tile):
    b, l, _ = x.shape
    n = b * l
    x2 = x.reshape(n, D_MODEL)
    qk, vr, plr, mqk, mvo, pif, pmg = _proj_call(x2, p["g1"], p["w_in"], row_tile)
    r3 = lambda a: a.reshape(b, l, a.shape[-1])
    s0 = None if gla_s0 is None else gla_s0.reshape(b, GLA_QK, GLA_DV)
    ga, gla_s = _gla_call(r3(qk), r3(vr), r3(plr), s0, p["wg2_p"], p["bg"], p["gn"], chunk, seq_tile)
    gla_s = gla_s.reshape(b, GLA_HEADS, GLA_DK, GLA_DV)
    if ml_c0 is None:
        n0 = m0 = None
    else:
        n0 = ml_n0[:, :, None, :]
        m0 = jnp.broadcast_to(_pad_cols(ml_m0, LANES)[:, None, :], (b, 8, LANES))
    hb, ml_c, ml_n, m_b, new_conv = _mlstm_call(r3(mqk), r3(mvo), r3(pif), ml_c0, n0, m0, conv0,
                                                p["cw"], p["cb"], p["bif"], chunk, seq_tile)
    part = _merge_call(x2, ga.reshape(n, GLA_V), hb.reshape(n, ML_W), pmg,
                       p["wug"], p["wum"], p["wo"], p["g2"],
                       p["wr_hi"], p["wr_lo"], p["br"], placed, row_tile)
    states = (gla_s[None], ml_c[None], ml_n[:, :, 0, :][None], m_b[:, 0, 0:ML_HEADS][None], new_conv[None])
    return part, states


def kernel(x_prompt, x_sample, state_gla_S, state_mlstm_C, state_mlstm_n, state_mlstm_m, state_mlstm_conv, norm1_g, w_in, gla_w_gate2, gla_b_gate, gla_norm_g, w_up_gla, ml_conv_w, ml_conv_b, ml_b_i, ml_b_f, w_up_ml, w_out, norm2_g, router_g_w, router_g_b, router_e_w, router_e_b, moe_w_gate, moe_w_up, moe_w_down, final_g):
    assert norm1_g.shape[0] == 1, "single-layer trunk"
    p = _prep_weights(norm1_g[0], w_in[0], gla_w_gate2[0], gla_b_gate[0], gla_norm_g[0], w_up_gla[0],
                      ml_conv_w[0], ml_conv_b[0], ml_b_i[0], ml_b_f[0], w_up_ml[0], w_out[0],
                      norm2_g[0], router_g_w[0], router_g_b[0], router_e_w[0], router_e_b[0],
                      moe_w_gate[0], moe_w_up[0], moe_w_down[0], final_g)
    token_tile = 256
    dec_seq = x_sample.shape[1]
    assert (x_sample.shape[0] * dec_seq) % token_tile == 0 and x_prompt.shape[1] % 512 == 0
    part_p, sp = _mixers(x_prompt, None, None, None, None, None, p, jnp.zeros((1, LANES), F32),
                         chunk=128, seq_tile=512, row_tile=token_tile)
    placed = part_p[3][-1, 0:1, :] + part_p[3][-1, 2:3, :]
    part_s, ss = _mixers(x_sample, state_gla_S[0], state_mlstm_C[0], state_mlstm_n[0], state_mlstm_m[0],
                         state_mlstm_conv[0], p, placed,
                         chunk=dec_seq, seq_tile=dec_seq, row_tile=token_tile)
    yp, ys = _sparse_moe([part_p, part_s], p, token_tile)
    return (yp.reshape(x_prompt.shape), ys.reshape(x_sample.shape), *sp, *ss)
```

```python
import functools
import math

import numpy as np
import jax
import jax.numpy as jnp
from jax import lax
from jax.experimental import pallas as pl
from jax.experimental.pallas import tpu as pltpu

D_MODEL = 1024
GLA_HEADS = 4
GLA_DK = 64
GLA_DV = 128
GLA_GATE_RANK = 16
GLA_TAU = 16.0
ML_HEADS = 4
ML_DH = 128
CONV_W = 4
N_GROUPS = 4
EXPERTS_PER_GROUP = 8
N_EXPERTS = N_GROUPS * EXPERTS_PER_GROUP
D_EXPERT = 256
EPS = 1e-6

GLA_QK = GLA_HEADS * GLA_DK
GLA_V = GLA_HEADS * GLA_DV
ML_W = ML_HEADS * ML_DH

LANES = 128
VMEM_LIMIT = 56 * 1024 * 1024

W_GLA = 2 * GLA_QK + 2 * GLA_V
W_ML = 2 * ML_W + ML_W + ML_W
W_MG = 2 * D_MODEL
PROJ_WIDTHS = (W_GLA, LANES, W_ML, LANES, W_MG)
PROJ_SOURCE_WIDTHS = (W_GLA, GLA_GATE_RANK, W_ML, 2 * ML_HEADS, W_MG)

F32 = jnp.float32
BF16 = jnp.bfloat16

PROJ_OUTPUTS = ((2 * GLA_QK, F32), (2 * GLA_V, BF16), (LANES, F32), (2 * ML_W, F32), (2 * ML_W, BF16),
                (LANES, F32), (W_MG, BF16))


def _dot(a, b):
    return jnp.dot(a, b, preferred_element_type=F32)


def _dot_nt(a, b):
    return lax.dot_general(a, b, (((1,), (1,)), ((), ())), preferred_element_type=F32)


def _dot_tn(a, b):
    return lax.dot_general(a, b, (((0,), (0,)), ((), ())), preferred_element_type=F32)


def _split3(x):
    hi = x.astype(BF16)
    r1 = x - hi.astype(F32)
    mid = r1.astype(BF16)
    lo = (r1 - mid.astype(F32)).astype(BF16)
    return hi, mid, lo


def _dot_exact_lhs(m, x):
    hi, mid, lo = _split3(x)
    return _dot(m, hi) + _dot(m, mid) + _dot(m, lo)


def _log_sigmoid(z):
    return jnp.minimum(z, 0.0) - jnp.log1p(jnp.exp(-jnp.abs(z)))


def _sigmoid(z):
    return 1.0 / (1.0 + jnp.exp(-z))


def _rms(x, g):
    return x * lax.rsqrt(jnp.mean(x * x, axis=-1, keepdims=True) + EPS) * g


def _full_spec(shape):
    nd = len(shape)
    return pl.BlockSpec(shape, lambda *_: (0,) * nd)


def _proj_kernel(x_ref, g_ref, win_ref, qk_ref, vr_ref, lr_ref, mqk_ref, mvo_ref, if_ref, mg_ref, w_ref):
    starts = np.cumsum((0,) + PROJ_WIDTHS)

    @pl.when(pl.program_id(0) == 0)
    def _():
        src = np.cumsum((0,) + PROJ_SOURCE_WIDTHS)
        chunk = 512
        for g, width in enumerate(PROJ_SOURCE_WIDTHS):
            for r0 in range(0, width, chunk):
                rows = min(chunk, width - r0)
                w_ref[starts[g] + r0:starts[g] + r0 + rows, :] = (
                    win_ref[src[g] + r0:src[g] + r0 + rows, :].astype(BF16))
            if width < PROJ_WIDTHS[g]:
                w_ref[starts[g] + width:starts[g + 1], :] = jnp.zeros((PROJ_WIDTHS[g] - width, D_MODEL), BF16)

    h = _rms(x_ref[...], g_ref[...]).astype(BF16)

    def cols(group, lo, hi):
        return _dot_nt(h, w_ref[starts[group] + lo:starts[group] + hi, :])

    qk_ref[...] = cols(0, 0, 2 * GLA_QK)
    vr_ref[:, 0:GLA_V] = cols(0, 2 * GLA_QK, 2 * GLA_QK + GLA_V).astype(BF16)
    r = cols(0, 2 * GLA_QK + GLA_V, W_GLA)
    vr_ref[:, GLA_V:] = (r * _sigmoid(r)).astype(BF16)
    lr_ref[...] = cols(1, 0, LANES)
    mqk_ref[...] = cols(2, 0, 2 * ML_W)
    mvo_ref[:, 0:ML_W] = cols(2, 2 * ML_W, 3 * ML_W).astype(BF16)
    mvo_ref[:, ML_W:] = _sigmoid(cols(2, 3 * ML_W, W_ML)).astype(BF16)
    if_ref[...] = cols(3, 0, LANES)
    mg_ref[...] = _sigmoid(cols(4, 0, W_MG)).astype(BF16)


def _proj_call(x2, g, w_in, tm):
    n = x2.shape[0]
    assert w_in.shape == (sum(PROJ_SOURCE_WIDTHS), D_MODEL)
    return pl.pallas_call(
        _proj_kernel,
        grid=(n // tm,),
        in_specs=[pl.BlockSpec((tm, D_MODEL), lambda i: (i, 0)),
                  _full_spec(g.shape),
                  pl.BlockSpec(w_in.shape, lambda i: (0, 0), pipeline_mode=pl.Buffered(1))],
        out_specs=[pl.BlockSpec((tm, w), lambda i: (i, 0)) for w, _ in PROJ_OUTPUTS],
        out_shape=[jax.ShapeDtypeStruct((n, w), dt) for w, dt in PROJ_OUTPUTS],
        scratch_shapes=[pltpu.VMEM((sum(PROJ_WIDTHS), D_MODEL), BF16)],
        compiler_params=pltpu.CompilerParams(
            dimension_semantics=("arbitrary",), vmem_limit_bytes=VMEM_LIMIT),
        name="in_proj",
    )(x2, g, w_in)


def _gla_consts(c):
    nlev = int(math.log2(c))
    assert 1 << nlev == c
    t = np.arange(c)[:, None]
    j = np.arange(c)[None, :]
    lv = np.full((c, c), -1, np.int32)
    for l in range(nlev):
        h = c >> (l + 1)
        upper = (t % (2 * h)) >= h
        same = (j // (2 * h)) == (t // (2 * h))
        s_lower = (j % (2 * h)) < h
        lv[np.broadcast_to(upper, (c, c)) & same & s_lower] = l
    lv[np.eye(c, dtype=bool)] = nlev
    tri = (j <= t).astype(np.float32)
    return jnp.asarray(tri, BF16), jnp.asarray(np.concatenate([lv, lv], axis=1))


def _gla_kernel(*refs, c, t, has_state):
    if has_state:
        (qk_ref, vr_ref, plr_ref, s0_ref, wg2_ref, bg_ref, gn_ref, tri_ref, lv_ref,
         o_ref, sout_ref, s_scr) = refs
    else:
        (qk_ref, vr_ref, plr_ref, wg2_ref, bg_ref, gn_ref, tri_ref, lv_ref,
         o_ref, sout_ref, s_scr) = refs
    nlev = int(math.log2(c))
    step = pl.program_id(1)

    @pl.when(step == 0)
    def _():
        if has_state:
            s_scr[...] = s0_ref[...]
        else:
            s_scr[...] = jnp.zeros_like(s_scr)

    lane_k = lax.broadcasted_iota(jnp.int32, (c, GLA_QK), 1)
    first_of_pair = (lane_k % (2 * GLA_DK)) < GLA_DK
    row_k = lax.broadcasted_iota(jnp.int32, (GLA_QK, GLA_DV), 0)
    row_t = lax.broadcasted_iota(jnp.int32, (c, GLA_QK), 0)

    def block_ref(b, blk, idx):
        if blk >= 8:
            b3 = b.reshape(c // blk, blk, GLA_QK)
            return jnp.broadcast_to(b3[:, idx:idx + 1, :], b3.shape).reshape(c, GLA_QK)
        r = row_t % blk
        out = b
        for sh in range(-idx, blk - idx):
            if sh != 0:
                out = jnp.where(r - idx == sh, pltpu.roll(b, sh % c, axis=0), out)
        return out

    def chunk(ci, carry):
        r0 = ci * c
        rows = pl.ds(r0, c)
        q = qk_ref[rows, 0:GLA_QK] * (GLA_DK ** -0.5)
        k = qk_ref[rows, GLA_QK:2 * GLA_QK]
        v = vr_ref[rows, 0:GLA_V]
        glr = plr_ref[rows, :]
        g_hi = glr.astype(BF16)
        g_lo = (glr - g_hi.astype(F32)).astype(BF16)
        z = (_dot(g_hi, wg2_ref[0]) + _dot(g_lo, wg2_ref[0]) + _dot(g_hi, wg2_ref[1])) + bg_ref[...]
        la = _log_sigmoid(z) * (1.0 / GLA_TAU)
        b = _dot_exact_lhs(tri_ref[...], la)
        b_last = b[c - 1:c, :]
        qe = (q * jnp.exp(b)).astype(BF16)
        kl = (k * jnp.exp(b_last - b)).astype(BF16)
        dcol = jnp.exp(jnp.broadcast_to(b_last, (LANES, GLA_QK)).T)

        k_a = jnp.where(first_of_pair, k, 0.0)
        k_b = k - k_a
        lv2 = lv_ref[...]
        a = [jnp.zeros((c, 2 * c), F32) for _ in range(GLA_HEADS // 2)]
        for l in range(nlev + 1):
            if l < nlev:
                half = c >> (l + 1)
                d = b - block_ref(b, 2 * half, half - 1)
                e = jnp.exp(jnp.minimum(d, -d))
                qt, kta, ktb = q * e, k_a * e, k_b * e
            else:
                qt, kta, ktb = q, k_a, k_b
            qt, kta, ktb = qt.astype(BF16), kta.astype(BF16), ktb.astype(BF16)
            for pr in range(GLA_HEADS // 2):
                ls = slice(pr * 2 * GLA_DK, (pr + 1) * 2 * GLA_DK)
                rhs = jnp.concatenate([kta[:, ls], ktb[:, ls]], axis=0)
                a[pr] = jnp.where(lv2 == l, _dot_nt(qt[:, ls], rhs), a[pr])

        s_all = s_scr[...]
        s_bd = jnp.concatenate(
            [jnp.where((row_k // GLA_DK) == h, s_all, 0.0).astype(BF16) for h in range(GLA_HEADS)], axis=1)
        o_inter = _dot(qe, s_bd)
        u_all = _dot_tn(kl, v)
        for h in range(GLA_HEADS):
            vs = slice(h * GLA_DV, (h + 1) * GLA_DV)
            ks = slice(h * GLA_DK, (h + 1) * GLA_DK)
            a_h = a[h // 2][:, (h % 2) * c:(h % 2 + 1) * c]
            o = _dot(a_h.astype(BF16), v[:, vs]) + o_inter[:, vs]
            on = _rms(o, gn_ref[:, vs])
            gate = vr_ref[rows, GLA_V + h * GLA_DV:GLA_V + (h + 1) * GLA_DV]
            o_ref[rows, vs] = (on * gate.astype(F32)).astype(o_ref.dtype)
            s_scr[ks, :] = dcol[ks, :] * s_all[ks, :] + u_all[ks, vs]
        return carry

    for ci in range(t // c):
        chunk(ci, 0)

    @pl.when(step == pl.num_programs(1) - 1)
    def _():
        sout_ref[...] = s_scr[...]


def _gla_call(qk, vr, plr, s0, wg2_p, bg, gn, c, t):
    b, l, _ = qk.shape
    tri, lv2 = _gla_consts(c)
    has_state = s0 is not None
    tile = lambda w: pl.BlockSpec((None, t, w), lambda bi, i: (bi, i, 0))
    state_spec = pl.BlockSpec((None, GLA_QK, GLA_DV), lambda bi, i: (bi, 0, 0))
    in_specs = [tile(2 * GLA_QK), tile(2 * GLA_V), tile(LANES)]
    args = [qk, vr, plr]
    if has_state:
        in_specs.append(state_spec)
        args.append(s0)
    consts = [wg2_p, bg, gn, tri, lv2]
    in_specs += [_full_spec(x.shape) for x in consts]
    return pl.pallas_call(
        functools.partial(_gla_kernel, c=c, t=t, has_state=has_state),
        grid=(b, l // t),
        in_specs=in_specs,
        out_specs=[tile(GLA_V), state_spec],
        out_shape=[jax.ShapeDtypeStruct((b, l, GLA_V), BF16),
                   jax.ShapeDtypeStruct((b, GLA_QK, GLA_DV), F32)],
        scratch_shapes=[pltpu.VMEM((GLA_QK, GLA_DV), F32)],
        compiler_params=pltpu.CompilerParams(
            dimension_semantics=("arbitrary", "arbitrary"), vmem_limit_bytes=VMEM_LIMIT),
        name="gla",
    )(*args, *consts)


def _mlstm_kernel(*refs, c, t, has_state):
    if has_state:
        (mqk_ref, mvo_ref, pif_ref, c0_ref, n0_ref, m0_ref, cv0_ref, cw_ref, cb_ref, bif_ref, tri_ref, sel_ref,
         o_ref, cout_ref, nout_ref, mout_ref, cvout_ref, c_scr, m_scr, cv_scr, qk_scr) = refs
    else:
        (mqk_ref, mvo_ref, pif_ref, cw_ref, cb_ref, bif_ref, tri_ref, sel_ref,
         o_ref, cout_ref, nout_ref, mout_ref, cvout_ref, c_scr, m_scr, cv_scr, qk_scr) = refs
    step = pl.program_id(1)
    hist = CONV_W - 1
    pad = 8

    @pl.when(step == 0)
    def _():
        if has_state:
            for h in range(ML_HEADS):
                c_scr[h, :, 0:ML_DH] = c0_ref[h]
                c_scr[h, :, ML_DH:] = jnp.broadcast_to(n0_ref[h], (ML_DH, ML_DH)).T
            m_scr[...] = m0_ref[...]
            cv_scr[0:pad - hist, :] = jnp.zeros((pad - hist, 2 * ML_W), F32)
            cv_scr[pad - hist:pad, :] = cv0_ref[...]
        else:
            c_scr[...] = jnp.zeros_like(c_scr)
            m_scr[...] = jnp.zeros_like(m_scr)
            cv_scr[0:pad, :] = jnp.zeros((pad, 2 * ML_W), F32)

    cv_scr[pad:pad + t, :] = mqk_ref[...]
    acc = cb_ref[...] + cv_scr[pad:pad + t, :] * cw_ref[CONV_W - 1:CONV_W, :]
    for d in range(1, CONV_W):
        acc = acc + cv_scr[pad - d:pad - d + t, :] * cw_ref[CONV_W - 1 - d:CONV_W - d, :]
    conv = acc * _sigmoid(acc)
    qk_scr[:, 0:ML_W] = conv[:, 0:ML_W].astype(BF16)
    qk_scr[:, ML_W:] = (conv[:, ML_W:] * (ML_DH ** -0.5)).astype(BF16)
    cv_scr[0:pad, :] = cv_scr[t:t + pad, :]

    gts = pif_ref[...] + bif_ref[...]
    flog = pltpu.roll(_log_sigmoid(gts), LANES - ML_HEADS, axis=1)

    lane = lax.broadcasted_iota(jnp.int32, (c, LANES), 1)
    row_c = lax.broadcasted_iota(jnp.int32, (c, LANES), 0)
    causal = (lax.broadcasted_iota(jnp.int32, (c, c), 1) <= lax.broadcasted_iota(jnp.int32, (c, c), 0))
    ones_v = jnp.ones((c, ML_DH), BF16)

    def slots(pieces):
        out = jnp.zeros((c, LANES), F32)
        for j, piece in enumerate(pieces):
            if not isinstance(piece, float) and j > 0:
                piece = pltpu.roll(piece, ML_HEADS * j, axis=1)
            out = jnp.where((lane >= ML_HEADS * j) & (lane < ML_HEADS * (j + 1)), piece, out)
        return out

    def split3f(x):
        return [p.astype(F32) for p in _split3(x)]

    def chunk(ci, carry):
        r0 = ci * c
        rows = pl.ds(r0, c)
        ip = gts[r0:r0 + c]
        bc = _dot_exact_lhs(tri_ref[...], flog[r0:r0 + c])
        w = ip - bc
        cm = w
        for j in range(int(math.log2(c))):
            sh = 1 << j
            cm = jnp.where(row_c >= sh, jnp.maximum(cm, pltpu.roll(cm, sh, axis=0)), cm)
        mprev = m_scr[0:1, :]
        g = jnp.maximum(mprev, cm)
        g_last = g[c - 1:c, :]
        m_scr[...] = jnp.broadcast_to(bc[c - 1:c, :] + g_last, m_scr.shape)
        lhs_all = slots([1.0, 1.0, 1.0] + split3f(g))
        rhs = slots(split3f(w) + [-1.0, -1.0, -1.0]).astype(BF16)
        y = slots(split3f(mprev - g) + split3f(-(bc + g)) + split3f(w - g_last)).astype(BF16)
        for h in range(ML_HEADS):
            hs = slice(h * ML_DH, (h + 1) * ML_DH)
            qh = qk_scr[rows, h * ML_DH:(h + 1) * ML_DH]
            kh = qk_scr[rows, ML_W + h * ML_DH:ML_W + (h + 1) * ML_DH]
            vaug = jnp.concatenate(
                [mvo_ref[rows, h * ML_DH:(h + 1) * ML_DH], ones_v],
                axis=1)
            lhs = jnp.where(lane % ML_HEADS == h, lhs_all, 0.0).astype(BF16)
            p = jnp.exp(jnp.where(causal, _dot_nt(lhs, rhs), -jnp.inf))
            bx = jnp.exp(_dot(y, sel_ref[h]))
            w_inter = bx[:, 0:ML_DH]
            e_mt = bx[:, ML_DH:2 * ML_DH]
            w_state = bx[:, 2 * ML_DH:3 * ML_DH]
            s = (_dot_nt(qh, kh) * p).astype(BF16)
            caug = c_scr[h]
            nd = _dot(s, vaug) + jnp.concatenate([w_inter, w_inter], axis=1) * _dot(qh, caug.astype(BF16))
            hh = nd[:, 0:ML_DH] / jnp.maximum(jnp.abs(nd[:, ML_DH:]), e_mt)
            o_gate = mvo_ref[rows, ML_W + h * ML_DH:ML_W + (h + 1) * ML_DH]
            o_ref[rows, hs] = (o_gate.astype(F32) * hh).astype(o_ref.dtype)
            ks = (kh.astype(F32) * w_state).astype(BF16)
            dec = w_inter[c - 1:c, :]
            c_scr[h] = jnp.concatenate([dec, dec], axis=1) * caug + _dot_tn(ks, vaug)
        return carry

    for ci in range(t // c):
        chunk(ci, 0)

    @pl.when(step == pl.num_programs(1) - 1)
    def _():
        for h in range(ML_HEADS):
            cout_ref[h] = c_scr[h, :, 0:ML_DH]
            nout_ref[h] = c_scr[h, :, ML_DH:].T[0:8, :]
        mout_ref[...] = m_scr[...]
        cvout_ref[...] = mqk_ref[t - hist:t, :]


def _mlstm_call(mqk, mvo, pif, c0, n0, m0, cv0, cw, cb, bif, c, t):
    b, l, _ = mqk.shape
    has_state = c0 is not None
    tri = jnp.asarray(np.tril(np.ones((c, c), np.float32)), BF16)
    sel = np.zeros((ML_HEADS, LANES, 3 * ML_DH), np.float32)
    for h in range(ML_HEADS):
        for slot in range(9):
            sel[h, ML_HEADS * slot + h, (slot // 3) * ML_DH:(slot // 3 + 1) * ML_DH] = 1.0
    sel = jnp.asarray(sel, BF16)
    tile = lambda w: pl.BlockSpec((None, t, w), lambda bi, i: (bi, i, 0))
    heads = lambda *shape: pl.BlockSpec((None, ML_HEADS) + shape, lambda bi, i: (bi, 0, 0, 0))
    c_spec = heads(ML_DH, ML_DH)
    m_spec = pl.BlockSpec((None, 8, LANES), lambda bi, i: (bi, 0, 0))
    cv_spec = pl.BlockSpec((None, CONV_W - 1, 2 * ML_W), lambda bi, i: (bi, 0, 0))
    in_specs = [tile(2 * ML_W), tile(2 * ML_W), tile(LANES)]
    args = [mqk, mvo, pif]
    if has_state:
        in_specs += [c_spec, heads(1, ML_DH), m_spec, cv_spec]
        args += [c0, n0, m0, cv0]
    consts = [cw, cb, bif, tri, sel]
    in_specs += [_full_spec(x.shape) for x in consts]
    return pl.pallas_call(
        functools.partial(_mlstm_kernel, c=c, t=t, has_state=has_state),
        grid=(b, l // t),
        in_specs=in_specs,
        out_specs=[tile(ML_W), c_spec, heads(8, ML_DH), m_spec, cv_spec],
        out_shape=[jax.ShapeDtypeStruct((b, l, ML_W), BF16),
                   jax.ShapeDtypeStruct((b, ML_HEADS, ML_DH, ML_DH), F32),
                   jax.ShapeDtypeStruct((b, ML_HEADS, 8, ML_DH), F32),
                   jax.ShapeDtypeStruct((b, 8, LANES), F32),
                   jax.ShapeDtypeStruct((b, CONV_W - 1, 2 * ML_W), F32)],
        scratch_shapes=[pltpu.VMEM((ML_HEADS, ML_DH, 2 * ML_DH), F32),
                        pltpu.VMEM((8, LANES), F32),
                        pltpu.VMEM((8 + t, 2 * ML_W), F32),
                        pltpu.VMEM((t, 2 * ML_W), BF16)],
        compiler_params=pltpu.CompilerParams(
            dimension_semantics=("arbitrary", "arbitrary"), vmem_limit_bytes=VMEM_LIMIT),
        name="mlstm",
    )(*args, *consts)


def _merge_kernel(x_ref, ga_ref, hb_ref, pmg_ref, wug_ref, wum_ref, wo_ref, g2_ref,
                  wr_hi_ref, wr_lo_ref, br_ref, tril_ref, triu_ref, cnt0_ref,
                  x1_ref, hs_ref, meta_ref, tab_ref, cnt_scr):
    ya = _dot(ga_ref[...], wug_ref[...])
    yb = _dot(hb_ref[...], wum_ref[...])
    z = pmg_ref[:, 0:D_MODEL].astype(F32) * ya + pmg_ref[:, D_MODEL:].astype(F32) * yb
    x1 = x_ref[...] + _dot(z.astype(BF16), wo_ref[...])
    x1_ref[...] = x1
    hm = _rms(x1, g2_ref[...])
    hm_hi = hm.astype(BF16)
    hm_lo = (hm - hm_hi.astype(F32)).astype(BF16)
    hi_both = _dot(hm_hi, jnp.concatenate([wr_hi_ref[...], wr_lo_ref[...]], axis=1))
    logits = hi_both[:, 0:LANES] + hi_both[:, LANES:] + _dot(hm_lo, wr_hi_ref[...]) + br_ref[...]
    lane = lax.broadcasted_iota(jnp.int32, logits.shape, 1)
    neg = -jnp.inf
    is_g = (lane >= N_EXPERTS) & (lane < N_EXPERTS + N_GROUPS)
    lg = jnp.where(is_g, logits, neg)
    mg = jnp.max(lg, axis=-1, keepdims=True)
    p_top = 1.0 / jnp.sum(jnp.exp(lg - mg), axis=-1, keepdims=True)
    gi = jnp.min(jnp.where(lg == mg, lane, 2 * LANES), axis=-1, keepdims=True) - N_EXPERTS
    group_shift = int(math.log2(EXPERTS_PER_GROUP))
    sel = (lane < N_EXPERTS) & (jnp.right_shift(lane, group_shift) == gi)
    le = jnp.where(sel, logits, neg)
    v1 = jnp.max(le, axis=-1, keepdims=True)
    i1 = jnp.min(jnp.where(le == v1, lane, 2 * LANES), axis=-1, keepdims=True)
    le2 = jnp.where(lane == i1, neg, le)
    v2 = jnp.max(le2, axis=-1, keepdims=True)
    i2 = jnp.min(jnp.where(le2 == v2, lane, 2 * LANES), axis=-1, keepdims=True)
    e2 = jnp.exp(v2 - v1)
    wa = 1.0 / (1.0 + e2)
    wb = e2 / (1.0 + e2)

    @pl.when(pl.program_id(0) == 0)
    def _():
        cnt_scr[...] = cnt0_ref[...]

    tm = logits.shape[0]
    oh1 = lane == i1
    oh2 = lane == i2
    both = jnp.where(oh1 | oh2, 1.0, 0.0)
    cnt = jnp.sum(both, axis=0, keepdims=True)
    cnt = jnp.floor((cnt + (SEG_ALIGN - 1)) * (1.0 / SEG_ALIGN)) * SEG_ALIGN
    lower = _dot(jnp.broadcast_to(cnt, (8, LANES)).astype(BF16), triu_ref[...])[0:1, :]
    lpos = _dot(tril_ref[...], both.astype(BF16)) + lower
    lp1 = jnp.sum(jnp.where(oh1, lpos, 0.0), axis=-1, keepdims=True)
    lp2 = jnp.sum(jnp.where(oh2, lpos, 0.0), axis=-1, keepdims=True)
    pos = lax.broadcasted_iota(jnp.int32, (tm, hs_ref.shape[0]), 1)
    onehot = jnp.where((pos == lp1.astype(jnp.int32)) | (pos == lp2.astype(jnp.int32)), 1.0, 0.0)
    hs_ref[...] = _dot_tn(onehot.astype(BF16), hm_hi).astype(BF16)
    cols = (lp1, lp2, p_top * wa, p_top * wb)
    meta = jnp.zeros(logits.shape, F32)
    for ci, col in enumerate(cols):
        meta = jnp.where(lane == ci, col, meta)
    meta_ref[...] = meta
    row8 = lax.broadcasted_iota(jnp.int32, (8, LANES), 0)
    tab_ref[...] = jnp.where(row8 == 0, cnt, jnp.where(row8 == 1, lower, jnp.where(row8 == 2, cnt_scr[...], 0.0)))
    cnt_scr[...] += cnt


def _merge_call(x2, ga, hb, pmg, wug, wum, wo, g2, wr_hi, wr_lo, br, cnt0, tm):
    n = x2.shape[0]
    tile = lambda w: pl.BlockSpec((tm, w), lambda i: (i, 0))
    tiles = lambda *shape: pl.BlockSpec((None,) + shape, lambda i: (i, 0, 0))
    local_rows = 2 * tm + SEG_ALIGN * N_EXPERTS
    consts = [wug, wum, wo, g2, wr_hi, wr_lo, br,
              jnp.asarray(np.tril(np.ones((tm, tm), np.float32), -1), BF16),
              jnp.asarray(np.triu(np.ones((LANES, LANES), np.float32), 1), BF16),
              cnt0]
    return pl.pallas_call(
        _merge_kernel,
        grid=(n // tm,),
        in_specs=[tile(D_MODEL), tile(GLA_V), tile(ML_W), tile(W_MG)]
                 + [_full_spec(x.shape) for x in consts],
        out_specs=[tile(D_MODEL), tiles(local_rows, D_MODEL), tile(LANES), tiles(8, LANES)],
        out_shape=[jax.ShapeDtypeStruct((n, D_MODEL), F32),
                   jax.ShapeDtypeStruct((n // tm, local_rows, D_MODEL), BF16),
                   jax.ShapeDtypeStruct((n, LANES), F32),
                   jax.ShapeDtypeStruct((n // tm, 8, LANES), F32)],
        scratch_shapes=[pltpu.VMEM((1, LANES), F32)],
        compiler_params=pltpu.CompilerParams(
            dimension_semantics=("arbitrary",), vmem_limit_bytes=VMEM_LIMIT),
        name="merge",
    )(x2, ga, hb, pmg, *consts)


MOE_ROWS = 512
SEG_ALIGN = 16
SEG_SIZES = (512, 256, 128, 64, 32, 16)


def _segment_copies(fn, tables, tile, local_ref, buffer, flat_ref, sem, to_flat):
    cnt_ref, loff_ref, gpos_ref = tables

    def body(e, carry):
        idx = tile * N_EXPERTS + e
        n, lo, gp = cnt_ref[idx], loff_ref[idx], gpos_ref[idx]
        for size in SEG_SIZES:
            @pl.when((n & size) != 0)
            def _():
                off = n & ~(2 * size - 1)
                local = local_ref.at[buffer, pl.ds(pl.multiple_of(lo + off, SEG_ALIGN), size), :]
                flat = flat_ref.at[pl.ds(pl.multiple_of(gp + off, SEG_ALIGN), size), :]
                src, dst = (local, flat) if to_flat else (flat, local)
                fn(pltpu.make_async_copy(src, dst, sem))
        return carry

    lax.fori_loop(0, N_EXPERTS, body, 0)


def _dispatch_kernel(*refs, tile0, first):
    if first:
        cnt_ref, loff_ref, gpos_ref, tail_ref, hs_ref, xs_ref, zero_ref, sem, zsem = refs
    else:
        cnt_ref, loff_ref, gpos_ref, tail_ref, hs_ref, _, xs_ref, zero_ref, sem, zsem = refs
    step = pl.program_id(0)

    @pl.when((step == 0) & first)
    def _():
        zero_ref[...] = jnp.zeros_like(zero_ref)
        n_tiles = xs_ref.shape[0] // MOE_ROWS

        def clear(row):
            start = pl.multiple_of(row, MOE_ROWS)
            return pltpu.make_async_copy(zero_ref, xs_ref.at[pl.ds(start, MOE_ROWS), :], zsem)

        def unused(fn):
            def body(j, carry):
                fn(clear(j * MOE_ROWS))
                return carry
            lax.fori_loop(tail_ref[2 * N_EXPERTS], n_tiles, body, 0)

        for e in range(N_EXPERTS):
            @pl.when(tail_ref[N_EXPERTS + e] > 0)
            def _():
                clear(tail_ref[e]).start()
        unused(lambda c: c.start())
        for e in range(N_EXPERTS):
            @pl.when(tail_ref[N_EXPERTS + e] > 0)
            def _():
                clear(tail_ref[e]).wait()
        unused(lambda c: c.wait())

    tables = (cnt_ref, loff_ref, gpos_ref)
    _segment_copies(lambda c: c.start(), tables, tile0 + step, hs_ref, 0, xs_ref, sem, True)
    _segment_copies(lambda c: c.wait(), tables, tile0 + step, hs_ref, 0, xs_ref, sem, True)


def _dispatch_call(tables, tail, hs, tile0, n_rows, xs=None):
    first = xs is None
    any_spec = pl.BlockSpec(memory_space=pl.ANY)
    in_specs = [pl.BlockSpec((1,) + hs.shape[1:], lambda i, *_: (i, 0, 0))]
    args = [*tables, tail, hs]
    if not first:
        in_specs.append(any_spec)
        args.append(xs)
    return pl.pallas_call(
        functools.partial(_dispatch_kernel, tile0=tile0, first=first),
        grid_spec=pltpu.PrefetchScalarGridSpec(
            num_scalar_prefetch=4,
            grid=(hs.shape[0],),
            in_specs=in_specs,
            out_specs=any_spec,
            scratch_shapes=[pltpu.VMEM((MOE_ROWS, D_MODEL), hs.dtype),
                            pltpu.SemaphoreType.DMA, pltpu.SemaphoreType.DMA],
        ),
        out_shape=jax.ShapeDtypeStruct((n_rows, D_MODEL), hs.dtype),
        input_output_aliases={} if first else {len(args) - 1: 0},
        compiler_params=pltpu.CompilerParams(
            dimension_semantics=("arbitrary",), vmem_limit_bytes=VMEM_LIMIT),
        name="moe_dispatch",
    )(*args)


def _gmm_kernel(te_ref, nv_ref, xs_ref, wg_ref, wu_ref, wd_ref, o_ref, wgu_scr, wd_scr):
    j = pl.program_id(0)
    used = j < nv_ref[0]

    @pl.when(used & ((j == 0) | (te_ref[j] != te_ref[jnp.maximum(j - 1, 0)])))
    def _():
        wgu_scr[:, 0:D_EXPERT] = wg_ref[...].astype(BF16)
        wgu_scr[:, D_EXPERT:] = wu_ref[...].astype(BF16)
        wd_scr[...] = wd_ref[...].astype(BF16)

    @pl.when(used)
    def _():
        au = _dot(xs_ref[...], wgu_scr[...])
        a = au[:, 0:D_EXPERT]
        hh = (a * _sigmoid(a)) * au[:, D_EXPERT:]
        o_ref[...] = _dot(hh.astype(BF16), wd_scr[...]).astype(o_ref.dtype)

    @pl.when(jnp.logical_not(used))
    def _():
        o_ref[...] = jnp.zeros_like(o_ref)


def _gmm_call(tile_expert, n_valid, xs, wg, wu, wd):
    n_tiles = xs.shape[0] // MOE_ROWS
    rows = lambda j, te, nv: (jnp.minimum(j, nv[0] - 1), 0)
    wsel = lambda j, te, nv: (te[j], 0, 0)
    return pl.pallas_call(
        _gmm_kernel,
        grid_spec=pltpu.PrefetchScalarGridSpec(
            num_scalar_prefetch=2,
            grid=(n_tiles,),
            in_specs=[pl.BlockSpec((MOE_ROWS, D_MODEL), rows),
                      pl.BlockSpec((None, D_MODEL, D_EXPERT), wsel),
                      pl.BlockSpec((None, D_MODEL, D_EXPERT), wsel),
                      pl.BlockSpec((None, D_EXPERT, D_MODEL), wsel)],
            out_specs=pl.BlockSpec((MOE_ROWS, D_MODEL), lambda j, te, nv: (j, 0)),
            scratch_shapes=[pltpu.VMEM((D_MODEL, 2 * D_EXPERT), BF16),
                            pltpu.VMEM((D_EXPERT, D_MODEL), BF16)],
        ),
        out_shape=jax.ShapeDtypeStruct(xs.shape, xs.dtype),
        compiler_params=pltpu.CompilerParams(
            dimension_semantics=("arbitrary",), vmem_limit_bytes=VMEM_LIMIT),
        name="moe_grouped",
    )(tile_expert, n_valid, xs, wg, wu, wd)


def _combine_kernel(cnt_ref, loff_ref, gpos_ref, x1_ref, meta_ref, gf_ref, os_ref, y_ref, buf_ref, sem,
                    *, tile0):
    step = pl.program_id(0)
    n_steps = pl.num_programs(0)
    tm = x1_ref.shape[0]
    tables = (cnt_ref, loff_ref, gpos_ref)

    def fetch(fn, i):
        slot = i % 2
        _segment_copies(fn, tables, tile0 + i, buf_ref, slot, os_ref, sem.at[slot], False)

    @pl.when(step == 0)
    def _():
        buf_ref[...] = jnp.zeros_like(buf_ref)
        fetch(lambda c: c.start(), step)

    @pl.when(step + 1 < n_steps)
    def _():
        fetch(lambda c: c.start(), step + 1)

    fetch(lambda c: c.wait(), step)
    rows = buf_ref[step % 2]
    pos = lax.broadcasted_iota(jnp.int32, (tm, rows.shape[0]), 1)
    lp1 = meta_ref[:, 0:1].astype(jnp.int32)
    lp2 = meta_ref[:, 1:2].astype(jnp.int32)
    q = jnp.where(pos == lp1, meta_ref[:, 2:3], 0.0) + jnp.where(pos == lp2, meta_ref[:, 3:4], 0.0)
    y = x1_ref[...] + _dot(q.astype(BF16), rows)
    y_ref[...] = _rms(y, gf_ref[...])


def _combine_call(tables, x1, meta, gf, out_sorted, tile0, local_rows, tm):
    n = x1.shape[0]
    tile = lambda w: pl.BlockSpec((tm, w), lambda i, *_: (i, 0))
    return pl.pallas_call(
        functools.partial(_combine_kernel, tile0=tile0),
        grid_spec=pltpu.PrefetchScalarGridSpec(
            num_scalar_prefetch=3,
            grid=(n // tm,),
            in_specs=[tile(D_MODEL), tile(LANES), pl.BlockSpec(gf.shape, lambda i, *_: (0, 0)),
                      pl.BlockSpec(memory_space=pl.ANY)],
            out_specs=tile(D_MODEL),
            scratch_shapes=[pltpu.VMEM((2, local_rows, D_MODEL), out_sorted.dtype),
                            pltpu.SemaphoreType.DMA((2,))],
        ),
        out_shape=jax.ShapeDtypeStruct((n, D_MODEL), F32),
        compiler_params=pltpu.CompilerParams(
            dimension_semantics=("arbitrary",), vmem_limit_bytes=VMEM_LIMIT),
        name="moe_combine",
    )(*tables, x1, meta, gf, out_sorted)


def _sparse_moe(parts, p):
    token_tiles = [part[1].shape[0] for part in parts]
    n_tiles = -(-sum(part[1].shape[0] * part[1].shape[1] for part in parts) // MOE_ROWS) + N_EXPERTS
    tab = jnp.concatenate([part[3] for part in parts], axis=0)
    tab = tab[:, :, :N_EXPERTS].astype(jnp.int32)
    cnt, loff, before = tab[:, 0], tab[:, 1], tab[:, 2]
    total = before[-1] + cnt[-1]
    tiles = (total + (MOE_ROWS - 1)) // MOE_ROWS
    tile_end = jnp.cumsum(tiles)
    tile_start = tile_end - tiles
    n_valid = tile_end[-1:]
    gpos = tile_start[None, :] * MOE_ROWS + before
    j = jnp.minimum(jnp.arange(n_tiles, dtype=jnp.int32), n_valid - 1)
    tile_expert = jnp.sum((j[:, None] >= tile_end[None, :]).astype(jnp.int32), axis=1)
    tail = jnp.concatenate([(tile_end - 1) * MOE_ROWS, total, n_valid]).astype(jnp.int32)
    tables = (cnt.reshape(-1), loff.reshape(-1), gpos.reshape(-1))
    xs = None
    for part, tile0 in zip(parts, np.cumsum([0] + token_tiles[:-1])):
        xs = _dispatch_call(tables, tail, part[1], int(tile0), n_tiles * MOE_ROWS, xs)
    out_sorted = _gmm_call(tile_expert, n_valid.astype(jnp.int32), xs, p["wg"], p["wu"], p["wd"])
    return [_combine_call(tables, part[0], part[2], p["gf"], out_sorted, int(tile0),
                          part[1].shape[1], part[0].shape[0] // part[1].shape[0])
            for part, tile0 in zip(parts, np.cumsum([0] + token_tiles[:-1]))]


def _pad_cols(w, width):
    return jnp.pad(w, ((0, 0), (0, width - w.shape[1])))


def _prep_weights(norm1_g, w_in, gla_w_gate2, gla_b_gate, gla_norm_g, w_up_gla,
                  ml_conv_w, ml_conv_b, ml_b_i, ml_b_f, w_up_ml, w_out,
                  norm2_g, router_g_w, router_g_b, router_e_w, router_e_b,
                  moe_w_gate, moe_w_up, moe_w_down, final_g):
    wr =_pad_cols(jnp.concatenate([router_e_w, router_g_w], axis=1), LANES)
    wr_hi = wr.astype(BF16)
    wr_lo = (wr - wr_hi.astype(F32)).astype(BF16)
    br = _pad_cols(jnp.concatenate([router_e_b, router_g_b])[None, :], LANES)
    wg2 = jnp.pad(gla_w_gate2, ((0, LANES - GLA_GATE_RANK), (0, 0)))
    wg2_hi = wg2.astype(BF16)
    return dict(
        g1=norm1_g[None, :], w_in=w_in.T,
        wg2_p=jnp.stack([wg2_hi, (wg2 - wg2_hi.astype(F32)).astype(BF16)]),
        bg=gla_b_gate[None, :], gn=gla_norm_g[None, :],
        wug=w_up_gla.astype(BF16),
        cw=ml_conv_w, cb=ml_conv_b[None, :],
        bif=_pad_cols(jnp.concatenate([ml_b_i, ml_b_f])[None, :], LANES),
        wum=w_up_ml.astype(BF16), wo=w_out.astype(BF16),
        g2=norm2_g[None, :], wr_hi=wr_hi, wr_lo=wr_lo, br=br,
        wg=moe_w_gate.reshape(N_EXPERTS, D_MODEL, D_EXPERT),
        wu=moe_w_up.reshape(N_EXPERTS, D_MODEL, D_EXPERT),
        wd=moe_w_down.reshape(N_EXPERTS, D_EXPERT, D_MODEL),
        gf=final_g[None, :],
    )


def _mixers(x, gla_s0, ml_c0, ml_n0, ml_m0, conv0, p, placed, *, chunk, seq_tile, row_tile, merge_tile):
    b, l, _ = x.shape
    n = b * l
    x2 = x.reshape(n, D_MODEL)
    qk, vr, plr, mqk, mvo, pif, pmg = _proj_call(x2, p["g1"], p["w_in"], row_tile)
    r3 = lambda a: a.reshape(b, l, a.shape[-1])
    s0 = None if gla_s0 is None else gla_s0.reshape(b, GLA_QK, GLA_DV)
    ga, gla_s = _gla_call(r3(qk), r3(vr), r3(plr), s0, p["wg2_p"], p["bg"], p["gn"], chunk, seq_tile)
    gla_s = gla_s.reshape(b, GLA_HEADS, GLA_DK, GLA_DV)
    if ml_c0 is None:
        n0 = m0 = None
    else:
        n0 = ml_n0[:, :, None, :]
        m0 = jnp.broadcast_to(_pad_cols(ml_m0, LANES)[:, None, :], (b, 8, LANES))
    hb, ml_c, ml_n, m_b, new_conv = _mlstm_call(r3(mqk), r3(mvo), r3(pif), ml_c0, n0, m0, conv0,
                                                p["cw"], p["cb"], p["bif"], chunk, seq_tile)
    part = _merge_call(x2, ga.reshape(n, GLA_V), hb.reshape(n, ML_W), pmg,
                       p["wug"], p["wum"], p["wo"], p["g2"],
                       p["wr_hi"], p["wr_lo"], p["br"], placed, merge_tile)
    states = (gla_s[None], ml_c[None], ml_n[:, :, 0, :][None], m_b[:, 0, 0:ML_HEADS][None], new_conv[None])
    return part, states


def kernel(x_prompt, x_sample, state_gla_S, state_mlstm_C, state_mlstm_n, state_mlstm_m, state_mlstm_conv, norm1_g, w_in, gla_w_gate2, gla_b_gate, gla_norm_g, w_up_gla, ml_conv_w, ml_conv_b, ml_b_i, ml_b_f, w_up_ml, w_out, norm2_g, router_g_w, router_g_b, router_e_w, router_e_b, moe_w_gate, moe_w_up, moe_w_down, final_g):
    assert norm1_g.shape[0] == 1, "single-layer trunk"
    p = _prep_weights(norm1_g[0], w_in[0], gla_w_gate2[0], gla_b_gate[0], gla_norm_g[0], w_up_gla[0],
                      ml_conv_w[0], ml_conv_b[0], ml_b_i[0], ml_b_f[0], w_up_ml[0], w_out[0],
                      norm2_g[0], router_g_w[0], router_g_b[0], router_e_w[0], router_e_b[0],
                      moe_w_gate[0], moe_w_up[0], moe_w_down[0], final_g)
    dec_seq = x_sample.shape[1]
    n_sample = x_sample.shape[0] * dec_seq
    part_p, sp = _mixers(x_prompt, None, None, None, None, None, p, jnp.zeros((1, LANES), F32),
                         chunk=128, seq_tile=512, row_tile=256, merge_tile=512)
    placed = part_p[3][-1, 0:1, :] + part_p[3][-1, 2:3, :]
    part_s, ss = _mixers(x_sample, state_gla_S[0], state_mlstm_C[0], state_mlstm_n[0], state_mlstm_m[0],
                         state_mlstm_conv[0], p, placed,
                         chunk=dec_seq, seq_tile=dec_seq, row_tile=n_sample, merge_tile=n_sample)
    yp, ys = _sparse_moe([part_p, part_s], p)
    return (yp.reshape(x_prompt.shape), ys.reshape(x_sample.shape), *sp, *ss)
```

```python
import functools
import math

import numpy as np
import jax
import jax.numpy as jnp
from jax import lax
from jax.experimental import pallas as pl
from jax.experimental.pallas import tpu as pltpu

D_MODEL = 1024
GLA_HEADS = 4
GLA_DK = 64
GLA_DV = 128
GLA_GATE_RANK = 16
GLA_TAU = 16.0
ML_HEADS = 4
ML_DH = 128
CONV_W = 4
N_GROUPS = 4
EXPERTS_PER_GROUP = 8
N_EXPERTS = N_GROUPS * EXPERTS_PER_GROUP
D_EXPERT = 256
EPS = 1e-6

GLA_QK = GLA_HEADS * GLA_DK
GLA_V = GLA_HEADS * GLA_DV
ML_W = ML_HEADS * ML_DH

LANES = 128
VMEM_LIMIT = 56 * 1024 * 1024

W_GLA = 2 * GLA_QK + 2 * GLA_V
W_ML = 2 * ML_W + ML_W + ML_W
W_MG = 2 * D_MODEL
PROJ_WIDTHS = (W_GLA, LANES, W_ML, LANES, W_MG)
PROJ_SOURCE_WIDTHS = (W_GLA, GLA_GATE_RANK, W_ML, 2 * ML_HEADS, W_MG)

F32 = jnp.float32
BF16 = jnp.bfloat16

PROJ_OUTPUTS = ((2 * GLA_QK, F32), (2 * GLA_V, BF16), (LANES, F32), (2 * ML_W, F32), (2 * ML_W, BF16),
                (LANES, F32), (W_MG, BF16))


def _dot(a, b):
    return jnp.dot(a, b, preferred_element_type=F32)


def _dot_nt(a, b):
    return lax.dot_general(a, b, (((1,), (1,)), ((), ())), preferred_element_type=F32)


def _dot_tn(a, b):
    return lax.dot_general(a, b, (((0,), (0,)), ((), ())), preferred_element_type=F32)


def _split3(x):
    hi = x.astype(BF16)
    r1 = x - hi.astype(F32)
    mid = r1.astype(BF16)
    lo = (r1 - mid.astype(F32)).astype(BF16)
    return hi, mid, lo


def _dot_exact_lhs(m, x):
    hi, mid, lo = _split3(x)
    return _dot(m, hi) + _dot(m, mid) + _dot(m, lo)


def _log_sigmoid(z):
    return jnp.minimum(z, 0.0) - jnp.log1p(jnp.exp(-jnp.abs(z)))


def _sigmoid(z):
    return 1.0 / (1.0 + jnp.exp(-z))


def _rms(x, g):
    return x * lax.rsqrt(jnp.mean(x * x, axis=-1, keepdims=True) + EPS) * g


def _full_spec(shape):
    nd = len(shape)
    return pl.BlockSpec(shape, lambda *_: (0,) * nd)


CONV_PAD = 8


def _causal_conv_silu(stage_ref, x, cw_ref, cb_ref):
    t = x.shape[0]
    stage_ref[CONV_PAD:CONV_PAD + t, :] = x
    acc = cb_ref[...] + stage_ref[CONV_PAD:CONV_PAD + t, :] * cw_ref[CONV_W - 1:CONV_W, :]
    for d in range(1, CONV_W):
        acc = acc + stage_ref[CONV_PAD - d:CONV_PAD - d + t, :] * cw_ref[CONV_W - 1 - d:CONV_W - d, :]
    stage_ref[0:CONV_PAD, :] = stage_ref[t:t + CONV_PAD, :]
    return acc * _sigmoid(acc)


def _proj_kernel(x_ref, g_ref, win_ref, qk_ref, vr_ref, lr_ref, mqk_ref, mvo_ref, if_ref, mg_ref, w_ref):
    starts = np.cumsum((0,) + PROJ_WIDTHS)

    @pl.when(pl.program_id(0) == 0)
    def _():
        src = np.cumsum((0,) + PROJ_SOURCE_WIDTHS)
        chunk = 512
        for g, width in enumerate(PROJ_SOURCE_WIDTHS):
            for r0 in range(0, width, chunk):
                rows = min(chunk, width - r0)
                w_ref[starts[g] + r0:starts[g] + r0 + rows, :] = (
                    win_ref[src[g] + r0:src[g] + r0 + rows, :].astype(BF16))
            if width < PROJ_WIDTHS[g]:
                w_ref[starts[g] + width:starts[g + 1], :] = jnp.zeros((PROJ_WIDTHS[g] - width, D_MODEL), BF16)

    h = _rms(x_ref[...], g_ref[...]).astype(BF16)

    def cols(group, lo, hi):
        return _dot_nt(h, w_ref[starts[group] + lo:starts[group] + hi, :])

    qk_ref[...] = cols(0, 0, 2 * GLA_QK)
    vr_ref[:, 0:GLA_V] = cols(0, 2 * GLA_QK, 2 * GLA_QK + GLA_V).astype(BF16)
    r = cols(0, 2 * GLA_QK + GLA_V, W_GLA)
    vr_ref[:, GLA_V:] = (r * _sigmoid(r)).astype(BF16)
    lr_ref[...] = cols(1, 0, LANES)
    mqk_ref[...] = cols(2, 0, 2 * ML_W)
    mvo_ref[:, 0:ML_W] = cols(2, 2 * ML_W, 3 * ML_W).astype(BF16)
    mvo_ref[:, ML_W:] = _sigmoid(cols(2, 3 * ML_W, W_ML)).astype(BF16)
    if_ref[...] = cols(3, 0, LANES)
    mg_ref[...] = _sigmoid(cols(4, 0, W_MG)).astype(BF16)


def _proj_call(x2, g, w_in, tm):
    n = x2.shape[0]
    assert w_in.shape == (sum(PROJ_SOURCE_WIDTHS), D_MODEL)
    return pl.pallas_call(
        _proj_kernel,
        grid=(n // tm,),
        in_specs=[pl.BlockSpec((tm, D_MODEL), lambda i: (i, 0)),
                  _full_spec(g.shape),
                  pl.BlockSpec(w_in.shape, lambda i: (0, 0), pipeline_mode=pl.Buffered(1))],
        out_specs=[pl.BlockSpec((tm, w), lambda i: (i, 0)) for w, _ in PROJ_OUTPUTS],
        out_shape=[jax.ShapeDtypeStruct((n, w), dt) for w, dt in PROJ_OUTPUTS],
        scratch_shapes=[pltpu.VMEM((sum(PROJ_WIDTHS), D_MODEL), BF16)],
        compiler_params=pltpu.CompilerParams(
            dimension_semantics=("arbitrary",), vmem_limit_bytes=VMEM_LIMIT),
        name="in_proj",
    )(x2, g, w_in)


def _gla_consts(c):
    nlev = int(math.log2(c))
    assert 1 << nlev == c
    t = np.arange(c)[:, None]
    j = np.arange(c)[None, :]
    lv = np.full((c, c), -1, np.int32)
    for l in range(nlev):
        h = c >> (l + 1)
        upper = (t % (2 * h)) >= h
        same = (j // (2 * h)) == (t // (2 * h))
        s_lower = (j % (2 * h)) < h
        lv[np.broadcast_to(upper, (c, c)) & same & s_lower] = l
    lv[np.eye(c, dtype=bool)] = nlev
    tri = (j <= t).astype(np.float32)
    return jnp.asarray(tri, BF16), jnp.asarray(np.concatenate([lv, lv], axis=1))


def _gla_kernel(*refs, c, t, has_state):
    if has_state:
        (qk_ref, vr_ref, plr_ref, s0_ref, wg2_ref, bg_ref, gn_ref, tri_ref, lv_ref,
         o_ref, sout_ref, s_scr) = refs
    else:
        (qk_ref, vr_ref, plr_ref, wg2_ref, bg_ref, gn_ref, tri_ref, lv_ref,
         o_ref, sout_ref, s_scr) = refs
    nlev = int(math.log2(c))
    step = pl.program_id(1)

    @pl.when(step == 0)
    def _():
        if has_state:
            s_scr[...] = s0_ref[...]
        else:
            s_scr[...] = jnp.zeros_like(s_scr)

    lane_k = lax.broadcasted_iota(jnp.int32, (c, GLA_QK), 1)
    first_of_pair = (lane_k % (2 * GLA_DK)) < GLA_DK
    row_k = lax.broadcasted_iota(jnp.int32, (GLA_QK, GLA_DV), 0)
    row_t = lax.broadcasted_iota(jnp.int32, (c, GLA_QK), 0)

    def block_ref(b, blk, idx):
        if blk >= 8:
            b3 = b.reshape(c // blk, blk, GLA_QK)
            return jnp.broadcast_to(b3[:, idx:idx + 1, :], b3.shape).reshape(c, GLA_QK)
        r = row_t % blk
        out = b
        for sh in range(-idx, blk - idx):
            if sh != 0:
                out = jnp.where(r - idx == sh, pltpu.roll(b, sh % c, axis=0), out)
        return out

    def chunk(ci, carry):
        r0 = ci * c
        rows = pl.ds(r0, c)
        q = qk_ref[rows, 0:GLA_QK] * (GLA_DK ** -0.5)
        k = qk_ref[rows, GLA_QK:2 * GLA_QK]
        v = vr_ref[rows, 0:GLA_V]
        glr = plr_ref[rows, :]
        g_hi = glr.astype(BF16)
        g_lo = (glr - g_hi.astype(F32)).astype(BF16)
        z = (_dot(g_hi, wg2_ref[0]) + _dot(g_lo, wg2_ref[0]) + _dot(g_hi, wg2_ref[1])) + bg_ref[...]
        la = _log_sigmoid(z) * (1.0 / GLA_TAU)
        b = _dot_exact_lhs(tri_ref[...], la)
        b_last = b[c - 1:c, :]
        qe = (q * jnp.exp(b)).astype(BF16)
        kl = (k * jnp.exp(b_last - b)).astype(BF16)
        dcol = jnp.exp(jnp.broadcast_to(b_last, (LANES, GLA_QK)).T)

        k_a = jnp.where(first_of_pair, k, 0.0)
        k_b = k - k_a
        lv2 = lv_ref[...]
        a = [jnp.zeros((c, 2 * c), F32) for _ in range(GLA_HEADS // 2)]
        for l in range(nlev + 1):
            if l < nlev:
                half = c >> (l + 1)
                d = b - block_ref(b, 2 * half, half - 1)
                e = jnp.exp(jnp.minimum(d, -d))
                qt, kta, ktb = q * e, k_a * e, k_b * e
            else:
                qt, kta, ktb = q, k_a, k_b
            qt, kta, ktb = qt.astype(BF16), kta.astype(BF16), ktb.astype(BF16)
            for pr in range(GLA_HEADS // 2):
                ls = slice(pr * 2 * GLA_DK, (pr + 1) * 2 * GLA_DK)
                rhs = jnp.concatenate([kta[:, ls], ktb[:, ls]], axis=0)
                a[pr] = jnp.where(lv2 == l, _dot_nt(qt[:, ls], rhs), a[pr])

        s_all = s_scr[...]
        s_bd = jnp.concatenate(
            [jnp.where((row_k // GLA_DK) == h, s_all, 0.0).astype(BF16) for h in range(GLA_HEADS)], axis=1)
        o_inter = _dot(qe, s_bd)
        u_all = _dot_tn(kl, v)
        for h in range(GLA_HEADS):
            vs = slice(h * GLA_DV, (h + 1) * GLA_DV)
            ks = slice(h * GLA_DK, (h + 1) * GLA_DK)
            a_h = a[h // 2][:, (h % 2) * c:(h % 2 + 1) * c]
            o = _dot(a_h.astype(BF16), v[:, vs]) + o_inter[:, vs]
            on = _rms(o, gn_ref[:, vs])
            gate = vr_ref[rows, GLA_V + h * GLA_DV:GLA_V + (h + 1) * GLA_DV]
            o_ref[rows, vs] = (on * gate.astype(F32)).astype(o_ref.dtype)
            s_scr[ks, :] = dcol[ks, :] * s_all[ks, :] + u_all[ks, vs]
        return carry

    for ci in range(t // c):
        chunk(ci, 0)

    @pl.when(step == pl.num_programs(1) - 1)
    def _():
        sout_ref[...] = s_scr[...]


def _gla_call(qk, vr, plr, s0, wg2_p, bg, gn, c, t):
    b, l, _ = qk.shape
    tri, lv2 = _gla_consts(c)
    has_state = s0 is not None
    tile = lambda w: pl.BlockSpec((None, t, w), lambda bi, i: (bi, i, 0))
    state_spec = pl.BlockSpec((None, GLA_QK, GLA_DV), lambda bi, i: (bi, 0, 0))
    in_specs = [tile(2 * GLA_QK), tile(2 * GLA_V), tile(LANES)]
    args = [qk, vr, plr]
    if has_state:
        in_specs.append(state_spec)
        args.append(s0)
    consts = [wg2_p, bg, gn, tri, lv2]
    in_specs += [_full_spec(x.shape) for x in consts]
    return pl.pallas_call(
        functools.partial(_gla_kernel, c=c, t=t, has_state=has_state),
        grid=(b, l // t),
        in_specs=in_specs,
        out_specs=[tile(GLA_V), state_spec],
        out_shape=[jax.ShapeDtypeStruct((b, l, GLA_V), BF16),
                   jax.ShapeDtypeStruct((b, GLA_QK, GLA_DV), F32)],
        scratch_shapes=[pltpu.VMEM((GLA_QK, GLA_DV), F32)],
        compiler_params=pltpu.CompilerParams(
            dimension_semantics=("arbitrary", "arbitrary"), vmem_limit_bytes=VMEM_LIMIT),
        name="gla",
    )(*args, *consts)


def _mlstm_kernel(*refs, c, t, has_state):
    if has_state:
        (mqk_ref, mvo_ref, pif_ref, c0_ref, n0_ref, m0_ref, cv0_ref, cw_ref, cb_ref, bif_ref, tri_ref, sel_ref,
         o_ref, cout_ref, nout_ref, mout_ref, cvout_ref, c_scr, m_scr, cv_scr, qk_scr) = refs
    else:
        (mqk_ref, mvo_ref, pif_ref, cw_ref, cb_ref, bif_ref, tri_ref, sel_ref,
         o_ref, cout_ref, nout_ref, mout_ref, cvout_ref, c_scr, m_scr, cv_scr, qk_scr) = refs
    step = pl.program_id(1)
    hist = CONV_W - 1

    @pl.when(step == 0)
    def _():
        if has_state:
            for h in range(ML_HEADS):
                c_scr[h, :, 0:ML_DH] = c0_ref[h]
                c_scr[h, :, ML_DH:] = jnp.broadcast_to(n0_ref[h], (ML_DH, ML_DH)).T
            m_scr[...] = m0_ref[...]
            cv_scr[0:CONV_PAD - hist, :] = jnp.zeros((CONV_PAD - hist, 2 * ML_W), F32)
            cv_scr[CONV_PAD - hist:CONV_PAD, :] = cv0_ref[...]
        else:
            c_scr[...] = jnp.zeros_like(c_scr)
            m_scr[...] = jnp.zeros_like(m_scr)
            cv_scr[0:CONV_PAD, :] = jnp.zeros((CONV_PAD, 2 * ML_W), F32)

    conv = _causal_conv_silu(cv_scr, mqk_ref[...], cw_ref, cb_ref)
    qk_scr[:, 0:ML_W] = conv[:, 0:ML_W].astype(BF16)
    qk_scr[:, ML_W:] = (conv[:, ML_W:] * (ML_DH ** -0.5)).astype(BF16)

    gts = pif_ref[...] + bif_ref[...]
    flog = pltpu.roll(_log_sigmoid(gts), LANES - ML_HEADS, axis=1)

    lane = lax.broadcasted_iota(jnp.int32, (c, LANES), 1)
    row_c = lax.broadcasted_iota(jnp.int32, (c, LANES), 0)
    causal = (lax.broadcasted_iota(jnp.int32, (c, c), 1) <= lax.broadcasted_iota(jnp.int32, (c, c), 0))
    ones_v = jnp.ones((c, ML_DH), BF16)

    def slots(pieces):
        out = jnp.zeros((c, LANES), F32)
        for j, piece in enumerate(pieces):
            if not isinstance(piece, float) and j > 0:
                piece = pltpu.roll(piece, ML_HEADS * j, axis=1)
            out = jnp.where((lane >= ML_HEADS * j) & (lane < ML_HEADS * (j + 1)), piece, out)
        return out

    def split3f(x):
        return [p.astype(F32) for p in _split3(x)]

    def chunk(ci, carry):
        r0 = ci * c
        rows = pl.ds(r0, c)
        ip = gts[r0:r0 + c]
        bc = _dot_exact_lhs(tri_ref[...], flog[r0:r0 + c])
        w = ip - bc
        cm = w
        for j in range(int(math.log2(c))):
            sh = 1 << j
            cm = jnp.where(row_c >= sh, jnp.maximum(cm, pltpu.roll(cm, sh, axis=0)), cm)
        mprev = m_scr[0:1, :]
        g = jnp.maximum(mprev, cm)
        g_last = g[c - 1:c, :]
        m_scr[...] = jnp.broadcast_to(bc[c - 1:c, :] + g_last, m_scr.shape)
        lhs_all = slots([1.0, 1.0, 1.0] + split3f(g))
        rhs = slots(split3f(w) + [-1.0, -1.0, -1.0]).astype(BF16)
        y = slots(split3f(mprev - g) + split3f(-(bc + g)) + split3f(w - g_last)).astype(BF16)
        for h in range(ML_HEADS):
            hs = slice(h * ML_DH, (h + 1) * ML_DH)
            qh = qk_scr[rows, h * ML_DH:(h + 1) * ML_DH]
            kh = qk_scr[rows, ML_W + h * ML_DH:ML_W + (h + 1) * ML_DH]
            vaug = jnp.concatenate(
                [mvo_ref[rows, h * ML_DH:(h + 1) * ML_DH], ones_v],
                axis=1)
            lhs = jnp.where(lane % ML_HEADS == h, lhs_all, 0.0).astype(BF16)
            p = jnp.exp(jnp.where(causal, _dot_nt(lhs, rhs), -jnp.inf))
            bx = jnp.exp(_dot(y, sel_ref[h]))
            w_inter = bx[:, 0:ML_DH]
            e_mt = bx[:, ML_DH:2 * ML_DH]
            w_state = bx[:, 2 * ML_DH:3 * ML_DH]
            s = (_dot_nt(qh, kh) * p).astype(BF16)
            caug = c_scr[h]
            nd = _dot(s, vaug) + jnp.concatenate([w_inter, w_inter], axis=1) * _dot(qh, caug.astype(BF16))
            hh = nd[:, 0:ML_DH] / jnp.maximum(jnp.abs(nd[:, ML_DH:]), e_mt)
            o_gate = mvo_ref[rows, ML_W + h * ML_DH:ML_W + (h + 1) * ML_DH]
            o_ref[rows, hs] = (o_gate.astype(F32) * hh).astype(o_ref.dtype)
            ks = (kh.astype(F32) * w_state).astype(BF16)
            dec = w_inter[c - 1:c, :]
            c_scr[h] = jnp.concatenate([dec, dec], axis=1) * caug + _dot_tn(ks, vaug)
        return carry

    for ci in range(t // c):
        chunk(ci, 0)

    @pl.when(step == pl.num_programs(1) - 1)
    def _():
        for h in range(ML_HEADS):
            cout_ref[h] = c_scr[h, :, 0:ML_DH]
            nout_ref[h] = c_scr[h, :, ML_DH:].T[0:8, :]
        mout_ref[...] = m_scr[...]
        cvout_ref[...] = mqk_ref[t - hist:t, :]


def _mlstm_call(mqk, mvo, pif, c0, n0, m0, cv0, cw, cb, bif, c, t):
    b, l, _ = mqk.shape
    has_state = c0 is not None
    tri = jnp.asarray(np.tril(np.ones((c, c), np.float32)), BF16)
    sel = np.zeros((ML_HEADS, LANES, 3 * ML_DH), np.float32)
    for h in range(ML_HEADS):
        for slot in range(9):
            sel[h, ML_HEADS * slot + h, (slot // 3) * ML_DH:(slot // 3 + 1) * ML_DH] = 1.0
    sel = jnp.asarray(sel, BF16)
    tile = lambda w: pl.BlockSpec((None, t, w), lambda bi, i: (bi, i, 0))
    heads = lambda *shape: pl.BlockSpec((None, ML_HEADS) + shape, lambda bi, i: (bi, 0, 0, 0))
    c_spec = heads(ML_DH, ML_DH)
    m_spec = pl.BlockSpec((None, 8, LANES), lambda bi, i: (bi, 0, 0))
    cv_spec = pl.BlockSpec((None, CONV_W - 1, 2 * ML_W), lambda bi, i: (bi, 0, 0))
    in_specs = [tile(2 * ML_W), tile(2 * ML_W), tile(LANES)]
    args = [mqk, mvo, pif]
    consts = [cw, cb, bif, tri, sel]
    out_specs = [tile(ML_W), c_spec, heads(8, ML_DH), m_spec, cv_spec]
    out_shape = [jax.ShapeDtypeStruct((b, l, ML_W), BF16),
                 jax.ShapeDtypeStruct((b, ML_HEADS, ML_DH, ML_DH), F32),
                 jax.ShapeDtypeStruct((b, ML_HEADS, 8, ML_DH), F32),
                 jax.ShapeDtypeStruct((b, 8, LANES), F32),
                 jax.ShapeDtypeStruct((b, CONV_W - 1, 2 * ML_W), F32)]
    scratch = [pltpu.VMEM((ML_HEADS, ML_DH, 2 * ML_DH), F32), pltpu.VMEM((8, LANES), F32),
               pltpu.VMEM((CONV_PAD + t, 2 * ML_W), F32), pltpu.VMEM((t, 2 * ML_W), BF16)]
    if has_state:
        in_specs += [c_spec, heads(1, ML_DH), m_spec, cv_spec]
        args += [c0, n0, m0, cv0]
    in_specs += [_full_spec(x.shape) for x in consts]
    return pl.pallas_call(
        functools.partial(_mlstm_kernel, c=c, t=t, has_state=has_state),
        grid=(b, l // t),
        in_specs=in_specs,
        out_specs=out_specs,
        out_shape=out_shape,
        scratch_shapes=scratch,
        compiler_params=pltpu.CompilerParams(
            dimension_semantics=("arbitrary", "arbitrary"), vmem_limit_bytes=VMEM_LIMIT),
        name="mlstm",
    )(*args, *consts)


def _merge_kernel(x_ref, ga_ref, hb_ref, pmg_ref, wug_ref, wum_ref, wo_ref, g2_ref,
                  wr_hi_ref, wr_lo_ref, br_ref, tril_ref, triu_ref, cnt0_ref,
                  x1_ref, hs_ref, meta_ref, tab_ref, cnt_scr):
    ya = _dot(ga_ref[...], wug_ref[...])
    yb = _dot(hb_ref[...], wum_ref[...])
    z = pmg_ref[:, 0:D_MODEL].astype(F32) * ya + pmg_ref[:, D_MODEL:].astype(F32) * yb
    x1 = x_ref[...] + _dot(z.astype(BF16), wo_ref[...])
    x1_ref[...] = x1
    hm = _rms(x1, g2_ref[...])
    hm_hi = hm.astype(BF16)
    hm_lo = (hm - hm_hi.astype(F32)).astype(BF16)
    hi_both = _dot(hm_hi, jnp.concatenate([wr_hi_ref[...], wr_lo_ref[...]], axis=1))
    logits = hi_both[:, 0:LANES] + hi_both[:, LANES:] + _dot(hm_lo, wr_hi_ref[...]) + br_ref[...]
    lane = lax.broadcasted_iota(jnp.int32, logits.shape, 1)
    neg = -jnp.inf
    is_g = (lane >= N_EXPERTS) & (lane < N_EXPERTS + N_GROUPS)
    lg = jnp.where(is_g, logits, neg)
    mg = jnp.max(lg, axis=-1, keepdims=True)
    p_top = 1.0 / jnp.sum(jnp.exp(lg - mg), axis=-1, keepdims=True)
    gi = jnp.min(jnp.where(lg == mg, lane, 2 * LANES), axis=-1, keepdims=True) - N_EXPERTS
    group_shift = int(math.log2(EXPERTS_PER_GROUP))
    sel = (lane < N_EXPERTS) & (jnp.right_shift(lane, group_shift) == gi)
    le = jnp.where(sel, logits, neg)
    v1 = jnp.max(le, axis=-1, keepdims=True)
    i1 = jnp.min(jnp.where(le == v1, lane, 2 * LANES), axis=-1, keepdims=True)
    le2 = jnp.where(lane == i1, neg, le)
    v2 = jnp.max(le2, axis=-1, keepdims=True)
    i2 = jnp.min(jnp.where(le2 == v2, lane, 2 * LANES), axis=-1, keepdims=True)
    e2 = jnp.exp(v2 - v1)
    wa = 1.0 / (1.0 + e2)
    wb = e2 / (1.0 + e2)

    @pl.when(pl.program_id(0) == 0)
    def _():
        cnt_scr[...] = cnt0_ref[...]

    tm = logits.shape[0]
    oh1 = lane == i1
    oh2 = lane == i2
    both = jnp.where(oh1 | oh2, 1.0, 0.0)
    cnt = jnp.sum(both, axis=0, keepdims=True)
    cnt = jnp.floor((cnt + (SEG_ALIGN - 1)) * (1.0 / SEG_ALIGN)) * SEG_ALIGN
    lower = _dot(jnp.broadcast_to(cnt, (8, LANES)).astype(BF16), triu_ref[...])[0:1, :]
    lpos = _dot(tril_ref[...], both.astype(BF16)) + lower
    lp1 = jnp.sum(jnp.where(oh1, lpos, 0.0), axis=-1, keepdims=True)
    lp2 = jnp.sum(jnp.where(oh2, lpos, 0.0), axis=-1, keepdims=True)
    pos = lax.broadcasted_iota(jnp.int32, (tm, hs_ref.shape[0]), 1)
    onehot = jnp.where((pos == lp1.astype(jnp.int32)) | (pos == lp2.astype(jnp.int32)), 1.0, 0.0)
    hs_ref[...] = _dot_tn(onehot.astype(BF16), hm_hi).astype(BF16)
    cols = (lp1, lp2, p_top * wa, p_top * wb)
    meta = jnp.zeros(logits.shape, F32)
    for ci, col in enumerate(cols):
        meta = jnp.where(lane == ci, col, meta)
    meta_ref[...] = meta
    row8 = lax.broadcasted_iota(jnp.int32, (8, LANES), 0)
    tab_ref[...] = jnp.where(row8 == 0, cnt, jnp.where(row8 == 1, lower, jnp.where(row8 == 2, cnt_scr[...], 0.0)))
    cnt_scr[...] += cnt


def _merge_call(x2, ga, hb, pmg, wug, wum, wo, g2, wr_hi, wr_lo, br, cnt0, tm):
    n = x2.shape[0]
    tile = lambda w: pl.BlockSpec((tm, w), lambda i: (i, 0))
    tiles = lambda *shape: pl.BlockSpec((None,) + shape, lambda i: (i, 0, 0))
    local_rows = 2 * tm + SEG_ALIGN * N_EXPERTS
    consts = [wug, wum, wo, g2, wr_hi, wr_lo, br,
              jnp.asarray(np.tril(np.ones((tm, tm), np.float32), -1), BF16),
              jnp.asarray(np.triu(np.ones((LANES, LANES), np.float32), 1), BF16),
              cnt0]
    return pl.pallas_call(
        _merge_kernel,
        grid=(n // tm,),
        in_specs=[tile(D_MODEL), tile(GLA_V), tile(ML_W), tile(W_MG)]
                 + [_full_spec(x.shape) for x in consts],
        out_specs=[tile(D_MODEL), tiles(local_rows, D_MODEL), tile(LANES), tiles(8, LANES)],
        out_shape=[jax.ShapeDtypeStruct((n, D_MODEL), F32),
                   jax.ShapeDtypeStruct((n // tm, local_rows, D_MODEL), BF16),
                   jax.ShapeDtypeStruct((n, LANES), F32),
                   jax.ShapeDtypeStruct((n // tm, 8, LANES), F32)],
        scratch_shapes=[pltpu.VMEM((1, LANES), F32)],
        compiler_params=pltpu.CompilerParams(
            dimension_semantics=("arbitrary",), vmem_limit_bytes=VMEM_LIMIT),
        name="merge",
    )(x2, ga, hb, pmg, *consts)


MOE_ROWS = 512
SEG_ALIGN = 16
SEG_SIZES = (512, 256, 128, 64, 32, 16)


def _segment_copies(fn, tables, tile, local_ref, buffer, flat_ref, sem, to_flat):
    cnt_ref, loff_ref, gpos_ref = tables

    def body(e, carry):
        idx = tile * N_EXPERTS + e
        n, lo, gp = cnt_ref[idx], loff_ref[idx], gpos_ref[idx]
        for size in SEG_SIZES:
            @pl.when((n & size) != 0)
            def _():
                off = n & ~(2 * size - 1)
                local = local_ref.at[buffer, pl.ds(pl.multiple_of(lo + off, SEG_ALIGN), size), :]
                flat = flat_ref.at[pl.ds(pl.multiple_of(gp + off, SEG_ALIGN), size), :]
                src, dst = (local, flat) if to_flat else (flat, local)
                fn(pltpu.make_async_copy(src, dst, sem))
        return carry

    lax.fori_loop(0, N_EXPERTS, body, 0)


def _dispatch_kernel(*refs, tile0, first):
    if first:
        cnt_ref, loff_ref, gpos_ref, tail_ref, hs_ref, xs_ref, zero_ref, sem, zsem = refs
    else:
        cnt_ref, loff_ref, gpos_ref, tail_ref, hs_ref, _, xs_ref, zero_ref, sem, zsem = refs
    step = pl.program_id(0)

    @pl.when((step == 0) & first)
    def _():
        zero_ref[...] = jnp.zeros_like(zero_ref)
        n_tiles = xs_ref.shape[0] // MOE_ROWS

        def clear(row):
            start = pl.multiple_of(row, MOE_ROWS)
            return pltpu.make_async_copy(zero_ref, xs_ref.at[pl.ds(start, MOE_ROWS), :], zsem)

        def unused(fn):
            def body(j, carry):
                fn(clear(j * MOE_ROWS))
                return carry
            lax.fori_loop(tail_ref[2 * N_EXPERTS], n_tiles, body, 0)

        for e in range(N_EXPERTS):
            @pl.when(tail_ref[N_EXPERTS + e] > 0)
            def _():
                clear(tail_ref[e]).start()
        unused(lambda c: c.start())
        for e in range(N_EXPERTS):
            @pl.when(tail_ref[N_EXPERTS + e] > 0)
            def _():
                clear(tail_ref[e]).wait()
        unused(lambda c: c.wait())

    tables = (cnt_ref, loff_ref, gpos_ref)
    _segment_copies(lambda c: c.start(), tables, tile0 + step, hs_ref, 0, xs_ref, sem, True)
    _segment_copies(lambda c: c.wait(), tables, tile0 + step, hs_ref, 0, xs_ref, sem, True)


def _dispatch_call(tables, tail, hs, tile0, n_rows, xs=None):
    first = xs is None
    any_spec = pl.BlockSpec(memory_space=pl.ANY)
    in_specs = [pl.BlockSpec((1,) + hs.shape[1:], lambda i, *_: (i, 0, 0))]
    args = [*tables, tail, hs]
    if not first:
        in_specs.append(any_spec)
        args.append(xs)
    return pl.pallas_call(
        functools.partial(_dispatch_kernel, tile0=tile0, first=first),
        grid_spec=pltpu.PrefetchScalarGridSpec(
            num_scalar_prefetch=4,
            grid=(hs.shape[0],),
            in_specs=in_specs,
            out_specs=any_spec,
            scratch_shapes=[pltpu.VMEM((MOE_ROWS, D_MODEL), hs.dtype),
                            pltpu.SemaphoreType.DMA, pltpu.SemaphoreType.DMA],
        ),
        out_shape=jax.ShapeDtypeStruct((n_rows, D_MODEL), hs.dtype),
        input_output_aliases={} if first else {len(args) - 1: 0},
        compiler_params=pltpu.CompilerParams(
            dimension_semantics=("arbitrary",), vmem_limit_bytes=VMEM_LIMIT),
        name="moe_dispatch",
    )(*args)


def _gmm_kernel(te_ref, nv_ref, xs_ref, wg_ref, wu_ref, wd_ref, o_ref, wgu_scr, wd_scr):
    j = pl.program_id(0)
    used = j < nv_ref[0]

    @pl.when(used & ((j == 0) | (te_ref[j] != te_ref[jnp.maximum(j - 1, 0)])))
    def _():
        wgu_scr[:, 0:D_EXPERT] = wg_ref[...].astype(BF16)
        wgu_scr[:, D_EXPERT:] = wu_ref[...].astype(BF16)
        wd_scr[...] = wd_ref[...].astype(BF16)

    @pl.when(used)
    def _():
        au = _dot(xs_ref[...], wgu_scr[...])
        a = au[:, 0:D_EXPERT]
        hh = (a * _sigmoid(a)) * au[:, D_EXPERT:]
        o_ref[...] = _dot(hh.astype(BF16), wd_scr[...]).astype(o_ref.dtype)

    @pl.when(jnp.logical_not(used))
    def _():
        o_ref[...] = jnp.zeros_like(o_ref)


def _gmm_call(tile_expert, n_valid, xs, wg, wu, wd):
    n_tiles = xs.shape[0] // MOE_ROWS
    rows = lambda j, te, nv: (jnp.minimum(j, nv[0] - 1), 0)
    wsel = lambda j, te, nv: (te[j], 0, 0)
    return pl.pallas_call(
        _gmm_kernel,
        grid_spec=pltpu.PrefetchScalarGridSpec(
            num_scalar_prefetch=2,
            grid=(n_tiles,),
            in_specs=[pl.BlockSpec((MOE_ROWS, D_MODEL), rows),
                      pl.BlockSpec((None, D_MODEL, D_EXPERT), wsel),
                      pl.BlockSpec((None, D_MODEL, D_EXPERT), wsel),
                      pl.BlockSpec((None, D_EXPERT, D_MODEL), wsel)],
            out_specs=pl.BlockSpec((MOE_ROWS, D_MODEL), lambda j, te, nv: (j, 0)),
            scratch_shapes=[pltpu.VMEM((D_MODEL, 2 * D_EXPERT), BF16),
                            pltpu.VMEM((D_EXPERT, D_MODEL), BF16)],
        ),
        out_shape=jax.ShapeDtypeStruct(xs.shape, xs.dtype),
        compiler_params=pltpu.CompilerParams(
            dimension_semantics=("arbitrary",), vmem_limit_bytes=VMEM_LIMIT),
        name="moe_grouped",
    )(tile_expert, n_valid, xs, wg, wu, wd)


def _combine_kernel(cnt_ref, loff_ref, gpos_ref, x1_ref, meta_ref, gf_ref, os_ref, y_ref, buf_ref, sem,
                    *, tile0):
    step = pl.program_id(0)
    n_steps = pl.num_programs(0)
    tm = x1_ref.shape[0]
    tables = (cnt_ref, loff_ref, gpos_ref)

    def fetch(fn, i):
        slot = i % 2
        _segment_copies(fn, tables, tile0 + i, buf_ref, slot, os_ref, sem.at[slot], False)

    @pl.when(step == 0)
    def _():
        buf_ref[...] = jnp.zeros_like(buf_ref)
        fetch(lambda c: c.start(), step)

    @pl.when(step + 1 < n_steps)
    def _():
        fetch(lambda c: c.start(), step + 1)

    fetch(lambda c: c.wait(), step)
    rows = buf_ref[step % 2]
    pos = lax.broadcasted_iota(jnp.int32, (tm, rows.shape[0]), 1)
    lp1 = meta_ref[:, 0:1].astype(jnp.int32)
    lp2 = meta_ref[:, 1:2].astype(jnp.int32)
    q = jnp.where(pos == lp1, meta_ref[:, 2:3], 0.0) + jnp.where(pos == lp2, meta_ref[:, 3:4], 0.0)
    y = x1_ref[...] + _dot(q.astype(BF16), rows)
    y_ref[...] = _rms(y, gf_ref[...])


def _combine_call(tables, x1, meta, gf, out_sorted, tile0, local_rows, tm):
    n = x1.shape[0]
    tile = lambda w: pl.BlockSpec((tm, w), lambda i, *_: (i, 0))
    return pl.pallas_call(
        functools.partial(_combine_kernel, tile0=tile0),
        grid_spec=pltpu.PrefetchScalarGridSpec(
            num_scalar_prefetch=3,
            grid=(n // tm,),
            in_specs=[tile(D_MODEL), tile(LANES), pl.BlockSpec(gf.shape, lambda i, *_: (0, 0)),
                      pl.BlockSpec(memory_space=pl.ANY)],
            out_specs=tile(D_MODEL),
            scratch_shapes=[pltpu.VMEM((2, local_rows, D_MODEL), out_sorted.dtype),
                            pltpu.SemaphoreType.DMA((2,))],
        ),
        out_shape=jax.ShapeDtypeStruct((n, D_MODEL), F32),
        compiler_params=pltpu.CompilerParams(
            dimension_semantics=("arbitrary",), vmem_limit_bytes=VMEM_LIMIT),
        name="moe_combine",
    )(*tables, x1, meta, gf, out_sorted)


def _sparse_moe(parts, p):
    token_tiles = [part[1].shape[0] for part in parts]
    n_tiles = -(-sum(part[1].shape[0] * part[1].shape[1] for part in parts) // MOE_ROWS) + N_EXPERTS
    tab = jnp.concatenate([part[3] for part in parts], axis=0)
    tab = tab[:, :, :N_EXPERTS].astype(jnp.int32)
    cnt, loff, before = tab[:, 0], tab[:, 1], tab[:, 2]
    total = before[-1] + cnt[-1]
    tiles = (total + (MOE_ROWS - 1)) // MOE_ROWS
    tile_end = jnp.cumsum(tiles)
    tile_start = tile_end - tiles
    n_valid = tile_end[-1:]
    gpos = tile_start[None, :] * MOE_ROWS + before
    j = jnp.minimum(jnp.arange(n_tiles, dtype=jnp.int32), n_valid - 1)
    tile_expert = jnp.sum((j[:, None] >= tile_end[None, :]).astype(jnp.int32), axis=1)
    tail = jnp.concatenate([(tile_end - 1) * MOE_ROWS, total, n_valid]).astype(jnp.int32)
    tables = (cnt.reshape(-1), loff.reshape(-1), gpos.reshape(-1))
    xs = None
    for part, tile0 in zip(parts, np.cumsum([0] + token_tiles[:-1])):
        xs = _dispatch_call(tables, tail, part[1], int(tile0), n_tiles * MOE_ROWS, xs)
    out_sorted = _gmm_call(tile_expert, n_valid.astype(jnp.int32), xs, p["wg"], p["wu"], p["wd"])
    return [_combine_call(tables, part[0], part[2], p["gf"], out_sorted, int(tile0),
                          part[1].shape[1], part[0].shape[0] // part[1].shape[0])
            for part, tile0 in zip(parts, np.cumsum([0] + token_tiles[:-1]))]


def _pad_cols(w, width):
    return jnp.pad(w, ((0, 0), (0, width - w.shape[1])))


def _prep_weights(norm1_g, w_in, gla_w_gate2, gla_b_gate, gla_norm_g, w_up_gla,
                  ml_conv_w, ml_conv_b, ml_b_i, ml_b_f, w_up_ml, w_out,
                  norm2_g, router_g_w, router_g_b, router_e_w, router_e_b,
                  moe_w_gate, moe_w_up, moe_w_down, final_g):
    wr =_pad_cols(jnp.concatenate([router_e_w, router_g_w], axis=1), LANES)
    wr_hi = wr.astype(BF16)
    wr_lo = (wr - wr_hi.astype(F32)).astype(BF16)
    br = _pad_cols(jnp.concatenate([router_e_b, router_g_b])[None, :], LANES)
    wg2 = jnp.pad(gla_w_gate2, ((0, LANES - GLA_GATE_RANK), (0, 0)))
    wg2_hi = wg2.astype(BF16)
    return dict(
        g1=norm1_g[None, :], w_in=w_in.T,
        wg2_p=jnp.stack([wg2_hi, (wg2 - wg2_hi.astype(F32)).astype(BF16)]),
        bg=gla_b_gate[None, :], gn=gla_norm_g[None, :],
        wug=w_up_gla.astype(BF16),
        cw=ml_conv_w, cb=ml_conv_b[None, :],
        bif=_pad_cols(jnp.concatenate([ml_b_i, ml_b_f])[None, :], LANES),
        wum=w_up_ml.astype(BF16), wo=w_out.astype(BF16),
        g2=norm2_g[None, :], wr_hi=wr_hi, wr_lo=wr_lo, br=br,
        wg=moe_w_gate.reshape(N_EXPERTS, D_MODEL, D_EXPERT),
        wu=moe_w_up.reshape(N_EXPERTS, D_MODEL, D_EXPERT),
        wd=moe_w_down.reshape(N_EXPERTS, D_EXPERT, D_MODEL),
        gf=final_g[None, :],
    )


def _mixers(x, gla_s0, ml_c0, ml_n0, ml_m0, conv0, p, placed, *,
            gla_chunk, ml_chunk, seq_tile, row_tile, merge_tile):
    b, l, _ = x.shape
    n = b * l
    x2 = x.reshape(n, D_MODEL)
    qk, vr, plr, mqk, mvo, pif, pmg = _proj_call(x2, p["g1"], p["w_in"], row_tile)
    r3 = lambda a: a.reshape(b, l, a.shape[-1])
    s0 = None if gla_s0 is None else gla_s0.reshape(b, GLA_QK, GLA_DV)
    ga, gla_s = _gla_call(r3(qk), r3(vr), r3(plr), s0, p["wg2_p"], p["bg"], p["gn"], gla_chunk, seq_tile)
    gla_s = gla_s.reshape(b, GLA_HEADS, GLA_DK, GLA_DV)
    if ml_c0 is None:
        n0 = m0 = None
    else:
        n0 = ml_n0[:, :, None, :]
        m0 = jnp.broadcast_to(_pad_cols(ml_m0, LANES)[:, None, :], (b, 8, LANES))
    hb, ml_c, ml_n, m_b, new_conv = _mlstm_call(r3(mqk), r3(mvo), r3(pif), ml_c0, n0, m0, conv0,
                                                p["cw"], p["cb"], p["bif"], ml_chunk, seq_tile)
    part = _merge_call(x2, ga.reshape(n, GLA_V), hb.reshape(n, ML_W), pmg,
                       p["wug"], p["wum"], p["wo"], p["g2"],
                       p["wr_hi"], p["wr_lo"], p["br"], placed, merge_tile)
    states = (gla_s[None], ml_c[None], ml_n[:, :, 0, :][None], m_b[:, 0, 0:ML_HEADS][None], new_conv[None])
    return part, states


def kernel(x_prompt, x_sample, state_gla_S, state_mlstm_C, state_mlstm_n, state_mlstm_m, state_mlstm_conv, norm1_g, w_in, gla_w_gate2, gla_b_gate, gla_norm_g, w_up_gla, ml_conv_w, ml_conv_b, ml_b_i, ml_b_f, w_up_ml, w_out, norm2_g, router_g_w, router_g_b, router_e_w, router_e_b, moe_w_gate, moe_w_up, moe_w_down, final_g):
    assert norm1_g.shape[0] == 1, "single-layer trunk"
    p = _prep_weights(norm1_g[0], w_in[0], gla_w_gate2[0], gla_b_gate[0], gla_norm_g[0], w_up_gla[0],
                      ml_conv_w[0], ml_conv_b[0], ml_b_i[0], ml_b_f[0], w_up_ml[0], w_out[0],
                      norm2_g[0], router_g_w[0], router_g_b[0], router_e_w[0], router_e_b[0],
                      moe_w_gate[0], moe_w_up[0], moe_w_down[0], final_g)
    dec_seq = x_sample.shape[1]
    n_sample = x_sample.shape[0] * dec_seq
    part_p, sp = _mixers(x_prompt, None, None, None, None, None, p, jnp.zeros((1, LANES), F32),
                         gla_chunk=128, ml_chunk=256, seq_tile=1024, row_tile=256, merge_tile=512)
    placed = part_p[3][-1, 0:1, :] + part_p[3][-1, 2:3, :]
    part_s, ss = _mixers(x_sample, state_gla_S[0], state_mlstm_C[0], state_mlstm_n[0], state_mlstm_m[0],
                         state_mlstm_conv[0], p, placed,
                         gla_chunk=dec_seq, ml_chunk=dec_seq, seq_tile=dec_seq,
                         row_tile=n_sample, merge_tile=n_sample)
    yp, ys = _sparse_moe([part_p, part_s], p)
    return (yp.reshape(x_prompt.shape), ys.reshape(x_sample.shape), *sp, *ss)
```

```python
import functools
import math

import numpy as np
import jax
import jax.numpy as jnp
from jax import lax
from jax.experimental import pallas as pl
from jax.experimental.pallas import tpu as pltpu

D_MODEL = 1024
GLA_HEADS = 4
GLA_DK = 64
GLA_DV = 128
GLA_GATE_RANK = 16
GLA_TAU = 16.0
ML_HEADS = 4
ML_DH = 128
CONV_W = 4
N_GROUPS = 4
EXPERTS_PER_GROUP = 8
N_EXPERTS = N_GROUPS * EXPERTS_PER_GROUP
D_EXPERT = 256
EPS = 1e-6

GLA_QK = GLA_HEADS * GLA_DK
GLA_V = GLA_HEADS * GLA_DV
ML_W = ML_HEADS * ML_DH

LANES = 128
VMEM_LIMIT = 56 * 1024 * 1024

W_GLA = 2 * GLA_QK + 2 * GLA_V
W_ML = 2 * ML_W + ML_W + ML_W
W_MG = 2 * D_MODEL
PROJ_WIDTHS = (W_GLA, LANES, W_ML, LANES, W_MG)
PROJ_SOURCE_WIDTHS = (W_GLA, GLA_GATE_RANK, W_ML, 2 * ML_HEADS, W_MG)

F32 = jnp.float32
BF16 = jnp.bfloat16

PROJ_OUTPUTS = ((2 * GLA_QK, F32), (2 * GLA_V, BF16), (LANES, F32), (2 * ML_W, F32), (2 * ML_W, BF16),
                (LANES, F32), (W_MG, BF16))


def _dot(a, b):
    return jnp.dot(a, b, preferred_element_type=F32)


def _dot_nt(a, b):
    return lax.dot_general(a, b, (((1,), (1,)), ((), ())), preferred_element_type=F32)


def _dot_tn(a, b):
    return lax.dot_general(a, b, (((0,), (0,)), ((), ())), preferred_element_type=F32)


def _split3(x):
    hi = x.astype(BF16)
    r1 = x - hi.astype(F32)
    mid = r1.astype(BF16)
    lo = (r1 - mid.astype(F32)).astype(BF16)
    return hi, mid, lo


def _dot_exact_lhs(m, x):
    hi, mid, lo = _split3(x)
    return _dot(m, hi) + _dot(m, mid) + _dot(m, lo)


def _log_sigmoid(z):
    return jnp.minimum(z, 0.0) - jnp.log(1.0 + jnp.exp(-jnp.abs(z)))


def _sigmoid(z):
    return 1.0 / (1.0 + jnp.exp(-z))


def _rms(x, g):
    return x * lax.rsqrt(jnp.mean(x * x, axis=-1, keepdims=True) + EPS) * g


def _full_spec(shape):
    nd = len(shape)
    return pl.BlockSpec(shape, lambda *_: (0,) * nd)


CONV_PAD = 8


def _causal_conv_silu(stage_ref, x, cw_ref, cb_ref):
    t = x.shape[0]
    stage_ref[CONV_PAD:CONV_PAD + t, :] = x
    acc = cb_ref[...] + stage_ref[CONV_PAD:CONV_PAD + t, :] * cw_ref[CONV_W - 1:CONV_W, :]
    for d in range(1, CONV_W):
        acc = acc + stage_ref[CONV_PAD - d:CONV_PAD - d + t, :] * cw_ref[CONV_W - 1 - d:CONV_W - d, :]
    stage_ref[0:CONV_PAD, :] = stage_ref[t:t + CONV_PAD, :]
    return acc * _sigmoid(acc)


def _proj_kernel(x_ref, g_ref, win_ref, qk_ref, vr_ref, lr_ref, mqk_ref, mvo_ref, if_ref, mg_ref, w_ref):
    starts = np.cumsum((0,) + PROJ_WIDTHS)

    @pl.when(pl.program_id(0) == 0)
    def _():
        src = np.cumsum((0,) + PROJ_SOURCE_WIDTHS)
        chunk = 512
        for g, width in enumerate(PROJ_SOURCE_WIDTHS):
            for r0 in range(0, width, chunk):
                rows = min(chunk, width - r0)
                w_ref[starts[g] + r0:starts[g] + r0 + rows, :] = (
                    win_ref[src[g] + r0:src[g] + r0 + rows, :].astype(BF16))
            if width < PROJ_WIDTHS[g]:
                w_ref[starts[g] + width:starts[g + 1], :] = jnp.zeros((PROJ_WIDTHS[g] - width, D_MODEL), BF16)

    h = _rms(x_ref[...], g_ref[...]).astype(BF16)

    def cols(group, lo, hi):
        return _dot_nt(h, w_ref[starts[group] + lo:starts[group] + hi, :])

    qk_ref[...] = cols(0, 0, 2 * GLA_QK)
    vr_ref[:, 0:GLA_V] = cols(0, 2 * GLA_QK, 2 * GLA_QK + GLA_V).astype(BF16)
    r = cols(0, 2 * GLA_QK + GLA_V, W_GLA)
    vr_ref[:, GLA_V:] = (r * _sigmoid(r)).astype(BF16)
    lr_ref[...] = cols(1, 0, LANES)
    mqk_ref[...] = cols(2, 0, 2 * ML_W)
    mvo_ref[:, 0:ML_W] = cols(2, 2 * ML_W, 3 * ML_W).astype(BF16)
    mvo_ref[:, ML_W:] = _sigmoid(cols(2, 3 * ML_W, W_ML)).astype(BF16)
    if_ref[...] = cols(3, 0, LANES)
    mg_ref[...] = _sigmoid(cols(4, 0, W_MG)).astype(BF16)


def _proj_call(x2, g, w_in, tm):
    n = x2.shape[0]
    assert w_in.shape == (sum(PROJ_SOURCE_WIDTHS), D_MODEL)
    return pl.pallas_call(
        _proj_kernel,
        grid=(n // tm,),
        in_specs=[pl.BlockSpec((tm, D_MODEL), lambda i: (i, 0)),
                  _full_spec(g.shape),
                  pl.BlockSpec(w_in.shape, lambda i: (0, 0), pipeline_mode=pl.Buffered(1))],
        out_specs=[pl.BlockSpec((tm, w), lambda i: (i, 0)) for w, _ in PROJ_OUTPUTS],
        out_shape=[jax.ShapeDtypeStruct((n, w), dt) for w, dt in PROJ_OUTPUTS],
        scratch_shapes=[pltpu.VMEM((sum(PROJ_WIDTHS), D_MODEL), BF16)],
        compiler_params=pltpu.CompilerParams(
            dimension_semantics=("arbitrary",), vmem_limit_bytes=VMEM_LIMIT),
        name="in_proj",
    )(x2, g, w_in)


def _gla_consts(c):
    nlev = int(math.log2(c))
    assert 1 << nlev == c
    t = np.arange(c)[:, None]
    j = np.arange(c)[None, :]
    lv = np.full((c, c), -1, np.int32)
    for l in range(nlev):
        h = c >> (l + 1)
        upper = (t % (2 * h)) >= h
        same = (j // (2 * h)) == (t // (2 * h))
        s_lower = (j % (2 * h)) < h
        lv[np.broadcast_to(upper, (c, c)) & same & s_lower] = l
    lv[np.eye(c, dtype=bool)] = nlev
    tri = (j <= t).astype(np.float32)
    return jnp.asarray(tri, BF16), jnp.asarray(np.concatenate([lv, lv], axis=1))


def _gla_kernel(*refs, c, t, has_state):
    if has_state:
        (qk_ref, vr_ref, plr_ref, s0_ref, wg2_ref, bg_ref, gn_ref, tri_ref, lv_ref,
         o_ref, sout_ref, s_scr) = refs
    else:
        (qk_ref, vr_ref, plr_ref, wg2_ref, bg_ref, gn_ref, tri_ref, lv_ref,
         o_ref, sout_ref, s_scr) = refs
    nlev = int(math.log2(c))
    step = pl.program_id(1)

    @pl.when(step == 0)
    def _():
        if has_state:
            s_scr[...] = s0_ref[...]
        else:
            s_scr[...] = jnp.zeros_like(s_scr)

    n_chunks = t // c
    lane_k = lax.broadcasted_iota(jnp.int32, (t, GLA_QK), 1)
    first_of_pair = (lane_k % (2 * GLA_DK)) < GLA_DK
    row_k = lax.broadcasted_iota(jnp.int32, (GLA_QK, GLA_DV), 0)
    row_t = lax.broadcasted_iota(jnp.int32, (t, GLA_QK), 0)

    def block_ref(b, blk, idx):
        if blk >= 8:
            b3 = b.reshape(t // blk, blk, GLA_QK)
            return jnp.broadcast_to(b3[:, idx:idx + 1, :], b3.shape).reshape(t, GLA_QK)
        r = row_t % blk
        out = b
        for sh in range(-idx, blk - idx):
            if sh != 0:
                out = jnp.where(r - idx == sh, pltpu.roll(b, sh % t, axis=0), out)
        return out

    q = qk_ref[:, 0:GLA_QK] * (GLA_DK ** -0.5)
    k = qk_ref[:, GLA_QK:2 * GLA_QK]
    glr = plr_ref[...]
    g_hi = glr.astype(BF16)
    g_lo = (glr - g_hi.astype(F32)).astype(BF16)
    z = (_dot(g_hi, wg2_ref[0]) + _dot(g_lo, wg2_ref[0]) + _dot(g_hi, wg2_ref[1])) + bg_ref[...]
    la = _log_sigmoid(z) * (1.0 / GLA_TAU)
    b = jnp.concatenate([_dot_exact_lhs(tri_ref[...], la[ci * c:(ci + 1) * c])
                         for ci in range(n_chunks)], axis=0)
    b_last = block_ref(b, c, c - 1)
    qe = (q * jnp.exp(b)).astype(BF16)
    kl = (k * jnp.exp(b_last - b)).astype(BF16)

    k_a = jnp.where(first_of_pair, k, 0.0)
    k_b = k - k_a
    lv2 = lv_ref[...]
    factors = []
    for l in range(nlev + 1):
        if l < nlev:
            half = c >> (l + 1)
            d = b - block_ref(b, 2 * half, half - 1)
            e = jnp.exp(jnp.minimum(d, -d))
            qt, kta, ktb = q * e, k_a * e, k_b * e
        else:
            qt, kta, ktb = q, k_a, k_b
        factors.append((qt.astype(BF16), kta.astype(BF16), ktb.astype(BF16)))
    a = [[None] * (GLA_HEADS // 2) for _ in range(n_chunks)]
    for ci in range(n_chunks):
        rows = slice(ci * c, (ci + 1) * c)
        for pr in range(GLA_HEADS // 2):
            ls = slice(pr * 2 * GLA_DK, (pr + 1) * 2 * GLA_DK)
            acc = jnp.zeros((c, 2 * c), F32)
            for l, (qt, kta, ktb) in enumerate(factors):
                rhs = jnp.concatenate([kta[rows, ls], ktb[rows, ls]], axis=0)
                acc = jnp.where(lv2 == l, _dot_nt(qt[rows, ls], rhs), acc)
            a[ci][pr] = acc

    for ci in range(n_chunks):
        rows = slice(ci * c, (ci + 1) * c)
        v = vr_ref[rows, 0:GLA_V]
        s_all = s_scr[...]
        s_bd = jnp.concatenate(
            [jnp.where((row_k // GLA_DK) == h, s_all, 0.0).astype(BF16) for h in range(GLA_HEADS)], axis=1)
        o_inter = _dot(qe[rows], s_bd)
        u_all = _dot_tn(kl[rows], v)
        dcol = jnp.exp(jnp.broadcast_to(b[ci * c + c - 1:(ci + 1) * c, :], (LANES, GLA_QK)).T)
        for h in range(GLA_HEADS):
            vs = slice(h * GLA_DV, (h + 1) * GLA_DV)
            ks = slice(h * GLA_DK, (h + 1) * GLA_DK)
            a_h = a[ci][h // 2][:, (h % 2) * c:(h % 2 + 1) * c]
            o = _dot(a_h.astype(BF16), v[:, vs]) + o_inter[:, vs]
            on = _rms(o, gn_ref[:, vs])
            gate = vr_ref[rows, GLA_V + h * GLA_DV:GLA_V + (h + 1) * GLA_DV]
            o_ref[rows, vs] = (on * gate.astype(F32)).astype(o_ref.dtype)
            s_scr[ks, :] = dcol[ks, :] * s_all[ks, :] + u_all[ks, vs]

    @pl.when(step == pl.num_programs(1) - 1)
    def _():
        sout_ref[...] = s_scr[...]


def _gla_call(qk, vr, plr, s0, wg2_p, bg, gn, c, t):
    b, l, _ = qk.shape
    tri, lv2 = _gla_consts(c)
    has_state = s0 is not None
    tile = lambda w: pl.BlockSpec((None, t, w), lambda bi, i: (bi, i, 0))
    state_spec = pl.BlockSpec((None, GLA_QK, GLA_DV), lambda bi, i: (bi, 0, 0))
    in_specs = [tile(2 * GLA_QK), tile(2 * GLA_V), tile(LANES)]
    args = [qk, vr, plr]
    if has_state:
        in_specs.append(state_spec)
        args.append(s0)
    consts = [wg2_p, bg, gn, tri, lv2]
    in_specs += [_full_spec(x.shape) for x in consts]
    return pl.pallas_call(
        functools.partial(_gla_kernel, c=c, t=t, has_state=has_state),
        grid=(b, l // t),
        in_specs=in_specs,
        out_specs=[tile(GLA_V), state_spec],
        out_shape=[jax.ShapeDtypeStruct((b, l, GLA_V), BF16),
                   jax.ShapeDtypeStruct((b, GLA_QK, GLA_DV), F32)],
        scratch_shapes=[pltpu.VMEM((GLA_QK, GLA_DV), F32)],
        compiler_params=pltpu.CompilerParams(
            dimension_semantics=("arbitrary", "arbitrary"), vmem_limit_bytes=VMEM_LIMIT),
        name="gla",
    )(*args, *consts)


def _mlstm_kernel(*refs, c, t, has_state):
    if has_state:
        (mqk_ref, mvo_ref, pif_ref, c0_ref, n0_ref, m0_ref, cv0_ref, cw_ref, cb_ref, bif_ref, tri_ref, sel_ref,
         o_ref, cout_ref, nout_ref, mout_ref, cvout_ref, c_scr, m_scr, cv_scr, qk_scr) = refs
    else:
        (mqk_ref, mvo_ref, pif_ref, cw_ref, cb_ref, bif_ref, tri_ref, sel_ref,
         o_ref, cout_ref, nout_ref, mout_ref, cvout_ref, c_scr, m_scr, cv_scr, qk_scr) = refs
    step = pl.program_id(1)
    hist = CONV_W - 1

    @pl.when(step == 0)
    def _():
        if has_state:
            for h in range(ML_HEADS):
                c_scr[h, :, 0:ML_DH] = c0_ref[h]
                c_scr[h, :, ML_DH:] = jnp.broadcast_to(n0_ref[h], (ML_DH, ML_DH)).T
            m_scr[...] = m0_ref[...]
            cv_scr[0:CONV_PAD - hist, :] = jnp.zeros((CONV_PAD - hist, 2 * ML_W), F32)
            cv_scr[CONV_PAD - hist:CONV_PAD, :] = cv0_ref[...]
        else:
            c_scr[...] = jnp.zeros_like(c_scr)
            m_scr[...] = jnp.zeros_like(m_scr)
            cv_scr[0:CONV_PAD, :] = jnp.zeros((CONV_PAD, 2 * ML_W), F32)

    conv = _causal_conv_silu(cv_scr, mqk_ref[...], cw_ref, cb_ref)
    qk_scr[:, 0:ML_W] = conv[:, 0:ML_W].astype(BF16)
    qk_scr[:, ML_W:] = (conv[:, ML_W:] * (ML_DH ** -0.5)).astype(BF16)

    gts = pif_ref[...] + bif_ref[...]
    flog = pltpu.roll(_log_sigmoid(gts), LANES - ML_HEADS, axis=1)

    n_chunks = t // c
    lane = lax.broadcasted_iota(jnp.int32, (t, LANES), 1)
    row_c = lax.broadcasted_iota(jnp.int32, (t, LANES), 0) % c
    causal = (lax.broadcasted_iota(jnp.int32, (c, c), 1) <= lax.broadcasted_iota(jnp.int32, (c, c), 0))
    ones_v = jnp.ones((c, ML_DH), BF16)

    def slots(pieces):
        out = jnp.zeros((t, LANES), F32)
        for j, piece in enumerate(pieces):
            if not isinstance(piece, float) and j > 0:
                piece = pltpu.roll(piece, ML_HEADS * j, axis=1)
            out = jnp.where((lane >= ML_HEADS * j) & (lane < ML_HEADS * (j + 1)), piece, out)
        return out

    def split3f(x):
        return [p.astype(F32) for p in _split3(x)]

    def per_chunk(rows_of):
        return jnp.concatenate([jnp.broadcast_to(rows_of(ci), (c, LANES)) for ci in range(n_chunks)], axis=0)

    bc = jnp.concatenate([_dot_exact_lhs(tri_ref[...], flog[ci * c:(ci + 1) * c])
                          for ci in range(n_chunks)], axis=0)
    w = gts - bc
    cm = w
    for j in range(int(math.log2(c))):
        sh = 1 << j
        cm = jnp.where(row_c >= sh, jnp.maximum(cm, pltpu.roll(cm, sh, axis=0)), cm)
    m_in = [m_scr[0:1, :]]
    for ci in range(n_chunks):
        last = slice(ci * c + c - 1, (ci + 1) * c)
        m_in.append(bc[last, :] + jnp.maximum(m_in[ci], cm[last, :]))
    m_scr[...] = jnp.broadcast_to(m_in[n_chunks], m_scr.shape)
    mprev = per_chunk(lambda ci: m_in[ci])
    g = jnp.maximum(mprev, cm)
    g_last = per_chunk(lambda ci: g[ci * c + c - 1:(ci + 1) * c, :])
    lhs_all = slots([1.0, 1.0, 1.0] + split3f(g))
    rhs_all = slots(split3f(w) + [-1.0, -1.0, -1.0]).astype(BF16)
    y_all = slots(split3f(mprev - g) + split3f(-(bc + g)) + split3f(w - g_last)).astype(BF16)
    lhs_heads = [jnp.where(lane % ML_HEADS == h, lhs_all, 0.0).astype(BF16) for h in range(ML_HEADS)]

    for ci in range(n_chunks):
        r0 = ci * c
        rows = pl.ds(r0, c)
        for h in range(ML_HEADS):
            hs = slice(h * ML_DH, (h + 1) * ML_DH)
            qh = qk_scr[rows, h * ML_DH:(h + 1) * ML_DH]
            kh = qk_scr[rows, ML_W + h * ML_DH:ML_W + (h + 1) * ML_DH]
            vaug = jnp.concatenate(
                [mvo_ref[rows, h * ML_DH:(h + 1) * ML_DH], ones_v],
                axis=1)
            p = jnp.exp(jnp.where(causal, _dot_nt(lhs_heads[h][r0:r0 + c], rhs_all[r0:r0 + c]), -jnp.inf))
            bx = jnp.exp(_dot(y_all[r0:r0 + c], sel_ref[h]))
            w_inter = bx[:, 0:ML_DH]
            e_mt = bx[:, ML_DH:2 * ML_DH]
            w_state = bx[:, 2 * ML_DH:3 * ML_DH]
            s = (_dot_nt(qh, kh) * p).astype(BF16)
            caug = c_scr[h]
            nd = _dot(s, vaug) + jnp.concatenate([w_inter, w_inter], axis=1) * _dot(qh, caug.astype(BF16))
            hh = nd[:, 0:ML_DH] / jnp.maximum(jnp.abs(nd[:, ML_DH:]), e_mt)
            o_gate = mvo_ref[rows, ML_W + h * ML_DH:ML_W + (h + 1) * ML_DH]
            o_ref[rows, hs] = (o_gate.astype(F32) * hh).astype(o_ref.dtype)
            ks = (kh.astype(F32) * w_state).astype(BF16)
            dec = w_inter[c - 1:c, :]
            c_scr[h] = jnp.concatenate([dec, dec], axis=1) * caug + _dot_tn(ks, vaug)

    @pl.when(step == pl.num_programs(1) - 1)
    def _():
        for h in range(ML_HEADS):
            cout_ref[h] = c_scr[h, :, 0:ML_DH]
            nout_ref[h] = c_scr[h, :, ML_DH:].T[0:8, :]
        mout_ref[...] = m_scr[...]
        cvout_ref[...] = mqk_ref[t - hist:t, :]


def _mlstm_call(mqk, mvo, pif, c0, n0, m0, cv0, cw, cb, bif, c, t):
    b, l, _ = mqk.shape
    has_state = c0 is not None
    tri = jnp.asarray(np.tril(np.ones((c, c), np.float32)), BF16)
    sel = np.zeros((ML_HEADS, LANES, 3 * ML_DH), np.float32)
    for h in range(ML_HEADS):
        for slot in range(9):
            sel[h, ML_HEADS * slot + h, (slot // 3) * ML_DH:(slot // 3 + 1) * ML_DH] = 1.0
    sel = jnp.asarray(sel, BF16)
    tile = lambda w: pl.BlockSpec((None, t, w), lambda bi, i: (bi, i, 0))
    heads = lambda *shape: pl.BlockSpec((None, ML_HEADS) + shape, lambda bi, i: (bi, 0, 0, 0))
    c_spec = heads(ML_DH, ML_DH)
    m_spec = pl.BlockSpec((None, 8, LANES), lambda bi, i: (bi, 0, 0))
    cv_spec = pl.BlockSpec((None, CONV_W - 1, 2 * ML_W), lambda bi, i: (bi, 0, 0))
    in_specs = [tile(2 * ML_W), tile(2 * ML_W), tile(LANES)]
    args = [mqk, mvo, pif]
    consts = [cw, cb, bif, tri, sel]
    out_specs = [tile(ML_W), c_spec, heads(8, ML_DH), m_spec, cv_spec]
    out_shape = [jax.ShapeDtypeStruct((b, l, ML_W), BF16),
                 jax.ShapeDtypeStruct((b, ML_HEADS, ML_DH, ML_DH), F32),
                 jax.ShapeDtypeStruct((b, ML_HEADS, 8, ML_DH), F32),
                 jax.ShapeDtypeStruct((b, 8, LANES), F32),
                 jax.ShapeDtypeStruct((b, CONV_W - 1, 2 * ML_W), F32)]
    scratch = [pltpu.VMEM((ML_HEADS, ML_DH, 2 * ML_DH), F32), pltpu.VMEM((8, LANES), F32),
               pltpu.VMEM((CONV_PAD + t, 2 * ML_W), F32), pltpu.VMEM((t, 2 * ML_W), BF16)]
    if has_state:
        in_specs += [c_spec, heads(1, ML_DH), m_spec, cv_spec]
        args += [c0, n0, m0, cv0]
    in_specs += [_full_spec(x.shape) for x in consts]
    return pl.pallas_call(
        functools.partial(_mlstm_kernel, c=c, t=t, has_state=has_state),
        grid=(b, l // t),
        in_specs=in_specs,
        out_specs=out_specs,
        out_shape=out_shape,
        scratch_shapes=scratch,
        compiler_params=pltpu.CompilerParams(
            dimension_semantics=("arbitrary", "arbitrary"), vmem_limit_bytes=VMEM_LIMIT),
        name="mlstm",
    )(*args, *consts)


def _merge_kernel(x_ref, ga_ref, hb_ref, pmg_ref, wug_ref, wum_ref, wo_ref, g2_ref,
                  wr_hi_ref, wr_lo_ref, br_ref, tril_ref, triu_ref, cnt0_ref,
                  x1_ref, hs_ref, meta_ref, tab_ref, cnt_scr):
    ya = _dot(ga_ref[...], wug_ref[...])
    yb = _dot(hb_ref[...], wum_ref[...])
    z = pmg_ref[:, 0:D_MODEL].astype(F32) * ya + pmg_ref[:, D_MODEL:].astype(F32) * yb
    x1 = x_ref[...] + _dot(z.astype(BF16), wo_ref[...])
    x1_ref[...] = x1
    hm = _rms(x1, g2_ref[...])
    hm_hi = hm.astype(BF16)
    hm_lo = (hm - hm_hi.astype(F32)).astype(BF16)
    hi_both = _dot(hm_hi, jnp.concatenate([wr_hi_ref[...], wr_lo_ref[...]], axis=1))
    logits = hi_both[:, 0:LANES] + hi_both[:, LANES:] + _dot(hm_lo, wr_hi_ref[...]) + br_ref[...]
    lane = lax.broadcasted_iota(jnp.int32, logits.shape, 1)
    neg = -jnp.inf
    is_g = (lane >= N_EXPERTS) & (lane < N_EXPERTS + N_GROUPS)
    lg = jnp.where(is_g, logits, neg)
    mg = jnp.max(lg, axis=-1, keepdims=True)
    p_top = 1.0 / jnp.sum(jnp.exp(lg - mg), axis=-1, keepdims=True)
    gi = jnp.min(jnp.where(lg == mg, lane, 2 * LANES), axis=-1, keepdims=True) - N_EXPERTS
    group_shift = int(math.log2(EXPERTS_PER_GROUP))
    sel = (lane < N_EXPERTS) & (jnp.right_shift(lane, group_shift) == gi)
    le = jnp.where(sel, logits, neg)
    v1 = jnp.max(le, axis=-1, keepdims=True)
    i1 = jnp.min(jnp.where(le == v1, lane, 2 * LANES), axis=-1, keepdims=True)
    le2 = jnp.where(lane == i1, neg, le)
    v2 = jnp.max(le2, axis=-1, keepdims=True)
    i2 = jnp.min(jnp.where(le2 == v2, lane, 2 * LANES), axis=-1, keepdims=True)
    e2 = jnp.exp(v2 - v1)
    wa = 1.0 / (1.0 + e2)
    wb = e2 / (1.0 + e2)

    @pl.when(pl.program_id(0) == 0)
    def _():
        cnt_scr[...] = cnt0_ref[...]

    tm = logits.shape[0]
    oh1 = lane == i1
    oh2 = lane == i2
    both = jnp.where(oh1 | oh2, 1.0, 0.0)
    cnt = jnp.sum(both, axis=0, keepdims=True)
    cnt = jnp.floor((cnt + (SEG_ALIGN - 1)) * (1.0 / SEG_ALIGN)) * SEG_ALIGN
    lower = _dot(jnp.broadcast_to(cnt, (8, LANES)).astype(BF16), triu_ref[...])[0:1, :]
    lpos = _dot(tril_ref[...], both.astype(BF16)) + lower
    lp1 = jnp.sum(jnp.where(oh1, lpos, 0.0), axis=-1, keepdims=True)
    lp2 = jnp.sum(jnp.where(oh2, lpos, 0.0), axis=-1, keepdims=True)
    pos = lax.broadcasted_iota(jnp.int32, (tm, hs_ref.shape[0]), 1)
    onehot = jnp.where((pos == lp1.astype(jnp.int32)) | (pos == lp2.astype(jnp.int32)), 1.0, 0.0)
    hs_ref[...] = _dot_tn(onehot.astype(BF16), hm_hi).astype(BF16)
    cols = (lp1, lp2, p_top * wa, p_top * wb)
    meta = jnp.zeros(logits.shape, F32)
    for ci, col in enumerate(cols):
        meta = jnp.where(lane == ci, col, meta)
    meta_ref[...] = meta
    row8 = lax.broadcasted_iota(jnp.int32, (8, LANES), 0)
    tab_ref[...] = jnp.where(row8 == 0, cnt, jnp.where(row8 == 1, lower, jnp.where(row8 == 2, cnt_scr[...], 0.0)))
    cnt_scr[...] += cnt


def _merge_call(x2, ga, hb, pmg, wug, wum, wo, g2, wr_hi, wr_lo, br, cnt0, tm):
    n = x2.shape[0]
    tile = lambda w: pl.BlockSpec((tm, w), lambda i: (i, 0))
    tiles = lambda *shape: pl.BlockSpec((None,) + shape, lambda i: (i, 0, 0))
    local_rows = 2 * tm + SEG_ALIGN * N_EXPERTS
    consts = [wug, wum, wo, g2, wr_hi, wr_lo, br,
              jnp.asarray(np.tril(np.ones((tm, tm), np.float32), -1), BF16),
              jnp.asarray(np.triu(np.ones((LANES, LANES), np.float32), 1), BF16),
              cnt0]
    return pl.pallas_call(
        _merge_kernel,
        grid=(n // tm,),
        in_specs=[tile(D_MODEL), tile(GLA_V), tile(ML_W), tile(W_MG)]
                 + [_full_spec(x.shape) for x in consts],
        out_specs=[tile(D_MODEL), tiles(local_rows, D_MODEL), tile(LANES), tiles(8, LANES)],
        out_shape=[jax.ShapeDtypeStruct((n, D_MODEL), F32),
                   jax.ShapeDtypeStruct((n // tm, local_rows, D_MODEL), BF16),
                   jax.ShapeDtypeStruct((n, LANES), F32),
                   jax.ShapeDtypeStruct((n // tm, 8, LANES), F32)],
        scratch_shapes=[pltpu.VMEM((1, LANES), F32)],
        compiler_params=pltpu.CompilerParams(
            dimension_semantics=("arbitrary",), vmem_limit_bytes=VMEM_LIMIT),
        name="merge",
    )(x2, ga, hb, pmg, *consts)


MOE_ROWS = 512
SEG_ALIGN = 16
SEG_SIZES = (512, 256, 128, 64, 32, 16)


def _segment_copies(fn, tables, tile, local_ref, buffer, flat_ref, sem, to_flat):
    cnt_ref, loff_ref, gpos_ref = tables

    def body(e, carry):
        idx = tile * N_EXPERTS + e
        n, lo, gp = cnt_ref[idx], loff_ref[idx], gpos_ref[idx]
        for size in SEG_SIZES:
            @pl.when((n & size) != 0)
            def _():
                off = n & ~(2 * size - 1)
                local = local_ref.at[buffer, pl.ds(pl.multiple_of(lo + off, SEG_ALIGN), size), :]
                flat = flat_ref.at[pl.ds(pl.multiple_of(gp + off, SEG_ALIGN), size), :]
                src, dst = (local, flat) if to_flat else (flat, local)
                fn(pltpu.make_async_copy(src, dst, sem))
        return carry

    lax.fori_loop(0, N_EXPERTS, body, 0)


def _dispatch_kernel(*refs, tile0, first):
    if first:
        cnt_ref, loff_ref, gpos_ref, tail_ref, hs_ref, xs_ref, zero_ref, sem, zsem = refs
    else:
        cnt_ref, loff_ref, gpos_ref, tail_ref, hs_ref, _, xs_ref, zero_ref, sem, zsem = refs
    step = pl.program_id(0)

    @pl.when((step == 0) & first)
    def _():
        zero_ref[...] = jnp.zeros_like(zero_ref)
        n_tiles = xs_ref.shape[0] // MOE_ROWS

        def clear(row):
            start = pl.multiple_of(row, MOE_ROWS)
            return pltpu.make_async_copy(zero_ref, xs_ref.at[pl.ds(start, MOE_ROWS), :], zsem)

        def unused(fn):
            def body(j, carry):
                fn(clear(j * MOE_ROWS))
                return carry
            lax.fori_loop(tail_ref[2 * N_EXPERTS], n_tiles, body, 0)

        for e in range(N_EXPERTS):
            @pl.when(tail_ref[N_EXPERTS + e] > 0)
            def _():
                clear(tail_ref[e]).start()
        unused(lambda c: c.start())
        for e in range(N_EXPERTS):
            @pl.when(tail_ref[N_EXPERTS + e] > 0)
            def _():
                clear(tail_ref[e]).wait()
        unused(lambda c: c.wait())

    tables = (cnt_ref, loff_ref, gpos_ref)
    _segment_copies(lambda c: c.start(), tables, tile0 + step, hs_ref, 0, xs_ref, sem, True)
    _segment_copies(lambda c: c.wait(), tables, tile0 + step, hs_ref, 0, xs_ref, sem, True)


def _dispatch_call(tables, tail, hs, tile0, n_rows, xs=None):
    first = xs is None
    any_spec = pl.BlockSpec(memory_space=pl.ANY)
    in_specs = [pl.BlockSpec((1,) + hs.shape[1:], lambda i, *_: (i, 0, 0))]
    args = [*tables, tail, hs]
    if not first:
        in_specs.append(any_spec)
        args.append(xs)
    return pl.pallas_call(
        functools.partial(_dispatch_kernel, tile0=tile0, first=first),
        grid_spec=pltpu.PrefetchScalarGridSpec(
            num_scalar_prefetch=4,
            grid=(hs.shape[0],),
            in_specs=in_specs,
            out_specs=any_spec,
            scratch_shapes=[pltpu.VMEM((MOE_ROWS, D_MODEL), hs.dtype),
                            pltpu.SemaphoreType.DMA, pltpu.SemaphoreType.DMA],
        ),
        out_shape=jax.ShapeDtypeStruct((n_rows, D_MODEL), hs.dtype),
        input_output_aliases={} if first else {len(args) - 1: 0},
        compiler_params=pltpu.CompilerParams(
            dimension_semantics=("arbitrary",), vmem_limit_bytes=VMEM_LIMIT),
        name="moe_dispatch",
    )(*args)


def _gmm_kernel(te_ref, nv_ref, xs_ref, wg_ref, wu_ref, wd_ref, o_ref, wgu_scr, wd_scr):
    j = pl.program_id(0)
    used = j < nv_ref[0]

    @pl.when(used & ((j == 0) | (te_ref[j] != te_ref[jnp.maximum(j - 1, 0)])))
    def _():
        wgu_scr[:, 0:D_EXPERT] = wg_ref[...].astype(BF16)
        wgu_scr[:, D_EXPERT:] = wu_ref[...].astype(BF16)
        wd_scr[...] = wd_ref[...].astype(BF16)

    @pl.when(used)
    def _():
        au = _dot(xs_ref[...], wgu_scr[...])
        a = au[:, 0:D_EXPERT]
        hh = (a * _sigmoid(a)) * au[:, D_EXPERT:]
        o_ref[...] = _dot(hh.astype(BF16), wd_scr[...]).astype(o_ref.dtype)

    @pl.when(jnp.logical_not(used))
    def _():
        o_ref[...] = jnp.zeros_like(o_ref)


def _gmm_call(tile_expert, n_valid, xs, wg, wu, wd):
    n_tiles = xs.shape[0] // MOE_ROWS
    rows = lambda j, te, nv: (jnp.minimum(j, nv[0] - 1), 0)
    wsel = lambda j, te, nv: (te[j], 0, 0)
    return pl.pallas_call(
        _gmm_kernel,
        grid_spec=pltpu.PrefetchScalarGridSpec(
            num_scalar_prefetch=2,
            grid=(n_tiles,),
            in_specs=[pl.BlockSpec((MOE_ROWS, D_MODEL), rows),
                      pl.BlockSpec((None, D_MODEL, D_EXPERT), wsel),
                      pl.BlockSpec((None, D_MODEL, D_EXPERT), wsel),
                      pl.BlockSpec((None, D_EXPERT, D_MODEL), wsel)],
            out_specs=pl.BlockSpec((MOE_ROWS, D_MODEL), lambda j, te, nv: (j, 0)),
            scratch_shapes=[pltpu.VMEM((D_MODEL, 2 * D_EXPERT), BF16),
                            pltpu.VMEM((D_EXPERT, D_MODEL), BF16)],
        ),
        out_shape=jax.ShapeDtypeStruct(xs.shape, xs.dtype),
        compiler_params=pltpu.CompilerParams(
            dimension_semantics=("arbitrary",), vmem_limit_bytes=VMEM_LIMIT),
        name="moe_grouped",
    )(tile_expert, n_valid, xs, wg, wu, wd)


def _combine_kernel(cnt_ref, loff_ref, gpos_ref, x1_ref, meta_ref, gf_ref, os_ref, y_ref, buf_ref, sem,
                    *, tile0):
    step = pl.program_id(0)
    n_steps = pl.num_programs(0)
    tm = x1_ref.shape[0]
    tables = (cnt_ref, loff_ref, gpos_ref)

    def fetch(fn, i):
        slot = i % 2
        _segment_copies(fn, tables, tile0 + i, buf_ref, slot, os_ref, sem.at[slot], False)

    @pl.when(step == 0)
    def _():
        buf_ref[...] = jnp.zeros_like(buf_ref)
        fetch(lambda c: c.start(), step)

    @pl.when(step + 1 < n_steps)
    def _():
        fetch(lambda c: c.start(), step + 1)

    fetch(lambda c: c.wait(), step)
    rows = buf_ref[step % 2]
    pos = lax.broadcasted_iota(jnp.int32, (tm, rows.shape[0]), 1)
    lp1 = meta_ref[:, 0:1].astype(jnp.int32)
    lp2 = meta_ref[:, 1:2].astype(jnp.int32)
    q = jnp.where(pos == lp1, meta_ref[:, 2:3], 0.0) + jnp.where(pos == lp2, meta_ref[:, 3:4], 0.0)
    y = x1_ref[...] + _dot(q.astype(BF16), rows)
    y_ref[...] = _rms(y, gf_ref[...])


def _combine_call(tables, x1, meta, gf, out_sorted, tile0, local_rows, tm):
    n = x1.shape[0]
    tile = lambda w: pl.BlockSpec((tm, w), lambda i, *_: (i, 0))
    return pl.pallas_call(
        functools.partial(_combine_kernel, tile0=tile0),
        grid_spec=pltpu.PrefetchScalarGridSpec(
            num_scalar_prefetch=3,
            grid=(n // tm,),
            in_specs=[tile(D_MODEL), tile(LANES), pl.BlockSpec(gf.shape, lambda i, *_: (0, 0)),
                      pl.BlockSpec(memory_space=pl.ANY)],
            out_specs=tile(D_MODEL),
            scratch_shapes=[pltpu.VMEM((2, local_rows, D_MODEL), out_sorted.dtype),
                            pltpu.SemaphoreType.DMA((2,))],
        ),
        out_shape=jax.ShapeDtypeStruct((n, D_MODEL), F32),
        compiler_params=pltpu.CompilerParams(
            dimension_semantics=("arbitrary",), vmem_limit_bytes=VMEM_LIMIT),
        name="moe_combine",
    )(*tables, x1, meta, gf, out_sorted)


def _sparse_moe(parts, p):
    token_tiles = [part[1].shape[0] for part in parts]
    n_tiles = -(-sum(part[1].shape[0] * part[1].shape[1] for part in parts) // MOE_ROWS) + N_EXPERTS
    tab = jnp.concatenate([part[3] for part in parts], axis=0)
    tab = tab[:, :, :N_EXPERTS].astype(jnp.int32)
    cnt, loff, before = tab[:, 0], tab[:, 1], tab[:, 2]
    total = before[-1] + cnt[-1]
    tiles = (total + (MOE_ROWS - 1)) // MOE_ROWS
    tile_end = jnp.cumsum(tiles)
    tile_start = tile_end - tiles
    n_valid = tile_end[-1:]
    gpos = tile_start[None, :] * MOE_ROWS + before
    j = jnp.minimum(jnp.arange(n_tiles, dtype=jnp.int32), n_valid - 1)
    tile_expert = jnp.sum((j[:, None] >= tile_end[None, :]).astype(jnp.int32), axis=1)
    tail = jnp.concatenate([(tile_end - 1) * MOE_ROWS, total, n_valid]).astype(jnp.int32)
    tables = (cnt.reshape(-1), loff.reshape(-1), gpos.reshape(-1))
    xs = None
    for part, tile0 in zip(parts, np.cumsum([0] + token_tiles[:-1])):
        xs = _dispatch_call(tables, tail, part[1], int(tile0), n_tiles * MOE_ROWS, xs)
    out_sorted = _gmm_call(tile_expert, n_valid.astype(jnp.int32), xs, p["wg"], p["wu"], p["wd"])
    return [_combine_call(tables, part[0], part[2], p["gf"], out_sorted, int(tile0),
                          part[1].shape[1], part[0].shape[0] // part[1].shape[0])
            for part, tile0 in zip(parts, np.cumsum([0] + token_tiles[:-1]))]


def _pad_cols(w, width):
    return jnp.pad(w, ((0, 0), (0, width - w.shape[1])))


def _prep_weights(norm1_g, w_in, gla_w_gate2, gla_b_gate, gla_norm_g, w_up_gla,
                  ml_conv_w, ml_conv_b, ml_b_i, ml_b_f, w_up_ml, w_out,
                  norm2_g, router_g_w, router_g_b, router_e_w, router_e_b,
                  moe_w_gate, moe_w_up, moe_w_down, final_g):
    wr =_pad_cols(jnp.concatenate([router_e_w, router_g_w], axis=1), LANES)
    wr_hi = wr.astype(BF16)
    wr_lo = (wr - wr_hi.astype(F32)).astype(BF16)
    br = _pad_cols(jnp.concatenate([router_e_b, router_g_b])[None, :], LANES)
    wg2 = jnp.pad(gla_w_gate2, ((0, LANES - GLA_GATE_RANK), (0, 0)))
    wg2_hi = wg2.astype(BF16)
    return dict(
        g1=norm1_g[None, :], w_in=w_in.T,
        wg2_p=jnp.stack([wg2_hi, (wg2 - wg2_hi.astype(F32)).astype(BF16)]),
        bg=gla_b_gate[None, :], gn=gla_norm_g[None, :],
        wug=w_up_gla.astype(BF16),
        cw=ml_conv_w, cb=ml_conv_b[None, :],
        bif=_pad_cols(jnp.concatenate([ml_b_i, ml_b_f])[None, :], LANES),
        wum=w_up_ml.astype(BF16), wo=w_out.astype(BF16),
        g2=norm2_g[None, :], wr_hi=wr_hi, wr_lo=wr_lo, br=br,
        wg=moe_w_gate.reshape(N_EXPERTS, D_MODEL, D_EXPERT),
        wu=moe_w_up.reshape(N_EXPERTS, D_MODEL, D_EXPERT),
        wd=moe_w_down.reshape(N_EXPERTS, D_EXPERT, D_MODEL),
        gf=final_g[None, :],
    )


def _mixers(x, gla_s0, ml_c0, ml_n0, ml_m0, conv0, p, placed, *,
            gla_chunk, ml_chunk, seq_tile, row_tile, merge_tile):
    b, l, _ = x.shape
    n = b * l
    x2 = x.reshape(n, D_MODEL)
    qk, vr, plr, mqk, mvo, pif, pmg = _proj_call(x2, p["g1"], p["w_in"], row_tile)
    r3 = lambda a: a.reshape(b, l, a.shape[-1])
    s0 = None if gla_s0 is None else gla_s0.reshape(b, GLA_QK, GLA_DV)
    ga, gla_s = _gla_call(r3(qk), r3(vr), r3(plr), s0, p["wg2_p"], p["bg"], p["gn"], gla_chunk, seq_tile)
    gla_s = gla_s.reshape(b, GLA_HEADS, GLA_DK, GLA_DV)
    if ml_c0 is None:
        n0 = m0 = None
    else:
        n0 = ml_n0[:, :, None, :]
        m0 = jnp.broadcast_to(_pad_cols(ml_m0, LANES)[:, None, :], (b, 8, LANES))
    hb, ml_c, ml_n, m_b, new_conv = _mlstm_call(r3(mqk), r3(mvo), r3(pif), ml_c0, n0, m0, conv0,
                                                p["cw"], p["cb"], p["bif"], ml_chunk, seq_tile)
    part = _merge_call(x2, ga.reshape(n, GLA_V), hb.reshape(n, ML_W), pmg,
                       p["wug"], p["wum"], p["wo"], p["g2"],
                       p["wr_hi"], p["wr_lo"], p["br"], placed, merge_tile)
    states = (gla_s[None], ml_c[None], ml_n[:, :, 0, :][None], m_b[:, 0, 0:ML_HEADS][None], new_conv[None])
    return part, states


def kernel(x_prompt, x_sample, state_gla_S, state_mlstm_C, state_mlstm_n, state_mlstm_m, state_mlstm_conv, norm1_g, w_in, gla_w_gate2, gla_b_gate, gla_norm_g, w_up_gla, ml_conv_w, ml_conv_b, ml_b_i, ml_b_f, w_up_ml, w_out, norm2_g, router_g_w, router_g_b, router_e_w, router_e_b, moe_w_gate, moe_w_up, moe_w_down, final_g):
    assert norm1_g.shape[0] == 1, "single-layer trunk"
    p = _prep_weights(norm1_g[0], w_in[0], gla_w_gate2[0], gla_b_gate[0], gla_norm_g[0], w_up_gla[0],
                      ml_conv_w[0], ml_conv_b[0], ml_b_i[0], ml_b_f[0], w_up_ml[0], w_out[0],
                      norm2_g[0], router_g_w[0], router_g_b[0], router_e_w[0], router_e_b[0],
                      moe_w_gate[0], moe_w_up[0], moe_w_down[0], final_g)
    dec_seq = x_sample.shape[1]
    n_sample = x_sample.shape[0] * dec_seq
    part_p, sp = _mixers(x_prompt, None, None, None, None, None, p, jnp.zeros((1, LANES), F32),
                         gla_chunk=128, ml_chunk=256, seq_tile=1024, row_tile=256, merge_tile=512)
    placed = part_p[3][-1, 0:1, :] + part_p[3][-1, 2:3, :]
    part_s, ss = _mixers(x_sample, state_gla_S[0], state_mlstm_C[0], state_mlstm_n[0], state_mlstm_m[0],
                         state_mlstm_conv[0], p, placed,
                         gla_chunk=dec_seq, ml_chunk=dec_seq, seq_tile=dec_seq,
                         row_tile=n_sample, merge_tile=n_sample)
    yp, ys = _sparse_moe([part_p, part_s], p)
    return (yp.reshape(x_prompt.shape), ys.reshape(x_sample.shape), *sp, *ss)
```

```python
import functools
import math

import numpy as np
import jax
import jax.numpy as jnp
from jax import lax
from jax.experimental import pallas as pl
from jax.experimental.pallas import tpu as pltpu

D_MODEL = 1024
GLA_HEADS = 4
GLA_DK = 64
GLA_DV = 128
GLA_GATE_RANK = 16
GLA_TAU = 16.0
ML_HEADS = 4
ML_DH = 128
CONV_W = 4
N_GROUPS = 4
EXPERTS_PER_GROUP = 8
N_EXPERTS = N_GROUPS * EXPERTS_PER_GROUP
D_EXPERT = 256
EPS = 1e-6

GLA_QK = GLA_HEADS * GLA_DK
GLA_V = GLA_HEADS * GLA_DV
ML_W = ML_HEADS * ML_DH

LANES = 128
VMEM_LIMIT = 56 * 1024 * 1024

W_GLA = 2 * GLA_QK + 2 * GLA_V
W_ML = 2 * ML_W + ML_W + ML_W
W_MG = 2 * D_MODEL
PROJ_WIDTHS = (W_GLA, LANES, W_ML, LANES, W_MG)
PROJ_SOURCE_WIDTHS = (W_GLA, GLA_GATE_RANK, W_ML, 2 * ML_HEADS, W_MG)

F32 = jnp.float32
BF16 = jnp.bfloat16

PROJ_OUTPUTS = ((2 * GLA_QK, F32), (2 * GLA_V, BF16), (LANES, F32), (2 * ML_W, F32), (2 * ML_W, BF16),
                (LANES, F32), (W_MG, BF16))


def _dot(a, b):
    return jnp.dot(a, b, preferred_element_type=F32)


def _dot_nt(a, b):
    return lax.dot_general(a, b, (((1,), (1,)), ((), ())), preferred_element_type=F32)


def _dot_tn(a, b):
    return lax.dot_general(a, b, (((0,), (0,)), ((), ())), preferred_element_type=F32)


def _split3(x):
    hi = x.astype(BF16)
    r1 = x - hi.astype(F32)
    mid = r1.astype(BF16)
    lo = (r1 - mid.astype(F32)).astype(BF16)
    return hi, mid, lo


def _dot_exact_lhs(m, x):
    hi, mid, lo = _split3(x)
    return _dot(m, hi) + _dot(m, mid) + _dot(m, lo)


def _log_sigmoid(z):
    return jnp.minimum(z, 0.0) - jnp.log(1.0 + jnp.exp(-jnp.abs(z)))


def _sigmoid(z):
    return 1.0 / (1.0 + jnp.exp(-z))


def _rms(x, g):
    return x * lax.rsqrt(jnp.mean(x * x, axis=-1, keepdims=True) + EPS) * g


def _full_spec(shape):
    nd = len(shape)
    return pl.BlockSpec(shape, lambda *_: (0,) * nd)


CONV_PAD = 8


def _causal_conv_silu(stage_ref, x, cw_ref, cb_ref):
    t = x.shape[0]
    stage_ref[CONV_PAD:CONV_PAD + t, :] = x
    acc = cb_ref[...] + stage_ref[CONV_PAD:CONV_PAD + t, :] * cw_ref[CONV_W - 1:CONV_W, :]
    for d in range(1, CONV_W):
        acc = acc + stage_ref[CONV_PAD - d:CONV_PAD - d + t, :] * cw_ref[CONV_W - 1 - d:CONV_W - d, :]
    stage_ref[0:CONV_PAD, :] = stage_ref[t:t + CONV_PAD, :]
    return acc * _sigmoid(acc)


def _proj_kernel(x_ref, g_ref, win_ref, qk_ref, vr_ref, lr_ref, mqk_ref, mvo_ref, if_ref, mg_ref, w_ref):
    starts = np.cumsum((0,) + PROJ_WIDTHS)

    @pl.when(pl.program_id(0) == 0)
    def _():
        src = np.cumsum((0,) + PROJ_SOURCE_WIDTHS)
        chunk = 512
        for g, width in enumerate(PROJ_SOURCE_WIDTHS):
            for r0 in range(0, width, chunk):
                rows = min(chunk, width - r0)
                w_ref[starts[g] + r0:starts[g] + r0 + rows, :] = (
                    win_ref[src[g] + r0:src[g] + r0 + rows, :].astype(BF16))
            if width < PROJ_WIDTHS[g]:
                w_ref[starts[g] + width:starts[g + 1], :] = jnp.zeros((PROJ_WIDTHS[g] - width, D_MODEL), BF16)

    h = _rms(x_ref[...], g_ref[...]).astype(BF16)

    def cols(group, lo, hi):
        return _dot_nt(h, w_ref[starts[group] + lo:starts[group] + hi, :])

    qk_ref[...] = cols(0, 0, 2 * GLA_QK)
    vr_ref[:, 0:GLA_V] = cols(0, 2 * GLA_QK, 2 * GLA_QK + GLA_V).astype(BF16)
    r = cols(0, 2 * GLA_QK + GLA_V, W_GLA)
    vr_ref[:, GLA_V:] = (r * _sigmoid(r)).astype(BF16)
    lr_ref[...] = cols(1, 0, LANES)
    mqk_ref[...] = cols(2, 0, 2 * ML_W)
    mvo_ref[:, 0:ML_W] = cols(2, 2 * ML_W, 3 * ML_W).astype(BF16)
    mvo_ref[:, ML_W:] = _sigmoid(cols(2, 3 * ML_W, W_ML)).astype(BF16)
    if_ref[...] = cols(3, 0, LANES)
    mg_ref[...] = _sigmoid(cols(4, 0, W_MG)).astype(BF16)


def _proj_call(x2, g, w_in, tm):
    n = x2.shape[0]
    assert w_in.shape == (sum(PROJ_SOURCE_WIDTHS), D_MODEL)
    return pl.pallas_call(
        _proj_kernel,
        grid=(n // tm,),
        in_specs=[pl.BlockSpec((tm, D_MODEL), lambda i: (i, 0)),
                  _full_spec(g.shape),
                  pl.BlockSpec(w_in.shape, lambda i: (0, 0), pipeline_mode=pl.Buffered(1))],
        out_specs=[pl.BlockSpec((tm, w), lambda i: (i, 0)) for w, _ in PROJ_OUTPUTS],
        out_shape=[jax.ShapeDtypeStruct((n, w), dt) for w, dt in PROJ_OUTPUTS],
        scratch_shapes=[pltpu.VMEM((sum(PROJ_WIDTHS), D_MODEL), BF16)],
        compiler_params=pltpu.CompilerParams(
            dimension_semantics=("arbitrary",), vmem_limit_bytes=VMEM_LIMIT),
        name="in_proj",
    )(x2, g, w_in)


def _gla_consts(c):
    nlev = int(math.log2(c))
    assert 1 << nlev == c
    t = np.arange(c)[:, None]
    j = np.arange(c)[None, :]
    lv = np.full((c, c), -1, np.int32)
    for l in range(nlev):
        h = c >> (l + 1)
        upper = (t % (2 * h)) >= h
        same = (j // (2 * h)) == (t // (2 * h))
        s_lower = (j % (2 * h)) < h
        lv[np.broadcast_to(upper, (c, c)) & same & s_lower] = l
    lv[np.eye(c, dtype=bool)] = nlev
    tri = (j <= t).astype(np.float32)
    return jnp.asarray(tri, BF16), jnp.asarray(np.concatenate([lv, lv], axis=1))


def _gla_kernel(*refs, c, t, has_state):
    if has_state:
        (qk_ref, vr_ref, plr_ref, s0_ref, wg2_ref, bg_ref, gn_ref, tri_ref, lv_ref,
         o_ref, sout_ref, s_scr) = refs
    else:
        (qk_ref, vr_ref, plr_ref, wg2_ref, bg_ref, gn_ref, tri_ref, lv_ref,
         o_ref, sout_ref, s_scr) = refs
    nlev = int(math.log2(c))
    step = pl.program_id(1)

    @pl.when(step == 0)
    def _():
        if has_state:
            s_scr[...] = s0_ref[...]
        else:
            s_scr[...] = jnp.zeros_like(s_scr)

    n_chunks = t // c
    lane_k = lax.broadcasted_iota(jnp.int32, (t, GLA_QK), 1)
    first_of_pair = (lane_k % (2 * GLA_DK)) < GLA_DK
    row_k = lax.broadcasted_iota(jnp.int32, (GLA_QK, GLA_DV), 0)
    row_t = lax.broadcasted_iota(jnp.int32, (t, GLA_QK), 0)

    def block_ref(b, blk, idx):
        if blk >= 8:
            b3 = b.reshape(t // blk, blk, GLA_QK)
            return jnp.broadcast_to(b3[:, idx:idx + 1, :], b3.shape).reshape(t, GLA_QK)
        r = row_t % blk
        out = b
        for sh in range(-idx, blk - idx):
            if sh != 0:
                out = jnp.where(r - idx == sh, pltpu.roll(b, sh % t, axis=0), out)
        return out

    q = qk_ref[:, 0:GLA_QK] * (GLA_DK ** -0.5)
    k = qk_ref[:, GLA_QK:2 * GLA_QK]
    glr = plr_ref[...]
    g_hi = glr.astype(BF16)
    g_lo = (glr - g_hi.astype(F32)).astype(BF16)
    z = (_dot(g_hi, wg2_ref[0]) + _dot(g_lo, wg2_ref[0]) + _dot(g_hi, wg2_ref[1])) + bg_ref[...]
    la = _log_sigmoid(z) * (1.0 / GLA_TAU)
    b = jnp.concatenate([_dot_exact_lhs(tri_ref[...], la[ci * c:(ci + 1) * c])
                         for ci in range(n_chunks)], axis=0)
    b_last = block_ref(b, c, c - 1)
    qe = (q * jnp.exp(b)).astype(BF16)
    kl = (k * jnp.exp(b_last - b)).astype(BF16)

    k_a = jnp.where(first_of_pair, k, 0.0)
    k_b = k - k_a
    lv2 = lv_ref[...]
    factors = []
    for l in range(nlev + 1):
        if l < nlev:
            half = c >> (l + 1)
            d = b - block_ref(b, 2 * half, half - 1)
            e = jnp.exp(jnp.minimum(d, -d))
            qt, kta, ktb = q * e, k_a * e, k_b * e
        else:
            qt, kta, ktb = q, k_a, k_b
        factors.append((qt.astype(BF16), kta.astype(BF16), ktb.astype(BF16)))
    a = [[None] * (GLA_HEADS // 2) for _ in range(n_chunks)]
    for ci in range(n_chunks):
        rows = slice(ci * c, (ci + 1) * c)
        for pr in range(GLA_HEADS // 2):
            ls = slice(pr * 2 * GLA_DK, (pr + 1) * 2 * GLA_DK)
            acc = jnp.zeros((c, 2 * c), F32)
            for l, (qt, kta, ktb) in enumerate(factors):
                rhs = jnp.concatenate([kta[rows, ls], ktb[rows, ls]], axis=0)
                acc = jnp.where(lv2 == l, _dot_nt(qt[rows, ls], rhs), acc)
            a[ci][pr] = acc

    for ci in range(n_chunks):
        rows = slice(ci * c, (ci + 1) * c)
        v = vr_ref[rows, 0:GLA_V]
        s_all = s_scr[...]
        s_bd = jnp.concatenate(
            [jnp.where((row_k // GLA_DK) == h, s_all, 0.0).astype(BF16) for h in range(GLA_HEADS)], axis=1)
        o_inter = _dot(qe[rows], s_bd)
        u_all = _dot_tn(kl[rows], v)
        dcol = jnp.exp(jnp.broadcast_to(b[ci * c + c - 1:(ci + 1) * c, :], (LANES, GLA_QK)).T)
        for h in range(GLA_HEADS):
            vs = slice(h * GLA_DV, (h + 1) * GLA_DV)
            ks = slice(h * GLA_DK, (h + 1) * GLA_DK)
            a_h = a[ci][h // 2][:, (h % 2) * c:(h % 2 + 1) * c]
            o = _dot(a_h.astype(BF16), v[:, vs]) + o_inter[:, vs]
            on = _rms(o, gn_ref[:, vs])
            gate = vr_ref[rows, GLA_V + h * GLA_DV:GLA_V + (h + 1) * GLA_DV]
            o_ref[rows, vs] = (on * gate.astype(F32)).astype(o_ref.dtype)
            s_scr[ks, :] = dcol[ks, :] * s_all[ks, :] + u_all[ks, vs]

    @pl.when(step == pl.num_programs(1) - 1)
    def _():
        sout_ref[...] = s_scr[...]


def _gla_call(qk, vr, plr, s0, wg2_p, bg, gn, c, t):
    b, l, _ = qk.shape
    tri, lv2 = _gla_consts(c)
    has_state = s0 is not None
    tile = lambda w: pl.BlockSpec((None, t, w), lambda bi, i: (bi, i, 0))
    state_spec = pl.BlockSpec((None, GLA_QK, GLA_DV), lambda bi, i: (bi, 0, 0))
    in_specs = [tile(2 * GLA_QK), tile(2 * GLA_V), tile(LANES)]
    args = [qk, vr, plr]
    if has_state:
        in_specs.append(state_spec)
        args.append(s0)
    consts = [wg2_p, bg, gn, tri, lv2]
    in_specs += [_full_spec(x.shape) for x in consts]
    return pl.pallas_call(
        functools.partial(_gla_kernel, c=c, t=t, has_state=has_state),
        grid=(b, l // t),
        in_specs=in_specs,
        out_specs=[tile(GLA_V), state_spec],
        out_shape=[jax.ShapeDtypeStruct((b, l, GLA_V), BF16),
                   jax.ShapeDtypeStruct((b, GLA_QK, GLA_DV), F32)],
        scratch_shapes=[pltpu.VMEM((GLA_QK, GLA_DV), F32)],
        compiler_params=pltpu.CompilerParams(
            dimension_semantics=("arbitrary", "arbitrary"), vmem_limit_bytes=VMEM_LIMIT),
        name="gla",
    )(*args, *consts)


def _mlstm_kernel(*refs, c, t, has_state):
    if has_state:
        (mqk_ref, mvo_ref, pif_ref, c0_ref, n0_ref, m0_ref, cv0_ref, cw_ref, cb_ref, bif_ref, tri_ref, sel_ref,
         o_ref, cout_ref, nout_ref, mout_ref, cvout_ref, c_scr, m_scr, cv_scr, qk_scr) = refs
    else:
        (mqk_ref, mvo_ref, pif_ref, cw_ref, cb_ref, bif_ref, tri_ref, sel_ref,
         o_ref, cout_ref, nout_ref, mout_ref, cvout_ref, c_scr, m_scr, cv_scr, qk_scr) = refs
    step = pl.program_id(1)
    hist = CONV_W - 1

    @pl.when(step == 0)
    def _():
        if has_state:
            for h in range(ML_HEADS):
                c_scr[h, :, 0:ML_DH] = c0_ref[h]
                c_scr[h, :, ML_DH:] = jnp.broadcast_to(n0_ref[h], (ML_DH, ML_DH)).T
            m_scr[...] = m0_ref[...]
            cv_scr[0:CONV_PAD - hist, :] = jnp.zeros((CONV_PAD - hist, 2 * ML_W), F32)
            cv_scr[CONV_PAD - hist:CONV_PAD, :] = cv0_ref[...]
        else:
            c_scr[...] = jnp.zeros_like(c_scr)
            m_scr[...] = jnp.zeros_like(m_scr)
            cv_scr[0:CONV_PAD, :] = jnp.zeros((CONV_PAD, 2 * ML_W), F32)

    conv = _causal_conv_silu(cv_scr, mqk_ref[...], cw_ref, cb_ref)
    qk_scr[:, 0:ML_W] = conv[:, 0:ML_W].astype(BF16)
    qk_scr[:, ML_W:] = (conv[:, ML_W:] * (ML_DH ** -0.5)).astype(BF16)

    gts = pif_ref[...] + bif_ref[...]
    flog = pltpu.roll(_log_sigmoid(gts), LANES - ML_HEADS, axis=1)

    n_chunks = t // c
    lane = lax.broadcasted_iota(jnp.int32, (t, LANES), 1)
    row_c = lax.broadcasted_iota(jnp.int32, (t, LANES), 0) % c
    causal = (lax.broadcasted_iota(jnp.int32, (c, c), 1) <= lax.broadcasted_iota(jnp.int32, (c, c), 0))
    ones_v = jnp.ones((c, ML_DH), BF16)

    def slots(pieces):
        out = jnp.zeros((t, LANES), F32)
        for j, piece in enumerate(pieces):
            if not isinstance(piece, float) and j > 0:
                piece = pltpu.roll(piece, ML_HEADS * j, axis=1)
            out = jnp.where((lane >= ML_HEADS * j) & (lane < ML_HEADS * (j + 1)), piece, out)
        return out

    def split3f(x):
        return [p.astype(F32) for p in _split3(x)]

    def per_chunk(rows_of):
        return jnp.concatenate([jnp.broadcast_to(rows_of(ci), (c, LANES)) for ci in range(n_chunks)], axis=0)

    bc = jnp.concatenate([_dot_exact_lhs(tri_ref[...], flog[ci * c:(ci + 1) * c])
                          for ci in range(n_chunks)], axis=0)
    w = gts - bc
    cm = w
    for j in range(int(math.log2(c))):
        sh = 1 << j
        cm = jnp.where(row_c >= sh, jnp.maximum(cm, pltpu.roll(cm, sh, axis=0)), cm)
    m_in = [m_scr[0:1, :]]
    for ci in range(n_chunks):
        last = slice(ci * c + c - 1, (ci + 1) * c)
        m_in.append(bc[last, :] + jnp.maximum(m_in[ci], cm[last, :]))
    m_scr[...] = jnp.broadcast_to(m_in[n_chunks], m_scr.shape)
    mprev = per_chunk(lambda ci: m_in[ci])
    g = jnp.maximum(mprev, cm)
    g_last = per_chunk(lambda ci: g[ci * c + c - 1:(ci + 1) * c, :])
    lhs_all = slots([1.0, 1.0, 1.0] + split3f(g))
    rhs_all = slots(split3f(w) + [-1.0, -1.0, -1.0]).astype(BF16)
    y_all = slots(split3f(mprev - g) + split3f(-(bc + g)) + split3f(w - g_last)).astype(BF16)
    lhs_heads = [jnp.where(lane % ML_HEADS == h, lhs_all, 0.0).astype(BF16) for h in range(ML_HEADS)]

    for ci in range(n_chunks):
        r0 = ci * c
        rows = pl.ds(r0, c)
        for h in range(ML_HEADS):
            hs = slice(h * ML_DH, (h + 1) * ML_DH)
            qh = qk_scr[rows, h * ML_DH:(h + 1) * ML_DH]
            kh = qk_scr[rows, ML_W + h * ML_DH:ML_W + (h + 1) * ML_DH]
            vaug = jnp.concatenate(
                [mvo_ref[rows, h * ML_DH:(h + 1) * ML_DH], ones_v],
                axis=1)
            p = jnp.exp(jnp.where(causal, _dot_nt(lhs_heads[h][r0:r0 + c], rhs_all[r0:r0 + c]), -jnp.inf))
            bx = jnp.exp(_dot(y_all[r0:r0 + c], sel_ref[h]))
            w_inter = bx[:, 0:ML_DH]
            e_mt = bx[:, ML_DH:2 * ML_DH]
            w_state = bx[:, 2 * ML_DH:3 * ML_DH]
            s = (_dot_nt(qh, kh) * p).astype(BF16)
            caug = c_scr[h]
            nd = _dot(s, vaug) + jnp.concatenate([w_inter, w_inter], axis=1) * _dot(qh, caug.astype(BF16))
            hh = nd[:, 0:ML_DH] / jnp.maximum(jnp.abs(nd[:, ML_DH:]), e_mt)
            o_gate = mvo_ref[rows, ML_W + h * ML_DH:ML_W + (h + 1) * ML_DH]
            o_ref[rows, hs] = (o_gate.astype(F32) * hh).astype(o_ref.dtype)
            ks = (kh.astype(F32) * w_state).astype(BF16)
            dec = w_inter[c - 1:c, :]
            c_scr[h] = jnp.concatenate([dec, dec], axis=1) * caug + _dot_tn(ks, vaug)

    @pl.when(step == pl.num_programs(1) - 1)
    def _():
        for h in range(ML_HEADS):
            cout_ref[h] = c_scr[h, :, 0:ML_DH]
            nout_ref[h] = c_scr[h, :, ML_DH:].T[0:8, :]
        mout_ref[...] = m_scr[...]
        cvout_ref[...] = mqk_ref[t - hist:t, :]


def _mlstm_call(mqk, mvo, pif, c0, n0, m0, cv0, cw, cb, bif, c, t):
    b, l, _ = mqk.shape
    has_state = c0 is not None
    tri = jnp.asarray(np.tril(np.ones((c, c), np.float32)), BF16)
    sel = np.zeros((ML_HEADS, LANES, 3 * ML_DH), np.float32)
    for h in range(ML_HEADS):
        for slot in range(9):
            sel[h, ML_HEADS * slot + h, (slot // 3) * ML_DH:(slot // 3 + 1) * ML_DH] = 1.0
    sel = jnp.asarray(sel, BF16)
    tile = lambda w: pl.BlockSpec((None, t, w), lambda bi, i: (bi, i, 0))
    heads = lambda *shape: pl.BlockSpec((None, ML_HEADS) + shape, lambda bi, i: (bi, 0, 0, 0))
    c_spec = heads(ML_DH, ML_DH)
    m_spec = pl.BlockSpec((None, 8, LANES), lambda bi, i: (bi, 0, 0))
    cv_spec = pl.BlockSpec((None, CONV_W - 1, 2 * ML_W), lambda bi, i: (bi, 0, 0))
    in_specs = [tile(2 * ML_W), tile(2 * ML_W), tile(LANES)]
    args = [mqk, mvo, pif]
    consts = [cw, cb, bif, tri, sel]
    out_specs = [tile(ML_W), c_spec, heads(8, ML_DH), m_spec, cv_spec]
    out_shape = [jax.ShapeDtypeStruct((b, l, ML_W), BF16),
                 jax.ShapeDtypeStruct((b, ML_HEADS, ML_DH, ML_DH), F32),
                 jax.ShapeDtypeStruct((b, ML_HEADS, 8, ML_DH), F32),
                 jax.ShapeDtypeStruct((b, 8, LANES), F32),
                 jax.ShapeDtypeStruct((b, CONV_W - 1, 2 * ML_W), F32)]
    scratch = [pltpu.VMEM((ML_HEADS, ML_DH, 2 * ML_DH), F32), pltpu.VMEM((8, LANES), F32),
               pltpu.VMEM((CONV_PAD + t, 2 * ML_W), F32), pltpu.VMEM((t, 2 * ML_W), BF16)]
    if has_state:
        in_specs += [c_spec, heads(1, ML_DH), m_spec, cv_spec]
        args += [c0, n0, m0, cv0]
    in_specs += [_full_spec(x.shape) for x in consts]
    return pl.pallas_call(
        functools.partial(_mlstm_kernel, c=c, t=t, has_state=has_state),
        grid=(b, l // t),
        in_specs=in_specs,
        out_specs=out_specs,
        out_shape=out_shape,
        scratch_shapes=scratch,
        compiler_params=pltpu.CompilerParams(
            dimension_semantics=("arbitrary", "arbitrary"), vmem_limit_bytes=VMEM_LIMIT),
        name="mlstm",
    )(*args, *consts)


def _merge_kernel(x_ref, ga_ref, hb_ref, pmg_ref, wug_ref, wum_ref, wo_ref, g2_ref,
                  wr_hi_ref, wr_lo_ref, br_ref, tril_ref, triu_ref, cnt0_ref,
                  x1_ref, hs_ref, meta_ref, tab_ref, cnt_scr):
    ya = _dot(ga_ref[...], wug_ref[...])
    yb = _dot(hb_ref[...], wum_ref[...])
    z = pmg_ref[:, 0:D_MODEL].astype(F32) * ya + pmg_ref[:, D_MODEL:].astype(F32) * yb
    x1 = x_ref[...] + _dot(z.astype(BF16), wo_ref[...])
    x1_ref[...] = x1
    hm = _rms(x1, g2_ref[...])
    hm_hi = hm.astype(BF16)
    hm_lo = (hm - hm_hi.astype(F32)).astype(BF16)
    hi_both = _dot(hm_hi, jnp.concatenate([wr_hi_ref[...], wr_lo_ref[...]], axis=1))
    logits = hi_both[:, 0:LANES] + hi_both[:, LANES:] + _dot(hm_lo, wr_hi_ref[...]) + br_ref[...]
    lane = lax.broadcasted_iota(jnp.int32, logits.shape, 1)
    neg = -jnp.inf
    is_g = (lane >= N_EXPERTS) & (lane < N_EXPERTS + N_GROUPS)
    lg = jnp.where(is_g, logits, neg)
    mg = jnp.max(lg, axis=-1, keepdims=True)
    p_top = 1.0 / jnp.sum(jnp.exp(lg - mg), axis=-1, keepdims=True)
    gi = jnp.min(jnp.where(lg == mg, lane, 2 * LANES), axis=-1, keepdims=True) - N_EXPERTS
    group_shift = int(math.log2(EXPERTS_PER_GROUP))
    sel = (lane < N_EXPERTS) & (jnp.right_shift(lane, group_shift) == gi)
    le = jnp.where(sel, logits, neg)
    v1 = jnp.max(le, axis=-1, keepdims=True)
    i1 = jnp.min(jnp.where(le == v1, lane, 2 * LANES), axis=-1, keepdims=True)
    le2 = jnp.where(lane == i1, neg, le)
    v2 = jnp.max(le2, axis=-1, keepdims=True)
    i2 = jnp.min(jnp.where(le2 == v2, lane, 2 * LANES), axis=-1, keepdims=True)
    e2 = jnp.exp(v2 - v1)
    w1 = p_top * (1.0 / (1.0 + e2))
    w2 = p_top * (e2 / (1.0 + e2))

    @pl.when(pl.program_id(0) == 0)
    def _():
        cnt_scr[...] = cnt0_ref[...]

    tm = logits.shape[0]
    oh1 = lane == i1
    oh2 = lane == i2
    both = jnp.where(oh1 | oh2, 1.0, 0.0)
    cnt = jnp.sum(both, axis=0, keepdims=True)
    cnt = jnp.floor((cnt + (SEG_ALIGN - 1)) * (1.0 / SEG_ALIGN)) * SEG_ALIGN
    lower = _dot(jnp.broadcast_to(cnt, (8, LANES)).astype(BF16), triu_ref[...])[0:1, :]
    lpos = _dot(tril_ref[...], both.astype(BF16)) + lower
    lp1 = jnp.sum(jnp.where(oh1, lpos, 0.0), axis=-1, keepdims=True)
    lp2 = jnp.sum(jnp.where(oh2, lpos, 0.0), axis=-1, keepdims=True)
    pos = lax.broadcasted_iota(jnp.int32, (tm, hs_ref.shape[0]), 1)
    onehot = jnp.where((pos == lp1.astype(jnp.int32)) | (pos == lp2.astype(jnp.int32)), 1.0, 0.0)
    hs_ref[...] = _dot_tn(onehot.astype(BF16), hm_hi).astype(BF16)
    cols = (lp1, lp2, w1, w2)
    meta = jnp.zeros((tm, LANES), F32)
    for ci, col in enumerate(cols):
        meta = jnp.where(lane == ci, col, meta)
    meta_ref[...] = meta
    row8 = lax.broadcasted_iota(jnp.int32, (8, LANES), 0)
    tab_ref[...] = jnp.where(row8 == 0, cnt, jnp.where(row8 == 1, lower, jnp.where(row8 == 2, cnt_scr[...], 0.0)))
    cnt_scr[...] += cnt


def _merge_call(x2, ga, hb, pmg, wug, wum, wo, g2, wr_hi, wr_lo, br, cnt0, tm):
    n = x2.shape[0]
    tile = lambda w: pl.BlockSpec((tm, w), lambda i: (i, 0))
    tiles = lambda *shape: pl.BlockSpec((None,) + shape, lambda i: (i, 0, 0))
    local_rows = 2 * tm + SEG_ALIGN * N_EXPERTS
    consts = [wug, wum, wo, g2, wr_hi, wr_lo, br,
              jnp.asarray(np.tril(np.ones((tm, tm), np.float32), -1), BF16),
              jnp.asarray(np.triu(np.ones((LANES, LANES), np.float32), 1), BF16),
              cnt0]
    return pl.pallas_call(
        _merge_kernel,
        grid=(n // tm,),
        in_specs=[tile(D_MODEL), tile(GLA_V), tile(ML_W), tile(W_MG)]
                 + [_full_spec(x.shape) for x in consts],
        out_specs=[tile(D_MODEL), tiles(local_rows, D_MODEL), tile(LANES), tiles(8, LANES)],
        out_shape=[jax.ShapeDtypeStruct((n, D_MODEL), F32),
                   jax.ShapeDtypeStruct((n // tm, local_rows, D_MODEL), BF16),
                   jax.ShapeDtypeStruct((n, LANES), F32),
                   jax.ShapeDtypeStruct((n // tm, 8, LANES), F32)],
        scratch_shapes=[pltpu.VMEM((1, LANES), F32)],
        compiler_params=pltpu.CompilerParams(
            dimension_semantics=("arbitrary",), vmem_limit_bytes=VMEM_LIMIT),
        name="merge",
    )(x2, ga, hb, pmg, *consts)


MOE_ROWS = 512
SEG_ALIGN = 16


def _segment_copies(fn, tables, tile, local_ref, buffer, flat_ref, sem, to_flat):
    pieces_ref, gdst_ref = tables
    stride = gdst_ref.shape[0] // pieces_ref.shape[0]

    def body(p, carry):
        local = local_ref.at[buffer, pl.ds(pl.multiple_of(p * SEG_ALIGN, SEG_ALIGN), SEG_ALIGN), :]
        flat = flat_ref.at[pl.ds(pl.multiple_of(gdst_ref[tile * stride + p], SEG_ALIGN), SEG_ALIGN), :]
        src, dst = (local, flat) if to_flat else (flat, local)
        fn(pltpu.make_async_copy(src, dst, sem))
        return carry

    lax.fori_loop(0, pieces_ref[tile], body, 0)


def _dispatch_kernel(*refs, tile0, first):
    if first:
        pieces_ref, gdst_ref, tail_ref, hs_ref, xs_ref, zero_ref, sem, zsem = refs
    else:
        pieces_ref, gdst_ref, tail_ref, hs_ref, _, xs_ref, zero_ref, sem, zsem = refs
    step = pl.program_id(0)

    @pl.when((step == 0) & first)
    def _():
        zero_ref[...] = jnp.zeros_like(zero_ref)
        n_tiles = xs_ref.shape[0] // MOE_ROWS

        def clear(row):
            start = pl.multiple_of(row, MOE_ROWS)
            return pltpu.make_async_copy(zero_ref, xs_ref.at[pl.ds(start, MOE_ROWS), :], zsem)

        def unused(fn):
            def body(j, carry):
                fn(clear(j * MOE_ROWS))
                return carry
            lax.fori_loop(tail_ref[2 * N_EXPERTS], n_tiles, body, 0)

        for e in range(N_EXPERTS):
            @pl.when(tail_ref[N_EXPERTS + e] > 0)
            def _():
                clear(tail_ref[e]).start()
        unused(lambda c: c.start())
        for e in range(N_EXPERTS):
            @pl.when(tail_ref[N_EXPERTS + e] > 0)
            def _():
                clear(tail_ref[e]).wait()
        unused(lambda c: c.wait())

    tables = (pieces_ref, gdst_ref)
    _segment_copies(lambda c: c.start(), tables, tile0 + step, hs_ref, 0, xs_ref, sem, True)
    _segment_copies(lambda c: c.wait(), tables, tile0 + step, hs_ref, 0, xs_ref, sem, True)


def _dispatch_call(tables, tail, hs, tile0, n_rows, xs=None):
    first = xs is None
    any_spec = pl.BlockSpec(memory_space=pl.ANY)
    in_specs = [pl.BlockSpec((1,) + hs.shape[1:], lambda i, *_: (i, 0, 0))]
    args = [*tables, tail, hs]
    if not first:
        in_specs.append(any_spec)
        args.append(xs)
    return pl.pallas_call(
        functools.partial(_dispatch_kernel, tile0=tile0, first=first),
        grid_spec=pltpu.PrefetchScalarGridSpec(
            num_scalar_prefetch=3,
            grid=(hs.shape[0],),
            in_specs=in_specs,
            out_specs=any_spec,
            scratch_shapes=[pltpu.VMEM((MOE_ROWS, D_MODEL), hs.dtype),
                            pltpu.SemaphoreType.DMA, pltpu.SemaphoreType.DMA],
        ),
        out_shape=jax.ShapeDtypeStruct((n_rows, D_MODEL), hs.dtype),
        input_output_aliases={} if first else {len(args) - 1: 0},
        compiler_params=pltpu.CompilerParams(
            dimension_semantics=("arbitrary",), vmem_limit_bytes=VMEM_LIMIT),
        name="moe_dispatch",
    )(*args)


def _gmm_kernel(te_ref, nv_ref, xs_ref, wg_ref, wu_ref, wd_ref, o_ref, wgu_scr, wd_scr):
    j = pl.program_id(0)
    used = j < nv_ref[0]

    @pl.when(used & ((j == 0) | (te_ref[j] != te_ref[jnp.maximum(j - 1, 0)])))
    def _():
        wgu_scr[:, 0:D_EXPERT] = wg_ref[...].astype(BF16)
        wgu_scr[:, D_EXPERT:] = wu_ref[...].astype(BF16)
        wd_scr[...] = wd_ref[...].astype(BF16)

    @pl.when(used)
    def _():
        au = _dot(xs_ref[...], wgu_scr[...])
        a = au[:, 0:D_EXPERT]
        hh = (a * _sigmoid(a)) * au[:, D_EXPERT:]
        o_ref[...] = _dot(hh.astype(BF16), wd_scr[...]).astype(o_ref.dtype)

    @pl.when(jnp.logical_not(used))
    def _():
        o_ref[...] = jnp.zeros_like(o_ref)


def _gmm_call(tile_expert, n_valid, xs, wg, wu, wd):
    n_tiles = xs.shape[0] // MOE_ROWS
    rows = lambda j, te, nv: (jnp.minimum(j, nv[0] - 1), 0)
    wsel = lambda j, te, nv: (te[j], 0, 0)
    return pl.pallas_call(
        _gmm_kernel,
        grid_spec=pltpu.PrefetchScalarGridSpec(
            num_scalar_prefetch=2,
            grid=(n_tiles,),
            in_specs=[pl.BlockSpec((MOE_ROWS, D_MODEL), rows),
                      pl.BlockSpec((None, D_MODEL, D_EXPERT), wsel),
                      pl.BlockSpec((None, D_MODEL, D_EXPERT), wsel),
                      pl.BlockSpec((None, D_EXPERT, D_MODEL), wsel)],
            out_specs=pl.BlockSpec((MOE_ROWS, D_MODEL), lambda j, te, nv: (j, 0)),
            scratch_shapes=[pltpu.VMEM((D_MODEL, 2 * D_EXPERT), BF16),
                            pltpu.VMEM((D_EXPERT, D_MODEL), BF16)],
        ),
        out_shape=jax.ShapeDtypeStruct(xs.shape, xs.dtype),
        compiler_params=pltpu.CompilerParams(
            dimension_semantics=("arbitrary",), vmem_limit_bytes=VMEM_LIMIT),
        name="moe_grouped",
    )(tile_expert, n_valid, xs, wg, wu, wd)


def _combine_kernel(pieces_ref, gdst_ref, x1_ref, meta_ref, gf_ref, os_ref, y_ref, buf_ref, sem, *, tile0):
    step = pl.program_id(0)
    n_steps = pl.num_programs(0)
    tm = x1_ref.shape[0]
    tables = (pieces_ref, gdst_ref)

    def fetch(fn, i):
        slot = i % 2
        _segment_copies(fn, tables, tile0 + i, buf_ref, slot, os_ref, sem.at[slot], False)

    @pl.when(step == 0)
    def _():
        buf_ref[...] = jnp.zeros_like(buf_ref)
        fetch(lambda c: c.start(), step)

    @pl.when(step + 1 < n_steps)
    def _():
        fetch(lambda c: c.start(), step + 1)

    fetch(lambda c: c.wait(), step)
    rows = buf_ref[step % 2]
    pos = lax.broadcasted_iota(jnp.int32, (tm, rows.shape[0]), 1)
    lp1 = meta_ref[:, 0:1].astype(jnp.int32)
    lp2 = meta_ref[:, 1:2].astype(jnp.int32)
    q = jnp.where(pos == lp1, meta_ref[:, 2:3], 0.0) + jnp.where(pos == lp2, meta_ref[:, 3:4], 0.0)
    y = x1_ref[...] + _dot(q.astype(BF16), rows)
    y_ref[...] = _rms(y, gf_ref[...])


def _combine_call(tables, x1, meta, gf, out_sorted, tile0, local_rows, tm):
    n = x1.shape[0]
    tile = lambda w: pl.BlockSpec((tm, w), lambda i, *_: (i, 0))
    return pl.pallas_call(
        functools.partial(_combine_kernel, tile0=tile0),
        grid_spec=pltpu.PrefetchScalarGridSpec(
            num_scalar_prefetch=2,
            grid=(n // tm,),
            in_specs=[tile(D_MODEL), tile(LANES), pl.BlockSpec(gf.shape, lambda i, *_: (0, 0)),
                      pl.BlockSpec(memory_space=pl.ANY)],
            out_specs=tile(D_MODEL),
            scratch_shapes=[pltpu.VMEM((2, local_rows, D_MODEL), out_sorted.dtype),
                            pltpu.SemaphoreType.DMA((2,))],
        ),
        out_shape=jax.ShapeDtypeStruct((n, D_MODEL), F32),
        compiler_params=pltpu.CompilerParams(
            dimension_semantics=("arbitrary",), vmem_limit_bytes=VMEM_LIMIT),
        name="moe_combine",
    )(*tables, x1, meta, gf, out_sorted)


def _sparse_moe(parts, p):
    token_tiles = [part[1].shape[0] for part in parts]
    n_tiles = -(-sum(part[1].shape[0] * part[1].shape[1] for part in parts) // MOE_ROWS) + N_EXPERTS
    tab = jnp.concatenate([part[3] for part in parts], axis=0)
    tab = tab[:, :, :N_EXPERTS].astype(jnp.int32)
    cnt, before = tab[:, 0], tab[:, 2]
    total = before[-1] + cnt[-1]
    tiles = (total + (MOE_ROWS - 1)) // MOE_ROWS
    tile_end = jnp.cumsum(tiles)
    tile_start = tile_end - tiles
    n_valid = tile_end[-1:]
    gpos = tile_start[None, :] * MOE_ROWS + before
    j = jnp.minimum(jnp.arange(n_tiles, dtype=jnp.int32), n_valid - 1)
    tile_expert = jnp.sum((j[:, None] >= tile_end[None, :]).astype(jnp.int32), axis=1)
    tail = jnp.concatenate([(tile_end - 1) * MOE_ROWS, total, n_valid]).astype(jnp.int32)
    max_pieces = max(part[1].shape[1] for part in parts) // SEG_ALIGN
    pieces = cnt // SEG_ALIGN
    piece_end = jnp.cumsum(pieces, axis=1)
    piece = jnp.arange(max_pieces, dtype=jnp.int32)
    expert_of = jnp.sum((piece[None, :, None] >= piece_end[:, None, :]).astype(jnp.int32), axis=2)
    onehot = expert_of[:, :, None] == jnp.arange(N_EXPERTS, dtype=jnp.int32)
    first_row = gpos - SEG_ALIGN * (piece_end - pieces)
    gdst = jnp.sum(jnp.where(onehot, first_row[:, None, :], 0), axis=2) + SEG_ALIGN * piece[None, :]
    tables = (piece_end[:, -1], gdst.reshape(-1))
    xs = None
    for part, tile0 in zip(parts, np.cumsum([0] + token_tiles[:-1])):
        xs = _dispatch_call(tables, tail, part[1], int(tile0), n_tiles * MOE_ROWS, xs)
    out_sorted = _gmm_call(tile_expert, n_valid.astype(jnp.int32), xs, p["wg"], p["wu"], p["wd"])
    return [_combine_call(tables, part[0], part[2], p["gf"], out_sorted, int(tile0),
                          part[1].shape[1], part[0].shape[0] // part[1].shape[0])
            for part, tile0 in zip(parts, np.cumsum([0] + token_tiles[:-1]))]


def _pad_cols(w, width):
    return jnp.pad(w, ((0, 0), (0, width - w.shape[1])))


def _prep_weights(norm1_g, w_in, gla_w_gate2, gla_b_gate, gla_norm_g, w_up_gla,
                  ml_conv_w, ml_conv_b, ml_b_i, ml_b_f, w_up_ml, w_out,
                  norm2_g, router_g_w, router_g_b, router_e_w, router_e_b,
                  moe_w_gate, moe_w_up, moe_w_down, final_g):
    wr =_pad_cols(jnp.concatenate([router_e_w, router_g_w], axis=1), LANES)
    wr_hi = wr.astype(BF16)
    wr_lo = (wr - wr_hi.astype(F32)).astype(BF16)
    br = _pad_cols(jnp.concatenate([router_e_b, router_g_b])[None, :], LANES)
    wg2 = jnp.pad(gla_w_gate2, ((0, LANES - GLA_GATE_RANK), (0, 0)))
    wg2_hi = wg2.astype(BF16)
    return dict(
        g1=norm1_g[None, :], w_in=w_in.T,
        wg2_p=jnp.stack([wg2_hi, (wg2 - wg2_hi.astype(F32)).astype(BF16)]),
        bg=gla_b_gate[None, :], gn=gla_norm_g[None, :],
        wug=w_up_gla.astype(BF16),
        cw=ml_conv_w, cb=ml_conv_b[None, :],
        bif=_pad_cols(jnp.concatenate([ml_b_i, ml_b_f])[None, :], LANES),
        wum=w_up_ml.astype(BF16), wo=w_out.astype(BF16),
        g2=norm2_g[None, :], wr_hi=wr_hi, wr_lo=wr_lo, br=br,
        wg=moe_w_gate.reshape(N_EXPERTS, D_MODEL, D_EXPERT),
        wu=moe_w_up.reshape(N_EXPERTS, D_MODEL, D_EXPERT),
        wd=moe_w_down.reshape(N_EXPERTS, D_EXPERT, D_MODEL),
        gf=final_g[None, :],
    )


def _mixers(x, gla_s0, ml_c0, ml_n0, ml_m0, conv0, p, placed, *,
            gla_chunk, ml_chunk, seq_tile, row_tile, merge_tile):
    b, l, _ = x.shape
    n = b * l
    x2 = x.reshape(n, D_MODEL)
    qk, vr, plr, mqk, mvo, pif, pmg = _proj_call(x2, p["g1"], p["w_in"], row_tile)
    r3 = lambda a: a.reshape(b, l, a.shape[-1])
    s0 = None if gla_s0 is None else gla_s0.reshape(b, GLA_QK, GLA_DV)
    ga, gla_s = _gla_call(r3(qk), r3(vr), r3(plr), s0, p["wg2_p"], p["bg"], p["gn"], gla_chunk, seq_tile)
    gla_s = gla_s.reshape(b, GLA_HEADS, GLA_DK, GLA_DV)
    if ml_c0 is None:
        n0 = m0 = None
    else:
        n0 = ml_n0[:, :, None, :]
        m0 = jnp.broadcast_to(_pad_cols(ml_m0, LANES)[:, None, :], (b, 8, LANES))
    hb, ml_c, ml_n, m_b, new_conv = _mlstm_call(r3(mqk), r3(mvo), r3(pif), ml_c0, n0, m0, conv0,
                                                p["cw"], p["cb"], p["bif"], ml_chunk, seq_tile)
    part = _merge_call(x2, ga.reshape(n, GLA_V), hb.reshape(n, ML_W), pmg,
                       p["wug"], p["wum"], p["wo"], p["g2"],
                       p["wr_hi"], p["wr_lo"], p["br"], placed, merge_tile)
    states = (gla_s[None], ml_c[None], ml_n[:, :, 0, :][None], m_b[:, 0, 0:ML_HEADS][None], new_conv[None])
    return part, states


def kernel(x_prompt, x_sample, state_gla_S, state_mlstm_C, state_mlstm_n, state_mlstm_m, state_mlstm_conv, norm1_g, w_in, gla_w_gate2, gla_b_gate, gla_norm_g, w_up_gla, ml_conv_w, ml_conv_b, ml_b_i, ml_b_f, w_up_ml, w_out, norm2_g, router_g_w, router_g_b, router_e_w, router_e_b, moe_w_gate, moe_w_up, moe_w_down, final_g):
    assert norm1_g.shape[0] == 1, "single-layer trunk"
    p = _prep_weights(norm1_g[0], w_in[0], gla_w_gate2[0], gla_b_gate[0], gla_norm_g[0], w_up_gla[0],
                      ml_conv_w[0], ml_conv_b[0], ml_b_i[0], ml_b_f[0], w_up_ml[0], w_out[0],
                      norm2_g[0], router_g_w[0], router_g_b[0], router_e_w[0], router_e_b[0],
                      moe_w_gate[0], moe_w_up[0], moe_w_down[0], final_g)
    dec_seq = x_sample.shape[1]
    n_sample = x_sample.shape[0] * dec_seq
    part_p, sp = _mixers(x_prompt, None, None, None, None, None, p, jnp.zeros((1, LANES), F32),
                         gla_chunk=128, ml_chunk=256, seq_tile=1024, row_tile=256, merge_tile=512)
    placed = part_p[3][-1, 0:1, :] + part_p[3][-1, 2:3, :]
    part_s, ss = _mixers(x_sample, state_gla_S[0], state_mlstm_C[0], state_mlstm_n[0], state_mlstm_m[0],
                         state_mlstm_conv[0], p, placed,
                         gla_chunk=dec_seq, ml_chunk=dec_seq, seq_tile=dec_seq,
                         row_tile=n_sample, merge_tile=n_sample)
    yp, ys = _sparse_moe([part_p, part_s], p)
    return (yp.reshape(x_prompt.shape), ys.reshape(x_sample.shape), *sp, *ss)
```

```python
import functools
import math

import numpy as np
import jax
import jax.numpy as jnp
from jax import lax
from jax.experimental import pallas as pl
from jax.experimental.pallas import tpu as pltpu

D_MODEL = 1024
GLA_HEADS = 4
GLA_DK = 64
GLA_DV = 128
GLA_GATE_RANK = 16
GLA_TAU = 16.0
ML_HEADS = 4
ML_DH = 128
CONV_W = 4
N_GROUPS = 4
EXPERTS_PER_GROUP = 8
N_EXPERTS = N_GROUPS * EXPERTS_PER_GROUP
D_EXPERT = 256
EPS = 1e-6

GLA_QK = GLA_HEADS * GLA_DK
GLA_V = GLA_HEADS * GLA_DV
ML_W = ML_HEADS * ML_DH

LANES = 128
VMEM_LIMIT = 56 * 1024 * 1024

W_GLA = 2 * GLA_QK + 2 * GLA_V
W_ML = 2 * ML_W + ML_W + ML_W
W_MG = 2 * D_MODEL
PROJ_WIDTHS = (W_GLA, LANES, W_ML, LANES, W_MG)
PROJ_SOURCE_WIDTHS = (W_GLA, GLA_GATE_RANK, W_ML, 2 * ML_HEADS, W_MG)

F32 = jnp.float32
BF16 = jnp.bfloat16

PROJ_OUTPUTS = ((2 * GLA_QK, F32), (2 * GLA_V, BF16), (LANES, F32), (2 * ML_W, F32), (2 * ML_W, BF16),
                (LANES, F32), (W_MG, BF16))


def _dot(a, b):
    return jnp.dot(a, b, preferred_element_type=F32)


def _dot_nt(a, b):
    return lax.dot_general(a, b, (((1,), (1,)), ((), ())), preferred_element_type=F32)


def _dot_tn(a, b):
    return lax.dot_general(a, b, (((0,), (0,)), ((), ())), preferred_element_type=F32)


def _split3(x):
    hi = x.astype(BF16)
    r1 = x - hi.astype(F32)
    mid = r1.astype(BF16)
    lo = (r1 - mid.astype(F32)).astype(BF16)
    return hi, mid, lo


def _dot_exact_lhs(m, x):
    hi, mid, lo = _split3(x)
    return _dot(m, hi) + _dot(m, mid) + _dot(m, lo)


def _log_sigmoid(z):
    return jnp.minimum(z, 0.0) - jnp.log(1.0 + jnp.exp(-jnp.abs(z)))


def _sigmoid(z):
    return 1.0 / (1.0 + jnp.exp(-z))


def _rms(x, g):
    return x * lax.rsqrt(jnp.mean(x * x, axis=-1, keepdims=True) + EPS) * g


def _full_spec(shape):
    nd = len(shape)
    return pl.BlockSpec(shape, lambda *_: (0,) * nd)


CONV_PAD = 8


def _causal_conv_silu(stage_ref, x, cw_ref, cb_ref):
    t = x.shape[0]
    stage_ref[CONV_PAD:CONV_PAD + t, :] = x
    acc = cb_ref[...] + stage_ref[CONV_PAD:CONV_PAD + t, :] * cw_ref[CONV_W - 1:CONV_W, :]
    for d in range(1, CONV_W):
        acc = acc + stage_ref[CONV_PAD - d:CONV_PAD - d + t, :] * cw_ref[CONV_W - 1 - d:CONV_W - d, :]
    stage_ref[0:CONV_PAD, :] = stage_ref[t:t + CONV_PAD, :]
    return acc * _sigmoid(acc)


def _proj_kernel(x_ref, g_ref, win_ref, qk_ref, vr_ref, lr_ref, mqk_ref, mvo_ref, if_ref, mg_ref, w_ref):
    starts = np.cumsum((0,) + PROJ_WIDTHS)

    @pl.when(pl.program_id(0) == 0)
    def _():
        src = np.cumsum((0,) + PROJ_SOURCE_WIDTHS)
        chunk = 512
        for g, width in enumerate(PROJ_SOURCE_WIDTHS):
            for r0 in range(0, width, chunk):
                rows = min(chunk, width - r0)
                w_ref[starts[g] + r0:starts[g] + r0 + rows, :] = (
                    win_ref[src[g] + r0:src[g] + r0 + rows, :].astype(BF16))
            if width < PROJ_WIDTHS[g]:
                w_ref[starts[g] + width:starts[g + 1], :] = jnp.zeros((PROJ_WIDTHS[g] - width, D_MODEL), BF16)

    h = _rms(x_ref[...], g_ref[...]).astype(BF16)

    def cols(group, lo, hi):
        return _dot_nt(h, w_ref[starts[group] + lo:starts[group] + hi, :])

    qk_ref[...] = cols(0, 0, 2 * GLA_QK)
    vr_ref[:, 0:GLA_V] = cols(0, 2 * GLA_QK, 2 * GLA_QK + GLA_V).astype(BF16)
    r = cols(0, 2 * GLA_QK + GLA_V, W_GLA)
    vr_ref[:, GLA_V:] = (r * _sigmoid(r)).astype(BF16)
    lr_ref[...] = cols(1, 0, LANES)
    mqk_ref[...] = cols(2, 0, 2 * ML_W)
    mvo_ref[:, 0:ML_W] = cols(2, 2 * ML_W, 3 * ML_W).astype(BF16)
    mvo_ref[:, ML_W:] = _sigmoid(cols(2, 3 * ML_W, W_ML)).astype(BF16)
    if_ref[...] = cols(3, 0, LANES)
    mg_ref[...] = _sigmoid(cols(4, 0, W_MG)).astype(BF16)


def _proj_call(x2, g, w_in, tm):
    n = x2.shape[0]
    assert w_in.shape == (sum(PROJ_SOURCE_WIDTHS), D_MODEL)
    return pl.pallas_call(
        _proj_kernel,
        grid=(n // tm,),
        in_specs=[pl.BlockSpec((tm, D_MODEL), lambda i: (i, 0)),
                  _full_spec(g.shape),
                  pl.BlockSpec(w_in.shape, lambda i: (0, 0), pipeline_mode=pl.Buffered(1))],
        out_specs=[pl.BlockSpec((tm, w), lambda i: (i, 0)) for w, _ in PROJ_OUTPUTS],
        out_shape=[jax.ShapeDtypeStruct((n, w), dt) for w, dt in PROJ_OUTPUTS],
        scratch_shapes=[pltpu.VMEM((sum(PROJ_WIDTHS), D_MODEL), BF16)],
        compiler_params=pltpu.CompilerParams(
            dimension_semantics=("arbitrary",), vmem_limit_bytes=VMEM_LIMIT),
        name="in_proj",
    )(x2, g, w_in)


def _gla_consts(c):
    nlev = int(math.log2(c))
    assert 1 << nlev == c
    t = np.arange(c)[:, None]
    j = np.arange(c)[None, :]
    lv = np.full((c, c), -1, np.int32)
    for l in range(nlev):
        h = c >> (l + 1)
        upper = (t % (2 * h)) >= h
        same = (j // (2 * h)) == (t // (2 * h))
        s_lower = (j % (2 * h)) < h
        lv[np.broadcast_to(upper, (c, c)) & same & s_lower] = l
    lv[np.eye(c, dtype=bool)] = nlev
    tri = (j <= t).astype(np.float32)
    return jnp.asarray(tri, BF16), jnp.asarray(np.concatenate([lv, lv], axis=1))


def _gla_kernel(*refs, c, t, has_state):
    if has_state:
        (qk_ref, vr_ref, plr_ref, s0_ref, wg2_ref, bg_ref, gn_ref, tri_ref, lv_ref,
         o_ref, sout_ref, s_scr) = refs
    else:
        (qk_ref, vr_ref, plr_ref, wg2_ref, bg_ref, gn_ref, tri_ref, lv_ref,
         o_ref, sout_ref, s_scr) = refs
    nlev = int(math.log2(c))
    step = pl.program_id(1)

    @pl.when(step == 0)
    def _():
        if has_state:
            s_scr[...] = s0_ref[...]
        else:
            s_scr[...] = jnp.zeros_like(s_scr)

    n_chunks = t // c
    lane_k = lax.broadcasted_iota(jnp.int32, (t, GLA_QK), 1)
    first_of_pair = (lane_k % (2 * GLA_DK)) < GLA_DK
    row_k = lax.broadcasted_iota(jnp.int32, (GLA_QK, GLA_DV), 0)
    row_t = lax.broadcasted_iota(jnp.int32, (t, GLA_QK), 0)

    def block_ref(b, blk, idx):
        if blk >= 8:
            b3 = b.reshape(t // blk, blk, GLA_QK)
            return jnp.broadcast_to(b3[:, idx:idx + 1, :], b3.shape).reshape(t, GLA_QK)
        r = row_t % blk
        out = b
        for sh in range(-idx, blk - idx):
            if sh != 0:
                out = jnp.where(r - idx == sh, pltpu.roll(b, sh % t, axis=0), out)
        return out

    q = qk_ref[:, 0:GLA_QK] * (GLA_DK ** -0.5)
    k = qk_ref[:, GLA_QK:2 * GLA_QK]
    glr = plr_ref[...]
    g_hi = glr.astype(BF16)
    g_lo = (glr - g_hi.astype(F32)).astype(BF16)
    z = (_dot(g_hi, wg2_ref[0]) + _dot(g_lo, wg2_ref[0]) + _dot(g_hi, wg2_ref[1])) + bg_ref[...]
    la = _log_sigmoid(z) * (1.0 / GLA_TAU)
    b = jnp.concatenate([_dot_exact_lhs(tri_ref[...], la[ci * c:(ci + 1) * c])
                         for ci in range(n_chunks)], axis=0)
    b_last = block_ref(b, c, c - 1)
    qe = (q * jnp.exp(b)).astype(BF16)
    kl = (k * jnp.exp(b_last - b)).astype(BF16)

    k_a = jnp.where(first_of_pair, k, 0.0)
    k_b = k - k_a
    lv2 = lv_ref[...]
    factors = []
    for l in range(nlev + 1):
        if l < nlev:
            half = c >> (l + 1)
            d = b - block_ref(b, 2 * half, half - 1)
            e = jnp.exp(jnp.minimum(d, -d))
            qt, kta, ktb = q * e, k_a * e, k_b * e
        else:
            qt, kta, ktb = q, k_a, k_b
        factors.append((qt.astype(BF16), kta.astype(BF16), ktb.astype(BF16)))
    a = [[None] * (GLA_HEADS // 2) for _ in range(n_chunks)]
    for ci in range(n_chunks):
        rows = slice(ci * c, (ci + 1) * c)
        for pr in range(GLA_HEADS // 2):
            ls = slice(pr * 2 * GLA_DK, (pr + 1) * 2 * GLA_DK)
            acc = jnp.zeros((c, 2 * c), F32)
            for l, (qt, kta, ktb) in enumerate(factors):
                rhs = jnp.concatenate([kta[rows, ls], ktb[rows, ls]], axis=0)
                acc = jnp.where(lv2 == l, _dot_nt(qt[rows, ls], rhs), acc)
            a[ci][pr] = acc

    for ci in range(n_chunks):
        rows = slice(ci * c, (ci + 1) * c)
        v = vr_ref[rows, 0:GLA_V]
        s_all = s_scr[...]
        s_bd = jnp.concatenate(
            [jnp.where((row_k // GLA_DK) == h, s_all, 0.0).astype(BF16) for h in range(GLA_HEADS)], axis=1)
        o_inter = _dot(qe[rows], s_bd)
        u_all = _dot_tn(kl[rows], v)
        dcol = jnp.exp(jnp.broadcast_to(b[ci * c + c - 1:(ci + 1) * c, :], (LANES, GLA_QK)).T)
        for h in range(GLA_HEADS):
            vs = slice(h * GLA_DV, (h + 1) * GLA_DV)
            ks = slice(h * GLA_DK, (h + 1) * GLA_DK)
            a_h = a[ci][h // 2][:, (h % 2) * c:(h % 2 + 1) * c]
            o = _dot(a_h.astype(BF16), v[:, vs]) + o_inter[:, vs]
            on = _rms(o, gn_ref[:, vs])
            gate = vr_ref[rows, GLA_V + h * GLA_DV:GLA_V + (h + 1) * GLA_DV]
            o_ref[rows, vs] = (on * gate.astype(F32)).astype(o_ref.dtype)
            s_scr[ks, :] = dcol[ks, :] * s_all[ks, :] + u_all[ks, vs]

    @pl.when(step == pl.num_programs(1) - 1)
    def _():
        sout_ref[...] = s_scr[...]


def _gla_call(qk, vr, plr, s0, wg2_p, bg, gn, c, t):
    b, l, _ = qk.shape
    tri, lv2 = _gla_consts(c)
    has_state = s0 is not None
    tile = lambda w: pl.BlockSpec((None, t, w), lambda bi, i: (bi, i, 0))
    state_spec = pl.BlockSpec((None, GLA_QK, GLA_DV), lambda bi, i: (bi, 0, 0))
    in_specs = [tile(2 * GLA_QK), tile(2 * GLA_V), tile(LANES)]
    args = [qk, vr, plr]
    if has_state:
        in_specs.append(state_spec)
        args.append(s0)
    consts = [wg2_p, bg, gn, tri, lv2]
    in_specs += [_full_spec(x.shape) for x in consts]
    return pl.pallas_call(
        functools.partial(_gla_kernel, c=c, t=t, has_state=has_state),
        grid=(b, l // t),
        in_specs=in_specs,
        out_specs=[tile(GLA_V), state_spec],
        out_shape=[jax.ShapeDtypeStruct((b, l, GLA_V), BF16),
                   jax.ShapeDtypeStruct((b, GLA_QK, GLA_DV), F32)],
        scratch_shapes=[pltpu.VMEM((GLA_QK, GLA_DV), F32)],
        compiler_params=pltpu.CompilerParams(
            dimension_semantics=("arbitrary", "arbitrary"), vmem_limit_bytes=VMEM_LIMIT),
        name="gla",
    )(*args, *consts)


def _mlstm_kernel(*refs, c, t, has_state):
    if has_state:
        (mqk_ref, mvo_ref, pif_ref, c0_ref, n0_ref, m0_ref, cv0_ref, cw_ref, cb_ref, bif_ref, tri_ref, sel_ref,
         o_ref, cout_ref, nout_ref, mout_ref, cvout_ref, c_scr, m_scr, cv_scr, qk_scr) = refs
    else:
        (mqk_ref, mvo_ref, pif_ref, cw_ref, cb_ref, bif_ref, tri_ref, sel_ref,
         o_ref, cout_ref, nout_ref, mout_ref, cvout_ref, c_scr, m_scr, cv_scr, qk_scr) = refs
    step = pl.program_id(1)
    hist = CONV_W - 1

    @pl.when(step == 0)
    def _():
        if has_state:
            for h in range(ML_HEADS):
                c_scr[h, :, 0:ML_DH] = c0_ref[h]
                c_scr[h, :, ML_DH:] = jnp.broadcast_to(n0_ref[h], (ML_DH, ML_DH)).T
            m_scr[...] = m0_ref[...]
            cv_scr[0:CONV_PAD - hist, :] = jnp.zeros((CONV_PAD - hist, 2 * ML_W), F32)
            cv_scr[CONV_PAD - hist:CONV_PAD, :] = cv0_ref[...]
        else:
            c_scr[...] = jnp.zeros_like(c_scr)
            m_scr[...] = jnp.zeros_like(m_scr)
            cv_scr[0:CONV_PAD, :] = jnp.zeros((CONV_PAD, 2 * ML_W), F32)

    conv = _causal_conv_silu(cv_scr, mqk_ref[...], cw_ref, cb_ref)
    qk_scr[:, 0:ML_W] = conv[:, 0:ML_W].astype(BF16)
    qk_scr[:, ML_W:] = (conv[:, ML_W:] * (ML_DH ** -0.5)).astype(BF16)

    gts = pif_ref[...] + bif_ref[...]
    flog = pltpu.roll(_log_sigmoid(gts), LANES - ML_HEADS, axis=1)

    n_chunks = t // c
    lane = lax.broadcasted_iota(jnp.int32, (t, LANES), 1)
    row_c = lax.broadcasted_iota(jnp.int32, (t, LANES), 0) % c
    causal = (lax.broadcasted_iota(jnp.int32, (c, c), 1) <= lax.broadcasted_iota(jnp.int32, (c, c), 0))
    ones_v = jnp.ones((c, ML_DH), BF16)

    def slots(pieces):
        out = jnp.zeros((t, LANES), F32)
        for j, piece in enumerate(pieces):
            if not isinstance(piece, float) and j > 0:
                piece = pltpu.roll(piece, ML_HEADS * j, axis=1)
            out = jnp.where((lane >= ML_HEADS * j) & (lane < ML_HEADS * (j + 1)), piece, out)
        return out

    def split3f(x):
        return [p.astype(F32) for p in _split3(x)]

    def per_chunk(rows_of):
        return jnp.concatenate([jnp.broadcast_to(rows_of(ci), (c, LANES)) for ci in range(n_chunks)], axis=0)

    bc = jnp.concatenate([_dot_exact_lhs(tri_ref[...], flog[ci * c:(ci + 1) * c])
                          for ci in range(n_chunks)], axis=0)
    w = gts - bc
    cm = w
    for j in range(int(math.log2(c))):
        sh = 1 << j
        cm = jnp.where(row_c >= sh, jnp.maximum(cm, pltpu.roll(cm, sh, axis=0)), cm)
    m_in = [m_scr[0:1, :]]
    for ci in range(n_chunks):
        last = slice(ci * c + c - 1, (ci + 1) * c)
        m_in.append(bc[last, :] + jnp.maximum(m_in[ci], cm[last, :]))
    m_scr[...] = jnp.broadcast_to(m_in[n_chunks], m_scr.shape)
    mprev = per_chunk(lambda ci: m_in[ci])
    g = jnp.maximum(mprev, cm)
    g_last = per_chunk(lambda ci: g[ci * c + c - 1:(ci + 1) * c, :])
    lhs_all = slots([1.0, 1.0, 1.0] + split3f(g))
    rhs_all = slots(split3f(w) + [-1.0, -1.0, -1.0]).astype(BF16)
    y_all = slots(split3f(mprev - g) + split3f(-(bc + g)) + split3f(w - g_last)).astype(BF16)
    lhs_heads = [jnp.where(lane % ML_HEADS == h, lhs_all, 0.0).astype(BF16) for h in range(ML_HEADS)]

    for ci in range(n_chunks):
        r0 = ci * c
        rows = pl.ds(r0, c)
        for h in range(ML_HEADS):
            hs = slice(h * ML_DH, (h + 1) * ML_DH)
            qh = qk_scr[rows, h * ML_DH:(h + 1) * ML_DH]
            kh = qk_scr[rows, ML_W + h * ML_DH:ML_W + (h + 1) * ML_DH]
            vaug = jnp.concatenate(
                [mvo_ref[rows, h * ML_DH:(h + 1) * ML_DH], ones_v],
                axis=1)
            p = jnp.exp(jnp.where(causal, _dot_nt(lhs_heads[h][r0:r0 + c], rhs_all[r0:r0 + c]), -jnp.inf))
            bx = jnp.exp(_dot(y_all[r0:r0 + c], sel_ref[h]))
            w_inter = bx[:, 0:ML_DH]
            e_mt = bx[:, ML_DH:2 * ML_DH]
            w_state = bx[:, 2 * ML_DH:3 * ML_DH]
            s = (_dot_nt(qh, kh) * p).astype(BF16)
            caug = c_scr[h]
            nd = _dot(s, vaug) + jnp.concatenate([w_inter, w_inter], axis=1) * _dot(qh, caug.astype(BF16))
            hh = nd[:, 0:ML_DH] / jnp.maximum(jnp.abs(nd[:, ML_DH:]), e_mt)
            o_gate = mvo_ref[rows, ML_W + h * ML_DH:ML_W + (h + 1) * ML_DH]
            o_ref[rows, hs] = (o_gate.astype(F32) * hh).astype(o_ref.dtype)
            ks = (kh.astype(F32) * w_state).astype(BF16)
            dec = w_inter[c - 1:c, :]
            c_scr[h] = jnp.concatenate([dec, dec], axis=1) * caug + _dot_tn(ks, vaug)

    @pl.when(step == pl.num_programs(1) - 1)
    def _():
        for h in range(ML_HEADS):
            cout_ref[h] = c_scr[h, :, 0:ML_DH]
            nout_ref[h] = c_scr[h, :, ML_DH:].T[0:8, :]
        mout_ref[...] = m_scr[...]
        cvout_ref[...] = mqk_ref[t - hist:t, :]


def _mlstm_call(mqk, mvo, pif, c0, n0, m0, cv0, cw, cb, bif, c, t):
    b, l, _ = mqk.shape
    has_state = c0 is not None
    tri = jnp.asarray(np.tril(np.ones((c, c), np.float32)), BF16)
    sel = np.zeros((ML_HEADS, LANES, 3 * ML_DH), np.float32)
    for h in range(ML_HEADS):
        for slot in range(9):
            sel[h, ML_HEADS * slot + h, (slot // 3) * ML_DH:(slot // 3 + 1) * ML_DH] = 1.0
    sel = jnp.asarray(sel, BF16)
    tile = lambda w: pl.BlockSpec((None, t, w), lambda bi, i: (bi, i, 0))
    heads = lambda *shape: pl.BlockSpec((None, ML_HEADS) + shape, lambda bi, i: (bi, 0, 0, 0))
    c_spec = heads(ML_DH, ML_DH)
    m_spec = pl.BlockSpec((None, 8, LANES), lambda bi, i: (bi, 0, 0))
    cv_spec = pl.BlockSpec((None, CONV_W - 1, 2 * ML_W), lambda bi, i: (bi, 0, 0))
    in_specs = [tile(2 * ML_W), tile(2 * ML_W), tile(LANES)]
    args = [mqk, mvo, pif]
    consts = [cw, cb, bif, tri, sel]
    out_specs = [tile(ML_W), c_spec, heads(8, ML_DH), m_spec, cv_spec]
    out_shape = [jax.ShapeDtypeStruct((b, l, ML_W), BF16),
                 jax.ShapeDtypeStruct((b, ML_HEADS, ML_DH, ML_DH), F32),
                 jax.ShapeDtypeStruct((b, ML_HEADS, 8, ML_DH), F32),
                 jax.ShapeDtypeStruct((b, 8, LANES), F32),
                 jax.ShapeDtypeStruct((b, CONV_W - 1, 2 * ML_W), F32)]
    scratch = [pltpu.VMEM((ML_HEADS, ML_DH, 2 * ML_DH), F32), pltpu.VMEM((8, LANES), F32),
               pltpu.VMEM((CONV_PAD + t, 2 * ML_W), F32), pltpu.VMEM((t, 2 * ML_W), BF16)]
    if has_state:
        in_specs += [c_spec, heads(1, ML_DH), m_spec, cv_spec]
        args += [c0, n0, m0, cv0]
    in_specs += [_full_spec(x.shape) for x in consts]
    return pl.pallas_call(
        functools.partial(_mlstm_kernel, c=c, t=t, has_state=has_state),
        grid=(b, l // t),
        in_specs=in_specs,
        out_specs=out_specs,
        out_shape=out_shape,
        scratch_shapes=scratch,
        compiler_params=pltpu.CompilerParams(
            dimension_semantics=("arbitrary", "arbitrary"), vmem_limit_bytes=VMEM_LIMIT),
        name="mlstm",
    )(*args, *consts)


def _merge_kernel(x_ref, ga_ref, hb_ref, pmg_ref, wug_ref, wum_ref, wo_ref, g2_ref,
                  wr_hi_ref, wr_lo_ref, br_ref, tril_ref, triu_ref, cnt0_ref,
                  x1_ref, hs_ref, meta_ref, tab_ref, cnt_scr):
    ya = _dot(ga_ref[...], wug_ref[...])
    yb = _dot(hb_ref[...], wum_ref[...])
    z = pmg_ref[:, 0:D_MODEL].astype(F32) * ya + pmg_ref[:, D_MODEL:].astype(F32) * yb
    x1 = x_ref[...] + _dot(z.astype(BF16), wo_ref[...])
    x1_ref[...] = x1
    hm = _rms(x1, g2_ref[...])
    hm_hi = hm.astype(BF16)
    hm_lo = (hm - hm_hi.astype(F32)).astype(BF16)
    hi_both = _dot(hm_hi, jnp.concatenate([wr_hi_ref[...], wr_lo_ref[...]], axis=1))
    logits = hi_both[:, 0:LANES] + hi_both[:, LANES:] + _dot(hm_lo, wr_hi_ref[...]) + br_ref[...]
    lane = lax.broadcasted_iota(jnp.int32, logits.shape, 1)
    neg = -jnp.inf
    is_g = (lane >= N_EXPERTS) & (lane < N_EXPERTS + N_GROUPS)
    lg = jnp.where(is_g, logits, neg)
    mg = jnp.max(lg, axis=-1, keepdims=True)
    p_top = 1.0 / jnp.sum(jnp.exp(lg - mg), axis=-1, keepdims=True)
    gi = jnp.min(jnp.where(lg == mg, lane, 2 * LANES), axis=-1, keepdims=True) - N_EXPERTS
    group_shift = int(math.log2(EXPERTS_PER_GROUP))
    sel = (lane < N_EXPERTS) & (jnp.right_shift(lane, group_shift) == gi)
    le = jnp.where(sel, logits, neg)
    v1 = jnp.max(le, axis=-1, keepdims=True)
    i1 = jnp.min(jnp.where(le == v1, lane, 2 * LANES), axis=-1, keepdims=True)
    le2 = jnp.where(lane == i1, neg, le)
    v2 = jnp.max(le2, axis=-1, keepdims=True)
    i2 = jnp.min(jnp.where(le2 == v2, lane, 2 * LANES), axis=-1, keepdims=True)
    e2 = jnp.exp(v2 - v1)
    w1 = p_top * (1.0 / (1.0 + e2))
    w2 = p_top * (e2 / (1.0 + e2))

    @pl.when(pl.program_id(0) == 0)
    def _():
        cnt_scr[...] = cnt0_ref[...]

    tm = logits.shape[0]
    oh1 = lane == i1
    oh2 = lane == i2
    both = jnp.where(oh1 | oh2, 1.0, 0.0)
    cnt = jnp.sum(both, axis=0, keepdims=True)
    cnt = jnp.floor((cnt + (SEG_ALIGN - 1)) * (1.0 / SEG_ALIGN)) * SEG_ALIGN
    lower = _dot(jnp.broadcast_to(cnt, (8, LANES)).astype(BF16), triu_ref[...])[0:1, :]
    lpos = _dot(tril_ref[...], both.astype(BF16)) + lower
    lp1 = jnp.sum(jnp.where(oh1, lpos, 0.0), axis=-1, keepdims=True)
    lp2 = jnp.sum(jnp.where(oh2, lpos, 0.0), axis=-1, keepdims=True)
    pos = lax.broadcasted_iota(jnp.int32, (tm, hs_ref.shape[0]), 1)
    onehot = jnp.where((pos == lp1.astype(jnp.int32)) | (pos == lp2.astype(jnp.int32)), 1.0, 0.0)
    hs_ref[...] = _dot_tn(onehot.astype(BF16), hm_hi).astype(BF16)
    cols = (lp1, lp2, w1, w2)
    meta = jnp.zeros((tm, LANES), F32)
    for ci, col in enumerate(cols):
        meta = jnp.where(lane == ci, col, meta)
    meta_ref[...] = meta
    row8 = lax.broadcasted_iota(jnp.int32, (8, LANES), 0)
    tab_ref[...] = jnp.where(row8 == 0, cnt, jnp.where(row8 == 1, lower, jnp.where(row8 == 2, cnt_scr[...], 0.0)))
    cnt_scr[...] += cnt


def _merge_call(x2, ga, hb, pmg, wug, wum, wo, g2, wr_hi, wr_lo, br, cnt0, tm):
    n = x2.shape[0]
    tile = lambda w: pl.BlockSpec((tm, w), lambda i: (i, 0))
    tiles = lambda *shape: pl.BlockSpec((None,) + shape, lambda i: (i, 0, 0))
    local_rows = 2 * tm + SEG_ALIGN * N_EXPERTS
    consts = [wug, wum, wo, g2, wr_hi, wr_lo, br,
              jnp.asarray(np.tril(np.ones((tm, tm), np.float32), -1), BF16),
              jnp.asarray(np.triu(np.ones((LANES, LANES), np.float32), 1), BF16),
              cnt0]
    return pl.pallas_call(
        _merge_kernel,
        grid=(n // tm,),
        in_specs=[tile(D_MODEL), tile(GLA_V), tile(ML_W), tile(W_MG)]
                 + [_full_spec(x.shape) for x in consts],
        out_specs=[tile(D_MODEL), tiles(local_rows, D_MODEL), tile(LANES), tiles(8, LANES)],
        out_shape=[jax.ShapeDtypeStruct((n, D_MODEL), F32),
                   jax.ShapeDtypeStruct((n // tm, local_rows, D_MODEL), BF16),
                   jax.ShapeDtypeStruct((n, LANES), F32),
                   jax.ShapeDtypeStruct((n // tm, 8, LANES), F32)],
        scratch_shapes=[pltpu.VMEM((1, LANES), F32)],
        compiler_params=pltpu.CompilerParams(
            dimension_semantics=("arbitrary",), vmem_limit_bytes=VMEM_LIMIT),
        name="merge",
    )(x2, ga, hb, pmg, *consts)


MOE_ROWS = 512
SEG_ALIGN = 16


def _segment_copies(fn, tables, tile, local_ref, buffer, flat_ref, sem, to_flat):
    pieces_ref, gdst_ref = tables
    stride = gdst_ref.shape[0] // pieces_ref.shape[0]

    def body(p, carry):
        local = local_ref.at[buffer, pl.ds(pl.multiple_of(p * SEG_ALIGN, SEG_ALIGN), SEG_ALIGN), :]
        flat = flat_ref.at[pl.ds(pl.multiple_of(gdst_ref[tile * stride + p], SEG_ALIGN), SEG_ALIGN), :]
        src, dst = (local, flat) if to_flat else (flat, local)
        fn(pltpu.make_async_copy(src, dst, sem))
        return carry

    lax.fori_loop(0, pieces_ref[tile], body, 0)


def _gmm_kernel(te_ref, nv_ref, part_ref, tile_ref, row_ref, *refs, n_parts):
    hs_refs = refs[:n_parts]
    wg_ref, wu_ref, wd_ref, o_ref, x_buf, sem, wgu_scr, wd_scr = refs[n_parts:]
    j = pl.program_id(0)
    n_valid = nv_ref[0]
    used = j < n_valid
    pieces_per_tile = MOE_ROWS // SEG_ALIGN

    def fetch(fn, tile):
        slot = tile % 2
        for k in range(pieces_per_tile):
            g = tile * pieces_per_tile + k
            for part, hs_ref in enumerate(hs_refs):
                @pl.when(part_ref[g] == part)
                def _():
                    src = hs_ref.at[tile_ref[g], pl.ds(pl.multiple_of(row_ref[g], SEG_ALIGN), SEG_ALIGN), :]
                    dst = x_buf.at[slot, pl.ds(k * SEG_ALIGN, SEG_ALIGN), :]
                    fn(pltpu.make_async_copy(src, dst, sem.at[slot]))

    @pl.when(j == 0)
    def _():
        x_buf[...] = jnp.zeros_like(x_buf)
        fetch(lambda c: c.start(), j)

    @pl.when(j + 1 < n_valid)
    def _():
        fetch(lambda c: c.start(), j + 1)

    @pl.when(used & ((j == 0) | (te_ref[j] != te_ref[jnp.maximum(j - 1, 0)])))
    def _():
        wgu_scr[:, 0:D_EXPERT] = wg_ref[...].astype(BF16)
        wgu_scr[:, D_EXPERT:] = wu_ref[...].astype(BF16)
        wd_scr[...] = wd_ref[...].astype(BF16)

    @pl.when(used)
    def _():
        fetch(lambda c: c.wait(), j)
        au = _dot(x_buf[j % 2], wgu_scr[...])
        a = au[:, 0:D_EXPERT]
        hh = (a * _sigmoid(a)) * au[:, D_EXPERT:]
        o_ref[...] = _dot(hh.astype(BF16), wd_scr[...]).astype(o_ref.dtype)

    @pl.when(jnp.logical_not(used))
    def _():
        o_ref[...] = jnp.zeros_like(o_ref)


def _gmm_call(tile_expert, n_valid, sources, hs_parts, wg, wu, wd):
    n_tiles = tile_expert.shape[0]
    wsel = lambda j, te, *_: (te[j], 0, 0)
    any_spec = pl.BlockSpec(memory_space=pl.ANY)
    return pl.pallas_call(
        functools.partial(_gmm_kernel, n_parts=len(hs_parts)),
        grid_spec=pltpu.PrefetchScalarGridSpec(
            num_scalar_prefetch=5,
            grid=(n_tiles,),
            in_specs=[any_spec] * len(hs_parts)
                     + [pl.BlockSpec((None, D_MODEL, D_EXPERT), wsel),
                        pl.BlockSpec((None, D_MODEL, D_EXPERT), wsel),
                        pl.BlockSpec((None, D_EXPERT, D_MODEL), wsel)],
            out_specs=pl.BlockSpec((MOE_ROWS, D_MODEL), lambda j, *_: (j, 0)),
            scratch_shapes=[pltpu.VMEM((2, MOE_ROWS, D_MODEL), BF16),
                            pltpu.SemaphoreType.DMA((2,)),
                            pltpu.VMEM((D_MODEL, 2 * D_EXPERT), BF16),
                            pltpu.VMEM((D_EXPERT, D_MODEL), BF16)],
        ),
        out_shape=jax.ShapeDtypeStruct((n_tiles * MOE_ROWS, D_MODEL), BF16),
        compiler_params=pltpu.CompilerParams(
            dimension_semantics=("arbitrary",), vmem_limit_bytes=VMEM_LIMIT),
        name="moe_grouped",
    )(tile_expert, n_valid, *sources, *hs_parts, wg, wu, wd)


def _combine_kernel(pieces_ref, gdst_ref, x1_ref, meta_ref, gf_ref, os_ref, y_ref, buf_ref, sem, *, tile0):
    step = pl.program_id(0)
    n_steps = pl.num_programs(0)
    tm = x1_ref.shape[0]
    tables = (pieces_ref, gdst_ref)

    def fetch(fn, i):
        slot = i % 2
        _segment_copies(fn, tables, tile0 + i, buf_ref, slot, os_ref, sem.at[slot], False)

    @pl.when(step == 0)
    def _():
        buf_ref[...] = jnp.zeros_like(buf_ref)
        fetch(lambda c: c.start(), step)

    @pl.when(step + 1 < n_steps)
    def _():
        fetch(lambda c: c.start(), step + 1)

    fetch(lambda c: c.wait(), step)
    rows = buf_ref[step % 2]
    pos = lax.broadcasted_iota(jnp.int32, (tm, rows.shape[0]), 1)
    lp1 = meta_ref[:, 0:1].astype(jnp.int32)
    lp2 = meta_ref[:, 1:2].astype(jnp.int32)
    q = jnp.where(pos == lp1, meta_ref[:, 2:3], 0.0) + jnp.where(pos == lp2, meta_ref[:, 3:4], 0.0)
    y = x1_ref[...] + _dot(q.astype(BF16), rows)
    y_ref[...] = _rms(y, gf_ref[...])


def _combine_call(tables, x1, meta, gf, out_sorted, tile0, local_rows, tm):
    n = x1.shape[0]
    tile = lambda w: pl.BlockSpec((tm, w), lambda i, *_: (i, 0))
    return pl.pallas_call(
        functools.partial(_combine_kernel, tile0=tile0),
        grid_spec=pltpu.PrefetchScalarGridSpec(
            num_scalar_prefetch=2,
            grid=(n // tm,),
            in_specs=[tile(D_MODEL), tile(LANES), pl.BlockSpec(gf.shape, lambda i, *_: (0, 0)),
                      pl.BlockSpec(memory_space=pl.ANY)],
            out_specs=tile(D_MODEL),
            scratch_shapes=[pltpu.VMEM((2, local_rows, D_MODEL), out_sorted.dtype),
                            pltpu.SemaphoreType.DMA((2,))],
        ),
        out_shape=jax.ShapeDtypeStruct((n, D_MODEL), F32),
        compiler_params=pltpu.CompilerParams(
            dimension_semantics=("arbitrary",), vmem_limit_bytes=VMEM_LIMIT),
        name="moe_combine",
    )(*tables, x1, meta, gf, out_sorted)


def _sparse_moe(parts, p):
    token_tiles = [part[1].shape[0] for part in parts]
    n_tiles = -(-sum(part[1].shape[0] * part[1].shape[1] for part in parts) // MOE_ROWS) + N_EXPERTS
    tab = jnp.concatenate([part[3] for part in parts], axis=0)
    tab = tab[:, :, :N_EXPERTS].astype(jnp.int32)
    cnt, before = tab[:, 0], tab[:, 2]
    total = before[-1] + cnt[-1]
    tiles = (total + (MOE_ROWS - 1)) // MOE_ROWS
    tile_end = jnp.cumsum(tiles)
    tile_start = tile_end - tiles
    n_valid = tile_end[-1:]
    gpos = tile_start[None, :] * MOE_ROWS + before
    j = jnp.minimum(jnp.arange(n_tiles, dtype=jnp.int32), n_valid - 1)
    tile_expert = jnp.sum((j[:, None] >= tile_end[None, :]).astype(jnp.int32), axis=1)
    max_pieces = max(part[1].shape[1] for part in parts) // SEG_ALIGN
    pieces = cnt // SEG_ALIGN
    piece_end = jnp.cumsum(pieces, axis=1)
    piece_start = piece_end - pieces
    piece = jnp.arange(max_pieces, dtype=jnp.int32)
    expert_of = jnp.sum((piece[None, :, None] >= piece_end[:, None, :]).astype(jnp.int32), axis=2)
    onehot = expert_of[:, :, None] == jnp.arange(N_EXPERTS, dtype=jnp.int32)
    first_row = gpos - SEG_ALIGN * piece_start
    gdst = jnp.sum(jnp.where(onehot, first_row[:, None, :], 0), axis=2) + SEG_ALIGN * piece[None, :]
    tables = (piece_end[:, -1], gdst.reshape(-1))
    n_token_tiles = sum(token_tiles)
    over_tiles = jnp.cumsum(pieces, axis=0)
    e_j = tile_expert
    q0 = (jnp.arange(n_tiles, dtype=jnp.int32) - jnp.take(tile_start, e_j)) * (MOE_ROWS // SEG_ALIGN)
    q = q0[:, None] + jnp.arange(MOE_ROWS // SEG_ALIGN, dtype=jnp.int32)[None, :]
    valid = (jnp.arange(n_tiles)[:, None] < n_valid) & (q < jnp.take(over_tiles[-1], e_j)[:, None])
    per_tile = lambda table: jnp.take(table.T, e_j, axis=0)[:, None, :]
    src_tile = jnp.sum((q[:, :, None] >= per_tile(over_tiles)).astype(jnp.int32), axis=2)
    src_tile = jnp.minimum(src_tile, n_token_tiles - 1)
    at_src = lambda table: jnp.sum(jnp.where(
        src_tile[:, :, None] == jnp.arange(n_token_tiles, dtype=jnp.int32), per_tile(table), 0), axis=2)
    src_row = (at_src(piece_start) + q - at_src(over_tiles - pieces)) * SEG_ALIGN
    part_start = np.cumsum([0] + token_tiles)
    src_part = jnp.sum((src_tile[:, :, None] >= jnp.asarray(part_start[1:], jnp.int32)).astype(jnp.int32), axis=2)
    src_tile_in_part = src_tile - jnp.sum(jnp.where(
        src_part[:, :, None] == jnp.arange(len(parts), dtype=jnp.int32), jnp.asarray(part_start[:-1], jnp.int32), 0), axis=2)
    sources = (jnp.where(valid, src_part, len(parts)).reshape(-1), src_tile_in_part.reshape(-1), src_row.reshape(-1))
    out_sorted = _gmm_call(tile_expert, n_valid.astype(jnp.int32), sources, [part[1] for part in parts],
                           p["wg"], p["wu"], p["wd"])
    return [_combine_call(tables, part[0], part[2], p["gf"], out_sorted, int(tile0),
                          part[1].shape[1], part[0].shape[0] // part[1].shape[0])
            for part, tile0 in zip(parts, part_start[:-1])]


def _pad_cols(w, width):
    return jnp.pad(w, ((0, 0), (0, width - w.shape[1])))


def _prep_weights(norm1_g, w_in, gla_w_gate2, gla_b_gate, gla_norm_g, w_up_gla,
                  ml_conv_w, ml_conv_b, ml_b_i, ml_b_f, w_up_ml, w_out,
                  norm2_g, router_g_w, router_g_b, router_e_w, router_e_b,
                  moe_w_gate, moe_w_up, moe_w_down, final_g):
    wr =_pad_cols(jnp.concatenate([router_e_w, router_g_w], axis=1), LANES)
    wr_hi = wr.astype(BF16)
    wr_lo = (wr - wr_hi.astype(F32)).astype(BF16)
    br = _pad_cols(jnp.concatenate([router_e_b, router_g_b])[None, :], LANES)
    wg2 = jnp.pad(gla_w_gate2, ((0, LANES - GLA_GATE_RANK), (0, 0)))
    wg2_hi = wg2.astype(BF16)
    return dict(
        g1=norm1_g[None, :], w_in=w_in.T,
        wg2_p=jnp.stack([wg2_hi, (wg2 - wg2_hi.astype(F32)).astype(BF16)]),
        bg=gla_b_gate[None, :], gn=gla_norm_g[None, :],
        wug=w_up_gla.astype(BF16),
        cw=ml_conv_w, cb=ml_conv_b[None, :],
        bif=_pad_cols(jnp.concatenate([ml_b_i, ml_b_f])[None, :], LANES),
        wum=w_up_ml.astype(BF16), wo=w_out.astype(BF16),
        g2=norm2_g[None, :], wr_hi=wr_hi, wr_lo=wr_lo, br=br,
        wg=moe_w_gate.reshape(N_EXPERTS, D_MODEL, D_EXPERT),
        wu=moe_w_up.reshape(N_EXPERTS, D_MODEL, D_EXPERT),
        wd=moe_w_down.reshape(N_EXPERTS, D_EXPERT, D_MODEL),
        gf=final_g[None, :],
    )


def _mixers(x, gla_s0, ml_c0, ml_n0, ml_m0, conv0, p, placed, *,
            gla_chunk, ml_chunk, seq_tile, row_tile, merge_tile):
    b, l, _ = x.shape
    n = b * l
    x2 = x.reshape(n, D_MODEL)
    qk, vr, plr, mqk, mvo, pif, pmg = _proj_call(x2, p["g1"], p["w_in"], row_tile)
    r3 = lambda a: a.reshape(b, l, a.shape[-1])
    s0 = None if gla_s0 is None else gla_s0.reshape(b, GLA_QK, GLA_DV)
    ga, gla_s = _gla_call(r3(qk), r3(vr), r3(plr), s0, p["wg2_p"], p["bg"], p["gn"], gla_chunk, seq_tile)
    gla_s = gla_s.reshape(b, GLA_HEADS, GLA_DK, GLA_DV)
    if ml_c0 is None:
        n0 = m0 = None
    else:
        n0 = ml_n0[:, :, None, :]
        m0 = jnp.broadcast_to(_pad_cols(ml_m0, LANES)[:, None, :], (b, 8, LANES))
    hb, ml_c, ml_n, m_b, new_conv = _mlstm_call(r3(mqk), r3(mvo), r3(pif), ml_c0, n0, m0, conv0,
                                                p["cw"], p["cb"], p["bif"], ml_chunk, seq_tile)
    part = _merge_call(x2, ga.reshape(n, GLA_V), hb.reshape(n, ML_W), pmg,
                       p["wug"], p["wum"], p["wo"], p["g2"],
                       p["wr_hi"], p["wr_lo"], p["br"], placed, merge_tile)
    states = (gla_s[None], ml_c[None], ml_n[:, :, 0, :][None], m_b[:, 0, 0:ML_HEADS][None], new_conv[None])
    return part, states


def kernel(x_prompt, x_sample, state_gla_S, state_mlstm_C, state_mlstm_n, state_mlstm_m, state_mlstm_conv, norm1_g, w_in, gla_w_gate2, gla_b_gate, gla_norm_g, w_up_gla, ml_conv_w, ml_conv_b, ml_b_i, ml_b_f, w_up_ml, w_out, norm2_g, router_g_w, router_g_b, router_e_w, router_e_b, moe_w_gate, moe_w_up, moe_w_down, final_g):
    assert norm1_g.shape[0] == 1, "single-layer trunk"
    p = _prep_weights(norm1_g[0], w_in[0], gla_w_gate2[0], gla_b_gate[0], gla_norm_g[0], w_up_gla[0],
                      ml_conv_w[0], ml_conv_b[0], ml_b_i[0], ml_b_f[0], w_up_ml[0], w_out[0],
                      norm2_g[0], router_g_w[0], router_g_b[0], router_e_w[0], router_e_b[0],
                      moe_w_gate[0], moe_w_up[0], moe_w_down[0], final_g)
    dec_seq = x_sample.shape[1]
    n_sample = x_sample.shape[0] * dec_seq
    part_p, sp = _mixers(x_prompt, None, None, None, None, None, p, jnp.zeros((1, LANES), F32),
                         gla_chunk=128, ml_chunk=256, seq_tile=1024, row_tile=256, merge_tile=512)
    placed = part_p[3][-1, 0:1, :] + part_p[3][-1, 2:3, :]
    part_s, ss = _mixers(x_sample, state_gla_S[0], state_mlstm_C[0], state_mlstm_n[0], state_mlstm_m[0],
                         state_mlstm_conv[0], p, placed,
                         gla_chunk=dec_seq, ml_chunk=dec_seq, seq_tile=dec_seq,
                         row_tile=n_sample, merge_tile=n_sample)
    yp, ys = _sparse_moe([part_p, part_s], p)
    return (yp.reshape(x_prompt.shape), ys.reshape(x_sample.shape), *sp, *ss)
```

```python
import functools
import math

import numpy as np
import jax
import jax.numpy as jnp
from jax import lax
from jax.experimental import pallas as pl
from jax.experimental.pallas import tpu as pltpu

D_MODEL = 1024
GLA_HEADS = 4
GLA_DK = 64
GLA_DV = 128
GLA_GATE_RANK = 16
GLA_TAU = 16.0
ML_HEADS = 4
ML_DH = 128
CONV_W = 4
N_GROUPS = 4
EXPERTS_PER_GROUP = 8
N_EXPERTS = N_GROUPS * EXPERTS_PER_GROUP
D_EXPERT = 256
EPS = 1e-6

GLA_QK = GLA_HEADS * GLA_DK
GLA_V = GLA_HEADS * GLA_DV
ML_W = ML_HEADS * ML_DH

LANES = 128
VMEM_LIMIT = 56 * 1024 * 1024

W_GLA = 2 * GLA_QK + 2 * GLA_V
W_ML = 2 * ML_W + ML_W + ML_W
W_MG = 2 * D_MODEL
PROJ_WIDTHS = (W_GLA, LANES, W_ML, LANES, W_MG)
PROJ_SOURCE_WIDTHS = (W_GLA, GLA_GATE_RANK, W_ML, 2 * ML_HEADS, W_MG)

F32 = jnp.float32
BF16 = jnp.bfloat16

PROJ_OUTPUTS = ((2 * GLA_QK, F32), (2 * GLA_V, BF16), (LANES, F32), (2 * ML_W, F32), (2 * ML_W, BF16),
                (LANES, F32), (W_MG, BF16))


def _dot(a, b):
    return jnp.dot(a, b, preferred_element_type=F32)


def _dot_nt(a, b):
    return lax.dot_general(a, b, (((1,), (1,)), ((), ())), preferred_element_type=F32)


def _dot_tn(a, b):
    return lax.dot_general(a, b, (((0,), (0,)), ((), ())), preferred_element_type=F32)


def _split3(x):
    hi = x.astype(BF16)
    r1 = x - hi.astype(F32)
    mid = r1.astype(BF16)
    lo = (r1 - mid.astype(F32)).astype(BF16)
    return hi, mid, lo


def _dot_exact_lhs(m, x):
    hi, mid, lo = _split3(x)
    return _dot(m, hi) + _dot(m, mid) + _dot(m, lo)


def _log_sigmoid(z):
    return jnp.minimum(z, 0.0) - jnp.log(1.0 + jnp.exp(-jnp.abs(z)))


def _sigmoid(z):
    return 1.0 / (1.0 + jnp.exp(-z))


def _rms(x, g):
    return x * lax.rsqrt(jnp.mean(x * x, axis=-1, keepdims=True) + EPS) * g


def _full_spec(shape):
    nd = len(shape)
    return pl.BlockSpec(shape, lambda *_: (0,) * nd)


CONV_PAD = 8


def _causal_conv_silu(stage_ref, x, cw_ref, cb_ref):
    t = x.shape[0]
    stage_ref[CONV_PAD:CONV_PAD + t, :] = x
    acc = cb_ref[...] + stage_ref[CONV_PAD:CONV_PAD + t, :] * cw_ref[CONV_W - 1:CONV_W, :]
    for d in range(1, CONV_W):
        acc = acc + stage_ref[CONV_PAD - d:CONV_PAD - d + t, :] * cw_ref[CONV_W - 1 - d:CONV_W - d, :]
    stage_ref[0:CONV_PAD, :] = stage_ref[t:t + CONV_PAD, :]
    return acc * _sigmoid(acc)


def _proj_kernel(x_ref, g_ref, win_ref, qk_ref, vr_ref, lr_ref, mqk_ref, mvo_ref, if_ref, mg_ref, w_ref):
    starts = np.cumsum((0,) + PROJ_WIDTHS)

    @pl.when(pl.program_id(0) == 0)
    def _():
        src = np.cumsum((0,) + PROJ_SOURCE_WIDTHS)
        chunk = 512
        for g, width in enumerate(PROJ_SOURCE_WIDTHS):
            for r0 in range(0, width, chunk):
                rows = min(chunk, width - r0)
                w_ref[starts[g] + r0:starts[g] + r0 + rows, :] = (
                    win_ref[src[g] + r0:src[g] + r0 + rows, :].astype(BF16))
            if width < PROJ_WIDTHS[g]:
                w_ref[starts[g] + width:starts[g + 1], :] = jnp.zeros((PROJ_WIDTHS[g] - width, D_MODEL), BF16)

    h = _rms(x_ref[...], g_ref[...]).astype(BF16)

    def cols(group, lo, hi):
        return _dot_nt(h, w_ref[starts[group] + lo:starts[group] + hi, :])

    qk_ref[...] = cols(0, 0, 2 * GLA_QK)
    vr_ref[:, 0:GLA_V] = cols(0, 2 * GLA_QK, 2 * GLA_QK + GLA_V).astype(BF16)
    r = cols(0, 2 * GLA_QK + GLA_V, W_GLA)
    vr_ref[:, GLA_V:] = (r * _sigmoid(r)).astype(BF16)
    lr_ref[...] = cols(1, 0, LANES)
    mqk_ref[...] = cols(2, 0, 2 * ML_W)
    mvo_ref[:, 0:ML_W] = cols(2, 2 * ML_W, 3 * ML_W).astype(BF16)
    mvo_ref[:, ML_W:] = _sigmoid(cols(2, 3 * ML_W, W_ML)).astype(BF16)
    if_ref[...] = cols(3, 0, LANES)
    mg_ref[...] = _sigmoid(cols(4, 0, W_MG)).astype(BF16)


def _proj_call(x2, g, w_in, tm):
    n = x2.shape[0]
    assert w_in.shape == (sum(PROJ_SOURCE_WIDTHS), D_MODEL)
    return pl.pallas_call(
        _proj_kernel,
        grid=(n // tm,),
        in_specs=[pl.BlockSpec((tm, D_MODEL), lambda i: (i, 0)),
                  _full_spec(g.shape),
                  pl.BlockSpec(w_in.shape, lambda i: (0, 0), pipeline_mode=pl.Buffered(1))],
        out_specs=[pl.BlockSpec((tm, w), lambda i: (i, 0)) for w, _ in PROJ_OUTPUTS],
        out_shape=[jax.ShapeDtypeStruct((n, w), dt) for w, dt in PROJ_OUTPUTS],
        scratch_shapes=[pltpu.VMEM((sum(PROJ_WIDTHS), D_MODEL), BF16)],
        compiler_params=pltpu.CompilerParams(
            dimension_semantics=("arbitrary",), vmem_limit_bytes=VMEM_LIMIT),
        name="in_proj",
    )(x2, g, w_in)


def _gla_consts(c):
    nlev = int(math.log2(c))
    assert 1 << nlev == c
    t = np.arange(c)[:, None]
    j = np.arange(c)[None, :]
    lv = np.full((c, c), -1, np.int32)
    for l in range(nlev):
        h = c >> (l + 1)
        upper = (t % (2 * h)) >= h
        same = (j // (2 * h)) == (t // (2 * h))
        s_lower = (j % (2 * h)) < h
        lv[np.broadcast_to(upper, (c, c)) & same & s_lower] = l
    lv[np.eye(c, dtype=bool)] = nlev
    tri = (j <= t).astype(np.float32)
    return jnp.asarray(tri, BF16), jnp.asarray(np.concatenate([lv, lv], axis=1))


def _gla_kernel(*refs, c, t, has_state):
    if has_state:
        (qk_ref, vr_ref, plr_ref, s0_ref, wg2_ref, bg_ref, gn_ref, tri_ref, lv_ref,
         o_ref, sout_ref, s_scr) = refs
    else:
        (qk_ref, vr_ref, plr_ref, wg2_ref, bg_ref, gn_ref, tri_ref, lv_ref,
         o_ref, sout_ref, s_scr) = refs
    nlev = int(math.log2(c))
    step = pl.program_id(1)

    @pl.when(step == 0)
    def _():
        if has_state:
            s_scr[...] = s0_ref[...]
        else:
            s_scr[...] = jnp.zeros_like(s_scr)

    n_chunks = t // c
    lane_k = lax.broadcasted_iota(jnp.int32, (t, GLA_QK), 1)
    first_of_pair = (lane_k % (2 * GLA_DK)) < GLA_DK
    row_k = lax.broadcasted_iota(jnp.int32, (GLA_QK, GLA_DV), 0)
    row_t = lax.broadcasted_iota(jnp.int32, (t, GLA_QK), 0)

    def block_ref(b, blk, idx):
        if blk >= 8:
            b3 = b.reshape(t // blk, blk, GLA_QK)
            return jnp.broadcast_to(b3[:, idx:idx + 1, :], b3.shape).reshape(t, GLA_QK)
        r = row_t % blk
        out = b
        for sh in range(-idx, blk - idx):
            if sh != 0:
                out = jnp.where(r - idx == sh, pltpu.roll(b, sh % t, axis=0), out)
        return out

    q = qk_ref[:, 0:GLA_QK] * (GLA_DK ** -0.5)
    k = qk_ref[:, GLA_QK:2 * GLA_QK]
    glr = plr_ref[...]
    g_hi = glr.astype(BF16)
    g_lo = (glr - g_hi.astype(F32)).astype(BF16)
    z = (_dot(g_hi, wg2_ref[0]) + _dot(g_lo, wg2_ref[0]) + _dot(g_hi, wg2_ref[1])) + bg_ref[...]
    la = _log_sigmoid(z) * (1.0 / GLA_TAU)
    b = jnp.concatenate([_dot_exact_lhs(tri_ref[...], la[ci * c:(ci + 1) * c])
                         for ci in range(n_chunks)], axis=0)
    b_last = block_ref(b, c, c - 1)
    qe = (q * jnp.exp(b)).astype(BF16)
    kl = (k * jnp.exp(b_last - b)).astype(BF16)

    k_a = jnp.where(first_of_pair, k, 0.0)
    k_b = k - k_a
    lv2 = lv_ref[...]
    factors = []
    for l in range(nlev + 1):
        if l < nlev:
            half = c >> (l + 1)
            d = b - block_ref(b, 2 * half, half - 1)
            e = jnp.exp(jnp.minimum(d, -d))
            qt, kta, ktb = q * e, k_a * e, k_b * e
        else:
            qt, kta, ktb = q, k_a, k_b
        factors.append((qt.astype(BF16), kta.astype(BF16), ktb.astype(BF16)))
    a = [[None] * (GLA_HEADS // 2) for _ in range(n_chunks)]
    for ci in range(n_chunks):
        rows = slice(ci * c, (ci + 1) * c)
        for pr in range(GLA_HEADS // 2):
            ls = slice(pr * 2 * GLA_DK, (pr + 1) * 2 * GLA_DK)
            acc = jnp.zeros((c, 2 * c), F32)
            for l, (qt, kta, ktb) in enumerate(factors):
                rhs = jnp.concatenate([kta[rows, ls], ktb[rows, ls]], axis=0)
                acc = jnp.where(lv2 == l, _dot_nt(qt[rows, ls], rhs), acc)
            a[ci][pr] = acc

    for ci in range(n_chunks):
        rows = slice(ci * c, (ci + 1) * c)
        v = vr_ref[rows, 0:GLA_V]
        s_all = s_scr[...]
        s_bd = jnp.concatenate(
            [jnp.where((row_k // GLA_DK) == h, s_all, 0.0).astype(BF16) for h in range(GLA_HEADS)], axis=1)
        o_inter = _dot(qe[rows], s_bd)
        u_all = _dot_tn(kl[rows], v)
        dcol = jnp.exp(jnp.broadcast_to(b[ci * c + c - 1:(ci + 1) * c, :], (LANES, GLA_QK)).T)
        for h in range(GLA_HEADS):
            vs = slice(h * GLA_DV, (h + 1) * GLA_DV)
            ks = slice(h * GLA_DK, (h + 1) * GLA_DK)
            a_h = a[ci][h // 2][:, (h % 2) * c:(h % 2 + 1) * c]
            o = _dot(a_h.astype(BF16), v[:, vs]) + o_inter[:, vs]
            on = _rms(o, gn_ref[:, vs])
            gate = vr_ref[rows, GLA_V + h * GLA_DV:GLA_V + (h + 1) * GLA_DV]
            o_ref[rows, vs] = (on * gate.astype(F32)).astype(o_ref.dtype)
            s_scr[ks, :] = dcol[ks, :] * s_all[ks, :] + u_all[ks, vs]

    @pl.when(step == pl.num_programs(1) - 1)
    def _():
        sout_ref[...] = s_scr[...]


def _gla_call(qk, vr, plr, s0, wg2_p, bg, gn, c, t):
    b, l, _ = qk.shape
    tri, lv2 = _gla_consts(c)
    has_state = s0 is not None
    tile = lambda w: pl.BlockSpec((None, t, w), lambda bi, i: (bi, i, 0))
    state_spec = pl.BlockSpec((None, GLA_QK, GLA_DV), lambda bi, i: (bi, 0, 0))
    in_specs = [tile(2 * GLA_QK), tile(2 * GLA_V), tile(LANES)]
    args = [qk, vr, plr]
    if has_state:
        in_specs.append(state_spec)
        args.append(s0)
    consts = [wg2_p, bg, gn, tri, lv2]
    in_specs += [_full_spec(x.shape) for x in consts]
    return pl.pallas_call(
        functools.partial(_gla_kernel, c=c, t=t, has_state=has_state),
        grid=(b, l // t),
        in_specs=in_specs,
        out_specs=[tile(GLA_V), state_spec],
        out_shape=[jax.ShapeDtypeStruct((b, l, GLA_V), BF16),
                   jax.ShapeDtypeStruct((b, GLA_QK, GLA_DV), F32)],
        scratch_shapes=[pltpu.VMEM((GLA_QK, GLA_DV), F32)],
        compiler_params=pltpu.CompilerParams(
            dimension_semantics=("arbitrary", "arbitrary"), vmem_limit_bytes=VMEM_LIMIT),
        name="gla",
    )(*args, *consts)


def _mlstm_kernel(*refs, c, t, has_state):
    if has_state:
        (mqk_ref, mvo_ref, pif_ref, c0_ref, n0_ref, m0_ref, cv0_ref, cw_ref, cb_ref, bif_ref, tri_ref, sel_ref,
         o_ref, cout_ref, nout_ref, mout_ref, cvout_ref, c_scr, m_scr, cv_scr, qk_scr) = refs
    else:
        (mqk_ref, mvo_ref, pif_ref, cw_ref, cb_ref, bif_ref, tri_ref, sel_ref,
         o_ref, cout_ref, nout_ref, mout_ref, cvout_ref, c_scr, m_scr, cv_scr, qk_scr) = refs
    step = pl.program_id(1)
    hist = CONV_W - 1

    @pl.when(step == 0)
    def _():
        if has_state:
            for h in range(ML_HEADS):
                c_scr[h, :, 0:ML_DH] = c0_ref[h]
                c_scr[h, :, ML_DH:] = jnp.broadcast_to(n0_ref[h], (ML_DH, ML_DH)).T
            m_scr[...] = m0_ref[...]
            cv_scr[0:CONV_PAD - hist, :] = jnp.zeros((CONV_PAD - hist, 2 * ML_W), F32)
            cv_scr[CONV_PAD - hist:CONV_PAD, :] = cv0_ref[...]
        else:
            c_scr[...] = jnp.zeros_like(c_scr)
            m_scr[...] = jnp.zeros_like(m_scr)
            cv_scr[0:CONV_PAD, :] = jnp.zeros((CONV_PAD, 2 * ML_W), F32)

    conv = _causal_conv_silu(cv_scr, mqk_ref[...], cw_ref, cb_ref)
    qk_scr[:, 0:ML_W] = conv[:, 0:ML_W].astype(BF16)
    qk_scr[:, ML_W:] = (conv[:, ML_W:] * (ML_DH ** -0.5)).astype(BF16)

    gts = pif_ref[...] + bif_ref[...]
    flog = pltpu.roll(_log_sigmoid(gts), LANES - ML_HEADS, axis=1)

    n_chunks = t // c
    lane = lax.broadcasted_iota(jnp.int32, (t, LANES), 1)
    row_c = lax.broadcasted_iota(jnp.int32, (t, LANES), 0) % c
    causal = (lax.broadcasted_iota(jnp.int32, (c, c), 1) <= lax.broadcasted_iota(jnp.int32, (c, c), 0))
    ones_v = jnp.ones((c, ML_DH), BF16)

    def slots(pieces):
        out = jnp.zeros((t, LANES), F32)
        for j, piece in enumerate(pieces):
            if not isinstance(piece, float) and j > 0:
                piece = pltpu.roll(piece, ML_HEADS * j, axis=1)
            out = jnp.where((lane >= ML_HEADS * j) & (lane < ML_HEADS * (j + 1)), piece, out)
        return out

    def split3f(x):
        return [p.astype(F32) for p in _split3(x)]

    def per_chunk(rows_of):
        return jnp.concatenate([jnp.broadcast_to(rows_of(ci), (c, LANES)) for ci in range(n_chunks)], axis=0)

    bc = jnp.concatenate([_dot_exact_lhs(tri_ref[...], flog[ci * c:(ci + 1) * c])
                          for ci in range(n_chunks)], axis=0)
    w = gts - bc
    cm = w
    for j in range(int(math.log2(c))):
        sh = 1 << j
        cm = jnp.where(row_c >= sh, jnp.maximum(cm, pltpu.roll(cm, sh, axis=0)), cm)
    m_in = [m_scr[0:1, :]]
    for ci in range(n_chunks):
        last = slice(ci * c + c - 1, (ci + 1) * c)
        m_in.append(bc[last, :] + jnp.maximum(m_in[ci], cm[last, :]))
    m_scr[...] = jnp.broadcast_to(m_in[n_chunks], m_scr.shape)
    mprev = per_chunk(lambda ci: m_in[ci])
    g = jnp.maximum(mprev, cm)
    g_last = per_chunk(lambda ci: g[ci * c + c - 1:(ci + 1) * c, :])
    lhs_all = slots([1.0, 1.0, 1.0] + split3f(g))
    rhs_all = slots(split3f(w) + [-1.0, -1.0, -1.0]).astype(BF16)
    y_all = slots(split3f(mprev - g) + split3f(-(bc + g)) + split3f(w - g_last)).astype(BF16)
    lhs_heads = [jnp.where(lane % ML_HEADS == h, lhs_all, 0.0).astype(BF16) for h in range(ML_HEADS)]

    for ci in range(n_chunks):
        r0 = ci * c
        rows = pl.ds(r0, c)
        for h in range(ML_HEADS):
            hs = slice(h * ML_DH, (h + 1) * ML_DH)
            qh = qk_scr[rows, h * ML_DH:(h + 1) * ML_DH]
            kh = qk_scr[rows, ML_W + h * ML_DH:ML_W + (h + 1) * ML_DH]
            vaug = jnp.concatenate(
                [mvo_ref[rows, h * ML_DH:(h + 1) * ML_DH], ones_v],
                axis=1)
            p = jnp.exp(jnp.where(causal, _dot_nt(lhs_heads[h][r0:r0 + c], rhs_all[r0:r0 + c]), -jnp.inf))
            bx = jnp.exp(_dot(y_all[r0:r0 + c], sel_ref[h]))
            w_inter = bx[:, 0:ML_DH]
            e_mt = bx[:, ML_DH:2 * ML_DH]
            w_state = bx[:, 2 * ML_DH:3 * ML_DH]
            s = (_dot_nt(qh, kh) * p).astype(BF16)
            caug = c_scr[h]
            nd = _dot(s, vaug) + jnp.concatenate([w_inter, w_inter], axis=1) * _dot(qh, caug.astype(BF16))
            hh = nd[:, 0:ML_DH] / jnp.maximum(jnp.abs(nd[:, ML_DH:]), e_mt)
            o_gate = mvo_ref[rows, ML_W + h * ML_DH:ML_W + (h + 1) * ML_DH]
            o_ref[rows, hs] = (o_gate.astype(F32) * hh).astype(o_ref.dtype)
            ks = (kh.astype(F32) * w_state).astype(BF16)
            dec = w_inter[c - 1:c, :]
            c_scr[h] = jnp.concatenate([dec, dec], axis=1) * caug + _dot_tn(ks, vaug)

    @pl.when(step == pl.num_programs(1) - 1)
    def _():
        for h in range(ML_HEADS):
            cout_ref[h] = c_scr[h, :, 0:ML_DH]
            nout_ref[h] = c_scr[h, :, ML_DH:].T[0:8, :]
        mout_ref[...] = m_scr[...]
        cvout_ref[...] = mqk_ref[t - hist:t, :]


def _mlstm_call(mqk, mvo, pif, c0, n0, m0, cv0, cw, cb, bif, c, t):
    b, l, _ = mqk.shape
    has_state = c0 is not None
    tri = jnp.asarray(np.tril(np.ones((c, c), np.float32)), BF16)
    sel = np.zeros((ML_HEADS, LANES, 3 * ML_DH), np.float32)
    for h in range(ML_HEADS):
        for slot in range(9):
            sel[h, ML_HEADS * slot + h, (slot // 3) * ML_DH:(slot // 3 + 1) * ML_DH] = 1.0
    sel = jnp.asarray(sel, BF16)
    tile = lambda w: pl.BlockSpec((None, t, w), lambda bi, i: (bi, i, 0))
    heads = lambda *shape: pl.BlockSpec((None, ML_HEADS) + shape, lambda bi, i: (bi, 0, 0, 0))
    c_spec = heads(ML_DH, ML_DH)
    m_spec = pl.BlockSpec((None, 8, LANES), lambda bi, i: (bi, 0, 0))
    cv_spec = pl.BlockSpec((None, CONV_W - 1, 2 * ML_W), lambda bi, i: (bi, 0, 0))
    in_specs = [tile(2 * ML_W), tile(2 * ML_W), tile(LANES)]
    args = [mqk, mvo, pif]
    consts = [cw, cb, bif, tri, sel]
    out_specs = [tile(ML_W), c_spec, heads(8, ML_DH), m_spec, cv_spec]
    out_shape = [jax.ShapeDtypeStruct((b, l, ML_W), BF16),
                 jax.ShapeDtypeStruct((b, ML_HEADS, ML_DH, ML_DH), F32),
                 jax.ShapeDtypeStruct((b, ML_HEADS, 8, ML_DH), F32),
                 jax.ShapeDtypeStruct((b, 8, LANES), F32),
                 jax.ShapeDtypeStruct((b, CONV_W - 1, 2 * ML_W), F32)]
    scratch = [pltpu.VMEM((ML_HEADS, ML_DH, 2 * ML_DH), F32), pltpu.VMEM((8, LANES), F32),
               pltpu.VMEM((CONV_PAD + t, 2 * ML_W), F32), pltpu.VMEM((t, 2 * ML_W), BF16)]
    if has_state:
        in_specs += [c_spec, heads(1, ML_DH), m_spec, cv_spec]
        args += [c0, n0, m0, cv0]
    in_specs += [_full_spec(x.shape) for x in consts]
    return pl.pallas_call(
        functools.partial(_mlstm_kernel, c=c, t=t, has_state=has_state),
        grid=(b, l // t),
        in_specs=in_specs,
        out_specs=out_specs,
        out_shape=out_shape,
        scratch_shapes=scratch,
        compiler_params=pltpu.CompilerParams(
            dimension_semantics=("arbitrary", "arbitrary"), vmem_limit_bytes=VMEM_LIMIT),
        name="mlstm",
    )(*args, *consts)


def _merge_kernel(x_ref, ga_ref, hb_ref, pmg_ref, wug_ref, wum_ref, wo_ref, g2_ref,
                  wr_hi_ref, wr_lo_ref, br_ref, tril_ref, triu_ref, cnt0_ref,
                  x1_ref, hs_ref, meta_ref, tab_ref, cnt_scr):
    ya = _dot(ga_ref[...], wug_ref[...])
    yb = _dot(hb_ref[...], wum_ref[...])
    z = pmg_ref[:, 0:D_MODEL].astype(F32) * ya + pmg_ref[:, D_MODEL:].astype(F32) * yb
    x1 = x_ref[...] + _dot(z.astype(BF16), wo_ref[...])
    x1_ref[...] = x1
    hm = _rms(x1, g2_ref[...])
    hm_hi = hm.astype(BF16)
    hm_lo = (hm - hm_hi.astype(F32)).astype(BF16)
    hi_both = _dot(hm_hi, jnp.concatenate([wr_hi_ref[...], wr_lo_ref[...]], axis=1))
    logits = hi_both[:, 0:LANES] + hi_both[:, LANES:] + _dot(hm_lo, wr_hi_ref[...]) + br_ref[...]
    tm = logits.shape[0]
    n_out = -(-(N_EXPERTS + N_GROUPS) // 8) * 8
    lt = logits.T[0:n_out, :]
    out = lax.broadcasted_iota(jnp.int32, lt.shape, 0)
    neg = -jnp.inf
    is_g = (out >= N_EXPERTS) & (out < N_EXPERTS + N_GROUPS)
    lg = jnp.where(is_g, lt, neg)
    mg = jnp.max(lg, axis=0, keepdims=True)
    p_top = 1.0 / jnp.sum(jnp.exp(lg - mg), axis=0, keepdims=True)
    gi = jnp.min(jnp.where(lg == mg, out, 2 * LANES), axis=0, keepdims=True) - N_EXPERTS
    group_shift = int(math.log2(EXPERTS_PER_GROUP))
    sel = (out < N_EXPERTS) & (jnp.right_shift(out, group_shift) == gi)
    le = jnp.where(sel, lt, neg)
    v1 = jnp.max(le, axis=0, keepdims=True)
    i1 = jnp.min(jnp.where(le == v1, out, 2 * LANES), axis=0, keepdims=True)
    le2 = jnp.where(out == i1, neg, le)
    v2 = jnp.max(le2, axis=0, keepdims=True)
    i2 = jnp.min(jnp.where(le2 == v2, out, 2 * LANES), axis=0, keepdims=True)
    e2 = jnp.exp(v2 - v1)
    per_token = jnp.concatenate(
        [i1.astype(F32), i2.astype(F32), p_top * (1.0 / (1.0 + e2)), p_top * (e2 / (1.0 + e2)),
         jnp.zeros((LANES - 4, tm), F32)], axis=0).T
    i1 = per_token[:, 0:1].astype(jnp.int32)
    i2 = per_token[:, 1:2].astype(jnp.int32)
    w1 = per_token[:, 2:3]
    w2 = per_token[:, 3:4]
    lane = lax.broadcasted_iota(jnp.int32, logits.shape, 1)

    @pl.when(pl.program_id(0) == 0)
    def _():
        cnt_scr[...] = cnt0_ref[...]

    tm = logits.shape[0]
    oh1 = lane == i1
    oh2 = lane == i2
    both = jnp.where(oh1 | oh2, 1.0, 0.0)
    cnt = jnp.sum(both, axis=0, keepdims=True)
    cnt = jnp.floor((cnt + (SEG_ALIGN - 1)) * (1.0 / SEG_ALIGN)) * SEG_ALIGN
    lower = _dot(jnp.broadcast_to(cnt, (8, LANES)).astype(BF16), triu_ref[...])[0:1, :]
    lpos = _dot(tril_ref[...], both.astype(BF16)) + lower
    lp1 = jnp.sum(jnp.where(oh1, lpos, 0.0), axis=-1, keepdims=True)
    lp2 = jnp.sum(jnp.where(oh2, lpos, 0.0), axis=-1, keepdims=True)
    pos = lax.broadcasted_iota(jnp.int32, (tm, hs_ref.shape[0]), 1)
    onehot = jnp.where((pos == lp1.astype(jnp.int32)) | (pos == lp2.astype(jnp.int32)), 1.0, 0.0)
    hs_ref[...] = _dot_tn(onehot.astype(BF16), hm_hi).astype(BF16)
    cols = (lp1, lp2, w1, w2)
    meta = jnp.zeros((tm, LANES), F32)
    for ci, col in enumerate(cols):
        meta = jnp.where(lane == ci, col, meta)
    meta_ref[...] = meta
    row8 = lax.broadcasted_iota(jnp.int32, (8, LANES), 0)
    tab_ref[...] = jnp.where(row8 == 0, cnt, jnp.where(row8 == 1, lower, jnp.where(row8 == 2, cnt_scr[...], 0.0)))
    cnt_scr[...] += cnt


def _merge_call(x2, ga, hb, pmg, wug, wum, wo, g2, wr_hi, wr_lo, br, cnt0, tm):
    n = x2.shape[0]
    tile = lambda w: pl.BlockSpec((tm, w), lambda i: (i, 0))
    tiles = lambda *shape: pl.BlockSpec((None,) + shape, lambda i: (i, 0, 0))
    local_rows = 2 * tm + SEG_ALIGN * N_EXPERTS
    consts = [wug, wum, wo, g2, wr_hi, wr_lo, br,
              jnp.asarray(np.tril(np.ones((tm, tm), np.float32), -1), BF16),
              jnp.asarray(np.triu(np.ones((LANES, LANES), np.float32), 1), BF16),
              cnt0]
    return pl.pallas_call(
        _merge_kernel,
        grid=(n // tm,),
        in_specs=[tile(D_MODEL), tile(GLA_V), tile(ML_W), tile(W_MG)]
                 + [_full_spec(x.shape) for x in consts],
        out_specs=[tile(D_MODEL), tiles(local_rows, D_MODEL), tile(LANES), tiles(8, LANES)],
        out_shape=[jax.ShapeDtypeStruct((n, D_MODEL), F32),
                   jax.ShapeDtypeStruct((n // tm, local_rows, D_MODEL), BF16),
                   jax.ShapeDtypeStruct((n, LANES), F32),
                   jax.ShapeDtypeStruct((n // tm, 8, LANES), F32)],
        scratch_shapes=[pltpu.VMEM((1, LANES), F32)],
        compiler_params=pltpu.CompilerParams(
            dimension_semantics=("arbitrary",), vmem_limit_bytes=VMEM_LIMIT),
        name="merge",
    )(x2, ga, hb, pmg, *consts)


MOE_ROWS = 512
SEG_ALIGN = 16


def _segment_copies(fn, tables, tile, local_ref, buffer, flat_ref, sem, to_flat):
    pieces_ref, gdst_ref = tables
    stride = gdst_ref.shape[0] // pieces_ref.shape[0]

    def body(p, carry):
        local = local_ref.at[buffer, pl.ds(pl.multiple_of(p * SEG_ALIGN, SEG_ALIGN), SEG_ALIGN), :]
        flat = flat_ref.at[pl.ds(pl.multiple_of(gdst_ref[tile * stride + p], SEG_ALIGN), SEG_ALIGN), :]
        src, dst = (local, flat) if to_flat else (flat, local)
        fn(pltpu.make_async_copy(src, dst, sem))
        return carry

    lax.fori_loop(0, pieces_ref[tile], body, 0)


def _gmm_kernel(te_ref, nv_ref, part_ref, tile_ref, row_ref, *refs, n_parts):
    hs_refs = refs[:n_parts]
    wg_ref, wu_ref, wd_ref, o_ref, x_buf, sem, wgu_scr, wd_scr = refs[n_parts:]
    j = pl.program_id(0)
    n_valid = nv_ref[0]
    used = j < n_valid
    pieces_per_tile = MOE_ROWS // SEG_ALIGN

    def fetch(fn, tile):
        slot = tile % 2
        for k in range(pieces_per_tile):
            g = tile * pieces_per_tile + k
            for part, hs_ref in enumerate(hs_refs):
                @pl.when(part_ref[g] == part)
                def _():
                    src = hs_ref.at[tile_ref[g], pl.ds(pl.multiple_of(row_ref[g], SEG_ALIGN), SEG_ALIGN), :]
                    dst = x_buf.at[slot, pl.ds(k * SEG_ALIGN, SEG_ALIGN), :]
                    fn(pltpu.make_async_copy(src, dst, sem.at[slot]))

    @pl.when(j == 0)
    def _():
        x_buf[...] = jnp.zeros_like(x_buf)
        fetch(lambda c: c.start(), j)

    @pl.when(j + 1 < n_valid)
    def _():
        fetch(lambda c: c.start(), j + 1)

    @pl.when(used & ((j == 0) | (te_ref[j] != te_ref[jnp.maximum(j - 1, 0)])))
    def _():
        wgu_scr[:, 0:D_EXPERT] = wg_ref[...].astype(BF16)
        wgu_scr[:, D_EXPERT:] = wu_ref[...].astype(BF16)
        wd_scr[...] = wd_ref[...].astype(BF16)

    @pl.when(used)
    def _():
        fetch(lambda c: c.wait(), j)
        au = _dot(x_buf[j % 2], wgu_scr[...])
        a = au[:, 0:D_EXPERT]
        hh = (a * _sigmoid(a)) * au[:, D_EXPERT:]
        o_ref[...] = _dot(hh.astype(BF16), wd_scr[...]).astype(o_ref.dtype)

    @pl.when(jnp.logical_not(used))
    def _():
        o_ref[...] = jnp.zeros_like(o_ref)


def _gmm_call(tile_expert, n_valid, sources, hs_parts, wg, wu, wd):
    n_tiles = tile_expert.shape[0]
    wsel = lambda j, te, *_: (te[j], 0, 0)
    any_spec = pl.BlockSpec(memory_space=pl.ANY)
    return pl.pallas_call(
        functools.partial(_gmm_kernel, n_parts=len(hs_parts)),
        grid_spec=pltpu.PrefetchScalarGridSpec(
            num_scalar_prefetch=5,
            grid=(n_tiles,),
            in_specs=[any_spec] * len(hs_parts)
                     + [pl.BlockSpec((None, D_MODEL, D_EXPERT), wsel),
                        pl.BlockSpec((None, D_MODEL, D_EXPERT), wsel),
                        pl.BlockSpec((None, D_EXPERT, D_MODEL), wsel)],
            out_specs=pl.BlockSpec((MOE_ROWS, D_MODEL), lambda j, *_: (j, 0)),
            scratch_shapes=[pltpu.VMEM((2, MOE_ROWS, D_MODEL), BF16),
                            pltpu.SemaphoreType.DMA((2,)),
                            pltpu.VMEM((D_MODEL, 2 * D_EXPERT), BF16),
                            pltpu.VMEM((D_EXPERT, D_MODEL), BF16)],
        ),
        out_shape=jax.ShapeDtypeStruct((n_tiles * MOE_ROWS, D_MODEL), BF16),
        compiler_params=pltpu.CompilerParams(
            dimension_semantics=("arbitrary",), vmem_limit_bytes=VMEM_LIMIT),
        name="moe_grouped",
    )(tile_expert, n_valid, *sources, *hs_parts, wg, wu, wd)


def _combine_kernel(pieces_ref, gdst_ref, x1_ref, meta_ref, gf_ref, os_ref, y_ref, buf_ref, sem, *, tile0):
    step = pl.program_id(0)
    n_steps = pl.num_programs(0)
    tm = x1_ref.shape[0]
    tables = (pieces_ref, gdst_ref)

    def fetch(fn, i):
        slot = i % 2
        _segment_copies(fn, tables, tile0 + i, buf_ref, slot, os_ref, sem.at[slot], False)

    @pl.when(step == 0)
    def _():
        buf_ref[...] = jnp.zeros_like(buf_ref)
        fetch(lambda c: c.start(), step)

    @pl.when(step + 1 < n_steps)
    def _():
        fetch(lambda c: c.start(), step + 1)

    fetch(lambda c: c.wait(), step)
    rows = buf_ref[step % 2]
    pos = lax.broadcasted_iota(jnp.int32, (tm, rows.shape[0]), 1)
    lp1 = meta_ref[:, 0:1].astype(jnp.int32)
    lp2 = meta_ref[:, 1:2].astype(jnp.int32)
    q = jnp.where(pos == lp1, meta_ref[:, 2:3], 0.0) + jnp.where(pos == lp2, meta_ref[:, 3:4], 0.0)
    y = x1_ref[...] + _dot(q.astype(BF16), rows)
    y_ref[...] = _rms(y, gf_ref[...])


def _combine_call(tables, x1, meta, gf, out_sorted, tile0, local_rows, tm):
    n = x1.shape[0]
    tile = lambda w: pl.BlockSpec((tm, w), lambda i, *_: (i, 0))
    return pl.pallas_call(
        functools.partial(_combine_kernel, tile0=tile0),
        grid_spec=pltpu.PrefetchScalarGridSpec(
            num_scalar_prefetch=2,
            grid=(n // tm,),
            in_specs=[tile(D_MODEL), tile(LANES), pl.BlockSpec(gf.shape, lambda i, *_: (0, 0)),
                      pl.BlockSpec(memory_space=pl.ANY)],
            out_specs=tile(D_MODEL),
            scratch_shapes=[pltpu.VMEM((2, local_rows, D_MODEL), out_sorted.dtype),
                            pltpu.SemaphoreType.DMA((2,))],
        ),
        out_shape=jax.ShapeDtypeStruct((n, D_MODEL), F32),
        compiler_params=pltpu.CompilerParams(
            dimension_semantics=("arbitrary",), vmem_limit_bytes=VMEM_LIMIT),
        name="moe_combine",
    )(*tables, x1, meta, gf, out_sorted)


def _sparse_moe(parts, p):
    token_tiles = [part[1].shape[0] for part in parts]
    n_tiles = -(-sum(part[1].shape[0] * part[1].shape[1] for part in parts) // MOE_ROWS) + N_EXPERTS
    tab = jnp.concatenate([part[3] for part in parts], axis=0)
    tab = tab[:, :, :N_EXPERTS].astype(jnp.int32)
    cnt, before = tab[:, 0], tab[:, 2]
    total = before[-1] + cnt[-1]
    tiles = (total + (MOE_ROWS - 1)) // MOE_ROWS
    tile_end = jnp.cumsum(tiles)
    tile_start = tile_end - tiles
    n_valid = tile_end[-1:]
    gpos = tile_start[None, :] * MOE_ROWS + before
    j = jnp.minimum(jnp.arange(n_tiles, dtype=jnp.int32), n_valid - 1)
    tile_expert = jnp.sum((j[:, None] >= tile_end[None, :]).astype(jnp.int32), axis=1)
    max_pieces = max(part[1].shape[1] for part in parts) // SEG_ALIGN
    pieces = cnt // SEG_ALIGN
    piece_end = jnp.cumsum(pieces, axis=1)
    piece_start = piece_end - pieces
    piece = jnp.arange(max_pieces, dtype=jnp.int32)
    expert_of = jnp.sum((piece[None, :, None] >= piece_end[:, None, :]).astype(jnp.int32), axis=2)
    onehot = expert_of[:, :, None] == jnp.arange(N_EXPERTS, dtype=jnp.int32)
    first_row = gpos - SEG_ALIGN * piece_start
    gdst = jnp.sum(jnp.where(onehot, first_row[:, None, :], 0), axis=2) + SEG_ALIGN * piece[None, :]
    tables = (piece_end[:, -1], gdst.reshape(-1))
    n_token_tiles = sum(token_tiles)
    over_tiles = jnp.cumsum(pieces, axis=0)
    e_j = tile_expert
    q0 = (jnp.arange(n_tiles, dtype=jnp.int32) - jnp.take(tile_start, e_j)) * (MOE_ROWS // SEG_ALIGN)
    q = q0[:, None] + jnp.arange(MOE_ROWS // SEG_ALIGN, dtype=jnp.int32)[None, :]
    valid = (jnp.arange(n_tiles)[:, None] < n_valid) & (q < jnp.take(over_tiles[-1], e_j)[:, None])
    per_tile = lambda table: jnp.take(table.T, e_j, axis=0)[:, None, :]
    src_tile = jnp.sum((q[:, :, None] >= per_tile(over_tiles)).astype(jnp.int32), axis=2)
    src_tile = jnp.minimum(src_tile, n_token_tiles - 1)
    at_src = lambda table: jnp.sum(jnp.where(
        src_tile[:, :, None] == jnp.arange(n_token_tiles, dtype=jnp.int32), per_tile(table), 0), axis=2)
    src_row = (at_src(piece_start) + q - at_src(over_tiles - pieces)) * SEG_ALIGN
    part_start = np.cumsum([0] + token_tiles)
    src_part = jnp.sum((src_tile[:, :, None] >= jnp.asarray(part_start[1:], jnp.int32)).astype(jnp.int32), axis=2)
    src_tile_in_part = src_tile - jnp.sum(jnp.where(
        src_part[:, :, None] == jnp.arange(len(parts), dtype=jnp.int32), jnp.asarray(part_start[:-1], jnp.int32), 0), axis=2)
    sources = (jnp.where(valid, src_part, len(parts)).reshape(-1), src_tile_in_part.reshape(-1), src_row.reshape(-1))
    out_sorted = _gmm_call(tile_expert, n_valid.astype(jnp.int32), sources, [part[1] for part in parts],
                           p["wg"], p["wu"], p["wd"])
    return [_combine_call(tables, part[0], part[2], p["gf"], out_sorted, int(tile0),
                          part[1].shape[1], part[0].shape[0] // part[1].shape[0])
            for part, tile0 in zip(parts, part_start[:-1])]


def _pad_cols(w, width):
    return jnp.pad(w, ((0, 0), (0, width - w.shape[1])))


def _prep_weights(norm1_g, w_in, gla_w_gate2, gla_b_gate, gla_norm_g, w_up_gla,
                  ml_conv_w, ml_conv_b, ml_b_i, ml_b_f, w_up_ml, w_out,
                  norm2_g, router_g_w, router_g_b, router_e_w, router_e_b,
                  moe_w_gate, moe_w_up, moe_w_down, final_g):
    wr =_pad_cols(jnp.concatenate([router_e_w, router_g_w], axis=1), LANES)
    wr_hi = wr.astype(BF16)
    wr_lo = (wr - wr_hi.astype(F32)).astype(BF16)
    br = _pad_cols(jnp.concatenate([router_e_b, router_g_b])[None, :], LANES)
    wg2 = jnp.pad(gla_w_gate2, ((0, LANES - GLA_GATE_RANK), (0, 0)))
    wg2_hi = wg2.astype(BF16)
    return dict(
        g1=norm1_g[None, :], w_in=w_in.T,
        wg2_p=jnp.stack([wg2_hi, (wg2 - wg2_hi.astype(F32)).astype(BF16)]),
        bg=gla_b_gate[None, :], gn=gla_norm_g[None, :],
        wug=w_up_gla.astype(BF16),
        cw=ml_conv_w, cb=ml_conv_b[None, :],
        bif=_pad_cols(jnp.concatenate([ml_b_i, ml_b_f])[None, :], LANES),
        wum=w_up_ml.astype(BF16), wo=w_out.astype(BF16),
        g2=norm2_g[None, :], wr_hi=wr_hi, wr_lo=wr_lo, br=br,
        wg=moe_w_gate.reshape(N_EXPERTS, D_MODEL, D_EXPERT),
        wu=moe_w_up.reshape(N_EXPERTS, D_MODEL, D_EXPERT),
        wd=moe_w_down.reshape(N_EXPERTS, D_EXPERT, D_MODEL),
        gf=final_g[None, :],
    )


def _mixers(x, gla_s0, ml_c0, ml_n0, ml_m0, conv0, p, placed, *,
            gla_chunk, ml_chunk, seq_tile, row_tile, merge_tile):
    b, l, _ = x.shape
    n = b * l
    x2 = x.reshape(n, D_MODEL)
    qk, vr, plr, mqk, mvo, pif, pmg = _proj_call(x2, p["g1"], p["w_in"], row_tile)
    r3 = lambda a: a.reshape(b, l, a.shape[-1])
    s0 = None if gla_s0 is None else gla_s0.reshape(b, GLA_QK, GLA_DV)
    ga, gla_s = _gla_call(r3(qk), r3(vr), r3(plr), s0, p["wg2_p"], p["bg"], p["gn"], gla_chunk, seq_tile)
    gla_s = gla_s.reshape(b, GLA_HEADS, GLA_DK, GLA_DV)
    if ml_c0 is None:
        n0 = m0 = None
    else:
        n0 = ml_n0[:, :, None, :]
        m0 = jnp.broadcast_to(_pad_cols(ml_m0, LANES)[:, None, :], (b, 8, LANES))
    hb, ml_c, ml_n, m_b, new_conv = _mlstm_call(r3(mqk), r3(mvo), r3(pif), ml_c0, n0, m0, conv0,
                                                p["cw"], p["cb"], p["bif"], ml_chunk, seq_tile)
    part = _merge_call(x2, ga.reshape(n, GLA_V), hb.reshape(n, ML_W), pmg,
                       p["wug"], p["wum"], p["wo"], p["g2"],
                       p["wr_hi"], p["wr_lo"], p["br"], placed, merge_tile)
    states = (gla_s[None], ml_c[None], ml_n[:, :, 0, :][None], m_b[:, 0, 0:ML_HEADS][None], new_conv[None])
    return part, states


def kernel(x_prompt, x_sample, state_gla_S, state_mlstm_C, state_mlstm_n, state_mlstm_m, state_mlstm_conv, norm1_g, w_in, gla_w_gate2, gla_b_gate, gla_norm_g, w_up_gla, ml_conv_w, ml_conv_b, ml_b_i, ml_b_f, w_up_ml, w_out, norm2_g, router_g_w, router_g_b, router_e_w, router_e_b, moe_w_gate, moe_w_up, moe_w_down, final_g):
    assert norm1_g.shape[0] == 1, "single-layer trunk"
    p = _prep_weights(norm1_g[0], w_in[0], gla_w_gate2[0], gla_b_gate[0], gla_norm_g[0], w_up_gla[0],
                      ml_conv_w[0], ml_conv_b[0], ml_b_i[0], ml_b_f[0], w_up_ml[0], w_out[0],
                      norm2_g[0], router_g_w[0], router_g_b[0], router_e_w[0], router_e_b[0],
                      moe_w_gate[0], moe_w_up[0], moe_w_down[0], final_g)
    dec_seq = x_sample.shape[1]
    n_sample = x_sample.shape[0] * dec_seq
    part_p, sp = _mixers(x_prompt, None, None, None, None, None, p, jnp.zeros((1, LANES), F32),
                         gla_chunk=128, ml_chunk=256, seq_tile=1024, row_tile=256, merge_tile=512)
    placed = part_p[3][-1, 0:1, :] + part_p[3][-1, 2:3, :]
    part_s, ss = _mixers(x_sample, state_gla_S[0], state_mlstm_C[0], state_mlstm_n[0], state_mlstm_m[0],
                         state_mlstm_conv[0], p, placed,
                         gla_chunk=dec_seq, ml_chunk=dec_seq, seq_tile=dec_seq,
                         row_tile=n_sample, merge_tile=n_sample)
    yp, ys = _sparse_moe([part_p, part_s], p)
    return (yp.reshape(x_prompt.shape), ys.reshape(x_sample.shape), *sp, *ss)
```

```python
import functools
import math

import numpy as np
import jax
import jax.numpy as jnp
from jax import lax
from jax.experimental import pallas as pl
from jax.experimental.pallas import tpu as pltpu

D_MODEL = 1024
GLA_HEADS = 4
GLA_DK = 64
GLA_DV = 128
GLA_GATE_RANK = 16
GLA_TAU = 16.0
ML_HEADS = 4
ML_DH = 128
CONV_W = 4
N_GROUPS = 4
EXPERTS_PER_GROUP = 8
N_EXPERTS = N_GROUPS * EXPERTS_PER_GROUP
D_EXPERT = 256
EPS = 1e-6

GLA_QK = GLA_HEADS * GLA_DK
GLA_V = GLA_HEADS * GLA_DV
ML_W = ML_HEADS * ML_DH

LANES = 128
VMEM_LIMIT = 56 * 1024 * 1024

W_GLA = 2 * GLA_QK + 2 * GLA_V
W_ML = 2 * ML_W + ML_W + ML_W
W_MG = 2 * D_MODEL
PROJ_WIDTHS = (W_GLA, LANES, W_ML, LANES, W_MG)
PROJ_SOURCE_WIDTHS = (W_GLA, GLA_GATE_RANK, W_ML, 2 * ML_HEADS, W_MG)

F32 = jnp.float32
BF16 = jnp.bfloat16

PROJ_OUTPUTS = ((2 * GLA_QK, F32), (2 * GLA_V, BF16), (LANES, F32), (2 * ML_W, F32), (2 * ML_W, BF16),
                (LANES, F32), (W_MG, BF16))


def _dot(a, b):
    return jnp.dot(a, b, preferred_element_type=F32)


def _dot_nt(a, b):
    return lax.dot_general(a, b, (((1,), (1,)), ((), ())), preferred_element_type=F32)


def _dot_tn(a, b):
    return lax.dot_general(a, b, (((0,), (0,)), ((), ())), preferred_element_type=F32)


def _split3(x):
    hi = x.astype(BF16)
    r1 = x - hi.astype(F32)
    mid = r1.astype(BF16)
    lo = (r1 - mid.astype(F32)).astype(BF16)
    return hi, mid, lo


def _dot_exact_lhs(m, x):
    hi, mid, lo = _split3(x)
    return _dot(m, hi) + _dot(m, mid) + _dot(m, lo)


def _log_sigmoid(z):
    return jnp.minimum(z, 0.0) - jnp.log(1.0 + jnp.exp(-jnp.abs(z)))


def _sigmoid(z):
    return 1.0 / (1.0 + jnp.exp(-z))


def _rms(x, g):
    return x * lax.rsqrt(jnp.mean(x * x, axis=-1, keepdims=True) + EPS) * g


def _full_spec(shape):
    nd = len(shape)
    return pl.BlockSpec(shape, lambda *_: (0,) * nd)


CONV_PAD = 8


def _causal_conv_silu(stage_ref, x, cw_ref, cb_ref):
    t = x.shape[0]
    stage_ref[CONV_PAD:CONV_PAD + t, :] = x
    acc = cb_ref[...] + stage_ref[CONV_PAD:CONV_PAD + t, :] * cw_ref[CONV_W - 1:CONV_W, :]
    for d in range(1, CONV_W):
        acc = acc + stage_ref[CONV_PAD - d:CONV_PAD - d + t, :] * cw_ref[CONV_W - 1 - d:CONV_W - d, :]
    stage_ref[0:CONV_PAD, :] = stage_ref[t:t + CONV_PAD, :]
    return acc * _sigmoid(acc)


def _proj_kernel(x_ref, g_ref, win_ref, qk_ref, vr_ref, lr_ref, mqk_ref, mvo_ref, if_ref, mg_ref, w_ref):
    starts = np.cumsum((0,) + PROJ_WIDTHS)

    @pl.when(pl.program_id(0) == 0)
    def _():
        src = np.cumsum((0,) + PROJ_SOURCE_WIDTHS)
        chunk = 512
        for g, width in enumerate(PROJ_SOURCE_WIDTHS):
            for r0 in range(0, width, chunk):
                rows = min(chunk, width - r0)
                w_ref[starts[g] + r0:starts[g] + r0 + rows, :] = (
                    win_ref[src[g] + r0:src[g] + r0 + rows, :].astype(BF16))
            if width < PROJ_WIDTHS[g]:
                w_ref[starts[g] + width:starts[g + 1], :] = jnp.zeros((PROJ_WIDTHS[g] - width, D_MODEL), BF16)

    h = _rms(x_ref[...], g_ref[...]).astype(BF16)

    def cols(group, lo, hi):
        return _dot_nt(h, w_ref[starts[group] + lo:starts[group] + hi, :])

    qk_ref[...] = cols(0, 0, 2 * GLA_QK)
    vr_ref[:, 0:GLA_V] = cols(0, 2 * GLA_QK, 2 * GLA_QK + GLA_V).astype(BF16)
    r = cols(0, 2 * GLA_QK + GLA_V, W_GLA)
    vr_ref[:, GLA_V:] = (r * _sigmoid(r)).astype(BF16)
    lr_ref[...] = cols(1, 0, LANES)
    mqk_ref[...] = cols(2, 0, 2 * ML_W)
    mvo_ref[:, 0:ML_W] = cols(2, 2 * ML_W, 3 * ML_W).astype(BF16)
    mvo_ref[:, ML_W:] = _sigmoid(cols(2, 3 * ML_W, W_ML)).astype(BF16)
    if_ref[...] = cols(3, 0, LANES)
    mg_ref[...] = _sigmoid(cols(4, 0, W_MG)).astype(BF16)


def _proj_call(x2, g, w_in, tm):
    n = x2.shape[0]
    assert w_in.shape == (sum(PROJ_SOURCE_WIDTHS), D_MODEL)
    return pl.pallas_call(
        _proj_kernel,
        grid=(n // tm,),
        in_specs=[pl.BlockSpec((tm, D_MODEL), lambda i: (i, 0)),
                  _full_spec(g.shape),
                  pl.BlockSpec(w_in.shape, lambda i: (0, 0), pipeline_mode=pl.Buffered(1))],
        out_specs=[pl.BlockSpec((tm, w), lambda i: (i, 0)) for w, _ in PROJ_OUTPUTS],
        out_shape=[jax.ShapeDtypeStruct((n, w), dt) for w, dt in PROJ_OUTPUTS],
        scratch_shapes=[pltpu.VMEM((sum(PROJ_WIDTHS), D_MODEL), BF16)],
        compiler_params=pltpu.CompilerParams(
            dimension_semantics=("arbitrary",), vmem_limit_bytes=VMEM_LIMIT),
        name="in_proj",
    )(x2, g, w_in)


def _gla_consts(c):
    nlev = int(math.log2(c))
    assert 1 << nlev == c
    t = np.arange(c)[:, None]
    j = np.arange(c)[None, :]
    lv = np.full((c, c), -1, np.int32)
    for l in range(nlev):
        h = c >> (l + 1)
        upper = (t % (2 * h)) >= h
        same = (j // (2 * h)) == (t // (2 * h))
        s_lower = (j % (2 * h)) < h
        lv[np.broadcast_to(upper, (c, c)) & same & s_lower] = l
    lv[np.eye(c, dtype=bool)] = nlev
    tri = (j <= t).astype(np.float32)
    return jnp.asarray(tri, BF16), jnp.asarray(np.concatenate([lv, lv], axis=1))


def _gla_kernel(*refs, c, t, has_state, streams):
    if has_state:
        (qk_ref, vr_ref, plr_ref, s0_ref, wg2_ref, bg_ref, gn_ref, tri_ref, lv_ref,
         o_ref, sout_ref, s_scr) = refs
    else:
        (qk_ref, vr_ref, plr_ref, wg2_ref, bg_ref, gn_ref, tri_ref, lv_ref,
         o_ref, sout_ref, s_scr) = refs
    nlev = int(math.log2(c))
    step = pl.program_id(1)

    if not streams:
        @pl.when(step == 0)
        def _():
            if has_state:
                s_scr[...] = s0_ref[...]
            else:
                s_scr[...] = jnp.zeros_like(s_scr)

    n_chunks = t // c
    lane_k = lax.broadcasted_iota(jnp.int32, (t, GLA_QK), 1)
    first_of_pair = (lane_k % (2 * GLA_DK)) < GLA_DK
    row_k = lax.broadcasted_iota(jnp.int32, (GLA_QK, GLA_DV), 0)
    row_t = lax.broadcasted_iota(jnp.int32, (t, GLA_QK), 0)

    def block_ref(b, blk, idx):
        if blk >= 8:
            b3 = b.reshape(t // blk, blk, GLA_QK)
            return jnp.broadcast_to(b3[:, idx:idx + 1, :], b3.shape).reshape(t, GLA_QK)
        r = row_t % blk
        out = b
        for sh in range(-idx, blk - idx):
            if sh != 0:
                out = jnp.where(r - idx == sh, pltpu.roll(b, sh % t, axis=0), out)
        return out

    q = qk_ref[:, 0:GLA_QK] * (GLA_DK ** -0.5)
    k = qk_ref[:, GLA_QK:2 * GLA_QK]
    glr = plr_ref[...]
    g_hi = glr.astype(BF16)
    g_lo = (glr - g_hi.astype(F32)).astype(BF16)
    z = (_dot(g_hi, wg2_ref[0]) + _dot(g_lo, wg2_ref[0]) + _dot(g_hi, wg2_ref[1])) + bg_ref[...]
    la = _log_sigmoid(z) * (1.0 / GLA_TAU)
    b = jnp.concatenate([_dot_exact_lhs(tri_ref[...], la[ci * c:(ci + 1) * c])
                         for ci in range(n_chunks)], axis=0)
    b_last = block_ref(b, c, c - 1)
    qe = (q * jnp.exp(b)).astype(BF16)
    kl = (k * jnp.exp(b_last - b)).astype(BF16)

    k_a = jnp.where(first_of_pair, k, 0.0)
    k_b = k - k_a
    lv2 = lv_ref[...]
    factors = []
    for l in range(nlev + 1):
        if l < nlev:
            half = c >> (l + 1)
            d = b - block_ref(b, 2 * half, half - 1)
            e = jnp.exp(jnp.minimum(d, -d))
            qt, kta, ktb = q * e, k_a * e, k_b * e
        else:
            qt, kta, ktb = q, k_a, k_b
        factors.append((qt.astype(BF16), kta.astype(BF16), ktb.astype(BF16)))
    a = [[None] * (GLA_HEADS // 2) for _ in range(n_chunks)]
    for ci in range(n_chunks):
        rows = slice(ci * c, (ci + 1) * c)
        for pr in range(GLA_HEADS // 2):
            ls = slice(pr * 2 * GLA_DK, (pr + 1) * 2 * GLA_DK)
            acc = jnp.zeros((c, 2 * c), F32)
            for l, (qt, kta, ktb) in enumerate(factors):
                rhs = jnp.concatenate([kta[rows, ls], ktb[rows, ls]], axis=0)
                acc = jnp.where(lv2 == l, _dot_nt(qt[rows, ls], rhs), acc)
            a[ci][pr] = acc

    for ci in range(n_chunks):
        rows = slice(ci * c, (ci + 1) * c)
        v = vr_ref[rows, 0:GLA_V]
        s_in, s_out = (s0_ref.at[ci], sout_ref.at[ci]) if streams else (s_scr, s_scr)
        s_all = s_in[...]
        s_bd = jnp.concatenate(
            [jnp.where((row_k // GLA_DK) == h, s_all, 0.0).astype(BF16) for h in range(GLA_HEADS)], axis=1)
        o_inter = _dot(qe[rows], s_bd)
        u_all = _dot_tn(kl[rows], v)
        dcol = jnp.exp(jnp.broadcast_to(b[ci * c + c - 1:(ci + 1) * c, :], (LANES, GLA_QK)).T)
        for h in range(GLA_HEADS):
            vs = slice(h * GLA_DV, (h + 1) * GLA_DV)
            ks = slice(h * GLA_DK, (h + 1) * GLA_DK)
            a_h = a[ci][h // 2][:, (h % 2) * c:(h % 2 + 1) * c]
            o = _dot(a_h.astype(BF16), v[:, vs]) + o_inter[:, vs]
            on = _rms(o, gn_ref[:, vs])
            gate = vr_ref[rows, GLA_V + h * GLA_DV:GLA_V + (h + 1) * GLA_DV]
            o_ref[rows, vs] = (on * gate.astype(F32)).astype(o_ref.dtype)
            s_out[ks, :] = dcol[ks, :] * s_all[ks, :] + u_all[ks, vs]

    if not streams:
        @pl.when(step == pl.num_programs(1) - 1)
        def _():
            sout_ref[...] = s_scr[...]


def _gla_call(qk, vr, plr, s0, wg2_p, bg, gn, c, t):
    seqs, seq_len, _ = qk.shape
    tri, lv2 = _gla_consts(c)
    has_state = s0 is not None
    streams = has_state and seq_len == c
    if streams:
        qk, vr, plr = (a.reshape(1, seqs * seq_len, a.shape[-1]) for a in (qk, vr, plr))
        t = seqs * seq_len
    b, l, _ = qk.shape
    tile = lambda w: pl.BlockSpec((None, t, w), lambda bi, i: (bi, i, 0))
    if streams:
        state_spec = pl.BlockSpec((seqs, GLA_QK, GLA_DV), lambda bi, i: (0, 0, 0))
    else:
        state_spec = pl.BlockSpec((None, GLA_QK, GLA_DV), lambda bi, i: (bi, 0, 0))
    in_specs = [tile(2 * GLA_QK), tile(2 * GLA_V), tile(LANES)]
    args = [qk, vr, plr]
    if has_state:
        in_specs.append(state_spec)
        args.append(s0)
    consts = [wg2_p, bg, gn, tri, lv2]
    in_specs += [_full_spec(x.shape) for x in consts]
    out, state = pl.pallas_call(
        functools.partial(_gla_kernel, c=c, t=t, has_state=has_state, streams=streams),
        grid=(b, l // t),
        in_specs=in_specs,
        out_specs=[tile(GLA_V), state_spec],
        out_shape=[jax.ShapeDtypeStruct((b, l, GLA_V), BF16),
                   jax.ShapeDtypeStruct((seqs, GLA_QK, GLA_DV), F32)],
        scratch_shapes=[pltpu.VMEM((GLA_QK, GLA_DV), F32)],
        compiler_params=pltpu.CompilerParams(
            dimension_semantics=("arbitrary", "arbitrary"), vmem_limit_bytes=VMEM_LIMIT),
        name="gla",
    )(*args, *consts)
    return out.reshape(seqs, seq_len, GLA_V), state


def _mlstm_kernel(*refs, c, t, has_state, streams):
    if has_state:
        (mqk_ref, mvo_ref, pif_ref, c0_ref, n0_ref, m0_ref, cv0_ref, cw_ref, cb_ref, bif_ref, tri_ref, sel_ref,
         o_ref, cout_ref, nout_ref, mout_ref, cvout_ref, c_scr, m_scr, cv_scr, qk_scr) = refs
    else:
        (mqk_ref, mvo_ref, pif_ref, cw_ref, cb_ref, bif_ref, tri_ref, sel_ref,
         o_ref, cout_ref, nout_ref, mout_ref, cvout_ref, c_scr, m_scr, cv_scr, qk_scr) = refs
    step = pl.program_id(1)
    hist = CONV_W - 1

    n_chunks = t // c

    def with_n(c_mat, n_row):
        return jnp.concatenate([c_mat, jnp.broadcast_to(n_row, (ML_DH, ML_DH)).T], axis=1)

    def conv_rows(rows):
        conv = _causal_conv_silu(cv_scr, mqk_ref[rows, :], cw_ref, cb_ref)
        qk_scr[rows, 0:ML_W] = conv[:, 0:ML_W].astype(BF16)
        qk_scr[rows, ML_W:] = (conv[:, ML_W:] * (ML_DH ** -0.5)).astype(BF16)

    if streams:
        for ci in range(n_chunks):
            cv_scr[0:CONV_PAD - hist, :] = jnp.zeros((CONV_PAD - hist, 2 * ML_W), F32)
            cv_scr[CONV_PAD - hist:CONV_PAD, :] = cv0_ref[ci]
            conv_rows(slice(ci * c, (ci + 1) * c))
            cvout_ref[ci] = mqk_ref[(ci + 1) * c - hist:(ci + 1) * c, :]
    else:
        @pl.when(step == 0)
        def _():
            if has_state:
                for h in range(ML_HEADS):
                    c_scr[h] = with_n(c0_ref[h], n0_ref[h])
                m_scr[...] = m0_ref[...]
                cv_scr[0:CONV_PAD - hist, :] = jnp.zeros((CONV_PAD - hist, 2 * ML_W), F32)
                cv_scr[CONV_PAD - hist:CONV_PAD, :] = cv0_ref[...]
            else:
                c_scr[...] = jnp.zeros_like(c_scr)
                m_scr[...] = jnp.zeros_like(m_scr)
                cv_scr[0:CONV_PAD, :] = jnp.zeros((CONV_PAD, 2 * ML_W), F32)

        conv_rows(slice(0, t))

    gts = pif_ref[...] + bif_ref[...]
    flog = pltpu.roll(_log_sigmoid(gts), LANES - ML_HEADS, axis=1)

    lane = lax.broadcasted_iota(jnp.int32, (t, LANES), 1)
    row_c = lax.broadcasted_iota(jnp.int32, (t, LANES), 0) % c
    causal = (lax.broadcasted_iota(jnp.int32, (c, c), 1) <= lax.broadcasted_iota(jnp.int32, (c, c), 0))
    ones_v = jnp.ones((c, ML_DH), BF16)

    def slots(pieces):
        out = jnp.zeros((t, LANES), F32)
        for j, piece in enumerate(pieces):
            if not isinstance(piece, float) and j > 0:
                piece = pltpu.roll(piece, ML_HEADS * j, axis=1)
            out = jnp.where((lane >= ML_HEADS * j) & (lane < ML_HEADS * (j + 1)), piece, out)
        return out

    def split3f(x):
        return [p.astype(F32) for p in _split3(x)]

    def per_chunk(rows_of):
        return jnp.concatenate([jnp.broadcast_to(rows_of(ci), (c, LANES)) for ci in range(n_chunks)], axis=0)

    bc = jnp.concatenate([_dot_exact_lhs(tri_ref[...], flog[ci * c:(ci + 1) * c])
                          for ci in range(n_chunks)], axis=0)
    w = gts - bc
    cm = w
    for j in range(int(math.log2(c))):
        sh = 1 << j
        cm = jnp.where(row_c >= sh, jnp.maximum(cm, pltpu.roll(cm, sh, axis=0)), cm)
    m_in, m_out = [], []
    for ci in range(n_chunks):
        last = slice(ci * c + c - 1, (ci + 1) * c)
        if streams:
            m_in.append(m0_ref[ci][0:1, :])
        else:
            m_in.append(m_scr[0:1, :] if ci == 0 else m_out[ci - 1])
        m_out.append(bc[last, :] + jnp.maximum(m_in[ci], cm[last, :]))
        if streams:
            mout_ref[ci] = jnp.broadcast_to(m_out[ci], mout_ref.shape[1:])
    if not streams:
        m_scr[...] = jnp.broadcast_to(m_out[-1], m_scr.shape)
    mprev = per_chunk(lambda ci: m_in[ci])
    g = jnp.maximum(mprev, cm)
    g_last = per_chunk(lambda ci: g[ci * c + c - 1:(ci + 1) * c, :])
    lhs_all = slots([1.0, 1.0, 1.0] + split3f(g))
    rhs_all = slots(split3f(w) + [-1.0, -1.0, -1.0]).astype(BF16)
    y_all = slots(split3f(mprev - g) + split3f(-(bc + g)) + split3f(w - g_last)).astype(BF16)
    lhs_heads = [jnp.where(lane % ML_HEADS == h, lhs_all, 0.0).astype(BF16) for h in range(ML_HEADS)]

    for ci in range(n_chunks):
        r0 = ci * c
        rows = pl.ds(r0, c)
        for h in range(ML_HEADS):
            hs = slice(h * ML_DH, (h + 1) * ML_DH)
            qh = qk_scr[rows, h * ML_DH:(h + 1) * ML_DH]
            kh = qk_scr[rows, ML_W + h * ML_DH:ML_W + (h + 1) * ML_DH]
            vaug = jnp.concatenate(
                [mvo_ref[rows, h * ML_DH:(h + 1) * ML_DH], ones_v],
                axis=1)
            p = jnp.exp(jnp.where(causal, _dot_nt(lhs_heads[h][r0:r0 + c], rhs_all[r0:r0 + c]), -jnp.inf))
            bx = jnp.exp(_dot(y_all[r0:r0 + c], sel_ref[h]))
            w_inter = bx[:, 0:ML_DH]
            e_mt = bx[:, ML_DH:2 * ML_DH]
            w_state = bx[:, 2 * ML_DH:3 * ML_DH]
            s = (_dot_nt(qh, kh) * p).astype(BF16)
            caug = with_n(c0_ref[ci, h], n0_ref[ci, h]) if streams else c_scr[h]
            nd = _dot(s, vaug) + jnp.concatenate([w_inter, w_inter], axis=1) * _dot(qh, caug.astype(BF16))
            hh = nd[:, 0:ML_DH] / jnp.maximum(jnp.abs(nd[:, ML_DH:]), e_mt)
            o_gate = mvo_ref[rows, ML_W + h * ML_DH:ML_W + (h + 1) * ML_DH]
            o_ref[rows, hs] = (o_gate.astype(F32) * hh).astype(o_ref.dtype)
            ks = (kh.astype(F32) * w_state).astype(BF16)
            dec = w_inter[c - 1:c, :]
            new = jnp.concatenate([dec, dec], axis=1) * caug + _dot_tn(ks, vaug)
            if streams:
                cout_ref[ci, h] = new[:, 0:ML_DH]
                nout_ref[ci, h] = new[:, ML_DH:].T[0:8, :]
            else:
                c_scr[h] = new

    if not streams:
        @pl.when(step == pl.num_programs(1) - 1)
        def _():
            for h in range(ML_HEADS):
                cout_ref[h] = c_scr[h, :, 0:ML_DH]
                nout_ref[h] = c_scr[h, :, ML_DH:].T[0:8, :]
            mout_ref[...] = m_scr[...]
            cvout_ref[...] = mqk_ref[t - hist:t, :]


def _mlstm_call(mqk, mvo, pif, c0, n0, m0, cv0, cw, cb, bif, c, t):
    seqs, seq_len, _ = mqk.shape
    has_state = c0 is not None
    streams = has_state and seq_len == c
    if streams:
        mqk, mvo, pif = (a.reshape(1, seqs * seq_len, a.shape[-1]) for a in (mqk, mvo, pif))
        t = seqs * seq_len
    b, l, _ = mqk.shape
    tri =jnp.asarray(np.tril(np.ones((c, c), np.float32)), BF16)
    sel = np.zeros((ML_HEADS, LANES, 3 * ML_DH), np.float32)
    for h in range(ML_HEADS):
        for slot in range(9):
            sel[h, ML_HEADS * slot + h, (slot // 3) * ML_DH:(slot // 3 + 1) * ML_DH] = 1.0
    sel = jnp.asarray(sel, BF16)
    tile = lambda w: pl.BlockSpec((None, t, w), lambda bi, i: (bi, i, 0))
    if streams:
        heads = lambda *shape: _full_spec((seqs, ML_HEADS) + shape)
        m_spec = _full_spec((seqs, 8, LANES))
        cv_spec = _full_spec((seqs, CONV_W - 1, 2 * ML_W))
    else:
        heads = lambda *shape: pl.BlockSpec((None, ML_HEADS) + shape, lambda bi, i: (bi, 0, 0, 0))
        m_spec = pl.BlockSpec((None, 8, LANES), lambda bi, i: (bi, 0, 0))
        cv_spec = pl.BlockSpec((None, CONV_W - 1, 2 * ML_W), lambda bi, i: (bi, 0, 0))
    c_spec = heads(ML_DH, ML_DH)
    in_specs = [tile(2 * ML_W), tile(2 * ML_W), tile(LANES)]
    args = [mqk, mvo, pif]
    consts = [cw, cb, bif, tri, sel]
    out_specs = [tile(ML_W), c_spec, heads(8, ML_DH), m_spec, cv_spec]
    out_shape = [jax.ShapeDtypeStruct((b, l, ML_W), BF16),
                 jax.ShapeDtypeStruct((seqs, ML_HEADS, ML_DH, ML_DH), F32),
                 jax.ShapeDtypeStruct((seqs, ML_HEADS, 8, ML_DH), F32),
                 jax.ShapeDtypeStruct((seqs, 8, LANES), F32),
                 jax.ShapeDtypeStruct((seqs, CONV_W - 1, 2 * ML_W), F32)]
    scratch = [pltpu.VMEM((ML_HEADS, ML_DH, 2 * ML_DH), F32), pltpu.VMEM((8, LANES), F32),
               pltpu.VMEM((CONV_PAD + t, 2 * ML_W), F32), pltpu.VMEM((t, 2 * ML_W), BF16)]
    if has_state:
        in_specs += [c_spec, heads(1, ML_DH), m_spec, cv_spec]
        args += [c0, n0, m0, cv0]
    in_specs += [_full_spec(x.shape) for x in consts]
    out, *states = pl.pallas_call(
        functools.partial(_mlstm_kernel, c=c, t=t, has_state=has_state, streams=streams),
        grid=(b, l // t),
        in_specs=in_specs,
        out_specs=out_specs,
        out_shape=out_shape,
        scratch_shapes=scratch,
        compiler_params=pltpu.CompilerParams(
            dimension_semantics=("arbitrary", "arbitrary"), vmem_limit_bytes=VMEM_LIMIT),
        name="mlstm",
    )(*args, *consts)
    return (out.reshape(seqs, seq_len, ML_W), *states)


def _merge_kernel(x_ref, ga_ref, hb_ref, pmg_ref, wug_ref, wum_ref, wo_ref, g2_ref,
                  wr_hi_ref, wr_lo_ref, br_ref, tril_ref, triu_ref, cnt0_ref,
                  x1_ref, hs_ref, meta_ref, tab_ref, cnt_scr):
    ya = _dot(ga_ref[...], wug_ref[...])
    yb = _dot(hb_ref[...], wum_ref[...])
    z = pmg_ref[:, 0:D_MODEL].astype(F32) * ya + pmg_ref[:, D_MODEL:].astype(F32) * yb
    x1 = x_ref[...] + _dot(z.astype(BF16), wo_ref[...])
    x1_ref[...] = x1
    hm = _rms(x1, g2_ref[...])
    hm_hi = hm.astype(BF16)
    hm_lo = (hm - hm_hi.astype(F32)).astype(BF16)
    hi_both = _dot(hm_hi, jnp.concatenate([wr_hi_ref[...], wr_lo_ref[...]], axis=1))
    logits = hi_both[:, 0:LANES] + hi_both[:, LANES:] + _dot(hm_lo, wr_hi_ref[...]) + br_ref[...]
    tm = logits.shape[0]
    n_out = -(-(N_EXPERTS + N_GROUPS) // 8) * 8
    lt = logits.T[0:n_out, :]
    out = lax.broadcasted_iota(jnp.int32, lt.shape, 0)
    neg = -jnp.inf
    is_g = (out >= N_EXPERTS) & (out < N_EXPERTS + N_GROUPS)
    lg = jnp.where(is_g, lt, neg)
    mg = jnp.max(lg, axis=0, keepdims=True)
    p_top = 1.0 / jnp.sum(jnp.exp(lg - mg), axis=0, keepdims=True)
    gi = jnp.min(jnp.where(lg == mg, out, 2 * LANES), axis=0, keepdims=True) - N_EXPERTS
    group_shift = int(math.log2(EXPERTS_PER_GROUP))
    sel = (out < N_EXPERTS) & (jnp.right_shift(out, group_shift) == gi)
    le = jnp.where(sel, lt, neg)
    v1 = jnp.max(le, axis=0, keepdims=True)
    i1 = jnp.min(jnp.where(le == v1, out, 2 * LANES), axis=0, keepdims=True)
    le2 = jnp.where(out == i1, neg, le)
    v2 = jnp.max(le2, axis=0, keepdims=True)
    i2 = jnp.min(jnp.where(le2 == v2, out, 2 * LANES), axis=0, keepdims=True)
    e2 = jnp.exp(v2 - v1)
    per_token = jnp.concatenate(
        [i1.astype(F32), i2.astype(F32), p_top * (1.0 / (1.0 + e2)), p_top * (e2 / (1.0 + e2)),
         jnp.zeros((LANES - 4, tm), F32)], axis=0).T
    i1 = per_token[:, 0:1].astype(jnp.int32)
    i2 = per_token[:, 1:2].astype(jnp.int32)
    w1 = per_token[:, 2:3]
    w2 = per_token[:, 3:4]
    lane = lax.broadcasted_iota(jnp.int32, logits.shape, 1)

    @pl.when(pl.program_id(0) == 0)
    def _():
        cnt_scr[...] = cnt0_ref[...]

    tm = logits.shape[0]
    oh1 = lane == i1
    oh2 = lane == i2
    both = jnp.where(oh1 | oh2, 1.0, 0.0)
    cnt = jnp.sum(both, axis=0, keepdims=True)
    cnt = jnp.floor((cnt + (SEG_ALIGN - 1)) * (1.0 / SEG_ALIGN)) * SEG_ALIGN
    lower = _dot(jnp.broadcast_to(cnt, (8, LANES)).astype(BF16), triu_ref[...])[0:1, :]
    lpos = _dot(tril_ref[...], both.astype(BF16)) + lower
    lp1 = jnp.sum(jnp.where(oh1, lpos, 0.0), axis=-1, keepdims=True)
    lp2 = jnp.sum(jnp.where(oh2, lpos, 0.0), axis=-1, keepdims=True)
    pos = lax.broadcasted_iota(jnp.int32, (tm, hs_ref.shape[0]), 1)
    onehot = jnp.where((pos == lp1.astype(jnp.int32)) | (pos == lp2.astype(jnp.int32)), 1.0, 0.0)
    hs_ref[...] = _dot_tn(onehot.astype(BF16), hm_hi).astype(BF16)
    cols = (lp1, lp2, w1, w2)
    meta = jnp.zeros((tm, LANES), F32)
    for ci, col in enumerate(cols):
        meta = jnp.where(lane == ci, col, meta)
    meta_ref[...] = meta
    row8 = lax.broadcasted_iota(jnp.int32, (8, LANES), 0)
    tab_ref[...] = jnp.where(row8 == 0, cnt, jnp.where(row8 == 1, lower, jnp.where(row8 == 2, cnt_scr[...], 0.0)))
    cnt_scr[...] += cnt


def _merge_call(x2, ga, hb, pmg, wug, wum, wo, g2, wr_hi, wr_lo, br, cnt0, tm):
    n = x2.shape[0]
    tile = lambda w: pl.BlockSpec((tm, w), lambda i: (i, 0))
    tiles = lambda *shape: pl.BlockSpec((None,) + shape, lambda i: (i, 0, 0))
    local_rows = 2 * tm + SEG_ALIGN * N_EXPERTS
    consts = [wug, wum, wo, g2, wr_hi, wr_lo, br,
              jnp.asarray(np.tril(np.ones((tm, tm), np.float32), -1), BF16),
              jnp.asarray(np.triu(np.ones((LANES, LANES), np.float32), 1), BF16),
              cnt0]
    return pl.pallas_call(
        _merge_kernel,
        grid=(n // tm,),
        in_specs=[tile(D_MODEL), tile(GLA_V), tile(ML_W), tile(W_MG)]
                 + [_full_spec(x.shape) for x in consts],
        out_specs=[tile(D_MODEL), tiles(local_rows, D_MODEL), tile(LANES), tiles(8, LANES)],
        out_shape=[jax.ShapeDtypeStruct((n, D_MODEL), F32),
                   jax.ShapeDtypeStruct((n // tm, local_rows, D_MODEL), BF16),
                   jax.ShapeDtypeStruct((n, LANES), F32),
                   jax.ShapeDtypeStruct((n // tm, 8, LANES), F32)],
        scratch_shapes=[pltpu.VMEM((1, LANES), F32)],
        compiler_params=pltpu.CompilerParams(
            dimension_semantics=("arbitrary",), vmem_limit_bytes=VMEM_LIMIT),
        name="merge",
    )(x2, ga, hb, pmg, *consts)


MOE_ROWS = 512
SEG_ALIGN = 16


def _segment_copies(fn, tables, tile, local_ref, buffer, flat_ref, sem, to_flat):
    pieces_ref, gdst_ref = tables
    stride = gdst_ref.shape[0] // pieces_ref.shape[0]

    def body(p, carry):
        local = local_ref.at[buffer, pl.ds(pl.multiple_of(p * SEG_ALIGN, SEG_ALIGN), SEG_ALIGN), :]
        flat = flat_ref.at[pl.ds(pl.multiple_of(gdst_ref[tile * stride + p], SEG_ALIGN), SEG_ALIGN), :]
        src, dst = (local, flat) if to_flat else (flat, local)
        fn(pltpu.make_async_copy(src, dst, sem))
        return carry

    lax.fori_loop(0, pieces_ref[tile], body, 0)


def _gmm_kernel(te_ref, nv_ref, part_ref, tile_ref, row_ref, *refs, n_parts):
    hs_refs = refs[:n_parts]
    wg_ref, wu_ref, wd_ref, o_ref, x_buf, sem, wgu_scr, wd_scr = refs[n_parts:]
    j = pl.program_id(0)
    n_valid = nv_ref[0]
    used = j < n_valid
    pieces_per_tile = MOE_ROWS // SEG_ALIGN

    def fetch(fn, tile):
        slot = tile % 2
        for k in range(pieces_per_tile):
            g = tile * pieces_per_tile + k
            for part, hs_ref in enumerate(hs_refs):
                @pl.when(part_ref[g] == part)
                def _():
                    src = hs_ref.at[tile_ref[g], pl.ds(pl.multiple_of(row_ref[g], SEG_ALIGN), SEG_ALIGN), :]
                    dst = x_buf.at[slot, pl.ds(k * SEG_ALIGN, SEG_ALIGN), :]
                    fn(pltpu.make_async_copy(src, dst, sem.at[slot]))

    @pl.when(j == 0)
    def _():
        x_buf[...] = jnp.zeros_like(x_buf)
        fetch(lambda c: c.start(), j)

    @pl.when(j + 1 < n_valid)
    def _():
        fetch(lambda c: c.start(), j + 1)

    @pl.when(used & ((j == 0) | (te_ref[j] != te_ref[jnp.maximum(j - 1, 0)])))
    def _():
        wgu_scr[:, 0:D_EXPERT] = wg_ref[...].astype(BF16)
        wgu_scr[:, D_EXPERT:] = wu_ref[...].astype(BF16)
        wd_scr[...] = wd_ref[...].astype(BF16)

    @pl.when(used)
    def _():
        fetch(lambda c: c.wait(), j)
        au = _dot(x_buf[j % 2], wgu_scr[...])
        a = au[:, 0:D_EXPERT]
        hh = (a * _sigmoid(a)) * au[:, D_EXPERT:]
        o_ref[...] = _dot(hh.astype(BF16), wd_scr[...]).astype(o_ref.dtype)

    @pl.when(jnp.logical_not(used))
    def _():
        o_ref[...] = jnp.zeros_like(o_ref)


def _gmm_call(tile_expert, n_valid, sources, hs_parts, wg, wu, wd):
    n_tiles = tile_expert.shape[0]
    wsel = lambda j, te, *_: (te[j], 0, 0)
    any_spec = pl.BlockSpec(memory_space=pl.ANY)
    return pl.pallas_call(
        functools.partial(_gmm_kernel, n_parts=len(hs_parts)),
        grid_spec=pltpu.PrefetchScalarGridSpec(
            num_scalar_prefetch=5,
            grid=(n_tiles,),
            in_specs=[any_spec] * len(hs_parts)
                     + [pl.BlockSpec((None, D_MODEL, D_EXPERT), wsel),
                        pl.BlockSpec((None, D_MODEL, D_EXPERT), wsel),
                        pl.BlockSpec((None, D_EXPERT, D_MODEL), wsel)],
            out_specs=pl.BlockSpec((MOE_ROWS, D_MODEL), lambda j, *_: (j, 0)),
            scratch_shapes=[pltpu.VMEM((2, MOE_ROWS, D_MODEL), BF16),
                            pltpu.SemaphoreType.DMA((2,)),
                            pltpu.VMEM((D_MODEL, 2 * D_EXPERT), BF16),
                            pltpu.VMEM((D_EXPERT, D_MODEL), BF16)],
        ),
        out_shape=jax.ShapeDtypeStruct((n_tiles * MOE_ROWS, D_MODEL), BF16),
        compiler_params=pltpu.CompilerParams(
            dimension_semantics=("arbitrary",), vmem_limit_bytes=VMEM_LIMIT),
        name="moe_grouped",
    )(tile_expert, n_valid, *sources, *hs_parts, wg, wu, wd)


def _combine_kernel(pieces_ref, gdst_ref, x1_ref, meta_ref, gf_ref, os_ref, y_ref, buf_ref, sem, *, tile0):
    step = pl.program_id(0)
    n_steps = pl.num_programs(0)
    tm = x1_ref.shape[0]
    tables = (pieces_ref, gdst_ref)

    def fetch(fn, i):
        slot = i % 2
        _segment_copies(fn, tables, tile0 + i, buf_ref, slot, os_ref, sem.at[slot], False)

    @pl.when(step == 0)
    def _():
        buf_ref[...] = jnp.zeros_like(buf_ref)
        fetch(lambda c: c.start(), step)

    @pl.when(step + 1 < n_steps)
    def _():
        fetch(lambda c: c.start(), step + 1)

    fetch(lambda c: c.wait(), step)
    rows = buf_ref[step % 2]
    pos = lax.broadcasted_iota(jnp.int32, (tm, rows.shape[0]), 1)
    lp1 = meta_ref[:, 0:1].astype(jnp.int32)
    lp2 = meta_ref[:, 1:2].astype(jnp.int32)
    q = jnp.where(pos == lp1, meta_ref[:, 2:3], 0.0) + jnp.where(pos == lp2, meta_ref[:, 3:4], 0.0)
    y = x1_ref[...] + _dot(q.astype(BF16), rows)
    y_ref[...] = _rms(y, gf_ref[...])


def _combine_call(tables, x1, meta, gf, out_sorted, tile0, local_rows, tm):
    n = x1.shape[0]
    tile = lambda w: pl.BlockSpec((tm, w), lambda i, *_: (i, 0))
    return pl.pallas_call(
        functools.partial(_combine_kernel, tile0=tile0),
        grid_spec=pltpu.PrefetchScalarGridSpec(
            num_scalar_prefetch=2,
            grid=(n // tm,),
            in_specs=[tile(D_MODEL), tile(LANES), pl.BlockSpec(gf.shape, lambda i, *_: (0, 0)),
                      pl.BlockSpec(memory_space=pl.ANY)],
            out_specs=tile(D_MODEL),
            scratch_shapes=[pltpu.VMEM((2, local_rows, D_MODEL), out_sorted.dtype),
                            pltpu.SemaphoreType.DMA((2,))],
        ),
        out_shape=jax.ShapeDtypeStruct((n, D_MODEL), F32),
        compiler_params=pltpu.CompilerParams(
            dimension_semantics=("arbitrary",), vmem_limit_bytes=VMEM_LIMIT),
        name="moe_combine",
    )(*tables, x1, meta, gf, out_sorted)


def _sparse_moe(parts, p):
    token_tiles = [part[1].shape[0] for part in parts]
    n_tiles = -(-sum(part[1].shape[0] * part[1].shape[1] for part in parts) // MOE_ROWS) + N_EXPERTS
    tab = jnp.concatenate([part[3] for part in parts], axis=0)
    tab = tab[:, :, :N_EXPERTS].astype(jnp.int32)
    cnt, before = tab[:, 0], tab[:, 2]
    total = before[-1] + cnt[-1]
    tiles = (total + (MOE_ROWS - 1)) // MOE_ROWS
    tile_end = jnp.cumsum(tiles)
    tile_start = tile_end - tiles
    n_valid = tile_end[-1:]
    gpos = tile_start[None, :] * MOE_ROWS + before
    j = jnp.minimum(jnp.arange(n_tiles, dtype=jnp.int32), n_valid - 1)
    tile_expert = jnp.sum((j[:, None] >= tile_end[None, :]).astype(jnp.int32), axis=1)
    max_pieces = max(part[1].shape[1] for part in parts) // SEG_ALIGN
    pieces = cnt // SEG_ALIGN
    piece_end = jnp.cumsum(pieces, axis=1)
    piece_start = piece_end - pieces
    piece = jnp.arange(max_pieces, dtype=jnp.int32)
    expert_of = jnp.sum((piece[None, :, None] >= piece_end[:, None, :]).astype(jnp.int32), axis=2)
    onehot = expert_of[:, :, None] == jnp.arange(N_EXPERTS, dtype=jnp.int32)
    first_row = gpos - SEG_ALIGN * piece_start
    gdst = jnp.sum(jnp.where(onehot, first_row[:, None, :], 0), axis=2) + SEG_ALIGN * piece[None, :]
    tables = (piece_end[:, -1], gdst.reshape(-1))
    n_token_tiles = sum(token_tiles)
    over_tiles = jnp.cumsum(pieces, axis=0)
    e_j = tile_expert
    q0 = (jnp.arange(n_tiles, dtype=jnp.int32) - jnp.take(tile_start, e_j)) * (MOE_ROWS // SEG_ALIGN)
    q = q0[:, None] + jnp.arange(MOE_ROWS // SEG_ALIGN, dtype=jnp.int32)[None, :]
    valid = (jnp.arange(n_tiles)[:, None] < n_valid) & (q < jnp.take(over_tiles[-1], e_j)[:, None])
    per_tile = lambda table: jnp.take(table.T, e_j, axis=0)[:, None, :]
    src_tile = jnp.sum((q[:, :, None] >= per_tile(over_tiles)).astype(jnp.int32), axis=2)
    src_tile = jnp.minimum(src_tile, n_token_tiles - 1)
    at_src = lambda table: jnp.sum(jnp.where(
        src_tile[:, :, None] == jnp.arange(n_token_tiles, dtype=jnp.int32), per_tile(table), 0), axis=2)
    src_row = (at_src(piece_start) + q - at_src(over_tiles - pieces)) * SEG_ALIGN
    part_start = np.cumsum([0] + token_tiles)
    src_part = jnp.sum((src_tile[:, :, None] >= jnp.asarray(part_start[1:], jnp.int32)).astype(jnp.int32), axis=2)
    src_tile_in_part = src_tile - jnp.sum(jnp.where(
        src_part[:, :, None] == jnp.arange(len(parts), dtype=jnp.int32), jnp.asarray(part_start[:-1], jnp.int32), 0), axis=2)
    sources = (jnp.where(valid, src_part, len(parts)).reshape(-1), src_tile_in_part.reshape(-1), src_row.reshape(-1))
    out_sorted = _gmm_call(tile_expert, n_valid.astype(jnp.int32), sources, [part[1] for part in parts],
                           p["wg"], p["wu"], p["wd"])
    return [_combine_call(tables, part[0], part[2], p["gf"], out_sorted, int(tile0),
                          part[1].shape[1], part[0].shape[0] // part[1].shape[0])
            for part, tile0 in zip(parts, part_start[:-1])]


def _pad_cols(w, width):
    return jnp.pad(w, ((0, 0), (0, width - w.shape[1])))


def _prep_weights(norm1_g, w_in, gla_w_gate2, gla_b_gate, gla_norm_g, w_up_gla,
                  ml_conv_w, ml_conv_b, ml_b_i, ml_b_f, w_up_ml, w_out,
                  norm2_g, router_g_w, router_g_b, router_e_w, router_e_b,
                  moe_w_gate, moe_w_up, moe_w_down, final_g):
    wr =_pad_cols(jnp.concatenate([router_e_w, router_g_w], axis=1), LANES)
    wr_hi = wr.astype(BF16)
    wr_lo = (wr - wr_hi.astype(F32)).astype(BF16)
    br = _pad_cols(jnp.concatenate([router_e_b, router_g_b])[None, :], LANES)
    wg2 = jnp.pad(gla_w_gate2, ((0, LANES - GLA_GATE_RANK), (0, 0)))
    wg2_hi = wg2.astype(BF16)
    return dict(
        g1=norm1_g[None, :], w_in=w_in.T,
        wg2_p=jnp.stack([wg2_hi, (wg2 - wg2_hi.astype(F32)).astype(BF16)]),
        bg=gla_b_gate[None, :], gn=gla_norm_g[None, :],
        wug=w_up_gla.astype(BF16),
        cw=ml_conv_w, cb=ml_conv_b[None, :],
        bif=_pad_cols(jnp.concatenate([ml_b_i, ml_b_f])[None, :], LANES),
        wum=w_up_ml.astype(BF16), wo=w_out.astype(BF16),
        g2=norm2_g[None, :], wr_hi=wr_hi, wr_lo=wr_lo, br=br,
        wg=moe_w_gate.reshape(N_EXPERTS, D_MODEL, D_EXPERT),
        wu=moe_w_up.reshape(N_EXPERTS, D_MODEL, D_EXPERT),
        wd=moe_w_down.reshape(N_EXPERTS, D_EXPERT, D_MODEL),
        gf=final_g[None, :],
    )


def _mixers(x, gla_s0, ml_c0, ml_n0, ml_m0, conv0, p, placed, *,
            gla_chunk, ml_chunk, seq_tile, row_tile, merge_tile):
    b, l, _ = x.shape
    n = b * l
    x2 = x.reshape(n, D_MODEL)
    qk, vr, plr, mqk, mvo, pif, pmg = _proj_call(x2, p["g1"], p["w_in"], row_tile)
    r3 = lambda a: a.reshape(b, l, a.shape[-1])
    s0 = None if gla_s0 is None else gla_s0.reshape(b, GLA_QK, GLA_DV)
    ga, gla_s = _gla_call(r3(qk), r3(vr), r3(plr), s0, p["wg2_p"], p["bg"], p["gn"], gla_chunk, seq_tile)
    gla_s = gla_s.reshape(b, GLA_HEADS, GLA_DK, GLA_DV)
    if ml_c0 is None:
        n0 = m0 = None
    else:
        n0 = ml_n0[:, :, None, :]
        m0 = jnp.broadcast_to(_pad_cols(ml_m0, LANES)[:, None, :], (b, 8, LANES))
    hb, ml_c, ml_n, m_b, new_conv = _mlstm_call(r3(mqk), r3(mvo), r3(pif), ml_c0, n0, m0, conv0,
                                                p["cw"], p["cb"], p["bif"], ml_chunk, seq_tile)
    part = _merge_call(x2, ga.reshape(n, GLA_V), hb.reshape(n, ML_W), pmg,
                       p["wug"], p["wum"], p["wo"], p["g2"],
                       p["wr_hi"], p["wr_lo"], p["br"], placed, merge_tile)
    states = (gla_s[None], ml_c[None], ml_n[:, :, 0, :][None], m_b[:, 0, 0:ML_HEADS][None], new_conv[None])
    return part, states


def kernel(x_prompt, x_sample, state_gla_S, state_mlstm_C, state_mlstm_n, state_mlstm_m, state_mlstm_conv, norm1_g, w_in, gla_w_gate2, gla_b_gate, gla_norm_g, w_up_gla, ml_conv_w, ml_conv_b, ml_b_i, ml_b_f, w_up_ml, w_out, norm2_g, router_g_w, router_g_b, router_e_w, router_e_b, moe_w_gate, moe_w_up, moe_w_down, final_g):
    assert norm1_g.shape[0] == 1, "single-layer trunk"
    p = _prep_weights(norm1_g[0], w_in[0], gla_w_gate2[0], gla_b_gate[0], gla_norm_g[0], w_up_gla[0],
                      ml_conv_w[0], ml_conv_b[0], ml_b_i[0], ml_b_f[0], w_up_ml[0], w_out[0],
                      norm2_g[0], router_g_w[0], router_g_b[0], router_e_w[0], router_e_b[0],
                      moe_w_gate[0], moe_w_up[0], moe_w_down[0], final_g)
    dec_seq = x_sample.shape[1]
    n_sample = x_sample.shape[0] * dec_seq
    part_p, sp = _mixers(x_prompt, None, None, None, None, None, p, jnp.zeros((1, LANES), F32),
                         gla_chunk=128, ml_chunk=256, seq_tile=1024, row_tile=256, merge_tile=512)
    placed = part_p[3][-1, 0:1, :] + part_p[3][-1, 2:3, :]
    part_s, ss = _mixers(x_sample, state_gla_S[0], state_mlstm_C[0], state_mlstm_n[0], state_mlstm_m[0],
                         state_mlstm_conv[0], p, placed,
                         gla_chunk=dec_seq, ml_chunk=dec_seq, seq_tile=dec_seq,
                         row_tile=n_sample, merge_tile=n_sample)
    yp, ys = _sparse_moe([part_p, part_s], p)
    return (yp.reshape(x_prompt.shape), ys.reshape(x_sample.shape), *sp, *ss)
```

```python
import functools
import math

import numpy as np
import jax
import jax.numpy as jnp
from jax import lax
from jax.experimental import pallas as pl
from jax.experimental.pallas import tpu as pltpu

D_MODEL = 1024
GLA_HEADS = 4
GLA_DK = 64
GLA_DV = 128
GLA_GATE_RANK = 16
GLA_TAU = 16.0
ML_HEADS = 4
ML_DH = 128
CONV_W = 4
N_GROUPS = 4
EXPERTS_PER_GROUP = 8
N_EXPERTS = N_GROUPS * EXPERTS_PER_GROUP
D_EXPERT = 256
EPS = 1e-6

GLA_QK = GLA_HEADS * GLA_DK
GLA_V = GLA_HEADS * GLA_DV
ML_W = ML_HEADS * ML_DH

LANES = 128
VMEM_LIMIT = 56 * 1024 * 1024

W_GLA = 2 * GLA_QK + 2 * GLA_V
W_ML = 2 * ML_W + ML_W + ML_W
W_MG = 2 * D_MODEL
PROJ_WIDTHS = (W_GLA, LANES, W_ML, LANES, W_MG)
PROJ_SOURCE_WIDTHS = (W_GLA, GLA_GATE_RANK, W_ML, 2 * ML_HEADS, W_MG)

F32 = jnp.float32
BF16 = jnp.bfloat16

PROJ_OUTPUTS = ((2 * GLA_QK, F32), (2 * GLA_V, BF16), (LANES, F32), (2 * ML_W, F32), (2 * ML_W, BF16),
                (LANES, F32), (W_MG, BF16))


def _dot(a, b):
    return jnp.dot(a, b, preferred_element_type=F32)


def _dot_nt(a, b):
    return lax.dot_general(a, b, (((1,), (1,)), ((), ())), preferred_element_type=F32)


def _dot_tn(a, b):
    return lax.dot_general(a, b, (((0,), (0,)), ((), ())), preferred_element_type=F32)


def _split3(x):
    hi = x.astype(BF16)
    r1 = x - hi.astype(F32)
    mid = r1.astype(BF16)
    lo = (r1 - mid.astype(F32)).astype(BF16)
    return hi, mid, lo


def _dot_exact_lhs(m, x):
    hi, mid, lo = _split3(x)
    return _dot(m, hi) + _dot(m, mid) + _dot(m, lo)


def _log_sigmoid(z):
    return jnp.minimum(z, 0.0) - jnp.log(1.0 + jnp.exp(-jnp.abs(z)))


def _sigmoid(z):
    return 1.0 / (1.0 + jnp.exp(-z))


def _rms(x, g):
    return x * lax.rsqrt(jnp.mean(x * x, axis=-1, keepdims=True) + EPS) * g


def _full_spec(shape):
    nd = len(shape)
    return pl.BlockSpec(shape, lambda *_: (0,) * nd)


CONV_PAD = 8


def _causal_conv_silu(stage_ref, x, cw_ref, cb_ref):
    t = x.shape[0]
    stage_ref[CONV_PAD:CONV_PAD + t, :] = x
    acc = cb_ref[...] + stage_ref[CONV_PAD:CONV_PAD + t, :] * cw_ref[CONV_W - 1:CONV_W, :]
    for d in range(1, CONV_W):
        acc = acc + stage_ref[CONV_PAD - d:CONV_PAD - d + t, :] * cw_ref[CONV_W - 1 - d:CONV_W - d, :]
    stage_ref[0:CONV_PAD, :] = stage_ref[t:t + CONV_PAD, :]
    return acc * _sigmoid(acc)


def _proj_kernel(x_ref, g_ref, win_ref, qk_ref, vr_ref, lr_ref, mqk_ref, mvo_ref, if_ref, mg_ref, w_ref):
    starts = np.cumsum((0,) + PROJ_WIDTHS)

    @pl.when(pl.program_id(0) == 0)
    def _():
        src = np.cumsum((0,) + PROJ_SOURCE_WIDTHS)
        chunk = 512
        for g, width in enumerate(PROJ_SOURCE_WIDTHS):
            for r0 in range(0, width, chunk):
                rows = min(chunk, width - r0)
                w_ref[starts[g] + r0:starts[g] + r0 + rows, :] = (
                    win_ref[src[g] + r0:src[g] + r0 + rows, :].astype(BF16))
            if width < PROJ_WIDTHS[g]:
                w_ref[starts[g] + width:starts[g + 1], :] = jnp.zeros((PROJ_WIDTHS[g] - width, D_MODEL), BF16)

    h = _rms(x_ref[...], g_ref[...]).astype(BF16)

    def cols(group, lo, hi):
        return _dot_nt(h, w_ref[starts[group] + lo:starts[group] + hi, :])

    qk_ref[...] = cols(0, 0, 2 * GLA_QK)
    vr_ref[:, 0:GLA_V] = cols(0, 2 * GLA_QK, 2 * GLA_QK + GLA_V).astype(BF16)
    r = cols(0, 2 * GLA_QK + GLA_V, W_GLA)
    vr_ref[:, GLA_V:] = (r * _sigmoid(r)).astype(BF16)
    lr_ref[...] = cols(1, 0, LANES)
    mqk_ref[...] = cols(2, 0, 2 * ML_W)
    mvo_ref[:, 0:ML_W] = cols(2, 2 * ML_W, 3 * ML_W).astype(BF16)
    mvo_ref[:, ML_W:] = _sigmoid(cols(2, 3 * ML_W, W_ML)).astype(BF16)
    if_ref[...] = cols(3, 0, LANES)
    mg_ref[...] = _sigmoid(cols(4, 0, W_MG)).astype(BF16)


def _proj_call(x2, g, w_in, tm):
    n = x2.shape[0]
    assert w_in.shape == (sum(PROJ_SOURCE_WIDTHS), D_MODEL)
    return pl.pallas_call(
        _proj_kernel,
        grid=(n // tm,),
        in_specs=[pl.BlockSpec((tm, D_MODEL), lambda i: (i, 0)),
                  _full_spec(g.shape),
                  pl.BlockSpec(w_in.shape, lambda i: (0, 0), pipeline_mode=pl.Buffered(1))],
        out_specs=[pl.BlockSpec((tm, w), lambda i: (i, 0)) for w, _ in PROJ_OUTPUTS],
        out_shape=[jax.ShapeDtypeStruct((n, w), dt) for w, dt in PROJ_OUTPUTS],
        scratch_shapes=[pltpu.VMEM((sum(PROJ_WIDTHS), D_MODEL), BF16)],
        compiler_params=pltpu.CompilerParams(
            dimension_semantics=("arbitrary",), vmem_limit_bytes=VMEM_LIMIT),
        name="in_proj",
    )(x2, g, w_in)


def _gla_consts(c):
    nlev = int(math.log2(c))
    assert 1 << nlev == c
    t = np.arange(c)[:, None]
    j = np.arange(c)[None, :]
    lv = np.full((c, c), -1, np.int32)
    for l in range(nlev):
        h = c >> (l + 1)
        upper = (t % (2 * h)) >= h
        same = (j // (2 * h)) == (t // (2 * h))
        s_lower = (j % (2 * h)) < h
        lv[np.broadcast_to(upper, (c, c)) & same & s_lower] = l
    lv[np.eye(c, dtype=bool)] = nlev
    tri = (j <= t).astype(np.float32)
    return jnp.asarray(tri, BF16), jnp.asarray(np.concatenate([lv, lv], axis=1))


def _gla_kernel(*refs, c, t, has_state, streams):
    if has_state:
        (qk_ref, vr_ref, plr_ref, s0_ref, wg2_ref, bg_ref, gn_ref, tri_ref, lv_ref,
         o_ref, sout_ref, s_scr) = refs
    else:
        (qk_ref, vr_ref, plr_ref, wg2_ref, bg_ref, gn_ref, tri_ref, lv_ref,
         o_ref, sout_ref, s_scr) = refs
    nlev = int(math.log2(c))
    step = pl.program_id(1)

    if not streams:
        @pl.when(step == 0)
        def _():
            if has_state:
                s_scr[...] = s0_ref[...]
            else:
                s_scr[...] = jnp.zeros_like(s_scr)

    n_chunks = t // c
    lane_k = lax.broadcasted_iota(jnp.int32, (t, GLA_QK), 1)
    first_of_pair = (lane_k % (2 * GLA_DK)) < GLA_DK
    row_k = lax.broadcasted_iota(jnp.int32, (GLA_QK, GLA_DV), 0)
    row_t = lax.broadcasted_iota(jnp.int32, (t, GLA_QK), 0)

    def block_ref(b, blk, idx):
        if blk >= 8:
            b3 = b.reshape(t // blk, blk, GLA_QK)
            return jnp.broadcast_to(b3[:, idx:idx + 1, :], b3.shape).reshape(t, GLA_QK)
        r = row_t % blk
        out = b
        for sh in range(-idx, blk - idx):
            if sh != 0:
                out = jnp.where(r - idx == sh, pltpu.roll(b, sh % t, axis=0), out)
        return out

    q = qk_ref[:, 0:GLA_QK] * (GLA_DK ** -0.5)
    k = qk_ref[:, GLA_QK:2 * GLA_QK]
    glr = plr_ref[...]
    g_hi = glr.astype(BF16)
    g_lo = (glr - g_hi.astype(F32)).astype(BF16)
    z = (_dot(g_hi, wg2_ref[0]) + _dot(g_lo, wg2_ref[0]) + _dot(g_hi, wg2_ref[1])) + bg_ref[...]
    la = _log_sigmoid(z) * (1.0 / GLA_TAU)
    b = jnp.concatenate([_dot_exact_lhs(tri_ref[...], la[ci * c:(ci + 1) * c])
                         for ci in range(n_chunks)], axis=0)
    b_last = block_ref(b, c, c - 1)
    qe = (q * jnp.exp(b)).astype(BF16)
    kl = (k * jnp.exp(b_last - b)).astype(BF16)

    k_a = jnp.where(first_of_pair, k, 0.0)
    k_b = k - k_a
    lv2 = lv_ref[...]
    factors = []
    for l in range(nlev + 1):
        if l < nlev:
            half = c >> (l + 1)
            d = b - block_ref(b, 2 * half, half - 1)
            e = jnp.exp(jnp.minimum(d, -d))
            qt, kta, ktb = q * e, k_a * e, k_b * e
        else:
            qt, kta, ktb = q, k_a, k_b
        factors.append((qt.astype(BF16), kta.astype(BF16), ktb.astype(BF16)))
    a = [[None] * (GLA_HEADS // 2) for _ in range(n_chunks)]
    for ci in range(n_chunks):
        rows = slice(ci * c, (ci + 1) * c)
        for pr in range(GLA_HEADS // 2):
            ls = slice(pr * 2 * GLA_DK, (pr + 1) * 2 * GLA_DK)
            acc = jnp.zeros((c, 2 * c), F32)
            for l, (qt, kta, ktb) in enumerate(factors):
                rhs = jnp.concatenate([kta[rows, ls], ktb[rows, ls]], axis=0)
                acc = jnp.where(lv2 == l, _dot_nt(qt[rows, ls], rhs), acc)
            a[ci][pr] = acc

    for ci in range(n_chunks):
        rows = slice(ci * c, (ci + 1) * c)
        v = vr_ref[rows, 0:GLA_V]
        s_in, s_out = (s0_ref.at[ci], sout_ref.at[ci]) if streams else (s_scr, s_scr)
        s_all = s_in[...]
        s_bd = jnp.concatenate(
            [jnp.where((row_k // GLA_DK) == h, s_all, 0.0).astype(BF16) for h in range(GLA_HEADS)], axis=1)
        o_inter = _dot(qe[rows], s_bd)
        u_all = _dot_tn(kl[rows], v)
        dcol = jnp.exp(jnp.broadcast_to(b[ci * c + c - 1:(ci + 1) * c, :], (LANES, GLA_QK)).T)
        for h in range(GLA_HEADS):
            vs = slice(h * GLA_DV, (h + 1) * GLA_DV)
            ks = slice(h * GLA_DK, (h + 1) * GLA_DK)
            a_h = a[ci][h // 2][:, (h % 2) * c:(h % 2 + 1) * c]
            o = _dot(a_h.astype(BF16), v[:, vs]) + o_inter[:, vs]
            on = _rms(o, gn_ref[:, vs])
            gate = vr_ref[rows, GLA_V + h * GLA_DV:GLA_V + (h + 1) * GLA_DV]
            o_ref[rows, vs] = (on * gate.astype(F32)).astype(o_ref.dtype)
            s_out[ks, :] = dcol[ks, :] * s_all[ks, :] + u_all[ks, vs]

    if not streams:
        @pl.when(step == pl.num_programs(1) - 1)
        def _():
            sout_ref[...] = s_scr[...]


def _gla_call(qk, vr, plr, s0, wg2_p, bg, gn, c, t):
    seqs, seq_len, _ = qk.shape
    tri, lv2 = _gla_consts(c)
    has_state = s0 is not None
    streams = has_state and seq_len == c
    if streams:
        qk, vr, plr = (a.reshape(1, seqs * seq_len, a.shape[-1]) for a in (qk, vr, plr))
        t = seqs * seq_len
    b, l, _ = qk.shape
    tile = lambda w: pl.BlockSpec((None, t, w), lambda bi, i: (bi, i, 0))
    if streams:
        state_spec = pl.BlockSpec((seqs, GLA_QK, GLA_DV), lambda bi, i: (0, 0, 0))
    else:
        state_spec = pl.BlockSpec((None, GLA_QK, GLA_DV), lambda bi, i: (bi, 0, 0))
    in_specs = [tile(2 * GLA_QK), tile(2 * GLA_V), tile(LANES)]
    args = [qk, vr, plr]
    if has_state:
        in_specs.append(state_spec)
        args.append(s0)
    consts = [wg2_p, bg, gn, tri, lv2]
    in_specs += [_full_spec(x.shape) for x in consts]
    out, state = pl.pallas_call(
        functools.partial(_gla_kernel, c=c, t=t, has_state=has_state, streams=streams),
        grid=(b, l // t),
        in_specs=in_specs,
        out_specs=[tile(GLA_V), state_spec],
        out_shape=[jax.ShapeDtypeStruct((b, l, GLA_V), BF16),
                   jax.ShapeDtypeStruct((seqs, GLA_QK, GLA_DV), F32)],
        scratch_shapes=[pltpu.VMEM((GLA_QK, GLA_DV), F32)],
        compiler_params=pltpu.CompilerParams(
            dimension_semantics=("arbitrary", "arbitrary"), vmem_limit_bytes=VMEM_LIMIT),
        name="gla",
    )(*args, *consts)
    return out.reshape(seqs, seq_len, GLA_V), state


def _mlstm_kernel(*refs, c, t, has_state, streams):
    if has_state:
        (mqk_ref, mvo_ref, pif_ref, c0_ref, n0_ref, m0_ref, cv0_ref, cw_ref, cb_ref, bif_ref, tri_ref, sel_ref,
         o_ref, cout_ref, nout_ref, mout_ref, cvout_ref, c_scr, m_scr, cv_scr, qk_scr) = refs
    else:
        (mqk_ref, mvo_ref, pif_ref, cw_ref, cb_ref, bif_ref, tri_ref, sel_ref,
         o_ref, cout_ref, nout_ref, mout_ref, cvout_ref, c_scr, m_scr, cv_scr, qk_scr) = refs
    step = pl.program_id(1)
    hist = CONV_W - 1

    n_chunks = t // c

    def with_n(c_mat, n_row):
        return jnp.concatenate([c_mat, jnp.broadcast_to(n_row, (ML_DH, ML_DH)).T], axis=1)

    def conv_rows(rows):
        conv = _causal_conv_silu(cv_scr, mqk_ref[rows, :], cw_ref, cb_ref)
        qk_scr[rows, 0:ML_W] = conv[:, 0:ML_W].astype(BF16)
        qk_scr[rows, ML_W:] = (conv[:, ML_W:] * (ML_DH ** -0.5)).astype(BF16)

    if streams:
        for ci in range(n_chunks):
            cv_scr[0:CONV_PAD - hist, :] = jnp.zeros((CONV_PAD - hist, 2 * ML_W), F32)
            cv_scr[CONV_PAD - hist:CONV_PAD, :] = cv0_ref[ci]
            conv_rows(slice(ci * c, (ci + 1) * c))
            cvout_ref[ci] = mqk_ref[(ci + 1) * c - hist:(ci + 1) * c, :]
    else:
        @pl.when(step == 0)
        def _():
            if has_state:
                for h in range(ML_HEADS):
                    c_scr[h] = with_n(c0_ref[h], n0_ref[h])
                m_scr[...] = m0_ref[...]
                cv_scr[0:CONV_PAD - hist, :] = jnp.zeros((CONV_PAD - hist, 2 * ML_W), F32)
                cv_scr[CONV_PAD - hist:CONV_PAD, :] = cv0_ref[...]
            else:
                c_scr[...] = jnp.zeros_like(c_scr)
                m_scr[...] = jnp.zeros_like(m_scr)
                cv_scr[0:CONV_PAD, :] = jnp.zeros((CONV_PAD, 2 * ML_W), F32)

        conv_rows(slice(0, t))

    gts = pif_ref[...] + bif_ref[...]
    flog = pltpu.roll(_log_sigmoid(gts), LANES - ML_HEADS, axis=1)

    lane = lax.broadcasted_iota(jnp.int32, (t, LANES), 1)
    row_c = lax.broadcasted_iota(jnp.int32, (t, LANES), 0) % c
    causal = (lax.broadcasted_iota(jnp.int32, (c, c), 1) <= lax.broadcasted_iota(jnp.int32, (c, c), 0))
    ones_v = jnp.ones((c, ML_DH), BF16)

    def slots(pieces):
        out = jnp.zeros((t, LANES), F32)
        for j, piece in enumerate(pieces):
            if not isinstance(piece, float) and j > 0:
                piece = pltpu.roll(piece, ML_HEADS * j, axis=1)
            out = jnp.where((lane >= ML_HEADS * j) & (lane < ML_HEADS * (j + 1)), piece, out)
        return out

    def split3f(x):
        return [p.astype(F32) for p in _split3(x)]

    def per_chunk(rows_of):
        return jnp.concatenate([jnp.broadcast_to(rows_of(ci), (c, LANES)) for ci in range(n_chunks)], axis=0)

    bc = jnp.concatenate([_dot_exact_lhs(tri_ref[...], flog[ci * c:(ci + 1) * c])
                          for ci in range(n_chunks)], axis=0)
    w = gts - bc
    cm = w
    for j in range(int(math.log2(c))):
        sh = 1 << j
        cm = jnp.where(row_c >= sh, jnp.maximum(cm, pltpu.roll(cm, sh, axis=0)), cm)
    m_in, m_out = [], []
    for ci in range(n_chunks):
        last = slice(ci * c + c - 1, (ci + 1) * c)
        if streams:
            m_in.append(m0_ref[ci][0:1, :])
        else:
            m_in.append(m_scr[0:1, :] if ci == 0 else m_out[ci - 1])
        m_out.append(bc[last, :] + jnp.maximum(m_in[ci], cm[last, :]))
        if streams:
            mout_ref[ci] = jnp.broadcast_to(m_out[ci], mout_ref.shape[1:])
    if not streams:
        m_scr[...] = jnp.broadcast_to(m_out[-1], m_scr.shape)
    mprev = per_chunk(lambda ci: m_in[ci])
    g = jnp.maximum(mprev, cm)
    g_last = per_chunk(lambda ci: g[ci * c + c - 1:(ci + 1) * c, :])
    lhs_all = slots([1.0, 1.0, 1.0] + split3f(g))
    rhs_all = slots(split3f(w) + [-1.0, -1.0, -1.0]).astype(BF16)
    y_all = slots(split3f(mprev - g) + split3f(-(bc + g)) + split3f(w - g_last)).astype(BF16)
    lhs_heads = [jnp.where(lane % ML_HEADS == h, lhs_all, 0.0).astype(BF16) for h in range(ML_HEADS)]

    for ci in range(n_chunks):
        r0 = ci * c
        rows = pl.ds(r0, c)
        for h in range(ML_HEADS):
            hs = slice(h * ML_DH, (h + 1) * ML_DH)
            qh = qk_scr[rows, h * ML_DH:(h + 1) * ML_DH]
            kh = qk_scr[rows, ML_W + h * ML_DH:ML_W + (h + 1) * ML_DH]
            vaug = jnp.concatenate(
                [mvo_ref[rows, h * ML_DH:(h + 1) * ML_DH], ones_v],
                axis=1)
            p = jnp.exp(jnp.where(causal, _dot_nt(lhs_heads[h][r0:r0 + c], rhs_all[r0:r0 + c]), -jnp.inf))
            bx = jnp.exp(_dot(y_all[r0:r0 + c], sel_ref[h]))
            w_inter = bx[:, 0:ML_DH]
            e_mt = bx[:, ML_DH:2 * ML_DH]
            w_state = bx[:, 2 * ML_DH:3 * ML_DH]
            s = (_dot_nt(qh, kh) * p).astype(BF16)
            caug = with_n(c0_ref[ci, h], n0_ref[ci, h]) if streams else c_scr[h]
            nd = _dot(s, vaug) + jnp.concatenate([w_inter, w_inter], axis=1) * _dot(qh, caug.astype(BF16))
            hh = nd[:, 0:ML_DH] / jnp.maximum(jnp.abs(nd[:, ML_DH:]), e_mt)
            o_gate = mvo_ref[rows, ML_W + h * ML_DH:ML_W + (h + 1) * ML_DH]
            o_ref[rows, hs] = (o_gate.astype(F32) * hh).astype(o_ref.dtype)
            ks = (kh.astype(F32) * w_state).astype(BF16)
            dec = w_inter[c - 1:c, :]
            new = jnp.concatenate([dec, dec], axis=1) * caug + _dot_tn(ks, vaug)
            if streams:
                cout_ref[ci, h] = new[:, 0:ML_DH]
                nout_ref[ci, h] = new[:, ML_DH:].T[0:8, :]
            else:
                c_scr[h] = new

    if not streams:
        @pl.when(step == pl.num_programs(1) - 1)
        def _():
            for h in range(ML_HEADS):
                cout_ref[h] = c_scr[h, :, 0:ML_DH]
                nout_ref[h] = c_scr[h, :, ML_DH:].T[0:8, :]
            mout_ref[...] = m_scr[...]
            cvout_ref[...] = mqk_ref[t - hist:t, :]


def _mlstm_call(mqk, mvo, pif, c0, n0, m0, cv0, cw, cb, bif, c, t):
    seqs, seq_len, _ = mqk.shape
    has_state = c0 is not None
    streams = has_state and seq_len == c
    if streams:
        mqk, mvo, pif = (a.reshape(1, seqs * seq_len, a.shape[-1]) for a in (mqk, mvo, pif))
        t = seqs * seq_len
    b, l, _ = mqk.shape
    tri =jnp.asarray(np.tril(np.ones((c, c), np.float32)), BF16)
    sel = np.zeros((ML_HEADS, LANES, 3 * ML_DH), np.float32)
    for h in range(ML_HEADS):
        for slot in range(9):
            sel[h, ML_HEADS * slot + h, (slot // 3) * ML_DH:(slot // 3 + 1) * ML_DH] = 1.0
    sel = jnp.asarray(sel, BF16)
    tile = lambda w: pl.BlockSpec((None, t, w), lambda bi, i: (bi, i, 0))
    if streams:
        heads = lambda *shape: _full_spec((seqs, ML_HEADS) + shape)
        m_spec = _full_spec((seqs, 8, LANES))
        cv_spec = _full_spec((seqs, CONV_W - 1, 2 * ML_W))
    else:
        heads = lambda *shape: pl.BlockSpec((None, ML_HEADS) + shape, lambda bi, i: (bi, 0, 0, 0))
        m_spec = pl.BlockSpec((None, 8, LANES), lambda bi, i: (bi, 0, 0))
        cv_spec = pl.BlockSpec((None, CONV_W - 1, 2 * ML_W), lambda bi, i: (bi, 0, 0))
    c_spec = heads(ML_DH, ML_DH)
    in_specs = [tile(2 * ML_W), tile(2 * ML_W), tile(LANES)]
    args = [mqk, mvo, pif]
    consts = [cw, cb, bif, tri, sel]
    out_specs = [tile(ML_W), c_spec, heads(8, ML_DH), m_spec, cv_spec]
    out_shape = [jax.ShapeDtypeStruct((b, l, ML_W), BF16),
                 jax.ShapeDtypeStruct((seqs, ML_HEADS, ML_DH, ML_DH), F32),
                 jax.ShapeDtypeStruct((seqs, ML_HEADS, 8, ML_DH), F32),
                 jax.ShapeDtypeStruct((seqs, 8, LANES), F32),
                 jax.ShapeDtypeStruct((seqs, CONV_W - 1, 2 * ML_W), F32)]
    scratch = [pltpu.VMEM((ML_HEADS, ML_DH, 2 * ML_DH), F32), pltpu.VMEM((8, LANES), F32),
               pltpu.VMEM((CONV_PAD + t, 2 * ML_W), F32), pltpu.VMEM((t, 2 * ML_W), BF16)]
    if has_state:
        in_specs += [c_spec, heads(1, ML_DH), m_spec, cv_spec]
        args += [c0, n0, m0, cv0]
    in_specs += [_full_spec(x.shape) for x in consts]
    out, *states = pl.pallas_call(
        functools.partial(_mlstm_kernel, c=c, t=t, has_state=has_state, streams=streams),
        grid=(b, l // t),
        in_specs=in_specs,
        out_specs=out_specs,
        out_shape=out_shape,
        scratch_shapes=scratch,
        compiler_params=pltpu.CompilerParams(
            dimension_semantics=("arbitrary", "arbitrary"), vmem_limit_bytes=VMEM_LIMIT),
        name="mlstm",
    )(*args, *consts)
    return (out.reshape(seqs, seq_len, ML_W), *states)


def _merge_kernel(x_ref, ga_ref, hb_ref, pmg_ref, wug_ref, wum_ref, wo_ref, g2_ref,
                  wr_hi_ref, wr_lo_ref, br_ref, tril_ref, triu_ref, cnt0_ref,
                  x1_ref, hs_ref, meta_ref, tab_ref, cnt_scr):
    ya = _dot(ga_ref[...], wug_ref[...])
    yb = _dot(hb_ref[...], wum_ref[...])
    z = pmg_ref[:, 0:D_MODEL].astype(F32) * ya + pmg_ref[:, D_MODEL:].astype(F32) * yb
    x1 = x_ref[...] + _dot(z.astype(BF16), wo_ref[...])
    x1_ref[...] = x1
    hm = _rms(x1, g2_ref[...])
    hm_hi = hm.astype(BF16)
    hm_lo = (hm - hm_hi.astype(F32)).astype(BF16)
    hi_both = _dot(hm_hi, jnp.concatenate([wr_hi_ref[...], wr_lo_ref[...]], axis=1))
    logits = hi_both[:, 0:LANES] + hi_both[:, LANES:] + _dot(hm_lo, wr_hi_ref[...]) + br_ref[...]
    tm = logits.shape[0]
    n_out = -(-(N_EXPERTS + N_GROUPS) // 8) * 8
    lt = logits.T[0:n_out, :]
    out = lax.broadcasted_iota(jnp.int32, lt.shape, 0)
    neg = -jnp.inf
    is_g = (out >= N_EXPERTS) & (out < N_EXPERTS + N_GROUPS)
    lg = jnp.where(is_g, lt, neg)
    mg = jnp.max(lg, axis=0, keepdims=True)
    p_top = 1.0 / jnp.sum(jnp.exp(lg - mg), axis=0, keepdims=True)
    gi = jnp.min(jnp.where(lg == mg, out, 2 * LANES), axis=0, keepdims=True) - N_EXPERTS
    group_shift = int(math.log2(EXPERTS_PER_GROUP))
    sel = (out < N_EXPERTS) & (jnp.right_shift(out, group_shift) == gi)
    le = jnp.where(sel, lt, neg)
    v1 = jnp.max(le, axis=0, keepdims=True)
    i1 = jnp.min(jnp.where(le == v1, out, 2 * LANES), axis=0, keepdims=True)
    le2 = jnp.where(out == i1, neg, le)
    v2 = jnp.max(le2, axis=0, keepdims=True)
    i2 = jnp.min(jnp.where(le2 == v2, out, 2 * LANES), axis=0, keepdims=True)
    e2 = jnp.exp(v2 - v1)
    per_token = jnp.concatenate(
        [i1.astype(F32), i2.astype(F32), p_top * (1.0 / (1.0 + e2)), p_top * (e2 / (1.0 + e2)),
         jnp.zeros((LANES - 4, tm), F32)], axis=0).T
    i1 = per_token[:, 0:1].astype(jnp.int32)
    i2 = per_token[:, 1:2].astype(jnp.int32)
    w1 = per_token[:, 2:3]
    w2 = per_token[:, 3:4]
    lane = lax.broadcasted_iota(jnp.int32, logits.shape, 1)

    @pl.when(pl.program_id(0) == 0)
    def _():
        cnt_scr[...] = cnt0_ref[...]

    tm = logits.shape[0]
    oh1 = lane == i1
    oh2 = lane == i2
    both = jnp.where(oh1 | oh2, 1.0, 0.0)
    cnt = jnp.sum(both, axis=0, keepdims=True)
    cnt = jnp.floor((cnt + (SEG_ALIGN - 1)) * (1.0 / SEG_ALIGN)) * SEG_ALIGN
    lower = _dot(jnp.broadcast_to(cnt, (8, LANES)).astype(BF16), triu_ref[...])[0:1, :]
    lpos = _dot(tril_ref[...], both.astype(BF16)) + lower
    lp1 = jnp.sum(jnp.where(oh1, lpos, 0.0), axis=-1, keepdims=True)
    lp2 = jnp.sum(jnp.where(oh2, lpos, 0.0), axis=-1, keepdims=True)
    pos = lax.broadcasted_iota(jnp.int32, (tm, hs_ref.shape[0]), 1)
    onehot = jnp.where((pos == lp1.astype(jnp.int32)) | (pos == lp2.astype(jnp.int32)), 1.0, 0.0)
    hs_ref[...] = _dot_tn(onehot.astype(BF16), hm_hi).astype(BF16)
    cols = (lp1, lp2, w1, w2)
    meta = jnp.zeros((tm, LANES), F32)
    for ci, col in enumerate(cols):
        meta = jnp.where(lane == ci, col, meta)
    meta_ref[...] = meta
    row8 = lax.broadcasted_iota(jnp.int32, (8, LANES), 0)
    tab_ref[...] = jnp.where(row8 == 0, cnt, jnp.where(row8 == 1, lower, jnp.where(row8 == 2, cnt_scr[...], 0.0)))
    cnt_scr[...] += cnt


def _merge_call(x2, ga, hb, pmg, wug, wum, wo, g2, wr_hi, wr_lo, br, cnt0, tm):
    n = x2.shape[0]
    tile = lambda w: pl.BlockSpec((tm, w), lambda i: (i, 0))
    tiles = lambda *shape: pl.BlockSpec((None,) + shape, lambda i: (i, 0, 0))
    local_rows = 2 * tm + SEG_ALIGN * N_EXPERTS
    consts = [wug, wum, wo, g2, wr_hi, wr_lo, br,
              jnp.asarray(np.tril(np.ones((tm, tm), np.float32), -1), BF16),
              jnp.asarray(np.triu(np.ones((LANES, LANES), np.float32), 1), BF16),
              cnt0]
    return pl.pallas_call(
        _merge_kernel,
        grid=(n // tm,),
        in_specs=[tile(D_MODEL), tile(GLA_V), tile(ML_W), tile(W_MG)]
                 + [_full_spec(x.shape) for x in consts],
        out_specs=[tile(D_MODEL), tiles(local_rows, D_MODEL), tile(LANES), tiles(8, LANES)],
        out_shape=[jax.ShapeDtypeStruct((n, D_MODEL), F32),
                   jax.ShapeDtypeStruct((n // tm, local_rows, D_MODEL), BF16),
                   jax.ShapeDtypeStruct((n, LANES), F32),
                   jax.ShapeDtypeStruct((n // tm, 8, LANES), F32)],
        scratch_shapes=[pltpu.VMEM((1, LANES), F32)],
        compiler_params=pltpu.CompilerParams(
            dimension_semantics=("arbitrary",), vmem_limit_bytes=VMEM_LIMIT),
        name="merge",
    )(x2, ga, hb, pmg, *consts)


MOE_ROWS = 512
SEG_ALIGN = 16


def _segment_copies(fn, tables, tile, local_ref, buffer, flat_ref, sem, to_flat):
    pieces_ref, gdst_ref = tables
    stride = gdst_ref.shape[0] // pieces_ref.shape[0]

    def body(p, carry):
        local = local_ref.at[buffer, pl.ds(pl.multiple_of(p * SEG_ALIGN, SEG_ALIGN), SEG_ALIGN), :]
        flat = flat_ref.at[pl.ds(pl.multiple_of(gdst_ref[tile * stride + p], SEG_ALIGN), SEG_ALIGN), :]
        src, dst = (local, flat) if to_flat else (flat, local)
        fn(pltpu.make_async_copy(src, dst, sem))
        return carry

    lax.fori_loop(0, pieces_ref[tile], body, 0)


def _gmm_kernel(te_ref, nv_ref, part_ref, row_ref, *refs, n_parts):
    hs_refs = refs[:n_parts]
    wg_ref, wu_ref, wd_ref, o_ref, x_buf, sem, wgu_scr, wd_scr = refs[n_parts:]
    j = pl.program_id(0)
    n_valid = nv_ref[0]
    used = j < n_valid
    pieces_per_tile = MOE_ROWS // SEG_ALIGN

    def piece_copy(hs_ref, row, slot, k):
        src = hs_ref.at[pl.ds(pl.multiple_of(row, SEG_ALIGN), SEG_ALIGN), :]
        return pltpu.make_async_copy(src, x_buf.at[slot, pl.ds(k * SEG_ALIGN, SEG_ALIGN), :], sem.at[slot])

    def fetch(tile):
        slot = tile % 2
        for k in range(pieces_per_tile):
            g = tile * pieces_per_tile + k
            for part, hs_ref in enumerate(hs_refs):
                @pl.when(part_ref[g] == part)
                def _():
                    piece_copy(hs_ref, row_ref[g], slot, k).start()

    def wait_tile(tile):
        slot = tile % 2
        for k in range(pieces_per_tile):
            piece_copy(hs_refs[0], 0, slot, k).wait()

    @pl.when(j == 0)
    def _():
        fetch(j)

    @pl.when(j + 1 < n_valid)
    def _():
        fetch(j + 1)

    @pl.when(used & ((j == 0) | (te_ref[j] != te_ref[jnp.maximum(j - 1, 0)])))
    def _():
        wgu_scr[:, 0:D_EXPERT] = wg_ref[...].astype(BF16)
        wgu_scr[:, D_EXPERT:] = wu_ref[...].astype(BF16)
        wd_scr[...] = wd_ref[...].astype(BF16)

    @pl.when(used)
    def _():
        wait_tile(j)
        au = _dot(x_buf[j % 2], wgu_scr[...])
        a = au[:, 0:D_EXPERT]
        hh = (a * _sigmoid(a)) * au[:, D_EXPERT:]
        o_ref[...] = _dot(hh.astype(BF16), wd_scr[...]).astype(o_ref.dtype)

    @pl.when(jnp.logical_not(used))
    def _():
        o_ref[...] = jnp.zeros_like(o_ref)


def _gmm_call(tile_expert, n_valid, sources, hs_parts, wg, wu, wd):
    n_tiles = tile_expert.shape[0]
    wsel = lambda j, te, *_: (te[j], 0, 0)
    any_spec = pl.BlockSpec(memory_space=pl.ANY)
    return pl.pallas_call(
        functools.partial(_gmm_kernel, n_parts=len(hs_parts)),
        grid_spec=pltpu.PrefetchScalarGridSpec(
            num_scalar_prefetch=4,
            grid=(n_tiles,),
            in_specs=[any_spec] * len(hs_parts)
                     + [pl.BlockSpec((None, D_MODEL, D_EXPERT), wsel),
                        pl.BlockSpec((None, D_MODEL, D_EXPERT), wsel),
                        pl.BlockSpec((None, D_EXPERT, D_MODEL), wsel)],
            out_specs=pl.BlockSpec((MOE_ROWS, D_MODEL), lambda j, *_: (j, 0)),
            scratch_shapes=[pltpu.VMEM((2, MOE_ROWS, D_MODEL), BF16),
                            pltpu.SemaphoreType.DMA((2,)),
                            pltpu.VMEM((D_MODEL, 2 * D_EXPERT), BF16),
                            pltpu.VMEM((D_EXPERT, D_MODEL), BF16)],
        ),
        out_shape=jax.ShapeDtypeStruct((n_tiles * MOE_ROWS, D_MODEL), BF16),
        compiler_params=pltpu.CompilerParams(
            dimension_semantics=("arbitrary",), vmem_limit_bytes=VMEM_LIMIT),
        name="moe_grouped",
    )(tile_expert, n_valid, *sources, *hs_parts, wg, wu, wd)


def _combine_kernel(pieces_ref, gdst_ref, x1_ref, meta_ref, gf_ref, os_ref, y_ref, buf_ref, sem, *, tile0):
    step = pl.program_id(0)
    n_steps = pl.num_programs(0)
    tm = x1_ref.shape[0]
    tables = (pieces_ref, gdst_ref)

    def fetch(fn, i):
        slot = i % 2
        _segment_copies(fn, tables, tile0 + i, buf_ref, slot, os_ref, sem.at[slot], False)

    @pl.when(step == 0)
    def _():
        buf_ref[...] = jnp.zeros_like(buf_ref)
        fetch(lambda c: c.start(), step)

    @pl.when(step + 1 < n_steps)
    def _():
        fetch(lambda c: c.start(), step + 1)

    fetch(lambda c: c.wait(), step)
    rows = buf_ref[step % 2]
    pos = lax.broadcasted_iota(jnp.int32, (tm, rows.shape[0]), 1)
    lp1 = meta_ref[:, 0:1].astype(jnp.int32)
    lp2 = meta_ref[:, 1:2].astype(jnp.int32)
    q = jnp.where(pos == lp1, meta_ref[:, 2:3], 0.0) + jnp.where(pos == lp2, meta_ref[:, 3:4], 0.0)
    y = x1_ref[...] + _dot(q.astype(BF16), rows)
    y_ref[...] = _rms(y, gf_ref[...])


def _combine_call(tables, x1, meta, gf, out_sorted, tile0, local_rows, tm):
    n = x1.shape[0]
    tile = lambda w: pl.BlockSpec((tm, w), lambda i, *_: (i, 0))
    return pl.pallas_call(
        functools.partial(_combine_kernel, tile0=tile0),
        grid_spec=pltpu.PrefetchScalarGridSpec(
            num_scalar_prefetch=2,
            grid=(n // tm,),
            in_specs=[tile(D_MODEL), tile(LANES), pl.BlockSpec(gf.shape, lambda i, *_: (0, 0)),
                      pl.BlockSpec(memory_space=pl.ANY)],
            out_specs=tile(D_MODEL),
            scratch_shapes=[pltpu.VMEM((2, local_rows, D_MODEL), out_sorted.dtype),
                            pltpu.SemaphoreType.DMA((2,))],
        ),
        out_shape=jax.ShapeDtypeStruct((n, D_MODEL), F32),
        compiler_params=pltpu.CompilerParams(
            dimension_semantics=("arbitrary",), vmem_limit_bytes=VMEM_LIMIT),
        name="moe_combine",
    )(*tables, x1, meta, gf, out_sorted)


def _sparse_moe(parts, p):
    token_tiles = [part[1].shape[0] for part in parts]
    n_tiles = -(-sum(part[1].shape[0] * part[1].shape[1] for part in parts) // MOE_ROWS) + N_EXPERTS
    tab = jnp.concatenate([part[3] for part in parts], axis=0)
    tab = tab[:, :, :N_EXPERTS].astype(jnp.int32)
    cnt, before = tab[:, 0], tab[:, 2]
    total = before[-1] + cnt[-1]
    tiles = (total + (MOE_ROWS - 1)) // MOE_ROWS
    tile_end = jnp.cumsum(tiles)
    tile_start = tile_end - tiles
    n_valid = tile_end[-1:]
    gpos = tile_start[None, :] * MOE_ROWS + before
    j = jnp.minimum(jnp.arange(n_tiles, dtype=jnp.int32), n_valid - 1)
    tile_expert = jnp.sum((j[:, None] >= tile_end[None, :]).astype(jnp.int32), axis=1)
    max_pieces = max(part[1].shape[1] for part in parts) // SEG_ALIGN
    pieces = cnt // SEG_ALIGN
    piece_end = jnp.cumsum(pieces, axis=1)
    piece_start = piece_end - pieces
    piece = jnp.arange(max_pieces, dtype=jnp.int32)
    expert_of = jnp.sum((piece[None, :, None] >= piece_end[:, None, :]).astype(jnp.int32), axis=2)
    onehot = expert_of[:, :, None] == jnp.arange(N_EXPERTS, dtype=jnp.int32)
    first_row = gpos - SEG_ALIGN * piece_start
    gdst = jnp.sum(jnp.where(onehot, first_row[:, None, :], 0), axis=2) + SEG_ALIGN * piece[None, :]
    tables = (piece_end[:, -1], gdst.reshape(-1))
    n_token_tiles = sum(token_tiles)
    over_tiles = jnp.cumsum(pieces, axis=0)
    e_j = tile_expert
    q0 = (jnp.arange(n_tiles, dtype=jnp.int32) - jnp.take(tile_start, e_j)) * (MOE_ROWS // SEG_ALIGN)
    q = q0[:, None] + jnp.arange(MOE_ROWS // SEG_ALIGN, dtype=jnp.int32)[None, :]
    valid = (jnp.arange(n_tiles)[:, None] < n_valid) & (q < jnp.take(over_tiles[-1], e_j)[:, None])
    per_tile = lambda table: jnp.take(table.T, e_j, axis=0)[:, None, :]
    src_tile = jnp.sum((q[:, :, None] >= per_tile(over_tiles)).astype(jnp.int32), axis=2)
    src_tile = jnp.minimum(src_tile, n_token_tiles - 1)
    at_src = lambda table: jnp.sum(jnp.where(
        src_tile[:, :, None] == jnp.arange(n_token_tiles, dtype=jnp.int32), per_tile(table), 0), axis=2)
    src_row = (at_src(piece_start) + q - at_src(over_tiles - pieces)) * SEG_ALIGN
    part_start = np.cumsum([0] + token_tiles)
    src_part = jnp.sum((src_tile[:, :, None] >= jnp.asarray(part_start[1:], jnp.int32)).astype(jnp.int32), axis=2)
    src_tile_in_part = src_tile - jnp.sum(jnp.where(
        src_part[:, :, None] == jnp.arange(len(parts), dtype=jnp.int32), jnp.asarray(part_start[:-1], jnp.int32), 0), axis=2)
    local_rows = jnp.asarray([part[1].shape[1] for part in parts], jnp.int32)
    rows_of_part = jnp.sum(jnp.where(src_part[:, :, None] == jnp.arange(len(parts), dtype=jnp.int32), local_rows, 0), axis=2)
    flat_row = src_tile_in_part * rows_of_part + src_row
    zero_row = parts[0][1].shape[1] - SEG_ALIGN
    assert zero_row >= 2 * (parts[0][0].shape[0] // parts[0][1].shape[0]) + (SEG_ALIGN - 1) * N_EXPERTS
    sources = (jnp.where(valid, src_part, 0).reshape(-1), jnp.where(valid, flat_row, zero_row).reshape(-1))
    out_sorted = _gmm_call(tile_expert, n_valid.astype(jnp.int32), sources,
                           [part[1].reshape(-1, D_MODEL) for part in parts], p["wg"], p["wu"], p["wd"])
    return [_combine_call(tables, part[0], part[2], p["gf"], out_sorted, int(tile0),
                          part[1].shape[1], part[0].shape[0] // part[1].shape[0])
            for part, tile0 in zip(parts, part_start[:-1])]


def _pad_cols(w, width):
    return jnp.pad(w, ((0, 0), (0, width - w.shape[1])))


def _prep_weights(norm1_g, w_in, gla_w_gate2, gla_b_gate, gla_norm_g, w_up_gla,
                  ml_conv_w, ml_conv_b, ml_b_i, ml_b_f, w_up_ml, w_out,
                  norm2_g, router_g_w, router_g_b, router_e_w, router_e_b,
                  moe_w_gate, moe_w_up, moe_w_down, final_g):
    wr =_pad_cols(jnp.concatenate([router_e_w, router_g_w], axis=1), LANES)
    wr_hi = wr.astype(BF16)
    wr_lo = (wr - wr_hi.astype(F32)).astype(BF16)
    br = _pad_cols(jnp.concatenate([router_e_b, router_g_b])[None, :], LANES)
    wg2 = jnp.pad(gla_w_gate2, ((0, LANES - GLA_GATE_RANK), (0, 0)))
    wg2_hi = wg2.astype(BF16)
    return dict(
        g1=norm1_g[None, :], w_in=w_in.T,
        wg2_p=jnp.stack([wg2_hi, (wg2 - wg2_hi.astype(F32)).astype(BF16)]),
        bg=gla_b_gate[None, :], gn=gla_norm_g[None, :],
        wug=w_up_gla.astype(BF16),
        cw=ml_conv_w, cb=ml_conv_b[None, :],
        bif=_pad_cols(jnp.concatenate([ml_b_i, ml_b_f])[None, :], LANES),
        wum=w_up_ml.astype(BF16), wo=w_out.astype(BF16),
        g2=norm2_g[None, :], wr_hi=wr_hi, wr_lo=wr_lo, br=br,
        wg=moe_w_gate.reshape(N_EXPERTS, D_MODEL, D_EXPERT),
        wu=moe_w_up.reshape(N_EXPERTS, D_MODEL, D_EXPERT),
        wd=moe_w_down.reshape(N_EXPERTS, D_EXPERT, D_MODEL),
        gf=final_g[None, :],
    )


def _mixers(x, gla_s0, ml_c0, ml_n0, ml_m0, conv0, p, placed, *,
            gla_chunk, ml_chunk, seq_tile, row_tile, merge_tile):
    b, l, _ = x.shape
    n = b * l
    x2 = x.reshape(n, D_MODEL)
    qk, vr, plr, mqk, mvo, pif, pmg = _proj_call(x2, p["g1"], p["w_in"], row_tile)
    r3 = lambda a: a.reshape(b, l, a.shape[-1])
    s0 = None if gla_s0 is None else gla_s0.reshape(b, GLA_QK, GLA_DV)
    ga, gla_s = _gla_call(r3(qk), r3(vr), r3(plr), s0, p["wg2_p"], p["bg"], p["gn"], gla_chunk, seq_tile)
    gla_s = gla_s.reshape(b, GLA_HEADS, GLA_DK, GLA_DV)
    if ml_c0 is None:
        n0 = m0 = None
    else:
        n0 = ml_n0[:, :, None, :]
        m0 = jnp.broadcast_to(_pad_cols(ml_m0, LANES)[:, None, :], (b, 8, LANES))
    hb, ml_c, ml_n, m_b, new_conv = _mlstm_call(r3(mqk), r3(mvo), r3(pif), ml_c0, n0, m0, conv0,
                                                p["cw"], p["cb"], p["bif"], ml_chunk, seq_tile)
    part = _merge_call(x2, ga.reshape(n, GLA_V), hb.reshape(n, ML_W), pmg,
                       p["wug"], p["wum"], p["wo"], p["g2"],
                       p["wr_hi"], p["wr_lo"], p["br"], placed, merge_tile)
    states = (gla_s[None], ml_c[None], ml_n[:, :, 0, :][None], m_b[:, 0, 0:ML_HEADS][None], new_conv[None])
    return part, states


def kernel(x_prompt, x_sample, state_gla_S, state_mlstm_C, state_mlstm_n, state_mlstm_m, state_mlstm_conv, norm1_g, w_in, gla_w_gate2, gla_b_gate, gla_norm_g, w_up_gla, ml_conv_w, ml_conv_b, ml_b_i, ml_b_f, w_up_ml, w_out, norm2_g, router_g_w, router_g_b, router_e_w, router_e_b, moe_w_gate, moe_w_up, moe_w_down, final_g):
    assert norm1_g.shape[0] == 1, "single-layer trunk"
    p = _prep_weights(norm1_g[0], w_in[0], gla_w_gate2[0], gla_b_gate[0], gla_norm_g[0], w_up_gla[0],
                      ml_conv_w[0], ml_conv_b[0], ml_b_i[0], ml_b_f[0], w_up_ml[0], w_out[0],
                      norm2_g[0], router_g_w[0], router_g_b[0], router_e_w[0], router_e_b[0],
                      moe_w_gate[0], moe_w_up[0], moe_w_down[0], final_g)
    dec_seq = x_sample.shape[1]
    n_sample = x_sample.shape[0] * dec_seq
    part_p, sp = _mixers(x_prompt, None, None, None, None, None, p, jnp.zeros((1, LANES), F32),
                         gla_chunk=128, ml_chunk=256, seq_tile=1024, row_tile=256, merge_tile=512)
    placed = part_p[3][-1, 0:1, :] + part_p[3][-1, 2:3, :]
    part_s, ss = _mixers(x_sample, state_gla_S[0], state_mlstm_C[0], state_mlstm_n[0], state_mlstm_m[0],
                         state_mlstm_conv[0], p, placed,
                         gla_chunk=dec_seq, ml_chunk=dec_seq, seq_tile=dec_seq,
                         row_tile=n_sample, merge_tile=n_sample)
    yp, ys = _sparse_moe([part_p, part_s], p)
    return (yp.reshape(x_prompt.shape), ys.reshape(x_sample.shape), *sp, *ss)
```

```python
import functools
import math

import numpy as np
import jax
import jax.numpy as jnp
from jax import lax
from jax.experimental import pallas as pl
from jax.experimental.pallas import tpu as pltpu

D_MODEL = 1024
GLA_HEADS = 4
GLA_DK = 64
GLA_DV = 128
GLA_GATE_RANK = 16
GLA_TAU = 16.0
ML_HEADS = 4
ML_DH = 128
CONV_W = 4
N_GROUPS = 4
EXPERTS_PER_GROUP = 8
N_EXPERTS = N_GROUPS * EXPERTS_PER_GROUP
D_EXPERT = 256
EPS = 1e-6

GLA_QK = GLA_HEADS * GLA_DK
GLA_V = GLA_HEADS * GLA_DV
ML_W = ML_HEADS * ML_DH

LANES = 128
VMEM_LIMIT = 56 * 1024 * 1024

W_GLA = 2 * GLA_QK + 2 * GLA_V
W_ML = 2 * ML_W + ML_W + ML_W
W_MG = 2 * D_MODEL
PROJ_WIDTHS = (W_GLA, LANES, W_ML, LANES, W_MG)
PROJ_SOURCE_WIDTHS = (W_GLA, GLA_GATE_RANK, W_ML, 2 * ML_HEADS, W_MG)

F32 = jnp.float32
BF16 = jnp.bfloat16

PROJ_OUTPUTS = ((2 * GLA_QK, F32), (2 * GLA_V, BF16), (LANES, F32), (2 * ML_W, F32), (2 * ML_W, BF16),
                (LANES, F32), (W_MG, BF16))


def _dot(a, b):
    return jnp.dot(a, b, preferred_element_type=F32)


def _dot_nt(a, b):
    return lax.dot_general(a, b, (((1,), (1,)), ((), ())), preferred_element_type=F32)


def _dot_tn(a, b):
    return lax.dot_general(a, b, (((0,), (0,)), ((), ())), preferred_element_type=F32)


def _split3(x):
    hi = x.astype(BF16)
    r1 = x - hi.astype(F32)
    mid = r1.astype(BF16)
    lo = (r1 - mid.astype(F32)).astype(BF16)
    return hi, mid, lo


def _dot_exact_lhs(m, x):
    hi, mid, lo = _split3(x)
    return _dot(m, hi) + _dot(m, mid) + _dot(m, lo)


def _log_sigmoid(z):
    return jnp.minimum(z, 0.0) - jnp.log(1.0 + jnp.exp(-jnp.abs(z)))


def _sigmoid(z):
    return 1.0 / (1.0 + jnp.exp(-z))


def _rms(x, g):
    return x * lax.rsqrt(jnp.mean(x * x, axis=-1, keepdims=True) + EPS) * g


def _full_spec(shape):
    nd = len(shape)
    return pl.BlockSpec(shape, lambda *_: (0,) * nd)


CONV_PAD = 8


def _causal_conv_silu(stage_ref, x, cw_ref, cb_ref):
    t = x.shape[0]
    stage_ref[CONV_PAD:CONV_PAD + t, :] = x
    acc = cb_ref[...] + stage_ref[CONV_PAD:CONV_PAD + t, :] * cw_ref[CONV_W - 1:CONV_W, :]
    for d in range(1, CONV_W):
        acc = acc + stage_ref[CONV_PAD - d:CONV_PAD - d + t, :] * cw_ref[CONV_W - 1 - d:CONV_W - d, :]
    stage_ref[0:CONV_PAD, :] = stage_ref[t:t + CONV_PAD, :]
    return acc * _sigmoid(acc)


def _proj_kernel(x_ref, g_ref, win_ref, qk_ref, vr_ref, lr_ref, mqk_ref, mvo_ref, if_ref, mg_ref, w_ref):
    starts = np.cumsum((0,) + PROJ_WIDTHS)

    @pl.when(pl.program_id(0) == 0)
    def _():
        src = np.cumsum((0,) + PROJ_SOURCE_WIDTHS)
        chunk = 512
        for g, width in enumerate(PROJ_SOURCE_WIDTHS):
            for r0 in range(0, width, chunk):
                rows = min(chunk, width - r0)
                w_ref[starts[g] + r0:starts[g] + r0 + rows, :] = (
                    win_ref[src[g] + r0:src[g] + r0 + rows, :].astype(BF16))
            if width < PROJ_WIDTHS[g]:
                w_ref[starts[g] + width:starts[g + 1], :] = jnp.zeros((PROJ_WIDTHS[g] - width, D_MODEL), BF16)

    h = _rms(x_ref[...], g_ref[...]).astype(BF16)

    def cols(group, lo, hi):
        return _dot_nt(h, w_ref[starts[group] + lo:starts[group] + hi, :])

    qk_ref[...] = cols(0, 0, 2 * GLA_QK)
    vr_ref[:, 0:GLA_V] = cols(0, 2 * GLA_QK, 2 * GLA_QK + GLA_V).astype(BF16)
    r = cols(0, 2 * GLA_QK + GLA_V, W_GLA)
    vr_ref[:, GLA_V:] = (r * _sigmoid(r)).astype(BF16)
    lr_ref[...] = cols(1, 0, LANES)
    mqk_ref[...] = cols(2, 0, 2 * ML_W)
    mvo_ref[:, 0:ML_W] = cols(2, 2 * ML_W, 3 * ML_W).astype(BF16)
    mvo_ref[:, ML_W:] = _sigmoid(cols(2, 3 * ML_W, W_ML)).astype(BF16)
    if_ref[...] = cols(3, 0, LANES)
    mg_ref[...] = _sigmoid(cols(4, 0, W_MG)).astype(BF16)


def _proj_call(x2, g, w_in, tm):
    n = x2.shape[0]
    assert w_in.shape == (sum(PROJ_SOURCE_WIDTHS), D_MODEL)
    return pl.pallas_call(
        _proj_kernel,
        grid=(n // tm,),
        in_specs=[pl.BlockSpec((tm, D_MODEL), lambda i: (i, 0)),
                  _full_spec(g.shape),
                  pl.BlockSpec(w_in.shape, lambda i: (0, 0), pipeline_mode=pl.Buffered(1))],
        out_specs=[pl.BlockSpec((tm, w), lambda i: (i, 0)) for w, _ in PROJ_OUTPUTS],
        out_shape=[jax.ShapeDtypeStruct((n, w), dt) for w, dt in PROJ_OUTPUTS],
        scratch_shapes=[pltpu.VMEM((sum(PROJ_WIDTHS), D_MODEL), BF16)],
        compiler_params=pltpu.CompilerParams(
            dimension_semantics=("arbitrary",), vmem_limit_bytes=VMEM_LIMIT),
        name="in_proj",
    )(x2, g, w_in)


def _gla_consts(c):
    nlev = int(math.log2(c))
    assert 1 << nlev == c
    t = np.arange(c)[:, None]
    j = np.arange(c)[None, :]
    lv = np.full((c, c), -1, np.int32)
    for l in range(nlev):
        h = c >> (l + 1)
        upper = (t % (2 * h)) >= h
        same = (j // (2 * h)) == (t // (2 * h))
        s_lower = (j % (2 * h)) < h
        lv[np.broadcast_to(upper, (c, c)) & same & s_lower] = l
    lv[np.eye(c, dtype=bool)] = nlev
    tri = (j <= t).astype(np.float32)
    return jnp.asarray(tri, BF16), jnp.asarray(np.concatenate([lv, lv], axis=1))


def _gla_kernel(*refs, c, t, has_state, streams):
    if has_state:
        (qk_ref, vr_ref, plr_ref, s0_ref, wg2_ref, bg_ref, gn_ref, tri_ref, lv_ref,
         o_ref, sout_ref, s_scr) = refs
    else:
        (qk_ref, vr_ref, plr_ref, wg2_ref, bg_ref, gn_ref, tri_ref, lv_ref,
         o_ref, sout_ref, s_scr) = refs
    nlev = int(math.log2(c))
    step = pl.program_id(1)

    if not streams:
        @pl.when(step == 0)
        def _():
            if has_state:
                s_scr[...] = s0_ref[...]
            else:
                s_scr[...] = jnp.zeros_like(s_scr)

    n_chunks = t // c
    lane_k = lax.broadcasted_iota(jnp.int32, (t, GLA_QK), 1)
    first_of_pair = (lane_k % (2 * GLA_DK)) < GLA_DK
    row_k = lax.broadcasted_iota(jnp.int32, (GLA_QK, GLA_DV), 0)
    row_t = lax.broadcasted_iota(jnp.int32, (t, GLA_QK), 0)

    def block_ref(b, blk, idx):
        if blk >= 8:
            b3 = b.reshape(t // blk, blk, GLA_QK)
            return jnp.broadcast_to(b3[:, idx:idx + 1, :], b3.shape).reshape(t, GLA_QK)
        r = row_t % blk
        out = b
        for sh in range(-idx, blk - idx):
            if sh != 0:
                out = jnp.where(r - idx == sh, pltpu.roll(b, sh % t, axis=0), out)
        return out

    q = qk_ref[:, 0:GLA_QK] * (GLA_DK ** -0.5)
    k = qk_ref[:, GLA_QK:2 * GLA_QK]
    glr = plr_ref[...]
    g_hi = glr.astype(BF16)
    g_lo = (glr - g_hi.astype(F32)).astype(BF16)
    z = (_dot(g_hi, wg2_ref[0]) + _dot(g_lo, wg2_ref[0]) + _dot(g_hi, wg2_ref[1])) + bg_ref[...]
    la = _log_sigmoid(z) * (math.log2(math.e) / GLA_TAU)
    b = jnp.concatenate([_dot_exact_lhs(tri_ref[...], la[ci * c:(ci + 1) * c])
                         for ci in range(n_chunks)], axis=0)
    b_last = block_ref(b, c, c - 1)
    qe = (q * jnp.exp2(b)).astype(BF16)
    kl = (k * jnp.exp2(b_last - b)).astype(BF16)

    k_a = jnp.where(first_of_pair, k, 0.0)
    k_b = k - k_a
    lv2 = lv_ref[...]
    factors = []
    for l in range(nlev + 1):
        if l < nlev:
            half = c >> (l + 1)
            d = b - block_ref(b, 2 * half, half - 1)
            e = jnp.exp2(jnp.minimum(d, -d))
            qt, kta, ktb = q * e, k_a * e, k_b * e
        else:
            qt, kta, ktb = q, k_a, k_b
        factors.append((qt.astype(BF16), kta.astype(BF16), ktb.astype(BF16)))
    a = [[None] * (GLA_HEADS // 2) for _ in range(n_chunks)]
    for ci in range(n_chunks):
        rows = slice(ci * c, (ci + 1) * c)
        for pr in range(GLA_HEADS // 2):
            ls = slice(pr * 2 * GLA_DK, (pr + 1) * 2 * GLA_DK)
            acc = jnp.zeros((c, 2 * c), F32)
            for l, (qt, kta, ktb) in enumerate(factors):
                rhs = jnp.concatenate([kta[rows, ls], ktb[rows, ls]], axis=0)
                acc = jnp.where(lv2 == l, _dot_nt(qt[rows, ls], rhs), acc)
            a[ci][pr] = acc

    for ci in range(n_chunks):
        rows = slice(ci * c, (ci + 1) * c)
        v = vr_ref[rows, 0:GLA_V]
        s_in, s_out = (s0_ref.at[ci], sout_ref.at[ci]) if streams else (s_scr, s_scr)
        s_all = s_in[...]
        s_bd = jnp.concatenate(
            [jnp.where((row_k // GLA_DK) == h, s_all, 0.0).astype(BF16) for h in range(GLA_HEADS)], axis=1)
        o_inter = _dot(qe[rows], s_bd)
        u_all = _dot_tn(kl[rows], v)
        dcol = jnp.exp2(jnp.broadcast_to(b[ci * c + c - 1:(ci + 1) * c, :], (LANES, GLA_QK)).T)
        for h in range(GLA_HEADS):
            vs = slice(h * GLA_DV, (h + 1) * GLA_DV)
            ks = slice(h * GLA_DK, (h + 1) * GLA_DK)
            a_h = a[ci][h // 2][:, (h % 2) * c:(h % 2 + 1) * c]
            o = _dot(a_h.astype(BF16), v[:, vs]) + o_inter[:, vs]
            on = _rms(o, gn_ref[:, vs])
            gate = vr_ref[rows, GLA_V + h * GLA_DV:GLA_V + (h + 1) * GLA_DV]
            o_ref[rows, vs] = (on * gate.astype(F32)).astype(o_ref.dtype)
            s_out[ks, :] = dcol[ks, :] * s_all[ks, :] + u_all[ks, vs]

    if not streams:
        @pl.when(step == pl.num_programs(1) - 1)
        def _():
            sout_ref[...] = s_scr[...]


def _gla_call(qk, vr, plr, s0, wg2_p, bg, gn, c, t):
    seqs, seq_len, _ = qk.shape
    tri, lv2 = _gla_consts(c)
    has_state = s0 is not None
    streams = has_state and seq_len == c
    if streams:
        qk, vr, plr = (a.reshape(1, seqs * seq_len, a.shape[-1]) for a in (qk, vr, plr))
        t = seqs * seq_len
    b, l, _ = qk.shape
    tile = lambda w: pl.BlockSpec((None, t, w), lambda bi, i: (bi, i, 0))
    if streams:
        state_spec = pl.BlockSpec((seqs, GLA_QK, GLA_DV), lambda bi, i: (0, 0, 0))
    else:
        state_spec = pl.BlockSpec((None, GLA_QK, GLA_DV), lambda bi, i: (bi, 0, 0))
    in_specs = [tile(2 * GLA_QK), tile(2 * GLA_V), tile(LANES)]
    args = [qk, vr, plr]
    if has_state:
        in_specs.append(state_spec)
        args.append(s0)
    consts = [wg2_p, bg, gn, tri, lv2]
    in_specs += [_full_spec(x.shape) for x in consts]
    out, state = pl.pallas_call(
        functools.partial(_gla_kernel, c=c, t=t, has_state=has_state, streams=streams),
        grid=(b, l // t),
        in_specs=in_specs,
        out_specs=[tile(GLA_V), state_spec],
        out_shape=[jax.ShapeDtypeStruct((b, l, GLA_V), BF16),
                   jax.ShapeDtypeStruct((seqs, GLA_QK, GLA_DV), F32)],
        scratch_shapes=[pltpu.VMEM((GLA_QK, GLA_DV), F32)],
        compiler_params=pltpu.CompilerParams(
            dimension_semantics=("arbitrary", "arbitrary"), vmem_limit_bytes=VMEM_LIMIT),
        name="gla",
    )(*args, *consts)
    return out.reshape(seqs, seq_len, GLA_V), state


def _mlstm_kernel(*refs, c, t, has_state, streams):
    if has_state:
        (mqk_ref, mvo_ref, pif_ref, c0_ref, n0_ref, m0_ref, cv0_ref, cw_ref, cb_ref, bif_ref, tri_ref, sel_ref,
         o_ref, cout_ref, nout_ref, mout_ref, cvout_ref, c_scr, m_scr, cv_scr, qk_scr) = refs
    else:
        (mqk_ref, mvo_ref, pif_ref, cw_ref, cb_ref, bif_ref, tri_ref, sel_ref,
         o_ref, cout_ref, nout_ref, mout_ref, cvout_ref, c_scr, m_scr, cv_scr, qk_scr) = refs
    step = pl.program_id(1)
    hist = CONV_W - 1

    n_chunks = t // c

    def with_n(c_mat, n_row):
        return jnp.concatenate([c_mat, jnp.broadcast_to(n_row, (ML_DH, ML_DH)).T], axis=1)

    def conv_rows(rows):
        conv = _causal_conv_silu(cv_scr, mqk_ref[rows, :], cw_ref, cb_ref)
        qk_scr[rows, 0:ML_W] = conv[:, 0:ML_W].astype(BF16)
        qk_scr[rows, ML_W:] = (conv[:, ML_W:] * (ML_DH ** -0.5)).astype(BF16)

    if streams:
        for ci in range(n_chunks):
            cv_scr[0:CONV_PAD - hist, :] = jnp.zeros((CONV_PAD - hist, 2 * ML_W), F32)
            cv_scr[CONV_PAD - hist:CONV_PAD, :] = cv0_ref[ci]
            conv_rows(slice(ci * c, (ci + 1) * c))
            cvout_ref[ci] = mqk_ref[(ci + 1) * c - hist:(ci + 1) * c, :]
    else:
        @pl.when(step == 0)
        def _():
            if has_state:
                for h in range(ML_HEADS):
                    c_scr[h] = with_n(c0_ref[h], n0_ref[h])
                m_scr[...] = m0_ref[...] * math.log2(math.e)
                cv_scr[0:CONV_PAD - hist, :] = jnp.zeros((CONV_PAD - hist, 2 * ML_W), F32)
                cv_scr[CONV_PAD - hist:CONV_PAD, :] = cv0_ref[...]
            else:
                c_scr[...] = jnp.zeros_like(c_scr)
                m_scr[...] = jnp.zeros_like(m_scr)
                cv_scr[0:CONV_PAD, :] = jnp.zeros((CONV_PAD, 2 * ML_W), F32)

        conv_rows(slice(0, t))

    log2e = math.log2(math.e)
    pre = pif_ref[...] + bif_ref[...]
    gts = pre * log2e
    flog = pltpu.roll(_log_sigmoid(pre) * log2e, LANES - ML_HEADS, axis=1)

    lane = lax.broadcasted_iota(jnp.int32, (t, LANES), 1)
    row_c = lax.broadcasted_iota(jnp.int32, (t, LANES), 0) % c
    causal = (lax.broadcasted_iota(jnp.int32, (c, c), 1) <= lax.broadcasted_iota(jnp.int32, (c, c), 0))
    ones_v = jnp.ones((c, ML_DH), BF16)

    def slots(pieces):
        out = jnp.zeros((t, LANES), F32)
        for j, piece in enumerate(pieces):
            if not isinstance(piece, float) and j > 0:
                piece = pltpu.roll(piece, ML_HEADS * j, axis=1)
            out = jnp.where((lane >= ML_HEADS * j) & (lane < ML_HEADS * (j + 1)), piece, out)
        return out

    def split3f(x):
        return [p.astype(F32) for p in _split3(x)]

    def per_chunk(rows_of):
        return jnp.concatenate([jnp.broadcast_to(rows_of(ci), (c, LANES)) for ci in range(n_chunks)], axis=0)

    bc = jnp.concatenate([_dot_exact_lhs(tri_ref[...], flog[ci * c:(ci + 1) * c])
                          for ci in range(n_chunks)], axis=0)
    w = gts - bc
    cm = w
    for j in range(int(math.log2(c))):
        sh = 1 << j
        cm = jnp.where(row_c >= sh, jnp.maximum(cm, pltpu.roll(cm, sh, axis=0)), cm)
    m_in, m_out = [], []
    for ci in range(n_chunks):
        last = slice(ci * c + c - 1, (ci + 1) * c)
        if streams:
            m_in.append(m0_ref[ci][0:1, :] * log2e)
        else:
            m_in.append(m_scr[0:1, :] if ci == 0 else m_out[ci - 1])
        m_out.append(bc[last, :] + jnp.maximum(m_in[ci], cm[last, :]))
        if streams:
            mout_ref[ci] = jnp.broadcast_to(m_out[ci] * (1.0 / log2e), mout_ref.shape[1:])
    if not streams:
        m_scr[...] = jnp.broadcast_to(m_out[-1], m_scr.shape)
    mprev = per_chunk(lambda ci: m_in[ci])
    g = jnp.maximum(mprev, cm)
    g_last = per_chunk(lambda ci: g[ci * c + c - 1:(ci + 1) * c, :])
    lhs_all = slots([1.0, 1.0, 1.0] + split3f(g))
    rhs_all = slots(split3f(w) + [-1.0, -1.0, -1.0]).astype(BF16)
    y_all = slots(split3f(mprev - g) + split3f(-(bc + g)) + split3f(w - g_last)).astype(BF16)
    lhs_heads = [jnp.where(lane % ML_HEADS == h, lhs_all, 0.0).astype(BF16) for h in range(ML_HEADS)]

    for ci in range(n_chunks):
        r0 = ci * c
        rows = pl.ds(r0, c)
        for h in range(ML_HEADS):
            hs = slice(h * ML_DH, (h + 1) * ML_DH)
            qh = qk_scr[rows, h * ML_DH:(h + 1) * ML_DH]
            kh = qk_scr[rows, ML_W + h * ML_DH:ML_W + (h + 1) * ML_DH]
            vaug = jnp.concatenate(
                [mvo_ref[rows, h * ML_DH:(h + 1) * ML_DH], ones_v],
                axis=1)
            p = jnp.exp2(jnp.where(causal, _dot_nt(lhs_heads[h][r0:r0 + c], rhs_all[r0:r0 + c]), -jnp.inf))
            bx = jnp.exp2(_dot(y_all[r0:r0 + c], sel_ref[h]))
            w_inter = bx[:, 0:ML_DH]
            e_mt = bx[:, ML_DH:2 * ML_DH]
            w_state = bx[:, 2 * ML_DH:3 * ML_DH]
            s = (_dot_nt(qh, kh) * p).astype(BF16)
            caug = with_n(c0_ref[ci, h], n0_ref[ci, h]) if streams else c_scr[h]
            nd = _dot(s, vaug) + jnp.concatenate([w_inter, w_inter], axis=1) * _dot(qh, caug.astype(BF16))
            hh = nd[:, 0:ML_DH] / jnp.maximum(jnp.abs(nd[:, ML_DH:]), e_mt)
            o_gate = mvo_ref[rows, ML_W + h * ML_DH:ML_W + (h + 1) * ML_DH]
            o_ref[rows, hs] = (o_gate.astype(F32) * hh).astype(o_ref.dtype)
            ks = (kh.astype(F32) * w_state).astype(BF16)
            dec = w_inter[c - 1:c, :]
            new = jnp.concatenate([dec, dec], axis=1) * caug + _dot_tn(ks, vaug)
            if streams:
                cout_ref[ci, h] = new[:, 0:ML_DH]
                nout_ref[ci, h] = new[:, ML_DH:].T[0:8, :]
            else:
                c_scr[h] = new

    if not streams:
        @pl.when(step == pl.num_programs(1) - 1)
        def _():
            for h in range(ML_HEADS):
                cout_ref[h] = c_scr[h, :, 0:ML_DH]
                nout_ref[h] = c_scr[h, :, ML_DH:].T[0:8, :]
            mout_ref[...] = m_scr[...] * (1.0 / log2e)
            cvout_ref[...] = mqk_ref[t - hist:t, :]


def _mlstm_call(mqk, mvo, pif, c0, n0, m0, cv0, cw, cb, bif, c, t):
    seqs, seq_len, _ = mqk.shape
    has_state = c0 is not None
    streams = has_state and seq_len == c
    if streams:
        mqk, mvo, pif = (a.reshape(1, seqs * seq_len, a.shape[-1]) for a in (mqk, mvo, pif))
        t = seqs * seq_len
    b, l, _ = mqk.shape
    tri =jnp.asarray(np.tril(np.ones((c, c), np.float32)), BF16)
    sel = np.zeros((ML_HEADS, LANES, 3 * ML_DH), np.float32)
    for h in range(ML_HEADS):
        for slot in range(9):
            sel[h, ML_HEADS * slot + h, (slot // 3) * ML_DH:(slot // 3 + 1) * ML_DH] = 1.0
    sel = jnp.asarray(sel, BF16)
    tile = lambda w: pl.BlockSpec((None, t, w), lambda bi, i: (bi, i, 0))
    if streams:
        heads = lambda *shape: _full_spec((seqs, ML_HEADS) + shape)
        m_spec = _full_spec((seqs, 8, LANES))
        cv_spec = _full_spec((seqs, CONV_W - 1, 2 * ML_W))
    else:
        heads = lambda *shape: pl.BlockSpec((None, ML_HEADS) + shape, lambda bi, i: (bi, 0, 0, 0))
        m_spec = pl.BlockSpec((None, 8, LANES), lambda bi, i: (bi, 0, 0))
        cv_spec = pl.BlockSpec((None, CONV_W - 1, 2 * ML_W), lambda bi, i: (bi, 0, 0))
    c_spec = heads(ML_DH, ML_DH)
    in_specs = [tile(2 * ML_W), tile(2 * ML_W), tile(LANES)]
    args = [mqk, mvo, pif]
    consts = [cw, cb, bif, tri, sel]
    out_specs = [tile(ML_W), c_spec, heads(8, ML_DH), m_spec, cv_spec]
    out_shape = [jax.ShapeDtypeStruct((b, l, ML_W), BF16),
                 jax.ShapeDtypeStruct((seqs, ML_HEADS, ML_DH, ML_DH), F32),
                 jax.ShapeDtypeStruct((seqs, ML_HEADS, 8, ML_DH), F32),
                 jax.ShapeDtypeStruct((seqs, 8, LANES), F32),
                 jax.ShapeDtypeStruct((seqs, CONV_W - 1, 2 * ML_W), F32)]
    scratch = [pltpu.VMEM((ML_HEADS, ML_DH, 2 * ML_DH), F32), pltpu.VMEM((8, LANES), F32),
               pltpu.VMEM((CONV_PAD + t, 2 * ML_W), F32), pltpu.VMEM((t, 2 * ML_W), BF16)]
    if has_state:
        in_specs += [c_spec, heads(1, ML_DH), m_spec, cv_spec]
        args += [c0, n0, m0, cv0]
    in_specs += [_full_spec(x.shape) for x in consts]
    out, *states = pl.pallas_call(
        functools.partial(_mlstm_kernel, c=c, t=t, has_state=has_state, streams=streams),
        grid=(b, l // t),
        in_specs=in_specs,
        out_specs=out_specs,
        out_shape=out_shape,
        scratch_shapes=scratch,
        compiler_params=pltpu.CompilerParams(
            dimension_semantics=("arbitrary", "arbitrary"), vmem_limit_bytes=VMEM_LIMIT),
        name="mlstm",
    )(*args, *consts)
    return (out.reshape(seqs, seq_len, ML_W), *states)


def _merge_kernel(x_ref, ga_ref, hb_ref, pmg_ref, wug_ref, wum_ref, wo_ref, g2_ref,
                  wr_hi_ref, wr_lo_ref, br_ref, tril_ref, triu_ref, cnt0_ref,
                  x1_ref, hs_ref, meta_ref, tab_ref, cnt_scr):
    ya = _dot(ga_ref[...], wug_ref[...])
    yb = _dot(hb_ref[...], wum_ref[...])
    z = pmg_ref[:, 0:D_MODEL].astype(F32) * ya + pmg_ref[:, D_MODEL:].astype(F32) * yb
    x1 = x_ref[...] + _dot(z.astype(BF16), wo_ref[...])
    x1_ref[...] = x1
    hm = _rms(x1, g2_ref[...])
    hm_hi = hm.astype(BF16)
    hm_lo = (hm - hm_hi.astype(F32)).astype(BF16)
    hi_both = _dot(hm_hi, jnp.concatenate([wr_hi_ref[...], wr_lo_ref[...]], axis=1))
    logits = hi_both[:, 0:LANES] + hi_both[:, LANES:] + _dot(hm_lo, wr_hi_ref[...]) + br_ref[...]
    tm = logits.shape[0]
    n_out = -(-(N_EXPERTS + N_GROUPS) // 8) * 8
    lt = logits.T[0:n_out, :]
    out = lax.broadcasted_iota(jnp.int32, lt.shape, 0)
    neg = -jnp.inf
    is_g = (out >= N_EXPERTS) & (out < N_EXPERTS + N_GROUPS)
    lg = jnp.where(is_g, lt, neg)
    mg = jnp.max(lg, axis=0, keepdims=True)
    p_top = 1.0 / jnp.sum(jnp.exp(lg - mg), axis=0, keepdims=True)
    gi = jnp.min(jnp.where(lg == mg, out, 2 * LANES), axis=0, keepdims=True) - N_EXPERTS
    group_shift = int(math.log2(EXPERTS_PER_GROUP))
    sel = (out < N_EXPERTS) & (jnp.right_shift(out, group_shift) == gi)
    le = jnp.where(sel, lt, neg)
    v1 = jnp.max(le, axis=0, keepdims=True)
    i1 = jnp.min(jnp.where(le == v1, out, 2 * LANES), axis=0, keepdims=True)
    le2 = jnp.where(out == i1, neg, le)
    v2 = jnp.max(le2, axis=0, keepdims=True)
    i2 = jnp.min(jnp.where(le2 == v2, out, 2 * LANES), axis=0, keepdims=True)
    e2 = jnp.exp(v2 - v1)
    per_token = jnp.concatenate(
        [i1.astype(F32), i2.astype(F32), p_top * (1.0 / (1.0 + e2)), p_top * (e2 / (1.0 + e2)),
         jnp.zeros((LANES - 4, tm), F32)], axis=0).T
    i1 = per_token[:, 0:1].astype(jnp.int32)
    i2 = per_token[:, 1:2].astype(jnp.int32)
    w1 = per_token[:, 2:3]
    w2 = per_token[:, 3:4]
    lane = lax.broadcasted_iota(jnp.int32, logits.shape, 1)

    @pl.when(pl.program_id(0) == 0)
    def _():
        cnt_scr[...] = cnt0_ref[...]

    oh1 = lane == i1
    oh2 = lane == i2
    both = jnp.where(oh1 | oh2, 1.0, 0.0)
    cnt = jnp.sum(both, axis=0, keepdims=True)
    cnt = jnp.floor((cnt + (SEG_ALIGN - 1)) * (1.0 / SEG_ALIGN)) * SEG_ALIGN
    lower = _dot(jnp.broadcast_to(cnt, (8, LANES)).astype(BF16), triu_ref[...])[0:1, :]
    lpos = _dot(tril_ref[...], both.astype(BF16)) + lower
    lp1 = jnp.sum(jnp.where(oh1, lpos, 0.0), axis=-1, keepdims=True)
    lp2 = jnp.sum(jnp.where(oh2, lpos, 0.0), axis=-1, keepdims=True)
    pos = lax.broadcasted_iota(jnp.int32, (tm, hs_ref.shape[0]), 1)
    onehot = jnp.where((pos == lp1.astype(jnp.int32)) | (pos == lp2.astype(jnp.int32)), 1.0, 0.0)
    hs_ref[...] = _dot_tn(onehot.astype(BF16), hm_hi).astype(BF16)
    cols = (lp1, lp2, w1, w2)
    meta = jnp.zeros((tm, LANES), F32)
    for ci, col in enumerate(cols):
        meta = jnp.where(lane == ci, col, meta)
    meta_ref[...] = meta
    row8 = lax.broadcasted_iota(jnp.int32, (8, LANES), 0)
    tab_ref[...] = jnp.where(row8 == 0, cnt, jnp.where(row8 == 1, lower, jnp.where(row8 == 2, cnt_scr[...], 0.0)))
    cnt_scr[...] += cnt


def _merge_call(x2, ga, hb, pmg, wug, wum, wo, g2, wr_hi, wr_lo, br, cnt0, tm):
    n = x2.shape[0]
    tile = lambda w: pl.BlockSpec((tm, w), lambda i: (i, 0))
    tiles = lambda *shape: pl.BlockSpec((None,) + shape, lambda i: (i, 0, 0))
    local_rows = 2 * tm + SEG_ALIGN * N_EXPERTS
    consts = [wug, wum, wo, g2, wr_hi, wr_lo, br,
              jnp.asarray(np.tril(np.ones((tm, tm), np.float32), -1), BF16),
              jnp.asarray(np.triu(np.ones((LANES, LANES), np.float32), 1), BF16),
              cnt0]
    return pl.pallas_call(
        _merge_kernel,
        grid=(n // tm,),
        in_specs=[tile(D_MODEL), tile(GLA_V), tile(ML_W), tile(W_MG)]
                 + [_full_spec(x.shape) for x in consts],
        out_specs=[tile(D_MODEL), tiles(local_rows, D_MODEL), tile(LANES), tiles(8, LANES)],
        out_shape=[jax.ShapeDtypeStruct((n, D_MODEL), F32),
                   jax.ShapeDtypeStruct((n // tm, local_rows, D_MODEL), BF16),
                   jax.ShapeDtypeStruct((n, LANES), F32),
                   jax.ShapeDtypeStruct((n // tm, 8, LANES), F32)],
        scratch_shapes=[pltpu.VMEM((1, LANES), F32)],
        compiler_params=pltpu.CompilerParams(
            dimension_semantics=("arbitrary",), vmem_limit_bytes=VMEM_LIMIT),
        name="merge",
    )(x2, ga, hb, pmg, *consts)


MOE_ROWS = 512
SEG_ALIGN = 16


def _segment_copies(fn, tables, tile, local_ref, buffer, flat_ref, sem, to_flat):
    pieces_ref, gdst_ref = tables
    stride = gdst_ref.shape[0] // pieces_ref.shape[0]

    def body(p, carry):
        local = local_ref.at[buffer, pl.ds(pl.multiple_of(p * SEG_ALIGN, SEG_ALIGN), SEG_ALIGN), :]
        flat = flat_ref.at[pl.ds(pl.multiple_of(gdst_ref[tile * stride + p], SEG_ALIGN), SEG_ALIGN), :]
        src, dst = (local, flat) if to_flat else (flat, local)
        fn(pltpu.make_async_copy(src, dst, sem))
        return carry

    lax.fori_loop(0, pieces_ref[tile], body, 0)


def _gmm_kernel(te_ref, nv_ref, part_ref, row_ref, *refs, n_parts):
    hs_refs = refs[:n_parts]
    wg_ref, wu_ref, wd_ref, o_ref, x_buf, sem, wgu_scr, wd_scr = refs[n_parts:]
    j = pl.program_id(0)
    n_valid = nv_ref[0]
    used = j < n_valid
    pieces_per_tile = MOE_ROWS // SEG_ALIGN

    def piece_copy(hs_ref, row, slot, k):
        src = hs_ref.at[pl.ds(pl.multiple_of(row, SEG_ALIGN), SEG_ALIGN), :]
        return pltpu.make_async_copy(src, x_buf.at[slot, pl.ds(k * SEG_ALIGN, SEG_ALIGN), :], sem.at[slot])

    def fetch(tile):
        slot = tile % 2
        for k in range(pieces_per_tile):
            g = tile * pieces_per_tile + k
            for part, hs_ref in enumerate(hs_refs):
                @pl.when(part_ref[g] == part)
                def _():
                    piece_copy(hs_ref, row_ref[g], slot, k).start()

    def wait_tile(tile):
        slot = tile % 2
        for k in range(pieces_per_tile):
            piece_copy(hs_refs[0], 0, slot, k).wait()

    @pl.when(j == 0)
    def _():
        fetch(j)

    @pl.when(j + 1 < n_valid)
    def _():
        fetch(j + 1)

    @pl.when(used & ((j == 0) | (te_ref[j] != te_ref[jnp.maximum(j - 1, 0)])))
    def _():
        wgu_scr[:, 0:D_EXPERT] = wg_ref[...].astype(BF16)
        wgu_scr[:, D_EXPERT:] = wu_ref[...].astype(BF16)
        wd_scr[...] = wd_ref[...].astype(BF16)

    @pl.when(used)
    def _():
        wait_tile(j)
        au = _dot(x_buf[j % 2], wgu_scr[...])
        a = au[:, 0:D_EXPERT]
        hh = (a * _sigmoid(a)) * au[:, D_EXPERT:]
        o_ref[...] = _dot(hh.astype(BF16), wd_scr[...]).astype(o_ref.dtype)

    @pl.when(jnp.logical_not(used))
    def _():
        o_ref[...] = jnp.zeros_like(o_ref)


def _gmm_call(tile_expert, n_valid, sources, hs_parts, wg, wu, wd):
    n_tiles = tile_expert.shape[0]
    wsel = lambda j, te, *_: (te[j], 0, 0)
    any_spec = pl.BlockSpec(memory_space=pl.ANY)
    return pl.pallas_call(
        functools.partial(_gmm_kernel, n_parts=len(hs_parts)),
        grid_spec=pltpu.PrefetchScalarGridSpec(
            num_scalar_prefetch=4,
            grid=(n_tiles,),
            in_specs=[any_spec] * len(hs_parts)
                     + [pl.BlockSpec((None, D_MODEL, D_EXPERT), wsel),
                        pl.BlockSpec((None, D_MODEL, D_EXPERT), wsel),
                        pl.BlockSpec((None, D_EXPERT, D_MODEL), wsel)],
            out_specs=pl.BlockSpec((MOE_ROWS, D_MODEL), lambda j, *_: (j, 0)),
            scratch_shapes=[pltpu.VMEM((2, MOE_ROWS, D_MODEL), BF16),
                            pltpu.SemaphoreType.DMA((2,)),
                            pltpu.VMEM((D_MODEL, 2 * D_EXPERT), BF16),
                            pltpu.VMEM((D_EXPERT, D_MODEL), BF16)],
        ),
        out_shape=jax.ShapeDtypeStruct((n_tiles * MOE_ROWS, D_MODEL), BF16),
        compiler_params=pltpu.CompilerParams(
            dimension_semantics=("arbitrary",), vmem_limit_bytes=VMEM_LIMIT),
        name="moe_grouped",
    )(tile_expert, n_valid, *sources, *hs_parts, wg, wu, wd)


def _combine_kernel(pieces_ref, gdst_ref, x1_ref, meta_ref, gf_ref, os_ref, y_ref, buf_ref, sem, *, tile0):
    step = pl.program_id(0)
    n_steps = pl.num_programs(0)
    tm = x1_ref.shape[0]
    tables = (pieces_ref, gdst_ref)

    def fetch(fn, i):
        slot = i % 2
        _segment_copies(fn, tables, tile0 + i, buf_ref, slot, os_ref, sem.at[slot], False)

    @pl.when(step == 0)
    def _():
        buf_ref[...] = jnp.zeros_like(buf_ref)
        fetch(lambda c: c.start(), step)

    @pl.when(step + 1 < n_steps)
    def _():
        fetch(lambda c: c.start(), step + 1)

    fetch(lambda c: c.wait(), step)
    rows = buf_ref[step % 2]
    pos = lax.broadcasted_iota(jnp.int32, (tm, rows.shape[0]), 1)
    lp1 = meta_ref[:, 0:1].astype(jnp.int32)
    lp2 = meta_ref[:, 1:2].astype(jnp.int32)
    q = jnp.where(pos == lp1, meta_ref[:, 2:3], 0.0) + jnp.where(pos == lp2, meta_ref[:, 3:4], 0.0)
    y = x1_ref[...] + _dot(q.astype(BF16), rows)
    y_ref[...] = _rms(y, gf_ref[...])


def _combine_call(tables, x1, meta, gf, out_sorted, tile0, local_rows, tm):
    n = x1.shape[0]
    tile = lambda w: pl.BlockSpec((tm, w), lambda i, *_: (i, 0))
    return pl.pallas_call(
        functools.partial(_combine_kernel, tile0=tile0),
        grid_spec=pltpu.PrefetchScalarGridSpec(
            num_scalar_prefetch=2,
            grid=(n // tm,),
            in_specs=[tile(D_MODEL), tile(LANES), pl.BlockSpec(gf.shape, lambda i, *_: (0, 0)),
                      pl.BlockSpec(memory_space=pl.ANY)],
            out_specs=tile(D_MODEL),
            scratch_shapes=[pltpu.VMEM((2, local_rows, D_MODEL), out_sorted.dtype),
                            pltpu.SemaphoreType.DMA((2,))],
        ),
        out_shape=jax.ShapeDtypeStruct((n, D_MODEL), F32),
        compiler_params=pltpu.CompilerParams(
            dimension_semantics=("arbitrary",), vmem_limit_bytes=VMEM_LIMIT),
        name="moe_combine",
    )(*tables, x1, meta, gf, out_sorted)


def _sparse_moe(parts, p):
    token_tiles = [part[1].shape[0] for part in parts]
    n_tiles = -(-sum(part[1].shape[0] * part[1].shape[1] for part in parts) // MOE_ROWS) + N_EXPERTS
    tab = jnp.concatenate([part[3] for part in parts], axis=0)
    tab = tab[:, :, :N_EXPERTS].astype(jnp.int32)
    cnt, before = tab[:, 0], tab[:, 2]
    total = before[-1] + cnt[-1]
    tiles = (total + (MOE_ROWS - 1)) // MOE_ROWS
    tile_end = jnp.cumsum(tiles)
    tile_start = tile_end - tiles
    n_valid = tile_end[-1:]
    gpos = tile_start[None, :] * MOE_ROWS + before
    j = jnp.minimum(jnp.arange(n_tiles, dtype=jnp.int32), n_valid - 1)
    tile_expert = jnp.sum((j[:, None] >= tile_end[None, :]).astype(jnp.int32), axis=1)
    max_pieces = max(part[1].shape[1] for part in parts) // SEG_ALIGN
    pieces = cnt // SEG_ALIGN
    piece_end = jnp.cumsum(pieces, axis=1)
    piece_start = piece_end - pieces
    piece = jnp.arange(max_pieces, dtype=jnp.int32)
    expert_of = jnp.sum((piece[None, :, None] >= piece_end[:, None, :]).astype(jnp.int32), axis=2)
    onehot = expert_of[:, :, None] == jnp.arange(N_EXPERTS, dtype=jnp.int32)
    first_row = gpos - SEG_ALIGN * piece_start
    gdst = jnp.sum(jnp.where(onehot, first_row[:, None, :], 0), axis=2) + SEG_ALIGN * piece[None, :]
    tables = (piece_end[:, -1], gdst.reshape(-1))
    n_token_tiles = sum(token_tiles)
    over_tiles = jnp.cumsum(pieces, axis=0)
    e_j = tile_expert
    q0 = (jnp.arange(n_tiles, dtype=jnp.int32) - jnp.take(tile_start, e_j)) * (MOE_ROWS // SEG_ALIGN)
    q = q0[:, None] + jnp.arange(MOE_ROWS // SEG_ALIGN, dtype=jnp.int32)[None, :]
    valid = (jnp.arange(n_tiles)[:, None] < n_valid) & (q < jnp.take(over_tiles[-1], e_j)[:, None])
    per_tile = lambda table: jnp.take(table.T, e_j, axis=0)[:, None, :]
    src_tile = jnp.sum((q[:, :, None] >= per_tile(over_tiles)).astype(jnp.int32), axis=2)
    src_tile = jnp.minimum(src_tile, n_token_tiles - 1)
    at_src = lambda table: jnp.sum(jnp.where(
        src_tile[:, :, None] == jnp.arange(n_token_tiles, dtype=jnp.int32), per_tile(table), 0), axis=2)
    src_row = (at_src(piece_start) + q - at_src(over_tiles - pieces)) * SEG_ALIGN
    part_start = np.cumsum([0] + token_tiles)
    src_part = jnp.sum((src_tile[:, :, None] >= jnp.asarray(part_start[1:], jnp.int32)).astype(jnp.int32), axis=2)
    src_tile_in_part = src_tile - jnp.sum(jnp.where(
        src_part[:, :, None] == jnp.arange(len(parts), dtype=jnp.int32), jnp.asarray(part_start[:-1], jnp.int32), 0), axis=2)
    local_rows = jnp.asarray([part[1].shape[1] for part in parts], jnp.int32)
    rows_of_part = jnp.sum(jnp.where(src_part[:, :, None] == jnp.arange(len(parts), dtype=jnp.int32), local_rows, 0), axis=2)
    flat_row = src_tile_in_part * rows_of_part + src_row
    zero_row = parts[0][1].shape[1] - SEG_ALIGN
    assert zero_row >= 2 * (parts[0][0].shape[0] // parts[0][1].shape[0]) + (SEG_ALIGN - 1) * N_EXPERTS
    sources = (jnp.where(valid, src_part, 0).reshape(-1), jnp.where(valid, flat_row, zero_row).reshape(-1))
    out_sorted = _gmm_call(tile_expert, n_valid.astype(jnp.int32), sources,
                           [part[1].reshape(-1, D_MODEL) for part in parts], p["wg"], p["wu"], p["wd"])
    return [_combine_call(tables, part[0], part[2], p["gf"], out_sorted, int(tile0),
                          part[1].shape[1], part[0].shape[0] // part[1].shape[0])
            for part, tile0 in zip(parts, part_start[:-1])]


def _pad_cols(w, width):
    return jnp.pad(w, ((0, 0), (0, width - w.shape[1])))


def _prep_weights(norm1_g, w_in, gla_w_gate2, gla_b_gate, gla_norm_g, w_up_gla,
                  ml_conv_w, ml_conv_b, ml_b_i, ml_b_f, w_up_ml, w_out,
                  norm2_g, router_g_w, router_g_b, router_e_w, router_e_b,
                  moe_w_gate, moe_w_up, moe_w_down, final_g):
    wr =_pad_cols(jnp.concatenate([router_e_w, router_g_w], axis=1), LANES)
    wr_hi = wr.astype(BF16)
    wr_lo = (wr - wr_hi.astype(F32)).astype(BF16)
    br = _pad_cols(jnp.concatenate([router_e_b, router_g_b])[None, :], LANES)
    wg2 = jnp.pad(gla_w_gate2, ((0, LANES - GLA_GATE_RANK), (0, 0)))
    wg2_hi = wg2.astype(BF16)
    return dict(
        g1=norm1_g[None, :], w_in=w_in.T,
        wg2_p=jnp.stack([wg2_hi, (wg2 - wg2_hi.astype(F32)).astype(BF16)]),
        bg=gla_b_gate[None, :], gn=gla_norm_g[None, :],
        wug=w_up_gla.astype(BF16),
        cw=ml_conv_w, cb=ml_conv_b[None, :],
        bif=_pad_cols(jnp.concatenate([ml_b_i, ml_b_f])[None, :], LANES),
        wum=w_up_ml.astype(BF16), wo=w_out.astype(BF16),
        g2=norm2_g[None, :], wr_hi=wr_hi, wr_lo=wr_lo, br=br,
        wg=moe_w_gate.reshape(N_EXPERTS, D_MODEL, D_EXPERT),
        wu=moe_w_up.reshape(N_EXPERTS, D_MODEL, D_EXPERT),
        wd=moe_w_down.reshape(N_EXPERTS, D_EXPERT, D_MODEL),
        gf=final_g[None, :],
    )


def _mixers(x, gla_s0, ml_c0, ml_n0, ml_m0, conv0, p, placed, *,
            gla_chunk, ml_chunk, seq_tile, row_tile, merge_tile):
    b, l, _ = x.shape
    n = b * l
    x2 = x.reshape(n, D_MODEL)
    qk, vr, plr, mqk, mvo, pif, pmg = _proj_call(x2, p["g1"], p["w_in"], row_tile)
    r3 = lambda a: a.reshape(b, l, a.shape[-1])
    s0 = None if gla_s0 is None else gla_s0.reshape(b, GLA_QK, GLA_DV)
    ga, gla_s = _gla_call(r3(qk), r3(vr), r3(plr), s0, p["wg2_p"], p["bg"], p["gn"], gla_chunk, seq_tile)
    gla_s = gla_s.reshape(b, GLA_HEADS, GLA_DK, GLA_DV)
    if ml_c0 is None:
        n0 = m0 = None
    else:
        n0 = ml_n0[:, :, None, :]
        m0 = jnp.broadcast_to(_pad_cols(ml_m0, LANES)[:, None, :], (b, 8, LANES))
    hb, ml_c, ml_n, m_b, new_conv = _mlstm_call(r3(mqk), r3(mvo), r3(pif), ml_c0, n0, m0, conv0,
                                                p["cw"], p["cb"], p["bif"], ml_chunk, seq_tile)
    part = _merge_call(x2, ga.reshape(n, GLA_V), hb.reshape(n, ML_W), pmg,
                       p["wug"], p["wum"], p["wo"], p["g2"],
                       p["wr_hi"], p["wr_lo"], p["br"], placed, merge_tile)
    states = (gla_s[None], ml_c[None], ml_n[:, :, 0, :][None], m_b[:, 0, 0:ML_HEADS][None], new_conv[None])
    return part, states


def kernel(x_prompt, x_sample, state_gla_S, state_mlstm_C, state_mlstm_n, state_mlstm_m, state_mlstm_conv, norm1_g, w_in, gla_w_gate2, gla_b_gate, gla_norm_g, w_up_gla, ml_conv_w, ml_conv_b, ml_b_i, ml_b_f, w_up_ml, w_out, norm2_g, router_g_w, router_g_b, router_e_w, router_e_b, moe_w_gate, moe_w_up, moe_w_down, final_g):
    assert norm1_g.shape[0] == 1, "single-layer trunk"
    p = _prep_weights(norm1_g[0], w_in[0], gla_w_gate2[0], gla_b_gate[0], gla_norm_g[0], w_up_gla[0],
                      ml_conv_w[0], ml_conv_b[0], ml_b_i[0], ml_b_f[0], w_up_ml[0], w_out[0],
                      norm2_g[0], router_g_w[0], router_g_b[0], router_e_w[0], router_e_b[0],
                      moe_w_gate[0], moe_w_up[0], moe_w_down[0], final_g)
    dec_seq = x_sample.shape[1]
    n_sample = x_sample.shape[0] * dec_seq
    part_p, sp = _mixers(x_prompt, None, None, None, None, None, p, jnp.zeros((1, LANES), F32),
                         gla_chunk=128, ml_chunk=256, seq_tile=1024, row_tile=256, merge_tile=512)
    placed = part_p[3][-1, 0:1, :] + part_p[3][-1, 2:3, :]
    part_s, ss = _mixers(x_sample, state_gla_S[0], state_mlstm_C[0], state_mlstm_n[0], state_mlstm_m[0],
                         state_mlstm_conv[0], p, placed,
                         gla_chunk=dec_seq, ml_chunk=dec_seq, seq_tile=dec_seq,
                         row_tile=n_sample, merge_tile=n_sample)
    yp, ys = _sparse_moe([part_p, part_s], p)
    return (yp.reshape(x_prompt.shape), ys.reshape(x_sample.shape), *sp, *ss)
```

```python
import functools
import math

import numpy as np
import jax
import jax.numpy as jnp
from jax import lax
from jax.experimental import pallas as pl
from jax.experimental.pallas import tpu as pltpu

D_MODEL = 1024
GLA_HEADS = 4
GLA_DK = 64
GLA_DV = 128
GLA_GATE_RANK = 16
GLA_TAU = 16.0
ML_HEADS = 4
ML_DH = 128
CONV_W = 4
N_GROUPS = 4
EXPERTS_PER_GROUP = 8
N_EXPERTS = N_GROUPS * EXPERTS_PER_GROUP
D_EXPERT = 256
EPS = 1e-6

GLA_QK = GLA_HEADS * GLA_DK
GLA_V = GLA_HEADS * GLA_DV
ML_W = ML_HEADS * ML_DH

LANES = 128
VMEM_LIMIT = 56 * 1024 * 1024

W_GLA = 2 * GLA_QK + 2 * GLA_V
W_ML = 2 * ML_W + ML_W + ML_W
W_MG = 2 * D_MODEL
PROJ_WIDTHS = (W_GLA, LANES, W_ML, LANES, W_MG)
PROJ_SOURCE_WIDTHS = (W_GLA, GLA_GATE_RANK, W_ML, 2 * ML_HEADS, W_MG)

F32 = jnp.float32
BF16 = jnp.bfloat16

PROJ_OUTPUTS = ((2 * GLA_QK, F32), (2 * GLA_V, BF16), (LANES, F32), (2 * ML_W, F32), (2 * ML_W, BF16),
                (LANES, F32), (W_MG, BF16))


def _dot(a, b):
    return jnp.dot(a, b, preferred_element_type=F32)


def _dot_nt(a, b):
    return lax.dot_general(a, b, (((1,), (1,)), ((), ())), preferred_element_type=F32)


def _dot_tn(a, b):
    return lax.dot_general(a, b, (((0,), (0,)), ((), ())), preferred_element_type=F32)


def _split3(x):
    hi = x.astype(BF16)
    r1 = x - hi.astype(F32)
    mid = r1.astype(BF16)
    lo = (r1 - mid.astype(F32)).astype(BF16)
    return hi, mid, lo


def _dot_exact_lhs(m, x):
    hi, mid, lo = _split3(x)
    return _dot(m, hi) + _dot(m, mid) + _dot(m, lo)


def _log_sigmoid(z):
    return jnp.minimum(z, 0.0) - jnp.log(1.0 + jnp.exp(-jnp.abs(z)))


def _sigmoid(z):
    return 1.0 / (1.0 + jnp.exp(-z))


def _rms(x, g):
    return x * lax.rsqrt(jnp.mean(x * x, axis=-1, keepdims=True) + EPS) * g


def _full_spec(shape):
    nd = len(shape)
    return pl.BlockSpec(shape, lambda *_: (0,) * nd)


CONV_PAD = 8


def _causal_conv_silu(stage_ref, x, cw_ref, cb_ref):
    t = x.shape[0]
    stage_ref[CONV_PAD:CONV_PAD + t, :] = x
    acc = cb_ref[...] + stage_ref[CONV_PAD:CONV_PAD + t, :] * cw_ref[CONV_W - 1:CONV_W, :]
    for d in range(1, CONV_W):
        acc = acc + stage_ref[CONV_PAD - d:CONV_PAD - d + t, :] * cw_ref[CONV_W - 1 - d:CONV_W - d, :]
    stage_ref[0:CONV_PAD, :] = stage_ref[t:t + CONV_PAD, :]
    return acc * _sigmoid(acc)


def _proj_kernel(x_ref, g_ref, win_ref, qk_ref, vr_ref, lr_ref, mqk_ref, mvo_ref, if_ref, mg_ref, w_ref):
    starts = np.cumsum((0,) + PROJ_WIDTHS)

    @pl.when(pl.program_id(0) == 0)
    def _():
        src = np.cumsum((0,) + PROJ_SOURCE_WIDTHS)
        chunk = 512
        for g, width in enumerate(PROJ_SOURCE_WIDTHS):
            for r0 in range(0, width, chunk):
                rows = min(chunk, width - r0)
                w_ref[starts[g] + r0:starts[g] + r0 + rows, :] = (
                    win_ref[src[g] + r0:src[g] + r0 + rows, :].astype(BF16))
            if width < PROJ_WIDTHS[g]:
                w_ref[starts[g] + width:starts[g + 1], :] = jnp.zeros((PROJ_WIDTHS[g] - width, D_MODEL), BF16)

    h = _rms(x_ref[...], g_ref[...]).astype(BF16)

    def cols(group, lo, hi):
        return _dot_nt(h, w_ref[starts[group] + lo:starts[group] + hi, :])

    qk_ref[...] = cols(0, 0, 2 * GLA_QK)
    vr_ref[:, 0:GLA_V] = cols(0, 2 * GLA_QK, 2 * GLA_QK + GLA_V).astype(BF16)
    r = cols(0, 2 * GLA_QK + GLA_V, W_GLA)
    vr_ref[:, GLA_V:] = (r * _sigmoid(r)).astype(BF16)
    lr_ref[...] = cols(1, 0, LANES)
    mqk_ref[...] = cols(2, 0, 2 * ML_W)
    mvo_ref[:, 0:ML_W] = cols(2, 2 * ML_W, 3 * ML_W).astype(BF16)
    mvo_ref[:, ML_W:] = _sigmoid(cols(2, 3 * ML_W, W_ML)).astype(BF16)
    if_ref[...] = cols(3, 0, LANES)
    mg_ref[...] = _sigmoid(cols(4, 0, W_MG)).astype(BF16)


def _proj_call(x2, g, w_in, tm):
    n = x2.shape[0]
    assert w_in.shape == (sum(PROJ_SOURCE_WIDTHS), D_MODEL)
    return pl.pallas_call(
        _proj_kernel,
        grid=(n // tm,),
        in_specs=[pl.BlockSpec((tm, D_MODEL), lambda i: (i, 0)),
                  _full_spec(g.shape),
                  pl.BlockSpec(w_in.shape, lambda i: (0, 0), pipeline_mode=pl.Buffered(1))],
        out_specs=[pl.BlockSpec((tm, w), lambda i: (i, 0)) for w, _ in PROJ_OUTPUTS],
        out_shape=[jax.ShapeDtypeStruct((n, w), dt) for w, dt in PROJ_OUTPUTS],
        scratch_shapes=[pltpu.VMEM((sum(PROJ_WIDTHS), D_MODEL), BF16)],
        compiler_params=pltpu.CompilerParams(
            dimension_semantics=("arbitrary",), vmem_limit_bytes=VMEM_LIMIT),
        name="in_proj",
    )(x2, g, w_in)


def _gla_consts(c):
    nlev = int(math.log2(c))
    assert 1 << nlev == c
    t = np.arange(c)[:, None]
    j = np.arange(c)[None, :]
    lv = np.full((c, c), -1, np.int32)
    for l in range(nlev):
        h = c >> (l + 1)
        upper = (t % (2 * h)) >= h
        same = (j // (2 * h)) == (t // (2 * h))
        s_lower = (j % (2 * h)) < h
        lv[np.broadcast_to(upper, (c, c)) & same & s_lower] = l
    lv[np.eye(c, dtype=bool)] = nlev
    tri = (j <= t).astype(np.float32)
    return jnp.asarray(tri, BF16), jnp.asarray(np.concatenate([lv, lv], axis=1))


def _gla_kernel(*refs, c, t, has_state, streams):
    if has_state:
        (qk_ref, vr_ref, plr_ref, s0_ref, wg2_ref, bg_ref, gn_ref, tri_ref, lv_ref,
         o_ref, sout_ref, s_scr) = refs
    else:
        (qk_ref, vr_ref, plr_ref, wg2_ref, bg_ref, gn_ref, tri_ref, lv_ref,
         o_ref, sout_ref, s_scr) = refs
    nlev = int(math.log2(c))
    step = pl.program_id(1)

    if not streams:
        @pl.when(step == 0)
        def _():
            if has_state:
                s_scr[...] = s0_ref[...]
            else:
                s_scr[...] = jnp.zeros_like(s_scr)

    n_chunks = t // c
    lane_k = lax.broadcasted_iota(jnp.int32, (t, GLA_QK), 1)
    first_of_pair = (lane_k % (2 * GLA_DK)) < GLA_DK
    row_k = lax.broadcasted_iota(jnp.int32, (GLA_QK, GLA_DV), 0)
    row_t = lax.broadcasted_iota(jnp.int32, (t, GLA_QK), 0)

    def block_ref(b, blk, idx):
        if blk >= 8:
            b3 = b.reshape(t // blk, blk, GLA_QK)
            return jnp.broadcast_to(b3[:, idx:idx + 1, :], b3.shape).reshape(t, GLA_QK)
        r = row_t % blk
        out = b
        for sh in range(-idx, blk - idx):
            if sh != 0:
                out = jnp.where(r - idx == sh, pltpu.roll(b, sh % t, axis=0), out)
        return out

    q = qk_ref[:, 0:GLA_QK] * (GLA_DK ** -0.5)
    k = qk_ref[:, GLA_QK:2 * GLA_QK]
    glr = plr_ref[...]
    g_hi = glr.astype(BF16)
    g_lo = (glr - g_hi.astype(F32)).astype(BF16)
    z = (_dot(g_hi, wg2_ref[0]) + _dot(g_lo, wg2_ref[0]) + _dot(g_hi, wg2_ref[1])) + bg_ref[...]
    la = _log_sigmoid(z) * (math.log2(math.e) / GLA_TAU)
    b = jnp.concatenate([_dot_exact_lhs(tri_ref[...], la[ci * c:(ci + 1) * c])
                         for ci in range(n_chunks)], axis=0)
    b_last = block_ref(b, c, c - 1)
    qe = (q * jnp.exp2(b)).astype(BF16)
    kl = (k * jnp.exp2(b_last - b)).astype(BF16)

    k_a = jnp.where(first_of_pair, k, 0.0)
    k_b = k - k_a
    lv2 = lv_ref[...]
    factors = []
    for l in range(nlev + 1):
        if l < nlev:
            half = c >> (l + 1)
            d = b - block_ref(b, 2 * half, half - 1)
            e = jnp.exp2(jnp.minimum(d, -d))
            qt, kta, ktb = q * e, k_a * e, k_b * e
        else:
            qt, kta, ktb = q, k_a, k_b
        factors.append((qt.astype(BF16), kta.astype(BF16), ktb.astype(BF16)))
    a = [[None] * (GLA_HEADS // 2) for _ in range(n_chunks)]
    for ci in range(n_chunks):
        rows = slice(ci * c, (ci + 1) * c)
        for pr in range(GLA_HEADS // 2):
            ls = slice(pr * 2 * GLA_DK, (pr + 1) * 2 * GLA_DK)
            acc = jnp.zeros((c, 2 * c), F32)
            for l, (qt, kta, ktb) in enumerate(factors):
                rhs = jnp.concatenate([kta[rows, ls], ktb[rows, ls]], axis=0)
                acc = jnp.where(lv2 == l, _dot_nt(qt[rows, ls], rhs), acc)
            a[ci][pr] = acc

    for ci in range(n_chunks):
        rows = slice(ci * c, (ci + 1) * c)
        v = vr_ref[rows, 0:GLA_V]
        s_in, s_out = (s0_ref.at[ci], sout_ref.at[ci]) if streams else (s_scr, s_scr)
        s_all = s_in[...]
        s_bd = jnp.concatenate(
            [jnp.where((row_k // GLA_DK) == h, s_all, 0.0).astype(BF16) for h in range(GLA_HEADS)], axis=1)
        o_inter = _dot(qe[rows], s_bd)
        u_all = _dot_tn(kl[rows], v)
        dcol = jnp.exp2(jnp.broadcast_to(b[ci * c + c - 1:(ci + 1) * c, :], (LANES, GLA_QK)).T)
        for h in range(GLA_HEADS):
            vs = slice(h * GLA_DV, (h + 1) * GLA_DV)
            ks = slice(h * GLA_DK, (h + 1) * GLA_DK)
            a_h = a[ci][h // 2][:, (h % 2) * c:(h % 2 + 1) * c]
            o = _dot(a_h.astype(BF16), v[:, vs]) + o_inter[:, vs]
            on = _rms(o, gn_ref[:, vs])
            gate = vr_ref[rows, GLA_V + h * GLA_DV:GLA_V + (h + 1) * GLA_DV]
            o_ref[rows, vs] = (on * gate.astype(F32)).astype(o_ref.dtype)
            s_out[ks, :] = dcol[ks, :] * s_all[ks, :] + u_all[ks, vs]

    if not streams:
        @pl.when(step == pl.num_programs(1) - 1)
        def _():
            sout_ref[...] = s_scr[...]


def _gla_call(qk, vr, plr, s0, wg2_p, bg, gn, c, t):
    seqs, seq_len, _ = qk.shape
    tri, lv2 = _gla_consts(c)
    has_state = s0 is not None
    streams = has_state and seq_len == c
    if streams:
        qk, vr, plr = (a.reshape(1, seqs * seq_len, a.shape[-1]) for a in (qk, vr, plr))
        t = seqs * seq_len
    b, l, _ = qk.shape
    tile = lambda w: pl.BlockSpec((None, t, w), lambda bi, i: (bi, i, 0))
    if streams:
        state_spec = pl.BlockSpec((seqs, GLA_QK, GLA_DV), lambda bi, i: (0, 0, 0))
    else:
        state_spec = pl.BlockSpec((None, GLA_QK, GLA_DV), lambda bi, i: (bi, 0, 0))
    in_specs = [tile(2 * GLA_QK), tile(2 * GLA_V), tile(LANES)]
    args = [qk, vr, plr]
    if has_state:
        in_specs.append(state_spec)
        args.append(s0)
    consts = [wg2_p, bg, gn, tri, lv2]
    in_specs += [_full_spec(x.shape) for x in consts]
    out, state = pl.pallas_call(
        functools.partial(_gla_kernel, c=c, t=t, has_state=has_state, streams=streams),
        grid=(b, l // t),
        in_specs=in_specs,
        out_specs=[tile(GLA_V), state_spec],
        out_shape=[jax.ShapeDtypeStruct((b, l, GLA_V), BF16),
                   jax.ShapeDtypeStruct((seqs, GLA_QK, GLA_DV), F32)],
        scratch_shapes=[pltpu.VMEM((GLA_QK, GLA_DV), F32)],
        compiler_params=pltpu.CompilerParams(
            dimension_semantics=("arbitrary", "arbitrary"), vmem_limit_bytes=VMEM_LIMIT),
        name="gla",
    )(*args, *consts)
    return out.reshape(seqs, seq_len, GLA_V), state


def _mlstm_kernel(*refs, c, t, has_state, streams):
    if has_state:
        (mqk_ref, mvo_ref, pif_ref, c0_ref, n0_ref, m0_ref, cv0_ref, cw_ref, cb_ref, bif_ref, tri_ref, sel_ref,
         o_ref, cout_ref, nout_ref, mout_ref, cvout_ref, c_scr, m_scr, cv_scr, qk_scr) = refs
    else:
        (mqk_ref, mvo_ref, pif_ref, cw_ref, cb_ref, bif_ref, tri_ref, sel_ref,
         o_ref, cout_ref, nout_ref, mout_ref, cvout_ref, c_scr, m_scr, cv_scr, qk_scr) = refs
    step = pl.program_id(1)
    hist = CONV_W - 1

    n_chunks = t // c

    def with_n(c_mat, n_row):
        return jnp.concatenate([c_mat, jnp.broadcast_to(n_row, (ML_DH, ML_DH)).T], axis=1)

    def conv_rows(rows):
        conv = _causal_conv_silu(cv_scr, mqk_ref[rows, :], cw_ref, cb_ref)
        qk_scr[rows, 0:ML_W] = conv[:, 0:ML_W].astype(BF16)
        qk_scr[rows, ML_W:] = (conv[:, ML_W:] * (ML_DH ** -0.5)).astype(BF16)

    if streams:
        for ci in range(n_chunks):
            cv_scr[0:CONV_PAD - hist, :] = jnp.zeros((CONV_PAD - hist, 2 * ML_W), F32)
            cv_scr[CONV_PAD - hist:CONV_PAD, :] = cv0_ref[ci]
            conv_rows(slice(ci * c, (ci + 1) * c))
            cvout_ref[ci] = mqk_ref[(ci + 1) * c - hist:(ci + 1) * c, :]
    else:
        @pl.when(step == 0)
        def _():
            if has_state:
                for h in range(ML_HEADS):
                    c_scr[h] = with_n(c0_ref[h], n0_ref[h])
                m_scr[...] = m0_ref[...] * math.log2(math.e)
                cv_scr[0:CONV_PAD - hist, :] = jnp.zeros((CONV_PAD - hist, 2 * ML_W), F32)
                cv_scr[CONV_PAD - hist:CONV_PAD, :] = cv0_ref[...]
            else:
                c_scr[...] = jnp.zeros_like(c_scr)
                m_scr[...] = jnp.zeros_like(m_scr)
                cv_scr[0:CONV_PAD, :] = jnp.zeros((CONV_PAD, 2 * ML_W), F32)

        conv_rows(slice(0, t))

    log2e = math.log2(math.e)
    pre = pif_ref[...] + bif_ref[...]
    gts = pre * log2e
    flog = pltpu.roll(_log_sigmoid(pre) * log2e, LANES - ML_HEADS, axis=1)

    lane = lax.broadcasted_iota(jnp.int32, (t, LANES), 1)
    row_c = lax.broadcasted_iota(jnp.int32, (t, LANES), 0) % c
    causal = (lax.broadcasted_iota(jnp.int32, (c, c), 1) <= lax.broadcasted_iota(jnp.int32, (c, c), 0))
    ones_v = jnp.ones((c, ML_DH), BF16)

    def slots(pieces):
        out = jnp.zeros((t, LANES), F32)
        for j, piece in enumerate(pieces):
            if not isinstance(piece, float) and j > 0:
                piece = pltpu.roll(piece, ML_HEADS * j, axis=1)
            out = jnp.where((lane >= ML_HEADS * j) & (lane < ML_HEADS * (j + 1)), piece, out)
        return out

    def split3f(x):
        return [p.astype(F32) for p in _split3(x)]

    def per_chunk(rows_of):
        return jnp.concatenate([jnp.broadcast_to(rows_of(ci), (c, LANES)) for ci in range(n_chunks)], axis=0)

    bc = jnp.concatenate([_dot_exact_lhs(tri_ref[...], flog[ci * c:(ci + 1) * c])
                          for ci in range(n_chunks)], axis=0)
    w = gts - bc
    cm = w
    for j in range(int(math.log2(c))):
        sh = 1 << j
        cm = jnp.where(row_c >= sh, jnp.maximum(cm, pltpu.roll(cm, sh, axis=0)), cm)
    m_in, m_out = [], []
    for ci in range(n_chunks):
        last = slice(ci * c + c - 1, (ci + 1) * c)
        if streams:
            m_in.append(m0_ref[ci][0:1, :] * log2e)
        else:
            m_in.append(m_scr[0:1, :] if ci == 0 else m_out[ci - 1])
        m_out.append(bc[last, :] + jnp.maximum(m_in[ci], cm[last, :]))
        if streams:
            mout_ref[ci] = jnp.broadcast_to(m_out[ci] * (1.0 / log2e), mout_ref.shape[1:])
    if not streams:
        m_scr[...] = jnp.broadcast_to(m_out[-1], m_scr.shape)
    mprev = per_chunk(lambda ci: m_in[ci])
    g = jnp.maximum(mprev, cm)
    g_last = per_chunk(lambda ci: g[ci * c + c - 1:(ci + 1) * c, :])
    lhs_all = slots([1.0, 1.0, 1.0] + split3f(g))
    rhs_all = slots(split3f(w) + [-1.0, -1.0, -1.0]).astype(BF16)
    y_all = slots(split3f(mprev - g) + split3f(-(bc + g)) + split3f(w - g_last)).astype(BF16)
    lhs_heads = [jnp.where(lane % ML_HEADS == h, lhs_all, 0.0).astype(BF16) for h in range(ML_HEADS)]

    for ci in range(n_chunks):
        r0 = ci * c
        rows = pl.ds(r0, c)
        for h in range(ML_HEADS):
            hs = slice(h * ML_DH, (h + 1) * ML_DH)
            qh = qk_scr[rows, h * ML_DH:(h + 1) * ML_DH]
            kh = qk_scr[rows, ML_W + h * ML_DH:ML_W + (h + 1) * ML_DH]
            vaug = jnp.concatenate(
                [mvo_ref[rows, h * ML_DH:(h + 1) * ML_DH], ones_v],
                axis=1)
            p = jnp.exp2(jnp.where(causal, _dot_nt(lhs_heads[h][r0:r0 + c], rhs_all[r0:r0 + c]), -jnp.inf))
            bx = jnp.exp2(_dot(y_all[r0:r0 + c], sel_ref[h]))
            w_inter = bx[:, 0:ML_DH]
            e_mt = bx[:, ML_DH:2 * ML_DH]
            w_state = bx[:, 2 * ML_DH:3 * ML_DH]
            s = (_dot_nt(qh, kh) * p).astype(BF16)
            caug = with_n(c0_ref[ci, h], n0_ref[ci, h]) if streams else c_scr[h]
            nd = _dot(s, vaug) + jnp.concatenate([w_inter, w_inter], axis=1) * _dot(qh, caug.astype(BF16))
            hh = nd[:, 0:ML_DH] / jnp.maximum(jnp.abs(nd[:, ML_DH:]), e_mt)
            o_gate = mvo_ref[rows, ML_W + h * ML_DH:ML_W + (h + 1) * ML_DH]
            o_ref[rows, hs] = (o_gate.astype(F32) * hh).astype(o_ref.dtype)
            ks = (kh.astype(F32) * w_state).astype(BF16)
            dec = w_inter[c - 1:c, :]
            new = jnp.concatenate([dec, dec], axis=1) * caug + _dot_tn(ks, vaug)
            if streams:
                cout_ref[ci, h] = new[:, 0:ML_DH]
                nout_ref[ci, h] = new[:, ML_DH:].T[0:8, :]
            else:
                c_scr[h] = new

    if not streams:
        @pl.when(step == pl.num_programs(1) - 1)
        def _():
            for h in range(ML_HEADS):
                cout_ref[h] = c_scr[h, :, 0:ML_DH]
                nout_ref[h] = c_scr[h, :, ML_DH:].T[0:8, :]
            mout_ref[...] = m_scr[...] * (1.0 / log2e)
            cvout_ref[...] = mqk_ref[t - hist:t, :]


def _mlstm_call(mqk, mvo, pif, c0, n0, m0, cv0, cw, cb, bif, c, t):
    seqs, seq_len, _ = mqk.shape
    has_state = c0 is not None
    streams = has_state and seq_len == c
    if streams:
        mqk, mvo, pif = (a.reshape(1, seqs * seq_len, a.shape[-1]) for a in (mqk, mvo, pif))
        t = seqs * seq_len
    b, l, _ = mqk.shape
    tri =jnp.asarray(np.tril(np.ones((c, c), np.float32)), BF16)
    sel = np.zeros((ML_HEADS, LANES, 3 * ML_DH), np.float32)
    for h in range(ML_HEADS):
        for slot in range(9):
            sel[h, ML_HEADS * slot + h, (slot // 3) * ML_DH:(slot // 3 + 1) * ML_DH] = 1.0
    sel = jnp.asarray(sel, BF16)
    tile = lambda w: pl.BlockSpec((None, t, w), lambda bi, i: (bi, i, 0))
    if streams:
        heads = lambda *shape: _full_spec((seqs, ML_HEADS) + shape)
        m_spec = _full_spec((seqs, 8, LANES))
        cv_spec = _full_spec((seqs, CONV_W - 1, 2 * ML_W))
    else:
        heads = lambda *shape: pl.BlockSpec((None, ML_HEADS) + shape, lambda bi, i: (bi, 0, 0, 0))
        m_spec = pl.BlockSpec((None, 8, LANES), lambda bi, i: (bi, 0, 0))
        cv_spec = pl.BlockSpec((None, CONV_W - 1, 2 * ML_W), lambda bi, i: (bi, 0, 0))
    c_spec = heads(ML_DH, ML_DH)
    in_specs = [tile(2 * ML_W), tile(2 * ML_W), tile(LANES)]
    args = [mqk, mvo, pif]
    consts = [cw, cb, bif, tri, sel]
    out_specs = [tile(ML_W), c_spec, heads(8, ML_DH), m_spec, cv_spec]
    out_shape = [jax.ShapeDtypeStruct((b, l, ML_W), BF16),
                 jax.ShapeDtypeStruct((seqs, ML_HEADS, ML_DH, ML_DH), F32),
                 jax.ShapeDtypeStruct((seqs, ML_HEADS, 8, ML_DH), F32),
                 jax.ShapeDtypeStruct((seqs, 8, LANES), F32),
                 jax.ShapeDtypeStruct((seqs, CONV_W - 1, 2 * ML_W), F32)]
    scratch = [pltpu.VMEM((ML_HEADS, ML_DH, 2 * ML_DH), F32), pltpu.VMEM((8, LANES), F32),
               pltpu.VMEM((CONV_PAD + t, 2 * ML_W), F32), pltpu.VMEM((t, 2 * ML_W), BF16)]
    if has_state:
        in_specs += [c_spec, heads(1, ML_DH), m_spec, cv_spec]
        args += [c0, n0, m0, cv0]
    in_specs += [_full_spec(x.shape) for x in consts]
    out, *states = pl.pallas_call(
        functools.partial(_mlstm_kernel, c=c, t=t, has_state=has_state, streams=streams),
        grid=(b, l // t),
        in_specs=in_specs,
        out_specs=out_specs,
        out_shape=out_shape,
        scratch_shapes=scratch,
        compiler_params=pltpu.CompilerParams(
            dimension_semantics=("arbitrary", "arbitrary"), vmem_limit_bytes=VMEM_LIMIT),
        name="mlstm",
    )(*args, *consts)
    return (out.reshape(seqs, seq_len, ML_W), *states)


def _merge_kernel(x_ref, ga_ref, hb_ref, pmg_ref, wug_ref, wum_ref, wo_ref, g2_ref,
                  wr_hi_ref, wr_lo_ref, br_ref, tril_ref, triu_ref, cnt0_ref,
                  x1_ref, hs_ref, meta_ref, tab_ref, cnt_scr):
    ya = _dot(ga_ref[...], wug_ref[...])
    yb = _dot(hb_ref[...], wum_ref[...])
    z = pmg_ref[:, 0:D_MODEL].astype(F32) * ya + pmg_ref[:, D_MODEL:].astype(F32) * yb
    x1 = x_ref[...] + _dot(z.astype(BF16), wo_ref[...])
    x1_ref[...] = x1
    hm = _rms(x1, g2_ref[...])
    hm_hi = hm.astype(BF16)
    hm_lo = (hm - hm_hi.astype(F32)).astype(BF16)
    hi_both = _dot(hm_hi, jnp.concatenate([wr_hi_ref[...], wr_lo_ref[...]], axis=1))
    logits = hi_both[:, 0:LANES] + hi_both[:, LANES:] + _dot(hm_lo, wr_hi_ref[...]) + br_ref[...]
    tm = logits.shape[0]
    n_out = -(-(N_EXPERTS + N_GROUPS) // 8) * 8
    lt = logits.T[0:n_out, :]
    out = lax.broadcasted_iota(jnp.int32, lt.shape, 0)
    neg = -jnp.inf
    is_g = (out >= N_EXPERTS) & (out < N_EXPERTS + N_GROUPS)
    lg = jnp.where(is_g, lt, neg)
    mg = jnp.max(lg, axis=0, keepdims=True)
    p_top = 1.0 / jnp.sum(jnp.exp(lg - mg), axis=0, keepdims=True)
    gi = jnp.min(jnp.where(lg == mg, out, 2 * LANES), axis=0, keepdims=True) - N_EXPERTS
    group_shift = int(math.log2(EXPERTS_PER_GROUP))
    sel = (out < N_EXPERTS) & (jnp.right_shift(out, group_shift) == gi)
    le = jnp.where(sel, lt, neg)
    v1 = jnp.max(le, axis=0, keepdims=True)
    i1 = jnp.min(jnp.where(le == v1, out, 2 * LANES), axis=0, keepdims=True)
    le2 = jnp.where(out == i1, neg, le)
    v2 = jnp.max(le2, axis=0, keepdims=True)
    i2 = jnp.min(jnp.where(le2 == v2, out, 2 * LANES), axis=0, keepdims=True)
    e2 = jnp.exp(v2 - v1)
    per_token = jnp.concatenate(
        [i1.astype(F32), i2.astype(F32), p_top * (1.0 / (1.0 + e2)), p_top * (e2 / (1.0 + e2)),
         jnp.zeros((LANES - 4, tm), F32)], axis=0).T
    i1 = per_token[:, 0:1].astype(jnp.int32)
    i2 = per_token[:, 1:2].astype(jnp.int32)
    w1 = per_token[:, 2:3]
    w2 = per_token[:, 3:4]
    lane = lax.broadcasted_iota(jnp.int32, logits.shape, 1)

    @pl.when(pl.program_id(0) == 0)
    def _():
        cnt_scr[...] = cnt0_ref[...]

    oh1 = lane == i1
    oh2 = lane == i2
    both = jnp.where(oh1 | oh2, 1.0, 0.0)
    cnt = jnp.sum(both, axis=0, keepdims=True)
    cnt = jnp.floor((cnt + (SEG_ALIGN - 1)) * (1.0 / SEG_ALIGN)) * SEG_ALIGN
    lower = _dot(jnp.broadcast_to(cnt, (8, LANES)).astype(BF16), triu_ref[...])[0:1, :]
    lpos = _dot(tril_ref[...], both.astype(BF16)) + lower
    lp1 = jnp.sum(jnp.where(oh1, lpos, 0.0), axis=-1, keepdims=True)
    lp2 = jnp.sum(jnp.where(oh2, lpos, 0.0), axis=-1, keepdims=True)
    pos = lax.broadcasted_iota(jnp.int32, (tm, hs_ref.shape[0]), 1)
    onehot = jnp.where((pos == lp1.astype(jnp.int32)) | (pos == lp2.astype(jnp.int32)), 1.0, 0.0)
    hs_ref[...] = _dot_tn(onehot.astype(BF16), hm_hi).astype(BF16)
    cols = (lp1, lp2, w1, w2)
    meta = jnp.zeros((tm, LANES), F32)
    for ci, col in enumerate(cols):
        meta = jnp.where(lane == ci, col, meta)
    meta_ref[...] = meta
    row8 = lax.broadcasted_iota(jnp.int32, (8, LANES), 0)
    tab_ref[...] = jnp.where(row8 == 0, cnt, jnp.where(row8 == 1, lower, jnp.where(row8 == 2, cnt_scr[...], 0.0)))
    cnt_scr[...] += cnt


def _merge_call(x2, ga, hb, pmg, wug, wum, wo, g2, wr_hi, wr_lo, br, cnt0, tm):
    n = x2.shape[0]
    tile = lambda w: pl.BlockSpec((tm, w), lambda i: (i, 0))
    tiles = lambda *shape: pl.BlockSpec((None,) + shape, lambda i: (i, 0, 0))
    local_rows = 2 * tm + SEG_ALIGN * N_EXPERTS
    consts = [wug, wum, wo, g2, wr_hi, wr_lo, br,
              jnp.asarray(np.tril(np.ones((tm, tm), np.float32), -1), BF16),
              jnp.asarray(np.triu(np.ones((LANES, LANES), np.float32), 1), BF16),
              cnt0]
    return pl.pallas_call(
        _merge_kernel,
        grid=(n // tm,),
        in_specs=[tile(D_MODEL), tile(GLA_V), tile(ML_W), tile(W_MG)]
                 + [_full_spec(x.shape) for x in consts],
        out_specs=[tile(D_MODEL), tiles(local_rows, D_MODEL), tile(LANES), tiles(8, LANES)],
        out_shape=[jax.ShapeDtypeStruct((n, D_MODEL), F32),
                   jax.ShapeDtypeStruct((n // tm, local_rows, D_MODEL), BF16),
                   jax.ShapeDtypeStruct((n, LANES), F32),
                   jax.ShapeDtypeStruct((n // tm, 8, LANES), F32)],
        scratch_shapes=[pltpu.VMEM((1, LANES), F32)],
        compiler_params=pltpu.CompilerParams(
            dimension_semantics=("arbitrary",), vmem_limit_bytes=VMEM_LIMIT),
        name="merge",
    )(x2, ga, hb, pmg, *consts)


MOE_ROWS = 512
SEG_ALIGN = 16


def _segment_copies(fn, tables, tile, local_ref, buffer, flat_ref, sem, to_flat):
    pieces_ref, gdst_ref = tables
    stride = gdst_ref.shape[0] // pieces_ref.shape[0]

    def body(p, carry):
        local = local_ref.at[buffer, pl.ds(pl.multiple_of(p * SEG_ALIGN, SEG_ALIGN), SEG_ALIGN), :]
        flat = flat_ref.at[pl.ds(pl.multiple_of(gdst_ref[tile * stride + p], SEG_ALIGN), SEG_ALIGN), :]
        src, dst = (local, flat) if to_flat else (flat, local)
        fn(pltpu.make_async_copy(src, dst, sem))
        return carry

    lax.fori_loop(0, pieces_ref[tile], body, 0)


def _gmm_kernel(te_ref, nv_ref, part_ref, row_ref, *refs, n_parts):
    hs_refs = refs[:n_parts]
    wg_ref, wu_ref, wd_ref, o_ref, x_buf, sem, wgu_scr, wd_scr = refs[n_parts:]
    j = pl.program_id(0)
    n_valid = nv_ref[0]
    used = j < n_valid
    pieces_per_tile = MOE_ROWS // SEG_ALIGN

    def piece_copy(hs_ref, row, slot, k):
        src = hs_ref.at[pl.ds(pl.multiple_of(row, SEG_ALIGN), SEG_ALIGN), :]
        return pltpu.make_async_copy(src, x_buf.at[slot, pl.ds(k * SEG_ALIGN, SEG_ALIGN), :], sem.at[slot])

    def fetch(tile):
        slot = tile % 2
        for k in range(pieces_per_tile):
            g = tile * pieces_per_tile + k
            row = row_ref[g]
            starts = [functools.partial(lambda hs_ref: piece_copy(hs_ref, row, slot, k).start(), hs_ref)
                      for hs_ref in hs_refs]
            lax.switch(part_ref[g], starts)

    def wait_tile(tile):
        slot = tile % 2
        for k in range(pieces_per_tile):
            piece_copy(hs_refs[0], 0, slot, k).wait()

    @pl.when(j == 0)
    def _():
        fetch(j)

    @pl.when(j + 1 < n_valid)
    def _():
        fetch(j + 1)

    @pl.when(used & ((j == 0) | (te_ref[j] != te_ref[jnp.maximum(j - 1, 0)])))
    def _():
        wgu_scr[:, 0:D_EXPERT] = wg_ref[...].astype(BF16)
        wgu_scr[:, D_EXPERT:] = wu_ref[...].astype(BF16)
        wd_scr[...] = wd_ref[...].astype(BF16)

    @pl.when(used)
    def _():
        wait_tile(j)
        au = _dot(x_buf[j % 2], wgu_scr[...])
        a = au[:, 0:D_EXPERT]
        hh = (a * _sigmoid(a)) * au[:, D_EXPERT:]
        o_ref[...] = _dot(hh.astype(BF16), wd_scr[...]).astype(o_ref.dtype)

    @pl.when(jnp.logical_not(used))
    def _():
        o_ref[...] = jnp.zeros_like(o_ref)


def _gmm_call(tile_expert, n_valid, sources, hs_parts, wg, wu, wd):
    n_tiles = tile_expert.shape[0]
    wsel = lambda j, te, *_: (te[j], 0, 0)
    any_spec = pl.BlockSpec(memory_space=pl.ANY)
    return pl.pallas_call(
        functools.partial(_gmm_kernel, n_parts=len(hs_parts)),
        grid_spec=pltpu.PrefetchScalarGridSpec(
            num_scalar_prefetch=4,
            grid=(n_tiles,),
            in_specs=[any_spec] * len(hs_parts)
                     + [pl.BlockSpec((None, D_MODEL, D_EXPERT), wsel),
                        pl.BlockSpec((None, D_MODEL, D_EXPERT), wsel),
                        pl.BlockSpec((None, D_EXPERT, D_MODEL), wsel)],
            out_specs=pl.BlockSpec((MOE_ROWS, D_MODEL), lambda j, *_: (j, 0)),
            scratch_shapes=[pltpu.VMEM((2, MOE_ROWS, D_MODEL), BF16),
                            pltpu.SemaphoreType.DMA((2,)),
                            pltpu.VMEM((D_MODEL, 2 * D_EXPERT), BF16),
                            pltpu.VMEM((D_EXPERT, D_MODEL), BF16)],
        ),
        out_shape=jax.ShapeDtypeStruct((n_tiles * MOE_ROWS, D_MODEL), BF16),
        compiler_params=pltpu.CompilerParams(
            dimension_semantics=("arbitrary",), vmem_limit_bytes=VMEM_LIMIT),
        name="moe_grouped",
    )(tile_expert, n_valid, *sources, *hs_parts, wg, wu, wd)


def _combine_kernel(pieces_ref, gdst_ref, x1_ref, meta_ref, gf_ref, os_ref, y_ref, buf_ref, sem, *, tile0):
    step = pl.program_id(0)
    n_steps = pl.num_programs(0)
    tm = x1_ref.shape[0]
    tables = (pieces_ref, gdst_ref)

    def fetch(fn, i):
        slot = i % 2
        _segment_copies(fn, tables, tile0 + i, buf_ref, slot, os_ref, sem.at[slot], False)

    @pl.when(step == 0)
    def _():
        buf_ref[...] = jnp.zeros_like(buf_ref)
        fetch(lambda c: c.start(), step)

    @pl.when(step + 1 < n_steps)
    def _():
        fetch(lambda c: c.start(), step + 1)

    fetch(lambda c: c.wait(), step)
    rows = buf_ref[step % 2]
    pos = lax.broadcasted_iota(jnp.int32, (tm, rows.shape[0]), 1)
    lp1 = meta_ref[:, 0:1].astype(jnp.int32)
    lp2 = meta_ref[:, 1:2].astype(jnp.int32)
    q = jnp.where(pos == lp1, meta_ref[:, 2:3], 0.0) + jnp.where(pos == lp2, meta_ref[:, 3:4], 0.0)
    y = x1_ref[...] + _dot(q.astype(BF16), rows)
    y_ref[...] = _rms(y, gf_ref[...])


def _combine_call(tables, x1, meta, gf, out_sorted, tile0, local_rows, tm):
    n = x1.shape[0]
    tile = lambda w: pl.BlockSpec((tm, w), lambda i, *_: (i, 0))
    return pl.pallas_call(
        functools.partial(_combine_kernel, tile0=tile0),
        grid_spec=pltpu.PrefetchScalarGridSpec(
            num_scalar_prefetch=2,
            grid=(n // tm,),
            in_specs=[tile(D_MODEL), tile(LANES), pl.BlockSpec(gf.shape, lambda i, *_: (0, 0)),
                      pl.BlockSpec(memory_space=pl.ANY)],
            out_specs=tile(D_MODEL),
            scratch_shapes=[pltpu.VMEM((2, local_rows, D_MODEL), out_sorted.dtype),
                            pltpu.SemaphoreType.DMA((2,))],
        ),
        out_shape=jax.ShapeDtypeStruct((n, D_MODEL), F32),
        compiler_params=pltpu.CompilerParams(
            dimension_semantics=("arbitrary",), vmem_limit_bytes=VMEM_LIMIT),
        name="moe_combine",
    )(*tables, x1, meta, gf, out_sorted)


def _sparse_moe(parts, p):
    token_tiles = [part[1].shape[0] for part in parts]
    n_tiles = -(-sum(part[1].shape[0] * part[1].shape[1] for part in parts) // MOE_ROWS) + N_EXPERTS
    tab = jnp.concatenate([part[3] for part in parts], axis=0)
    tab = tab[:, :, :N_EXPERTS].astype(jnp.int32)
    cnt, before = tab[:, 0], tab[:, 2]
    total = before[-1] + cnt[-1]
    tiles = (total + (MOE_ROWS - 1)) // MOE_ROWS
    tile_end = jnp.cumsum(tiles)
    tile_start = tile_end - tiles
    n_valid = tile_end[-1:]
    gpos = tile_start[None, :] * MOE_ROWS + before
    j = jnp.minimum(jnp.arange(n_tiles, dtype=jnp.int32), n_valid - 1)
    tile_expert = jnp.sum((j[:, None] >= tile_end[None, :]).astype(jnp.int32), axis=1)
    max_pieces = max(part[1].shape[1] for part in parts) // SEG_ALIGN
    pieces = cnt // SEG_ALIGN
    piece_end = jnp.cumsum(pieces, axis=1)
    piece_start = piece_end - pieces
    piece = jnp.arange(max_pieces, dtype=jnp.int32)
    expert_of = jnp.sum((piece[None, :, None] >= piece_end[:, None, :]).astype(jnp.int32), axis=2)
    onehot = expert_of[:, :, None] == jnp.arange(N_EXPERTS, dtype=jnp.int32)
    first_row = gpos - SEG_ALIGN * piece_start
    gdst = jnp.sum(jnp.where(onehot, first_row[:, None, :], 0), axis=2) + SEG_ALIGN * piece[None, :]
    tables = (piece_end[:, -1], gdst.reshape(-1))
    n_token_tiles = sum(token_tiles)
    over_tiles = jnp.cumsum(pieces, axis=0)
    e_j = tile_expert
    q0 = (jnp.arange(n_tiles, dtype=jnp.int32) - jnp.take(tile_start, e_j)) * (MOE_ROWS // SEG_ALIGN)
    q = q0[:, None] + jnp.arange(MOE_ROWS // SEG_ALIGN, dtype=jnp.int32)[None, :]
    valid = (jnp.arange(n_tiles)[:, None] < n_valid) & (q < jnp.take(over_tiles[-1], e_j)[:, None])
    per_tile = lambda table: jnp.take(table.T, e_j, axis=0)[:, None, :]
    src_tile = jnp.sum((q[:, :, None] >= per_tile(over_tiles)).astype(jnp.int32), axis=2)
    src_tile = jnp.minimum(src_tile, n_token_tiles - 1)
    at_src = lambda table: jnp.sum(jnp.where(
        src_tile[:, :, None] == jnp.arange(n_token_tiles, dtype=jnp.int32), per_tile(table), 0), axis=2)
    src_row = (at_src(piece_start) + q - at_src(over_tiles - pieces)) * SEG_ALIGN
    part_start = np.cumsum([0] + token_tiles)
    src_part = jnp.sum((src_tile[:, :, None] >= jnp.asarray(part_start[1:], jnp.int32)).astype(jnp.int32), axis=2)
    src_tile_in_part = src_tile - jnp.sum(jnp.where(
        src_part[:, :, None] == jnp.arange(len(parts), dtype=jnp.int32), jnp.asarray(part_start[:-1], jnp.int32), 0), axis=2)
    local_rows = jnp.asarray([part[1].shape[1] for part in parts], jnp.int32)
    rows_of_part = jnp.sum(jnp.where(src_part[:, :, None] == jnp.arange(len(parts), dtype=jnp.int32), local_rows, 0), axis=2)
    flat_row = src_tile_in_part * rows_of_part + src_row
    zero_row = parts[0][1].shape[1] - SEG_ALIGN
    assert zero_row >= 2 * (parts[0][0].shape[0] // parts[0][1].shape[0]) + (SEG_ALIGN - 1) * N_EXPERTS
    sources = (jnp.where(valid, src_part, 0).reshape(-1), jnp.where(valid, flat_row, zero_row).reshape(-1))
    out_sorted = _gmm_call(tile_expert, n_valid.astype(jnp.int32), sources,
                           [part[1].reshape(-1, D_MODEL) for part in parts], p["wg"], p["wu"], p["wd"])
    return [_combine_call(tables, part[0], part[2], p["gf"], out_sorted, int(tile0),
                          part[1].shape[1], part[0].shape[0] // part[1].shape[0])
            for part, tile0 in zip(parts, part_start[:-1])]


def _pad_cols(w, width):
    return jnp.pad(w, ((0, 0), (0, width - w.shape[1])))


def _prep_weights(norm1_g, w_in, gla_w_gate2, gla_b_gate, gla_norm_g, w_up_gla,
                  ml_conv_w, ml_conv_b, ml_b_i, ml_b_f, w_up_ml, w_out,
                  norm2_g, router_g_w, router_g_b, router_e_w, router_e_b,
                  moe_w_gate, moe_w_up, moe_w_down, final_g):
    wr =_pad_cols(jnp.concatenate([router_e_w, router_g_w], axis=1), LANES)
    wr_hi = wr.astype(BF16)
    wr_lo = (wr - wr_hi.astype(F32)).astype(BF16)
    br = _pad_cols(jnp.concatenate([router_e_b, router_g_b])[None, :], LANES)
    wg2 = jnp.pad(gla_w_gate2, ((0, LANES - GLA_GATE_RANK), (0, 0)))
    wg2_hi = wg2.astype(BF16)
    return dict(
        g1=norm1_g[None, :], w_in=w_in.T,
        wg2_p=jnp.stack([wg2_hi, (wg2 - wg2_hi.astype(F32)).astype(BF16)]),
        bg=gla_b_gate[None, :], gn=gla_norm_g[None, :],
        wug=w_up_gla.astype(BF16),
        cw=ml_conv_w, cb=ml_conv_b[None, :],
        bif=_pad_cols(jnp.concatenate([ml_b_i, ml_b_f])[None, :], LANES),
        wum=w_up_ml.astype(BF16), wo=w_out.astype(BF16),
        g2=norm2_g[None, :], wr_hi=wr_hi, wr_lo=wr_lo, br=br,
        wg=moe_w_gate.reshape(N_EXPERTS, D_MODEL, D_EXPERT),
        wu=moe_w_up.reshape(N_EXPERTS, D_MODEL, D_EXPERT),
        wd=moe_w_down.reshape(N_EXPERTS, D_EXPERT, D_MODEL),
        gf=final_g[None, :],
    )


def _mixers(x, gla_s0, ml_c0, ml_n0, ml_m0, conv0, p, placed, *,
            gla_chunk, ml_chunk, seq_tile, row_tile, merge_tile):
    b, l, _ = x.shape
    n = b * l
    x2 = x.reshape(n, D_MODEL)
    qk, vr, plr, mqk, mvo, pif, pmg = _proj_call(x2, p["g1"], p["w_in"], row_tile)
    r3 = lambda a: a.reshape(b, l, a.shape[-1])
    s0 = None if gla_s0 is None else gla_s0.reshape(b, GLA_QK, GLA_DV)
    ga, gla_s = _gla_call(r3(qk), r3(vr), r3(plr), s0, p["wg2_p"], p["bg"], p["gn"], gla_chunk, seq_tile)
    gla_s = gla_s.reshape(b, GLA_HEADS, GLA_DK, GLA_DV)
    if ml_c0 is None:
        n0 = m0 = None
    else:
        n0 = ml_n0[:, :, None, :]
        m0 = jnp.broadcast_to(_pad_cols(ml_m0, LANES)[:, None, :], (b, 8, LANES))
    hb, ml_c, ml_n, m_b, new_conv = _mlstm_call(r3(mqk), r3(mvo), r3(pif), ml_c0, n0, m0, conv0,
                                                p["cw"], p["cb"], p["bif"], ml_chunk, seq_tile)
    part = _merge_call(x2, ga.reshape(n, GLA_V), hb.reshape(n, ML_W), pmg,
                       p["wug"], p["wum"], p["wo"], p["g2"],
                       p["wr_hi"], p["wr_lo"], p["br"], placed, merge_tile)
    states = (gla_s[None], ml_c[None], ml_n[:, :, 0, :][None], m_b[:, 0, 0:ML_HEADS][None], new_conv[None])
    return part, states


def kernel(x_prompt, x_sample, state_gla_S, state_mlstm_C, state_mlstm_n, state_mlstm_m, state_mlstm_conv, norm1_g, w_in, gla_w_gate2, gla_b_gate, gla_norm_g, w_up_gla, ml_conv_w, ml_conv_b, ml_b_i, ml_b_f, w_up_ml, w_out, norm2_g, router_g_w, router_g_b, router_e_w, router_e_b, moe_w_gate, moe_w_up, moe_w_down, final_g):
    assert norm1_g.shape[0] == 1, "single-layer trunk"
    p = _prep_weights(norm1_g[0], w_in[0], gla_w_gate2[0], gla_b_gate[0], gla_norm_g[0], w_up_gla[0],
                      ml_conv_w[0], ml_conv_b[0], ml_b_i[0], ml_b_f[0], w_up_ml[0], w_out[0],
                      norm2_g[0], router_g_w[0], router_g_b[0], router_e_w[0], router_e_b[0],
                      moe_w_gate[0], moe_w_up[0], moe_w_down[0], final_g)
    dec_seq = x_sample.shape[1]
    n_sample = x_sample.shape[0] * dec_seq
    part_p, sp = _mixers(x_prompt, None, None, None, None, None, p, jnp.zeros((1, LANES), F32),
                         gla_chunk=128, ml_chunk=256, seq_tile=1024, row_tile=256, merge_tile=512)
    placed = part_p[3][-1, 0:1, :] + part_p[3][-1, 2:3, :]
    part_s, ss = _mixers(x_sample, state_gla_S[0], state_mlstm_C[0], state_mlstm_n[0], state_mlstm_m[0],
                         state_mlstm_conv[0], p, placed,
                         gla_chunk=dec_seq, ml_chunk=dec_seq, seq_tile=dec_seq,
                         row_tile=n_sample, merge_tile=n_sample)
    yp, ys = _sparse_moe([part_p, part_s], p)
    return (yp.reshape(x_prompt.shape), ys.reshape(x_sample.shape), *sp, *ss)
```

```python
import functools
import math

import numpy as np
import jax
import jax.numpy as jnp
from jax import lax
from jax.experimental import pallas as pl
from jax.experimental.pallas import tpu as pltpu

D_MODEL = 1024
GLA_HEADS = 4
GLA_DK = 64
GLA_DV = 128
GLA_GATE_RANK = 16
GLA_TAU = 16.0
ML_HEADS = 4
ML_DH = 128
CONV_W = 4
N_GROUPS = 4
EXPERTS_PER_GROUP = 8
N_EXPERTS = N_GROUPS * EXPERTS_PER_GROUP
D_EXPERT = 256
EPS = 1e-6

GLA_QK = GLA_HEADS * GLA_DK
GLA_V = GLA_HEADS * GLA_DV
ML_W = ML_HEADS * ML_DH

LANES = 128
VMEM_LIMIT = 56 * 1024 * 1024

W_GLA = 2 * GLA_QK + 2 * GLA_V
W_ML = 2 * ML_W + ML_W + ML_W
W_MG = 2 * D_MODEL
PROJ_WIDTHS = (W_GLA, LANES, W_ML, LANES, W_MG)
PROJ_SOURCE_WIDTHS = (W_GLA, GLA_GATE_RANK, W_ML, 2 * ML_HEADS, W_MG)

F32 = jnp.float32
BF16 = jnp.bfloat16

PROJ_OUTPUTS = ((2 * GLA_QK, F32), (2 * GLA_V, BF16), (LANES, F32), (2 * ML_W, F32), (2 * ML_W, BF16),
                (LANES, F32), (W_MG, BF16))


def _dot(a, b):
    return jnp.dot(a, b, preferred_element_type=F32)


def _dot_nt(a, b):
    return lax.dot_general(a, b, (((1,), (1,)), ((), ())), preferred_element_type=F32)


def _dot_tn(a, b):
    return lax.dot_general(a, b, (((0,), (0,)), ((), ())), preferred_element_type=F32)


def _split3(x):
    hi = x.astype(BF16)
    r1 = x - hi.astype(F32)
    mid = r1.astype(BF16)
    lo = (r1 - mid.astype(F32)).astype(BF16)
    return hi, mid, lo


def _dot_exact_lhs(m, x):
    hi, mid, lo = _split3(x)
    return _dot(m, hi) + _dot(m, mid) + _dot(m, lo)


def _log_sigmoid(z):
    return jnp.minimum(z, 0.0) - jnp.log(1.0 + jnp.exp(-jnp.abs(z)))


def _sigmoid(z):
    return 1.0 / (1.0 + jnp.exp(-z))


def _rms(x, g):
    return x * lax.rsqrt(jnp.mean(x * x, axis=-1, keepdims=True) + EPS) * g


def _full_spec(shape):
    nd = len(shape)
    return pl.BlockSpec(shape, lambda *_: (0,) * nd)


CONV_PAD = 8


def _causal_conv_silu(stage_ref, x, cw_ref, cb_ref):
    t = x.shape[0]
    stage_ref[CONV_PAD:CONV_PAD + t, :] = x
    acc = cb_ref[...] + stage_ref[CONV_PAD:CONV_PAD + t, :] * cw_ref[CONV_W - 1:CONV_W, :]
    for d in range(1, CONV_W):
        acc = acc + stage_ref[CONV_PAD - d:CONV_PAD - d + t, :] * cw_ref[CONV_W - 1 - d:CONV_W - d, :]
    stage_ref[0:CONV_PAD, :] = stage_ref[t:t + CONV_PAD, :]
    return acc * _sigmoid(acc)


def _proj_kernel(x_ref, g_ref, win_ref, qk_ref, vr_ref, lr_ref, mqk_ref, mvo_ref, if_ref, mg_ref, w_ref):
    starts = np.cumsum((0,) + PROJ_WIDTHS)

    @pl.when(pl.program_id(0) == 0)
    def _():
        src = np.cumsum((0,) + PROJ_SOURCE_WIDTHS)
        chunk = 512
        for g, width in enumerate(PROJ_SOURCE_WIDTHS):
            for r0 in range(0, width, chunk):
                rows = min(chunk, width - r0)
                w_ref[starts[g] + r0:starts[g] + r0 + rows, :] = (
                    win_ref[src[g] + r0:src[g] + r0 + rows, :].astype(BF16))
            if width < PROJ_WIDTHS[g]:
                w_ref[starts[g] + width:starts[g + 1], :] = jnp.zeros((PROJ_WIDTHS[g] - width, D_MODEL), BF16)

    h = _rms(x_ref[...], g_ref[...]).astype(BF16)

    def cols(group, lo, hi):
        return _dot_nt(h, w_ref[starts[group] + lo:starts[group] + hi, :])

    qk_ref[...] = cols(0, 0, 2 * GLA_QK)
    vr_ref[:, 0:GLA_V] = cols(0, 2 * GLA_QK, 2 * GLA_QK + GLA_V).astype(BF16)
    r = cols(0, 2 * GLA_QK + GLA_V, W_GLA)
    vr_ref[:, GLA_V:] = (r * _sigmoid(r)).astype(BF16)
    lr_ref[...] = cols(1, 0, LANES)
    mqk_ref[...] = cols(2, 0, 2 * ML_W)
    mvo_ref[:, 0:ML_W] = cols(2, 2 * ML_W, 3 * ML_W).astype(BF16)
    mvo_ref[:, ML_W:] = _sigmoid(cols(2, 3 * ML_W, W_ML)).astype(BF16)
    if_ref[...] = cols(3, 0, LANES)
    mg_ref[...] = _sigmoid(cols(4, 0, W_MG)).astype(BF16)


def _proj_call(x2, g, w_in, tm):
    n = x2.shape[0]
    assert w_in.shape == (sum(PROJ_SOURCE_WIDTHS), D_MODEL)
    return pl.pallas_call(
        _proj_kernel,
        grid=(n // tm,),
        in_specs=[pl.BlockSpec((tm, D_MODEL), lambda i: (i, 0)),
                  _full_spec(g.shape),
                  pl.BlockSpec(w_in.shape, lambda i: (0, 0), pipeline_mode=pl.Buffered(1))],
        out_specs=[pl.BlockSpec((tm, w), lambda i: (i, 0)) for w, _ in PROJ_OUTPUTS],
        out_shape=[jax.ShapeDtypeStruct((n, w), dt) for w, dt in PROJ_OUTPUTS],
        scratch_shapes=[pltpu.VMEM((sum(PROJ_WIDTHS), D_MODEL), BF16)],
        compiler_params=pltpu.CompilerParams(
            dimension_semantics=("arbitrary",), vmem_limit_bytes=VMEM_LIMIT),
        name="in_proj",
    )(x2, g, w_in)


def _gla_consts(c):
    nlev = int(math.log2(c))
    assert 1 << nlev == c
    t = np.arange(c)[:, None]
    j = np.arange(c)[None, :]
    lv = np.full((c, c), -1, np.int32)
    for l in range(nlev):
        h = c >> (l + 1)
        upper = (t % (2 * h)) >= h
        same = (j // (2 * h)) == (t // (2 * h))
        s_lower = (j % (2 * h)) < h
        lv[np.broadcast_to(upper, (c, c)) & same & s_lower] = l
    lv[np.eye(c, dtype=bool)] = nlev
    tri = (j <= t).astype(np.float32)
    return jnp.asarray(tri, BF16), jnp.asarray(np.concatenate([lv, lv], axis=1))


def _gla_kernel(*refs, c, t, has_state, streams):
    if has_state:
        (qk_ref, vr_ref, plr_ref, s0_ref, wg2_ref, bg_ref, gn_ref, tri_ref, lv_ref,
         o_ref, sout_ref, s_scr) = refs
    else:
        (qk_ref, vr_ref, plr_ref, wg2_ref, bg_ref, gn_ref, tri_ref, lv_ref,
         o_ref, sout_ref, s_scr) = refs
    nlev = int(math.log2(c))
    step = pl.program_id(1)

    if not streams:
        @pl.when(step == 0)
        def _():
            if has_state:
                s_scr[...] = s0_ref[...]
            else:
                s_scr[...] = jnp.zeros_like(s_scr)

    n_chunks = t // c
    lane_k = lax.broadcasted_iota(jnp.int32, (t, GLA_QK), 1)
    first_of_pair = (lane_k % (2 * GLA_DK)) < GLA_DK
    row_k = lax.broadcasted_iota(jnp.int32, (GLA_QK, GLA_DV), 0)
    row_t = lax.broadcasted_iota(jnp.int32, (t, GLA_QK), 0)

    def block_ref(b, blk, idx):
        if blk >= 8:
            b3 = b.reshape(t // blk, blk, GLA_QK)
            return jnp.broadcast_to(b3[:, idx:idx + 1, :], b3.shape).reshape(t, GLA_QK)
        r = row_t % blk
        out = b
        for sh in range(-idx, blk - idx):
            if sh != 0:
                out = jnp.where(r - idx == sh, pltpu.roll(b, sh % t, axis=0), out)
        return out

    q = qk_ref[:, 0:GLA_QK] * (GLA_DK ** -0.5)
    k = qk_ref[:, GLA_QK:2 * GLA_QK]
    glr = plr_ref[...]
    g_hi = glr.astype(BF16)
    g_lo = (glr - g_hi.astype(F32)).astype(BF16)
    z = (_dot(g_hi, wg2_ref[0]) + _dot(g_lo, wg2_ref[0]) + _dot(g_hi, wg2_ref[1])) + bg_ref[...]
    la = _log_sigmoid(z) * (math.log2(math.e) / GLA_TAU)
    b = jnp.concatenate([_dot_exact_lhs(tri_ref[...], la[ci * c:(ci + 1) * c])
                         for ci in range(n_chunks)], axis=0)
    b_last = block_ref(b, c, c - 1)
    qe = (q * jnp.exp2(b)).astype(BF16)
    kl = (k * jnp.exp2(b_last - b)).astype(BF16)

    k_a = jnp.where(first_of_pair, k, 0.0)
    k_b = k - k_a
    lv2 = lv_ref[...]
    factors = []
    for l in range(nlev + 1):
        if l < nlev:
            half = c >> (l + 1)
            d = b - block_ref(b, 2 * half, half - 1)
            e = jnp.exp2(jnp.minimum(d, -d))
            qt, kta, ktb = q * e, k_a * e, k_b * e
        else:
            qt, kta, ktb = q, k_a, k_b
        factors.append((qt.astype(BF16), kta.astype(BF16), ktb.astype(BF16)))
    a = [[None] * (GLA_HEADS // 2) for _ in range(n_chunks)]
    for ci in range(n_chunks):
        rows = slice(ci * c, (ci + 1) * c)
        for pr in range(GLA_HEADS // 2):
            ls = slice(pr * 2 * GLA_DK, (pr + 1) * 2 * GLA_DK)
            acc = jnp.zeros((c, 2 * c), F32)
            for l, (qt, kta, ktb) in enumerate(factors):
                rhs = jnp.concatenate([kta[rows, ls], ktb[rows, ls]], axis=0)
                acc = jnp.where(lv2 == l, _dot_nt(qt[rows, ls], rhs), acc)
            a[ci][pr] = acc

    for ci in range(n_chunks):
        rows = slice(ci * c, (ci + 1) * c)
        v = vr_ref[rows, 0:GLA_V]
        s_in, s_out = (s0_ref.at[ci], sout_ref.at[ci]) if streams else (s_scr, s_scr)
        s_all = s_in[...]
        s_bd = jnp.concatenate(
            [jnp.where((row_k // GLA_DK) == h, s_all, 0.0).astype(BF16) for h in range(GLA_HEADS)], axis=1)
        o_inter = _dot(qe[rows], s_bd)
        u_all = _dot_tn(kl[rows], v)
        dcol = jnp.exp2(jnp.broadcast_to(b[ci * c + c - 1:(ci + 1) * c, :], (LANES, GLA_QK)).T)
        for h in range(GLA_HEADS):
            vs = slice(h * GLA_DV, (h + 1) * GLA_DV)
            ks = slice(h * GLA_DK, (h + 1) * GLA_DK)
            a_h = a[ci][h // 2][:, (h % 2) * c:(h % 2 + 1) * c]
            o = _dot(a_h.astype(BF16), v[:, vs]) + o_inter[:, vs]
            on = _rms(o, gn_ref[:, vs])
            gate = vr_ref[rows, GLA_V + h * GLA_DV:GLA_V + (h + 1) * GLA_DV]
            o_ref[rows, vs] = (on * gate.astype(F32)).astype(o_ref.dtype)
            s_out[ks, :] = dcol[ks, :] * s_all[ks, :] + u_all[ks, vs]

    if not streams:
        @pl.when(step == pl.num_programs(1) - 1)
        def _():
            sout_ref[...] = s_scr[...]


def _gla_call(qk, vr, plr, s0, wg2_p, bg, gn, c, t):
    seqs, seq_len, _ = qk.shape
    tri, lv2 = _gla_consts(c)
    has_state = s0 is not None
    streams = has_state and seq_len == c
    if streams:
        qk, vr, plr = (a.reshape(1, seqs * seq_len, a.shape[-1]) for a in (qk, vr, plr))
        t = seqs * seq_len
    b, l, _ = qk.shape
    tile = lambda w: pl.BlockSpec((None, t, w), lambda bi, i: (bi, i, 0))
    if streams:
        state_spec = pl.BlockSpec((seqs, GLA_QK, GLA_DV), lambda bi, i: (0, 0, 0))
    else:
        state_spec = pl.BlockSpec((None, GLA_QK, GLA_DV), lambda bi, i: (bi, 0, 0))
    in_specs = [tile(2 * GLA_QK), tile(2 * GLA_V), tile(LANES)]
    args = [qk, vr, plr]
    if has_state:
        in_specs.append(state_spec)
        args.append(s0)
    consts = [wg2_p, bg, gn, tri, lv2]
    in_specs += [_full_spec(x.shape) for x in consts]
    out, state = pl.pallas_call(
        functools.partial(_gla_kernel, c=c, t=t, has_state=has_state, streams=streams),
        grid=(b, l // t),
        in_specs=in_specs,
        out_specs=[tile(GLA_V), state_spec],
        out_shape=[jax.ShapeDtypeStruct((b, l, GLA_V), BF16),
                   jax.ShapeDtypeStruct((seqs, GLA_QK, GLA_DV), F32)],
        scratch_shapes=[pltpu.VMEM((GLA_QK, GLA_DV), F32)],
        compiler_params=pltpu.CompilerParams(
            dimension_semantics=("arbitrary", "arbitrary"), vmem_limit_bytes=VMEM_LIMIT),
        name="gla",
    )(*args, *consts)
    return out.reshape(seqs, seq_len, GLA_V), state


def _mlstm_kernel(*refs, c, t, has_state, streams):
    if has_state:
        (mqk_ref, mvo_ref, pif_ref, c0_ref, n0_ref, m0_ref, cv0_ref, cw_ref, cb_ref, bif_ref, tri_ref, sel_ref,
         o_ref, cout_ref, nout_ref, mout_ref, cvout_ref, c_scr, m_scr, cv_scr, qk_scr) = refs
    else:
        (mqk_ref, mvo_ref, pif_ref, cw_ref, cb_ref, bif_ref, tri_ref, sel_ref,
         o_ref, cout_ref, nout_ref, mout_ref, cvout_ref, c_scr, m_scr, cv_scr, qk_scr) = refs
    step = pl.program_id(1)
    hist = CONV_W - 1

    n_chunks = t // c

    def with_n(c_mat, n_row):
        return jnp.concatenate([c_mat, jnp.broadcast_to(n_row, (ML_DH, ML_DH)).T], axis=1)

    def conv_rows(rows):
        conv = _causal_conv_silu(cv_scr, mqk_ref[rows, :], cw_ref, cb_ref)
        qk_scr[rows, 0:ML_W] = conv[:, 0:ML_W].astype(BF16)
        qk_scr[rows, ML_W:] = (conv[:, ML_W:] * (ML_DH ** -0.5)).astype(BF16)

    if streams:
        for ci in range(n_chunks):
            cv_scr[0:CONV_PAD - hist, :] = jnp.zeros((CONV_PAD - hist, 2 * ML_W), F32)
            cv_scr[CONV_PAD - hist:CONV_PAD, :] = cv0_ref[ci]
            conv_rows(slice(ci * c, (ci + 1) * c))
            cvout_ref[ci] = mqk_ref[(ci + 1) * c - hist:(ci + 1) * c, :]
    else:
        @pl.when(step == 0)
        def _():
            if has_state:
                for h in range(ML_HEADS):
                    c_scr[h] = with_n(c0_ref[h], n0_ref[h])
                m_scr[...] = m0_ref[...] * math.log2(math.e)
                cv_scr[0:CONV_PAD - hist, :] = jnp.zeros((CONV_PAD - hist, 2 * ML_W), F32)
                cv_scr[CONV_PAD - hist:CONV_PAD, :] = cv0_ref[...]
            else:
                c_scr[...] = jnp.zeros_like(c_scr)
                m_scr[...] = jnp.zeros_like(m_scr)
                cv_scr[0:CONV_PAD, :] = jnp.zeros((CONV_PAD, 2 * ML_W), F32)

        conv_rows(slice(0, t))

    log2e = math.log2(math.e)
    pre = pif_ref[...] + bif_ref[...]
    gts = pre * log2e
    flog = pltpu.roll(_log_sigmoid(pre) * log2e, LANES - ML_HEADS, axis=1)

    lane = lax.broadcasted_iota(jnp.int32, (t, LANES), 1)
    row_c = lax.broadcasted_iota(jnp.int32, (t, LANES), 0) % c
    causal = (lax.broadcasted_iota(jnp.int32, (c, c), 1) <= lax.broadcasted_iota(jnp.int32, (c, c), 0))
    ones_v = jnp.ones((c, ML_DH), BF16)

    def slots(pieces):
        out = jnp.zeros((t, LANES), F32)
        for j, piece in enumerate(pieces):
            if not isinstance(piece, float) and j > 0:
                piece = pltpu.roll(piece, ML_HEADS * j, axis=1)
            out = jnp.where((lane >= ML_HEADS * j) & (lane < ML_HEADS * (j + 1)), piece, out)
        return out

    def split3f(x):
        return [p.astype(F32) for p in _split3(x)]

    def per_chunk(rows_of):
        return jnp.concatenate([jnp.broadcast_to(rows_of(ci), (c, LANES)) for ci in range(n_chunks)], axis=0)

    bc = jnp.concatenate([_dot_exact_lhs(tri_ref[...], flog[ci * c:(ci + 1) * c])
                          for ci in range(n_chunks)], axis=0)
    w = gts - bc
    cm = w
    for j in range(int(math.log2(c))):
        sh = 1 << j
        cm = jnp.where(row_c >= sh, jnp.maximum(cm, pltpu.roll(cm, sh, axis=0)), cm)
    m_in, m_out = [], []
    for ci in range(n_chunks):
        last = slice(ci * c + c - 1, (ci + 1) * c)
        if streams:
            m_in.append(m0_ref[ci][0:1, :] * log2e)
        else:
            m_in.append(m_scr[0:1, :] if ci == 0 else m_out[ci - 1])
        m_out.append(bc[last, :] + jnp.maximum(m_in[ci], cm[last, :]))
        if streams:
            mout_ref[ci] = jnp.broadcast_to(m_out[ci] * (1.0 / log2e), mout_ref.shape[1:])
    if not streams:
        m_scr[...] = jnp.broadcast_to(m_out[-1], m_scr.shape)
    mprev = per_chunk(lambda ci: m_in[ci])
    g = jnp.maximum(mprev, cm)
    g_last = per_chunk(lambda ci: g[ci * c + c - 1:(ci + 1) * c, :])
    lhs_all = slots([1.0, 1.0, 1.0] + split3f(g))
    rhs_all = slots(split3f(w) + [-1.0, -1.0, -1.0]).astype(BF16)
    y_all = slots(split3f(mprev - g) + split3f(-(bc + g)) + split3f(w - g_last)).astype(BF16)
    lhs_heads = [jnp.where(lane % ML_HEADS == h, lhs_all, 0.0).astype(BF16) for h in range(ML_HEADS)]

    for ci in range(n_chunks):
        r0 = ci * c
        rows = pl.ds(r0, c)
        for h in range(ML_HEADS):
            hs = slice(h * ML_DH, (h + 1) * ML_DH)
            qh = qk_scr[rows, h * ML_DH:(h + 1) * ML_DH]
            kh = qk_scr[rows, ML_W + h * ML_DH:ML_W + (h + 1) * ML_DH]
            vaug = jnp.concatenate(
                [mvo_ref[rows, h * ML_DH:(h + 1) * ML_DH], ones_v],
                axis=1)
            p = jnp.exp2(jnp.where(causal, _dot_nt(lhs_heads[h][r0:r0 + c], rhs_all[r0:r0 + c]), -jnp.inf))
            bx = jnp.exp2(_dot(y_all[r0:r0 + c], sel_ref[h]))
            w_inter = bx[:, 0:ML_DH]
            e_mt = bx[:, ML_DH:2 * ML_DH]
            w_state = bx[:, 2 * ML_DH:3 * ML_DH]
            s = (_dot_nt(qh, kh) * p).astype(BF16)
            caug = with_n(c0_ref[ci, h], n0_ref[ci, h]) if streams else c_scr[h]
            nd = _dot(s, vaug) + jnp.concatenate([w_inter, w_inter], axis=1) * _dot(qh, caug.astype(BF16))
            hh = nd[:, 0:ML_DH] / jnp.maximum(jnp.abs(nd[:, ML_DH:]), e_mt)
            o_gate = mvo_ref[rows, ML_W + h * ML_DH:ML_W + (h + 1) * ML_DH]
            o_ref[rows, hs] = (o_gate.astype(F32) * hh).astype(o_ref.dtype)
            ks = (kh.astype(F32) * w_state).astype(BF16)
            dec = w_inter[c - 1:c, :]
            new = jnp.concatenate([dec, dec], axis=1) * caug + _dot_tn(ks, vaug)
            if streams:
                cout_ref[ci, h] = new[:, 0:ML_DH]
                nout_ref[ci, h] = new[:, ML_DH:].T[0:8, :]
            else:
                c_scr[h] = new

    if not streams:
        @pl.when(step == pl.num_programs(1) - 1)
        def _():
            for h in range(ML_HEADS):
                cout_ref[h] = c_scr[h, :, 0:ML_DH]
                nout_ref[h] = c_scr[h, :, ML_DH:].T[0:8, :]
            mout_ref[...] = m_scr[...] * (1.0 / log2e)
            cvout_ref[...] = mqk_ref[t - hist:t, :]


def _mlstm_call(mqk, mvo, pif, c0, n0, m0, cv0, cw, cb, bif, c, t):
    seqs, seq_len, _ = mqk.shape
    has_state = c0 is not None
    streams = has_state and seq_len == c
    if streams:
        mqk, mvo, pif = (a.reshape(1, seqs * seq_len, a.shape[-1]) for a in (mqk, mvo, pif))
        t = seqs * seq_len
    b, l, _ = mqk.shape
    tri =jnp.asarray(np.tril(np.ones((c, c), np.float32)), BF16)
    sel = np.zeros((ML_HEADS, LANES, 3 * ML_DH), np.float32)
    for h in range(ML_HEADS):
        for slot in range(9):
            sel[h, ML_HEADS * slot + h, (slot // 3) * ML_DH:(slot // 3 + 1) * ML_DH] = 1.0
    sel = jnp.asarray(sel, BF16)
    tile = lambda w: pl.BlockSpec((None, t, w), lambda bi, i: (bi, i, 0))
    if streams:
        heads = lambda *shape: _full_spec((seqs, ML_HEADS) + shape)
        m_spec = _full_spec((seqs, 8, LANES))
        cv_spec = _full_spec((seqs, CONV_W - 1, 2 * ML_W))
    else:
        heads = lambda *shape: pl.BlockSpec((None, ML_HEADS) + shape, lambda bi, i: (bi, 0, 0, 0))
        m_spec = pl.BlockSpec((None, 8, LANES), lambda bi, i: (bi, 0, 0))
        cv_spec = pl.BlockSpec((None, CONV_W - 1, 2 * ML_W), lambda bi, i: (bi, 0, 0))
    c_spec = heads(ML_DH, ML_DH)
    in_specs = [tile(2 * ML_W), tile(2 * ML_W), tile(LANES)]
    args = [mqk, mvo, pif]
    consts = [cw, cb, bif, tri, sel]
    out_specs = [tile(ML_W), c_spec, heads(8, ML_DH), m_spec, cv_spec]
    out_shape = [jax.ShapeDtypeStruct((b, l, ML_W), BF16),
                 jax.ShapeDtypeStruct((seqs, ML_HEADS, ML_DH, ML_DH), F32),
                 jax.ShapeDtypeStruct((seqs, ML_HEADS, 8, ML_DH), F32),
                 jax.ShapeDtypeStruct((seqs, 8, LANES), F32),
                 jax.ShapeDtypeStruct((seqs, CONV_W - 1, 2 * ML_W), F32)]
    scratch = [pltpu.VMEM((ML_HEADS, ML_DH, 2 * ML_DH), F32), pltpu.VMEM((8, LANES), F32),
               pltpu.VMEM((CONV_PAD + t, 2 * ML_W), F32), pltpu.VMEM((t, 2 * ML_W), BF16)]
    if has_state:
        in_specs += [c_spec, heads(1, ML_DH), m_spec, cv_spec]
        args += [c0, n0, m0, cv0]
    in_specs += [_full_spec(x.shape) for x in consts]
    out, *states = pl.pallas_call(
        functools.partial(_mlstm_kernel, c=c, t=t, has_state=has_state, streams=streams),
        grid=(b, l // t),
        in_specs=in_specs,
        out_specs=out_specs,
        out_shape=out_shape,
        scratch_shapes=scratch,
        compiler_params=pltpu.CompilerParams(
            dimension_semantics=("arbitrary", "arbitrary"), vmem_limit_bytes=VMEM_LIMIT),
        name="mlstm",
    )(*args, *consts)
    return (out.reshape(seqs, seq_len, ML_W), *states)


def _merge_kernel(x_ref, ga_ref, hb_ref, pmg_ref, wug_ref, wum_ref, wo_ref, g2_ref,
                  wr_hi_ref, wr_lo_ref, br_ref, tril_ref, triu_ref, cnt0_ref,
                  x1_ref, hs_ref, meta_ref, tab_ref, cnt_scr):
    ya = _dot(ga_ref[...], wug_ref[...])
    yb = _dot(hb_ref[...], wum_ref[...])
    z = pmg_ref[:, 0:D_MODEL].astype(F32) * ya + pmg_ref[:, D_MODEL:].astype(F32) * yb
    x1 = x_ref[...] + _dot(z.astype(BF16), wo_ref[...])
    x1_ref[...] = x1
    hm = _rms(x1, g2_ref[...])
    hm_hi = hm.astype(BF16)
    hm_lo = (hm - hm_hi.astype(F32)).astype(BF16)
    hi_both = _dot(hm_hi, jnp.concatenate([wr_hi_ref[...], wr_lo_ref[...]], axis=1))
    logits = hi_both[:, 0:LANES] + hi_both[:, LANES:] + _dot(hm_lo, wr_hi_ref[...]) + br_ref[...]
    tm = logits.shape[0]
    n_out = -(-(N_EXPERTS + N_GROUPS) // 8) * 8
    lt = logits.T[0:n_out, :]
    out = lax.broadcasted_iota(jnp.int32, lt.shape, 0)
    neg = -jnp.inf
    is_g = (out >= N_EXPERTS) & (out < N_EXPERTS + N_GROUPS)
    lg = jnp.where(is_g, lt, neg)
    mg = jnp.max(lg, axis=0, keepdims=True)
    p_top = 1.0 / jnp.sum(jnp.exp(lg - mg), axis=0, keepdims=True)
    gi = jnp.min(jnp.where(lg == mg, out, 2 * LANES), axis=0, keepdims=True) - N_EXPERTS
    group_shift = int(math.log2(EXPERTS_PER_GROUP))
    sel = (out < N_EXPERTS) & (jnp.right_shift(out, group_shift) == gi)
    le = jnp.where(sel, lt, neg)
    v1 = jnp.max(le, axis=0, keepdims=True)
    i1 = jnp.min(jnp.where(le == v1, out, 2 * LANES), axis=0, keepdims=True)
    le2 = jnp.where(out == i1, neg, le)
    v2 = jnp.max(le2, axis=0, keepdims=True)
    i2 = jnp.min(jnp.where(le2 == v2, out, 2 * LANES), axis=0, keepdims=True)
    e2 = jnp.exp(v2 - v1)
    per_token = jnp.concatenate(
        [i1.astype(F32), i2.astype(F32), p_top * (1.0 / (1.0 + e2)), p_top * (e2 / (1.0 + e2)),
         jnp.zeros((LANES - 4, tm), F32)], axis=0).T
    i1 = per_token[:, 0:1].astype(jnp.int32)
    i2 = per_token[:, 1:2].astype(jnp.int32)
    w1 = per_token[:, 2:3]
    w2 = per_token[:, 3:4]
    lane = lax.broadcasted_iota(jnp.int32, logits.shape, 1)

    @pl.when(pl.program_id(0) == 0)
    def _():
        cnt_scr[...] = cnt0_ref[...]

    oh1 = lane == i1
    oh2 = lane == i2
    both = jnp.where(oh1 | oh2, 1.0, 0.0)
    cnt = jnp.sum(both, axis=0, keepdims=True)
    cnt = jnp.floor((cnt + (SEG_ALIGN - 1)) * (1.0 / SEG_ALIGN)) * SEG_ALIGN
    lower = _dot(jnp.broadcast_to(cnt, (8, LANES)).astype(BF16), triu_ref[...])[0:1, :]
    lpos = _dot(tril_ref[...], both.astype(BF16)) + lower
    lp1 = jnp.sum(jnp.where(oh1, lpos, 0.0), axis=-1, keepdims=True)
    lp2 = jnp.sum(jnp.where(oh2, lpos, 0.0), axis=-1, keepdims=True)
    pos = lax.broadcasted_iota(jnp.int32, (tm, hs_ref.shape[0]), 1)
    onehot = jnp.where((pos == lp1.astype(jnp.int32)) | (pos == lp2.astype(jnp.int32)), 1.0, 0.0)
    hs_ref[...] = _dot_tn(onehot.astype(BF16), hm_hi).astype(BF16)
    cols = (lp1, lp2, w1, w2)
    meta = jnp.zeros((tm, LANES), F32)
    for ci, col in enumerate(cols):
        meta = jnp.where(lane == ci, col, meta)
    meta_ref[...] = meta
    row8 = lax.broadcasted_iota(jnp.int32, (8, LANES), 0)
    tab_ref[...] = jnp.where(row8 == 0, cnt, jnp.where(row8 == 1, lower, jnp.where(row8 == 2, cnt_scr[...], 0.0)))
    cnt_scr[...] += cnt


def _merge_call(x2, ga, hb, pmg, wug, wum, wo, g2, wr_hi, wr_lo, br, cnt0, tm):
    n = x2.shape[0]
    tile = lambda w: pl.BlockSpec((tm, w), lambda i: (i, 0))
    tiles = lambda *shape: pl.BlockSpec((None,) + shape, lambda i: (i, 0, 0))
    local_rows = 2 * tm + SEG_ALIGN * N_EXPERTS
    consts = [wug, wum, wo, g2, wr_hi, wr_lo, br,
              jnp.asarray(np.tril(np.ones((tm, tm), np.float32), -1), BF16),
              jnp.asarray(np.triu(np.ones((LANES, LANES), np.float32), 1), BF16),
              cnt0]
    return pl.pallas_call(
        _merge_kernel,
        grid=(n // tm,),
        in_specs=[tile(D_MODEL), tile(GLA_V), tile(ML_W), tile(W_MG)]
                 + [_full_spec(x.shape) for x in consts],
        out_specs=[tile(D_MODEL), tiles(local_rows, D_MODEL), tile(LANES), tiles(8, LANES)],
        out_shape=[jax.ShapeDtypeStruct((n, D_MODEL), F32),
                   jax.ShapeDtypeStruct((n // tm, local_rows, D_MODEL), BF16),
                   jax.ShapeDtypeStruct((n, LANES), F32),
                   jax.ShapeDtypeStruct((n // tm, 8, LANES), F32)],
        scratch_shapes=[pltpu.VMEM((1, LANES), F32)],
        compiler_params=pltpu.CompilerParams(
            dimension_semantics=("arbitrary",), vmem_limit_bytes=VMEM_LIMIT),
        name="merge",
    )(x2, ga, hb, pmg, *consts)


MOE_ROWS = 512
SEG_ALIGN = 16


def _segment_copies(fn, tables, tile, local_ref, buffer, flat_ref, sem, to_flat):
    pieces_ref, gdst_ref = tables
    stride = gdst_ref.shape[0] // pieces_ref.shape[0]

    def body(p, carry):
        local = local_ref.at[buffer, pl.ds(pl.multiple_of(p * SEG_ALIGN, SEG_ALIGN), SEG_ALIGN), :]
        flat = flat_ref.at[pl.ds(pl.multiple_of(gdst_ref[tile * stride + p], SEG_ALIGN), SEG_ALIGN), :]
        src, dst = (local, flat) if to_flat else (flat, local)
        fn(pltpu.make_async_copy(src, dst, sem))
        return carry

    lax.fori_loop(0, pieces_ref[tile], body, 0)


def _gmm_kernel(te_ref, nv_ref, part_ref, row_ref, *refs, n_parts):
    hs_refs = refs[:n_parts]
    wg_ref, wu_ref, wd_ref, o_ref, x_buf, sem, wgu_scr, wd_scr = refs[n_parts:]
    j = pl.program_id(0)
    n_valid = nv_ref[0]
    used = j < n_valid
    pieces_per_tile = MOE_ROWS // SEG_ALIGN

    def piece_copy(hs_ref, row, slot, k):
        src = hs_ref.at[pl.ds(pl.multiple_of(row, SEG_ALIGN), SEG_ALIGN), :]
        return pltpu.make_async_copy(src, x_buf.at[slot, pl.ds(k * SEG_ALIGN, SEG_ALIGN), :], sem.at[slot])

    def fetch(tile):
        slot = tile % 2
        for k in range(pieces_per_tile):
            g = tile * pieces_per_tile + k
            row = row_ref[g]
            starts = [functools.partial(lambda hs_ref: piece_copy(hs_ref, row, slot, k).start(), hs_ref)
                      for hs_ref in hs_refs]
            lax.switch(part_ref[g], starts)

    def wait_tile(tile):
        slot = tile % 2
        for k in range(pieces_per_tile):
            piece_copy(hs_refs[0], 0, slot, k).wait()

    @pl.when(j == 0)
    def _():
        fetch(j)

    @pl.when(j + 1 < n_valid)
    def _():
        fetch(j + 1)

    @pl.when(used & ((j == 0) | (te_ref[j] != te_ref[jnp.maximum(j - 1, 0)])))
    def _():
        wgu_scr[:, 0:D_EXPERT] = wg_ref[...].astype(BF16)
        wgu_scr[:, D_EXPERT:] = wu_ref[...].astype(BF16)
        wd_scr[...] = wd_ref[...].astype(BF16)

    @pl.when(used)
    def _():
        wait_tile(j)
        au = _dot(x_buf[j % 2], wgu_scr[...])
        a = au[:, 0:D_EXPERT]
        hh = (a * _sigmoid(a)) * au[:, D_EXPERT:]
        o_ref[...] = _dot(hh.astype(BF16), wd_scr[...]).astype(o_ref.dtype)

    @pl.when(jnp.logical_not(used))
    def _():
        o_ref[...] = jnp.zeros_like(o_ref)


def _gmm_call(tile_expert, n_valid, sources, hs_parts, wg, wu, wd):
    n_tiles = tile_expert.shape[0]
    wsel = lambda j, te, *_: (te[j], 0, 0)
    any_spec = pl.BlockSpec(memory_space=pl.ANY)
    return pl.pallas_call(
        functools.partial(_gmm_kernel, n_parts=len(hs_parts)),
        grid_spec=pltpu.PrefetchScalarGridSpec(
            num_scalar_prefetch=4,
            grid=(n_tiles,),
            in_specs=[any_spec] * len(hs_parts)
                     + [pl.BlockSpec((None, D_MODEL, D_EXPERT), wsel),
                        pl.BlockSpec((None, D_MODEL, D_EXPERT), wsel),
                        pl.BlockSpec((None, D_EXPERT, D_MODEL), wsel)],
            out_specs=pl.BlockSpec((MOE_ROWS, D_MODEL), lambda j, *_: (j, 0)),
            scratch_shapes=[pltpu.VMEM((2, MOE_ROWS, D_MODEL), BF16),
                            pltpu.SemaphoreType.DMA((2,)),
                            pltpu.VMEM((D_MODEL, 2 * D_EXPERT), BF16),
                            pltpu.VMEM((D_EXPERT, D_MODEL), BF16)],
        ),
        out_shape=jax.ShapeDtypeStruct((n_tiles * MOE_ROWS, D_MODEL), BF16),
        compiler_params=pltpu.CompilerParams(
            dimension_semantics=("arbitrary",), vmem_limit_bytes=VMEM_LIMIT),
        name="moe_grouped",
    )(tile_expert, n_valid, *sources, *hs_parts, wg, wu, wd)


def _combine_kernel(pieces_ref, gdst_ref, x1_ref, meta_ref, gf_ref, os_ref, y_ref, buf_ref, sem, *, tile0):
    step = pl.program_id(0)
    n_steps = pl.num_programs(0)
    tm = x1_ref.shape[0]
    tables = (pieces_ref, gdst_ref)

    def fetch(fn, i):
        slot = i % 2
        _segment_copies(fn, tables, tile0 + i, buf_ref, slot, os_ref, sem.at[slot], False)

    @pl.when(step == 0)
    def _():
        buf_ref[...] = jnp.zeros_like(buf_ref)
        fetch(lambda c: c.start(), step)

    @pl.when(step + 1 < n_steps)
    def _():
        fetch(lambda c: c.start(), step + 1)

    fetch(lambda c: c.wait(), step)
    rows = buf_ref[step % 2]
    pos = lax.broadcasted_iota(jnp.int32, (tm, rows.shape[0]), 1)
    lp1 = meta_ref[:, 0:1].astype(jnp.int32)
    lp2 = meta_ref[:, 1:2].astype(jnp.int32)
    q = jnp.where(pos == lp1, meta_ref[:, 2:3], jnp.where(pos == lp2, meta_ref[:, 3:4], 0.0))
    y = x1_ref[...] + _dot(q.astype(BF16), rows)
    y_ref[...] = _rms(y, gf_ref[...])


def _combine_call(tables, x1, meta, gf, out_sorted, tile0, local_rows, tm):
    n = x1.shape[0]
    tile = lambda w: pl.BlockSpec((tm, w), lambda i, *_: (i, 0))
    return pl.pallas_call(
        functools.partial(_combine_kernel, tile0=tile0),
        grid_spec=pltpu.PrefetchScalarGridSpec(
            num_scalar_prefetch=2,
            grid=(n // tm,),
            in_specs=[tile(D_MODEL), tile(LANES), pl.BlockSpec(gf.shape, lambda i, *_: (0, 0)),
                      pl.BlockSpec(memory_space=pl.ANY)],
            out_specs=tile(D_MODEL),
            scratch_shapes=[pltpu.VMEM((2, local_rows, D_MODEL), out_sorted.dtype),
                            pltpu.SemaphoreType.DMA((2,))],
        ),
        out_shape=jax.ShapeDtypeStruct((n, D_MODEL), F32),
        compiler_params=pltpu.CompilerParams(
            dimension_semantics=("arbitrary",), vmem_limit_bytes=VMEM_LIMIT),
        name="moe_combine",
    )(*tables, x1, meta, gf, out_sorted)


def _sparse_moe(parts, p):
    token_tiles = [part[1].shape[0] for part in parts]
    n_tiles = -(-sum(part[1].shape[0] * part[1].shape[1] for part in parts) // MOE_ROWS) + N_EXPERTS
    tab = jnp.concatenate([part[3] for part in parts], axis=0)
    tab = tab[:, :, :N_EXPERTS].astype(jnp.int32)
    cnt, before = tab[:, 0], tab[:, 2]
    total = before[-1] + cnt[-1]
    tiles = (total + (MOE_ROWS - 1)) // MOE_ROWS
    tile_end = jnp.cumsum(tiles)
    tile_start = tile_end - tiles
    n_valid = tile_end[-1:]
    gpos = tile_start[None, :] * MOE_ROWS + before
    j = jnp.minimum(jnp.arange(n_tiles, dtype=jnp.int32), n_valid - 1)
    tile_expert = jnp.sum((j[:, None] >= tile_end[None, :]).astype(jnp.int32), axis=1)
    max_pieces = max(part[1].shape[1] for part in parts) // SEG_ALIGN
    pieces = cnt // SEG_ALIGN
    piece_end = jnp.cumsum(pieces, axis=1)
    piece_start = piece_end - pieces
    piece = jnp.arange(max_pieces, dtype=jnp.int32)
    expert_of = jnp.sum((piece[None, :, None] >= piece_end[:, None, :]).astype(jnp.int32), axis=2)
    onehot = expert_of[:, :, None] == jnp.arange(N_EXPERTS, dtype=jnp.int32)
    first_row = gpos - SEG_ALIGN * piece_start
    gdst = jnp.sum(jnp.where(onehot, first_row[:, None, :], 0), axis=2) + SEG_ALIGN * piece[None, :]
    tables = (piece_end[:, -1], gdst.reshape(-1))
    n_token_tiles = sum(token_tiles)
    over_tiles = jnp.cumsum(pieces, axis=0)
    e_j = tile_expert
    q0 = (jnp.arange(n_tiles, dtype=jnp.int32) - jnp.take(tile_start, e_j)) * (MOE_ROWS // SEG_ALIGN)
    q = q0[:, None] + jnp.arange(MOE_ROWS // SEG_ALIGN, dtype=jnp.int32)[None, :]
    valid = (jnp.arange(n_tiles)[:, None] < n_valid) & (q < jnp.take(over_tiles[-1], e_j)[:, None])
    per_tile = lambda table: jnp.take(table.T, e_j, axis=0)[:, None, :]
    src_tile = jnp.sum((q[:, :, None] >= per_tile(over_tiles)).astype(jnp.int32), axis=2)
    src_tile = jnp.minimum(src_tile, n_token_tiles - 1)
    at_src = lambda table: jnp.sum(jnp.where(
        src_tile[:, :, None] == jnp.arange(n_token_tiles, dtype=jnp.int32), per_tile(table), 0), axis=2)
    src_row = (at_src(piece_start) + q - at_src(over_tiles - pieces)) * SEG_ALIGN
    part_start = np.cumsum([0] + token_tiles)
    src_part = jnp.sum((src_tile[:, :, None] >= jnp.asarray(part_start[1:], jnp.int32)).astype(jnp.int32), axis=2)
    src_tile_in_part = src_tile - jnp.sum(jnp.where(
        src_part[:, :, None] == jnp.arange(len(parts), dtype=jnp.int32), jnp.asarray(part_start[:-1], jnp.int32), 0), axis=2)
    local_rows = jnp.asarray([part[1].shape[1] for part in parts], jnp.int32)
    rows_of_part = jnp.sum(jnp.where(src_part[:, :, None] == jnp.arange(len(parts), dtype=jnp.int32), local_rows, 0), axis=2)
    flat_row = src_tile_in_part * rows_of_part + src_row
    zero_row = parts[0][1].shape[1] - SEG_ALIGN
    assert zero_row >= 2 * (parts[0][0].shape[0] // parts[0][1].shape[0]) + (SEG_ALIGN - 1) * N_EXPERTS
    sources = (jnp.where(valid, src_part, 0).reshape(-1), jnp.where(valid, flat_row, zero_row).reshape(-1))
    out_sorted = _gmm_call(tile_expert, n_valid.astype(jnp.int32), sources,
                           [part[1].reshape(-1, D_MODEL) for part in parts], p["wg"], p["wu"], p["wd"])
    return [_combine_call(tables, part[0], part[2], p["gf"], out_sorted, int(tile0),
                          part[1].shape[1], part[0].shape[0] // part[1].shape[0])
            for part, tile0 in zip(parts, part_start[:-1])]


def _pad_cols(w, width):
    return jnp.pad(w, ((0, 0), (0, width - w.shape[1])))


def _prep_weights(norm1_g, w_in, gla_w_gate2, gla_b_gate, gla_norm_g, w_up_gla,
                  ml_conv_w, ml_conv_b, ml_b_i, ml_b_f, w_up_ml, w_out,
                  norm2_g, router_g_w, router_g_b, router_e_w, router_e_b,
                  moe_w_gate, moe_w_up, moe_w_down, final_g):
    wr =_pad_cols(jnp.concatenate([router_e_w, router_g_w], axis=1), LANES)
    wr_hi = wr.astype(BF16)
    wr_lo = (wr - wr_hi.astype(F32)).astype(BF16)
    br = _pad_cols(jnp.concatenate([router_e_b, router_g_b])[None, :], LANES)
    wg2 = jnp.pad(gla_w_gate2, ((0, LANES - GLA_GATE_RANK), (0, 0)))
    wg2_hi = wg2.astype(BF16)
    return dict(
        g1=norm1_g[None, :], w_in=w_in.T,
        wg2_p=jnp.stack([wg2_hi, (wg2 - wg2_hi.astype(F32)).astype(BF16)]),
        bg=gla_b_gate[None, :], gn=gla_norm_g[None, :],
        wug=w_up_gla.astype(BF16),
        cw=ml_conv_w, cb=ml_conv_b[None, :],
        bif=_pad_cols(jnp.concatenate([ml_b_i, ml_b_f])[None, :], LANES),
        wum=w_up_ml.astype(BF16), wo=w_out.astype(BF16),
        g2=norm2_g[None, :], wr_hi=wr_hi, wr_lo=wr_lo, br=br,
        wg=moe_w_gate.reshape(N_EXPERTS, D_MODEL, D_EXPERT),
        wu=moe_w_up.reshape(N_EXPERTS, D_MODEL, D_EXPERT),
        wd=moe_w_down.reshape(N_EXPERTS, D_EXPERT, D_MODEL),
        gf=final_g[None, :],
    )


def _mixers(x, gla_s0, ml_c0, ml_n0, ml_m0, conv0, p, placed, *,
            gla_chunk, ml_chunk, seq_tile, row_tile, merge_tile):
    b, l, _ = x.shape
    n = b * l
    x2 = x.reshape(n, D_MODEL)
    qk, vr, plr, mqk, mvo, pif, pmg = _proj_call(x2, p["g1"], p["w_in"], row_tile)
    r3 = lambda a: a.reshape(b, l, a.shape[-1])
    s0 = None if gla_s0 is None else gla_s0.reshape(b, GLA_QK, GLA_DV)
    ga, gla_s = _gla_call(r3(qk), r3(vr), r3(plr), s0, p["wg2_p"], p["bg"], p["gn"], gla_chunk, seq_tile)
    gla_s = gla_s.reshape(b, GLA_HEADS, GLA_DK, GLA_DV)
    if ml_c0 is None:
        n0 = m0 = None
    else:
        n0 = ml_n0[:, :, None, :]
        m0 = jnp.broadcast_to(_pad_cols(ml_m0, LANES)[:, None, :], (b, 8, LANES))
    hb, ml_c, ml_n, m_b, new_conv = _mlstm_call(r3(mqk), r3(mvo), r3(pif), ml_c0, n0, m0, conv0,
                                                p["cw"], p["cb"], p["bif"], ml_chunk, seq_tile)
    part = _merge_call(x2, ga.reshape(n, GLA_V), hb.reshape(n, ML_W), pmg,
                       p["wug"], p["wum"], p["wo"], p["g2"],
                       p["wr_hi"], p["wr_lo"], p["br"], placed, merge_tile)
    states = (gla_s[None], ml_c[None], ml_n[:, :, 0, :][None], m_b[:, 0, 0:ML_HEADS][None], new_conv[None])
    return part, states


def kernel(x_prompt, x_sample, state_gla_S, state_mlstm_C, state_mlstm_n, state_mlstm_m, state_mlstm_conv, norm1_g, w_in, gla_w_gate2, gla_b_gate, gla_norm_g, w_up_gla, ml_conv_w, ml_conv_b, ml_b_i, ml_b_f, w_up_ml, w_out, norm2_g, router_g_w, router_g_b, router_e_w, router_e_b, moe_w_gate, moe_w_up, moe_w_down, final_g):
    assert norm1_g.shape[0] == 1, "single-layer trunk"
    p = _prep_weights(norm1_g[0], w_in[0], gla_w_gate2[0], gla_b_gate[0], gla_norm_g[0], w_up_gla[0],
                      ml_conv_w[0], ml_conv_b[0], ml_b_i[0], ml_b_f[0], w_up_ml[0], w_out[0],
                      norm2_g[0], router_g_w[0], router_g_b[0], router_e_w[0], router_e_b[0],
                      moe_w_gate[0], moe_w_up[0], moe_w_down[0], final_g)
    dec_seq = x_sample.shape[1]
    n_sample = x_sample.shape[0] * dec_seq
    part_p, sp = _mixers(x_prompt, None, None, None, None, None, p, jnp.zeros((1, LANES), F32),
                         gla_chunk=128, ml_chunk=256, seq_tile=1024, row_tile=256, merge_tile=512)
    placed = part_p[3][-1, 0:1, :] + part_p[3][-1, 2:3, :]
    part_s, ss = _mixers(x_sample, state_gla_S[0], state_mlstm_C[0], state_mlstm_n[0], state_mlstm_m[0],
                         state_mlstm_conv[0], p, placed,
                         gla_chunk=dec_seq, ml_chunk=dec_seq, seq_tile=dec_seq,
                         row_tile=n_sample, merge_tile=n_sample)
    yp, ys = _sparse_moe([part_p, part_s], p)
    return (yp.reshape(x_prompt.shape), ys.reshape(x_sample.shape), *sp, *ss)
```

```python
import functools
import math

import numpy as np
import jax
import jax.numpy as jnp
from jax import lax
from jax.experimental import pallas as pl
from jax.experimental.pallas import tpu as pltpu

D_MODEL = 1024
GLA_HEADS = 4
GLA_DK = 64
GLA_DV = 128
GLA_GATE_RANK = 16
GLA_TAU = 16.0
ML_HEADS = 4
ML_DH = 128
CONV_W = 4
N_GROUPS = 4
EXPERTS_PER_GROUP = 8
N_EXPERTS = N_GROUPS * EXPERTS_PER_GROUP
D_EXPERT = 256
EPS = 1e-6

GLA_QK = GLA_HEADS * GLA_DK
GLA_V = GLA_HEADS * GLA_DV
ML_W = ML_HEADS * ML_DH

LANES = 128
VMEM_LIMIT = 56 * 1024 * 1024

W_GLA = 2 * GLA_QK + 2 * GLA_V
W_ML = 2 * ML_W + ML_W + ML_W
W_MG = 2 * D_MODEL
PROJ_WIDTHS = (W_GLA, LANES, W_ML, LANES, W_MG)
PROJ_SOURCE_WIDTHS = (W_GLA, GLA_GATE_RANK, W_ML, 2 * ML_HEADS, W_MG)

F32 = jnp.float32
BF16 = jnp.bfloat16

PROJ_OUTPUTS = ((2 * GLA_QK, F32), (2 * GLA_V, BF16), (LANES, F32), (2 * ML_W, F32), (2 * ML_W, BF16),
                (LANES, F32), (W_MG, BF16))


def _dot(a, b):
    return jnp.dot(a, b, preferred_element_type=F32)


def _dot_nt(a, b):
    return lax.dot_general(a, b, (((1,), (1,)), ((), ())), preferred_element_type=F32)


def _dot_tn(a, b):
    return lax.dot_general(a, b, (((0,), (0,)), ((), ())), preferred_element_type=F32)


def _split3(x):
    hi = x.astype(BF16)
    r1 = x - hi.astype(F32)
    mid = r1.astype(BF16)
    lo = (r1 - mid.astype(F32)).astype(BF16)
    return hi, mid, lo


def _dot_exact_lhs(m, x):
    hi, mid, lo = _split3(x)
    return _dot(m, hi) + _dot(m, mid) + _dot(m, lo)


def _log_sigmoid(z):
    return jnp.minimum(z, 0.0) - jnp.log(1.0 + jnp.exp(-jnp.abs(z)))


def _sigmoid(z):
    return 1.0 / (1.0 + jnp.exp(-z))


def _rms(x, g):
    return x * lax.rsqrt(jnp.mean(x * x, axis=-1, keepdims=True) + EPS) * g


def _full_spec(shape):
    nd = len(shape)
    return pl.BlockSpec(shape, lambda *_: (0,) * nd)


CONV_PAD = 8


def _causal_conv_silu(stage_ref, x, cw_ref, cb_ref):
    t = x.shape[0]
    stage_ref[CONV_PAD:CONV_PAD + t, :] = x
    acc = cb_ref[...] + stage_ref[CONV_PAD:CONV_PAD + t, :] * cw_ref[CONV_W - 1:CONV_W, :]
    for d in range(1, CONV_W):
        acc = acc + stage_ref[CONV_PAD - d:CONV_PAD - d + t, :] * cw_ref[CONV_W - 1 - d:CONV_W - d, :]
    stage_ref[0:CONV_PAD, :] = stage_ref[t:t + CONV_PAD, :]
    return acc * _sigmoid(acc)


def _proj_kernel(x_ref, g_ref, win_ref, qk_ref, vr_ref, lr_ref, mqk_ref, mvo_ref, if_ref, mg_ref, w_ref):
    starts = np.cumsum((0,) + PROJ_WIDTHS)

    @pl.when(pl.program_id(0) == 0)
    def _():
        src = np.cumsum((0,) + PROJ_SOURCE_WIDTHS)
        chunk = 512
        for g, width in enumerate(PROJ_SOURCE_WIDTHS):
            for r0 in range(0, width, chunk):
                rows = min(chunk, width - r0)
                w_ref[starts[g] + r0:starts[g] + r0 + rows, :] = (
                    win_ref[src[g] + r0:src[g] + r0 + rows, :].astype(BF16))
            if width < PROJ_WIDTHS[g]:
                w_ref[starts[g] + width:starts[g + 1], :] = jnp.zeros((PROJ_WIDTHS[g] - width, D_MODEL), BF16)

    h = _rms(x_ref[...], g_ref[...]).astype(BF16)

    def cols(group, lo, hi):
        return _dot_nt(h, w_ref[starts[group] + lo:starts[group] + hi, :])

    qk_ref[...] = cols(0, 0, 2 * GLA_QK)
    vr_ref[:, 0:GLA_V] = cols(0, 2 * GLA_QK, 2 * GLA_QK + GLA_V).astype(BF16)
    r = cols(0, 2 * GLA_QK + GLA_V, W_GLA)
    vr_ref[:, GLA_V:] = (r * _sigmoid(r)).astype(BF16)
    lr_ref[...] = cols(1, 0, LANES)
    mqk_ref[...] = cols(2, 0, 2 * ML_W)
    mvo_ref[:, 0:ML_W] = cols(2, 2 * ML_W, 3 * ML_W).astype(BF16)
    mvo_ref[:, ML_W:] = _sigmoid(cols(2, 3 * ML_W, W_ML)).astype(BF16)
    if_ref[...] = cols(3, 0, LANES)
    mg_ref[...] = _sigmoid(cols(4, 0, W_MG)).astype(BF16)


def _proj_call(x2, g, w_in, tm):
    n = x2.shape[0]
    assert w_in.shape == (sum(PROJ_SOURCE_WIDTHS), D_MODEL)
    return pl.pallas_call(
        _proj_kernel,
        grid=(n // tm,),
        in_specs=[pl.BlockSpec((tm, D_MODEL), lambda i: (i, 0)),
                  _full_spec(g.shape),
                  pl.BlockSpec(w_in.shape, lambda i: (0, 0), pipeline_mode=pl.Buffered(1))],
        out_specs=[pl.BlockSpec((tm, w), lambda i: (i, 0)) for w, _ in PROJ_OUTPUTS],
        out_shape=[jax.ShapeDtypeStruct((n, w), dt) for w, dt in PROJ_OUTPUTS],
        scratch_shapes=[pltpu.VMEM((sum(PROJ_WIDTHS), D_MODEL), BF16)],
        compiler_params=pltpu.CompilerParams(
            dimension_semantics=("arbitrary",), vmem_limit_bytes=VMEM_LIMIT),
        name="in_proj",
    )(x2, g, w_in)


def _gla_consts(c):
    nlev = int(math.log2(c))
    assert 1 << nlev == c
    t = np.arange(c)[:, None]
    j = np.arange(c)[None, :]
    lv = np.full((c, c), -1, np.int32)
    for l in range(nlev):
        h = c >> (l + 1)
        upper = (t % (2 * h)) >= h
        same = (j // (2 * h)) == (t // (2 * h))
        s_lower = (j % (2 * h)) < h
        lv[np.broadcast_to(upper, (c, c)) & same & s_lower] = l
    lv[np.eye(c, dtype=bool)] = nlev
    tri = (j <= t).astype(np.float32)
    return jnp.asarray(tri, BF16), jnp.asarray(np.concatenate([lv, lv], axis=1))


def _gla_kernel(*refs, c, t, has_state, streams):
    if has_state:
        (qk_ref, vr_ref, plr_ref, s0_ref, wg2_ref, bg_ref, gn_ref, tri_ref, lv_ref,
         o_ref, sout_ref, s_scr) = refs
    else:
        (qk_ref, vr_ref, plr_ref, wg2_ref, bg_ref, gn_ref, tri_ref, lv_ref,
         o_ref, sout_ref, s_scr) = refs
    nlev = int(math.log2(c))
    step = pl.program_id(1)

    if not streams:
        @pl.when(step == 0)
        def _():
            if has_state:
                s_scr[...] = s0_ref[...]
            else:
                s_scr[...] = jnp.zeros_like(s_scr)

    n_chunks = t // c
    lane_k = lax.broadcasted_iota(jnp.int32, (t, GLA_QK), 1)
    first_of_pair = (lane_k % (2 * GLA_DK)) < GLA_DK
    row_k = lax.broadcasted_iota(jnp.int32, (GLA_QK, GLA_DV), 0)
    row_t = lax.broadcasted_iota(jnp.int32, (t, GLA_QK), 0)

    def block_ref(b, blk, idx):
        if blk >= 8:
            b3 = b.reshape(t // blk, blk, GLA_QK)
            return jnp.broadcast_to(b3[:, idx:idx + 1, :], b3.shape).reshape(t, GLA_QK)
        r = row_t % blk
        out = b
        for sh in range(-idx, blk - idx):
            if sh != 0:
                out = jnp.where(r - idx == sh, pltpu.roll(b, sh % t, axis=0), out)
        return out

    q = qk_ref[:, 0:GLA_QK] * (GLA_DK ** -0.5)
    k = qk_ref[:, GLA_QK:2 * GLA_QK]
    glr = plr_ref[...]
    g_hi = glr.astype(BF16)
    g_lo = (glr - g_hi.astype(F32)).astype(BF16)
    z = (_dot(g_hi, wg2_ref[0]) + _dot(g_lo, wg2_ref[0]) + _dot(g_hi, wg2_ref[1])) + bg_ref[...]
    la = _log_sigmoid(z) * (math.log2(math.e) / GLA_TAU)
    b = jnp.concatenate([_dot_exact_lhs(tri_ref[...], la[ci * c:(ci + 1) * c])
                         for ci in range(n_chunks)], axis=0)
    b_last = block_ref(b, c, c - 1)
    qe = (q * jnp.exp2(b)).astype(BF16)
    kl = (k * jnp.exp2(b_last - b)).astype(BF16)

    k_a = jnp.where(first_of_pair, k, 0.0)
    k_b = k - k_a
    lv2 = lv_ref[...]
    factors = []
    for l in range(nlev + 1):
        if l < nlev:
            half = c >> (l + 1)
            d = b - block_ref(b, 2 * half, half - 1)
            e = jnp.exp2(jnp.minimum(d, -d))
            qt, kta, ktb = q * e, k_a * e, k_b * e
        else:
            qt, kta, ktb = q, k_a, k_b
        factors.append((qt.astype(BF16), kta.astype(BF16), ktb.astype(BF16)))
    a = [[None] * (GLA_HEADS // 2) for _ in range(n_chunks)]
    for ci in range(n_chunks):
        rows = slice(ci * c, (ci + 1) * c)
        for pr in range(GLA_HEADS // 2):
            ls = slice(pr * 2 * GLA_DK, (pr + 1) * 2 * GLA_DK)
            acc = jnp.zeros((c, 2 * c), F32)
            for l, (qt, kta, ktb) in enumerate(factors):
                rhs = jnp.concatenate([kta[rows, ls], ktb[rows, ls]], axis=0)
                acc = jnp.where(lv2 == l, _dot_nt(qt[rows, ls], rhs), acc)
            a[ci][pr] = acc

    for ci in range(n_chunks):
        rows = slice(ci * c, (ci + 1) * c)
        v = vr_ref[rows, 0:GLA_V]
        s_in, s_out = (s0_ref.at[ci], sout_ref.at[ci]) if streams else (s_scr, s_scr)
        s_all = s_in[...]
        s_bd = jnp.concatenate(
            [jnp.where((row_k // GLA_DK) == h, s_all, 0.0).astype(BF16) for h in range(GLA_HEADS)], axis=1)
        o_inter = _dot(qe[rows], s_bd)
        u_all = _dot_tn(kl[rows], v)
        dcol = jnp.exp2(jnp.broadcast_to(b[ci * c + c - 1:(ci + 1) * c, :], (LANES, GLA_QK)).T)
        for h in range(GLA_HEADS):
            vs = slice(h * GLA_DV, (h + 1) * GLA_DV)
            ks = slice(h * GLA_DK, (h + 1) * GLA_DK)
            a_h = a[ci][h // 2][:, (h % 2) * c:(h % 2 + 1) * c]
            o = _dot(a_h.astype(BF16), v[:, vs]) + o_inter[:, vs]
            on = _rms(o, gn_ref[:, vs])
            gate = vr_ref[rows, GLA_V + h * GLA_DV:GLA_V + (h + 1) * GLA_DV]
            o_ref[rows, vs] = (on * gate.astype(F32)).astype(o_ref.dtype)
            s_out[ks, :] = dcol[ks, :] * s_all[ks, :] + u_all[ks, vs]

    if not streams:
        @pl.when(step == pl.num_programs(1) - 1)
        def _():
            sout_ref[...] = s_scr[...]


def _gla_call(qk, vr, plr, s0, wg2_p, bg, gn, c, t):
    seqs, seq_len, _ = qk.shape
    tri, lv2 = _gla_consts(c)
    has_state = s0 is not None
    streams = has_state and seq_len == c
    if streams:
        qk, vr, plr = (a.reshape(1, seqs * seq_len, a.shape[-1]) for a in (qk, vr, plr))
        t = seqs * seq_len
    b, l, _ = qk.shape
    tile = lambda w: pl.BlockSpec((None, t, w), lambda bi, i: (bi, i, 0))
    if streams:
        state_spec = pl.BlockSpec((seqs, GLA_QK, GLA_DV), lambda bi, i: (0, 0, 0))
    else:
        state_spec = pl.BlockSpec((None, GLA_QK, GLA_DV), lambda bi, i: (bi, 0, 0))
    in_specs = [tile(2 * GLA_QK), tile(2 * GLA_V), tile(LANES)]
    args = [qk, vr, plr]
    if has_state:
        in_specs.append(state_spec)
        args.append(s0)
    consts = [wg2_p, bg, gn, tri, lv2]
    in_specs += [_full_spec(x.shape) for x in consts]
    out, state = pl.pallas_call(
        functools.partial(_gla_kernel, c=c, t=t, has_state=has_state, streams=streams),
        grid=(b, l // t),
        in_specs=in_specs,
        out_specs=[tile(GLA_V), state_spec],
        out_shape=[jax.ShapeDtypeStruct((b, l, GLA_V), BF16),
                   jax.ShapeDtypeStruct((seqs, GLA_QK, GLA_DV), F32)],
        scratch_shapes=[pltpu.VMEM((GLA_QK, GLA_DV), F32)],
        compiler_params=pltpu.CompilerParams(
            dimension_semantics=("arbitrary", "arbitrary"), vmem_limit_bytes=VMEM_LIMIT),
        name="gla",
    )(*args, *consts)
    return out.reshape(seqs, seq_len, GLA_V), state


def _mlstm_kernel(*refs, c, t, has_state, streams):
    if has_state:
        (mqk_ref, mvo_ref, pif_ref, c0_ref, n0_ref, m0_ref, cv0_ref, cw_ref, cb_ref, bif_ref, tri_ref, sel_ref,
         o_ref, cout_ref, nout_ref, mout_ref, cvout_ref, c_scr, m_scr, cv_scr, qk_scr) = refs
    else:
        (mqk_ref, mvo_ref, pif_ref, cw_ref, cb_ref, bif_ref, tri_ref, sel_ref,
         o_ref, cout_ref, nout_ref, mout_ref, cvout_ref, c_scr, m_scr, cv_scr, qk_scr) = refs
    step = pl.program_id(1)
    hist = CONV_W - 1

    n_chunks = t // c

    def with_n(c_mat, n_row):
        return jnp.concatenate([c_mat, jnp.broadcast_to(n_row, (ML_DH, ML_DH)).T], axis=1)

    def conv_rows(rows):
        conv = _causal_conv_silu(cv_scr, mqk_ref[rows, :], cw_ref, cb_ref)
        qk_scr[rows, 0:ML_W] = conv[:, 0:ML_W].astype(BF16)
        qk_scr[rows, ML_W:] = (conv[:, ML_W:] * (ML_DH ** -0.5)).astype(BF16)

    if streams:
        for ci in range(n_chunks):
            cv_scr[0:CONV_PAD - hist, :] = jnp.zeros((CONV_PAD - hist, 2 * ML_W), F32)
            cv_scr[CONV_PAD - hist:CONV_PAD, :] = cv0_ref[ci]
            conv_rows(slice(ci * c, (ci + 1) * c))
            cvout_ref[ci] = mqk_ref[(ci + 1) * c - hist:(ci + 1) * c, :]
    else:
        @pl.when(step == 0)
        def _():
            if has_state:
                for h in range(ML_HEADS):
                    c_scr[h] = with_n(c0_ref[h], n0_ref[h])
                m_scr[...] = m0_ref[...] * math.log2(math.e)
                cv_scr[0:CONV_PAD - hist, :] = jnp.zeros((CONV_PAD - hist, 2 * ML_W), F32)
                cv_scr[CONV_PAD - hist:CONV_PAD, :] = cv0_ref[...]
            else:
                c_scr[...] = jnp.zeros_like(c_scr)
                m_scr[...] = jnp.zeros_like(m_scr)
                cv_scr[0:CONV_PAD, :] = jnp.zeros((CONV_PAD, 2 * ML_W), F32)

        conv_rows(slice(0, t))

    log2e = math.log2(math.e)
    pre = pif_ref[...] + bif_ref[...]
    gts = pre * log2e
    flog = pltpu.roll(_log_sigmoid(pre) * log2e, LANES - ML_HEADS, axis=1)

    lane = lax.broadcasted_iota(jnp.int32, (t, LANES), 1)
    row_c = lax.broadcasted_iota(jnp.int32, (t, LANES), 0) % c
    causal = (lax.broadcasted_iota(jnp.int32, (c, c), 1) <= lax.broadcasted_iota(jnp.int32, (c, c), 0))
    ones_v = jnp.ones((c, ML_DH), BF16)

    def slots(pieces):
        out = jnp.zeros((t, LANES), F32)
        for j, piece in enumerate(pieces):
            if not isinstance(piece, float) and j > 0:
                piece = pltpu.roll(piece, ML_HEADS * j, axis=1)
            out = jnp.where((lane >= ML_HEADS * j) & (lane < ML_HEADS * (j + 1)), piece, out)
        return out

    def split3f(x):
        return [p.astype(F32) for p in _split3(x)]

    def per_chunk(rows_of):
        return jnp.concatenate([jnp.broadcast_to(rows_of(ci), (c, LANES)) for ci in range(n_chunks)], axis=0)

    bc = jnp.concatenate([_dot_exact_lhs(tri_ref[...], flog[ci * c:(ci + 1) * c])
                          for ci in range(n_chunks)], axis=0)
    w = gts - bc
    cm = w
    for j in range(int(math.log2(c))):
        sh = 1 << j
        cm = jnp.where(row_c >= sh, jnp.maximum(cm, pltpu.roll(cm, sh, axis=0)), cm)
    m_in, m_out = [], []
    for ci in range(n_chunks):
        last = slice(ci * c + c - 1, (ci + 1) * c)
        if streams:
            m_in.append(m0_ref[ci][0:1, :] * log2e)
        else:
            m_in.append(m_scr[0:1, :] if ci == 0 else m_out[ci - 1])
        m_out.append(bc[last, :] + jnp.maximum(m_in[ci], cm[last, :]))
        if streams:
            mout_ref[ci] = jnp.broadcast_to(m_out[ci] * (1.0 / log2e), mout_ref.shape[1:])
    if not streams:
        m_scr[...] = jnp.broadcast_to(m_out[-1], m_scr.shape)
    mprev = per_chunk(lambda ci: m_in[ci])
    g = jnp.maximum(mprev, cm)
    g_last = per_chunk(lambda ci: g[ci * c + c - 1:(ci + 1) * c, :])
    lhs_all = slots([1.0, 1.0, 1.0] + split3f(g))
    rhs_all = slots(split3f(w) + [-1.0, -1.0, -1.0]).astype(BF16)
    y_all = slots(split3f(mprev - g) + split3f(-(bc + g)) + split3f(w - g_last)).astype(BF16)
    lhs_heads = [jnp.where(lane % ML_HEADS == h, lhs_all, 0.0).astype(BF16) for h in range(ML_HEADS)]

    for ci in range(n_chunks):
        r0 = ci * c
        rows = pl.ds(r0, c)
        for h in range(ML_HEADS):
            hs = slice(h * ML_DH, (h + 1) * ML_DH)
            qh = qk_scr[rows, h * ML_DH:(h + 1) * ML_DH]
            kh = qk_scr[rows, ML_W + h * ML_DH:ML_W + (h + 1) * ML_DH]
            vaug = jnp.concatenate(
                [mvo_ref[rows, h * ML_DH:(h + 1) * ML_DH], ones_v],
                axis=1)
            p = jnp.exp2(jnp.where(causal, _dot_nt(lhs_heads[h][r0:r0 + c], rhs_all[r0:r0 + c]), -jnp.inf))
            bx = jnp.exp2(_dot(y_all[r0:r0 + c], sel_ref[h]))
            w_inter = bx[:, 0:ML_DH]
            e_mt = bx[:, ML_DH:2 * ML_DH]
            w_state = bx[:, 2 * ML_DH:3 * ML_DH]
            s = (_dot_nt(qh, kh) * p).astype(BF16)
            caug = with_n(c0_ref[ci, h], n0_ref[ci, h]) if streams else c_scr[h]
            nd = _dot(s, vaug) + jnp.concatenate([w_inter, w_inter], axis=1) * _dot(qh, caug.astype(BF16))
            hh = nd[:, 0:ML_DH] / jnp.maximum(jnp.abs(nd[:, ML_DH:]), e_mt)
            o_gate = mvo_ref[rows, ML_W + h * ML_DH:ML_W + (h + 1) * ML_DH]
            o_ref[rows, hs] = (o_gate.astype(F32) * hh).astype(o_ref.dtype)
            ks = (kh.astype(F32) * w_state).astype(BF16)
            dec = w_inter[c - 1:c, :]
            new = jnp.concatenate([dec, dec], axis=1) * caug + _dot_tn(ks, vaug)
            if streams:
                cout_ref[ci, h] = new[:, 0:ML_DH]
                nout_ref[ci, h] = new[:, ML_DH:].T[0:8, :]
            else:
                c_scr[h] = new

    if not streams:
        @pl.when(step == pl.num_programs(1) - 1)
        def _():
            for h in range(ML_HEADS):
                cout_ref[h] = c_scr[h, :, 0:ML_DH]
                nout_ref[h] = c_scr[h, :, ML_DH:].T[0:8, :]
            mout_ref[...] = m_scr[...] * (1.0 / log2e)
            cvout_ref[...] = mqk_ref[t - hist:t, :]


def _mlstm_call(mqk, mvo, pif, c0, n0, m0, cv0, cw, cb, bif, c, t):
    seqs, seq_len, _ = mqk.shape
    has_state = c0 is not None
    streams = has_state and seq_len == c
    if streams:
        mqk, mvo, pif = (a.reshape(1, seqs * seq_len, a.shape[-1]) for a in (mqk, mvo, pif))
        t = seqs * seq_len
    b, l, _ = mqk.shape
    tri =jnp.asarray(np.tril(np.ones((c, c), np.float32)), BF16)
    sel = np.zeros((ML_HEADS, LANES, 3 * ML_DH), np.float32)
    for h in range(ML_HEADS):
        for slot in range(9):
            sel[h, ML_HEADS * slot + h, (slot // 3) * ML_DH:(slot // 3 + 1) * ML_DH] = 1.0
    sel = jnp.asarray(sel, BF16)
    tile = lambda w: pl.BlockSpec((None, t, w), lambda bi, i: (bi, i, 0))
    if streams:
        heads = lambda *shape: _full_spec((seqs, ML_HEADS) + shape)
        m_spec = _full_spec((seqs, 8, LANES))
        cv_spec = _full_spec((seqs, CONV_W - 1, 2 * ML_W))
    else:
        heads = lambda *shape: pl.BlockSpec((None, ML_HEADS) + shape, lambda bi, i: (bi, 0, 0, 0))
        m_spec = pl.BlockSpec((None, 8, LANES), lambda bi, i: (bi, 0, 0))
        cv_spec = pl.BlockSpec((None, CONV_W - 1, 2 * ML_W), lambda bi, i: (bi, 0, 0))
    c_spec = heads(ML_DH, ML_DH)
    in_specs = [tile(2 * ML_W), tile(2 * ML_W), tile(LANES)]
    args = [mqk, mvo, pif]
    consts = [cw, cb, bif, tri, sel]
    out_specs = [tile(ML_W), c_spec, heads(8, ML_DH), m_spec, cv_spec]
    out_shape = [jax.ShapeDtypeStruct((b, l, ML_W), BF16),
                 jax.ShapeDtypeStruct((seqs, ML_HEADS, ML_DH, ML_DH), F32),
                 jax.ShapeDtypeStruct((seqs, ML_HEADS, 8, ML_DH), F32),
                 jax.ShapeDtypeStruct((seqs, 8, LANES), F32),
                 jax.ShapeDtypeStruct((seqs, CONV_W - 1, 2 * ML_W), F32)]
    scratch = [pltpu.VMEM((ML_HEADS, ML_DH, 2 * ML_DH), F32), pltpu.VMEM((8, LANES), F32),
               pltpu.VMEM((CONV_PAD + t, 2 * ML_W), F32), pltpu.VMEM((t, 2 * ML_W), BF16)]
    if has_state:
        in_specs += [c_spec, heads(1, ML_DH), m_spec, cv_spec]
        args += [c0, n0, m0, cv0]
    in_specs += [_full_spec(x.shape) for x in consts]
    out, *states = pl.pallas_call(
        functools.partial(_mlstm_kernel, c=c, t=t, has_state=has_state, streams=streams),
        grid=(b, l // t),
        in_specs=in_specs,
        out_specs=out_specs,
        out_shape=out_shape,
        scratch_shapes=scratch,
        compiler_params=pltpu.CompilerParams(
            dimension_semantics=("arbitrary", "arbitrary"), vmem_limit_bytes=VMEM_LIMIT),
        name="mlstm",
    )(*args, *consts)
    return (out.reshape(seqs, seq_len, ML_W), *states)


N_MERGE_INPUTS = 14


def _merge_kernel(*refs, n_real, aliased_hs):
    if aliased_hs:
        refs = refs[:N_MERGE_INPUTS] + refs[N_MERGE_INPUTS + 1:]
    hs_ref = refs[N_MERGE_INPUTS + 1]
    step = pl.program_id(0)

    @pl.when(step < n_real)
    def _():
        _merge_tile(*refs)

    @pl.when(step >= n_real)
    def _():
        hs_ref[...] = jnp.zeros_like(hs_ref)


def _merge_tile(x_ref, ga_ref, hb_ref, pmg_ref, wug_ref, wum_ref, wo_ref, g2_ref,
                wr_hi_ref, wr_lo_ref, br_ref, tril_ref, triu_ref, cnt0_ref,
                x1_ref, hs_ref, meta_ref, tab_ref, cnt_scr):
    ya = _dot(ga_ref[...], wug_ref[...])
    yb = _dot(hb_ref[...], wum_ref[...])
    z = pmg_ref[:, 0:D_MODEL].astype(F32) * ya + pmg_ref[:, D_MODEL:].astype(F32) * yb
    x1 = x_ref[...] + _dot(z.astype(BF16), wo_ref[...])
    x1_ref[...] = x1
    hm = _rms(x1, g2_ref[...])
    hm_hi = hm.astype(BF16)
    hm_lo = (hm - hm_hi.astype(F32)).astype(BF16)
    hi_both = _dot(hm_hi, jnp.concatenate([wr_hi_ref[...], wr_lo_ref[...]], axis=1))
    logits = hi_both[:, 0:LANES] + hi_both[:, LANES:] + _dot(hm_lo, wr_hi_ref[...]) + br_ref[...]
    tm = logits.shape[0]
    n_out = -(-(N_EXPERTS + N_GROUPS) // 8) * 8
    lt = logits.T[0:n_out, :]
    out = lax.broadcasted_iota(jnp.int32, lt.shape, 0)
    neg = -jnp.inf
    is_g = (out >= N_EXPERTS) & (out < N_EXPERTS + N_GROUPS)
    lg = jnp.where(is_g, lt, neg)
    mg = jnp.max(lg, axis=0, keepdims=True)
    p_top = 1.0 / jnp.sum(jnp.exp(lg - mg), axis=0, keepdims=True)
    gi = jnp.min(jnp.where(lg == mg, out, 2 * LANES), axis=0, keepdims=True) - N_EXPERTS
    group_shift = int(math.log2(EXPERTS_PER_GROUP))
    sel = (out < N_EXPERTS) & (jnp.right_shift(out, group_shift) == gi)
    le = jnp.where(sel, lt, neg)
    v1 = jnp.max(le, axis=0, keepdims=True)
    i1 = jnp.min(jnp.where(le == v1, out, 2 * LANES), axis=0, keepdims=True)
    le2 = jnp.where(out == i1, neg, le)
    v2 = jnp.max(le2, axis=0, keepdims=True)
    i2 = jnp.min(jnp.where(le2 == v2, out, 2 * LANES), axis=0, keepdims=True)
    e2 = jnp.exp(v2 - v1)
    per_token = jnp.concatenate(
        [i1.astype(F32), i2.astype(F32), p_top * (1.0 / (1.0 + e2)), p_top * (e2 / (1.0 + e2)),
         jnp.zeros((LANES - 4, tm), F32)], axis=0).T
    i1 = per_token[:, 0:1].astype(jnp.int32)
    i2 = per_token[:, 1:2].astype(jnp.int32)
    w1 = per_token[:, 2:3]
    w2 = per_token[:, 3:4]
    lane = lax.broadcasted_iota(jnp.int32, logits.shape, 1)

    @pl.when(pl.program_id(0) == 0)
    def _():
        cnt_scr[...] = cnt0_ref[...]

    oh1 = lane == i1
    oh2 = lane == i2
    both = jnp.where(oh1 | oh2, 1.0, 0.0)
    cnt = jnp.sum(both, axis=0, keepdims=True)
    cnt = jnp.floor((cnt + (SEG_ALIGN - 1)) * (1.0 / SEG_ALIGN)) * SEG_ALIGN
    lower = _dot(jnp.broadcast_to(cnt, (8, LANES)).astype(BF16), triu_ref[...])[0:1, :]
    lpos = _dot(tril_ref[...], both.astype(BF16)) + lower
    lp1 = jnp.sum(jnp.where(oh1, lpos, 0.0), axis=-1, keepdims=True)
    lp2 = jnp.sum(jnp.where(oh2, lpos, 0.0), axis=-1, keepdims=True)
    pos = lax.broadcasted_iota(jnp.int32, (tm, hs_ref.shape[0]), 1)
    onehot = jnp.where((pos == lp1.astype(jnp.int32)) | (pos == lp2.astype(jnp.int32)), 1.0, 0.0)
    hs_ref[...] = _dot_tn(onehot.astype(BF16), hm_hi).astype(BF16)
    cols = (lp1, lp2, w1, w2)
    meta = jnp.zeros((tm, LANES), F32)
    for ci, col in enumerate(cols):
        meta = jnp.where(lane == ci, col, meta)
    meta_ref[...] = meta
    row8 = lax.broadcasted_iota(jnp.int32, (8, LANES), 0)
    tab_ref[...] = jnp.where(row8 == 0, cnt, jnp.where(row8 == 1, lower, jnp.where(row8 == 2, cnt_scr[...], 0.0)))
    cnt_scr[...] += cnt


def _merge_call(x2, ga, hb, pmg, wug, wum, wo, g2, wr_hi, wr_lo, br, cnt0, tm, spare_tiles=0, hs_all=None):
    n = x2.shape[0]
    n_real = n // tm
    real = lambda i: jnp.minimum(i, n_real - 1)
    tile = lambda w: pl.BlockSpec((tm, w), lambda i: (real(i), 0))
    consts = [wug, wum, wo, g2, wr_hi, wr_lo, br,
              jnp.asarray(np.tril(np.ones((tm, tm), np.float32), -1), BF16),
              jnp.asarray(np.triu(np.ones((LANES, LANES), np.float32), 1), BF16),
              cnt0]
    assert 4 + len(consts) == N_MERGE_INPUTS
    args = [x2, ga, hb, pmg, *consts]
    in_specs = [tile(D_MODEL), tile(GLA_V), tile(ML_W), tile(W_MG)] + [_full_spec(x.shape) for x in consts]
    if hs_all is None:
        local_rows = 2 * tm + SEG_ALIGN * N_EXPERTS
        hs_shape, first_block, aliases = (n_real + spare_tiles, local_rows, D_MODEL), 0, {}
    else:
        hs_shape, first_block, aliases = hs_all.shape, hs_all.shape[0] - n_real, {len(args): 1}
        assert hs_shape[1] >= 2 * tm + SEG_ALIGN * N_EXPERTS
        args.append(hs_all)
        in_specs.append(pl.BlockSpec(memory_space=pl.ANY))
    return pl.pallas_call(
        functools.partial(_merge_kernel, n_real=n_real, aliased_hs=hs_all is not None),
        grid=(n_real + spare_tiles,),
        in_specs=in_specs,
        out_specs=[tile(D_MODEL),
                   pl.BlockSpec((None,) + hs_shape[1:], lambda i: (first_block + i, 0, 0)),
                   tile(LANES),
                   pl.BlockSpec((None, 8, LANES), lambda i: (real(i), 0, 0))],
        out_shape=[jax.ShapeDtypeStruct((n, D_MODEL), F32),
                   jax.ShapeDtypeStruct(hs_shape, BF16),
                   jax.ShapeDtypeStruct((n, LANES), F32),
                   jax.ShapeDtypeStruct((n_real, 8, LANES), F32)],
        scratch_shapes=[pltpu.VMEM((1, LANES), F32)],
        input_output_aliases=aliases,
        compiler_params=pltpu.CompilerParams(
            dimension_semantics=("arbitrary",), vmem_limit_bytes=VMEM_LIMIT),
        name="merge",
    )(*args)


MOE_ROWS = 512
SEG_ALIGN = 16


def _segment_copies(fn, tables, tile, local_ref, buffer, flat_ref, sem, to_flat):
    pieces_ref, gdst_ref = tables
    stride = gdst_ref.shape[0] // pieces_ref.shape[0]

    def body(p, carry):
        local = local_ref.at[buffer, pl.ds(pl.multiple_of(p * SEG_ALIGN, SEG_ALIGN), SEG_ALIGN), :]
        flat = flat_ref.at[pl.ds(pl.multiple_of(gdst_ref[tile * stride + p], SEG_ALIGN), SEG_ALIGN), :]
        src, dst = (local, flat) if to_flat else (flat, local)
        fn(pltpu.make_async_copy(src, dst, sem))
        return carry

    lax.fori_loop(0, pieces_ref[tile], body, 0)


def _gmm_kernel(te_ref, nv_ref, row_ref, hs_ref, wg_ref, wu_ref, wd_ref, o_ref, x_buf, sem, wgu_scr, wd_scr):
    j = pl.program_id(0)
    n_valid = nv_ref[0]
    used = j < n_valid
    pieces_per_tile = MOE_ROWS // SEG_ALIGN

    def piece_copy(row, slot, k):
        src = hs_ref.at[pl.ds(pl.multiple_of(row, SEG_ALIGN), SEG_ALIGN), :]
        return pltpu.make_async_copy(src, x_buf.at[slot, pl.ds(k * SEG_ALIGN, SEG_ALIGN), :], sem.at[slot])

    def fetch(tile):
        for k in range(pieces_per_tile):
            piece_copy(row_ref[tile * pieces_per_tile + k], tile % 2, k).start()

    def wait_tile(tile):
        for k in range(pieces_per_tile):
            piece_copy(0, tile % 2, k).wait()

    @pl.when(j == 0)
    def _():
        fetch(j)

    @pl.when(j + 1 < n_valid)
    def _():
        fetch(j + 1)

    @pl.when(used & ((j == 0) | (te_ref[j] != te_ref[jnp.maximum(j - 1, 0)])))
    def _():
        wgu_scr[:, 0:D_EXPERT] = wg_ref[...].astype(BF16)
        wgu_scr[:, D_EXPERT:] = wu_ref[...].astype(BF16)
        wd_scr[...] = wd_ref[...].astype(BF16)

    @pl.when(used)
    def _():
        wait_tile(j)
        au = _dot(x_buf[j % 2], wgu_scr[...])
        a = au[:, 0:D_EXPERT]
        hh = (a * _sigmoid(a)) * au[:, D_EXPERT:]
        o_ref[...] = _dot(hh.astype(BF16), wd_scr[...]).astype(o_ref.dtype)

    @pl.when(jnp.logical_not(used))
    def _():
        o_ref[...] = jnp.zeros_like(o_ref)


def _gmm_call(tile_expert, n_valid, source_rows, hs_rows, wg, wu, wd):
    n_tiles = tile_expert.shape[0]
    wsel = lambda j, te, *_: (te[j], 0, 0)
    return pl.pallas_call(
        _gmm_kernel,
        grid_spec=pltpu.PrefetchScalarGridSpec(
            num_scalar_prefetch=3,
            grid=(n_tiles,),
            in_specs=[pl.BlockSpec(memory_space=pl.ANY),
                      pl.BlockSpec((None, D_MODEL, D_EXPERT), wsel),
                      pl.BlockSpec((None, D_MODEL, D_EXPERT), wsel),
                      pl.BlockSpec((None, D_EXPERT, D_MODEL), wsel)],
            out_specs=pl.BlockSpec((MOE_ROWS, D_MODEL), lambda j, *_: (j, 0)),
            scratch_shapes=[pltpu.VMEM((2, MOE_ROWS, D_MODEL), BF16),
                            pltpu.SemaphoreType.DMA((2,)),
                            pltpu.VMEM((D_MODEL, 2 * D_EXPERT), BF16),
                            pltpu.VMEM((D_EXPERT, D_MODEL), BF16)],
        ),
        out_shape=jax.ShapeDtypeStruct((n_tiles * MOE_ROWS, D_MODEL), BF16),
        compiler_params=pltpu.CompilerParams(
            dimension_semantics=("arbitrary",), vmem_limit_bytes=VMEM_LIMIT),
        name="moe_grouped",
    )(tile_expert, n_valid, source_rows, hs_rows, wg, wu, wd)


def _combine_kernel(pieces_ref, gdst_ref, x1_ref, meta_ref, gf_ref, os_ref, y_ref, buf_ref, sem, *, tile0):
    step = pl.program_id(0)
    n_steps = pl.num_programs(0)
    tm = x1_ref.shape[0]
    tables = (pieces_ref, gdst_ref)

    def fetch(fn, i):
        slot = i % 2
        _segment_copies(fn, tables, tile0 + i, buf_ref, slot, os_ref, sem.at[slot], False)

    @pl.when(step == 0)
    def _():
        buf_ref[...] = jnp.zeros_like(buf_ref)
        fetch(lambda c: c.start(), step)

    @pl.when(step + 1 < n_steps)
    def _():
        fetch(lambda c: c.start(), step + 1)

    fetch(lambda c: c.wait(), step)
    rows = buf_ref[step % 2]
    pos = lax.broadcasted_iota(jnp.int32, (tm, rows.shape[0]), 1)
    lp1 = meta_ref[:, 0:1].astype(jnp.int32)
    lp2 = meta_ref[:, 1:2].astype(jnp.int32)
    q = jnp.where(pos == lp1, meta_ref[:, 2:3], jnp.where(pos == lp2, meta_ref[:, 3:4], 0.0))
    y = x1_ref[...] + _dot(q.astype(BF16), rows)
    y_ref[...] = _rms(y, gf_ref[...])


def _combine_call(tables, x1, meta, gf, out_sorted, tile0, local_rows, tm):
    n = x1.shape[0]
    tile = lambda w: pl.BlockSpec((tm, w), lambda i, *_: (i, 0))
    return pl.pallas_call(
        functools.partial(_combine_kernel, tile0=tile0),
        grid_spec=pltpu.PrefetchScalarGridSpec(
            num_scalar_prefetch=2,
            grid=(n // tm,),
            in_specs=[tile(D_MODEL), tile(LANES), pl.BlockSpec(gf.shape, lambda i, *_: (0, 0)),
                      pl.BlockSpec(memory_space=pl.ANY)],
            out_specs=tile(D_MODEL),
            scratch_shapes=[pltpu.VMEM((2, local_rows, D_MODEL), out_sorted.dtype),
                            pltpu.SemaphoreType.DMA((2,))],
        ),
        out_shape=jax.ShapeDtypeStruct((n, D_MODEL), F32),
        compiler_params=pltpu.CompilerParams(
            dimension_semantics=("arbitrary",), vmem_limit_bytes=VMEM_LIMIT),
        name="moe_combine",
    )(*tables, x1, meta, gf, out_sorted)


def _sparse_moe(parts, hs_all, p):
    token_tiles = [part[2].shape[0] for part in parts]
    n_token_tiles, local_rows = hs_all.shape[0], hs_all.shape[1]
    assert n_token_tiles == sum(token_tiles)
    n_tiles = -(-(n_token_tiles * local_rows) // MOE_ROWS) + N_EXPERTS
    tab = jnp.concatenate([part[2] for part in parts], axis=0)
    tab = tab[:, :, :N_EXPERTS].astype(jnp.int32)
    cnt, before = tab[:, 0], tab[:, 2]
    total = before[-1] + cnt[-1]
    tiles = (total + (MOE_ROWS - 1)) // MOE_ROWS
    tile_end = jnp.cumsum(tiles)
    tile_start = tile_end - tiles
    n_valid = tile_end[-1:]
    gpos = tile_start[None, :] * MOE_ROWS + before
    j = jnp.minimum(jnp.arange(n_tiles, dtype=jnp.int32), n_valid - 1)
    tile_expert = jnp.sum((j[:, None] >= tile_end[None, :]).astype(jnp.int32), axis=1)
    max_pieces = local_rows // SEG_ALIGN
    pieces = cnt // SEG_ALIGN
    piece_end = jnp.cumsum(pieces, axis=1)
    piece_start = piece_end - pieces
    piece = jnp.arange(max_pieces, dtype=jnp.int32)
    expert_of = jnp.sum((piece[None, :, None] >= piece_end[:, None, :]).astype(jnp.int32), axis=2)
    onehot = expert_of[:, :, None] == jnp.arange(N_EXPERTS, dtype=jnp.int32)
    first_row = gpos - SEG_ALIGN * piece_start
    gdst = jnp.sum(jnp.where(onehot, first_row[:, None, :], 0), axis=2) + SEG_ALIGN * piece[None, :]
    tables = (piece_end[:, -1], gdst.reshape(-1))
    over_tiles = jnp.cumsum(pieces, axis=0)
    e_j = tile_expert
    q0 = (jnp.arange(n_tiles, dtype=jnp.int32) - jnp.take(tile_start, e_j)) * (MOE_ROWS // SEG_ALIGN)
    q = q0[:, None] + jnp.arange(MOE_ROWS // SEG_ALIGN, dtype=jnp.int32)[None, :]
    valid = (jnp.arange(n_tiles)[:, None] < n_valid) & (q < jnp.take(over_tiles[-1], e_j)[:, None])
    per_tile = lambda table: jnp.take(table.T, e_j, axis=0)[:, None, :]
    src_tile = jnp.sum((q[:, :, None] >= per_tile(over_tiles)).astype(jnp.int32), axis=2)
    src_tile = jnp.minimum(src_tile, n_token_tiles - 1)
    at_src = lambda table: jnp.sum(jnp.where(
        src_tile[:, :, None] == jnp.arange(n_token_tiles, dtype=jnp.int32), per_tile(table), 0), axis=2)
    src_row = (at_src(piece_start) + q - at_src(over_tiles - pieces)) * SEG_ALIGN
    flat_row = src_tile * local_rows + src_row
    zero_row = local_rows - SEG_ALIGN
    assert zero_row >= 2 * max(part[0].shape[0] // part[2].shape[0] for part in parts) + (SEG_ALIGN - 1) * N_EXPERTS
    out_sorted = _gmm_call(tile_expert, n_valid.astype(jnp.int32), jnp.where(valid, flat_row, zero_row).reshape(-1),
                           hs_all.reshape(-1, D_MODEL), p["wg"], p["wu"], p["wd"])
    part_start = np.cumsum([0] + token_tiles)
    return [_combine_call(tables, part[0], part[1], p["gf"], out_sorted, int(tile0),
                          local_rows, part[0].shape[0] // part[2].shape[0])
            for part, tile0 in zip(parts, part_start[:-1])]


def _pad_cols(w, width):
    return jnp.pad(w, ((0, 0), (0, width - w.shape[1])))


def _prep_weights(norm1_g, w_in, gla_w_gate2, gla_b_gate, gla_norm_g, w_up_gla,
                  ml_conv_w, ml_conv_b, ml_b_i, ml_b_f, w_up_ml, w_out,
                  norm2_g, router_g_w, router_g_b, router_e_w, router_e_b,
                  moe_w_gate, moe_w_up, moe_w_down, final_g):
    wr =_pad_cols(jnp.concatenate([router_e_w, router_g_w], axis=1), LANES)
    wr_hi = wr.astype(BF16)
    wr_lo = (wr - wr_hi.astype(F32)).astype(BF16)
    br = _pad_cols(jnp.concatenate([router_e_b, router_g_b])[None, :], LANES)
    wg2 = jnp.pad(gla_w_gate2, ((0, LANES - GLA_GATE_RANK), (0, 0)))
    wg2_hi = wg2.astype(BF16)
    return dict(
        g1=norm1_g[None, :], w_in=w_in.T,
        wg2_p=jnp.stack([wg2_hi, (wg2 - wg2_hi.astype(F32)).astype(BF16)]),
        bg=gla_b_gate[None, :], gn=gla_norm_g[None, :],
        wug=w_up_gla.astype(BF16),
        cw=ml_conv_w, cb=ml_conv_b[None, :],
        bif=_pad_cols(jnp.concatenate([ml_b_i, ml_b_f])[None, :], LANES),
        wum=w_up_ml.astype(BF16), wo=w_out.astype(BF16),
        g2=norm2_g[None, :], wr_hi=wr_hi, wr_lo=wr_lo, br=br,
        wg=moe_w_gate.reshape(N_EXPERTS, D_MODEL, D_EXPERT),
        wu=moe_w_up.reshape(N_EXPERTS, D_MODEL, D_EXPERT),
        wd=moe_w_down.reshape(N_EXPERTS, D_EXPERT, D_MODEL),
        gf=final_g[None, :],
    )


def _mixers(x, gla_s0, ml_c0, ml_n0, ml_m0, conv0, p, placed, *,
            gla_chunk, ml_chunk, seq_tile, row_tile, merge_tile, spare_tiles=0, hs_all=None):
    b, l, _ = x.shape
    n = b * l
    x2 = x.reshape(n, D_MODEL)
    qk, vr, plr, mqk, mvo, pif, pmg = _proj_call(x2, p["g1"], p["w_in"], row_tile)
    r3 = lambda a: a.reshape(b, l, a.shape[-1])
    s0 = None if gla_s0 is None else gla_s0.reshape(b, GLA_QK, GLA_DV)
    ga, gla_s = _gla_call(r3(qk), r3(vr), r3(plr), s0, p["wg2_p"], p["bg"], p["gn"], gla_chunk, seq_tile)
    gla_s = gla_s.reshape(b, GLA_HEADS, GLA_DK, GLA_DV)
    if ml_c0 is None:
        n0 = m0 = None
    else:
        n0 = ml_n0[:, :, None, :]
        m0 = jnp.broadcast_to(_pad_cols(ml_m0, LANES)[:, None, :], (b, 8, LANES))
    hb, ml_c, ml_n, m_b, new_conv = _mlstm_call(r3(mqk), r3(mvo), r3(pif), ml_c0, n0, m0, conv0,
                                                p["cw"], p["cb"], p["bif"], ml_chunk, seq_tile)
    part = _merge_call(x2, ga.reshape(n, GLA_V), hb.reshape(n, ML_W), pmg,
                       p["wug"], p["wum"], p["wo"], p["g2"],
                       p["wr_hi"], p["wr_lo"], p["br"], placed, merge_tile, spare_tiles, hs_all)
    states = (gla_s[None], ml_c[None], ml_n[:, :, 0, :][None], m_b[:, 0, 0:ML_HEADS][None], new_conv[None])
    return part, states


def kernel(x_prompt, x_sample, state_gla_S, state_mlstm_C, state_mlstm_n, state_mlstm_m, state_mlstm_conv, norm1_g, w_in, gla_w_gate2, gla_b_gate, gla_norm_g, w_up_gla, ml_conv_w, ml_conv_b, ml_b_i, ml_b_f, w_up_ml, w_out, norm2_g, router_g_w, router_g_b, router_e_w, router_e_b, moe_w_gate, moe_w_up, moe_w_down, final_g):
    assert norm1_g.shape[0] == 1, "single-layer trunk"
    p = _prep_weights(norm1_g[0], w_in[0], gla_w_gate2[0], gla_b_gate[0], gla_norm_g[0], w_up_gla[0],
                      ml_conv_w[0], ml_conv_b[0], ml_b_i[0], ml_b_f[0], w_up_ml[0], w_out[0],
                      norm2_g[0], router_g_w[0], router_g_b[0], router_e_w[0], router_e_b[0],
                      moe_w_gate[0], moe_w_up[0], moe_w_down[0], final_g)
    dec_seq = x_sample.shape[1]
    n_sample = x_sample.shape[0] * dec_seq
    (x1_p, hs_all, meta_p, tab_p), sp = _mixers(
        x_prompt, None, None, None, None, None, p, jnp.zeros((1, LANES), F32),
        gla_chunk=128, ml_chunk=256, seq_tile=1024, row_tile=256, merge_tile=512, spare_tiles=1)
    placed = tab_p[-1, 0:1, :] + tab_p[-1, 2:3, :]
    (x1_s, hs_all, meta_s, tab_s), ss = _mixers(
        x_sample, state_gla_S[0], state_mlstm_C[0], state_mlstm_n[0], state_mlstm_m[0],
        state_mlstm_conv[0], p, placed,
        gla_chunk=dec_seq, ml_chunk=dec_seq, seq_tile=dec_seq,
        row_tile=n_sample, merge_tile=n_sample, hs_all=hs_all)
    yp, ys = _sparse_moe([(x1_p, meta_p, tab_p), (x1_s, meta_s, tab_s)], hs_all, p)
    return (yp.reshape(x_prompt.shape), ys.reshape(x_sample.shape), *sp, *ss)
```

```python
import functools
import math

import numpy as np
import jax
import jax.numpy as jnp
from jax import lax
from jax.experimental import pallas as pl
from jax.experimental.pallas import tpu as pltpu

D_MODEL = 1024
GLA_HEADS = 4
GLA_DK = 64
GLA_DV = 128
GLA_GATE_RANK = 16
GLA_TAU = 16.0
ML_HEADS = 4
ML_DH = 128
CONV_W = 4
N_GROUPS = 4
EXPERTS_PER_GROUP = 8
N_EXPERTS = N_GROUPS * EXPERTS_PER_GROUP
D_EXPERT = 256
EPS = 1e-6

GLA_QK = GLA_HEADS * GLA_DK
GLA_V = GLA_HEADS * GLA_DV
ML_W = ML_HEADS * ML_DH

LANES = 128
VMEM_LIMIT = 56 * 1024 * 1024

W_GLA = 2 * GLA_QK + 2 * GLA_V
W_ML = 2 * ML_W + ML_W + ML_W
W_MG = 2 * D_MODEL
PROJ_WIDTHS = (W_GLA, LANES, W_ML, W_MG)
_W_IN_STARTS = np.cumsum((0, W_GLA, GLA_GATE_RANK, W_ML, 2 * ML_HEADS, W_MG))
PROJ_SOURCES = (((_W_IN_STARTS[0], W_GLA),),
                ((_W_IN_STARTS[1], GLA_GATE_RANK), (_W_IN_STARTS[3], 2 * ML_HEADS)),
                ((_W_IN_STARTS[2], W_ML),),
                ((_W_IN_STARTS[4], W_MG),))

F32 = jnp.float32
BF16 = jnp.bfloat16

PROJ_OUTPUTS = ((2 * GLA_QK, F32), (2 * GLA_V, BF16), (LANES, F32), (2 * ML_W, F32), (2 * ML_W, BF16),
                (W_MG, BF16))


def _dot(a, b):
    return jnp.dot(a, b, preferred_element_type=F32)


def _dot_nt(a, b):
    return lax.dot_general(a, b, (((1,), (1,)), ((), ())), preferred_element_type=F32)


def _dot_tn(a, b):
    return lax.dot_general(a, b, (((0,), (0,)), ((), ())), preferred_element_type=F32)


def _split3(x):
    hi = x.astype(BF16)
    r1 = x - hi.astype(F32)
    mid = r1.astype(BF16)
    lo = (r1 - mid.astype(F32)).astype(BF16)
    return hi, mid, lo


def _dot_exact_lhs(m, x):
    hi, mid, lo = _split3(x)
    return _dot(m, hi) + _dot(m, mid) + _dot(m, lo)


def _log_sigmoid(z):
    return jnp.minimum(z, 0.0) - jnp.log(1.0 + jnp.exp(-jnp.abs(z)))


def _sigmoid(z):
    return 1.0 / (1.0 + jnp.exp(-z))


def _rms(x, g):
    return x * lax.rsqrt(jnp.mean(x * x, axis=-1, keepdims=True) + EPS) * g


def _full_spec(shape):
    nd = len(shape)
    return pl.BlockSpec(shape, lambda *_: (0,) * nd)


CONV_PAD = 8


def _causal_conv_silu(stage_ref, x, cw_ref, cb_ref):
    t = x.shape[0]
    stage_ref[CONV_PAD:CONV_PAD + t, :] = x
    acc = cb_ref[...] + stage_ref[CONV_PAD:CONV_PAD + t, :] * cw_ref[CONV_W - 1:CONV_W, :]
    for d in range(1, CONV_W):
        acc = acc + stage_ref[CONV_PAD - d:CONV_PAD - d + t, :] * cw_ref[CONV_W - 1 - d:CONV_W - d, :]
    stage_ref[0:CONV_PAD, :] = stage_ref[t:t + CONV_PAD, :]
    return acc * _sigmoid(acc)


def _proj_kernel(x_ref, g_ref, win_ref, qk_ref, vr_ref, lr_ref, mqk_ref, mvo_ref, mg_ref, w_ref):
    starts = np.cumsum((0,) + PROJ_WIDTHS)

    @pl.when(pl.program_id(0) == 0)
    def _():
        chunk = 512
        for g, sources in enumerate(PROJ_SOURCES):
            dst = starts[g]
            for src, width in sources:
                for r0 in range(0, width, chunk):
                    rows = min(chunk, width - r0)
                    w_ref[dst + r0:dst + r0 + rows, :] = win_ref[src + r0:src + r0 + rows, :].astype(BF16)
                dst += width
            if dst < starts[g + 1]:
                w_ref[dst:starts[g + 1], :] = jnp.zeros((starts[g + 1] - dst, D_MODEL), BF16)

    h = _rms(x_ref[...], g_ref[...]).astype(BF16)

    def cols(group, lo, hi):
        return _dot_nt(h, w_ref[starts[group] + lo:starts[group] + hi, :])

    qk_ref[...] = cols(0, 0, 2 * GLA_QK)
    vr_ref[:, 0:GLA_V] = cols(0, 2 * GLA_QK, 2 * GLA_QK + GLA_V).astype(BF16)
    r = cols(0, 2 * GLA_QK + GLA_V, W_GLA)
    vr_ref[:, GLA_V:] = (r * _sigmoid(r)).astype(BF16)
    lr_ref[...] = cols(1, 0, LANES)
    mqk_ref[...] = cols(2, 0, 2 * ML_W)
    mvo_ref[:, 0:ML_W] = cols(2, 2 * ML_W, 3 * ML_W).astype(BF16)
    mvo_ref[:, ML_W:] = _sigmoid(cols(2, 3 * ML_W, W_ML)).astype(BF16)
    mg_ref[...] = _sigmoid(cols(3, 0, W_MG)).astype(BF16)


def _proj_call(x2, g, w_in, tm):
    n = x2.shape[0]
    assert w_in.shape == (_W_IN_STARTS[-1], D_MODEL)
    return pl.pallas_call(
        _proj_kernel,
        grid=(n // tm,),
        in_specs=[pl.BlockSpec((tm, D_MODEL), lambda i: (i, 0)),
                  _full_spec(g.shape),
                  pl.BlockSpec(w_in.shape, lambda i: (0, 0), pipeline_mode=pl.Buffered(1))],
        out_specs=[pl.BlockSpec((tm, w), lambda i: (i, 0)) for w, _ in PROJ_OUTPUTS],
        out_shape=[jax.ShapeDtypeStruct((n, w), dt) for w, dt in PROJ_OUTPUTS],
        scratch_shapes=[pltpu.VMEM((sum(PROJ_WIDTHS), D_MODEL), BF16)],
        compiler_params=pltpu.CompilerParams(
            dimension_semantics=("arbitrary",), vmem_limit_bytes=VMEM_LIMIT),
        name="in_proj",
    )(x2, g, w_in)


def _gla_consts(c):
    nlev = int(math.log2(c))
    assert 1 << nlev == c
    t = np.arange(c)[:, None]
    j = np.arange(c)[None, :]
    lv = np.full((c, c), -1, np.int32)
    for l in range(nlev):
        h = c >> (l + 1)
        upper = (t % (2 * h)) >= h
        same = (j // (2 * h)) == (t // (2 * h))
        s_lower = (j % (2 * h)) < h
        lv[np.broadcast_to(upper, (c, c)) & same & s_lower] = l
    lv[np.eye(c, dtype=bool)] = nlev
    tri = (j <= t).astype(np.float32)
    return jnp.asarray(tri, BF16), jnp.asarray(np.concatenate([lv, lv], axis=1))


def _gla_kernel(*refs, c, t, has_state, streams):
    if has_state:
        (qk_ref, vr_ref, plr_ref, s0_ref, wg2_ref, bg_ref, gn_ref, tri_ref, lv_ref,
         o_ref, sout_ref, s_scr) = refs
    else:
        (qk_ref, vr_ref, plr_ref, wg2_ref, bg_ref, gn_ref, tri_ref, lv_ref,
         o_ref, sout_ref, s_scr) = refs
    nlev = int(math.log2(c))
    step = pl.program_id(1)

    if not streams:
        @pl.when(step == 0)
        def _():
            if has_state:
                s_scr[...] = s0_ref[...]
            else:
                s_scr[...] = jnp.zeros_like(s_scr)

    n_chunks = t // c
    lane_k = lax.broadcasted_iota(jnp.int32, (t, GLA_QK), 1)
    first_of_pair = (lane_k % (2 * GLA_DK)) < GLA_DK
    row_k = lax.broadcasted_iota(jnp.int32, (GLA_QK, GLA_DV), 0)
    row_t = lax.broadcasted_iota(jnp.int32, (t, GLA_QK), 0)

    def block_ref(b, blk, idx):
        if blk >= 8:
            b3 = b.reshape(t // blk, blk, GLA_QK)
            return jnp.broadcast_to(b3[:, idx:idx + 1, :], b3.shape).reshape(t, GLA_QK)
        r = row_t % blk
        out = b
        for sh in range(-idx, blk - idx):
            if sh != 0:
                out = jnp.where(r - idx == sh, pltpu.roll(b, sh % t, axis=0), out)
        return out

    q = qk_ref[:, 0:GLA_QK] * (GLA_DK ** -0.5)
    k = qk_ref[:, GLA_QK:2 * GLA_QK]
    glr = plr_ref[...]
    g_hi = glr.astype(BF16)
    g_lo = (glr - g_hi.astype(F32)).astype(BF16)
    z = (_dot(g_hi, wg2_ref[0]) + _dot(g_lo, wg2_ref[0]) + _dot(g_hi, wg2_ref[1])) + bg_ref[...]
    la = _log_sigmoid(z) * (math.log2(math.e) / GLA_TAU)
    b = jnp.concatenate([_dot_exact_lhs(tri_ref[...], la[ci * c:(ci + 1) * c])
                         for ci in range(n_chunks)], axis=0)
    b_last = block_ref(b, c, c - 1)
    qe = (q * jnp.exp2(b)).astype(BF16)
    kl = (k * jnp.exp2(b_last - b)).astype(BF16)

    k_a = jnp.where(first_of_pair, k, 0.0)
    k_b = k - k_a
    lv2 = lv_ref[...]
    factors = []
    for l in range(nlev + 1):
        if l < nlev:
            half = c >> (l + 1)
            d = b - block_ref(b, 2 * half, half - 1)
            e = jnp.exp2(jnp.minimum(d, -d))
            qt, kta, ktb = q * e, k_a * e, k_b * e
        else:
            qt, kta, ktb = q, k_a, k_b
        factors.append((qt.astype(BF16), kta.astype(BF16), ktb.astype(BF16)))
    a = [[None] * (GLA_HEADS // 2) for _ in range(n_chunks)]
    for ci in range(n_chunks):
        rows = slice(ci * c, (ci + 1) * c)
        for pr in range(GLA_HEADS // 2):
            ls = slice(pr * 2 * GLA_DK, (pr + 1) * 2 * GLA_DK)
            acc = jnp.zeros((c, 2 * c), F32)
            for l, (qt, kta, ktb) in enumerate(factors):
                rhs = jnp.concatenate([kta[rows, ls], ktb[rows, ls]], axis=0)
                acc = jnp.where(lv2 == l, _dot_nt(qt[rows, ls], rhs), acc)
            a[ci][pr] = acc

    for ci in range(n_chunks):
        rows = slice(ci * c, (ci + 1) * c)
        v = vr_ref[rows, 0:GLA_V]
        s_in, s_out = (s0_ref.at[ci], sout_ref.at[ci]) if streams else (s_scr, s_scr)
        s_all = s_in[...]
        s_bd = jnp.concatenate(
            [jnp.where((row_k // GLA_DK) == h, s_all, 0.0).astype(BF16) for h in range(GLA_HEADS)], axis=1)
        o_inter = _dot(qe[rows], s_bd)
        u_all = _dot_tn(kl[rows], v)
        dcol = jnp.exp2(jnp.broadcast_to(b[ci * c + c - 1:(ci + 1) * c, :], (LANES, GLA_QK)).T)
        for h in range(GLA_HEADS):
            vs = slice(h * GLA_DV, (h + 1) * GLA_DV)
            ks = slice(h * GLA_DK, (h + 1) * GLA_DK)
            a_h = a[ci][h // 2][:, (h % 2) * c:(h % 2 + 1) * c]
            o = _dot(a_h.astype(BF16), v[:, vs]) + o_inter[:, vs]
            on = _rms(o, gn_ref[:, vs])
            gate = vr_ref[rows, GLA_V + h * GLA_DV:GLA_V + (h + 1) * GLA_DV]
            o_ref[rows, vs] = (on * gate.astype(F32)).astype(o_ref.dtype)
            s_out[ks, :] = dcol[ks, :] * s_all[ks, :] + u_all[ks, vs]

    if not streams:
        @pl.when(step == pl.num_programs(1) - 1)
        def _():
            sout_ref[...] = s_scr[...]


def _gla_call(qk, vr, plr, s0, wg2_p, bg, gn, c, t):
    seqs, seq_len, _ = qk.shape
    tri, lv2 = _gla_consts(c)
    has_state = s0 is not None
    streams = has_state and seq_len == c
    if streams:
        qk, vr, plr = (a.reshape(1, seqs * seq_len, a.shape[-1]) for a in (qk, vr, plr))
        t = seqs * seq_len
    b, l, _ = qk.shape
    tile = lambda w: pl.BlockSpec((None, t, w), lambda bi, i: (bi, i, 0))
    if streams:
        state_spec = pl.BlockSpec((seqs, GLA_QK, GLA_DV), lambda bi, i: (0, 0, 0))
    else:
        state_spec = pl.BlockSpec((None, GLA_QK, GLA_DV), lambda bi, i: (bi, 0, 0))
    in_specs = [tile(2 * GLA_QK), tile(2 * GLA_V), tile(LANES)]
    args = [qk, vr, plr]
    if has_state:
        in_specs.append(state_spec)
        args.append(s0)
    consts = [wg2_p, bg, gn, tri, lv2]
    in_specs += [_full_spec(x.shape) for x in consts]
    out, state = pl.pallas_call(
        functools.partial(_gla_kernel, c=c, t=t, has_state=has_state, streams=streams),
        grid=(b, l // t),
        in_specs=in_specs,
        out_specs=[tile(GLA_V), state_spec],
        out_shape=[jax.ShapeDtypeStruct((b, l, GLA_V), BF16),
                   jax.ShapeDtypeStruct((seqs, GLA_QK, GLA_DV), F32)],
        scratch_shapes=[pltpu.VMEM((GLA_QK, GLA_DV), F32)],
        compiler_params=pltpu.CompilerParams(
            dimension_semantics=("arbitrary", "arbitrary"), vmem_limit_bytes=VMEM_LIMIT),
        name="gla",
    )(*args, *consts)
    return out.reshape(seqs, seq_len, GLA_V), state


def _mlstm_kernel(*refs, c, t, has_state, streams):
    if has_state:
        (mqk_ref, mvo_ref, pif_ref, c0_ref, n0_ref, m0_ref, cv0_ref, cw_ref, cb_ref, bif_ref, tri_ref, sel_ref,
         o_ref, cout_ref, nout_ref, mout_ref, cvout_ref, c_scr, m_scr, cv_scr, qk_scr) = refs
    else:
        (mqk_ref, mvo_ref, pif_ref, cw_ref, cb_ref, bif_ref, tri_ref, sel_ref,
         o_ref, cout_ref, nout_ref, mout_ref, cvout_ref, c_scr, m_scr, cv_scr, qk_scr) = refs
    step = pl.program_id(1)
    hist = CONV_W - 1

    n_chunks = t // c

    def with_n(c_mat, n_row):
        return jnp.concatenate([c_mat, jnp.broadcast_to(n_row, (ML_DH, ML_DH)).T], axis=1)

    def conv_rows(rows):
        conv = _causal_conv_silu(cv_scr, mqk_ref[rows, :], cw_ref, cb_ref)
        qk_scr[rows, 0:ML_W] = conv[:, 0:ML_W].astype(BF16)
        qk_scr[rows, ML_W:] = (conv[:, ML_W:] * (ML_DH ** -0.5)).astype(BF16)

    if streams:
        for ci in range(n_chunks):
            cv_scr[0:CONV_PAD - hist, :] = jnp.zeros((CONV_PAD - hist, 2 * ML_W), F32)
            cv_scr[CONV_PAD - hist:CONV_PAD, :] = cv0_ref[ci]
            conv_rows(slice(ci * c, (ci + 1) * c))
            cvout_ref[ci] = mqk_ref[(ci + 1) * c - hist:(ci + 1) * c, :]
    else:
        @pl.when(step == 0)
        def _():
            if has_state:
                for h in range(ML_HEADS):
                    c_scr[h] = with_n(c0_ref[h], n0_ref[h])
                m_scr[...] = m0_ref[...] * math.log2(math.e)
                cv_scr[0:CONV_PAD - hist, :] = jnp.zeros((CONV_PAD - hist, 2 * ML_W), F32)
                cv_scr[CONV_PAD - hist:CONV_PAD, :] = cv0_ref[...]
            else:
                c_scr[...] = jnp.zeros_like(c_scr)
                m_scr[...] = jnp.zeros_like(m_scr)
                cv_scr[0:CONV_PAD, :] = jnp.zeros((CONV_PAD, 2 * ML_W), F32)

        conv_rows(slice(0, t))

    log2e = math.log2(math.e)
    pre = pltpu.roll(pif_ref[...], LANES - GLA_GATE_RANK, axis=1) + bif_ref[...]
    gts = pre * log2e
    flog = pltpu.roll(_log_sigmoid(pre) * log2e, LANES - ML_HEADS, axis=1)

    lane = lax.broadcasted_iota(jnp.int32, (t, LANES), 1)
    row_c = lax.broadcasted_iota(jnp.int32, (t, LANES), 0) % c
    causal = (lax.broadcasted_iota(jnp.int32, (c, c), 1) <= lax.broadcasted_iota(jnp.int32, (c, c), 0))
    ones_v = jnp.ones((c, ML_DH), BF16)

    def slots(pieces):
        out = jnp.zeros((t, LANES), F32)
        for j, piece in enumerate(pieces):
            if not isinstance(piece, float) and j > 0:
                piece = pltpu.roll(piece, ML_HEADS * j, axis=1)
            out = jnp.where((lane >= ML_HEADS * j) & (lane < ML_HEADS * (j + 1)), piece, out)
        return out

    def split3f(x):
        return [p.astype(F32) for p in _split3(x)]

    def per_chunk(rows_of):
        return jnp.concatenate([jnp.broadcast_to(rows_of(ci), (c, LANES)) for ci in range(n_chunks)], axis=0)

    bc = jnp.concatenate([_dot_exact_lhs(tri_ref[...], flog[ci * c:(ci + 1) * c])
                          for ci in range(n_chunks)], axis=0)
    w = gts - bc
    cm = w
    for j in range(int(math.log2(c))):
        sh = 1 << j
        cm = jnp.where(row_c >= sh, jnp.maximum(cm, pltpu.roll(cm, sh, axis=0)), cm)
    m_in, m_out = [], []
    for ci in range(n_chunks):
        last = slice(ci * c + c - 1, (ci + 1) * c)
        if streams:
            m_in.append(m0_ref[ci][0:1, :] * log2e)
        else:
            m_in.append(m_scr[0:1, :] if ci == 0 else m_out[ci - 1])
        m_out.append(bc[last, :] + jnp.maximum(m_in[ci], cm[last, :]))
        if streams:
            mout_ref[ci] = jnp.broadcast_to(m_out[ci] * (1.0 / log2e), mout_ref.shape[1:])
    if not streams:
        m_scr[...] = jnp.broadcast_to(m_out[-1], m_scr.shape)
    mprev = per_chunk(lambda ci: m_in[ci])
    g = jnp.maximum(mprev, cm)
    g_last = per_chunk(lambda ci: g[ci * c + c - 1:(ci + 1) * c, :])
    lhs_all = slots([1.0, 1.0, 1.0] + split3f(g))
    rhs_all = slots(split3f(w) + [-1.0, -1.0, -1.0]).astype(BF16)
    y_all = slots(split3f(mprev - g) + split3f(-(bc + g)) + split3f(w - g_last)).astype(BF16)
    lhs_heads = [jnp.where(lane % ML_HEADS == h, lhs_all, 0.0).astype(BF16) for h in range(ML_HEADS)]

    for ci in range(n_chunks):
        r0 = ci * c
        rows = pl.ds(r0, c)
        for h in range(ML_HEADS):
            hs = slice(h * ML_DH, (h + 1) * ML_DH)
            qh = qk_scr[rows, h * ML_DH:(h + 1) * ML_DH]
            kh = qk_scr[rows, ML_W + h * ML_DH:ML_W + (h + 1) * ML_DH]
            vaug = jnp.concatenate(
                [mvo_ref[rows, h * ML_DH:(h + 1) * ML_DH], ones_v],
                axis=1)
            p = jnp.exp2(jnp.where(causal, _dot_nt(lhs_heads[h][r0:r0 + c], rhs_all[r0:r0 + c]), -jnp.inf))
            bx = jnp.exp2(_dot(y_all[r0:r0 + c], sel_ref[h]))
            w_inter = bx[:, 0:ML_DH]
            e_mt = bx[:, ML_DH:2 * ML_DH]
            w_state = bx[:, 2 * ML_DH:3 * ML_DH]
            s = (_dot_nt(qh, kh) * p).astype(BF16)
            caug = with_n(c0_ref[ci, h], n0_ref[ci, h]) if streams else c_scr[h]
            nd = _dot(s, vaug) + jnp.concatenate([w_inter, w_inter], axis=1) * _dot(qh, caug.astype(BF16))
            hh = nd[:, 0:ML_DH] / jnp.maximum(jnp.abs(nd[:, ML_DH:]), e_mt)
            o_gate = mvo_ref[rows, ML_W + h * ML_DH:ML_W + (h + 1) * ML_DH]
            o_ref[rows, hs] = (o_gate.astype(F32) * hh).astype(o_ref.dtype)
            ks = (kh.astype(F32) * w_state).astype(BF16)
            dec = w_inter[c - 1:c, :]
            new = jnp.concatenate([dec, dec], axis=1) * caug + _dot_tn(ks, vaug)
            if streams:
                cout_ref[ci, h] = new[:, 0:ML_DH]
                nout_ref[ci, h] = new[:, ML_DH:].T[0:8, :]
            else:
                c_scr[h] = new

    if not streams:
        @pl.when(step == pl.num_programs(1) - 1)
        def _():
            for h in range(ML_HEADS):
                cout_ref[h] = c_scr[h, :, 0:ML_DH]
                nout_ref[h] = c_scr[h, :, ML_DH:].T[0:8, :]
            mout_ref[...] = m_scr[...] * (1.0 / log2e)
            cvout_ref[...] = mqk_ref[t - hist:t, :]


def _mlstm_call(mqk, mvo, pif, c0, n0, m0, cv0, cw, cb, bif, c, t):
    seqs, seq_len, _ = mqk.shape
    has_state = c0 is not None
    streams = has_state and seq_len == c
    if streams:
        mqk, mvo, pif = (a.reshape(1, seqs * seq_len, a.shape[-1]) for a in (mqk, mvo, pif))
        t = seqs * seq_len
    b, l, _ = mqk.shape
    tri =jnp.asarray(np.tril(np.ones((c, c), np.float32)), BF16)
    sel = np.zeros((ML_HEADS, LANES, 3 * ML_DH), np.float32)
    for h in range(ML_HEADS):
        for slot in range(9):
            sel[h, ML_HEADS * slot + h, (slot // 3) * ML_DH:(slot // 3 + 1) * ML_DH] = 1.0
    sel = jnp.asarray(sel, BF16)
    tile = lambda w: pl.BlockSpec((None, t, w), lambda bi, i: (bi, i, 0))
    if streams:
        heads = lambda *shape: _full_spec((seqs, ML_HEADS) + shape)
        m_spec = _full_spec((seqs, 8, LANES))
        cv_spec = _full_spec((seqs, CONV_W - 1, 2 * ML_W))
    else:
        heads = lambda *shape: pl.BlockSpec((None, ML_HEADS) + shape, lambda bi, i: (bi, 0, 0, 0))
        m_spec = pl.BlockSpec((None, 8, LANES), lambda bi, i: (bi, 0, 0))
        cv_spec = pl.BlockSpec((None, CONV_W - 1, 2 * ML_W), lambda bi, i: (bi, 0, 0))
    c_spec = heads(ML_DH, ML_DH)
    in_specs = [tile(2 * ML_W), tile(2 * ML_W), tile(LANES)]
    args = [mqk, mvo, pif]
    consts = [cw, cb, bif, tri, sel]
    out_specs = [tile(ML_W), c_spec, heads(8, ML_DH), m_spec, cv_spec]
    out_shape = [jax.ShapeDtypeStruct((b, l, ML_W), BF16),
                 jax.ShapeDtypeStruct((seqs, ML_HEADS, ML_DH, ML_DH), F32),
                 jax.ShapeDtypeStruct((seqs, ML_HEADS, 8, ML_DH), F32),
                 jax.ShapeDtypeStruct((seqs, 8, LANES), F32),
                 jax.ShapeDtypeStruct((seqs, CONV_W - 1, 2 * ML_W), F32)]
    scratch = [pltpu.VMEM((ML_HEADS, ML_DH, 2 * ML_DH), F32), pltpu.VMEM((8, LANES), F32),
               pltpu.VMEM((CONV_PAD + t, 2 * ML_W), F32), pltpu.VMEM((t, 2 * ML_W), BF16)]
    if has_state:
        in_specs += [c_spec, heads(1, ML_DH), m_spec, cv_spec]
        args += [c0, n0, m0, cv0]
    in_specs += [_full_spec(x.shape) for x in consts]
    out, *states = pl.pallas_call(
        functools.partial(_mlstm_kernel, c=c, t=t, has_state=has_state, streams=streams),
        grid=(b, l // t),
        in_specs=in_specs,
        out_specs=out_specs,
        out_shape=out_shape,
        scratch_shapes=scratch,
        compiler_params=pltpu.CompilerParams(
            dimension_semantics=("arbitrary", "arbitrary"), vmem_limit_bytes=VMEM_LIMIT),
        name="mlstm",
    )(*args, *consts)
    return (out.reshape(seqs, seq_len, ML_W), *states)


N_MERGE_INPUTS = 14


def _merge_kernel(*refs, n_real, aliased_hs):
    if aliased_hs:
        refs = refs[:N_MERGE_INPUTS] + refs[N_MERGE_INPUTS + 1:]
    hs_ref = refs[N_MERGE_INPUTS + 1]
    step = pl.program_id(0)

    @pl.when(step < n_real)
    def _():
        _merge_tile(*refs)

    @pl.when(step >= n_real)
    def _():
        hs_ref[...] = jnp.zeros_like(hs_ref)


def _merge_tile(x_ref, ga_ref, hb_ref, pmg_ref, wug_ref, wum_ref, wo_ref, g2_ref,
                wr_hi_ref, wr_lo_ref, br_ref, tril_ref, triu_ref, cnt0_ref,
                x1_ref, hs_ref, meta_ref, tab_ref, cnt_scr):
    ya = _dot(ga_ref[...], wug_ref[...])
    yb = _dot(hb_ref[...], wum_ref[...])
    z = pmg_ref[:, 0:D_MODEL].astype(F32) * ya + pmg_ref[:, D_MODEL:].astype(F32) * yb
    x1 = x_ref[...] + _dot(z.astype(BF16), wo_ref[...])
    x1_ref[...] = x1
    hm = _rms(x1, g2_ref[...])
    hm_hi = hm.astype(BF16)
    hm_lo = (hm - hm_hi.astype(F32)).astype(BF16)
    hi_both = _dot(hm_hi, jnp.concatenate([wr_hi_ref[...], wr_lo_ref[...]], axis=1))
    logits = hi_both[:, 0:LANES] + hi_both[:, LANES:] + _dot(hm_lo, wr_hi_ref[...]) + br_ref[...]
    tm = logits.shape[0]
    n_out = -(-(N_EXPERTS + N_GROUPS) // 8) * 8
    lt = logits.T[0:n_out, :]
    out = lax.broadcasted_iota(jnp.int32, lt.shape, 0)
    neg = -jnp.inf
    is_g = (out >= N_EXPERTS) & (out < N_EXPERTS + N_GROUPS)
    lg = jnp.where(is_g, lt, neg)
    mg = jnp.max(lg, axis=0, keepdims=True)
    p_top = 1.0 / jnp.sum(jnp.exp(lg - mg), axis=0, keepdims=True)
    gi = jnp.min(jnp.where(lg == mg, out, 2 * LANES), axis=0, keepdims=True) - N_EXPERTS
    group_shift = int(math.log2(EXPERTS_PER_GROUP))
    sel = (out < N_EXPERTS) & (jnp.right_shift(out, group_shift) == gi)
    le = jnp.where(sel, lt, neg)
    v1 = jnp.max(le, axis=0, keepdims=True)
    i1 = jnp.min(jnp.where(le == v1, out, 2 * LANES), axis=0, keepdims=True)
    le2 = jnp.where(out == i1, neg, le)
    v2 = jnp.max(le2, axis=0, keepdims=True)
    i2 = jnp.min(jnp.where(le2 == v2, out, 2 * LANES), axis=0, keepdims=True)
    e2 = jnp.exp(v2 - v1)
    per_token = jnp.concatenate(
        [i1.astype(F32), i2.astype(F32), p_top * (1.0 / (1.0 + e2)), p_top * (e2 / (1.0 + e2)),
         jnp.zeros((LANES - 4, tm), F32)], axis=0).T
    i1 = per_token[:, 0:1].astype(jnp.int32)
    i2 = per_token[:, 1:2].astype(jnp.int32)
    w1 = per_token[:, 2:3]
    w2 = per_token[:, 3:4]
    lane = lax.broadcasted_iota(jnp.int32, logits.shape, 1)

    @pl.when(pl.program_id(0) == 0)
    def _():
        cnt_scr[...] = cnt0_ref[...]

    oh1 = lane == i1
    oh2 = lane == i2
    both = jnp.where(oh1 | oh2, 1.0, 0.0)
    cnt = jnp.sum(both, axis=0, keepdims=True)
    cnt = jnp.floor((cnt + (SEG_ALIGN - 1)) * (1.0 / SEG_ALIGN)) * SEG_ALIGN
    lower = _dot(jnp.broadcast_to(cnt, (8, LANES)).astype(BF16), triu_ref[...])[0:1, :]
    lpos = _dot(tril_ref[...], both.astype(BF16)) + lower
    lp1 = jnp.sum(jnp.where(oh1, lpos, 0.0), axis=-1, keepdims=True)
    lp2 = jnp.sum(jnp.where(oh2, lpos, 0.0), axis=-1, keepdims=True)
    pos = lax.broadcasted_iota(jnp.int32, (tm, hs_ref.shape[0]), 1)
    onehot = jnp.where((pos == lp1.astype(jnp.int32)) | (pos == lp2.astype(jnp.int32)), 1.0, 0.0)
    hs_ref[...] = _dot_tn(onehot.astype(BF16), hm_hi).astype(BF16)
    cols = (lp1, lp2, w1, w2)
    meta = jnp.zeros((tm, LANES), F32)
    for ci, col in enumerate(cols):
        meta = jnp.where(lane == ci, col, meta)
    meta_ref[...] = meta
    row8 = lax.broadcasted_iota(jnp.int32, (8, LANES), 0)
    tab_ref[...] = jnp.where(row8 == 0, cnt, jnp.where(row8 == 1, lower, jnp.where(row8 == 2, cnt_scr[...], 0.0)))
    cnt_scr[...] += cnt


def _merge_call(x2, ga, hb, pmg, wug, wum, wo, g2, wr_hi, wr_lo, br, cnt0, tm, spare_tiles=0, hs_all=None):
    n = x2.shape[0]
    n_real = n // tm
    real = lambda i: jnp.minimum(i, n_real - 1)
    tile = lambda w: pl.BlockSpec((tm, w), lambda i: (real(i), 0))
    consts = [wug, wum, wo, g2, wr_hi, wr_lo, br,
              jnp.asarray(np.tril(np.ones((tm, tm), np.float32), -1), BF16),
              jnp.asarray(np.triu(np.ones((LANES, LANES), np.float32), 1), BF16),
              cnt0]
    assert 4 + len(consts) == N_MERGE_INPUTS
    args = [x2, ga, hb, pmg, *consts]
    in_specs = [tile(D_MODEL), tile(GLA_V), tile(ML_W), tile(W_MG)] + [_full_spec(x.shape) for x in consts]
    if hs_all is None:
        local_rows = 2 * tm + SEG_ALIGN * N_EXPERTS
        hs_shape, first_block, aliases = (n_real + spare_tiles, local_rows, D_MODEL), 0, {}
    else:
        hs_shape, first_block, aliases = hs_all.shape, hs_all.shape[0] - n_real, {len(args): 1}
        assert hs_shape[1] >= 2 * tm + SEG_ALIGN * N_EXPERTS
        args.append(hs_all)
        in_specs.append(pl.BlockSpec(memory_space=pl.ANY))
    return pl.pallas_call(
        functools.partial(_merge_kernel, n_real=n_real, aliased_hs=hs_all is not None),
        grid=(n_real + spare_tiles,),
        in_specs=in_specs,
        out_specs=[tile(D_MODEL),
                   pl.BlockSpec((None,) + hs_shape[1:], lambda i: (first_block + i, 0, 0)),
                   tile(LANES),
                   pl.BlockSpec((None, 8, LANES), lambda i: (real(i), 0, 0))],
        out_shape=[jax.ShapeDtypeStruct((n, D_MODEL), F32),
                   jax.ShapeDtypeStruct(hs_shape, BF16),
                   jax.ShapeDtypeStruct((n, LANES), F32),
                   jax.ShapeDtypeStruct((n_real, 8, LANES), F32)],
        scratch_shapes=[pltpu.VMEM((1, LANES), F32)],
        input_output_aliases=aliases,
        compiler_params=pltpu.CompilerParams(
            dimension_semantics=("arbitrary",), vmem_limit_bytes=VMEM_LIMIT),
        name="merge",
    )(*args)


MOE_ROWS = 512
SEG_ALIGN = 16


def _segment_copies(fn, tables, tile, local_ref, buffer, flat_ref, sem, to_flat):
    pieces_ref, gdst_ref = tables
    stride = gdst_ref.shape[0] // pieces_ref.shape[0]

    def body(p, carry):
        local = local_ref.at[buffer, pl.ds(pl.multiple_of(p * SEG_ALIGN, SEG_ALIGN), SEG_ALIGN), :]
        flat = flat_ref.at[pl.ds(pl.multiple_of(gdst_ref[tile * stride + p], SEG_ALIGN), SEG_ALIGN), :]
        src, dst = (local, flat) if to_flat else (flat, local)
        fn(pltpu.make_async_copy(src, dst, sem))
        return carry

    lax.fori_loop(0, pieces_ref[tile], body, 0)


def _gmm_kernel(te_ref, nv_ref, row_ref, hs_ref, wg_ref, wu_ref, wd_ref, o_ref, x_buf, sem, wgu_scr, wd_scr):
    j = pl.program_id(0)
    n_valid = nv_ref[0]
    used = j < n_valid
    pieces_per_tile = MOE_ROWS // SEG_ALIGN

    def piece_copy(row, slot, k):
        src = hs_ref.at[pl.ds(pl.multiple_of(row, SEG_ALIGN), SEG_ALIGN), :]
        return pltpu.make_async_copy(src, x_buf.at[slot, pl.ds(k * SEG_ALIGN, SEG_ALIGN), :], sem.at[slot])

    def fetch(tile):
        for k in range(pieces_per_tile):
            piece_copy(row_ref[tile * pieces_per_tile + k], tile % 2, k).start()

    def wait_tile(tile):
        for k in range(pieces_per_tile):
            piece_copy(0, tile % 2, k).wait()

    @pl.when(j == 0)
    def _():
        fetch(j)

    @pl.when(j + 1 < n_valid)
    def _():
        fetch(j + 1)

    @pl.when(used & ((j == 0) | (te_ref[j] != te_ref[jnp.maximum(j - 1, 0)])))
    def _():
        wgu_scr[:, 0:D_EXPERT] = wg_ref[...].astype(BF16)
        wgu_scr[:, D_EXPERT:] = wu_ref[...].astype(BF16)
        wd_scr[...] = wd_ref[...].astype(BF16)

    @pl.when(used)
    def _():
        wait_tile(j)
        au = _dot(x_buf[j % 2], wgu_scr[...])
        a = au[:, 0:D_EXPERT]
        hh = (a * _sigmoid(a)) * au[:, D_EXPERT:]
        o_ref[...] = _dot(hh.astype(BF16), wd_scr[...]).astype(o_ref.dtype)

    @pl.when(jnp.logical_not(used))
    def _():
        o_ref[...] = jnp.zeros_like(o_ref)


def _gmm_call(tile_expert, n_valid, source_rows, hs_rows, wg, wu, wd):
    n_tiles = tile_expert.shape[0]
    wsel = lambda j, te, *_: (te[j], 0, 0)
    return pl.pallas_call(
        _gmm_kernel,
        grid_spec=pltpu.PrefetchScalarGridSpec(
            num_scalar_prefetch=3,
            grid=(n_tiles,),
            in_specs=[pl.BlockSpec(memory_space=pl.ANY),
                      pl.BlockSpec((None, D_MODEL, D_EXPERT), wsel),
                      pl.BlockSpec((None, D_MODEL, D_EXPERT), wsel),
                      pl.BlockSpec((None, D_EXPERT, D_MODEL), wsel)],
            out_specs=pl.BlockSpec((MOE_ROWS, D_MODEL), lambda j, *_: (j, 0)),
            scratch_shapes=[pltpu.VMEM((2, MOE_ROWS, D_MODEL), BF16),
                            pltpu.SemaphoreType.DMA((2,)),
                            pltpu.VMEM((D_MODEL, 2 * D_EXPERT), BF16),
                            pltpu.VMEM((D_EXPERT, D_MODEL), BF16)],
        ),
        out_shape=jax.ShapeDtypeStruct((n_tiles * MOE_ROWS, D_MODEL), BF16),
        compiler_params=pltpu.CompilerParams(
            dimension_semantics=("arbitrary",), vmem_limit_bytes=VMEM_LIMIT),
        name="moe_grouped",
    )(tile_expert, n_valid, source_rows, hs_rows, wg, wu, wd)


def _combine_kernel(pieces_ref, gdst_ref, x1_ref, meta_ref, gf_ref, os_ref, y_ref, buf_ref, sem, *, tile0):
    step = pl.program_id(0)
    n_steps = pl.num_programs(0)
    tm = x1_ref.shape[0]
    tables = (pieces_ref, gdst_ref)

    def fetch(fn, i):
        slot = i % 2
        _segment_copies(fn, tables, tile0 + i, buf_ref, slot, os_ref, sem.at[slot], False)

    @pl.when(step == 0)
    def _():
        buf_ref[...] = jnp.zeros_like(buf_ref)
        fetch(lambda c: c.start(), step)

    @pl.when(step + 1 < n_steps)
    def _():
        fetch(lambda c: c.start(), step + 1)

    fetch(lambda c: c.wait(), step)
    rows = buf_ref[step % 2]
    pos = lax.broadcasted_iota(jnp.int32, (tm, rows.shape[0]), 1)
    lp1 = meta_ref[:, 0:1].astype(jnp.int32)
    lp2 = meta_ref[:, 1:2].astype(jnp.int32)
    q = jnp.where(pos == lp1, meta_ref[:, 2:3], jnp.where(pos == lp2, meta_ref[:, 3:4], 0.0))
    y = x1_ref[...] + _dot(q.astype(BF16), rows)
    y_ref[...] = _rms(y, gf_ref[...])


def _combine_call(tables, x1, meta, gf, out_sorted, tile0, local_rows, tm):
    n = x1.shape[0]
    tile = lambda w: pl.BlockSpec((tm, w), lambda i, *_: (i, 0))
    return pl.pallas_call(
        functools.partial(_combine_kernel, tile0=tile0),
        grid_spec=pltpu.PrefetchScalarGridSpec(
            num_scalar_prefetch=2,
            grid=(n // tm,),
            in_specs=[tile(D_MODEL), tile(LANES), pl.BlockSpec(gf.shape, lambda i, *_: (0, 0)),
                      pl.BlockSpec(memory_space=pl.ANY)],
            out_specs=tile(D_MODEL),
            scratch_shapes=[pltpu.VMEM((2, local_rows, D_MODEL), out_sorted.dtype),
                            pltpu.SemaphoreType.DMA((2,))],
        ),
        out_shape=jax.ShapeDtypeStruct((n, D_MODEL), F32),
        compiler_params=pltpu.CompilerParams(
            dimension_semantics=("arbitrary",), vmem_limit_bytes=VMEM_LIMIT),
        name="moe_combine",
    )(*tables, x1, meta, gf, out_sorted)


def _sparse_moe(parts, hs_all, p):
    token_tiles = [part[2].shape[0] for part in parts]
    n_token_tiles, local_rows = hs_all.shape[0], hs_all.shape[1]
    assert n_token_tiles == sum(token_tiles)
    n_tiles = -(-(n_token_tiles * local_rows) // MOE_ROWS) + N_EXPERTS
    tab = jnp.concatenate([part[2] for part in parts], axis=0)
    tab = tab[:, :, :N_EXPERTS].astype(jnp.int32)
    cnt, before = tab[:, 0], tab[:, 2]
    total = before[-1] + cnt[-1]
    tiles = (total + (MOE_ROWS - 1)) // MOE_ROWS
    tile_end = jnp.cumsum(tiles)
    tile_start = tile_end - tiles
    n_valid = tile_end[-1:]
    gpos = tile_start[None, :] * MOE_ROWS + before
    j = jnp.minimum(jnp.arange(n_tiles, dtype=jnp.int32), n_valid - 1)
    tile_expert = jnp.sum((j[:, None] >= tile_end[None, :]).astype(jnp.int32), axis=1)
    max_pieces = local_rows // SEG_ALIGN
    pieces = cnt // SEG_ALIGN
    piece_end = jnp.cumsum(pieces, axis=1)
    piece_start = piece_end - pieces
    piece = jnp.arange(max_pieces, dtype=jnp.int32)
    expert_of = jnp.sum((piece[None, :, None] >= piece_end[:, None, :]).astype(jnp.int32), axis=2)
    onehot = expert_of[:, :, None] == jnp.arange(N_EXPERTS, dtype=jnp.int32)
    first_row = gpos - SEG_ALIGN * piece_start
    gdst = jnp.sum(jnp.where(onehot, first_row[:, None, :], 0), axis=2) + SEG_ALIGN * piece[None, :]
    tables = (piece_end[:, -1], gdst.reshape(-1))
    over_tiles = jnp.cumsum(pieces, axis=0)
    e_j = tile_expert
    q0 = (jnp.arange(n_tiles, dtype=jnp.int32) - jnp.take(tile_start, e_j)) * (MOE_ROWS // SEG_ALIGN)
    q = q0[:, None] + jnp.arange(MOE_ROWS // SEG_ALIGN, dtype=jnp.int32)[None, :]
    valid = (jnp.arange(n_tiles)[:, None] < n_valid) & (q < jnp.take(over_tiles[-1], e_j)[:, None])
    per_tile = lambda table: jnp.take(table.T, e_j, axis=0)[:, None, :]
    src_tile = jnp.sum((q[:, :, None] >= per_tile(over_tiles)).astype(jnp.int32), axis=2)
    src_tile = jnp.minimum(src_tile, n_token_tiles - 1)
    at_src = lambda table: jnp.sum(jnp.where(
        src_tile[:, :, None] == jnp.arange(n_token_tiles, dtype=jnp.int32), per_tile(table), 0), axis=2)
    src_row = (at_src(piece_start) + q - at_src(over_tiles - pieces)) * SEG_ALIGN
    flat_row = src_tile * local_rows + src_row
    zero_row = local_rows - SEG_ALIGN
    assert zero_row >= 2 * max(part[0].shape[0] // part[2].shape[0] for part in parts) + (SEG_ALIGN - 1) * N_EXPERTS
    out_sorted = _gmm_call(tile_expert, n_valid.astype(jnp.int32), jnp.where(valid, flat_row, zero_row).reshape(-1),
                           hs_all.reshape(-1, D_MODEL), p["wg"], p["wu"], p["wd"])
    part_start = np.cumsum([0] + token_tiles)
    return [_combine_call(tables, part[0], part[1], p["gf"], out_sorted, int(tile0),
                          local_rows, part[0].shape[0] // part[2].shape[0])
            for part, tile0 in zip(parts, part_start[:-1])]


def _pad_cols(w, width):
    return jnp.pad(w, ((0, 0), (0, width - w.shape[1])))


def _prep_weights(norm1_g, w_in, gla_w_gate2, gla_b_gate, gla_norm_g, w_up_gla,
                  ml_conv_w, ml_conv_b, ml_b_i, ml_b_f, w_up_ml, w_out,
                  norm2_g, router_g_w, router_g_b, router_e_w, router_e_b,
                  moe_w_gate, moe_w_up, moe_w_down, final_g):
    wr =_pad_cols(jnp.concatenate([router_e_w, router_g_w], axis=1), LANES)
    wr_hi = wr.astype(BF16)
    wr_lo = (wr - wr_hi.astype(F32)).astype(BF16)
    br = _pad_cols(jnp.concatenate([router_e_b, router_g_b])[None, :], LANES)
    wg2 = jnp.pad(gla_w_gate2, ((0, LANES - GLA_GATE_RANK), (0, 0)))
    wg2_hi = wg2.astype(BF16)
    return dict(
        g1=norm1_g[None, :], w_in=w_in.T,
        wg2_p=jnp.stack([wg2_hi, (wg2 - wg2_hi.astype(F32)).astype(BF16)]),
        bg=gla_b_gate[None, :], gn=gla_norm_g[None, :],
        wug=w_up_gla.astype(BF16),
        cw=ml_conv_w, cb=ml_conv_b[None, :],
        bif=_pad_cols(jnp.concatenate([ml_b_i, ml_b_f])[None, :], LANES),
        wum=w_up_ml.astype(BF16), wo=w_out.astype(BF16),
        g2=norm2_g[None, :], wr_hi=wr_hi, wr_lo=wr_lo, br=br,
        wg=moe_w_gate.reshape(N_EXPERTS, D_MODEL, D_EXPERT),
        wu=moe_w_up.reshape(N_EXPERTS, D_MODEL, D_EXPERT),
        wd=moe_w_down.reshape(N_EXPERTS, D_EXPERT, D_MODEL),
        gf=final_g[None, :],
    )


def _mixers(x, gla_s0, ml_c0, ml_n0, ml_m0, conv0, p, placed, *,
            gla_chunk, ml_chunk, seq_tile, row_tile, merge_tile, spare_tiles=0, hs_all=None):
    b, l, _ = x.shape
    n = b * l
    x2 = x.reshape(n, D_MODEL)
    qk, vr, plr, mqk, mvo, pmg = _proj_call(x2, p["g1"], p["w_in"], row_tile)
    pif = plr
    r3 = lambda a: a.reshape(b, l, a.shape[-1])
    s0 = None if gla_s0 is None else gla_s0.reshape(b, GLA_QK, GLA_DV)
    ga, gla_s = _gla_call(r3(qk), r3(vr), r3(plr), s0, p["wg2_p"], p["bg"], p["gn"], gla_chunk, seq_tile)
    gla_s = gla_s.reshape(b, GLA_HEADS, GLA_DK, GLA_DV)
    if ml_c0 is None:
        n0 = m0 = None
    else:
        n0 = ml_n0[:, :, None, :]
        m0 = jnp.broadcast_to(_pad_cols(ml_m0, LANES)[:, None, :], (b, 8, LANES))
    hb, ml_c, ml_n, m_b, new_conv = _mlstm_call(r3(mqk), r3(mvo), r3(pif), ml_c0, n0, m0, conv0,
                                                p["cw"], p["cb"], p["bif"], ml_chunk, seq_tile)
    part = _merge_call(x2, ga.reshape(n, GLA_V), hb.reshape(n, ML_W), pmg,
                       p["wug"], p["wum"], p["wo"], p["g2"],
                       p["wr_hi"], p["wr_lo"], p["br"], placed, merge_tile, spare_tiles, hs_all)
    states = (gla_s[None], ml_c[None], ml_n[:, :, 0, :][None], m_b[:, 0, 0:ML_HEADS][None], new_conv[None])
    return part, states


def kernel(x_prompt, x_sample, state_gla_S, state_mlstm_C, state_mlstm_n, state_mlstm_m, state_mlstm_conv, norm1_g, w_in, gla_w_gate2, gla_b_gate, gla_norm_g, w_up_gla, ml_conv_w, ml_conv_b, ml_b_i, ml_b_f, w_up_ml, w_out, norm2_g, router_g_w, router_g_b, router_e_w, router_e_b, moe_w_gate, moe_w_up, moe_w_down, final_g):
    assert norm1_g.shape[0] == 1, "single-layer trunk"
    p = _prep_weights(norm1_g[0], w_in[0], gla_w_gate2[0], gla_b_gate[0], gla_norm_g[0], w_up_gla[0],
                      ml_conv_w[0], ml_conv_b[0], ml_b_i[0], ml_b_f[0], w_up_ml[0], w_out[0],
                      norm2_g[0], router_g_w[0], router_g_b[0], router_e_w[0], router_e_b[0],
                      moe_w_gate[0], moe_w_up[0], moe_w_down[0], final_g)
    dec_seq = x_sample.shape[1]
    n_sample = x_sample.shape[0] * dec_seq
    (x1_p, hs_all, meta_p, tab_p), sp = _mixers(
        x_prompt, None, None, None, None, None, p, jnp.zeros((1, LANES), F32),
        gla_chunk=128, ml_chunk=256, seq_tile=1024, row_tile=256, merge_tile=512, spare_tiles=1)
    placed = tab_p[-1, 0:1, :] + tab_p[-1, 2:3, :]
    (x1_s, hs_all, meta_s, tab_s), ss = _mixers(
        x_sample, state_gla_S[0], state_mlstm_C[0], state_mlstm_n[0], state_mlstm_m[0],
        state_mlstm_conv[0], p, placed,
        gla_chunk=dec_seq, ml_chunk=dec_seq, seq_tile=dec_seq,
        row_tile=n_sample, merge_tile=n_sample, hs_all=hs_all)
    yp, ys = _sparse_moe([(x1_p, meta_p, tab_p), (x1_s, meta_s, tab_s)], hs_all, p)
    return (yp.reshape(x_prompt.shape), ys.reshape(x_sample.shape), *sp, *ss)
```

```python
import functools
import math

import numpy as np
import jax
import jax.numpy as jnp
from jax import lax
from jax.experimental import pallas as pl
from jax.experimental.pallas import tpu as pltpu

D_MODEL = 1024
GLA_HEADS = 4
GLA_DK = 64
GLA_DV = 128
GLA_GATE_RANK = 16
GLA_TAU = 16.0
ML_HEADS = 4
ML_DH = 128
CONV_W = 4
N_GROUPS = 4
EXPERTS_PER_GROUP = 8
N_EXPERTS = N_GROUPS * EXPERTS_PER_GROUP
D_EXPERT = 256
EPS = 1e-6

GLA_QK = GLA_HEADS * GLA_DK
GLA_V = GLA_HEADS * GLA_DV
ML_W = ML_HEADS * ML_DH

LANES = 128
VMEM_LIMIT = 56 * 1024 * 1024

W_GLA = 2 * GLA_QK + 2 * GLA_V
W_ML = 2 * ML_W + ML_W + ML_W
W_MG = 2 * D_MODEL
PROJ_WIDTHS = (W_GLA, LANES, W_ML, W_MG)
GATE_LANE0 = 16
_W_IN_STARTS = np.cumsum((0, W_GLA, GLA_GATE_RANK, W_ML, 2 * ML_HEADS, W_MG))
PROJ_SOURCES = (((0, _W_IN_STARTS[0], W_GLA),),
                ((0, _W_IN_STARTS[3], 2 * ML_HEADS), (GATE_LANE0, _W_IN_STARTS[1], GLA_GATE_RANK)),
                ((0, _W_IN_STARTS[2], W_ML),),
                ((0, _W_IN_STARTS[4], W_MG),))

F32 = jnp.float32
BF16 = jnp.bfloat16

PROJ_OUTPUTS = ((2 * GLA_QK, F32), (2 * GLA_V, BF16), (LANES, F32), (2 * ML_W, F32), (2 * ML_W, BF16),
                (W_MG, BF16))


def _dot(a, b):
    return jnp.dot(a, b, preferred_element_type=F32)


def _dot_nt(a, b):
    return lax.dot_general(a, b, (((1,), (1,)), ((), ())), preferred_element_type=F32)


def _dot_tn(a, b):
    return lax.dot_general(a, b, (((0,), (0,)), ((), ())), preferred_element_type=F32)


def _split3(x):
    hi = x.astype(BF16)
    r1 = x - hi.astype(F32)
    mid = r1.astype(BF16)
    lo = (r1 - mid.astype(F32)).astype(BF16)
    return hi, mid, lo


def _dot_exact_lhs(m, x):
    hi, mid, lo = _split3(x)
    return _dot(m, hi) + _dot(m, mid) + _dot(m, lo)


def _log_sigmoid(z):
    return jnp.minimum(z, 0.0) - jnp.log(1.0 + jnp.exp(-jnp.abs(z)))


def _sigmoid(z):
    return 1.0 / (1.0 + jnp.exp(-z))


def _rms(x, g):
    return x * lax.rsqrt(jnp.mean(x * x, axis=-1, keepdims=True) + EPS) * g


def _full_spec(shape):
    nd = len(shape)
    return pl.BlockSpec(shape, lambda *_: (0,) * nd)


CONV_PAD = 8


def _causal_conv_silu(stage_ref, x, cw_ref, cb_ref):
    t = x.shape[0]
    stage_ref[CONV_PAD:CONV_PAD + t, :] = x
    acc = cb_ref[...] + stage_ref[CONV_PAD:CONV_PAD + t, :] * cw_ref[CONV_W - 1:CONV_W, :]
    for d in range(1, CONV_W):
        acc = acc + stage_ref[CONV_PAD - d:CONV_PAD - d + t, :] * cw_ref[CONV_W - 1 - d:CONV_W - d, :]
    stage_ref[0:CONV_PAD, :] = stage_ref[t:t + CONV_PAD, :]
    return acc * _sigmoid(acc)


def _proj_kernel(x_ref, g_ref, win_ref, qk_ref, vr_ref, lr_ref, mqk_ref, mvo_ref, mg_ref, w_ref):
    starts = np.cumsum((0,) + PROJ_WIDTHS)

    @pl.when(pl.program_id(0) == 0)
    def _():
        chunk = 512
        for g, sources in enumerate(PROJ_SOURCES):
            if sum(width for _, _, width in sources) < PROJ_WIDTHS[g]:
                w_ref[starts[g]:starts[g + 1], :] = jnp.zeros((PROJ_WIDTHS[g], D_MODEL), BF16)
            for col, src, width in sources:
                dst = starts[g] + col
                for r0 in range(0, width, chunk):
                    rows = min(chunk, width - r0)
                    w_ref[dst + r0:dst + r0 + rows, :] = win_ref[src + r0:src + r0 + rows, :].astype(BF16)

    h = _rms(x_ref[...], g_ref[...]).astype(BF16)

    def cols(group, lo, hi):
        return _dot_nt(h, w_ref[starts[group] + lo:starts[group] + hi, :])

    qk_ref[...] = cols(0, 0, 2 * GLA_QK)
    vr_ref[:, 0:GLA_V] = cols(0, 2 * GLA_QK, 2 * GLA_QK + GLA_V).astype(BF16)
    r = cols(0, 2 * GLA_QK + GLA_V, W_GLA)
    vr_ref[:, GLA_V:] = (r * _sigmoid(r)).astype(BF16)
    lr_ref[...] = cols(1, 0, LANES)
    mqk_ref[...] = cols(2, 0, 2 * ML_W)
    mvo_ref[:, 0:ML_W] = cols(2, 2 * ML_W, 3 * ML_W).astype(BF16)
    mvo_ref[:, ML_W:] = _sigmoid(cols(2, 3 * ML_W, W_ML)).astype(BF16)
    mg_ref[...] = _sigmoid(cols(3, 0, W_MG)).astype(BF16)


def _proj_call(x2, g, w_in, tm):
    n = x2.shape[0]
    assert w_in.shape == (_W_IN_STARTS[-1], D_MODEL)
    return pl.pallas_call(
        _proj_kernel,
        grid=(n // tm,),
        in_specs=[pl.BlockSpec((tm, D_MODEL), lambda i: (i, 0)),
                  _full_spec(g.shape),
                  pl.BlockSpec(w_in.shape, lambda i: (0, 0), pipeline_mode=pl.Buffered(1))],
        out_specs=[pl.BlockSpec((tm, w), lambda i: (i, 0)) for w, _ in PROJ_OUTPUTS],
        out_shape=[jax.ShapeDtypeStruct((n, w), dt) for w, dt in PROJ_OUTPUTS],
        scratch_shapes=[pltpu.VMEM((sum(PROJ_WIDTHS), D_MODEL), BF16)],
        compiler_params=pltpu.CompilerParams(
            dimension_semantics=("arbitrary",), vmem_limit_bytes=VMEM_LIMIT),
        name="in_proj",
    )(x2, g, w_in)


def _gla_consts(c):
    nlev = int(math.log2(c))
    assert 1 << nlev == c
    t = np.arange(c)[:, None]
    j = np.arange(c)[None, :]
    lv = np.full((c, c), -1, np.int32)
    for l in range(nlev):
        h = c >> (l + 1)
        upper = (t % (2 * h)) >= h
        same = (j // (2 * h)) == (t // (2 * h))
        s_lower = (j % (2 * h)) < h
        lv[np.broadcast_to(upper, (c, c)) & same & s_lower] = l
    lv[np.eye(c, dtype=bool)] = nlev
    tri = (j <= t).astype(np.float32)
    return jnp.asarray(tri, BF16), jnp.asarray(np.concatenate([lv, lv], axis=1))


def _gla_kernel(*refs, c, t, has_state, streams):
    if has_state:
        (qk_ref, vr_ref, plr_ref, s0_ref, wg2_ref, bg_ref, gn_ref, tri_ref, lv_ref,
         o_ref, sout_ref, s_scr) = refs
    else:
        (qk_ref, vr_ref, plr_ref, wg2_ref, bg_ref, gn_ref, tri_ref, lv_ref,
         o_ref, sout_ref, s_scr) = refs
    nlev = int(math.log2(c))
    step = pl.program_id(1)

    if not streams:
        @pl.when(step == 0)
        def _():
            if has_state:
                s_scr[...] = s0_ref[...]
            else:
                s_scr[...] = jnp.zeros_like(s_scr)

    n_chunks = t // c
    lane_k = lax.broadcasted_iota(jnp.int32, (t, GLA_QK), 1)
    first_of_pair = (lane_k % (2 * GLA_DK)) < GLA_DK
    row_k = lax.broadcasted_iota(jnp.int32, (GLA_QK, GLA_DV), 0)
    row_t = lax.broadcasted_iota(jnp.int32, (t, GLA_QK), 0)

    def block_ref(b, blk, idx):
        if blk >= 8:
            b3 = b.reshape(t // blk, blk, GLA_QK)
            return jnp.broadcast_to(b3[:, idx:idx + 1, :], b3.shape).reshape(t, GLA_QK)
        r = row_t % blk
        out = b
        for sh in range(-idx, blk - idx):
            if sh != 0:
                out = jnp.where(r - idx == sh, pltpu.roll(b, sh % t, axis=0), out)
        return out

    q = qk_ref[:, 0:GLA_QK] * (GLA_DK ** -0.5)
    k = qk_ref[:, GLA_QK:2 * GLA_QK]
    glr = plr_ref[...]
    g_hi = glr.astype(BF16)
    g_lo = (glr - g_hi.astype(F32)).astype(BF16)
    z = (_dot(g_hi, wg2_ref[0]) + _dot(g_lo, wg2_ref[0]) + _dot(g_hi, wg2_ref[1])) + bg_ref[...]
    la = _log_sigmoid(z) * (math.log2(math.e) / GLA_TAU)
    b = jnp.concatenate([_dot_exact_lhs(tri_ref[...], la[ci * c:(ci + 1) * c])
                         for ci in range(n_chunks)], axis=0)
    b_last = block_ref(b, c, c - 1)
    qe = (q * jnp.exp2(b)).astype(BF16)
    kl = (k * jnp.exp2(b_last - b)).astype(BF16)

    k_a = jnp.where(first_of_pair, k, 0.0)
    k_b = k - k_a
    lv2 = lv_ref[...]
    factors = []
    for l in range(nlev + 1):
        if l < nlev:
            half = c >> (l + 1)
            d = b - block_ref(b, 2 * half, half - 1)
            e = jnp.exp2(jnp.minimum(d, -d))
            qt, kta, ktb = q * e, k_a * e, k_b * e
        else:
            qt, kta, ktb = q, k_a, k_b
        factors.append((qt.astype(BF16), kta.astype(BF16), ktb.astype(BF16)))
    a = [[None] * (GLA_HEADS // 2) for _ in range(n_chunks)]
    for ci in range(n_chunks):
        rows = slice(ci * c, (ci + 1) * c)
        for pr in range(GLA_HEADS // 2):
            ls = slice(pr * 2 * GLA_DK, (pr + 1) * 2 * GLA_DK)
            acc = jnp.zeros((c, 2 * c), F32)
            for l, (qt, kta, ktb) in enumerate(factors):
                rhs = jnp.concatenate([kta[rows, ls], ktb[rows, ls]], axis=0)
                acc = jnp.where(lv2 == l, _dot_nt(qt[rows, ls], rhs), acc)
            a[ci][pr] = acc

    for ci in range(n_chunks):
        rows = slice(ci * c, (ci + 1) * c)
        v = vr_ref[rows, 0:GLA_V]
        s_in, s_out = (s0_ref.at[ci], sout_ref.at[ci]) if streams else (s_scr, s_scr)
        s_all = s_in[...]
        s_bd = jnp.concatenate(
            [jnp.where((row_k // GLA_DK) == h, s_all, 0.0).astype(BF16) for h in range(GLA_HEADS)], axis=1)
        o_inter = _dot(qe[rows], s_bd)
        u_all = _dot_tn(kl[rows], v)
        dcol = jnp.exp2(jnp.broadcast_to(b[ci * c + c - 1:(ci + 1) * c, :], (LANES, GLA_QK)).T)
        for h in range(GLA_HEADS):
            vs = slice(h * GLA_DV, (h + 1) * GLA_DV)
            ks = slice(h * GLA_DK, (h + 1) * GLA_DK)
            a_h = a[ci][h // 2][:, (h % 2) * c:(h % 2 + 1) * c]
            o = _dot(a_h.astype(BF16), v[:, vs]) + o_inter[:, vs]
            on = _rms(o, gn_ref[:, vs])
            gate = vr_ref[rows, GLA_V + h * GLA_DV:GLA_V + (h + 1) * GLA_DV]
            o_ref[rows, vs] = (on * gate.astype(F32)).astype(o_ref.dtype)
            s_out[ks, :] = dcol[ks, :] * s_all[ks, :] + u_all[ks, vs]

    if not streams:
        @pl.when(step == pl.num_programs(1) - 1)
        def _():
            sout_ref[...] = s_scr[...]


def _gla_call(qk, vr, plr, s0, wg2_p, bg, gn, c, t):
    seqs, seq_len, _ = qk.shape
    tri, lv2 = _gla_consts(c)
    has_state = s0 is not None
    streams = has_state and seq_len == c
    if streams:
        qk, vr, plr = (a.reshape(1, seqs * seq_len, a.shape[-1]) for a in (qk, vr, plr))
        t = seqs * seq_len
    b, l, _ = qk.shape
    tile = lambda w: pl.BlockSpec((None, t, w), lambda bi, i: (bi, i, 0))
    if streams:
        state_spec = pl.BlockSpec((seqs, GLA_QK, GLA_DV), lambda bi, i: (0, 0, 0))
    else:
        state_spec = pl.BlockSpec((None, GLA_QK, GLA_DV), lambda bi, i: (bi, 0, 0))
    in_specs = [tile(2 * GLA_QK), tile(2 * GLA_V), tile(LANES)]
    args = [qk, vr, plr]
    if has_state:
        in_specs.append(state_spec)
        args.append(s0)
    consts = [wg2_p, bg, gn, tri, lv2]
    in_specs += [_full_spec(x.shape) for x in consts]
    out, state = pl.pallas_call(
        functools.partial(_gla_kernel, c=c, t=t, has_state=has_state, streams=streams),
        grid=(b, l // t),
        in_specs=in_specs,
        out_specs=[tile(GLA_V), state_spec],
        out_shape=[jax.ShapeDtypeStruct((b, l, GLA_V), BF16),
                   jax.ShapeDtypeStruct((seqs, GLA_QK, GLA_DV), F32)],
        scratch_shapes=[pltpu.VMEM((GLA_QK, GLA_DV), F32)],
        compiler_params=pltpu.CompilerParams(
            dimension_semantics=("arbitrary", "arbitrary"), vmem_limit_bytes=VMEM_LIMIT),
        name="gla",
    )(*args, *consts)
    return out.reshape(seqs, seq_len, GLA_V), state


def _mlstm_kernel(*refs, c, t, has_state, streams):
    if has_state:
        (mqk_ref, mvo_ref, pif_ref, c0_ref, n0_ref, m0_ref, cv0_ref, cw_ref, cb_ref, bif_ref, tri_ref, sel_ref,
         o_ref, cout_ref, nout_ref, mout_ref, cvout_ref, c_scr, m_scr, cv_scr, qk_scr) = refs
    else:
        (mqk_ref, mvo_ref, pif_ref, cw_ref, cb_ref, bif_ref, tri_ref, sel_ref,
         o_ref, cout_ref, nout_ref, mout_ref, cvout_ref, c_scr, m_scr, cv_scr, qk_scr) = refs
    step = pl.program_id(1)
    hist = CONV_W - 1

    n_chunks = t // c

    def with_n(c_mat, n_row):
        return jnp.concatenate([c_mat, jnp.broadcast_to(n_row, (ML_DH, ML_DH)).T], axis=1)

    def conv_rows(rows):
        conv = _causal_conv_silu(cv_scr, mqk_ref[rows, :], cw_ref, cb_ref)
        qk_scr[rows, 0:ML_W] = conv[:, 0:ML_W].astype(BF16)
        qk_scr[rows, ML_W:] = (conv[:, ML_W:] * (ML_DH ** -0.5)).astype(BF16)

    if streams:
        for ci in range(n_chunks):
            cv_scr[0:CONV_PAD - hist, :] = jnp.zeros((CONV_PAD - hist, 2 * ML_W), F32)
            cv_scr[CONV_PAD - hist:CONV_PAD, :] = cv0_ref[ci]
            conv_rows(slice(ci * c, (ci + 1) * c))
            cvout_ref[ci] = mqk_ref[(ci + 1) * c - hist:(ci + 1) * c, :]
    else:
        @pl.when(step == 0)
        def _():
            if has_state:
                for h in range(ML_HEADS):
                    c_scr[h] = with_n(c0_ref[h], n0_ref[h])
                m_scr[...] = m0_ref[...] * math.log2(math.e)
                cv_scr[0:CONV_PAD - hist, :] = jnp.zeros((CONV_PAD - hist, 2 * ML_W), F32)
                cv_scr[CONV_PAD - hist:CONV_PAD, :] = cv0_ref[...]
            else:
                c_scr[...] = jnp.zeros_like(c_scr)
                m_scr[...] = jnp.zeros_like(m_scr)
                cv_scr[0:CONV_PAD, :] = jnp.zeros((CONV_PAD, 2 * ML_W), F32)

        conv_rows(slice(0, t))

    log2e = math.log2(math.e)
    pre = pif_ref[...] + bif_ref[...]
    gts = pre * log2e
    flog = pltpu.roll(_log_sigmoid(pre) * log2e, LANES - ML_HEADS, axis=1)

    lane = lax.broadcasted_iota(jnp.int32, (t, LANES), 1)
    row_c = lax.broadcasted_iota(jnp.int32, (t, LANES), 0) % c
    causal = (lax.broadcasted_iota(jnp.int32, (c, c), 1) <= lax.broadcasted_iota(jnp.int32, (c, c), 0))
    ones_v = jnp.ones((c, ML_DH), BF16)

    def slots(pieces):
        out = jnp.zeros((t, LANES), F32)
        for j, piece in enumerate(pieces):
            if not isinstance(piece, float) and j > 0:
                piece = pltpu.roll(piece, ML_HEADS * j, axis=1)
            out = jnp.where((lane >= ML_HEADS * j) & (lane < ML_HEADS * (j + 1)), piece, out)
        return out

    def split3f(x):
        return [p.astype(F32) for p in _split3(x)]

    def per_chunk(rows_of):
        return jnp.concatenate([jnp.broadcast_to(rows_of(ci), (c, LANES)) for ci in range(n_chunks)], axis=0)

    bc = jnp.concatenate([_dot_exact_lhs(tri_ref[...], flog[ci * c:(ci + 1) * c])
                          for ci in range(n_chunks)], axis=0)
    w = gts - bc
    cm = w
    for j in range(int(math.log2(c))):
        sh = 1 << j
        cm = jnp.where(row_c >= sh, jnp.maximum(cm, pltpu.roll(cm, sh, axis=0)), cm)
    m_in, m_out = [], []
    for ci in range(n_chunks):
        last = slice(ci * c + c - 1, (ci + 1) * c)
        if streams:
            m_in.append(m0_ref[ci][0:1, :] * log2e)
        else:
            m_in.append(m_scr[0:1, :] if ci == 0 else m_out[ci - 1])
        m_out.append(bc[last, :] + jnp.maximum(m_in[ci], cm[last, :]))
        if streams:
            mout_ref[ci] = jnp.broadcast_to(m_out[ci] * (1.0 / log2e), mout_ref.shape[1:])
    if not streams:
        m_scr[...] = jnp.broadcast_to(m_out[-1], m_scr.shape)
    mprev = per_chunk(lambda ci: m_in[ci])
    g = jnp.maximum(mprev, cm)
    g_last = per_chunk(lambda ci: g[ci * c + c - 1:(ci + 1) * c, :])
    lhs_all = slots([1.0, 1.0, 1.0] + split3f(g))
    rhs_all = slots(split3f(w) + [-1.0, -1.0, -1.0]).astype(BF16)
    y_all = slots(split3f(mprev - g) + split3f(-(bc + g)) + split3f(w - g_last)).astype(BF16)
    lhs_heads = [jnp.where(lane % ML_HEADS == h, lhs_all, 0.0).astype(BF16) for h in range(ML_HEADS)]

    for ci in range(n_chunks):
        r0 = ci * c
        rows = pl.ds(r0, c)
        for h in range(ML_HEADS):
            hs = slice(h * ML_DH, (h + 1) * ML_DH)
            qh = qk_scr[rows, h * ML_DH:(h + 1) * ML_DH]
            kh = qk_scr[rows, ML_W + h * ML_DH:ML_W + (h + 1) * ML_DH]
            vaug = jnp.concatenate(
                [mvo_ref[rows, h * ML_DH:(h + 1) * ML_DH], ones_v],
                axis=1)
            p = jnp.exp2(jnp.where(causal, _dot_nt(lhs_heads[h][r0:r0 + c], rhs_all[r0:r0 + c]), -jnp.inf))
            bx = jnp.exp2(_dot(y_all[r0:r0 + c], sel_ref[h]))
            w_inter = bx[:, 0:ML_DH]
            e_mt = bx[:, ML_DH:2 * ML_DH]
            w_state = bx[:, 2 * ML_DH:3 * ML_DH]
            s = (_dot_nt(qh, kh) * p).astype(BF16)
            caug = with_n(c0_ref[ci, h], n0_ref[ci, h]) if streams else c_scr[h]
            nd = _dot(s, vaug) + jnp.concatenate([w_inter, w_inter], axis=1) * _dot(qh, caug.astype(BF16))
            hh = nd[:, 0:ML_DH] / jnp.maximum(jnp.abs(nd[:, ML_DH:]), e_mt)
            o_gate = mvo_ref[rows, ML_W + h * ML_DH:ML_W + (h + 1) * ML_DH]
            o_ref[rows, hs] = (o_gate.astype(F32) * hh).astype(o_ref.dtype)
            ks = (kh.astype(F32) * w_state).astype(BF16)
            dec = w_inter[c - 1:c, :]
            new = jnp.concatenate([dec, dec], axis=1) * caug + _dot_tn(ks, vaug)
            if streams:
                cout_ref[ci, h] = new[:, 0:ML_DH]
                nout_ref[ci, h] = new[:, ML_DH:].T[0:8, :]
            else:
                c_scr[h] = new

    if not streams:
        @pl.when(step == pl.num_programs(1) - 1)
        def _():
            for h in range(ML_HEADS):
                cout_ref[h] = c_scr[h, :, 0:ML_DH]
                nout_ref[h] = c_scr[h, :, ML_DH:].T[0:8, :]
            mout_ref[...] = m_scr[...] * (1.0 / log2e)
            cvout_ref[...] = mqk_ref[t - hist:t, :]


def _mlstm_call(mqk, mvo, pif, c0, n0, m0, cv0, cw, cb, bif, c, t):
    seqs, seq_len, _ = mqk.shape
    has_state = c0 is not None
    streams = has_state and seq_len == c
    if streams:
        mqk, mvo, pif = (a.reshape(1, seqs * seq_len, a.shape[-1]) for a in (mqk, mvo, pif))
        t = seqs * seq_len
    b, l, _ = mqk.shape
    tri =jnp.asarray(np.tril(np.ones((c, c), np.float32)), BF16)
    sel = np.zeros((ML_HEADS, LANES, 3 * ML_DH), np.float32)
    for h in range(ML_HEADS):
        for slot in range(9):
            sel[h, ML_HEADS * slot + h, (slot // 3) * ML_DH:(slot // 3 + 1) * ML_DH] = 1.0
    sel = jnp.asarray(sel, BF16)
    tile = lambda w: pl.BlockSpec((None, t, w), lambda bi, i: (bi, i, 0))
    if streams:
        heads = lambda *shape: _full_spec((seqs, ML_HEADS) + shape)
        m_spec = _full_spec((seqs, 8, LANES))
        cv_spec = _full_spec((seqs, CONV_W - 1, 2 * ML_W))
    else:
        heads = lambda *shape: pl.BlockSpec((None, ML_HEADS) + shape, lambda bi, i: (bi, 0, 0, 0))
        m_spec = pl.BlockSpec((None, 8, LANES), lambda bi, i: (bi, 0, 0))
        cv_spec = pl.BlockSpec((None, CONV_W - 1, 2 * ML_W), lambda bi, i: (bi, 0, 0))
    c_spec = heads(ML_DH, ML_DH)
    in_specs = [tile(2 * ML_W), tile(2 * ML_W), tile(LANES)]
    args = [mqk, mvo, pif]
    consts = [cw, cb, bif, tri, sel]
    out_specs = [tile(ML_W), c_spec, heads(8, ML_DH), m_spec, cv_spec]
    out_shape = [jax.ShapeDtypeStruct((b, l, ML_W), BF16),
                 jax.ShapeDtypeStruct((seqs, ML_HEADS, ML_DH, ML_DH), F32),
                 jax.ShapeDtypeStruct((seqs, ML_HEADS, 8, ML_DH), F32),
                 jax.ShapeDtypeStruct((seqs, 8, LANES), F32),
                 jax.ShapeDtypeStruct((seqs, CONV_W - 1, 2 * ML_W), F32)]
    scratch = [pltpu.VMEM((ML_HEADS, ML_DH, 2 * ML_DH), F32), pltpu.VMEM((8, LANES), F32),
               pltpu.VMEM((CONV_PAD + t, 2 * ML_W), F32), pltpu.VMEM((t, 2 * ML_W), BF16)]
    if has_state:
        in_specs += [c_spec, heads(1, ML_DH), m_spec, cv_spec]
        args += [c0, n0, m0, cv0]
    in_specs += [_full_spec(x.shape) for x in consts]
    out, *states = pl.pallas_call(
        functools.partial(_mlstm_kernel, c=c, t=t, has_state=has_state, streams=streams),
        grid=(b, l // t),
        in_specs=in_specs,
        out_specs=out_specs,
        out_shape=out_shape,
        scratch_shapes=scratch,
        compiler_params=pltpu.CompilerParams(
            dimension_semantics=("arbitrary", "arbitrary"), vmem_limit_bytes=VMEM_LIMIT),
        name="mlstm",
    )(*args, *consts)
    return (out.reshape(seqs, seq_len, ML_W), *states)


N_MERGE_INPUTS = 14


def _merge_kernel(*refs, n_real, aliased_hs):
    if aliased_hs:
        refs = refs[:N_MERGE_INPUTS] + refs[N_MERGE_INPUTS + 1:]
    hs_ref = refs[N_MERGE_INPUTS + 1]
    step = pl.program_id(0)

    @pl.when(step < n_real)
    def _():
        _merge_tile(*refs)

    @pl.when(step >= n_real)
    def _():
        hs_ref[...] = jnp.zeros_like(hs_ref)


def _merge_tile(x_ref, ga_ref, hb_ref, pmg_ref, wug_ref, wum_ref, wo_ref, g2_ref,
                wr_hi_ref, wr_lo_ref, br_ref, tril_ref, triu_ref, cnt0_ref,
                x1_ref, hs_ref, meta_ref, tab_ref, cnt_scr):
    ya = _dot(ga_ref[...], wug_ref[...])
    yb = _dot(hb_ref[...], wum_ref[...])
    z = pmg_ref[:, 0:D_MODEL].astype(F32) * ya + pmg_ref[:, D_MODEL:].astype(F32) * yb
    x1 = x_ref[...] + _dot(z.astype(BF16), wo_ref[...])
    x1_ref[...] = x1
    hm = _rms(x1, g2_ref[...])
    hm_hi = hm.astype(BF16)
    hm_lo = (hm - hm_hi.astype(F32)).astype(BF16)
    hi_both = _dot(hm_hi, jnp.concatenate([wr_hi_ref[...], wr_lo_ref[...]], axis=1))
    logits = hi_both[:, 0:LANES] + hi_both[:, LANES:] + _dot(hm_lo, wr_hi_ref[...]) + br_ref[...]
    tm = logits.shape[0]
    n_out = -(-(N_EXPERTS + N_GROUPS) // 8) * 8
    lt = logits.T[0:n_out, :]
    out = lax.broadcasted_iota(jnp.int32, lt.shape, 0)
    neg = -jnp.inf
    is_g = (out >= N_EXPERTS) & (out < N_EXPERTS + N_GROUPS)
    lg = jnp.where(is_g, lt, neg)
    mg = jnp.max(lg, axis=0, keepdims=True)
    p_top = 1.0 / jnp.sum(jnp.exp(lg - mg), axis=0, keepdims=True)
    gi = jnp.min(jnp.where(lg == mg, out, 2 * LANES), axis=0, keepdims=True) - N_EXPERTS
    group_shift = int(math.log2(EXPERTS_PER_GROUP))
    sel = (out < N_EXPERTS) & (jnp.right_shift(out, group_shift) == gi)
    le = jnp.where(sel, lt, neg)
    v1 = jnp.max(le, axis=0, keepdims=True)
    i1 = jnp.min(jnp.where(le == v1, out, 2 * LANES), axis=0, keepdims=True)
    le2 = jnp.where(out == i1, neg, le)
    v2 = jnp.max(le2, axis=0, keepdims=True)
    i2 = jnp.min(jnp.where(le2 == v2, out, 2 * LANES), axis=0, keepdims=True)
    e2 = jnp.exp(v2 - v1)
    per_token = jnp.concatenate(
        [i1.astype(F32), i2.astype(F32), p_top * (1.0 / (1.0 + e2)), p_top * (e2 / (1.0 + e2)),
         jnp.zeros((LANES - 4, tm), F32)], axis=0).T
    i1 = per_token[:, 0:1].astype(jnp.int32)
    i2 = per_token[:, 1:2].astype(jnp.int32)
    w1 = per_token[:, 2:3]
    w2 = per_token[:, 3:4]
    lane = lax.broadcasted_iota(jnp.int32, logits.shape, 1)

    @pl.when(pl.program_id(0) == 0)
    def _():
        cnt_scr[...] = cnt0_ref[...]

    oh1 = lane == i1
    oh2 = lane == i2
    both = jnp.where(oh1 | oh2, 1.0, 0.0)
    cnt = jnp.sum(both, axis=0, keepdims=True)
    cnt = jnp.floor((cnt + (SEG_ALIGN - 1)) * (1.0 / SEG_ALIGN)) * SEG_ALIGN
    lower = _dot(jnp.broadcast_to(cnt, (8, LANES)).astype(BF16), triu_ref[...])[0:1, :]
    lpos = _dot(tril_ref[...], both.astype(BF16)) + lower
    lp1 = jnp.sum(jnp.where(oh1, lpos, 0.0), axis=-1, keepdims=True)
    lp2 = jnp.sum(jnp.where(oh2, lpos, 0.0), axis=-1, keepdims=True)
    pos = lax.broadcasted_iota(jnp.int32, (tm, hs_ref.shape[0]), 1)
    onehot = jnp.where((pos == lp1.astype(jnp.int32)) | (pos == lp2.astype(jnp.int32)), 1.0, 0.0)
    hs_ref[...] = _dot_tn(onehot.astype(BF16), hm_hi).astype(BF16)
    cols = (lp1, lp2, w1, w2)
    meta = jnp.zeros((tm, LANES), F32)
    for ci, col in enumerate(cols):
        meta = jnp.where(lane == ci, col, meta)
    meta_ref[...] = meta
    row8 = lax.broadcasted_iota(jnp.int32, (8, LANES), 0)
    tab_ref[...] = jnp.where(row8 == 0, cnt, jnp.where(row8 == 1, lower, jnp.where(row8 == 2, cnt_scr[...], 0.0)))
    cnt_scr[...] += cnt


def _merge_call(x2, ga, hb, pmg, wug, wum, wo, g2, wr_hi, wr_lo, br, cnt0, tm, spare_tiles=0, hs_all=None):
    n = x2.shape[0]
    n_real = n // tm
    real = lambda i: jnp.minimum(i, n_real - 1)
    tile = lambda w: pl.BlockSpec((tm, w), lambda i: (real(i), 0))
    consts = [wug, wum, wo, g2, wr_hi, wr_lo, br,
              jnp.asarray(np.tril(np.ones((tm, tm), np.float32), -1), BF16),
              jnp.asarray(np.triu(np.ones((LANES, LANES), np.float32), 1), BF16),
              cnt0]
    assert 4 + len(consts) == N_MERGE_INPUTS
    args = [x2, ga, hb, pmg, *consts]
    in_specs = [tile(D_MODEL), tile(GLA_V), tile(ML_W), tile(W_MG)] + [_full_spec(x.shape) for x in consts]
    if hs_all is None:
        local_rows = 2 * tm + SEG_ALIGN * N_EXPERTS
        hs_shape, first_block, aliases = (n_real + spare_tiles, local_rows, D_MODEL), 0, {}
    else:
        hs_shape, first_block, aliases = hs_all.shape, hs_all.shape[0] - n_real, {len(args): 1}
        assert hs_shape[1] >= 2 * tm + SEG_ALIGN * N_EXPERTS
        args.append(hs_all)
        in_specs.append(pl.BlockSpec(memory_space=pl.ANY))
    return pl.pallas_call(
        functools.partial(_merge_kernel, n_real=n_real, aliased_hs=hs_all is not None),
        grid=(n_real + spare_tiles,),
        in_specs=in_specs,
        out_specs=[tile(D_MODEL),
                   pl.BlockSpec((None,) + hs_shape[1:], lambda i: (first_block + i, 0, 0)),
                   tile(LANES),
                   pl.BlockSpec((None, 8, LANES), lambda i: (real(i), 0, 0))],
        out_shape=[jax.ShapeDtypeStruct((n, D_MODEL), F32),
                   jax.ShapeDtypeStruct(hs_shape, BF16),
                   jax.ShapeDtypeStruct((n, LANES), F32),
                   jax.ShapeDtypeStruct((n_real, 8, LANES), F32)],
        scratch_shapes=[pltpu.VMEM((1, LANES), F32)],
        input_output_aliases=aliases,
        compiler_params=pltpu.CompilerParams(
            dimension_semantics=("arbitrary",), vmem_limit_bytes=VMEM_LIMIT),
        name="merge",
    )(*args)


MOE_ROWS = 512
SEG_ALIGN = 16


def _segment_copies(fn, tables, tile, local_ref, buffer, flat_ref, sem, to_flat):
    pieces_ref, gdst_ref = tables
    stride = gdst_ref.shape[0] // pieces_ref.shape[0]

    def body(p, carry):
        local = local_ref.at[buffer, pl.ds(pl.multiple_of(p * SEG_ALIGN, SEG_ALIGN), SEG_ALIGN), :]
        flat = flat_ref.at[pl.ds(pl.multiple_of(gdst_ref[tile * stride + p], SEG_ALIGN), SEG_ALIGN), :]
        src, dst = (local, flat) if to_flat else (flat, local)
        fn(pltpu.make_async_copy(src, dst, sem))
        return carry

    lax.fori_loop(0, pieces_ref[tile], body, 0)


def _gmm_kernel(te_ref, nv_ref, row_ref, hs_ref, wg_ref, wu_ref, wd_ref, o_ref, x_buf, sem, wgu_scr, wd_scr):
    j = pl.program_id(0)
    n_valid = nv_ref[0]
    used = j < n_valid
    pieces_per_tile = MOE_ROWS // SEG_ALIGN

    def piece_copy(row, slot, k):
        src = hs_ref.at[pl.ds(pl.multiple_of(row, SEG_ALIGN), SEG_ALIGN), :]
        return pltpu.make_async_copy(src, x_buf.at[slot, pl.ds(k * SEG_ALIGN, SEG_ALIGN), :], sem.at[slot])

    def fetch(tile):
        for k in range(pieces_per_tile):
            piece_copy(row_ref[tile * pieces_per_tile + k], tile % 2, k).start()

    def wait_tile(tile):
        for k in range(pieces_per_tile):
            piece_copy(0, tile % 2, k).wait()

    @pl.when(j == 0)
    def _():
        fetch(j)

    @pl.when(j + 1 < n_valid)
    def _():
        fetch(j + 1)

    @pl.when(used & ((j == 0) | (te_ref[j] != te_ref[jnp.maximum(j - 1, 0)])))
    def _():
        wgu_scr[:, 0:D_EXPERT] = wg_ref[...].astype(BF16)
        wgu_scr[:, D_EXPERT:] = wu_ref[...].astype(BF16)
        wd_scr[...] = wd_ref[...].astype(BF16)

    @pl.when(used)
    def _():
        wait_tile(j)
        au = _dot(x_buf[j % 2], wgu_scr[...])
        a = au[:, 0:D_EXPERT]
        hh = (a * _sigmoid(a)) * au[:, D_EXPERT:]
        o_ref[...] = _dot(hh.astype(BF16), wd_scr[...]).astype(o_ref.dtype)

    @pl.when(jnp.logical_not(used))
    def _():
        o_ref[...] = jnp.zeros_like(o_ref)


def _gmm_call(tile_expert, n_valid, source_rows, hs_rows, wg, wu, wd):
    n_tiles = tile_expert.shape[0]
    wsel = lambda j, te, *_: (te[j], 0, 0)
    return pl.pallas_call(
        _gmm_kernel,
        grid_spec=pltpu.PrefetchScalarGridSpec(
            num_scalar_prefetch=3,
            grid=(n_tiles,),
            in_specs=[pl.BlockSpec(memory_space=pl.ANY),
                      pl.BlockSpec((None, D_MODEL, D_EXPERT), wsel),
                      pl.BlockSpec((None, D_MODEL, D_EXPERT), wsel),
                      pl.BlockSpec((None, D_EXPERT, D_MODEL), wsel)],
            out_specs=pl.BlockSpec((MOE_ROWS, D_MODEL), lambda j, *_: (j, 0)),
            scratch_shapes=[pltpu.VMEM((2, MOE_ROWS, D_MODEL), BF16),
                            pltpu.SemaphoreType.DMA((2,)),
                            pltpu.VMEM((D_MODEL, 2 * D_EXPERT), BF16),
                            pltpu.VMEM((D_EXPERT, D_MODEL), BF16)],
        ),
        out_shape=jax.ShapeDtypeStruct((n_tiles * MOE_ROWS, D_MODEL), BF16),
        compiler_params=pltpu.CompilerParams(
            dimension_semantics=("arbitrary",), vmem_limit_bytes=VMEM_LIMIT),
        name="moe_grouped",
    )(tile_expert, n_valid, source_rows, hs_rows, wg, wu, wd)


def _combine_kernel(pieces_ref, gdst_ref, x1_ref, meta_ref, gf_ref, os_ref, y_ref, buf_ref, sem, *, tile0):
    step = pl.program_id(0)
    n_steps = pl.num_programs(0)
    tm = x1_ref.shape[0]
    tables = (pieces_ref, gdst_ref)

    def fetch(fn, i):
        slot = i % 2
        _segment_copies(fn, tables, tile0 + i, buf_ref, slot, os_ref, sem.at[slot], False)

    @pl.when(step == 0)
    def _():
        buf_ref[...] = jnp.zeros_like(buf_ref)
        fetch(lambda c: c.start(), step)

    @pl.when(step + 1 < n_steps)
    def _():
        fetch(lambda c: c.start(), step + 1)

    fetch(lambda c: c.wait(), step)
    rows = buf_ref[step % 2]
    pos = lax.broadcasted_iota(jnp.int32, (tm, rows.shape[0]), 1)
    lp1 = meta_ref[:, 0:1].astype(jnp.int32)
    lp2 = meta_ref[:, 1:2].astype(jnp.int32)
    q = jnp.where(pos == lp1, meta_ref[:, 2:3], jnp.where(pos == lp2, meta_ref[:, 3:4], 0.0))
    y = x1_ref[...] + _dot(q.astype(BF16), rows)
    y_ref[...] = _rms(y, gf_ref[...])


def _combine_call(tables, x1, meta, gf, out_sorted, tile0, local_rows, tm):
    n = x1.shape[0]
    tile = lambda w: pl.BlockSpec((tm, w), lambda i, *_: (i, 0))
    return pl.pallas_call(
        functools.partial(_combine_kernel, tile0=tile0),
        grid_spec=pltpu.PrefetchScalarGridSpec(
            num_scalar_prefetch=2,
            grid=(n // tm,),
            in_specs=[tile(D_MODEL), tile(LANES), pl.BlockSpec(gf.shape, lambda i, *_: (0, 0)),
                      pl.BlockSpec(memory_space=pl.ANY)],
            out_specs=tile(D_MODEL),
            scratch_shapes=[pltpu.VMEM((2, local_rows, D_MODEL), out_sorted.dtype),
                            pltpu.SemaphoreType.DMA((2,))],
        ),
        out_shape=jax.ShapeDtypeStruct((n, D_MODEL), F32),
        compiler_params=pltpu.CompilerParams(
            dimension_semantics=("arbitrary",), vmem_limit_bytes=VMEM_LIMIT),
        name="moe_combine",
    )(*tables, x1, meta, gf, out_sorted)


def _sparse_moe(parts, hs_all, p):
    token_tiles = [part[2].shape[0] for part in parts]
    n_token_tiles, local_rows = hs_all.shape[0], hs_all.shape[1]
    assert n_token_tiles == sum(token_tiles)
    n_tiles = -(-(n_token_tiles * local_rows) // MOE_ROWS) + N_EXPERTS
    tab = jnp.concatenate([part[2] for part in parts], axis=0)
    tab = tab[:, :, :N_EXPERTS].astype(jnp.int32)
    cnt, before = tab[:, 0], tab[:, 2]
    total = before[-1] + cnt[-1]
    tiles = (total + (MOE_ROWS - 1)) // MOE_ROWS
    tile_end = jnp.cumsum(tiles)
    tile_start = tile_end - tiles
    n_valid = tile_end[-1:]
    gpos = tile_start[None, :] * MOE_ROWS + before
    j = jnp.minimum(jnp.arange(n_tiles, dtype=jnp.int32), n_valid - 1)
    tile_expert = jnp.sum((j[:, None] >= tile_end[None, :]).astype(jnp.int32), axis=1)
    max_pieces = local_rows // SEG_ALIGN
    pieces = cnt // SEG_ALIGN
    piece_end = jnp.cumsum(pieces, axis=1)
    piece_start = piece_end - pieces
    piece = jnp.arange(max_pieces, dtype=jnp.int32)
    expert_of = jnp.sum((piece[None, :, None] >= piece_end[:, None, :]).astype(jnp.int32), axis=2)
    onehot = expert_of[:, :, None] == jnp.arange(N_EXPERTS, dtype=jnp.int32)
    first_row = gpos - SEG_ALIGN * piece_start
    gdst = jnp.sum(jnp.where(onehot, first_row[:, None, :], 0), axis=2) + SEG_ALIGN * piece[None, :]
    tables = (piece_end[:, -1], gdst.reshape(-1))
    over_tiles = jnp.cumsum(pieces, axis=0)
    e_j = tile_expert
    q0 = (jnp.arange(n_tiles, dtype=jnp.int32) - jnp.take(tile_start, e_j)) * (MOE_ROWS // SEG_ALIGN)
    q = q0[:, None] + jnp.arange(MOE_ROWS // SEG_ALIGN, dtype=jnp.int32)[None, :]
    valid = (jnp.arange(n_tiles)[:, None] < n_valid) & (q < jnp.take(over_tiles[-1], e_j)[:, None])
    per_tile = lambda table: jnp.take(table.T, e_j, axis=0)[:, None, :]
    src_tile = jnp.sum((q[:, :, None] >= per_tile(over_tiles)).astype(jnp.int32), axis=2)
    src_tile = jnp.minimum(src_tile, n_token_tiles - 1)
    at_src = lambda table: jnp.sum(jnp.where(
        src_tile[:, :, None] == jnp.arange(n_token_tiles, dtype=jnp.int32), per_tile(table), 0), axis=2)
    src_row = (at_src(piece_start) + q - at_src(over_tiles - pieces)) * SEG_ALIGN
    flat_row = src_tile * local_rows + src_row
    zero_row = local_rows - SEG_ALIGN
    assert zero_row >= 2 * max(part[0].shape[0] // part[2].shape[0] for part in parts) + (SEG_ALIGN - 1) * N_EXPERTS
    out_sorted = _gmm_call(tile_expert, n_valid.astype(jnp.int32), jnp.where(valid, flat_row, zero_row).reshape(-1),
                           hs_all.reshape(-1, D_MODEL), p["wg"], p["wu"], p["wd"])
    part_start = np.cumsum([0] + token_tiles)
    return [_combine_call(tables, part[0], part[1], p["gf"], out_sorted, int(tile0),
                          local_rows, part[0].shape[0] // part[2].shape[0])
            for part, tile0 in zip(parts, part_start[:-1])]


def _pad_cols(w, width):
    return jnp.pad(w, ((0, 0), (0, width - w.shape[1])))


def _prep_weights(norm1_g, w_in, gla_w_gate2, gla_b_gate, gla_norm_g, w_up_gla,
                  ml_conv_w, ml_conv_b, ml_b_i, ml_b_f, w_up_ml, w_out,
                  norm2_g, router_g_w, router_g_b, router_e_w, router_e_b,
                  moe_w_gate, moe_w_up, moe_w_down, final_g):
    wr =_pad_cols(jnp.concatenate([router_e_w, router_g_w], axis=1), LANES)
    wr_hi = wr.astype(BF16)
    wr_lo = (wr - wr_hi.astype(F32)).astype(BF16)
    br = _pad_cols(jnp.concatenate([router_e_b, router_g_b])[None, :], LANES)
    wg2 = jnp.pad(gla_w_gate2, ((GATE_LANE0, LANES - GATE_LANE0 - GLA_GATE_RANK), (0, 0)))
    wg2_hi = wg2.astype(BF16)
    return dict(
        g1=norm1_g[None, :], w_in=w_in.T,
        wg2_p=jnp.stack([wg2_hi, (wg2 - wg2_hi.astype(F32)).astype(BF16)]),
        bg=gla_b_gate[None, :], gn=gla_norm_g[None, :],
        wug=w_up_gla.astype(BF16),
        cw=ml_conv_w, cb=ml_conv_b[None, :],
        bif=_pad_cols(jnp.concatenate([ml_b_i, ml_b_f])[None, :], LANES),
        wum=w_up_ml.astype(BF16), wo=w_out.astype(BF16),
        g2=norm2_g[None, :], wr_hi=wr_hi, wr_lo=wr_lo, br=br,
        wg=moe_w_gate.reshape(N_EXPERTS, D_MODEL, D_EXPERT),
        wu=moe_w_up.reshape(N_EXPERTS, D_MODEL, D_EXPERT),
        wd=moe_w_down.reshape(N_EXPERTS, D_EXPERT, D_MODEL),
        gf=final_g[None, :],
    )


def _mixers(x, gla_s0, ml_c0, ml_n0, ml_m0, conv0, p, placed, *,
            gla_chunk, ml_chunk, seq_tile, row_tile, merge_tile, spare_tiles=0, hs_all=None):
    b, l, _ = x.shape
    n = b * l
    x2 = x.reshape(n, D_MODEL)
    qk, vr, plr, mqk, mvo, pmg = _proj_call(x2, p["g1"], p["w_in"], row_tile)
    pif = plr
    r3 = lambda a: a.reshape(b, l, a.shape[-1])
    s0 = None if gla_s0 is None else gla_s0.reshape(b, GLA_QK, GLA_DV)
    ga, gla_s = _gla_call(r3(qk), r3(vr), r3(plr), s0, p["wg2_p"], p["bg"], p["gn"], gla_chunk, seq_tile)
    gla_s = gla_s.reshape(b, GLA_HEADS, GLA_DK, GLA_DV)
    if ml_c0 is None:
        n0 = m0 = None
    else:
        n0 = ml_n0[:, :, None, :]
        m0 = jnp.broadcast_to(_pad_cols(ml_m0, LANES)[:, None, :], (b, 8, LANES))
    hb, ml_c, ml_n, m_b, new_conv = _mlstm_call(r3(mqk), r3(mvo), r3(pif), ml_c0, n0, m0, conv0,
                                                p["cw"], p["cb"], p["bif"], ml_chunk, seq_tile)
    part = _merge_call(x2, ga.reshape(n, GLA_V), hb.reshape(n, ML_W), pmg,
                       p["wug"], p["wum"], p["wo"], p["g2"],
                       p["wr_hi"], p["wr_lo"], p["br"], placed, merge_tile, spare_tiles, hs_all)
    states = (gla_s[None], ml_c[None], ml_n[:, :, 0, :][None], m_b[:, 0, 0:ML_HEADS][None], new_conv[None])
    return part, states


def kernel(x_prompt, x_sample, state_gla_S, state_mlstm_C, state_mlstm_n, state_mlstm_m, state_mlstm_conv, norm1_g, w_in, gla_w_gate2, gla_b_gate, gla_norm_g, w_up_gla, ml_conv_w, ml_conv_b, ml_b_i, ml_b_f, w_up_ml, w_out, norm2_g, router_g_w, router_g_b, router_e_w, router_e_b, moe_w_gate, moe_w_up, moe_w_down, final_g):
    assert norm1_g.shape[0] == 1, "single-layer trunk"
    p = _prep_weights(norm1_g[0], w_in[0], gla_w_gate2[0], gla_b_gate[0], gla_norm_g[0], w_up_gla[0],
                      ml_conv_w[0], ml_conv_b[0], ml_b_i[0], ml_b_f[0], w_up_ml[0], w_out[0],
                      norm2_g[0], router_g_w[0], router_g_b[0], router_e_w[0], router_e_b[0],
                      moe_w_gate[0], moe_w_up[0], moe_w_down[0], final_g)
    dec_seq = x_sample.shape[1]
    n_sample = x_sample.shape[0] * dec_seq
    (x1_p, hs_all, meta_p, tab_p), sp = _mixers(
        x_prompt, None, None, None, None, None, p, jnp.zeros((1, LANES), F32),
        gla_chunk=128, ml_chunk=256, seq_tile=1024, row_tile=256, merge_tile=512, spare_tiles=1)
    placed = tab_p[-1, 0:1, :] + tab_p[-1, 2:3, :]
    (x1_s, hs_all, meta_s, tab_s), ss = _mixers(
        x_sample, state_gla_S[0], state_mlstm_C[0], state_mlstm_n[0], state_mlstm_m[0],
        state_mlstm_conv[0], p, placed,
        gla_chunk=dec_seq, ml_chunk=dec_seq, seq_tile=dec_seq,
        row_tile=n_sample, merge_tile=n_sample, hs_all=hs_all)
    yp, ys = _sparse_moe([(x1_p, meta_p, tab_p), (x1_s, meta_s, tab_s)], hs_all, p)
    return (yp.reshape(x_prompt.shape), ys.reshape(x_sample.shape), *sp, *ss)
```

```python
import functools
import math

import numpy as np
import jax
import jax.numpy as jnp
from jax import lax
from jax.experimental import pallas as pl
from jax.experimental.pallas import tpu as pltpu

D_MODEL = 1024
GLA_HEADS = 4
GLA_DK = 64
GLA_DV = 128
GLA_GATE_RANK = 16
GLA_TAU = 16.0
ML_HEADS = 4
ML_DH = 128
CONV_W = 4
N_GROUPS = 4
EXPERTS_PER_GROUP = 8
N_EXPERTS = N_GROUPS * EXPERTS_PER_GROUP
D_EXPERT = 256
EPS = 1e-6

GLA_QK = GLA_HEADS * GLA_DK
GLA_V = GLA_HEADS * GLA_DV
ML_W = ML_HEADS * ML_DH

LANES = 128
VMEM_LIMIT = 56 * 1024 * 1024

W_GLA = 2 * GLA_QK + 2 * GLA_V
W_ML = 2 * ML_W + ML_W + ML_W
W_MG = 2 * D_MODEL
PROJ_WIDTHS = (W_GLA, LANES, W_ML, W_MG)
GATE_LANE0 = 16
_W_IN_STARTS = np.cumsum((0, W_GLA, GLA_GATE_RANK, W_ML, 2 * ML_HEADS, W_MG))
PROJ_SOURCES = (((0, _W_IN_STARTS[0], W_GLA),),
                ((0, _W_IN_STARTS[3], 2 * ML_HEADS), (GATE_LANE0, _W_IN_STARTS[1], GLA_GATE_RANK)),
                ((0, _W_IN_STARTS[2], W_ML),),
                ((0, _W_IN_STARTS[4], W_MG),))

F32 = jnp.float32
BF16 = jnp.bfloat16

PROJ_OUTPUTS = ((2 * GLA_QK, F32), (2 * GLA_V, BF16), (LANES, F32), (2 * ML_W, F32), (2 * ML_W, BF16),
                (W_MG, BF16))


def _dot(a, b):
    return jnp.dot(a, b, preferred_element_type=F32)


def _dot_nt(a, b):
    return lax.dot_general(a, b, (((1,), (1,)), ((), ())), preferred_element_type=F32)


def _dot_tn(a, b):
    return lax.dot_general(a, b, (((0,), (0,)), ((), ())), preferred_element_type=F32)


def _split3(x):
    hi = x.astype(BF16)
    r1 = x - hi.astype(F32)
    mid = r1.astype(BF16)
    lo = (r1 - mid.astype(F32)).astype(BF16)
    return hi, mid, lo


def _dot_exact_lhs(m, x):
    hi, mid, lo = _split3(x)
    return _dot(m, hi) + _dot(m, mid) + _dot(m, lo)


def _log_sigmoid(z):
    return jnp.minimum(z, 0.0) - jnp.log(1.0 + jnp.exp(-jnp.abs(z)))


def _sigmoid(z):
    return 1.0 / (1.0 + jnp.exp(-z))


def _rms(x, g):
    return x * lax.rsqrt(jnp.mean(x * x, axis=-1, keepdims=True) + EPS) * g


def _full_spec(shape):
    nd = len(shape)
    return pl.BlockSpec(shape, lambda *_: (0,) * nd)


CONV_PAD = 8


def _causal_conv_silu(stage_ref, x, cw_ref, cb_ref):
    t = x.shape[0]
    stage_ref[CONV_PAD:CONV_PAD + t, :] = x
    acc = cb_ref[...] + stage_ref[CONV_PAD:CONV_PAD + t, :] * cw_ref[CONV_W - 1:CONV_W, :]
    for d in range(1, CONV_W):
        acc = acc + stage_ref[CONV_PAD - d:CONV_PAD - d + t, :] * cw_ref[CONV_W - 1 - d:CONV_W - d, :]
    stage_ref[0:CONV_PAD, :] = stage_ref[t:t + CONV_PAD, :]
    return acc * _sigmoid(acc)


def _proj_kernel(x_ref, g_ref, win_ref, qk_ref, vr_ref, lr_ref, mqk_ref, mvo_ref, mg_ref, w_ref):
    starts = np.cumsum((0,) + PROJ_WIDTHS)

    @pl.when(pl.program_id(0) == 0)
    def _():
        chunk = 512
        for g, sources in enumerate(PROJ_SOURCES):
            if sum(width for _, _, width in sources) < PROJ_WIDTHS[g]:
                w_ref[starts[g]:starts[g + 1], :] = jnp.zeros((PROJ_WIDTHS[g], D_MODEL), BF16)
            for col, src, width in sources:
                dst = starts[g] + col
                for r0 in range(0, width, chunk):
                    rows = min(chunk, width - r0)
                    w_ref[dst + r0:dst + r0 + rows, :] = win_ref[src + r0:src + r0 + rows, :].astype(BF16)

    h = _rms(x_ref[...], g_ref[...]).astype(BF16)

    def cols(group, lo, hi):
        return _dot_nt(h, w_ref[starts[group] + lo:starts[group] + hi, :])

    qk_ref[...] = cols(0, 0, 2 * GLA_QK)
    vr_ref[:, 0:GLA_V] = cols(0, 2 * GLA_QK, 2 * GLA_QK + GLA_V).astype(BF16)
    r = cols(0, 2 * GLA_QK + GLA_V, W_GLA)
    vr_ref[:, GLA_V:] = (r * _sigmoid(r)).astype(BF16)
    lr_ref[...] = cols(1, 0, LANES)
    mqk_ref[...] = cols(2, 0, 2 * ML_W)
    mvo_ref[:, 0:ML_W] = cols(2, 2 * ML_W, 3 * ML_W).astype(BF16)
    mvo_ref[:, ML_W:] = _sigmoid(cols(2, 3 * ML_W, W_ML)).astype(BF16)
    mg_ref[...] = _sigmoid(cols(3, 0, W_MG)).astype(BF16)


def _proj_call(x2, g, w_in, tm):
    n = x2.shape[0]
    assert w_in.shape == (_W_IN_STARTS[-1], D_MODEL)
    return pl.pallas_call(
        _proj_kernel,
        grid=(n // tm,),
        in_specs=[pl.BlockSpec((tm, D_MODEL), lambda i: (i, 0)),
                  _full_spec(g.shape),
                  pl.BlockSpec(w_in.shape, lambda i: (0, 0), pipeline_mode=pl.Buffered(1))],
        out_specs=[pl.BlockSpec((tm, w), lambda i: (i, 0)) for w, _ in PROJ_OUTPUTS],
        out_shape=[jax.ShapeDtypeStruct((n, w), dt) for w, dt in PROJ_OUTPUTS],
        scratch_shapes=[pltpu.VMEM((sum(PROJ_WIDTHS), D_MODEL), BF16)],
        compiler_params=pltpu.CompilerParams(
            dimension_semantics=("arbitrary",), vmem_limit_bytes=VMEM_LIMIT),
        name="in_proj",
    )(x2, g, w_in)


def _gla_consts(c):
    nlev = int(math.log2(c))
    assert 1 << nlev == c
    t = np.arange(c)[:, None]
    j = np.arange(c)[None, :]
    lv = np.full((c, c), -1, np.int32)
    for l in range(nlev):
        h = c >> (l + 1)
        upper = (t % (2 * h)) >= h
        same = (j // (2 * h)) == (t // (2 * h))
        s_lower = (j % (2 * h)) < h
        lv[np.broadcast_to(upper, (c, c)) & same & s_lower] = l
    lv[np.eye(c, dtype=bool)] = nlev
    tri = (j <= t).astype(np.float32)
    return jnp.asarray(tri, BF16), jnp.asarray(np.concatenate([lv, lv], axis=1))


def _gla_kernel(*refs, c, t, has_state, streams):
    if has_state:
        (qk_ref, vr_ref, plr_ref, s0_ref, wg2_ref, bg_ref, gn_ref, tri_ref, lv_ref,
         o_ref, sout_ref, s_scr) = refs
    else:
        (qk_ref, vr_ref, plr_ref, wg2_ref, bg_ref, gn_ref, tri_ref, lv_ref,
         o_ref, sout_ref, s_scr) = refs
    nlev = int(math.log2(c))
    step = pl.program_id(1)

    if not streams:
        @pl.when(step == 0)
        def _():
            if has_state:
                s_scr[...] = s0_ref[...]
            else:
                s_scr[...] = jnp.zeros_like(s_scr)

    n_chunks = t // c
    lane_k = lax.broadcasted_iota(jnp.int32, (t, GLA_QK), 1)
    first_of_pair = (lane_k % (2 * GLA_DK)) < GLA_DK
    row_k = lax.broadcasted_iota(jnp.int32, (GLA_QK, GLA_DV), 0)
    row_t = lax.broadcasted_iota(jnp.int32, (t, GLA_QK), 0)

    def block_ref(b, blk, idx):
        if blk >= 8:
            b3 = b.reshape(t // blk, blk, GLA_QK)
            return jnp.broadcast_to(b3[:, idx:idx + 1, :], b3.shape).reshape(t, GLA_QK)
        r = row_t % blk
        out = b
        for sh in range(-idx, blk - idx):
            if sh != 0:
                out = jnp.where(r - idx == sh, pltpu.roll(b, sh % t, axis=0), out)
        return out

    q = qk_ref[:, 0:GLA_QK] * (GLA_DK ** -0.5)
    k = qk_ref[:, GLA_QK:2 * GLA_QK]
    glr = plr_ref[...]
    g_hi = glr.astype(BF16)
    g_lo = (glr - g_hi.astype(F32)).astype(BF16)
    z = (_dot(g_hi, wg2_ref[0]) + _dot(g_lo, wg2_ref[0]) + _dot(g_hi, wg2_ref[1])) + bg_ref[...]
    la = _log_sigmoid(z) * (math.log2(math.e) / GLA_TAU)
    b = jnp.concatenate([_dot_exact_lhs(tri_ref[...], la[ci * c:(ci + 1) * c])
                         for ci in range(n_chunks)], axis=0)
    b_last = block_ref(b, c, c - 1)
    qe = (q * jnp.exp2(b)).astype(BF16)
    kl = (k * jnp.exp2(b_last - b)).astype(BF16)

    k_a = jnp.where(first_of_pair, k, 0.0)
    k_b = k - k_a
    lv2 = lv_ref[...]
    factors = []
    for l in range(nlev + 1):
        if l < nlev:
            half = c >> (l + 1)
            d = b - block_ref(b, 2 * half, half - 1)
            e = jnp.exp2(jnp.minimum(d, -d))
            qt, kta, ktb = q * e, k_a * e, k_b * e
        else:
            qt, kta, ktb = q, k_a, k_b
        factors.append((qt.astype(BF16), kta.astype(BF16), ktb.astype(BF16)))
    a = [[None] * (GLA_HEADS // 2) for _ in range(n_chunks)]
    for ci in range(n_chunks):
        rows = slice(ci * c, (ci + 1) * c)
        for pr in range(GLA_HEADS // 2):
            ls = slice(pr * 2 * GLA_DK, (pr + 1) * 2 * GLA_DK)
            acc = jnp.zeros((c, 2 * c), F32)
            for l, (qt, kta, ktb) in enumerate(factors):
                rhs = jnp.concatenate([kta[rows, ls], ktb[rows, ls]], axis=0)
                acc = jnp.where(lv2 == l, _dot_nt(qt[rows, ls], rhs), acc)
            a[ci][pr] = acc

    for ci in range(n_chunks):
        rows = slice(ci * c, (ci + 1) * c)
        v = vr_ref[rows, 0:GLA_V]
        s_in, s_out = (s0_ref.at[ci], sout_ref.at[ci]) if streams else (s_scr, s_scr)
        s_all = s_in[...]
        s_bd = jnp.concatenate(
            [jnp.where((row_k // GLA_DK) == h, s_all, 0.0).astype(BF16) for h in range(GLA_HEADS)], axis=1)
        o_inter = _dot(qe[rows], s_bd)
        u_all = _dot_tn(kl[rows], v)
        dcol = jnp.exp2(jnp.broadcast_to(b[ci * c + c - 1:(ci + 1) * c, :], (LANES, GLA_QK)).T)
        for h in range(GLA_HEADS):
            vs = slice(h * GLA_DV, (h + 1) * GLA_DV)
            ks = slice(h * GLA_DK, (h + 1) * GLA_DK)
            a_h = a[ci][h // 2][:, (h % 2) * c:(h % 2 + 1) * c]
            o = _dot(a_h.astype(BF16), v[:, vs]) + o_inter[:, vs]
            on = _rms(o, gn_ref[:, vs])
            gate = vr_ref[rows, GLA_V + h * GLA_DV:GLA_V + (h + 1) * GLA_DV]
            o_ref[rows, vs] = (on * gate.astype(F32)).astype(o_ref.dtype)
            s_out[ks, :] = dcol[ks, :] * s_all[ks, :] + u_all[ks, vs]

    if not streams:
        @pl.when(step == pl.num_programs(1) - 1)
        def _():
            sout_ref[...] = s_scr[...]


def _gla_call(qk, vr, plr, s0, wg2_p, bg, gn, c, t):
    seqs, seq_len, _ = qk.shape
    tri, lv2 = _gla_consts(c)
    has_state = s0 is not None
    streams = has_state and seq_len == c
    if streams:
        qk, vr, plr = (a.reshape(1, seqs * seq_len, a.shape[-1]) for a in (qk, vr, plr))
        t = seqs * seq_len
    b, l, _ = qk.shape
    tile = lambda w: pl.BlockSpec((None, t, w), lambda bi, i: (bi, i, 0))
    if streams:
        state_spec = pl.BlockSpec((seqs, GLA_QK, GLA_DV), lambda bi, i: (0, 0, 0))
    else:
        state_spec = pl.BlockSpec((None, GLA_QK, GLA_DV), lambda bi, i: (bi, 0, 0))
    in_specs = [tile(2 * GLA_QK), tile(2 * GLA_V), tile(LANES)]
    args = [qk, vr, plr]
    if has_state:
        in_specs.append(state_spec)
        args.append(s0)
    consts = [wg2_p, bg, gn, tri, lv2]
    in_specs += [_full_spec(x.shape) for x in consts]
    out, state = pl.pallas_call(
        functools.partial(_gla_kernel, c=c, t=t, has_state=has_state, streams=streams),
        grid=(b, l // t),
        in_specs=in_specs,
        out_specs=[tile(GLA_V), state_spec],
        out_shape=[jax.ShapeDtypeStruct((b, l, GLA_V), BF16),
                   jax.ShapeDtypeStruct((seqs, GLA_QK, GLA_DV), F32)],
        scratch_shapes=[pltpu.VMEM((GLA_QK, GLA_DV), F32)],
        compiler_params=pltpu.CompilerParams(
            dimension_semantics=("arbitrary", "arbitrary"), vmem_limit_bytes=VMEM_LIMIT),
        name="gla",
    )(*args, *consts)
    return out.reshape(seqs, seq_len, GLA_V), state


def _mlstm_kernel(*refs, c, t, has_state, streams):
    if has_state:
        (mqk_ref, mvo_ref, pif_ref, c0_ref, n0_ref, m0_ref, cv0_ref, cw_ref, cb_ref, bif_ref, tri_ref, sel_ref,
         o_ref, cout_ref, nout_ref, mout_ref, cvout_ref, c_scr, m_scr, cv_scr, qk_scr) = refs
    else:
        (mqk_ref, mvo_ref, pif_ref, cw_ref, cb_ref, bif_ref, tri_ref, sel_ref,
         o_ref, cout_ref, nout_ref, mout_ref, cvout_ref, c_scr, m_scr, cv_scr, qk_scr) = refs
    step = pl.program_id(1)
    hist = CONV_W - 1

    n_chunks = t // c

    def with_n(c_mat, n_row):
        return jnp.concatenate([c_mat, jnp.broadcast_to(n_row, (ML_DH, ML_DH)).T], axis=1)

    def conv_rows(rows):
        conv = _causal_conv_silu(cv_scr, mqk_ref[rows, :], cw_ref, cb_ref)
        qk_scr[rows, 0:ML_W] = conv[:, 0:ML_W].astype(BF16)
        qk_scr[rows, ML_W:] = (conv[:, ML_W:] * (ML_DH ** -0.5)).astype(BF16)

    if streams:
        for ci in range(n_chunks):
            cv_scr[0:CONV_PAD - hist, :] = jnp.zeros((CONV_PAD - hist, 2 * ML_W), F32)
            cv_scr[CONV_PAD - hist:CONV_PAD, :] = cv0_ref[ci]
            conv_rows(slice(ci * c, (ci + 1) * c))
            cvout_ref[ci] = mqk_ref[(ci + 1) * c - hist:(ci + 1) * c, :]
    else:
        @pl.when(step == 0)
        def _():
            if has_state:
                for h in range(ML_HEADS):
                    c_scr[h] = with_n(c0_ref[h], n0_ref[h])
                m_scr[...] = m0_ref[...] * math.log2(math.e)
                cv_scr[0:CONV_PAD - hist, :] = jnp.zeros((CONV_PAD - hist, 2 * ML_W), F32)
                cv_scr[CONV_PAD - hist:CONV_PAD, :] = cv0_ref[...]
            else:
                c_scr[...] = jnp.zeros_like(c_scr)
                m_scr[...] = jnp.zeros_like(m_scr)
                cv_scr[0:CONV_PAD, :] = jnp.zeros((CONV_PAD, 2 * ML_W), F32)

        conv_rows(slice(0, t))

    log2e = math.log2(math.e)
    pre = pif_ref[...] + bif_ref[...]
    gts = pre * log2e
    flog = pltpu.roll(_log_sigmoid(pre) * log2e, LANES - ML_HEADS, axis=1)

    lane = lax.broadcasted_iota(jnp.int32, (t, LANES), 1)
    row_c = lax.broadcasted_iota(jnp.int32, (t, LANES), 0) % c
    causal = (lax.broadcasted_iota(jnp.int32, (c, c), 1) <= lax.broadcasted_iota(jnp.int32, (c, c), 0))
    ones_v = jnp.ones((c, ML_DH), BF16)

    def slots(pieces):
        out = jnp.zeros((t, LANES), F32)
        for j, piece in enumerate(pieces):
            if not isinstance(piece, float) and j > 0:
                piece = pltpu.roll(piece, ML_HEADS * j, axis=1)
            out = jnp.where((lane >= ML_HEADS * j) & (lane < ML_HEADS * (j + 1)), piece, out)
        return out

    def split3f(x):
        return [p.astype(F32) for p in _split3(x)]

    def per_chunk(rows_of):
        return jnp.concatenate([jnp.broadcast_to(rows_of(ci), (c, LANES)) for ci in range(n_chunks)], axis=0)

    bc = jnp.concatenate([_dot_exact_lhs(tri_ref[...], flog[ci * c:(ci + 1) * c])
                          for ci in range(n_chunks)], axis=0)
    w = gts - bc
    cm = w
    for j in range(int(math.log2(c))):
        sh = 1 << j
        cm = jnp.where(row_c >= sh, jnp.maximum(cm, pltpu.roll(cm, sh, axis=0)), cm)
    m_in, m_out = [], []
    for ci in range(n_chunks):
        last = slice(ci * c + c - 1, (ci + 1) * c)
        if streams:
            m_in.append(m0_ref[ci][0:1, :] * log2e)
        else:
            m_in.append(m_scr[0:1, :] if ci == 0 else m_out[ci - 1])
        m_out.append(bc[last, :] + jnp.maximum(m_in[ci], cm[last, :]))
        if streams:
            mout_ref[ci] = jnp.broadcast_to(m_out[ci] * (1.0 / log2e), mout_ref.shape[1:])
    if not streams:
        m_scr[...] = jnp.broadcast_to(m_out[-1], m_scr.shape)
    mprev = per_chunk(lambda ci: m_in[ci])
    g = jnp.maximum(mprev, cm)
    g_last = per_chunk(lambda ci: g[ci * c + c - 1:(ci + 1) * c, :])
    lhs_all = slots([1.0, 1.0, 1.0] + split3f(g))
    rhs_all = slots(split3f(w) + [-1.0, -1.0, -1.0]).astype(BF16)
    y_all = slots(split3f(mprev - g) + split3f(-(bc + g)) + split3f(w - g_last)).astype(BF16)
    lhs_heads = [jnp.where(lane % ML_HEADS == h, lhs_all, 0.0).astype(BF16) for h in range(ML_HEADS)]

    for ci in range(n_chunks):
        r0 = ci * c
        rows = pl.ds(r0, c)
        for h in range(ML_HEADS):
            hs = slice(h * ML_DH, (h + 1) * ML_DH)
            qh = qk_scr[rows, h * ML_DH:(h + 1) * ML_DH]
            kh = qk_scr[rows, ML_W + h * ML_DH:ML_W + (h + 1) * ML_DH]
            vaug = jnp.concatenate(
                [mvo_ref[rows, h * ML_DH:(h + 1) * ML_DH], ones_v],
                axis=1)
            p = jnp.exp2(jnp.where(causal, _dot_nt(lhs_heads[h][r0:r0 + c], rhs_all[r0:r0 + c]), -jnp.inf))
            bx = jnp.exp2(_dot(y_all[r0:r0 + c], sel_ref[h]))
            w_inter = bx[:, 0:ML_DH]
            e_mt = bx[:, ML_DH:2 * ML_DH]
            w_state = bx[:, 2 * ML_DH:3 * ML_DH]
            s = (_dot_nt(qh, kh) * p).astype(BF16)
            caug = with_n(c0_ref[ci, h], n0_ref[ci, h]) if streams else c_scr[h]
            nd = _dot(s, vaug) + jnp.concatenate([w_inter, w_inter], axis=1) * _dot(qh, caug.astype(BF16))
            hh = nd[:, 0:ML_DH] / jnp.maximum(jnp.abs(nd[:, ML_DH:]), e_mt)
            o_gate = mvo_ref[rows, ML_W + h * ML_DH:ML_W + (h + 1) * ML_DH]
            o_ref[rows, hs] = (o_gate.astype(F32) * hh).astype(o_ref.dtype)
            ks = (kh.astype(F32) * w_state).astype(BF16)
            dec = w_inter[c - 1:c, :]
            new = jnp.concatenate([dec, dec], axis=1) * caug + _dot_tn(ks, vaug)
            if streams:
                cout_ref[ci, h] = new[:, 0:ML_DH]
                nout_ref[ci, h] = new[:, ML_DH:].T[0:8, :]
            else:
                c_scr[h] = new

    if not streams:
        @pl.when(step == pl.num_programs(1) - 1)
        def _():
            for h in range(ML_HEADS):
                cout_ref[h] = c_scr[h, :, 0:ML_DH]
                nout_ref[h] = c_scr[h, :, ML_DH:].T[0:8, :]
            mout_ref[...] = m_scr[...] * (1.0 / log2e)
            cvout_ref[...] = mqk_ref[t - hist:t, :]


def _mlstm_call(mqk, mvo, pif, c0, n0, m0, cv0, cw, cb, bif, c, t):
    seqs, seq_len, _ = mqk.shape
    has_state = c0 is not None
    streams = has_state and seq_len == c
    if streams:
        mqk, mvo, pif = (a.reshape(1, seqs * seq_len, a.shape[-1]) for a in (mqk, mvo, pif))
        t = seqs * seq_len
    b, l, _ = mqk.shape
    tri =jnp.asarray(np.tril(np.ones((c, c), np.float32)), BF16)
    sel = np.zeros((ML_HEADS, LANES, 3 * ML_DH), np.float32)
    for h in range(ML_HEADS):
        for slot in range(9):
            sel[h, ML_HEADS * slot + h, (slot // 3) * ML_DH:(slot // 3 + 1) * ML_DH] = 1.0
    sel = jnp.asarray(sel, BF16)
    tile = lambda w: pl.BlockSpec((None, t, w), lambda bi, i: (bi, i, 0))
    if streams:
        heads = lambda *shape: _full_spec((seqs, ML_HEADS) + shape)
        m_spec = _full_spec((seqs, 8, LANES))
        cv_spec = _full_spec((seqs, CONV_W - 1, 2 * ML_W))
    else:
        heads = lambda *shape: pl.BlockSpec((None, ML_HEADS) + shape, lambda bi, i: (bi, 0, 0, 0))
        m_spec = pl.BlockSpec((None, 8, LANES), lambda bi, i: (bi, 0, 0))
        cv_spec = pl.BlockSpec((None, CONV_W - 1, 2 * ML_W), lambda bi, i: (bi, 0, 0))
    c_spec = heads(ML_DH, ML_DH)
    in_specs = [tile(2 * ML_W), tile(2 * ML_W), tile(LANES)]
    args = [mqk, mvo, pif]
    consts = [cw, cb, bif, tri, sel]
    out_specs = [tile(ML_W), c_spec, heads(8, ML_DH), m_spec, cv_spec]
    out_shape = [jax.ShapeDtypeStruct((b, l, ML_W), BF16),
                 jax.ShapeDtypeStruct((seqs, ML_HEADS, ML_DH, ML_DH), F32),
                 jax.ShapeDtypeStruct((seqs, ML_HEADS, 8, ML_DH), F32),
                 jax.ShapeDtypeStruct((seqs, 8, LANES), F32),
                 jax.ShapeDtypeStruct((seqs, CONV_W - 1, 2 * ML_W), F32)]
    scratch = [pltpu.VMEM((ML_HEADS, ML_DH, 2 * ML_DH), F32), pltpu.VMEM((8, LANES), F32),
               pltpu.VMEM((CONV_PAD + t, 2 * ML_W), F32), pltpu.VMEM((t, 2 * ML_W), BF16)]
    if has_state:
        in_specs += [c_spec, heads(1, ML_DH), m_spec, cv_spec]
        args += [c0, n0, m0, cv0]
    in_specs += [_full_spec(x.shape) for x in consts]
    out, *states = pl.pallas_call(
        functools.partial(_mlstm_kernel, c=c, t=t, has_state=has_state, streams=streams),
        grid=(b, l // t),
        in_specs=in_specs,
        out_specs=out_specs,
        out_shape=out_shape,
        scratch_shapes=scratch,
        compiler_params=pltpu.CompilerParams(
            dimension_semantics=("arbitrary", "arbitrary"), vmem_limit_bytes=VMEM_LIMIT),
        name="mlstm",
    )(*args, *consts)
    return (out.reshape(seqs, seq_len, ML_W), *states)


N_MERGE_INPUTS = 14


def _merge_kernel(*refs, n_real, aliased_hs):
    if aliased_hs:
        refs = refs[:N_MERGE_INPUTS] + refs[N_MERGE_INPUTS + 1:]
    hs_ref = refs[N_MERGE_INPUTS + 1]
    step = pl.program_id(0)

    @pl.when(step < n_real)
    def _():
        _merge_tile(*refs)

    @pl.when(step >= n_real)
    def _():
        hs_ref[...] = jnp.zeros_like(hs_ref)


def _merge_tile(x_ref, ga_ref, hb_ref, pmg_ref, wug_ref, wum_ref, wo_ref, g2_ref,
                wr_hi_ref, wr_lo_ref, br_ref, tril_ref, triu_ref, cnt0_ref,
                x1_ref, hs_ref, meta_ref, tab_ref, cnt_scr):
    ya = _dot(ga_ref[...], wug_ref[...])
    yb = _dot(hb_ref[...], wum_ref[...])
    z = pmg_ref[:, 0:D_MODEL].astype(F32) * ya + pmg_ref[:, D_MODEL:].astype(F32) * yb
    x1 = x_ref[...] + _dot(z.astype(BF16), wo_ref[...])
    x1_ref[...] = x1
    hm = _rms(x1, g2_ref[...])
    hm_hi = hm.astype(BF16)
    hm_lo = (hm - hm_hi.astype(F32)).astype(BF16)
    hi_both = _dot(hm_hi, jnp.concatenate([wr_hi_ref[...], wr_lo_ref[...]], axis=1))
    logits = hi_both[:, 0:LANES] + hi_both[:, LANES:] + _dot(hm_lo, wr_hi_ref[...]) + br_ref[...]
    tm = logits.shape[0]
    n_out = -(-(N_EXPERTS + N_GROUPS) // 8) * 8
    lt = logits.T[0:n_out, :]
    out = lax.broadcasted_iota(jnp.int32, lt.shape, 0)
    neg = -jnp.inf
    is_g = (out >= N_EXPERTS) & (out < N_EXPERTS + N_GROUPS)
    lg = jnp.where(is_g, lt, neg)
    mg = jnp.max(lg, axis=0, keepdims=True)
    p_top = 1.0 / jnp.sum(jnp.exp(lg - mg), axis=0, keepdims=True)
    gi = jnp.min(jnp.where(lg == mg, out, 2 * LANES), axis=0, keepdims=True) - N_EXPERTS
    group_shift = int(math.log2(EXPERTS_PER_GROUP))
    sel = (out < N_EXPERTS) & (jnp.right_shift(out, group_shift) == gi)
    le = jnp.where(sel, lt, neg)
    v1 = jnp.max(le, axis=0, keepdims=True)
    i1 = jnp.min(jnp.where(le == v1, out, 2 * LANES), axis=0, keepdims=True)
    le2 = jnp.where(out == i1, neg, le)
    v2 = jnp.max(le2, axis=0, keepdims=True)
    i2 = jnp.min(jnp.where(le2 == v2, out, 2 * LANES), axis=0, keepdims=True)
    e2 = jnp.exp(v2 - v1)
    per_token = jnp.concatenate(
        [i1.astype(F32), i2.astype(F32), p_top * (1.0 / (1.0 + e2)), p_top * (e2 / (1.0 + e2)),
         jnp.zeros((LANES - 4, tm), F32)], axis=0).T
    i1 = per_token[:, 0:1].astype(jnp.int32)
    i2 = per_token[:, 1:2].astype(jnp.int32)
    w1 = per_token[:, 2:3]
    w2 = per_token[:, 3:4]
    lane = lax.broadcasted_iota(jnp.int32, logits.shape, 1)

    @pl.when(pl.program_id(0) == 0)
    def _():
        cnt_scr[...] = cnt0_ref[...]

    oh1 = lane == i1
    oh2 = lane == i2
    both = jnp.where(oh1 | oh2, 1.0, 0.0)
    cnt = jnp.sum(both, axis=0, keepdims=True)
    cnt = jnp.floor((cnt + (SEG_ALIGN - 1)) * (1.0 / SEG_ALIGN)) * SEG_ALIGN
    lower = _dot(jnp.broadcast_to(cnt, (8, LANES)).astype(BF16), triu_ref[...])[0:1, :]
    lpos = _dot(tril_ref[...], both.astype(BF16)) + lower
    lp1 = jnp.sum(jnp.where(oh1, lpos, 0.0), axis=-1, keepdims=True)
    lp2 = jnp.sum(jnp.where(oh2, lpos, 0.0), axis=-1, keepdims=True)
    pos = lax.broadcasted_iota(jnp.int32, (tm, hs_ref.shape[0]), 1)
    onehot = jnp.where((pos == lp1.astype(jnp.int32)) | (pos == lp2.astype(jnp.int32)), 1.0, 0.0)
    hs_ref[...] = _dot_tn(onehot.astype(BF16), hm_hi).astype(BF16)
    cols = (lp1, lp2, w1, w2)
    meta = jnp.zeros((tm, LANES), F32)
    for ci, col in enumerate(cols):
        meta = jnp.where(lane == ci, col, meta)
    meta_ref[...] = meta
    row8 = lax.broadcasted_iota(jnp.int32, (8, LANES), 0)
    tab_ref[...] = jnp.where(row8 == 0, cnt, jnp.where(row8 == 1, lower, jnp.where(row8 == 2, cnt_scr[...], 0.0)))
    cnt_scr[...] += cnt


def _merge_call(x2, ga, hb, pmg, wug, wum, wo, g2, wr_hi, wr_lo, br, cnt0, tm, spare_tiles=0, hs_all=None):
    n = x2.shape[0]
    n_real = n // tm
    real = lambda i: jnp.minimum(i, n_real - 1)
    tile = lambda w: pl.BlockSpec((tm, w), lambda i: (real(i), 0))
    consts = [wug, wum, wo, g2, wr_hi, wr_lo, br,
              jnp.asarray(np.tril(np.ones((tm, tm), np.float32), -1), BF16),
              jnp.asarray(np.triu(np.ones((LANES, LANES), np.float32), 1), BF16),
              cnt0]
    assert 4 + len(consts) == N_MERGE_INPUTS
    args = [x2, ga, hb, pmg, *consts]
    in_specs = [tile(D_MODEL), tile(GLA_V), tile(ML_W), tile(W_MG)] + [_full_spec(x.shape) for x in consts]
    if hs_all is None:
        local_rows = 2 * tm + SEG_ALIGN * N_EXPERTS
        hs_shape, first_block, aliases = (n_real + spare_tiles, local_rows, D_MODEL), 0, {}
    else:
        hs_shape, first_block, aliases = hs_all.shape, hs_all.shape[0] - n_real, {len(args): 1}
        assert hs_shape[1] >= 2 * tm + SEG_ALIGN * N_EXPERTS
        args.append(hs_all)
        in_specs.append(pl.BlockSpec(memory_space=pl.ANY))
    return pl.pallas_call(
        functools.partial(_merge_kernel, n_real=n_real, aliased_hs=hs_all is not None),
        grid=(n_real + spare_tiles,),
        in_specs=in_specs,
        out_specs=[tile(D_MODEL),
                   pl.BlockSpec((None,) + hs_shape[1:], lambda i: (first_block + i, 0, 0)),
                   tile(LANES),
                   pl.BlockSpec((None, 8, LANES), lambda i: (real(i), 0, 0))],
        out_shape=[jax.ShapeDtypeStruct((n, D_MODEL), F32),
                   jax.ShapeDtypeStruct(hs_shape, BF16),
                   jax.ShapeDtypeStruct((n, LANES), F32),
                   jax.ShapeDtypeStruct((n_real, 8, LANES), F32)],
        scratch_shapes=[pltpu.VMEM((1, LANES), F32)],
        input_output_aliases=aliases,
        compiler_params=pltpu.CompilerParams(
            dimension_semantics=("arbitrary",), vmem_limit_bytes=VMEM_LIMIT),
        name="merge",
    )(*args)


MOE_ROWS = 512
SEG_ALIGN = 16


def _segment_copies(fn, tables, tile, local_ref, buffer, flat_ref, sem, to_flat):
    pieces_ref, gdst_ref = tables
    stride = gdst_ref.shape[0] // pieces_ref.shape[0]

    def body(p, carry):
        local = local_ref.at[buffer, pl.ds(pl.multiple_of(p * SEG_ALIGN, SEG_ALIGN), SEG_ALIGN), :]
        flat = flat_ref.at[pl.ds(pl.multiple_of(gdst_ref[tile * stride + p], SEG_ALIGN), SEG_ALIGN), :]
        src, dst = (local, flat) if to_flat else (flat, local)
        fn(pltpu.make_async_copy(src, dst, sem))
        return carry

    lax.fori_loop(0, pieces_ref[tile], body, 0)


def _gmm_kernel(te_ref, nv_ref, row_ref, hs_ref, wg_ref, wu_ref, wd_ref, o_ref, x_buf, sem, wgu_scr, wd_scr):
    j = pl.program_id(0)
    n_valid = nv_ref[0]
    used = j < n_valid
    pieces_per_tile = MOE_ROWS // SEG_ALIGN

    def piece_copy(row, slot, k):
        src = hs_ref.at[pl.ds(pl.multiple_of(row, SEG_ALIGN), SEG_ALIGN), :]
        return pltpu.make_async_copy(src, x_buf.at[slot, pl.ds(k * SEG_ALIGN, SEG_ALIGN), :], sem.at[slot])

    def fetch(tile):
        for k in range(pieces_per_tile):
            piece_copy(row_ref[tile * pieces_per_tile + k], tile % 2, k).start(priority=k % 2)

    def wait_tile(tile):
        for k in range(pieces_per_tile):
            piece_copy(0, tile % 2, k).wait()

    @pl.when(j == 0)
    def _():
        fetch(j)

    @pl.when(j + 1 < n_valid)
    def _():
        fetch(j + 1)

    @pl.when(used & ((j == 0) | (te_ref[j] != te_ref[jnp.maximum(j - 1, 0)])))
    def _():
        wgu_scr[:, 0:D_EXPERT] = wg_ref[...].astype(BF16)
        wgu_scr[:, D_EXPERT:] = wu_ref[...].astype(BF16)
        wd_scr[...] = wd_ref[...].astype(BF16)

    @pl.when(used)
    def _():
        wait_tile(j)
        au = _dot(x_buf[j % 2], wgu_scr[...])
        a = au[:, 0:D_EXPERT]
        hh = (a * _sigmoid(a)) * au[:, D_EXPERT:]
        o_ref[...] = _dot(hh.astype(BF16), wd_scr[...]).astype(o_ref.dtype)

    @pl.when(jnp.logical_not(used))
    def _():
        o_ref[...] = jnp.zeros_like(o_ref)


def _gmm_call(tile_expert, n_valid, source_rows, hs_rows, wg, wu, wd):
    n_tiles = tile_expert.shape[0]
    wsel = lambda j, te, *_: (te[j], 0, 0)
    return pl.pallas_call(
        _gmm_kernel,
        grid_spec=pltpu.PrefetchScalarGridSpec(
            num_scalar_prefetch=3,
            grid=(n_tiles,),
            in_specs=[pl.BlockSpec(memory_space=pl.ANY),
                      pl.BlockSpec((None, D_MODEL, D_EXPERT), wsel),
                      pl.BlockSpec((None, D_MODEL, D_EXPERT), wsel),
                      pl.BlockSpec((None, D_EXPERT, D_MODEL), wsel)],
            out_specs=pl.BlockSpec((MOE_ROWS, D_MODEL), lambda j, *_: (j, 0)),
            scratch_shapes=[pltpu.VMEM((2, MOE_ROWS, D_MODEL), BF16),
                            pltpu.SemaphoreType.DMA((2,)),
                            pltpu.VMEM((D_MODEL, 2 * D_EXPERT), BF16),
                            pltpu.VMEM((D_EXPERT, D_MODEL), BF16)],
        ),
        out_shape=jax.ShapeDtypeStruct((n_tiles * MOE_ROWS, D_MODEL), BF16),
        compiler_params=pltpu.CompilerParams(
            dimension_semantics=("arbitrary",), vmem_limit_bytes=VMEM_LIMIT),
        name="moe_grouped",
    )(tile_expert, n_valid, source_rows, hs_rows, wg, wu, wd)


def _combine_kernel(pieces_ref, gdst_ref, x1_ref, meta_ref, gf_ref, os_ref, y_ref, buf_ref, sem, *, tile0):
    step = pl.program_id(0)
    n_steps = pl.num_programs(0)
    tm = x1_ref.shape[0]
    tables = (pieces_ref, gdst_ref)

    def fetch(fn, i):
        slot = i % 2
        _segment_copies(fn, tables, tile0 + i, buf_ref, slot, os_ref, sem.at[slot], False)

    @pl.when(step == 0)
    def _():
        buf_ref[...] = jnp.zeros_like(buf_ref)
        fetch(lambda c: c.start(), step)

    @pl.when(step + 1 < n_steps)
    def _():
        fetch(lambda c: c.start(), step + 1)

    fetch(lambda c: c.wait(), step)
    rows = buf_ref[step % 2]
    pos = lax.broadcasted_iota(jnp.int32, (tm, rows.shape[0]), 1)
    lp1 = meta_ref[:, 0:1].astype(jnp.int32)
    lp2 = meta_ref[:, 1:2].astype(jnp.int32)
    q = jnp.where(pos == lp1, meta_ref[:, 2:3], jnp.where(pos == lp2, meta_ref[:, 3:4], 0.0))
    y = x1_ref[...] + _dot(q.astype(BF16), rows)
    y_ref[...] = _rms(y, gf_ref[...])


def _combine_call(tables, x1, meta, gf, out_sorted, tile0, local_rows, tm):
    n = x1.shape[0]
    tile = lambda w: pl.BlockSpec((tm, w), lambda i, *_: (i, 0))
    return pl.pallas_call(
        functools.partial(_combine_kernel, tile0=tile0),
        grid_spec=pltpu.PrefetchScalarGridSpec(
            num_scalar_prefetch=2,
            grid=(n // tm,),
            in_specs=[tile(D_MODEL), tile(LANES), pl.BlockSpec(gf.shape, lambda i, *_: (0, 0)),
                      pl.BlockSpec(memory_space=pl.ANY)],
            out_specs=tile(D_MODEL),
            scratch_shapes=[pltpu.VMEM((2, local_rows, D_MODEL), out_sorted.dtype),
                            pltpu.SemaphoreType.DMA((2,))],
        ),
        out_shape=jax.ShapeDtypeStruct((n, D_MODEL), F32),
        compiler_params=pltpu.CompilerParams(
            dimension_semantics=("arbitrary",), vmem_limit_bytes=VMEM_LIMIT),
        name="moe_combine",
    )(*tables, x1, meta, gf, out_sorted)


def _sparse_moe(parts, hs_all, p):
    token_tiles = [part[2].shape[0] for part in parts]
    n_token_tiles, local_rows = hs_all.shape[0], hs_all.shape[1]
    assert n_token_tiles == sum(token_tiles)
    n_tiles = -(-(n_token_tiles * local_rows) // MOE_ROWS) + N_EXPERTS
    tab = jnp.concatenate([part[2] for part in parts], axis=0)
    tab = tab[:, :, :N_EXPERTS].astype(jnp.int32)
    cnt, before = tab[:, 0], tab[:, 2]
    total = before[-1] + cnt[-1]
    tiles = (total + (MOE_ROWS - 1)) // MOE_ROWS
    tile_end = jnp.cumsum(tiles)
    tile_start = tile_end - tiles
    n_valid = tile_end[-1:]
    gpos = tile_start[None, :] * MOE_ROWS + before
    j = jnp.minimum(jnp.arange(n_tiles, dtype=jnp.int32), n_valid - 1)
    tile_expert = jnp.sum((j[:, None] >= tile_end[None, :]).astype(jnp.int32), axis=1)
    max_pieces = local_rows // SEG_ALIGN
    pieces = cnt // SEG_ALIGN
    piece_end = jnp.cumsum(pieces, axis=1)
    piece_start = piece_end - pieces
    piece = jnp.arange(max_pieces, dtype=jnp.int32)
    expert_of = jnp.sum((piece[None, :, None] >= piece_end[:, None, :]).astype(jnp.int32), axis=2)
    onehot = expert_of[:, :, None] == jnp.arange(N_EXPERTS, dtype=jnp.int32)
    first_row = gpos - SEG_ALIGN * piece_start
    gdst = jnp.sum(jnp.where(onehot, first_row[:, None, :], 0), axis=2) + SEG_ALIGN * piece[None, :]
    tables = (piece_end[:, -1], gdst.reshape(-1))
    over_tiles = jnp.cumsum(pieces, axis=0)
    e_j = tile_expert
    q0 = (jnp.arange(n_tiles, dtype=jnp.int32) - jnp.take(tile_start, e_j)) * (MOE_ROWS // SEG_ALIGN)
    q = q0[:, None] + jnp.arange(MOE_ROWS // SEG_ALIGN, dtype=jnp.int32)[None, :]
    valid = (jnp.arange(n_tiles)[:, None] < n_valid) & (q < jnp.take(over_tiles[-1], e_j)[:, None])
    per_tile = lambda table: jnp.take(table.T, e_j, axis=0)[:, None, :]
    src_tile = jnp.sum((q[:, :, None] >= per_tile(over_tiles)).astype(jnp.int32), axis=2)
    src_tile = jnp.minimum(src_tile, n_token_tiles - 1)
    at_src = lambda table: jnp.sum(jnp.where(
        src_tile[:, :, None] == jnp.arange(n_token_tiles, dtype=jnp.int32), per_tile(table), 0), axis=2)
    src_row = (at_src(piece_start) + q - at_src(over_tiles - pieces)) * SEG_ALIGN
    flat_row = src_tile * local_rows + src_row
    zero_row = local_rows - SEG_ALIGN
    assert zero_row >= 2 * max(part[0].shape[0] // part[2].shape[0] for part in parts) + (SEG_ALIGN - 1) * N_EXPERTS
    out_sorted = _gmm_call(tile_expert, n_valid.astype(jnp.int32), jnp.where(valid, flat_row, zero_row).reshape(-1),
                           hs_all.reshape(-1, D_MODEL), p["wg"], p["wu"], p["wd"])
    part_start = np.cumsum([0] + token_tiles)
    return [_combine_call(tables, part[0], part[1], p["gf"], out_sorted, int(tile0),
                          local_rows, part[0].shape[0] // part[2].shape[0])
            for part, tile0 in zip(parts, part_start[:-1])]


def _pad_cols(w, width):
    return jnp.pad(w, ((0, 0), (0, width - w.shape[1])))


def _prep_weights(norm1_g, w_in, gla_w_gate2, gla_b_gate, gla_norm_g, w_up_gla,
                  ml_conv_w, ml_conv_b, ml_b_i, ml_b_f, w_up_ml, w_out,
                  norm2_g, router_g_w, router_g_b, router_e_w, router_e_b,
                  moe_w_gate, moe_w_up, moe_w_down, final_g):
    wr =_pad_cols(jnp.concatenate([router_e_w, router_g_w], axis=1), LANES)
    wr_hi = wr.astype(BF16)
    wr_lo = (wr - wr_hi.astype(F32)).astype(BF16)
    br = _pad_cols(jnp.concatenate([router_e_b, router_g_b])[None, :], LANES)
    wg2 = jnp.pad(gla_w_gate2, ((GATE_LANE0, LANES - GATE_LANE0 - GLA_GATE_RANK), (0, 0)))
    wg2_hi = wg2.astype(BF16)
    return dict(
        g1=norm1_g[None, :], w_in=w_in.T,
        wg2_p=jnp.stack([wg2_hi, (wg2 - wg2_hi.astype(F32)).astype(BF16)]),
        bg=gla_b_gate[None, :], gn=gla_norm_g[None, :],
        wug=w_up_gla.astype(BF16),
        cw=ml_conv_w, cb=ml_conv_b[None, :],
        bif=_pad_cols(jnp.concatenate([ml_b_i, ml_b_f])[None, :], LANES),
        wum=w_up_ml.astype(BF16), wo=w_out.astype(BF16),
        g2=norm2_g[None, :], wr_hi=wr_hi, wr_lo=wr_lo, br=br,
        wg=moe_w_gate.reshape(N_EXPERTS, D_MODEL, D_EXPERT),
        wu=moe_w_up.reshape(N_EXPERTS, D_MODEL, D_EXPERT),
        wd=moe_w_down.reshape(N_EXPERTS, D_EXPERT, D_MODEL),
        gf=final_g[None, :],
    )


def _mixers(x, gla_s0, ml_c0, ml_n0, ml_m0, conv0, p, placed, *,
            gla_chunk, ml_chunk, seq_tile, row_tile, merge_tile, spare_tiles=0, hs_all=None):
    b, l, _ = x.shape
    n = b * l
    x2 = x.reshape(n, D_MODEL)
    qk, vr, plr, mqk, mvo, pmg = _proj_call(x2, p["g1"], p["w_in"], row_tile)
    pif = plr
    r3 = lambda a: a.reshape(b, l, a.shape[-1])
    s0 = None if gla_s0 is None else gla_s0.reshape(b, GLA_QK, GLA_DV)
    ga, gla_s = _gla_call(r3(qk), r3(vr), r3(plr), s0, p["wg2_p"], p["bg"], p["gn"], gla_chunk, seq_tile)
    gla_s = gla_s.reshape(b, GLA_HEADS, GLA_DK, GLA_DV)
    if ml_c0 is None:
        n0 = m0 = None
    else:
        n0 = ml_n0[:, :, None, :]
        m0 = jnp.broadcast_to(_pad_cols(ml_m0, LANES)[:, None, :], (b, 8, LANES))
    hb, ml_c, ml_n, m_b, new_conv = _mlstm_call(r3(mqk), r3(mvo), r3(pif), ml_c0, n0, m0, conv0,
                                                p["cw"], p["cb"], p["bif"], ml_chunk, seq_tile)
    part = _merge_call(x2, ga.reshape(n, GLA_V), hb.reshape(n, ML_W), pmg,
                       p["wug"], p["wum"], p["wo"], p["g2"],
                       p["wr_hi"], p["wr_lo"], p["br"], placed, merge_tile, spare_tiles, hs_all)
    states = (gla_s[None], ml_c[None], ml_n[:, :, 0, :][None], m_b[:, 0, 0:ML_HEADS][None], new_conv[None])
    return part, states


def kernel(x_prompt, x_sample, state_gla_S, state_mlstm_C, state_mlstm_n, state_mlstm_m, state_mlstm_conv, norm1_g, w_in, gla_w_gate2, gla_b_gate, gla_norm_g, w_up_gla, ml_conv_w, ml_conv_b, ml_b_i, ml_b_f, w_up_ml, w_out, norm2_g, router_g_w, router_g_b, router_e_w, router_e_b, moe_w_gate, moe_w_up, moe_w_down, final_g):
    assert norm1_g.shape[0] == 1, "single-layer trunk"
    p = _prep_weights(norm1_g[0], w_in[0], gla_w_gate2[0], gla_b_gate[0], gla_norm_g[0], w_up_gla[0],
                      ml_conv_w[0], ml_conv_b[0], ml_b_i[0], ml_b_f[0], w_up_ml[0], w_out[0],
                      norm2_g[0], router_g_w[0], router_g_b[0], router_e_w[0], router_e_b[0],
                      moe_w_gate[0], moe_w_up[0], moe_w_down[0], final_g)
    dec_seq = x_sample.shape[1]
    n_sample = x_sample.shape[0] * dec_seq
    (x1_p, hs_all, meta_p, tab_p), sp = _mixers(
        x_prompt, None, None, None, None, None, p, jnp.zeros((1, LANES), F32),
        gla_chunk=128, ml_chunk=256, seq_tile=1024, row_tile=256, merge_tile=512, spare_tiles=1)
    placed = tab_p[-1, 0:1, :] + tab_p[-1, 2:3, :]
    (x1_s, hs_all, meta_s, tab_s), ss = _mixers(
        x_sample, state_gla_S[0], state_mlstm_C[0], state_mlstm_n[0], state_mlstm_m[0],
        state_mlstm_conv[0], p, placed,
        gla_chunk=dec_seq, ml_chunk=dec_seq, seq_tile=dec_seq,
        row_tile=n_sample, merge_tile=n_sample, hs_all=hs_all)
    yp, ys = _sparse_moe([(x1_p, meta_p, tab_p), (x1_s, meta_s, tab_s)], hs_all, p)
    return (yp.reshape(x_prompt.shape), ys.reshape(x_sample.shape), *sp, *ss)
```
